```python
import jax, jax.numpy as jnp
from jax import lax
import numpy as np

D_MODEL = 1024
BATCH = 8
SEQ = 8192
DEPTH = 1

CONV_WIDTH = 512
CONV_K = 3
ATTN_GROUPS = ((128, 1), (512, 4), (2048, 16))
N_GROUPS = 3
HEADS_PER_GROUP = 4
N_HEADS = N_GROUPS * HEADS_PER_GROUP
HEAD_DIM = 64
ATTN_WIDTH = N_HEADS * HEAD_DIM
ATTN_BLOCK = 128
D_FF = 2816
FFN_K = 3
EPS = 1e-6
NEG_INF = -1e30
IN_SIZES = (CONV_WIDTH, CONV_WIDTH, CONV_WIDTH, ATTN_WIDTH, ATTN_WIDTH, ATTN_WIDTH, D_MODEL, D_MODEL)
D_IN = sum(IN_SIZES)

kernel_name = "hybrid_shortconv_dilated_swa_convffn"


def rms_norm(x, g):
    xf = x.astype(jnp.float32)
    y = xf * lax.rsqrt(jnp.mean(xf * xf, axis=-1, keepdims=True) + EPS)
    return (y * g.astype(jnp.float32)).astype(x.dtype)


def causal_dwconv(x, w, b):
    K, C = w.shape
    y = lax.conv_general_dilated(
        x, w[:, None, :], window_strides=(1,), padding=[(K - 1, 0)],
        dimension_numbers=("NWC", "WIO", "NWC"), feature_group_count=C)
    return y + b


def dilated_window_attention(q, k, v, window, dilation):
    B, S, H, dh = q.shape
    Q = ATTN_BLOCK
    n_back = window // dilation
    L = S // dilation
    Lp = L + (-L) % Q
    nb = Lp // Q

    def to_streams(t):
        t = t.reshape(B, L, dilation, H, dh).transpose(0, 2, 1, 3, 4)
        return jnp.pad(t, ((0, 0), (0, 0), (0, Lp - L), (0, 0), (0, 0)))

    qs, ks, vs = to_streams(q), to_streams(k), to_streams(v)
    qb = qs.reshape(B, dilation, nb, Q, H, dh)

    def band_blocks(t):
        tp = jnp.pad(t, ((0, 0), (0, 0), (Q, 0), (0, 0), (0, 0)))
        prev = tp[:, :, :Lp].reshape(B, dilation, nb, Q, H, dh)
        cur = t.reshape(B, dilation, nb, Q, H, dh)
        return jnp.concatenate([prev, cur], axis=3)

    kb, vb = band_blocks(ks), band_blocks(vs)
    scores = jnp.einsum("brnqhe,brnkhe->brnhqk", qb, kb,
                        preferred_element_type=jnp.float32) * (dh ** -0.5)
    qi = jnp.arange(Q)[:, None]
    kj = jnp.arange(2 * Q)[None, :]
    dist = qi + Q - kj
    key_pos = jnp.arange(nb)[:, None, None] * Q + kj - Q
    valid = (dist >= 0) & (dist <= n_back) & (key_pos >= 0)
    scores = jnp.where(valid[:, None], scores, NEG_INF)
    m = jnp.max(scores, axis=-1)
    p = jnp.exp(scores - m[..., None])
    l = jnp.sum(p, axis=-1)
    o = jnp.einsum("brnhqk,brnkhe->brnqhe", p.astype(v.dtype), vb,
                   preferred_element_type=jnp.float32)
    o = o / jnp.swapaxes(l, 3, 4)[..., None]

    def from_streams(t):
        t = t.reshape((B, dilation, Lp) + t.shape[4:])[:, :, :L]
        t = jnp.moveaxis(t, 1, 2)
        return t.reshape((B, S) + t.shape[3:])

    return (from_streams(o), from_streams(jnp.swapaxes(m, 3, 4)),
            from_streams(jnp.swapaxes(l, 3, 4)))


def _fwd_setup_inputs(seed: int = 0) -> dict:
    key = jax.random.key(seed)
    ks = jax.random.split(key, 16)
    f32 = jnp.float32

    def nrm(k, shape, scale):
        return jax.random.normal(k, shape, f32) * scale

    return {
        "x": nrm(ks[0], (BATCH, SEQ, D_MODEL), 1.0),
        "norm_mix_g": 1.0 + nrm(ks[1], (DEPTH, D_MODEL), 0.02),
        "w_in": nrm(ks[2], (DEPTH, D_MODEL, D_IN), D_MODEL ** -0.5),
        "b_gate": nrm(ks[3], (DEPTH, 2, D_MODEL), 0.01),
        "conv_a_w": nrm(ks[4], (DEPTH, CONV_K, CONV_WIDTH), CONV_K ** -0.5),
        "conv_a_b": nrm(ks[5], (DEPTH, CONV_WIDTH), 0.01),
        "w_proj_a": nrm(ks[6], (DEPTH, CONV_WIDTH, D_MODEL), CONV_WIDTH ** -0.5),
        "w_proj_b": nrm(ks[7], (DEPTH, ATTN_WIDTH, D_MODEL), ATTN_WIDTH ** -0.5),
        "w_out": nrm(ks[8], (DEPTH, D_MODEL, D_MODEL), D_MODEL ** -0.5),
        "norm_ffn_g": 1.0 + nrm(ks[9], (DEPTH, D_MODEL), 0.02),
        "w_up": nrm(ks[10], (DEPTH, D_MODEL, 2 * D_FF), D_MODEL ** -0.5),
        "ffn_conv_w": nrm(ks[11], (DEPTH, FFN_K, 2 * D_FF), FFN_K ** -0.5),
        "ffn_conv_b": nrm(ks[12], (DEPTH, 2 * D_FF), 0.01),
        "w_down": nrm(ks[13], (DEPTH, D_FF, D_MODEL), D_FF ** -0.5),
        "final_norm_g": 1.0 + nrm(ks[14], (D_MODEL,), 0.02),
    }


def _fwd_reference(x, norm_mix_g, w_in, b_gate, conv_a_w, conv_a_b, w_proj_a, w_proj_b,
              w_out, norm_ffn_g, w_up, ffn_conv_w, ffn_conv_b, w_down, final_norm_g):
    B, S, _ = x.shape
    split_points = [int(s) for s in np.cumsum(IN_SIZES)[:-1]]
    for layer in range(DEPTH):
        h = rms_norm(x, norm_mix_g[layer])
        proj = h @ w_in[layer]
        a_b, a_c, a_v, q, k, v, g_a, g_b = jnp.split(proj, split_points, axis=-1)

        y_a = a_b * causal_dwconv(a_c * a_v, conv_a_w[layer], conv_a_b[layer])
        y_a = y_a @ w_proj_a[layer]

        q = q.reshape(B, S, N_GROUPS, HEADS_PER_GROUP, HEAD_DIM)
        k = k.reshape(B, S, N_GROUPS, HEADS_PER_GROUP, HEAD_DIM)
        v = v.reshape(B, S, N_GROUPS, HEADS_PER_GROUP, HEAD_DIM)
        outs, ms, ls = [], [], []
        for gi, (window, dilation) in enumerate(ATTN_GROUPS):
            o_g, m_g, l_g = dilated_window_attention(
                q[:, :, gi], k[:, :, gi], v[:, :, gi], window, dilation)
            outs.append(o_g)
            ms.append(m_g)
            ls.append(l_g)
        m_all = jnp.stack(ms, axis=2)
        l_all = jnp.stack(ls, axis=2)
        o_all = jnp.stack(outs, axis=2)
        w_den = l_all * jnp.exp(m_all - jnp.max(m_all, axis=2, keepdims=True))
        alpha = w_den / jnp.sum(w_den, axis=2, keepdims=True)
        y_b = (alpha[..., None] * o_all).reshape(B, S, ATTN_WIDTH).astype(x.dtype)
        y_b = y_b @ w_proj_b[layer]

        merged = (jax.nn.sigmoid(g_a + b_gate[layer, 0]) * y_a
                  + jax.nn.sigmoid(g_b + b_gate[layer, 1]) * y_b)
        x = x + merged @ w_out[layer]

        h = rms_norm(x, norm_ffn_g[layer])
        up = causal_dwconv(h @ w_up[layer], ffn_conv_w[layer], ffn_conv_b[layer])
        gate, val = jnp.split(up, 2, axis=-1)
        x = x + (jax.nn.silu(gate) * val) @ w_down[layer]
    return rms_norm(x, final_norm_g)


import jax as _jax
import jax.numpy as _jnp

TWIN_FORMAT = 'train_step'
FWD_PARAMS = ['x', 'norm_mix_g', 'w_in', 'b_gate', 'conv_a_w', 'conv_a_b', 'w_proj_a', 'w_proj_b', 'w_out', 'norm_ffn_g', 'w_up', 'ffn_conv_w', 'ffn_conv_b', 'w_down', 'final_norm_g']
TWIN_WEIGHTS = ['norm_mix_g', 'w_in', 'b_gate', 'conv_a_w', 'conv_a_b', 'w_proj_a', 'w_proj_b', 'w_out', 'norm_ffn_g', 'w_up', 'ffn_conv_w', 'ffn_conv_b', 'w_down', 'final_norm_g']
TWIN_DIFF_INPUT = 'x'
TWIN_INPUTS = ['x', 'norm_mix_g', 'w_in', 'b_gate', 'conv_a_w', 'conv_a_b', 'w_proj_a', 'w_proj_b', 'w_out', 'norm_ffn_g', 'w_up', 'ffn_conv_w', 'ffn_conv_b', 'w_down', 'final_norm_g', 'loss_target', 'm_norm_mix_g', 'm_w_in', 'm_b_gate', 'm_conv_a_w', 'm_conv_a_b', 'm_w_proj_a', 'm_w_proj_b', 'm_w_out', 'm_norm_ffn_g', 'm_w_up', 'm_ffn_conv_w', 'm_ffn_conv_b', 'm_w_down', 'm_final_norm_g', 'v_norm_mix_g', 'v_w_in', 'v_b_gate', 'v_conv_a_w', 'v_conv_a_b', 'v_w_proj_a', 'v_w_proj_b', 'v_w_out', 'v_norm_ffn_g', 'v_w_up', 'v_ffn_conv_w', 'v_ffn_conv_b', 'v_w_down', 'v_final_norm_g']
TWIN_OUTPUTS = ['loss', 'grad_x', 'grad_norm_mix_g', 'grad_w_in', 'grad_b_gate', 'grad_conv_a_w', 'grad_conv_a_b', 'grad_w_proj_a', 'grad_w_proj_b', 'grad_w_out', 'grad_norm_ffn_g', 'grad_w_up', 'grad_ffn_conv_w', 'grad_ffn_conv_b', 'grad_w_down', 'grad_final_norm_g', 'delta_norm_mix_g', 'delta_w_in', 'delta_b_gate', 'delta_conv_a_w', 'delta_conv_a_b', 'delta_w_proj_a', 'delta_w_proj_b', 'delta_w_out', 'delta_norm_ffn_g', 'delta_w_up', 'delta_ffn_conv_w', 'delta_ffn_conv_b', 'delta_w_down', 'delta_final_norm_g', 'new_m_norm_mix_g', 'new_m_w_in', 'new_m_b_gate', 'new_m_conv_a_w', 'new_m_conv_a_b', 'new_m_w_proj_a', 'new_m_w_proj_b', 'new_m_w_out', 'new_m_norm_ffn_g', 'new_m_w_up', 'new_m_ffn_conv_w', 'new_m_ffn_conv_b', 'new_m_w_down', 'new_m_final_norm_g', 'new_v_norm_mix_g', 'new_v_w_in', 'new_v_b_gate', 'new_v_conv_a_w', 'new_v_conv_a_b', 'new_v_w_proj_a', 'new_v_w_proj_b', 'new_v_w_out', 'new_v_norm_ffn_g', 'new_v_w_up', 'new_v_ffn_conv_w', 'new_v_ffn_conv_b', 'new_v_w_down', 'new_v_final_norm_g']
TWIN_LEAF_KINDS = {'loss': 'loss', 'grad_x': 'grad_x', 'grad_norm_mix_g': 'grad_w', 'grad_w_in': 'grad_w', 'grad_b_gate': 'grad_w', 'grad_conv_a_w': 'grad_w', 'grad_conv_a_b': 'grad_w', 'grad_w_proj_a': 'grad_w', 'grad_w_proj_b': 'grad_w', 'grad_w_out': 'grad_w', 'grad_norm_ffn_g': 'grad_w', 'grad_w_up': 'grad_w', 'grad_ffn_conv_w': 'grad_w', 'grad_ffn_conv_b': 'grad_w', 'grad_w_down': 'grad_w', 'grad_final_norm_g': 'grad_w', 'delta_norm_mix_g': 'delta_w', 'delta_w_in': 'delta_w', 'delta_b_gate': 'delta_w', 'delta_conv_a_w': 'delta_w', 'delta_conv_a_b': 'delta_w', 'delta_w_proj_a': 'delta_w', 'delta_w_proj_b': 'delta_w', 'delta_w_out': 'delta_w', 'delta_norm_ffn_g': 'delta_w', 'delta_w_up': 'delta_w', 'delta_ffn_conv_w': 'delta_w', 'delta_ffn_conv_b': 'delta_w', 'delta_w_down': 'delta_w', 'delta_final_norm_g': 'delta_w', 'new_m_norm_mix_g': 'new_m', 'new_m_w_in': 'new_m', 'new_m_b_gate': 'new_m', 'new_m_conv_a_w': 'new_m', 'new_m_conv_a_b': 'new_m', 'new_m_w_proj_a': 'new_m', 'new_m_w_proj_b': 'new_m', 'new_m_w_out': 'new_m', 'new_m_norm_ffn_g': 'new_m', 'new_m_w_up': 'new_m', 'new_m_ffn_conv_w': 'new_m', 'new_m_ffn_conv_b': 'new_m', 'new_m_w_down': 'new_m', 'new_m_final_norm_g': 'new_m', 'new_v_norm_mix_g': 'new_v', 'new_v_w_in': 'new_v', 'new_v_b_gate': 'new_v', 'new_v_conv_a_w': 'new_v', 'new_v_conv_a_b': 'new_v', 'new_v_w_proj_a': 'new_v', 'new_v_w_proj_b': 'new_v', 'new_v_w_out': 'new_v', 'new_v_norm_ffn_g': 'new_v', 'new_v_w_up': 'new_v', 'new_v_ffn_conv_w': 'new_v', 'new_v_ffn_conv_b': 'new_v', 'new_v_w_down': 'new_v', 'new_v_final_norm_g': 'new_v'}


def _forward(args):
    return _fwd_reference(*[args[k] for k in FWD_PARAMS])


def _output_shape():
    def fwd():
        inp = _fwd_setup_inputs(0)
        return _fwd_reference(*[inp[k] for k in FWD_PARAMS])
    out = _jax.eval_shape(fwd)
    return out.shape, out.dtype

N_MICROBATCH = 1
ADAM_LR = 0.001
ADAM_B1 = 0.9
ADAM_B2 = 0.999
ADAM_EPS = 1e-08
ADAM_WD = 0.01
ADAM_STEP = 10
PER_EXAMPLE_BATCH_AXIS = {'x': 0, 'loss_target': 0}
SHARED_INPUTS = []
_WEIGHT_DTYPES = {'norm_mix_g': _jnp.float32, 'w_in': _jnp.float32, 'b_gate': _jnp.float32, 'conv_a_w': _jnp.float32, 'conv_a_b': _jnp.float32, 'w_proj_a': _jnp.float32, 'w_proj_b': _jnp.float32, 'w_out': _jnp.float32, 'norm_ffn_g': _jnp.float32, 'w_up': _jnp.float32, 'ffn_conv_w': _jnp.float32, 'ffn_conv_b': _jnp.float32, 'w_down': _jnp.float32, 'final_norm_g': _jnp.float32}
MOMENT_SCALE = {'norm_mix_g': 2.326957e-01, 'w_in': 9.777598e-02, 'b_gate': 3.691148e-02, 'conv_a_w': 1.953631e-01, 'conv_a_b': 1.908543e-01, 'w_proj_a': 1.321887e-01, 'w_proj_b': 1.445652e-02, 'w_out': 1.323409e-01, 'norm_ffn_g': 1.749267e-01, 'w_up': 7.469769e-02, 'ffn_conv_w': 7.292797e-02, 'ffn_conv_b': 7.163657e-02, 'w_down': 1.219842e-01, 'final_norm_g': 6.404793e+01}


def _to_microbatches(a, axis):
    t = _jnp.moveaxis(a, axis, 0)
    t = t.reshape((N_MICROBATCH, t.shape[0] // N_MICROBATCH) + t.shape[1:])
    return _jnp.moveaxis(t, 1, axis + 1)


def setup_inputs(seed: int = 0) -> dict:
    inp = _fwd_setup_inputs(seed)
    key = _jax.random.fold_in(_jax.random.key(seed), 7919)
    shape, _ = _output_shape()
    out = dict(inp)
    out["loss_target"] = _jax.random.normal(_jax.random.fold_in(key, 0), shape, _jnp.float32)
    for i, name in enumerate(TWIN_WEIGHTS):
        w = inp[name].astype(_jnp.float32)
        if MOMENT_SCALE is None:
            s = _jnp.sqrt(_jnp.mean(_jnp.square(w)) + 1e-30)
        else:
            s = MOMENT_SCALE[name]
        km, kv = _jax.random.split(_jax.random.fold_in(key, i + 1))
        out[name] = w
        out["m_" + name] = s * _jax.random.normal(km, w.shape, _jnp.float32)
        out["v_" + name] = (s * s) * _jax.random.uniform(kv, w.shape, _jnp.float32, 0.5, 1.5)
    if N_MICROBATCH > 1:
        for name, axis in PER_EXAMPLE_BATCH_AXIS.items():
            out[name] = _to_microbatches(out[name], axis)
    return {'x': out['x'], 'norm_mix_g': out['norm_mix_g'], 'w_in': out['w_in'], 'b_gate': out['b_gate'], 'conv_a_w': out['conv_a_w'], 'conv_a_b': out['conv_a_b'], 'w_proj_a': out['w_proj_a'], 'w_proj_b': out['w_proj_b'], 'w_out': out['w_out'], 'norm_ffn_g': out['norm_ffn_g'], 'w_up': out['w_up'], 'ffn_conv_w': out['ffn_conv_w'], 'ffn_conv_b': out['ffn_conv_b'], 'w_down': out['w_down'], 'final_norm_g': out['final_norm_g'], 'loss_target': out['loss_target'], 'm_norm_mix_g': out['m_norm_mix_g'], 'm_w_in': out['m_w_in'], 'm_b_gate': out['m_b_gate'], 'm_conv_a_w': out['m_conv_a_w'], 'm_conv_a_b': out['m_conv_a_b'], 'm_w_proj_a': out['m_w_proj_a'], 'm_w_proj_b': out['m_w_proj_b'], 'm_w_out': out['m_w_out'], 'm_norm_ffn_g': out['m_norm_ffn_g'], 'm_w_up': out['m_w_up'], 'm_ffn_conv_w': out['m_ffn_conv_w'], 'm_ffn_conv_b': out['m_ffn_conv_b'], 'm_w_down': out['m_w_down'], 'm_final_norm_g': out['m_final_norm_g'], 'v_norm_mix_g': out['v_norm_mix_g'], 'v_w_in': out['v_w_in'], 'v_b_gate': out['v_b_gate'], 'v_conv_a_w': out['v_conv_a_w'], 'v_conv_a_b': out['v_conv_a_b'], 'v_w_proj_a': out['v_w_proj_a'], 'v_w_proj_b': out['v_w_proj_b'], 'v_w_out': out['v_w_out'], 'v_norm_ffn_g': out['v_norm_ffn_g'], 'v_w_up': out['v_w_up'], 'v_ffn_conv_w': out['v_ffn_conv_w'], 'v_ffn_conv_b': out['v_ffn_conv_b'], 'v_w_down': out['v_w_down'], 'v_final_norm_g': out['v_final_norm_g']}


def _loss(weights, diff, rest, loss_target):
    with _jax.named_scope("forward"):
        args = {**rest, TWIN_DIFF_INPUT: diff, **{k: w.astype(_WEIGHT_DTYPES[k]) for k, w in weights.items()}}
        y = _forward(args)
    with _jax.named_scope("loss_head"):
        err = _jnp.square(y.astype(_jnp.float32) - loss_target)
        return 0.5 * _jnp.sum(_jnp.mean(err, axis=-1)) if err.ndim else 0.5 * err


def _adamw(w, g, m, v):
    m = ADAM_B1 * m + (1.0 - ADAM_B1) * g
    v = ADAM_B2 * v + (1.0 - ADAM_B2) * _jnp.square(g)
    m_hat = m / (1.0 - ADAM_B1 ** ADAM_STEP)
    v_hat = v / (1.0 - ADAM_B2 ** ADAM_STEP)
    delta = -ADAM_LR * (m_hat / (_jnp.sqrt(v_hat) + ADAM_EPS) + ADAM_WD * w)
    return delta, m, v


def reference(x, norm_mix_g, w_in, b_gate, conv_a_w, conv_a_b, w_proj_a, w_proj_b, w_out, norm_ffn_g, w_up, ffn_conv_w, ffn_conv_b, w_down, final_norm_g, loss_target, m_norm_mix_g, m_w_in, m_b_gate, m_conv_a_w, m_conv_a_b, m_w_proj_a, m_w_proj_b, m_w_out, m_norm_ffn_g, m_w_up, m_ffn_conv_w, m_ffn_conv_b, m_w_down, m_final_norm_g, v_norm_mix_g, v_w_in, v_b_gate, v_conv_a_w, v_conv_a_b, v_w_proj_a, v_w_proj_b, v_w_out, v_norm_ffn_g, v_w_up, v_ffn_conv_w, v_ffn_conv_b, v_w_down, v_final_norm_g):
    given = dict(x=x, norm_mix_g=norm_mix_g, w_in=w_in, b_gate=b_gate, conv_a_w=conv_a_w, conv_a_b=conv_a_b, w_proj_a=w_proj_a, w_proj_b=w_proj_b, w_out=w_out, norm_ffn_g=norm_ffn_g, w_up=w_up, ffn_conv_w=ffn_conv_w, ffn_conv_b=ffn_conv_b, w_down=w_down, final_norm_g=final_norm_g, loss_target=loss_target, m_norm_mix_g=m_norm_mix_g, m_w_in=m_w_in, m_b_gate=m_b_gate, m_conv_a_w=m_conv_a_w, m_conv_a_b=m_conv_a_b, m_w_proj_a=m_w_proj_a, m_w_proj_b=m_w_proj_b, m_w_out=m_w_out, m_norm_ffn_g=m_norm_ffn_g, m_w_up=m_w_up, m_ffn_conv_w=m_ffn_conv_w, m_ffn_conv_b=m_ffn_conv_b, m_w_down=m_w_down, m_final_norm_g=m_final_norm_g, v_norm_mix_g=v_norm_mix_g, v_w_in=v_w_in, v_b_gate=v_b_gate, v_conv_a_w=v_conv_a_w, v_conv_a_b=v_conv_a_b, v_w_proj_a=v_w_proj_a, v_w_proj_b=v_w_proj_b, v_w_out=v_w_out, v_norm_ffn_g=v_norm_ffn_g, v_w_up=v_w_up, v_ffn_conv_w=v_ffn_conv_w, v_ffn_conv_b=v_ffn_conv_b, v_w_down=v_w_down, v_final_norm_g=v_final_norm_g)
    weights = {n: given[n] for n in TWIN_WEIGHTS}
    shared = {n: given[n] for n in SHARED_INPUTS}
    per_example = {n: given[n] for n in ['x']}
    grad_fn = _jax.value_and_grad(_loss, argnums=(0, 1))

    def one_microbatch(ex, loss_target):
        ex = dict(ex)
        diff = ex.pop(TWIN_DIFF_INPUT)
        return grad_fn(weights, diff, {**shared, **ex}, loss_target)

    if N_MICROBATCH == 1:
        loss, (grad_w, grad_x) = one_microbatch(per_example, given["loss_target"])
    else:
        def body(carry, xs):
            loss_sum, grad_sum = carry
            l_k, (gw_k, gx_k) = one_microbatch(xs[0], xs[1])
            with _jax.named_scope("update"):
                return (loss_sum + l_k, _jax.tree.map(_jnp.add, grad_sum, gw_k)), gx_k

        init = (_jnp.zeros((), _jnp.float32), _jax.tree.map(_jnp.zeros_like, weights))
        (loss, grad_w), grad_x = _jax.lax.scan(body, init, (per_example, given["loss_target"]))
    with _jax.named_scope("update"):
        delta_w, new_m, new_v = {}, {}, {}
        for n in TWIN_WEIGHTS:
            delta_w[n], new_m[n], new_v[n] = _adamw(weights[n], grad_w[n], given["m_" + n], given["v_" + n])
    return (loss, grad_x, *[grad_w[n] for n in TWIN_WEIGHTS], *[delta_w[n] for n in TWIN_WEIGHTS],
            *[new_m[n] for n in TWIN_WEIGHTS], *[new_v[n] for n in TWIN_WEIGHTS])
```

```python
import functools

import jax
import jax.numpy as jnp
from jax import lax
from jax.experimental import pallas as pl
from jax.experimental.pallas import tpu as pltpu

F32 = jnp.float32
BF16 = jnp.bfloat16
MESH = pl.DeviceIdType.MESH

N_DEV = 8
RMS_EPS = 1e-6
NEG_INF = -1e30
N_GROUPS = 3
HEADS_PER_GROUP = 4
HEAD_DIM = 64
GROUP_W = HEADS_PER_GROUP * HEAD_DIM
ATTN_W = N_GROUPS * GROUP_W
QBLK = 128
ATTN_SCALE = HEAD_DIM ** -0.5

ADAM_LR = 0.001
ADAM_B1 = 0.9
ADAM_B2 = 0.999
ADAM_EPS = 1e-08
ADAM_WD = 0.01
ADAM_STEP = 10

HALO = 16
LANES = 128
SUBLANES = 8
VMEM_LIMIT_BYTES = 56 * 1024 * 1024


def _params(*sem):
    return pltpu.CompilerParams(dimension_semantics=sem, vmem_limit_bytes=VMEM_LIMIT_BYTES)


def _pick_tile(n, cap):
    if n <= cap:
        return n
    best = None
    for t in range(LANES, cap + 1, LANES):
        if n % t == 0:
            best = t
    assert best is not None, (n, cap)
    return best


def _rows(tm, c, j=0):
    return pl.BlockSpec((tm, c), lambda m: (m, j))


def _prev_halo(tm, c):
    return pl.BlockSpec((HALO, c), lambda m: (jnp.maximum(m * (tm // HALO) - 1, 0), 0))


def _next_halo(tm, c, t_total):
    last = t_total // HALO - 1
    return pl.BlockSpec((HALO, c), lambda m: (jnp.minimum((m + 1) * (tm // HALO), last), 0))


def _resident(shape):
    nd = len(shape)
    return pl.BlockSpec(shape, lambda *_: (0,) * nd, pipeline_mode=pl.Buffered(1))


def _acc_spec(c):
    return pl.BlockSpec((SUBLANES, c), lambda *_: (0, 0))


def _shift_down(u, halo, k):
    ext = jnp.concatenate([halo, u], axis=0)
    return pltpu.roll(ext, k, 0)[HALO:, :]


def _shift_up(u, halo, k):
    ext = jnp.concatenate([u, halo], axis=0)
    return pltpu.roll(ext, ext.shape[0] - k, 0)[: u.shape[0], :]


def _stack_rows(rows, c):
    idx = lax.broadcasted_iota(jnp.int32, (SUBLANES, c), 0)
    out = jnp.zeros((SUBLANES, c), F32)
    for i, r in enumerate(rows):
        out = out + jnp.where(idx == i, r, 0.0)
    return out


def _colsum(v):
    return jnp.sum(v, axis=0, keepdims=True)


def _sigmoid(v):
    return 1.0 / (1.0 + jnp.exp(-v))


def _rms_fwd(xv, g):
    r = lax.rsqrt(jnp.mean(xv * xv, axis=-1, keepdims=True) + RMS_EPS)
    return xv * r * g, r


def _rms_bwd(xv, g, dy):
    r = lax.rsqrt(jnp.mean(xv * xv, axis=-1, keepdims=True) + RMS_EPS)
    xn = xv * r
    dxn = dy * g
    dx = r * (dxn - xn * jnp.mean(dxn * xn, axis=-1, keepdims=True))
    return dx, dy * xn


def _dot(a, b):
    return jnp.dot(a, b, preferred_element_type=F32)


def _dot_nt(a, b):
    return lax.dot_general(a, b, (((1,), (1,)), ((), ())), preferred_element_type=F32)


def _dot_tn(a, b):
    return lax.dot_general(a, b, (((0,), (0,)), ((), ())), preferred_element_type=F32)


def _rms_matmul(x, g, w, widths, name):
    t, d = x.shape
    n = w.shape[1]
    assert sum(widths) == n
    tm = min(256, t)
    spans, lo = [], 0
    for wd in widths:
        spans.append((lo, wd))
        lo += wd

    def body(x_ref, g_ref, w_ref, *o_refs):
        h, _ = _rms_fwd(x_ref[...], g_ref[...])
        h = h.astype(BF16)
        for o_ref, (c0, wd) in zip(o_refs, spans):
            o_ref[...] = _dot(h, w_ref[:, c0:c0 + wd]).astype(BF16)

    return pl.pallas_call(
        body, name=name, grid=(t // tm,),
        in_specs=[_rows(tm, d), _resident((1, d)), _resident((d, n))],
        out_specs=[_rows(tm, wd) for wd in widths],
        out_shape=[jax.ShapeDtypeStruct((t, wd), BF16) for wd in widths],
        compiler_params=_params("parallel"),
    )(x, g, w)


def _head_masks():
    lane = lax.broadcasted_iota(jnp.int32, (1, GROUP_W), 1)
    return lane, [(lane // HEAD_DIM) == h for h in range(HEADS_PER_GROUP)]


def _band_masks(has_other):
    row = lax.broadcasted_iota(jnp.int32, (QBLK, QBLK), 0)
    col = lax.broadcasted_iota(jnp.int32, (QBLK, QBLK), 1)
    return col <= row, (col >= row) & has_other


def _attn_fwd(qs, ks, vs):
    ng, t, _ = qs.shape
    nb = t // QBLK

    def body(q_ref, kc_ref, kp_ref, vc_ref, vp_ref, o_ref, lse_ref):
        g = pl.program_id(0)
        b = pl.program_id(1)
        per_stream = nb >> (2 * g)
        has_prev = lax.rem(b, per_stream) != 0
        mask_c, mask_p = _band_masks(has_prev)
        _, heads = _head_masks()
        q, kc, kp, vc, vp = q_ref[0], kc_ref[0], kp_ref[0], vc_ref[0], vp_ref[0]
        o_acc = jnp.zeros((QBLK, GROUP_W), F32)
        lse_acc = jnp.zeros((QBLK, GROUP_W), F32)
        for hm in heads:
            qh = jnp.where(hm, q, jnp.zeros_like(q))
            sc = jnp.where(mask_c, _dot_nt(qh, kc) * ATTN_SCALE, NEG_INF)
            sp = jnp.where(mask_p, _dot_nt(qh, kp) * ATTN_SCALE, NEG_INF)
            mx = jnp.maximum(jnp.max(sc, axis=1, keepdims=True), jnp.max(sp, axis=1, keepdims=True))
            pc = jnp.exp(sc - mx)
            pp = jnp.exp(sp - mx)
            den = jnp.sum(pc, axis=1, keepdims=True) + jnp.sum(pp, axis=1, keepdims=True)
            oh = _dot(pc.astype(BF16), vc) + _dot(pp.astype(BF16), vp)
            o_acc = jnp.where(hm, oh / den, o_acc)
            lse_acc = jnp.where(hm, mx + jnp.log(den), lse_acc)
        o_ref[0] = o_acc.astype(BF16)
        lse_ref[0] = lse_acc

    blk = (1, QBLK, GROUP_W)
    cur = pl.BlockSpec(blk, lambda g, b: (g, b, 0))
    prev = pl.BlockSpec(blk, lambda g, b: (g, jnp.maximum(b - 1, 0), 0))
    return pl.pallas_call(
        body, name="attn_fwd", grid=(ng, nb),
        in_specs=[cur, cur, prev, cur, prev],
        out_specs=[cur, cur],
        out_shape=[jax.ShapeDtypeStruct((ng, t, GROUP_W), BF16), jax.ShapeDtypeStruct((ng, t, GROUP_W), F32)],
        compiler_params=_params("parallel", "parallel"),
    )(qs, ks, ks, vs, vs)


def _group_softmax(lse):
    parts = [lse[:, i * GROUP_W:(i + 1) * GROUP_W] for i in range(N_GROUPS)]
    mx = jnp.maximum(jnp.maximum(parts[0], parts[1]), parts[2])
    es = [jnp.exp(p - mx) for p in parts]
    den = es[0] + es[1] + es[2]
    return [e / den for e in es]


def _mixer_out(x, abcv, gates, o, lse, conv_w, conv_b, b_gate, w_pa, w_pb, w_o):
    t, d = x.shape
    cw = conv_w.shape[1]
    tm = min(256, t)

    def body(x_ref, abcv_ref, halo_ref, gates_ref, o_ref, lse_ref, cw_ref, cb_ref, bg_ref, wpa_ref, wpb_ref,
             wo_ref, x1_ref, ya_ref, yb_ref, yap_ref, ybp_ref, mg_ref):
        m = pl.program_id(0)
        ab = abcv_ref[:, 0:cw].astype(F32)
        u = abcv_ref[:, cw:2 * cw].astype(F32) * abcv_ref[:, 2 * cw:3 * cw].astype(F32)
        hu = halo_ref[:, cw:2 * cw].astype(F32) * halo_ref[:, 2 * cw:3 * cw].astype(F32)
        hu = jnp.where(m > 0, hu, 0.0)
        cv = (cw_ref[0:1, :] * _shift_down(u, hu, 2) + cw_ref[1:2, :] * _shift_down(u, hu, 1)
              + cw_ref[2:3, :] * u + cb_ref[...])
        ya = (ab * cv).astype(BF16)
        ya_ref[...] = ya
        alphas = _group_softmax(lse_ref[...])
        for i in range(N_GROUPS):
            sl = slice(i * GROUP_W, (i + 1) * GROUP_W)
            yb_ref[:, sl] = (alphas[i] * o_ref[:, sl].astype(F32)).astype(BF16)
        yap = _dot(ya, wpa_ref[...])
        ybp = _dot(yb_ref[...], wpb_ref[...])
        yap_ref[...] = yap.astype(BF16)
        ybp_ref[...] = ybp.astype(BF16)
        sa = _sigmoid(gates_ref[:, 0:d].astype(F32) + bg_ref[0:1, :])
        sb = _sigmoid(gates_ref[:, d:2 * d].astype(F32) + bg_ref[1:2, :])
        merged = (sa * yap + sb * ybp).astype(BF16)
        mg_ref[...] = merged
        x1_ref[...] = x_ref[...] + _dot(merged, wo_ref[...])

    return pl.pallas_call(
        body, name="mixer_out", grid=(t // tm,),
        in_specs=[_rows(tm, d), _rows(tm, 3 * cw), _prev_halo(tm, 3 * cw), _rows(tm, 2 * d), _rows(tm, ATTN_W),
                  _rows(tm, ATTN_W), _resident((3, cw)), _resident((1, cw)), _resident((2, d)),
                  _resident((cw, d)), _resident((ATTN_W, d)), _resident((d, d))],
        out_specs=[_rows(tm, d), _rows(tm, cw), _rows(tm, ATTN_W), _rows(tm, d), _rows(tm, d), _rows(tm, d)],
        out_shape=[jax.ShapeDtypeStruct((t, d), F32), jax.ShapeDtypeStruct((t, cw), BF16),
                   jax.ShapeDtypeStruct((t, ATTN_W), BF16), jax.ShapeDtypeStruct((t, d), BF16),
                   jax.ShapeDtypeStruct((t, d), BF16), jax.ShapeDtypeStruct((t, d), BF16)],
        compiler_params=_params("parallel"),
    )(x, abcv, abcv, gates, o, lse, conv_w, conv_b, b_gate, w_pa, w_pb, w_o)


def _ffn_conv(p_ref, halo_ref, w_ref, b_ref, m, c0, wd):
    p = p_ref[:, c0:c0 + wd].astype(F32)
    hp = jnp.where(m > 0, halo_ref[:, c0:c0 + wd].astype(F32), 0.0)
    p1 = _shift_down(p, hp, 1)
    p2 = _shift_down(p, hp, 2)
    up = (w_ref[0:1, c0:c0 + wd] * p2 + w_ref[1:2, c0:c0 + wd] * p1 + w_ref[2:3, c0:c0 + wd] * p
          + b_ref[:, c0:c0 + wd])
    return up, p, p1, p2


def _ffn_loss(x1, up_pre, target, conv_w, conv_b, w_d, g_f):
    t, d = x1.shape
    dff = w_d.shape[0]
    tm = min(256, t)
    ck = _pick_tile(dff, 1408)

    def body(x1_ref, up_ref, halo_ref, tg_ref, cw_ref, cb_ref, wd_ref, gf_ref, act_ref, dx2_ref, acc_ref, loss_ref):
        m = pl.program_id(0)

        @pl.when(m == 0)
        def _():
            acc_ref[...] = jnp.zeros_like(acc_ref)
            loss_ref[...] = jnp.zeros_like(loss_ref)

        x2 = x1_ref[...]
        for c0 in range(0, dff, ck):
            gate = _ffn_conv(up_ref, halo_ref, cw_ref, cb_ref, m, c0, ck)[0]
            val = _ffn_conv(up_ref, halo_ref, cw_ref, cb_ref, m, dff + c0, ck)[0]
            act = (gate * _sigmoid(gate) * val).astype(BF16)
            act_ref[:, c0:c0 + ck] = act
            x2 = x2 + _dot(act, wd_ref[c0:c0 + ck, :])
        y, _ = _rms_fwd(x2, gf_ref[...])
        diff = y - tg_ref[...]
        loss_ref[...] += 0.5 * jnp.sum(jnp.mean(diff * diff, axis=-1, keepdims=True))
        dx2, dg = _rms_bwd(x2, gf_ref[...], diff * (1.0 / d))
        dx2_ref[...] = dx2
        acc_ref[...] += _stack_rows([_colsum(dg)], d)

    return pl.pallas_call(
        body, name="ffn_loss", grid=(t // tm,),
        in_specs=[_rows(tm, d), _rows(tm, 2 * dff), _prev_halo(tm, 2 * dff), _rows(tm, d),
                  _resident((3, 2 * dff)), _resident((1, 2 * dff)), _resident((dff, d)), _resident((1, d))],
        out_specs=[_rows(tm, dff), _rows(tm, d), _acc_spec(d), _acc_spec(LANES)],
        out_shape=[jax.ShapeDtypeStruct((t, dff), BF16), jax.ShapeDtypeStruct((t, d), F32),
                   jax.ShapeDtypeStruct((SUBLANES, d), F32), jax.ShapeDtypeStruct((SUBLANES, LANES), F32)],
        compiler_params=_params("arbitrary"),
    )(x1, up_pre, up_pre, target, conv_w, conv_b, w_d, g_f)


def _ffn_act_bwd(dx2, up_pre, conv_w, conv_b, w_dt):
    t, d = dx2.shape
    dff = w_dt.shape[1]
    tm = min(256, t)
    ck = _pick_tile(dff, 1408)

    def body(dx2_ref, up_ref, halo_ref, cw_ref, cb_ref, wdt_ref, dup_ref, acc_ref):
        m = pl.program_id(0)

        @pl.when(m == 0)
        def _():
            acc_ref[...] = jnp.zeros_like(acc_ref)

        dx2v = dx2_ref[...].astype(BF16)
        for c0 in range(0, dff, ck):
            dact = _dot(dx2v, wdt_ref[:, c0:c0 + ck])
            gate, gp, gp1, gp2 = _ffn_conv(up_ref, halo_ref, cw_ref, cb_ref, m, c0, ck)
            val, vp, vp1, vp2 = _ffn_conv(up_ref, halo_ref, cw_ref, cb_ref, m, dff + c0, ck)
            sg = _sigmoid(gate)
            dval = dact * gate * sg
            dgate = dact * val * sg * (1.0 + gate * (1.0 - sg))
            dup_ref[:, c0:c0 + ck] = dgate.astype(BF16)
            dup_ref[:, dff + c0:dff + c0 + ck] = dval.astype(BF16)
            for base, dv, p0, p1, p2 in ((c0, dgate, gp, gp1, gp2), (dff + c0, dval, vp, vp1, vp2)):
                acc_ref[:, base:base + ck] += _stack_rows(
                    [_colsum(dv * p2), _colsum(dv * p1), _colsum(dv * p0), _colsum(dv)], ck)

    return pl.pallas_call(
        body, name="ffn_act_bwd", grid=(t // tm,),
        in_specs=[_rows(tm, d), _rows(tm, 2 * dff), _prev_halo(tm, 2 * dff), _resident((3, 2 * dff)),
                  _resident((1, 2 * dff)), _resident((d, dff))],
        out_specs=[_rows(tm, 2 * dff), _acc_spec(2 * dff)],
        out_shape=[jax.ShapeDtypeStruct((t, 2 * dff), BF16), jax.ShapeDtypeStruct((SUBLANES, 2 * dff), F32)],
        compiler_params=_params("arbitrary"),
    )(dx2, up_pre, up_pre, conv_w, conv_b, w_dt)


def _ffn_up_bwd(dup, x1, dx2, conv_w, w_ut, g2):
    t, d = x1.shape
    n = dup.shape[1]
    tm = min(256, t)
    ck = _pick_tile(n, 1408)
    last = t // tm - 1

    def body(dup_ref, nxt_ref, x1_ref, dx2_ref, cw_ref, wut_ref, g2_ref, dpre_ref, dx1_ref, acc_ref):
        m = pl.program_id(0)

        @pl.when(m == 0)
        def _():
            acc_ref[...] = jnp.zeros_like(acc_ref)

        dh = jnp.zeros((tm, d), F32)
        for c0 in range(0, n, ck):
            du = dup_ref[:, c0:c0 + ck].astype(F32)
            hn = jnp.where(m < last, nxt_ref[:, c0:c0 + ck].astype(F32), 0.0)
            dpre = (cw_ref[2:3, c0:c0 + ck] * du + cw_ref[1:2, c0:c0 + ck] * _shift_up(du, hn, 1)
                    + cw_ref[0:1, c0:c0 + ck] * _shift_up(du, hn, 2)).astype(BF16)
            dpre_ref[:, c0:c0 + ck] = dpre
            dh = dh + _dot(dpre, wut_ref[c0:c0 + ck, :])
        dx, dg = _rms_bwd(x1_ref[...], g2_ref[...], dh)
        dx1_ref[...] = dx2_ref[...] + dx
        acc_ref[...] += _stack_rows([_colsum(dg)], d)

    return pl.pallas_call(
        body, name="ffn_up_bwd", grid=(t // tm,),
        in_specs=[_rows(tm, n), _next_halo(tm, n, t), _rows(tm, d), _rows(tm, d), _resident((3, n)),
                  _resident((n, d)), _resident((1, d))],
        out_specs=[_rows(tm, n), _rows(tm, d), _acc_spec(d)],
        out_shape=[jax.ShapeDtypeStruct((t, n), BF16), jax.ShapeDtypeStruct((t, d), F32),
                   jax.ShapeDtypeStruct((SUBLANES, d), F32)],
        compiler_params=_params("arbitrary"),
    )(dup, dup, x1, dx2, conv_w, w_ut, g2)


def _tn_matmul(a, b, name, gain=None):
    t, mdim = a.shape
    n = b.shape[1]
    tk = min(512, t)
    tmm = mdim if gain is not None else _pick_tile(mdim, 1408)
    tn = _pick_tile(n, 1536)

    def body(*refs):
        if gain is not None:
            a_ref, b_ref, g_ref, o_ref = refs
        else:
            a_ref, b_ref, o_ref = refs
        k = pl.program_id(2)

        @pl.when(k == 0)
        def _():
            o_ref[...] = jnp.zeros_like(o_ref)

        av = a_ref[...]
        if gain is not None:
            av = _rms_fwd(av, g_ref[...])[0]
        o_ref[...] += _dot_tn(av.astype(BF16), b_ref[...].astype(BF16))

    in_specs = [pl.BlockSpec((tk, tmm), lambda i, j, k: (k, i)), pl.BlockSpec((tk, tn), lambda i, j, k: (k, j))]
    args = [a, b]
    if gain is not None:
        in_specs.append(pl.BlockSpec((1, mdim), lambda i, j, k: (0, 0)))
        args.append(gain)
    return pl.pallas_call(
        body, name=name, grid=(mdim // tmm, n // tn, t // tk),
        in_specs=in_specs,
        out_specs=pl.BlockSpec((tmm, tn), lambda i, j, k: (i, j)),
        out_shape=jax.ShapeDtypeStruct((mdim, n), F32),
        compiler_params=_params("parallel", "parallel", "arbitrary"),
    )(*args)


def _mixer_bwd(dx1, gates, yap, ybp, o, lse, b_gate, w_ot, w_pat, w_pbt):
    t, d = dx1.shape
    cw = w_pat.shape[1]
    tm = min(256, t)

    def body(dx1_ref, gates_ref, yap_ref, ybp_ref, o_ref, lse_ref, bg_ref, wot_ref, wpat_ref, wpbt_ref,
             dgates_ref, dyap_ref, dybp_ref, dya_ref, do_ref, dl_ref, acc_ref):
        m = pl.program_id(0)

        @pl.when(m == 0)
        def _():
            acc_ref[...] = jnp.zeros_like(acc_ref)

        dmg = _dot(dx1_ref[...].astype(BF16), wot_ref[...])
        sa = _sigmoid(gates_ref[:, 0:d].astype(F32) + bg_ref[0:1, :])
        sb = _sigmoid(gates_ref[:, d:2 * d].astype(F32) + bg_ref[1:2, :])
        dyap = (dmg * sa).astype(BF16)
        dybp = (dmg * sb).astype(BF16)
        dga = dmg * yap_ref[...].astype(F32) * sa * (1.0 - sa)
        dgb = dmg * ybp_ref[...].astype(F32) * sb * (1.0 - sb)
        dyap_ref[...] = dyap
        dybp_ref[...] = dybp
        dgates_ref[:, 0:d] = dga.astype(BF16)
        dgates_ref[:, d:2 * d] = dgb.astype(BF16)
        acc_ref[...] += _stack_rows([_colsum(dga), _colsum(dgb)], d)
        dya_ref[...] = _dot(dyap, wpat_ref[...]).astype(BF16)
        dyb = _dot(dybp, wpbt_ref[...])

        ri = lax.broadcasted_iota(jnp.int32, (GROUP_W, GROUP_W), 0) // HEAD_DIM
        ci = lax.broadcasted_iota(jnp.int32, (GROUP_W, GROUP_W), 1) // HEAD_DIM
        same_head = (ri == ci).astype(BF16)
        alphas = _group_softmax(lse_ref[...])
        dtot = jnp.zeros((tm, GROUP_W), F32)
        for i in range(N_GROUPS):
            sl = slice(i * GROUP_W, (i + 1) * GROUP_W)
            dov = alphas[i] * dyb[:, sl]
            do_ref[:, sl] = dov.astype(BF16)
            prod = dov * o_ref[:, sl].astype(F32)
            hi = prod.astype(BF16)
            lo = (prod - hi.astype(F32)).astype(BF16)
            dtot = dtot + _dot(hi, same_head) + _dot(lo, same_head)
        for i in range(N_GROUPS):
            sl = slice(i * GROUP_W, (i + 1) * GROUP_W)
            dl_ref[:, sl] = alphas[i] * dtot

    return pl.pallas_call(
        body, name="mixer_bwd", grid=(t // tm,),
        in_specs=[_rows(tm, d), _rows(tm, 2 * d), _rows(tm, d), _rows(tm, d), _rows(tm, ATTN_W), _rows(tm, ATTN_W),
                  _resident((2, d)), _resident((d, d)), _resident((d, cw)), _resident((d, ATTN_W))],
        out_specs=[_rows(tm, 2 * d), _rows(tm, d), _rows(tm, d), _rows(tm, cw), _rows(tm, ATTN_W),
                   _rows(tm, ATTN_W), _acc_spec(d)],
        out_shape=[jax.ShapeDtypeStruct((t, 2 * d), BF16), jax.ShapeDtypeStruct((t, d), BF16),
                   jax.ShapeDtypeStruct((t, d), BF16), jax.ShapeDtypeStruct((t, cw), BF16),
                   jax.ShapeDtypeStruct((t, ATTN_W), BF16), jax.ShapeDtypeStruct((t, ATTN_W), F32),
                   jax.ShapeDtypeStruct((SUBLANES, d), F32)],
        compiler_params=_params("arbitrary"),
    )(dx1, gates, yap, ybp, o, lse, b_gate, w_ot, w_pat, w_pbt)


def _attn_bwd(qs, ks, vs, dos, lses, dls):
    ng, t, _ = qs.shape
    nb = t // QBLK

    def body(q_ref, qn_ref, kc_ref, kp_ref, vc_ref, vp_ref, do_ref, don_ref, lse_ref, lsen_ref, dl_ref, dln_ref,
             dq_ref, dk_ref, dv_ref):
        g = pl.program_id(0)
        b = pl.program_id(1)
        per_stream = nb >> (2 * g)
        has_prev = lax.rem(b, per_stream) != 0
        has_next = lax.rem(b + 1, per_stream) != 0
        mask_c, mask_p = _band_masks(has_prev)
        _, mask_n = _band_masks(has_next)
        lane, heads = _head_masks()
        q, qn, kc, kp, vc, vp = q_ref[0], qn_ref[0], kc_ref[0], kp_ref[0], vc_ref[0], vp_ref[0]
        do, don = do_ref[0], don_ref[0]
        lse, lsen, dl, dln = lse_ref[0], lsen_ref[0], dl_ref[0], dln_ref[0]
        dq_acc = jnp.zeros((QBLK, GROUP_W), F32)
        dk_acc = jnp.zeros((QBLK, GROUP_W), F32)
        dv_acc = jnp.zeros((QBLK, GROUP_W), F32)

        def pair(qh, doh, k, v, mask, lse_h, dl_h):
            s = jnp.where(mask, _dot_nt(qh, k) * ATTN_SCALE, NEG_INF)
            p = jnp.exp(s - lse_h)
            ds = p * (_dot_nt(doh, v) - dl_h)
            return p.astype(BF16), ds.astype(BF16)

        for h, hm in enumerate(heads):
            first = lane == h * HEAD_DIM

            def col(v):
                return jnp.sum(jnp.where(first, v, 0.0), axis=1, keepdims=True)

            zero = jnp.zeros_like(q)
            qh, qnh = jnp.where(hm, q, zero), jnp.where(hm, qn, zero)
            doh, donh = jnp.where(hm, do, zero), jnp.where(hm, don, zero)
            lse_h, dl_h, lsen_h, dln_h = col(lse), col(dl), col(lsen), col(dln)
            p_c, ds_c = pair(qh, doh, kc, vc, mask_c, lse_h, dl_h)
            _, ds_p = pair(qh, doh, kp, vp, mask_p, lse_h, dl_h)
            p_n, ds_n = pair(qnh, donh, kc, vc, mask_n, lsen_h, dln_h)
            dq_h = _dot(ds_c, kc) + _dot(ds_p, kp)
            dk_h = _dot_tn(ds_c, q) + _dot_tn(ds_n, qn)
            dv_h = _dot_tn(p_c, do) + _dot_tn(p_n, don)
            dq_acc = jnp.where(hm, dq_h, dq_acc)
            dk_acc = jnp.where(hm, dk_h, dk_acc)
            dv_acc = jnp.where(hm, dv_h, dv_acc)
        dq_ref[0] = (dq_acc * ATTN_SCALE).astype(BF16)
        dk_ref[0] = (dk_acc * ATTN_SCALE).astype(BF16)
        dv_ref[0] = dv_acc.astype(BF16)

    blk = (1, QBLK, GROUP_W)
    cur = pl.BlockSpec(blk, lambda g, b: (g, b, 0))
    prev = pl.BlockSpec(blk, lambda g, b: (g, jnp.maximum(b - 1, 0), 0))
    nxt = pl.BlockSpec(blk, lambda g, b: (g, jnp.minimum(b + 1, nb - 1), 0))
    out = jax.ShapeDtypeStruct((ng, t, GROUP_W), BF16)
    return pl.pallas_call(
        body, name="attn_bwd", grid=(ng, nb),
        in_specs=[cur, nxt, cur, prev, cur, prev, cur, nxt, cur, nxt, cur, nxt],
        out_specs=[cur, cur, cur],
        out_shape=[out, out, out],
        compiler_params=_params("parallel", "parallel"),
    )(qs, qs, ks, ks, vs, vs, dos, dos, lses, lses, dls, dls)


def _conv_mixer_bwd(abcv, dya, conv_w, conv_b):
    t = abcv.shape[0]
    cw = conv_w.shape[1]
    tm = min(256, t)
    last = t // tm - 1

    def body(a_ref, ap_ref, an_ref, dya_ref, dyan_ref, cw_ref, cb_ref, d_ref, acc_ref):
        m = pl.program_id(0)

        @pl.when(m == 0)
        def _():
            acc_ref[...] = jnp.zeros_like(acc_ref)

        ab = a_ref[:, 0:cw].astype(F32)
        ac = a_ref[:, cw:2 * cw].astype(F32)
        av = a_ref[:, 2 * cw:3 * cw].astype(F32)
        u = ac * av
        hu = ap_ref[:, cw:2 * cw].astype(F32) * ap_ref[:, 2 * cw:3 * cw].astype(F32)
        hu = jnp.where(m > 0, hu, 0.0)
        u1 = _shift_down(u, hu, 1)
        u2 = _shift_down(u, hu, 2)
        cv = cw_ref[0:1, :] * u2 + cw_ref[1:2, :] * u1 + cw_ref[2:3, :] * u + cb_ref[...]
        dya_v = dya_ref[...].astype(F32)
        dcv = dya_v * ab
        ndcv = jnp.where(m < last, dyan_ref[...].astype(F32) * an_ref[:, 0:cw].astype(F32), 0.0)
        du = (cw_ref[2:3, :] * dcv + cw_ref[1:2, :] * _shift_up(dcv, ndcv, 1)
              + cw_ref[0:1, :] * _shift_up(dcv, ndcv, 2))
        d_ref[:, 0:cw] = (dya_v * cv).astype(BF16)
        d_ref[:, cw:2 * cw] = (du * av).astype(BF16)
        d_ref[:, 2 * cw:3 * cw] = (du * ac).astype(BF16)
        acc_ref[...] += _stack_rows([_colsum(dcv * u2), _colsum(dcv * u1), _colsum(dcv * u), _colsum(dcv)], cw)

    return pl.pallas_call(
        body, name="conv_mixer_bwd", grid=(t // tm,),
        in_specs=[_rows(tm, 3 * cw), _prev_halo(tm, 3 * cw), _next_halo(tm, 3 * cw, t), _rows(tm, cw),
                  _next_halo(tm, cw, t), _resident((3, cw)), _resident((1, cw))],
        out_specs=[_rows(tm, 3 * cw), _acc_spec(cw)],
        out_shape=[jax.ShapeDtypeStruct((t, 3 * cw), BF16), jax.ShapeDtypeStruct((SUBLANES, cw), F32)],
        compiler_params=_params("arbitrary"),
    )(abcv, abcv, abcv, dya, dya, conv_w, conv_b)


def _in_proj_bwd(x, dx1, dabcv, dqkv, dgates, w_int, g1):
    t, d = x.shape
    parts = (dabcv, dqkv, dgates)
    widths = [p.shape[1] for p in parts]
    n = sum(widths)
    tm = min(256, t)

    def body(x_ref, dx1_ref, p0_ref, p1_ref, p2_ref, wt_ref, g_ref, dx_ref, acc_ref):
        m = pl.program_id(0)

        @pl.when(m == 0)
        def _():
            acc_ref[...] = jnp.zeros_like(acc_ref)

        dh = jnp.zeros((tm, d), F32)
        r0 = 0
        for p_ref, wd in zip((p0_ref, p1_ref, p2_ref), widths):
            dh = dh + _dot(p_ref[...], wt_ref[r0:r0 + wd, :])
            r0 += wd
        dx, dg = _rms_bwd(x_ref[...], g_ref[...], dh)
        dx_ref[...] = dx1_ref[...] + dx
        acc_ref[...] += _stack_rows([_colsum(dg)], d)

    return pl.pallas_call(
        body, name="in_proj_bwd", grid=(t // tm,),
        in_specs=[_rows(tm, d), _rows(tm, d)] + [_rows(tm, wd) for wd in widths] + [_resident((n, d)), _resident((1, d))],
        out_specs=[_rows(tm, d), _acc_spec(d)],
        out_shape=[jax.ShapeDtypeStruct((t, d), F32), jax.ShapeDtypeStruct((SUBLANES, d), F32)],
        compiler_params=_params("arbitrary"),
    )(x, dx1, dabcv, dqkv, dgates, w_int, g1)


def _to_streams(a):
    t = a.shape[0]
    outs = []
    for g in range(N_GROUPS):
        dil = 4 ** g
        part = a[:, g * GROUP_W:(g + 1) * GROUP_W]
        outs.append(part.reshape(t // dil, dil, GROUP_W).transpose(1, 0, 2).reshape(t, GROUP_W))
    return jnp.stack(outs)


def _from_streams(s):
    t = s.shape[1]
    outs = []
    for g in range(N_GROUPS):
        dil = 4 ** g
        outs.append(s[g].reshape(dil, t // dil, GROUP_W).transpose(1, 0, 2).reshape(t, GROUP_W))
    return jnp.concatenate(outs, axis=1)


def _local_step(x, target, p):
    cw = p["conv_a_w"].shape[1]
    d = x.shape[1]
    abcv, qkv, gates = _rms_matmul(x, p["norm_mix_g"], p["w_in"], (3 * cw, 3 * ATTN_W, 2 * d), "in_proj")
    qs = _to_streams(qkv[:, 0:ATTN_W])
    ks = _to_streams(qkv[:, ATTN_W:2 * ATTN_W])
    vs = _to_streams(qkv[:, 2 * ATTN_W:])
    o_s, lse_s = _attn_fwd(qs, ks, vs)
    o = _from_streams(o_s)
    lse = _from_streams(lse_s)
    x1, ya, yb, yap, ybp, merged = _mixer_out(x, abcv, gates, o, lse, p["conv_a_w"], p["conv_a_b"], p["b_gate"],
                                              p["w_proj_a"], p["w_proj_b"], p["w_out"])
    (up_pre,) = _rms_matmul(x1, p["norm_ffn_g"], p["w_up"], (p["w_up"].shape[1],), "up_proj")
    act, dx2, acc_gf, loss = _ffn_loss(x1, up_pre, target, p["ffn_conv_w"], p["ffn_conv_b"], p["w_down"],
                                       p["final_norm_g"])

    dup, acc_fc = _ffn_act_bwd(dx2, up_pre, p["ffn_conv_w"], p["ffn_conv_b"], p["w_down_t"])
    g_w_down = _tn_matmul(act, dx2, "dw_down")
    dpre, dx1, acc_g2 = _ffn_up_bwd(dup, x1, dx2, p["ffn_conv_w"], p["w_up_t"], p["norm_ffn_g"])
    g_w_up = _tn_matmul(x1, dpre, "dw_up", gain=p["norm_ffn_g"])
    dgates, dyap, dybp, dya, do, dl, acc_bg = _mixer_bwd(dx1, gates, yap, ybp, o, lse, p["b_gate"], p["w_out_t"],
                                                         p["w_proj_a_t"], p["w_proj_b_t"])
    g_w_out = _tn_matmul(merged, dx1, "dw_out")
    g_w_pa = _tn_matmul(ya, dyap, "dw_proj_a")
    g_w_pb = _tn_matmul(yb, dybp, "dw_proj_b")
    dq_s, dk_s, dv_s = _attn_bwd(qs, ks, vs, _to_streams(do), lse_s, _to_streams(dl))
    dqkv = jnp.concatenate([_from_streams(dq_s), _from_streams(dk_s), _from_streams(dv_s)], axis=1)
    dabcv, acc_ca = _conv_mixer_bwd(abcv, dya, p["conv_a_w"], p["conv_a_b"])
    dx, acc_g1 = _in_proj_bwd(x, dx1, dabcv, dqkv, dgates, p["w_in_t"], p["norm_mix_g"])
    g_w_in = jnp.concatenate([_tn_matmul(x, part, nm, gain=p["norm_mix_g"])
                              for part, nm in ((dabcv, "dw_in_a"), (dqkv, "dw_in_qkv"), (dgates, "dw_in_g"))], axis=1)
    big = dict(w_in=g_w_in, w_proj_a=g_w_pa, w_proj_b=g_w_pb, w_out=g_w_out, w_up=g_w_up, w_down=g_w_down)
    small = dict(norm_mix_g=acc_g1[0:1], b_gate=acc_bg[0:2], conv_a_w=acc_ca[0:3], conv_a_b=acc_ca[3:4],
                 norm_ffn_g=acc_g2[0:1], ffn_conv_w=acc_fc[0:3], ffn_conv_b=acc_fc[3:4], final_norm_g=acc_gf[0:1])
    return loss[0, 0], dx, big, small


def _mesh_pos():
    return lax.axis_index("x"), lax.axis_index("y"), lax.axis_index("c")


def _dev_index(px, py, pc):
    return 4 * px + 2 * py + pc


ANY = pl.BlockSpec(memory_space=pl.ANY)


def _all_gather(shards):
    n = len(shards)

    def body(*refs):
        ins, outs = refs[:n], refs[n:2 * n]
        send_sems, recv_sems, local_sems = refs[2 * n:]
        x, y, c = _mesh_pos()
        me, sibling = (x, y, c), (x, y, 1 - c)
        chips = [(1 - x, y), (x, 1 - y), (1 - x, 1 - y)]

        def copy(i, k, block, to, src=None):
            rows = outs[i].at[_dev_index(*block)]
            return pltpu.make_async_remote_copy(
                src_ref=rows if src is None else src, dst_ref=rows, send_sem=send_sems.at[i, k],
                recv_sem=recv_sems.at[i, k], device_id=to, device_id_type=MESH)

        mine, first, passed = [], [], []
        for i in range(n):
            cp = pltpu.make_async_copy(ins[i], outs[i].at[_dev_index(*me)], local_sems.at[i])
            cp.start()
            mine.append(cp)
            first.append(copy(i, 0, me, sibling, src=ins[i]))
            first += [copy(i, 1 + j, me, (*chip, c), src=ins[i]) for j, chip in enumerate(chips)]
        for cp in first:
            cp.start()
        for i in range(n):
            for j, chip in enumerate(chips):
                copy(i, 1 + j, (*chip, c), me).wait_recv()
                fw = copy(i, 4 + j, (*chip, c), sibling)
                fw.start()
                passed.append(fw)
        for i in range(n):
            copy(i, 0, sibling, me).wait_recv()
            for j, chip in enumerate(chips):
                copy(i, 4 + j, (*chip, 1 - c), me).wait_recv()
        for cp in first + passed:
            cp.wait_send()
        for cp in mine:
            cp.wait()

    return pl.pallas_call(
        body, name="all_gather_weights",
        in_specs=[ANY] * n, out_specs=[ANY] * n,
        out_shape=[jax.ShapeDtypeStruct((N_DEV,) + s.shape, s.dtype) for s in shards],
        scratch_shapes=[pltpu.SemaphoreType.DMA((n, 7)), pltpu.SemaphoreType.DMA((n, 7)),
                        pltpu.SemaphoreType.DMA((n,))],
    )(*shards)


def _exchange_sibling(parts):
    n = len(parts)

    def body(*refs):
        ins, outs = refs[:n], refs[n:2 * n]
        send_sems, recv_sems = refs[2 * n:]
        x, y, c = _mesh_pos()
        sibling = (x, y, 1 - c)
        copies = []
        for i in range(n):
            for q in range(4):
                cp = pltpu.make_async_remote_copy(
                    src_ref=ins[i].at[2 * q + 1 - c], dst_ref=outs[i].at[q], send_sem=send_sems.at[i, q],
                    recv_sem=recv_sems.at[i, q], device_id=sibling, device_id_type=MESH)
                cp.start()
                copies.append(cp)
        for cp in copies:
            cp.wait()

    return pl.pallas_call(
        body, name="grad_exchange_sibling",
        in_specs=[ANY] * n, out_specs=[ANY] * n,
        out_shape=[jax.ShapeDtypeStruct((4,) + s.shape[1:], s.dtype) for s in parts],
        scratch_shapes=[pltpu.SemaphoreType.DMA((n, 4)), pltpu.SemaphoreType.DMA((n, 4))],
    )(*parts)


def _add_sibling(part, got, core):
    _, r, c = part.shape
    tr = r if r <= 512 else 256

    def body(core_ref, p_ref, g_ref, o_ref):
        del core_ref
        o_ref[...] = (p_ref[...].astype(F32) + g_ref[...].astype(F32)).astype(BF16)

    return pl.pallas_call(
        body, name="grad_add_sibling",
        grid_spec=pltpu.PrefetchScalarGridSpec(
            num_scalar_prefetch=1, grid=(4, r // tr),
            in_specs=[pl.BlockSpec((None, tr, c), lambda q, i, core: (2 * q + core[0], i, 0)),
                      pl.BlockSpec((None, tr, c), lambda q, i, core: (q, i, 0))],
            out_specs=pl.BlockSpec((None, tr, c), lambda q, i, core: (q, i, 0))),
        out_shape=jax.ShapeDtypeStruct((4, r, c), BF16),
        compiler_params=_params("parallel", "parallel"),
    )(core, part, got)


def _exchange_chips(sums):
    n = len(sums)

    def body(*refs):
        ins, outs = refs[:n], refs[n:2 * n]
        send_sems, recv_sems = refs[2 * n:]
        x, y, c = _mesh_pos()
        chips = [(1 - x, y), (x, 1 - y), (1 - x, 1 - y)]
        copies = []
        for i in range(n):
            for k, (px, py) in enumerate(chips):
                cp = pltpu.make_async_remote_copy(
                    src_ref=ins[i].at[2 * px + py], dst_ref=outs[i].at[k], send_sem=send_sems.at[i, k],
                    recv_sem=recv_sems.at[i, k], device_id=(px, py, c), device_id_type=MESH)
                cp.start()
                copies.append(cp)
        for cp in copies:
            cp.wait()

    return pl.pallas_call(
        body, name="grad_exchange_chips",
        in_specs=[ANY] * n, out_specs=[ANY] * n,
        out_shape=[jax.ShapeDtypeStruct((3,) + s.shape[1:], s.dtype) for s in sums],
        scratch_shapes=[pltpu.SemaphoreType.DMA((n, 3)), pltpu.SemaphoreType.DMA((n, 3))],
    )(*sums)


def _all_reduce_small(v):
    r = v.shape[0]

    def body(v_ref, o_ref, gath, send_sems, recv_sems):
        x, y, c = _mesh_pos()
        me = _dev_index(x, y, c)
        gath[me] = v_ref[...]
        flips = [(kx, ky, kc) for kx in (0, 1) for ky in (0, 1) for kc in (0, 1)][1:]
        copies = []
        for k, (kx, ky, kc) in enumerate(flips):
            px = 1 - x if kx else x
            py = 1 - y if ky else y
            pc = 1 - c if kc else c
            cp = pltpu.make_async_remote_copy(
                src_ref=v_ref, dst_ref=gath.at[me], send_sem=send_sems.at[k], recv_sem=recv_sems.at[k],
                device_id=(px, py, pc), device_id_type=MESH)
            cp.start()
            copies.append((cp, _dev_index(px, py, pc)))
        for k, (cp, peer) in enumerate(copies):
            pltpu.make_async_remote_copy(
                src_ref=v_ref, dst_ref=gath.at[peer], send_sem=send_sems.at[k], recv_sem=recv_sems.at[k],
                device_id=(x, y, c), device_id_type=MESH).wait_recv()
        for cp, _ in copies:
            cp.wait_send()
        total = gath[0]
        for j in range(1, N_DEV):
            total = total + gath[j]
        o_ref[...] = total

    return pl.pallas_call(
        body, name="all_reduce_small",
        in_specs=[pl.BlockSpec(memory_space=pltpu.VMEM)], out_specs=pl.BlockSpec(memory_space=pltpu.VMEM),
        out_shape=jax.ShapeDtypeStruct((r, LANES), F32),
        scratch_shapes=[pltpu.VMEM((N_DEV, r, LANES), F32), pltpu.SemaphoreType.DMA((7,)),
                        pltpu.SemaphoreType.DMA((7,))],
    )(v)


def _adamw_math(w, g, m, v):
    m2 = ADAM_B1 * m + (1.0 - ADAM_B1) * g
    v2 = ADAM_B2 * v + (1.0 - ADAM_B2) * (g * g)
    m_hat = m2 / (1.0 - ADAM_B1 ** ADAM_STEP)
    v_hat = v2 / (1.0 - ADAM_B2 ** ADAM_STEP)
    delta = -ADAM_LR * (m_hat / (jnp.sqrt(v_hat) + ADAM_EPS) + ADAM_WD * w)
    return delta, m2, v2


def _adamw_big(w, m, v, chip_sum, got, chip):
    r, c = w.shape
    tr = r if r <= 512 else 256

    def body(chip_ref, w_ref, m_ref, v_ref, s_ref, g0_ref, g1_ref, g2_ref, g_out, d_out, m_out, v_out):
        del chip_ref
        g = (s_ref[...].astype(F32) + g0_ref[...].astype(F32) + g1_ref[...].astype(F32) + g2_ref[...].astype(F32))
        delta, m2, v2 = _adamw_math(w_ref[...], g, m_ref[...], v_ref[...])
        g_out[...] = g
        d_out[...] = delta
        m_out[...] = m2
        v_out[...] = v2

    plain = pl.BlockSpec((tr, c), lambda i, chip: (i, 0))
    out = jax.ShapeDtypeStruct((r, c), F32)
    return pl.pallas_call(
        body, name="adamw_big",
        grid_spec=pltpu.PrefetchScalarGridSpec(
            num_scalar_prefetch=1, grid=(r // tr,),
            in_specs=[plain, plain, plain, pl.BlockSpec((None, tr, c), lambda i, chip: (chip[0], i, 0))]
            + [pl.BlockSpec((None, tr, c), functools.partial(lambda k, i, chip: (k, i, 0), k)) for k in range(3)],
            out_specs=[plain] * 4),
        out_shape=[out] * 4,
        compiler_params=_params("parallel"),
    )(chip, w, m, v, chip_sum, got, got, got)


def _adamw_small(w, g, m, v):
    def body(w_ref, g_ref, m_ref, v_ref, d_out, m_out, v_out):
        delta, m2, v2 = _adamw_math(w_ref[...], g_ref[...], m_ref[...], v_ref[...])
        d_out[...] = delta
        m_out[...] = m2
        v_out[...] = v2

    out = jax.ShapeDtypeStruct(w.shape, F32)
    return pl.pallas_call(body, name="adamw_small", out_shape=[out] * 3)(w, g, m, v)


BIG = ("w_in", "w_proj_a", "w_proj_b", "w_out", "w_up", "w_down")
ROW_SHARDED = ("w_out", "w_down")
SMALL = ("norm_mix_g", "b_gate", "conv_a_w", "conv_a_b", "norm_ffn_g", "ffn_conv_w", "ffn_conv_b", "final_norm_g")
SMALL_SHARDED = ("b_gate", "conv_a_w", "ffn_conv_w")
WEIGHTS = ("norm_mix_g", "w_in", "b_gate", "conv_a_w", "conv_a_b", "w_proj_a", "w_proj_b", "w_out", "norm_ffn_g",
           "w_up", "ffn_conv_w", "ffn_conv_b", "w_down", "final_norm_g")


def _pack(vectors, rows):
    flat = jnp.concatenate([v.reshape(-1) for v in vectors])
    return jnp.pad(flat, (0, rows * LANES - flat.shape[0])).reshape(rows, LANES)


def _packed_rows(count):
    rows = -(-count // LANES)
    return -(-rows // SUBLANES) * SUBLANES


def _unpack(packed, shapes):
    flat = packed.reshape(-1)
    out, lo = [], 0
    for s in shapes:
        size = 1
        for dim in s:
            size *= dim
        out.append(flat[lo:lo + size].reshape(s))
        lo += size
    return out


def _full_from_gathered(name, gathered):
    _, r, c = gathered.shape
    if name in ROW_SHARDED:
        full = gathered.reshape(N_DEV * r, c)
        return full, full.T
    return (gathered.transpose(1, 0, 2).reshape(r, N_DEV * c), gathered.transpose(0, 2, 1).reshape(N_DEV * c, r))


def _by_destination(name, grad):
    rr, cc = grad.shape
    g = grad.astype(BF16)
    if name in ROW_SHARDED:
        return g.reshape(N_DEV, rr // N_DEV, cc)
    return g.reshape(rr, N_DEV, cc // N_DEV).transpose(1, 0, 2)


def kernel(x, norm_mix_g, w_in, b_gate, conv_a_w, conv_a_b, w_proj_a, w_proj_b, w_out, norm_ffn_g, w_up, ffn_conv_w, ffn_conv_b, w_down, final_norm_g, loss_target, m_norm_mix_g, m_w_in, m_b_gate, m_conv_a_w, m_conv_a_b, m_w_proj_a, m_w_proj_b, m_w_out, m_norm_ffn_g, m_w_up, m_ffn_conv_w, m_ffn_conv_b, m_w_down, m_final_norm_g, v_norm_mix_g, v_w_in, v_b_gate, v_conv_a_w, v_conv_a_b, v_w_proj_a, v_w_proj_b, v_w_out, v_norm_ffn_g, v_w_up, v_ffn_conv_w, v_ffn_conv_b, v_w_down, v_final_norm_g):
    given = dict(locals())
    shard = {n: given[n] for n in WEIGHTS}
    mom_m = {n: given["m_" + n] for n in WEIGHTS}
    mom_v = {n: given["v_" + n] for n in WEIGHTS}
    xi, yi, ci = _mesh_pos()
    me = _dev_index(xi, yi, ci)
    chip = (2 * xi + yi).astype(jnp.int32).reshape(1)
    core = ci.astype(jnp.int32).reshape(1)

    big2d = {n: shard[n].reshape(shard[n].shape[-2:]) for n in BIG}
    small_shapes = [shard[n].shape[1:] for n in SMALL_SHARDED]
    n_small = sum(s[0] * s[1] for s in small_shapes)
    packed_small = _pack([shard[n] for n in SMALL_SHARDED], _packed_rows(n_small))
    gathered = _all_gather([big2d[n].astype(BF16) for n in BIG] + [packed_small])
    p = {}
    for n, g in zip(BIG, gathered[:-1]):
        p[n], p[n + "_t"] = _full_from_gathered(n, g)
    per_dev = [_unpack(gathered[-1][j], small_shapes) for j in range(N_DEV)]
    for i, n in enumerate(SMALL_SHARDED):
        p[n] = jnp.concatenate([per_dev[j][i] for j in range(N_DEV)], axis=1)
    p["norm_mix_g"], p["norm_ffn_g"] = shard["norm_mix_g"], shard["norm_ffn_g"]
    p["conv_a_b"], p["ffn_conv_b"] = shard["conv_a_b"], shard["ffn_conv_b"]
    p["final_norm_g"] = shard["final_norm_g"].reshape(1, -1)

    loss_part, dx, g_big, g_small = _local_step(x[0], loss_target[0], p)

    parts = [_by_destination(n, g_big[n]) for n in BIG]
    from_sibling = _exchange_sibling(parts)
    chip_sums = [_add_sibling(a, b, core) for a, b in zip(parts, from_sibling)]
    from_chips = _exchange_chips(chip_sums)
    results = {}
    for n, s, got in zip(BIG, chip_sums, from_chips):
        lead = shard[n].shape
        outs = _adamw_big(big2d[n], mom_m[n].reshape(lead[-2:]), mom_v[n].reshape(lead[-2:]), s, got, chip)
        results[n] = [o.reshape(lead) for o in outs]

    small_full_shapes = [g_small[n].shape for n in SMALL]
    n_vec = sum(s[0] * s[1] for s in small_full_shapes) + 1
    packed = _pack([g_small[n] for n in SMALL] + [loss_part.reshape(1)], _packed_rows(n_vec))
    reduced = _all_reduce_small(packed)
    *g_full, loss_vec = _unpack(reduced, small_full_shapes + [(1,)])
    loss = loss_vec[0]
    own_g = []
    for n, g in zip(SMALL, g_full):
        if n in SMALL_SHARDED:
            width = shard[n].shape[-1]
            g = lax.dynamic_slice_in_dim(g, me * width, width, axis=1)
        own_g.append(g.reshape(shard[n].shape))
    own_shapes = [shard[n].shape for n in SMALL]
    rows = _packed_rows(sum(g.size for g in own_g))
    small_out = _adamw_small(_pack([shard[n] for n in SMALL], rows), _pack(own_g, rows),
                             _pack([mom_m[n] for n in SMALL], rows), _pack([mom_v[n] for n in SMALL], rows))
    deltas, new_ms, new_vs = (_unpack(o, own_shapes) for o in small_out)
    for i, n in enumerate(SMALL):
        results[n] = [own_g[i], deltas[i], new_ms[i], new_vs[i]]

    grad_x = dx.reshape(x.shape)
    return (loss, grad_x, *[results[n][0] for n in WEIGHTS], *[results[n][1] for n in WEIGHTS],
            *[results[n][2] for n in WEIGHTS], *[results[n][3] for n in WEIGHTS])
```

```python
import functools

import jax
import jax.numpy as jnp
from jax import lax
from jax.experimental import pallas as pl
from jax.experimental.pallas import tpu as pltpu

F32 = jnp.float32
BF16 = jnp.bfloat16
MESH = pl.DeviceIdType.MESH

N_DEV = 8
RMS_EPS = 1e-6
NEG_INF = -1e30
N_GROUPS = 3
HEADS_PER_GROUP = 4
HEAD_DIM = 64
GROUP_W = HEADS_PER_GROUP * HEAD_DIM
ATTN_W = N_GROUPS * GROUP_W
QBLK = 128
ATTN_SCALE = HEAD_DIM ** -0.5

ADAM_LR = 0.001
ADAM_B1 = 0.9
ADAM_B2 = 0.999
ADAM_EPS = 1e-08
ADAM_WD = 0.01
ADAM_STEP = 10

HALO = 16
LANES = 128
SUBLANES = 8
VMEM_LIMIT_BYTES = 56 * 1024 * 1024


def _params(*sem):
    return pltpu.CompilerParams(dimension_semantics=sem, vmem_limit_bytes=VMEM_LIMIT_BYTES)


def _pick_tile(n, cap):
    if n <= cap:
        return n
    best = None
    for t in range(LANES, cap + 1, LANES):
        if n % t == 0:
            best = t
    assert best is not None, (n, cap)
    return best


def _rows(tm, c, j=0):
    return pl.BlockSpec((tm, c), lambda m: (m, j))


def _prev_halo(tm, c):
    return pl.BlockSpec((HALO, c), lambda m: (jnp.maximum(m * (tm // HALO) - 1, 0), 0))


def _next_halo(tm, c, t_total):
    last = t_total // HALO - 1
    return pl.BlockSpec((HALO, c), lambda m: (jnp.minimum((m + 1) * (tm // HALO), last), 0))


def _resident(shape):
    nd = len(shape)
    return pl.BlockSpec(shape, lambda *_: (0,) * nd, pipeline_mode=pl.Buffered(1))


def _acc_spec(c):
    return pl.BlockSpec((SUBLANES, c), lambda *_: (0, 0))


def _shift_down(u, halo, k):
    ext = jnp.concatenate([halo, u], axis=0)
    return pltpu.roll(ext, k, 0)[HALO:, :]


def _shift_up(u, halo, k):
    ext = jnp.concatenate([u, halo], axis=0)
    return pltpu.roll(ext, ext.shape[0] - k, 0)[: u.shape[0], :]


def _stack_rows(rows, c):
    idx = lax.broadcasted_iota(jnp.int32, (SUBLANES, c), 0)
    out = jnp.zeros((SUBLANES, c), F32)
    for i, r in enumerate(rows):
        out = out + jnp.where(idx == i, r, 0.0)
    return out


def _colsum(v):
    return jnp.sum(v, axis=0, keepdims=True)


def _sigmoid(v):
    return 1.0 / (1.0 + jnp.exp(-v))


def _rms_fwd(xv, g):
    r = lax.rsqrt(jnp.mean(xv * xv, axis=-1, keepdims=True) + RMS_EPS)
    return xv * r * g, r


def _rms_bwd(xv, g, dy):
    r = lax.rsqrt(jnp.mean(xv * xv, axis=-1, keepdims=True) + RMS_EPS)
    xn = xv * r
    dxn = dy * g
    dx = r * (dxn - xn * jnp.mean(dxn * xn, axis=-1, keepdims=True))
    return dx, dy * xn


def _dot(a, b):
    return jnp.dot(a, b, preferred_element_type=F32)


def _dot_nt(a, b):
    return lax.dot_general(a, b, (((1,), (1,)), ((), ())), preferred_element_type=F32)


def _dot_tn(a, b):
    return lax.dot_general(a, b, (((0,), (0,)), ((), ())), preferred_element_type=F32)


ANY = pl.BlockSpec(memory_space=pl.ANY)


def _mesh_pos():
    return lax.axis_index("x"), lax.axis_index("y"), lax.axis_index("c")


def _dev_index(px, py, pc):
    return 4 * px + 2 * py + pc


class _Exchange:
    def __init__(self, mode, arrays):
        self.mode, self.arrays = mode, list(arrays)
        n = len(self.arrays)
        if mode == "gather":
            self.out_shape = [jax.ShapeDtypeStruct((N_DEV,) + a.shape, a.dtype) for a in self.arrays]
        else:
            self.out_shape = [jax.ShapeDtypeStruct(a.shape, a.dtype) for a in self.arrays]
        self.scratch = [pltpu.SemaphoreType.DMA((n, N_DEV - 1)), pltpu.SemaphoreType.DMA((n, N_DEV - 1)),
                        pltpu.SemaphoreType.DMA((n,))]

    def _peers(self):
        x, y, c = _mesh_pos()
        flips = [(kx, ky, kc) for kx in (0, 1) for ky in (0, 1) for kc in (0, 1)][1:]
        peers = [(1 - x if kx else x, 1 - y if ky else y, 1 - c if kc else c) for kx, ky, kc in flips]
        return _dev_index(x, y, c), peers

    def _copy(self, ins, outs, sems, i, k, peer, me, sending):
        src = ins[i] if self.mode == "gather" else ins[i].at[_dev_index(*peer)]
        dst = outs[i].at[me if sending else _dev_index(*peer)]
        return pltpu.make_async_remote_copy(src_ref=src, dst_ref=dst, send_sem=sems[0].at[i, k],
                                            recv_sem=sems[1].at[i, k], device_id=peer, device_id_type=MESH)

    def _own(self, ins, outs, sems, i, me):
        return pltpu.make_async_copy(ins[i], outs[i].at[me], sems[2].at[i])

    def start(self, ins, outs, sems):
        me, peers = self._peers()
        for i in range(len(ins)):
            if self.mode == "gather":
                self._own(ins, outs, sems, i, me).start()
            for k, peer in enumerate(peers):
                self._copy(ins, outs, sems, i, k, peer, me, True).start()

    def wait(self, ins, outs, sems):
        me, peers = self._peers()
        for i in range(len(ins)):
            for k, peer in enumerate(peers):
                self._copy(ins, outs, sems, i, k, peer, me, False).wait_recv()
            for k, peer in enumerate(peers):
                self._copy(ins, outs, sems, i, k, peer, me, True).wait_send()
            if self.mode == "gather":
                self._own(ins, outs, sems, i, me).wait()


def _call(body, *, name, grid, in_specs, out_specs, out_shape, args, semantics, carry=None):
    if carry is None:
        return pl.pallas_call(body, name=name, grid=grid, in_specs=in_specs, out_specs=out_specs,
                              out_shape=out_shape, compiler_params=_params(*semantics))(*args)
    n_in, n_out, n_x = len(in_specs), len(out_specs), len(carry.arrays)

    def carried(*refs):
        ins, x_ins = refs[:n_in], refs[n_in:n_in + n_x]
        outs = refs[n_in + n_x:n_in + n_x + n_out]
        x_outs = refs[n_in + n_x + n_out:n_in + 2 * n_x + n_out]
        sems = refs[n_in + 2 * n_x + n_out:]
        first = functools.reduce(jnp.logical_and, [pl.program_id(a) == 0 for a in range(len(grid))])
        last = functools.reduce(jnp.logical_and, [pl.program_id(a) == grid[a] - 1 for a in range(len(grid))])

        @pl.when(first)
        def _():
            carry.start(x_ins, x_outs, sems)

        body(*ins, *outs)

        @pl.when(last)
        def _():
            carry.wait(x_ins, x_outs, sems)

    res = pl.pallas_call(
        carried, name=name, grid=grid, in_specs=list(in_specs) + [ANY] * n_x,
        out_specs=list(out_specs) + [ANY] * n_x, out_shape=list(out_shape) + carry.out_shape,
        scratch_shapes=carry.scratch, compiler_params=_params(*["arbitrary"] * len(grid)),
    )(*args, *carry.arrays)
    return list(res[:n_out]), list(res[n_out:])


def _rms_matmul(x, g, w, widths, name, carry=None):
    t, d = x.shape
    n = w.shape[1]
    assert sum(widths) == n
    tm = min(256, t)
    spans, lo = [], 0
    for wd in widths:
        spans.append((lo, wd))
        lo += wd

    def body(x_ref, g_ref, w_ref, *o_refs):
        h, _ = _rms_fwd(x_ref[...], g_ref[...])
        h = h.astype(BF16)
        for o_ref, (c0, wd) in zip(o_refs, spans):
            o_ref[...] = _dot(h, w_ref[:, c0:c0 + wd]).astype(BF16)

    return _call(
        body, name=name, grid=(t // tm,),
        in_specs=[_rows(tm, d), _resident((1, d)), _resident((d, n))],
        out_specs=[_rows(tm, wd) for wd in widths],
        out_shape=[jax.ShapeDtypeStruct((t, wd), BF16) for wd in widths],
        args=(x, g, w), semantics=("parallel",), carry=carry)


def _head_masks():
    lane = lax.broadcasted_iota(jnp.int32, (1, GROUP_W), 1)
    return lane, [(lane // HEAD_DIM) == h for h in range(HEADS_PER_GROUP)]


def _band_masks(has_other):
    row = lax.broadcasted_iota(jnp.int32, (QBLK, QBLK), 0)
    col = lax.broadcasted_iota(jnp.int32, (QBLK, QBLK), 1)
    return col <= row, (col >= row) & has_other


def _attn_fwd(qs, ks, vs, carry=None):
    ng, t, _ = qs.shape
    nb = t // QBLK

    def body(q_ref, kc_ref, kp_ref, vc_ref, vp_ref, o_ref, lse_ref):
        g = pl.program_id(0)
        b = pl.program_id(1)
        per_stream = nb >> (2 * g)
        has_prev = lax.rem(b, per_stream) != 0
        mask_c, mask_p = _band_masks(has_prev)
        _, heads = _head_masks()
        q, kc, kp, vc, vp = q_ref[0], kc_ref[0], kp_ref[0], vc_ref[0], vp_ref[0]
        o_acc = jnp.zeros((QBLK, GROUP_W), F32)
        lse_acc = jnp.zeros((QBLK, GROUP_W), F32)
        for hm in heads:
            qh = jnp.where(hm, q, jnp.zeros_like(q))
            sc = jnp.where(mask_c, _dot_nt(qh, kc) * ATTN_SCALE, NEG_INF)
            sp = jnp.where(mask_p, _dot_nt(qh, kp) * ATTN_SCALE, NEG_INF)
            mx = jnp.maximum(jnp.max(sc, axis=1, keepdims=True), jnp.max(sp, axis=1, keepdims=True))
            pc = jnp.exp(sc - mx)
            pp = jnp.exp(sp - mx)
            den = jnp.sum(pc, axis=1, keepdims=True) + jnp.sum(pp, axis=1, keepdims=True)
            oh = _dot(pc.astype(BF16), vc) + _dot(pp.astype(BF16), vp)
            o_acc = jnp.where(hm, oh / den, o_acc)
            lse_acc = jnp.where(hm, mx + jnp.log(den), lse_acc)
        o_ref[0] = o_acc.astype(BF16)
        lse_ref[0] = lse_acc

    blk = (1, QBLK, GROUP_W)
    cur = pl.BlockSpec(blk, lambda g, b: (g, b, 0))
    prev = pl.BlockSpec(blk, lambda g, b: (g, jnp.maximum(b - 1, 0), 0))
    return _call(
        body, name="attn_fwd", grid=(ng, nb),
        in_specs=[cur, cur, prev, cur, prev],
        out_specs=[cur, cur],
        out_shape=[jax.ShapeDtypeStruct((ng, t, GROUP_W), BF16), jax.ShapeDtypeStruct((ng, t, GROUP_W), F32)],
        args=(qs, ks, ks, vs, vs), semantics=("parallel", "parallel"), carry=carry)


def _group_softmax(lse):
    parts = [lse[:, i * GROUP_W:(i + 1) * GROUP_W] for i in range(N_GROUPS)]
    mx = jnp.maximum(jnp.maximum(parts[0], parts[1]), parts[2])
    es = [jnp.exp(p - mx) for p in parts]
    den = es[0] + es[1] + es[2]
    return [e / den for e in es]


def _mixer_out(x, abcv, gates, o, lse, conv_w, conv_b, b_gate, w_pa, w_pb, w_o):
    t, d = x.shape
    cw = conv_w.shape[1]
    tm = min(256, t)

    def body(x_ref, abcv_ref, halo_ref, gates_ref, o_ref, lse_ref, cw_ref, cb_ref, bg_ref, wpa_ref, wpb_ref,
             wo_ref, x1_ref, ya_ref, yb_ref, yap_ref, ybp_ref, mg_ref):
        m = pl.program_id(0)
        ab = abcv_ref[:, 0:cw].astype(F32)
        u = abcv_ref[:, cw:2 * cw].astype(F32) * abcv_ref[:, 2 * cw:3 * cw].astype(F32)
        hu = halo_ref[:, cw:2 * cw].astype(F32) * halo_ref[:, 2 * cw:3 * cw].astype(F32)
        hu = jnp.where(m > 0, hu, 0.0)
        cv = (cw_ref[0:1, :] * _shift_down(u, hu, 2) + cw_ref[1:2, :] * _shift_down(u, hu, 1)
              + cw_ref[2:3, :] * u + cb_ref[...])
        ya = (ab * cv).astype(BF16)
        ya_ref[...] = ya
        alphas = _group_softmax(lse_ref[...])
        for i in range(N_GROUPS):
            sl = slice(i * GROUP_W, (i + 1) * GROUP_W)
            yb_ref[:, sl] = (alphas[i] * o_ref[:, sl].astype(F32)).astype(BF16)
        yap = _dot(ya, wpa_ref[...])
        ybp = _dot(yb_ref[...], wpb_ref[...])
        yap_ref[...] = yap.astype(BF16)
        ybp_ref[...] = ybp.astype(BF16)
        sa = _sigmoid(gates_ref[:, 0:d].astype(F32) + bg_ref[0:1, :])
        sb = _sigmoid(gates_ref[:, d:2 * d].astype(F32) + bg_ref[1:2, :])
        merged = (sa * yap + sb * ybp).astype(BF16)
        mg_ref[...] = merged
        x1_ref[...] = x_ref[...] + _dot(merged, wo_ref[...])

    return pl.pallas_call(
        body, name="mixer_out", grid=(t // tm,),
        in_specs=[_rows(tm, d), _rows(tm, 3 * cw), _prev_halo(tm, 3 * cw), _rows(tm, 2 * d), _rows(tm, ATTN_W),
                  _rows(tm, ATTN_W), _resident((3, cw)), _resident((1, cw)), _resident((2, d)),
                  _resident((cw, d)), _resident((ATTN_W, d)), _resident((d, d))],
        out_specs=[_rows(tm, d), _rows(tm, cw), _rows(tm, ATTN_W), _rows(tm, d), _rows(tm, d), _rows(tm, d)],
        out_shape=[jax.ShapeDtypeStruct((t, d), F32), jax.ShapeDtypeStruct((t, cw), BF16),
                   jax.ShapeDtypeStruct((t, ATTN_W), BF16), jax.ShapeDtypeStruct((t, d), BF16),
                   jax.ShapeDtypeStruct((t, d), BF16), jax.ShapeDtypeStruct((t, d), BF16)],
        compiler_params=_params("parallel"),
    )(x, abcv, abcv, gates, o, lse, conv_w, conv_b, b_gate, w_pa, w_pb, w_o)


def _ffn_conv(p_ref, halo_ref, w_ref, b_ref, m, c0, wd):
    p = p_ref[:, c0:c0 + wd].astype(F32)
    hp = jnp.where(m > 0, halo_ref[:, c0:c0 + wd].astype(F32), 0.0)
    p1 = _shift_down(p, hp, 1)
    p2 = _shift_down(p, hp, 2)
    up = (w_ref[0:1, c0:c0 + wd] * p2 + w_ref[1:2, c0:c0 + wd] * p1 + w_ref[2:3, c0:c0 + wd] * p
          + b_ref[:, c0:c0 + wd])
    return up, p, p1, p2


def _ffn_loss(x1, up_pre, target, conv_w, conv_b, w_d, g_f):
    t, d = x1.shape
    dff = w_d.shape[0]
    tm = min(256, t)
    ck = _pick_tile(dff, 1408)

    def body(x1_ref, up_ref, halo_ref, tg_ref, cw_ref, cb_ref, wd_ref, gf_ref, act_ref, dx2_ref, acc_ref, loss_ref):
        m = pl.program_id(0)

        @pl.when(m == 0)
        def _():
            acc_ref[...] = jnp.zeros_like(acc_ref)
            loss_ref[...] = jnp.zeros_like(loss_ref)

        x2 = x1_ref[...]
        for c0 in range(0, dff, ck):
            gate = _ffn_conv(up_ref, halo_ref, cw_ref, cb_ref, m, c0, ck)[0]
            val = _ffn_conv(up_ref, halo_ref, cw_ref, cb_ref, m, dff + c0, ck)[0]
            act = (gate * _sigmoid(gate) * val).astype(BF16)
            act_ref[:, c0:c0 + ck] = act
            x2 = x2 + _dot(act, wd_ref[c0:c0 + ck, :])
        y, _ = _rms_fwd(x2, gf_ref[...])
        diff = y - tg_ref[...]
        loss_ref[...] += 0.5 * jnp.sum(jnp.mean(diff * diff, axis=-1, keepdims=True))
        dx2, dg = _rms_bwd(x2, gf_ref[...], diff * (1.0 / d))
        dx2_ref[...] = dx2
        acc_ref[...] += _stack_rows([_colsum(dg)], d)

    return pl.pallas_call(
        body, name="ffn_loss", grid=(t // tm,),
        in_specs=[_rows(tm, d), _rows(tm, 2 * dff), _prev_halo(tm, 2 * dff), _rows(tm, d),
                  _resident((3, 2 * dff)), _resident((1, 2 * dff)), _resident((dff, d)), _resident((1, d))],
        out_specs=[_rows(tm, dff), _rows(tm, d), _acc_spec(d), _acc_spec(LANES)],
        out_shape=[jax.ShapeDtypeStruct((t, dff), BF16), jax.ShapeDtypeStruct((t, d), F32),
                   jax.ShapeDtypeStruct((SUBLANES, d), F32), jax.ShapeDtypeStruct((SUBLANES, LANES), F32)],
        compiler_params=_params("arbitrary"),
    )(x1, up_pre, up_pre, target, conv_w, conv_b, w_d, g_f)


def _ffn_act_bwd(dx2, up_pre, conv_w, conv_b, w_d):
    t, d = dx2.shape
    dff = w_d.shape[0]
    tm = min(256, t)
    ck = _pick_tile(dff, 1408)

    def body(dx2_ref, up_ref, halo_ref, cw_ref, cb_ref, wd_ref, dup_ref, acc_ref):
        m = pl.program_id(0)

        @pl.when(m == 0)
        def _():
            acc_ref[...] = jnp.zeros_like(acc_ref)

        dx2v = dx2_ref[...].astype(BF16)
        for c0 in range(0, dff, ck):
            dact = _dot_nt(dx2v, wd_ref[c0:c0 + ck, :])
            gate, gp, gp1, gp2 = _ffn_conv(up_ref, halo_ref, cw_ref, cb_ref, m, c0, ck)
            val, vp, vp1, vp2 = _ffn_conv(up_ref, halo_ref, cw_ref, cb_ref, m, dff + c0, ck)
            sg = _sigmoid(gate)
            dval = dact * gate * sg
            dgate = dact * val * sg * (1.0 + gate * (1.0 - sg))
            dup_ref[:, c0:c0 + ck] = dgate.astype(BF16)
            dup_ref[:, dff + c0:dff + c0 + ck] = dval.astype(BF16)
            for base, dv, p0, p1, p2 in ((c0, dgate, gp, gp1, gp2), (dff + c0, dval, vp, vp1, vp2)):
                acc_ref[:, base:base + ck] += _stack_rows(
                    [_colsum(dv * p2), _colsum(dv * p1), _colsum(dv * p0), _colsum(dv)], ck)

    return pl.pallas_call(
        body, name="ffn_act_bwd", grid=(t // tm,),
        in_specs=[_rows(tm, d), _rows(tm, 2 * dff), _prev_halo(tm, 2 * dff), _resident((3, 2 * dff)),
                  _resident((1, 2 * dff)), _resident((dff, d))],
        out_specs=[_rows(tm, 2 * dff), _acc_spec(2 * dff)],
        out_shape=[jax.ShapeDtypeStruct((t, 2 * dff), BF16), jax.ShapeDtypeStruct((SUBLANES, 2 * dff), F32)],
        compiler_params=_params("arbitrary"),
    )(dx2, up_pre, up_pre, conv_w, conv_b, w_d)


def _ffn_up_bwd(dup, x1, dx2, conv_w, w_u, g2, carry=None):
    t, d = x1.shape
    n = dup.shape[1]
    tm = min(256, t)
    ck = _pick_tile(n, 1408)
    last = t // tm - 1

    def body(dup_ref, nxt_ref, x1_ref, dx2_ref, cw_ref, wu_ref, g2_ref, dpre_ref, dx1_ref, acc_ref):
        m = pl.program_id(0)

        @pl.when(m == 0)
        def _():
            acc_ref[...] = jnp.zeros_like(acc_ref)

        dh = jnp.zeros((tm, d), F32)
        for c0 in range(0, n, ck):
            du = dup_ref[:, c0:c0 + ck].astype(F32)
            hn = jnp.where(m < last, nxt_ref[:, c0:c0 + ck].astype(F32), 0.0)
            dpre = (cw_ref[2:3, c0:c0 + ck] * du + cw_ref[1:2, c0:c0 + ck] * _shift_up(du, hn, 1)
                    + cw_ref[0:1, c0:c0 + ck] * _shift_up(du, hn, 2)).astype(BF16)
            dpre_ref[:, c0:c0 + ck] = dpre
            dh = dh + _dot_nt(dpre, wu_ref[:, c0:c0 + ck])
        dx, dg = _rms_bwd(x1_ref[...], g2_ref[...], dh)
        dx1_ref[...] = dx2_ref[...] + dx
        acc_ref[...] += _stack_rows([_colsum(dg)], d)

    return _call(
        body, name="ffn_up_bwd", grid=(t // tm,),
        in_specs=[_rows(tm, n), _next_halo(tm, n, t), _rows(tm, d), _rows(tm, d), _resident((3, n)),
                  _resident((d, n)), _resident((1, d))],
        out_specs=[_rows(tm, n), _rows(tm, d), _acc_spec(d)],
        out_shape=[jax.ShapeDtypeStruct((t, n), BF16), jax.ShapeDtypeStruct((t, d), F32),
                   jax.ShapeDtypeStruct((SUBLANES, d), F32)],
        args=(dup, dup, x1, dx2, conv_w, w_u, g2), semantics=("arbitrary",), carry=carry)


def _tn_matmul(a, b, name, gain=None):
    t, mdim = a.shape
    n = b.shape[1]
    tk = min(512, t)
    tmm = mdim if gain is not None else _pick_tile(mdim, 1408)
    tn = _pick_tile(n, 1536)

    def body(*refs):
        if gain is not None:
            a_ref, b_ref, g_ref, o_ref, acc_ref = refs
        else:
            a_ref, b_ref, o_ref, acc_ref = refs
        k = pl.program_id(2)

        @pl.when(k == 0)
        def _():
            acc_ref[...] = jnp.zeros_like(acc_ref)

        av = a_ref[...]
        if gain is not None:
            av = _rms_fwd(av, g_ref[...])[0]
        acc_ref[...] += _dot_tn(av.astype(BF16), b_ref[...].astype(BF16))

        @pl.when(k == t // tk - 1)
        def _():
            o_ref[...] = acc_ref[...].astype(BF16)

    in_specs = [pl.BlockSpec((tk, tmm), lambda i, j, k: (k, i)), pl.BlockSpec((tk, tn), lambda i, j, k: (k, j))]
    args = [a, b]
    if gain is not None:
        in_specs.append(pl.BlockSpec((1, mdim), lambda i, j, k: (0, 0)))
        args.append(gain)
    return pl.pallas_call(
        body, name=name, grid=(mdim // tmm, n // tn, t // tk),
        in_specs=in_specs,
        out_specs=pl.BlockSpec((tmm, tn), lambda i, j, k: (i, j)),
        out_shape=jax.ShapeDtypeStruct((mdim, n), BF16),
        scratch_shapes=[pltpu.VMEM((tmm, tn), F32)],
        compiler_params=_params("parallel", "parallel", "arbitrary"),
    )(*args)


def _mixer_bwd(dx1, gates, yap, ybp, o, lse, b_gate, w_o, w_pa, w_pb):
    t, d = dx1.shape
    cw = w_pa.shape[0]
    tm = min(256, t)

    def body(dx1_ref, gates_ref, yap_ref, ybp_ref, o_ref, lse_ref, bg_ref, wo_ref, wpa_ref, wpb_ref,
             dgates_ref, dyap_ref, dybp_ref, dya_ref, do_ref, dl_ref, acc_ref):
        m = pl.program_id(0)

        @pl.when(m == 0)
        def _():
            acc_ref[...] = jnp.zeros_like(acc_ref)

        dmg = _dot_nt(dx1_ref[...].astype(BF16), wo_ref[...])
        sa = _sigmoid(gates_ref[:, 0:d].astype(F32) + bg_ref[0:1, :])
        sb = _sigmoid(gates_ref[:, d:2 * d].astype(F32) + bg_ref[1:2, :])
        dyap = (dmg * sa).astype(BF16)
        dybp = (dmg * sb).astype(BF16)
        dga = dmg * yap_ref[...].astype(F32) * sa * (1.0 - sa)
        dgb = dmg * ybp_ref[...].astype(F32) * sb * (1.0 - sb)
        dyap_ref[...] = dyap
        dybp_ref[...] = dybp
        dgates_ref[:, 0:d] = dga.astype(BF16)
        dgates_ref[:, d:2 * d] = dgb.astype(BF16)
        acc_ref[...] += _stack_rows([_colsum(dga), _colsum(dgb)], d)
        dya_ref[...] = _dot_nt(dyap, wpa_ref[...]).astype(BF16)
        dyb = _dot_nt(dybp, wpb_ref[...])

        ri = lax.broadcasted_iota(jnp.int32, (GROUP_W, GROUP_W), 0) // HEAD_DIM
        ci = lax.broadcasted_iota(jnp.int32, (GROUP_W, GROUP_W), 1) // HEAD_DIM
        same_head = (ri == ci).astype(BF16)
        alphas = _group_softmax(lse_ref[...])
        dtot = jnp.zeros((tm, GROUP_W), F32)
        for i in range(N_GROUPS):
            sl = slice(i * GROUP_W, (i + 1) * GROUP_W)
            dov = alphas[i] * dyb[:, sl]
            do_ref[:, sl] = dov.astype(BF16)
            prod = dov * o_ref[:, sl].astype(F32)
            hi = prod.astype(BF16)
            lo = (prod - hi.astype(F32)).astype(BF16)
            dtot = dtot + _dot(hi, same_head) + _dot(lo, same_head)
        for i in range(N_GROUPS):
            sl = slice(i * GROUP_W, (i + 1) * GROUP_W)
            dl_ref[:, sl] = alphas[i] * dtot

    return pl.pallas_call(
        body, name="mixer_bwd", grid=(t // tm,),
        in_specs=[_rows(tm, d), _rows(tm, 2 * d), _rows(tm, d), _rows(tm, d), _rows(tm, ATTN_W), _rows(tm, ATTN_W),
                  _resident((2, d)), _resident((d, d)), _resident((cw, d)), _resident((ATTN_W, d))],
        out_specs=[_rows(tm, 2 * d), _rows(tm, d), _rows(tm, d), _rows(tm, cw), _rows(tm, ATTN_W),
                   _rows(tm, ATTN_W), _acc_spec(d)],
        out_shape=[jax.ShapeDtypeStruct((t, 2 * d), BF16), jax.ShapeDtypeStruct((t, d), BF16),
                   jax.ShapeDtypeStruct((t, d), BF16), jax.ShapeDtypeStruct((t, cw), BF16),
                   jax.ShapeDtypeStruct((t, ATTN_W), BF16), jax.ShapeDtypeStruct((t, ATTN_W), F32),
                   jax.ShapeDtypeStruct((SUBLANES, d), F32)],
        compiler_params=_params("arbitrary"),
    )(dx1, gates, yap, ybp, o, lse, b_gate, w_o, w_pa, w_pb)


def _attn_bwd(qs, ks, vs, dos, lses, dls, carry=None):
    ng, t, _ = qs.shape
    nb = t // QBLK

    def body(q_ref, qn_ref, kc_ref, kp_ref, vc_ref, vp_ref, do_ref, don_ref, lse_ref, lsen_ref, dl_ref, dln_ref,
             dq_ref, dk_ref, dv_ref):
        g = pl.program_id(0)
        b = pl.program_id(1)
        per_stream = nb >> (2 * g)
        has_prev = lax.rem(b, per_stream) != 0
        has_next = lax.rem(b + 1, per_stream) != 0
        mask_c, mask_p = _band_masks(has_prev)
        _, mask_n = _band_masks(has_next)
        lane, heads = _head_masks()
        q, qn, kc, kp, vc, vp = q_ref[0], qn_ref[0], kc_ref[0], kp_ref[0], vc_ref[0], vp_ref[0]
        do, don = do_ref[0], don_ref[0]
        lse, lsen, dl, dln = lse_ref[0], lsen_ref[0], dl_ref[0], dln_ref[0]
        dq_acc = jnp.zeros((QBLK, GROUP_W), F32)
        dk_acc = jnp.zeros((QBLK, GROUP_W), F32)
        dv_acc = jnp.zeros((QBLK, GROUP_W), F32)

        def pair(qh, doh, k, v, mask, lse_h, dl_h):
            s = jnp.where(mask, _dot_nt(qh, k) * ATTN_SCALE, NEG_INF)
            p = jnp.exp(s - lse_h)
            ds = p * (_dot_nt(doh, v) - dl_h)
            return p.astype(BF16), ds.astype(BF16)

        for h, hm in enumerate(heads):
            first = lane == h * HEAD_DIM

            def col(v):
                return jnp.sum(jnp.where(first, v, 0.0), axis=1, keepdims=True)

            zero = jnp.zeros_like(q)
            qh, qnh = jnp.where(hm, q, zero), jnp.where(hm, qn, zero)
            doh, donh = jnp.where(hm, do, zero), jnp.where(hm, don, zero)
            lse_h, dl_h, lsen_h, dln_h = col(lse), col(dl), col(lsen), col(dln)
            p_c, ds_c = pair(qh, doh, kc, vc, mask_c, lse_h, dl_h)
            _, ds_p = pair(qh, doh, kp, vp, mask_p, lse_h, dl_h)
            p_n, ds_n = pair(qnh, donh, kc, vc, mask_n, lsen_h, dln_h)
            dq_h = _dot(ds_c, kc) + _dot(ds_p, kp)
            dk_h = _dot_tn(ds_c, q) + _dot_tn(ds_n, qn)
            dv_h = _dot_tn(p_c, do) + _dot_tn(p_n, don)
            dq_acc = jnp.where(hm, dq_h, dq_acc)
            dk_acc = jnp.where(hm, dk_h, dk_acc)
            dv_acc = jnp.where(hm, dv_h, dv_acc)
        dq_ref[0] = (dq_acc * ATTN_SCALE).astype(BF16)
        dk_ref[0] = (dk_acc * ATTN_SCALE).astype(BF16)
        dv_ref[0] = dv_acc.astype(BF16)

    blk = (1, QBLK, GROUP_W)
    cur = pl.BlockSpec(blk, lambda g, b: (g, b, 0))
    prev = pl.BlockSpec(blk, lambda g, b: (g, jnp.maximum(b - 1, 0), 0))
    nxt = pl.BlockSpec(blk, lambda g, b: (g, jnp.minimum(b + 1, nb - 1), 0))
    out = jax.ShapeDtypeStruct((ng, t, GROUP_W), BF16)
    return _call(
        body, name="attn_bwd", grid=(ng, nb),
        in_specs=[cur, nxt, cur, prev, cur, prev, cur, nxt, cur, nxt, cur, nxt],
        out_specs=[cur, cur, cur],
        out_shape=[out, out, out],
        args=(qs, qs, ks, ks, vs, vs, dos, dos, lses, lses, dls, dls), semantics=("parallel", "parallel"),
        carry=carry)


def _conv_mixer_bwd(abcv, dya, conv_w, conv_b):
    t = abcv.shape[0]
    cw = conv_w.shape[1]
    tm = min(256, t)
    last = t // tm - 1

    def body(a_ref, ap_ref, an_ref, dya_ref, dyan_ref, cw_ref, cb_ref, d_ref, acc_ref):
        m = pl.program_id(0)

        @pl.when(m == 0)
        def _():
            acc_ref[...] = jnp.zeros_like(acc_ref)

        ab = a_ref[:, 0:cw].astype(F32)
        ac = a_ref[:, cw:2 * cw].astype(F32)
        av = a_ref[:, 2 * cw:3 * cw].astype(F32)
        u = ac * av
        hu = ap_ref[:, cw:2 * cw].astype(F32) * ap_ref[:, 2 * cw:3 * cw].astype(F32)
        hu = jnp.where(m > 0, hu, 0.0)
        u1 = _shift_down(u, hu, 1)
        u2 = _shift_down(u, hu, 2)
        cv = cw_ref[0:1, :] * u2 + cw_ref[1:2, :] * u1 + cw_ref[2:3, :] * u + cb_ref[...]
        dya_v = dya_ref[...].astype(F32)
        dcv = dya_v * ab
        ndcv = jnp.where(m < last, dyan_ref[...].astype(F32) * an_ref[:, 0:cw].astype(F32), 0.0)
        du = (cw_ref[2:3, :] * dcv + cw_ref[1:2, :] * _shift_up(dcv, ndcv, 1)
              + cw_ref[0:1, :] * _shift_up(dcv, ndcv, 2))
        d_ref[:, 0:cw] = (dya_v * cv).astype(BF16)
        d_ref[:, cw:2 * cw] = (du * av).astype(BF16)
        d_ref[:, 2 * cw:3 * cw] = (du * ac).astype(BF16)
        acc_ref[...] += _stack_rows([_colsum(dcv * u2), _colsum(dcv * u1), _colsum(dcv * u), _colsum(dcv)], cw)

    return pl.pallas_call(
        body, name="conv_mixer_bwd", grid=(t // tm,),
        in_specs=[_rows(tm, 3 * cw), _prev_halo(tm, 3 * cw), _next_halo(tm, 3 * cw, t), _rows(tm, cw),
                  _next_halo(tm, cw, t), _resident((3, cw)), _resident((1, cw))],
        out_specs=[_rows(tm, 3 * cw), _acc_spec(cw)],
        out_shape=[jax.ShapeDtypeStruct((t, 3 * cw), BF16), jax.ShapeDtypeStruct((SUBLANES, cw), F32)],
        compiler_params=_params("arbitrary"),
    )(abcv, abcv, abcv, dya, dya, conv_w, conv_b)


def _in_proj_bwd(x, dx1, dabcv, dqkv, dgates, w_in, g1, carry=None):
    t, d = x.shape
    parts = (dabcv, dqkv, dgates)
    widths = [p.shape[1] for p in parts]
    n = sum(widths)
    tm = min(256, t)

    def body(x_ref, dx1_ref, p0_ref, p1_ref, p2_ref, w_ref, g_ref, dx_ref, acc_ref):
        m = pl.program_id(0)

        @pl.when(m == 0)
        def _():
            acc_ref[...] = jnp.zeros_like(acc_ref)

        dh = jnp.zeros((tm, d), F32)
        r0 = 0
        for p_ref, wd in zip((p0_ref, p1_ref, p2_ref), widths):
            dh = dh + _dot_nt(p_ref[...], w_ref[:, r0:r0 + wd])
            r0 += wd
        dx, dg = _rms_bwd(x_ref[...], g_ref[...], dh)
        dx_ref[...] = dx1_ref[...] + dx
        acc_ref[...] += _stack_rows([_colsum(dg)], d)

    return _call(
        body, name="in_proj_bwd", grid=(t // tm,),
        in_specs=[_rows(tm, d), _rows(tm, d)] + [_rows(tm, wd) for wd in widths] + [_resident((d, n)), _resident((1, d))],
        out_specs=[_rows(tm, d), _acc_spec(d)],
        out_shape=[jax.ShapeDtypeStruct((t, d), F32), jax.ShapeDtypeStruct((SUBLANES, d), F32)],
        args=(x, dx1, dabcv, dqkv, dgates, w_in, g1), semantics=("arbitrary",), carry=carry)


def _to_streams(a):
    t = a.shape[0]
    outs = []
    for g in range(N_GROUPS):
        dil = 4 ** g
        part = a[:, g * GROUP_W:(g + 1) * GROUP_W]
        outs.append(part.reshape(t // dil, dil, GROUP_W).transpose(1, 0, 2).reshape(t, GROUP_W))
    return jnp.stack(outs)


def _from_streams(s):
    t = s.shape[1]
    outs = []
    for g in range(N_GROUPS):
        dil = 4 ** g
        outs.append(s[g].reshape(dil, t // dil, GROUP_W).transpose(1, 0, 2).reshape(t, GROUP_W))
    return jnp.concatenate(outs, axis=1)


def _local_step(x, target, p, late):
    cw = p["conv_a_w"].shape[1]
    d = x.shape[1]
    (abcv, qkv, gates), (g_up,) = _rms_matmul(x, p["norm_mix_g"], p["w_in"], (3 * cw, 3 * ATTN_W, 2 * d), "in_proj",
                                              carry=_Exchange("gather", [late["w_up"]]))
    w_up = _full_from_gathered("w_up", g_up)
    qs = _to_streams(qkv[:, 0:ATTN_W])
    ks = _to_streams(qkv[:, ATTN_W:2 * ATTN_W])
    vs = _to_streams(qkv[:, 2 * ATTN_W:])
    (o_s, lse_s), (g_down,) = _attn_fwd(qs, ks, vs, carry=_Exchange("gather", [late["w_down"]]))
    w_down = _full_from_gathered("w_down", g_down)
    o = _from_streams(o_s)
    lse = _from_streams(lse_s)
    x1, ya, yb, yap, ybp, merged = _mixer_out(x, abcv, gates, o, lse, p["conv_a_w"], p["conv_a_b"], p["b_gate"],
                                              p["w_proj_a"], p["w_proj_b"], p["w_out"])
    (up_pre,) = _rms_matmul(x1, p["norm_ffn_g"], w_up, (w_up.shape[1],), "up_proj")
    act, dx2, acc_gf, loss = _ffn_loss(x1, up_pre, target, p["ffn_conv_w"], p["ffn_conv_b"], w_down,
                                       p["final_norm_g"])

    parts, got = {}, {}
    dup, acc_fc = _ffn_act_bwd(dx2, up_pre, p["ffn_conv_w"], p["ffn_conv_b"], w_down)
    parts["w_down"] = _by_destination("w_down", _tn_matmul(act, dx2, "dw_down"))
    (dpre, dx1, acc_g2), (got["w_down"],) = _ffn_up_bwd(dup, x1, dx2, p["ffn_conv_w"], w_up, p["norm_ffn_g"],
                                                        carry=_Exchange("scatter", [parts["w_down"]]))
    parts["w_up"] = _by_destination("w_up", _tn_matmul(x1, dpre, "dw_up", gain=p["norm_ffn_g"]))
    dgates, dyap, dybp, dya, do, dl, acc_bg = _mixer_bwd(dx1, gates, yap, ybp, o, lse, p["b_gate"], p["w_out"],
                                                         p["w_proj_a"], p["w_proj_b"])
    parts["w_out"] = _by_destination("w_out", _tn_matmul(merged, dx1, "dw_out"))
    parts["w_proj_a"] = _by_destination("w_proj_a", _tn_matmul(ya, dyap, "dw_proj_a"))
    parts["w_proj_b"] = _by_destination("w_proj_b", _tn_matmul(yb, dybp, "dw_proj_b"))
    riders = ("w_up", "w_out", "w_proj_a", "w_proj_b")
    (dq_s, dk_s, dv_s), received = _attn_bwd(qs, ks, vs, _to_streams(do), lse_s, _to_streams(dl),
                                             carry=_Exchange("scatter", [parts[n] for n in riders]))
    got.update(zip(riders, received))
    dqkv = jnp.concatenate([_from_streams(dq_s), _from_streams(dk_s), _from_streams(dv_s)], axis=1)
    dabcv, acc_ca = _conv_mixer_bwd(abcv, dya, p["conv_a_w"], p["conv_a_b"])
    g_w_in = jnp.concatenate([_tn_matmul(x, part, nm, gain=p["norm_mix_g"])
                              for part, nm in ((dabcv, "dw_in_a"), (dqkv, "dw_in_qkv"), (dgates, "dw_in_g"))], axis=1)
    parts["w_in"] = _by_destination("w_in", g_w_in)
    (dx, acc_g1), (got["w_in"],) = _in_proj_bwd(x, dx1, dabcv, dqkv, dgates, p["w_in"], p["norm_mix_g"],
                                                carry=_Exchange("scatter", [parts["w_in"]]))
    small = dict(norm_mix_g=acc_g1[0:1], b_gate=acc_bg[0:2], conv_a_w=acc_ca[0:3], conv_a_b=acc_ca[3:4],
                 norm_ffn_g=acc_g2[0:1], ffn_conv_w=acc_fc[0:3], ffn_conv_b=acc_fc[3:4], final_norm_g=acc_gf[0:1])
    return loss[0, 0], dx, parts, got, small


def _all_gather(shards):
    n = len(shards)

    def body(*refs):
        ins, outs = refs[:n], refs[n:2 * n]
        send_sems, recv_sems, local_sems = refs[2 * n:]
        x, y, c = _mesh_pos()
        me, sibling = (x, y, c), (x, y, 1 - c)
        chips = [(1 - x, y), (x, 1 - y), (1 - x, 1 - y)]

        def copy(i, k, block, to, src=None):
            rows = outs[i].at[_dev_index(*block)]
            return pltpu.make_async_remote_copy(
                src_ref=rows if src is None else src, dst_ref=rows, send_sem=send_sems.at[i, k],
                recv_sem=recv_sems.at[i, k], device_id=to, device_id_type=MESH)

        mine, first, passed = [], [], []
        for i in range(n):
            cp = pltpu.make_async_copy(ins[i], outs[i].at[_dev_index(*me)], local_sems.at[i])
            cp.start()
            mine.append(cp)
            first.append(copy(i, 0, me, sibling, src=ins[i]))
            first += [copy(i, 1 + j, me, (*chip, c), src=ins[i]) for j, chip in enumerate(chips)]
        for cp in first:
            cp.start()
        for i in range(n):
            for j, chip in enumerate(chips):
                copy(i, 1 + j, (*chip, c), me).wait_recv()
                fw = copy(i, 4 + j, (*chip, c), sibling)
                fw.start()
                passed.append(fw)
        for i in range(n):
            copy(i, 0, sibling, me).wait_recv()
            for j, chip in enumerate(chips):
                copy(i, 4 + j, (*chip, 1 - c), me).wait_recv()
        for cp in first + passed:
            cp.wait_send()
        for cp in mine:
            cp.wait()

    return pl.pallas_call(
        body, name="all_gather_weights",
        in_specs=[ANY] * n, out_specs=[ANY] * n,
        out_shape=[jax.ShapeDtypeStruct((N_DEV,) + s.shape, s.dtype) for s in shards],
        scratch_shapes=[pltpu.SemaphoreType.DMA((n, 7)), pltpu.SemaphoreType.DMA((n, 7)),
                        pltpu.SemaphoreType.DMA((n,))],
    )(*shards)


def _all_reduce_small(v):
    r = v.shape[0]

    def body(v_ref, o_ref, gath, send_sems, recv_sems):
        x, y, c = _mesh_pos()
        me = _dev_index(x, y, c)
        gath[me] = v_ref[...]
        flips = [(kx, ky, kc) for kx in (0, 1) for ky in (0, 1) for kc in (0, 1)][1:]
        copies = []
        for k, (kx, ky, kc) in enumerate(flips):
            px = 1 - x if kx else x
            py = 1 - y if ky else y
            pc = 1 - c if kc else c
            cp = pltpu.make_async_remote_copy(
                src_ref=v_ref, dst_ref=gath.at[me], send_sem=send_sems.at[k], recv_sem=recv_sems.at[k],
                device_id=(px, py, pc), device_id_type=MESH)
            cp.start()
            copies.append((cp, _dev_index(px, py, pc)))
        for k, (cp, peer) in enumerate(copies):
            pltpu.make_async_remote_copy(
                src_ref=v_ref, dst_ref=gath.at[peer], send_sem=send_sems.at[k], recv_sem=recv_sems.at[k],
                device_id=(x, y, c), device_id_type=MESH).wait_recv()
        for cp, _ in copies:
            cp.wait_send()
        total = gath[0]
        for j in range(1, N_DEV):
            total = total + gath[j]
        o_ref[...] = total

    return pl.pallas_call(
        body, name="all_reduce_small",
        in_specs=[pl.BlockSpec(memory_space=pltpu.VMEM)], out_specs=pl.BlockSpec(memory_space=pltpu.VMEM),
        out_shape=jax.ShapeDtypeStruct((r, LANES), F32),
        scratch_shapes=[pltpu.VMEM((N_DEV, r, LANES), F32), pltpu.SemaphoreType.DMA((7,)),
                        pltpu.SemaphoreType.DMA((7,))],
    )(v)


def _adamw_math(w, g, m, v):
    m2 = ADAM_B1 * m + (1.0 - ADAM_B1) * g
    v2 = ADAM_B2 * v + (1.0 - ADAM_B2) * (g * g)
    m_hat = m2 / (1.0 - ADAM_B1 ** ADAM_STEP)
    v_hat = v2 / (1.0 - ADAM_B2 ** ADAM_STEP)
    delta = -ADAM_LR * (m_hat / (jnp.sqrt(v_hat) + ADAM_EPS) + ADAM_WD * w)
    return delta, m2, v2


def _adamw_big(w, m, v, part, got, me):
    r, c = w.shape
    tr = r if r <= 512 else 256

    def body(me_ref, w_ref, m_ref, v_ref, own_ref, *rest):
        del me_ref
        got_refs, (g_out, d_out, m_out, v_out) = rest[:N_DEV - 1], rest[N_DEV - 1:]
        g = own_ref[...].astype(F32)
        for ref in got_refs:
            g = g + ref[...].astype(F32)
        delta, m2, v2 = _adamw_math(w_ref[...], g, m_ref[...], v_ref[...])
        g_out[...] = g
        d_out[...] = delta
        m_out[...] = m2
        v_out[...] = v2

    def peer_block(k):
        return pl.BlockSpec((None, tr, c), lambda i, me_ref: (jnp.bitwise_xor(me_ref[0], k), i, 0))

    plain = pl.BlockSpec((tr, c), lambda i, me_ref: (i, 0))
    out = jax.ShapeDtypeStruct((r, c), F32)
    return pl.pallas_call(
        body, name="adamw_big",
        grid_spec=pltpu.PrefetchScalarGridSpec(
            num_scalar_prefetch=1, grid=(r // tr,),
            in_specs=[plain, plain, plain] + [peer_block(k) for k in range(N_DEV)],
            out_specs=[plain] * 4),
        out_shape=[out] * 4,
        compiler_params=_params("parallel"),
    )(me, w, m, v, part, *([got] * (N_DEV - 1)))


def _adamw_small(w, g, m, v):
    def body(w_ref, g_ref, m_ref, v_ref, d_out, m_out, v_out):
        delta, m2, v2 = _adamw_math(w_ref[...], g_ref[...], m_ref[...], v_ref[...])
        d_out[...] = delta
        m_out[...] = m2
        v_out[...] = v2

    out = jax.ShapeDtypeStruct(w.shape, F32)
    return pl.pallas_call(body, name="adamw_small", out_shape=[out] * 3)(w, g, m, v)


BIG = ("w_in", "w_proj_a", "w_proj_b", "w_out", "w_up", "w_down")
EARLY = ("w_in", "w_proj_a", "w_proj_b", "w_out")
LATE = ("w_up", "w_down")
ROW_SHARDED = ("w_out", "w_down")
SMALL = ("norm_mix_g", "b_gate", "conv_a_w", "conv_a_b", "norm_ffn_g", "ffn_conv_w", "ffn_conv_b", "final_norm_g")
SMALL_SHARDED = ("b_gate", "conv_a_w", "ffn_conv_w")
WEIGHTS = ("norm_mix_g", "w_in", "b_gate", "conv_a_w", "conv_a_b", "w_proj_a", "w_proj_b", "w_out", "norm_ffn_g",
           "w_up", "ffn_conv_w", "ffn_conv_b", "w_down", "final_norm_g")


def _pack(vectors, rows):
    flat = jnp.concatenate([v.reshape(-1) for v in vectors])
    return jnp.pad(flat, (0, rows * LANES - flat.shape[0])).reshape(rows, LANES)


def _packed_rows(count):
    rows = -(-count // LANES)
    return -(-rows // SUBLANES) * SUBLANES


def _unpack(packed, shapes):
    flat = packed.reshape(-1)
    out, lo = [], 0
    for s in shapes:
        size = 1
        for dim in s:
            size *= dim
        out.append(flat[lo:lo + size].reshape(s))
        lo += size
    return out


def _full_from_gathered(name, gathered):
    _, r, c = gathered.shape
    if name in ROW_SHARDED:
        return gathered.reshape(N_DEV * r, c)
    return gathered.transpose(1, 0, 2).reshape(r, N_DEV * c)


def _by_destination(name, grad):
    rr, cc = grad.shape
    g = grad.astype(BF16)
    if name in ROW_SHARDED:
        return g.reshape(N_DEV, rr // N_DEV, cc)
    return g.reshape(rr, N_DEV, cc // N_DEV).transpose(1, 0, 2)


def kernel(x, norm_mix_g, w_in, b_gate, conv_a_w, conv_a_b, w_proj_a, w_proj_b, w_out, norm_ffn_g, w_up, ffn_conv_w, ffn_conv_b, w_down, final_norm_g, loss_target, m_norm_mix_g, m_w_in, m_b_gate, m_conv_a_w, m_conv_a_b, m_w_proj_a, m_w_proj_b, m_w_out, m_norm_ffn_g, m_w_up, m_ffn_conv_w, m_ffn_conv_b, m_w_down, m_final_norm_g, v_norm_mix_g, v_w_in, v_b_gate, v_conv_a_w, v_conv_a_b, v_w_proj_a, v_w_proj_b, v_w_out, v_norm_ffn_g, v_w_up, v_ffn_conv_w, v_ffn_conv_b, v_w_down, v_final_norm_g):
    given = dict(locals())
    shard = {n: given[n] for n in WEIGHTS}
    mom_m = {n: given["m_" + n] for n in WEIGHTS}
    mom_v = {n: given["v_" + n] for n in WEIGHTS}
    xi, yi, ci = _mesh_pos()
    me = _dev_index(xi, yi, ci)
    me1 = me.astype(jnp.int32).reshape(1)

    big2d = {n: shard[n].reshape(shard[n].shape[-2:]) for n in BIG}
    small_shapes = [shard[n].shape[1:] for n in SMALL_SHARDED]
    n_small = sum(s[0] * s[1] for s in small_shapes)
    packed_small = _pack([shard[n] for n in SMALL_SHARDED], _packed_rows(n_small))
    gathered = _all_gather([big2d[n].astype(BF16) for n in EARLY] + [packed_small])
    p = {n: _full_from_gathered(n, g) for n, g in zip(EARLY, gathered[:-1])}
    per_dev = [_unpack(gathered[-1][j], small_shapes) for j in range(N_DEV)]
    for i, n in enumerate(SMALL_SHARDED):
        p[n] = jnp.concatenate([per_dev[j][i] for j in range(N_DEV)], axis=1)
    p["norm_mix_g"], p["norm_ffn_g"] = shard["norm_mix_g"], shard["norm_ffn_g"]
    p["conv_a_b"], p["ffn_conv_b"] = shard["conv_a_b"], shard["ffn_conv_b"]
    p["final_norm_g"] = shard["final_norm_g"].reshape(1, -1)
    late = {n: big2d[n].astype(BF16) for n in LATE}

    loss_part, dx, parts, got, g_small = _local_step(x[0], loss_target[0], p, late)

    results = {}
    for n in BIG:
        lead = shard[n].shape
        outs = _adamw_big(big2d[n], mom_m[n].reshape(lead[-2:]), mom_v[n].reshape(lead[-2:]), parts[n], got[n], me1)
        results[n] = [o.reshape(lead) for o in outs]

    small_full_shapes = [g_small[n].shape for n in SMALL]
    n_vec = sum(s[0] * s[1] for s in small_full_shapes) + 1
    packed = _pack([g_small[n] for n in SMALL] + [loss_part.reshape(1)], _packed_rows(n_vec))
    reduced = _all_reduce_small(packed)
    *g_full, loss_vec = _unpack(reduced, small_full_shapes + [(1,)])
    loss = loss_vec[0]
    own_g = []
    for n, g in zip(SMALL, g_full):
        if n in SMALL_SHARDED:
            width = shard[n].shape[-1]
            g = lax.dynamic_slice_in_dim(g, me * width, width, axis=1)
        own_g.append(g.reshape(shard[n].shape))
    own_shapes = [shard[n].shape for n in SMALL]
    rows = _packed_rows(sum(g.size for g in own_g))
    small_out = _adamw_small(_pack([shard[n] for n in SMALL], rows), _pack(own_g, rows),
                             _pack([mom_m[n] for n in SMALL], rows), _pack([mom_v[n] for n in SMALL], rows))
    deltas, new_ms, new_vs = (_unpack(o, own_shapes) for o in small_out)
    for i, n in enumerate(SMALL):
        results[n] = [own_g[i], deltas[i], new_ms[i], new_vs[i]]

    grad_x = dx.reshape(x.shape)
    return (loss, grad_x, *[results[n][0] for n in WEIGHTS], *[results[n][1] for n in WEIGHTS],
            *[results[n][2] for n in WEIGHTS], *[results[n][3] for n in WEIGHTS])
```

```python
import functools

import jax
import jax.numpy as jnp
from jax import lax
from jax.experimental import pallas as pl
from jax.experimental.pallas import tpu as pltpu

F32 = jnp.float32
BF16 = jnp.bfloat16
MESH = pl.DeviceIdType.MESH

N_DEV = 8
RMS_EPS = 1e-6
NEG_INF = -1e30
N_GROUPS = 3
DILATIONS = (1, 4, 16)
HEADS_PER_GROUP = 4
HEAD_DIM = 64
GROUP_W = HEADS_PER_GROUP * HEAD_DIM
ATTN_W = N_GROUPS * GROUP_W
QBLK = 128
ATTN_SCALE = HEAD_DIM ** -0.5

ADAM_LR = 0.001
ADAM_B1 = 0.9
ADAM_B2 = 0.999
ADAM_EPS = 1e-08
ADAM_WD = 0.01
ADAM_STEP = 10

HALO = 16
LANES = 128
SUBLANES = 8
VMEM_LIMIT_BYTES = 56 * 1024 * 1024


def _params(*sem):
    return pltpu.CompilerParams(dimension_semantics=sem, vmem_limit_bytes=VMEM_LIMIT_BYTES)


def _pick_tile(n, cap):
    if n <= cap:
        return n
    best = None
    for t in range(LANES, cap + 1, LANES):
        if n % t == 0:
            best = t
    assert best is not None, (n, cap)
    return best


def _rows(tm, c, j=0):
    return pl.BlockSpec((tm, c), lambda m: (m, j))


def _prev_halo(tm, c):
    return pl.BlockSpec((HALO, c), lambda m: (jnp.maximum(m * (tm // HALO) - 1, 0), 0))


def _next_halo(tm, c, t_total):
    last = t_total // HALO - 1
    return pl.BlockSpec((HALO, c), lambda m: (jnp.minimum((m + 1) * (tm // HALO), last), 0))


def _resident(shape):
    nd = len(shape)
    return pl.BlockSpec(shape, lambda *_: (0,) * nd, pipeline_mode=pl.Buffered(1))


def _acc_spec(c):
    return pl.BlockSpec((SUBLANES, c), lambda *_: (0, 0))


def _shift_down(u, halo, k):
    ext = jnp.concatenate([halo, u], axis=0)
    return pltpu.roll(ext, k, 0)[HALO:, :]


def _shift_up(u, halo, k):
    ext = jnp.concatenate([u, halo], axis=0)
    return pltpu.roll(ext, ext.shape[0] - k, 0)[: u.shape[0], :]


def _stack_rows(rows, c):
    idx = lax.broadcasted_iota(jnp.int32, (SUBLANES, c), 0)
    out = jnp.zeros((SUBLANES, c), F32)
    for i, r in enumerate(rows):
        out = out + jnp.where(idx == i, r, 0.0)
    return out


def _colsum(v):
    return jnp.sum(v, axis=0, keepdims=True)


def _sigmoid(v):
    return 1.0 / (1.0 + jnp.exp(-v))


def _rms_fwd(xv, g):
    r = lax.rsqrt(jnp.mean(xv * xv, axis=-1, keepdims=True) + RMS_EPS)
    return xv * r * g, r


def _rms_bwd(xv, g, dy):
    r = lax.rsqrt(jnp.mean(xv * xv, axis=-1, keepdims=True) + RMS_EPS)
    xn = xv * r
    dxn = dy * g
    dx = r * (dxn - xn * jnp.mean(dxn * xn, axis=-1, keepdims=True))
    return dx, dy * xn


def _dot(a, b):
    return jnp.dot(a, b, preferred_element_type=F32)


def _dot_nt(a, b):
    return lax.dot_general(a, b, (((1,), (1,)), ((), ())), preferred_element_type=F32)


def _dot_tn(a, b):
    return lax.dot_general(a, b, (((0,), (0,)), ((), ())), preferred_element_type=F32)


def _perm(dil, n, inverse=False):
    i = lax.broadcasted_iota(jnp.int32, (n, n), 0)
    j = lax.broadcasted_iota(jnp.int32, (n, n), 1)
    if inverse:
        i, j = j, i
    per = n // dil
    return (j == (i % per) * dil + i // per).astype(BF16)


def _permute_rows(pm, v):
    if v.dtype == BF16:
        return _dot(pm, v).astype(BF16)
    h1 = v.astype(BF16)
    r1 = v - h1.astype(F32)
    h2 = r1.astype(BF16)
    h3 = (r1 - h2.astype(F32)).astype(BF16)
    return _dot(pm, h1) + _dot(pm, h2) + _dot(pm, h3)


def _stream_view(a, dil):
    t, c = a.shape
    return a.reshape(dil, t // dil, c)


def _stream_spec(dil, tm, c):
    return pl.BlockSpec((dil, tm // dil, c), lambda m: (0, m, 0))


def _load_streams(ref, dil, tm):
    v = ref[...].reshape(tm, ref.shape[-1])
    return v if dil == 1 else _permute_rows(_perm(dil, tm, inverse=True), v)


def _store_streams(ref, dil, tm, v):
    if dil > 1:
        v = _permute_rows(_perm(dil, tm), v)
    ref[...] = v.reshape(ref.shape).astype(ref.dtype)


ANY = pl.BlockSpec(memory_space=pl.ANY)


def _mesh_pos():
    return lax.axis_index("x"), lax.axis_index("y"), lax.axis_index("c")


def _dev_index(px, py, pc):
    return 4 * px + 2 * py + pc


class _Exchange:
    def __init__(self, mode, arrays):
        self.mode, self.arrays = mode, list(arrays)
        n = len(self.arrays)
        if mode == "gather":
            self.out_shape = [jax.ShapeDtypeStruct((N_DEV,) + a.shape, a.dtype) for a in self.arrays]
        else:
            self.out_shape = [jax.ShapeDtypeStruct(a.shape, a.dtype) for a in self.arrays]
        self.scratch = [pltpu.SemaphoreType.DMA((n, N_DEV - 1)), pltpu.SemaphoreType.DMA((n, N_DEV - 1)),
                        pltpu.SemaphoreType.DMA((n,))]

    def _peers(self):
        x, y, c = _mesh_pos()
        flips = [(kx, ky, kc) for kx in (0, 1) for ky in (0, 1) for kc in (0, 1)][1:]
        peers = [(1 - x if kx else x, 1 - y if ky else y, 1 - c if kc else c) for kx, ky, kc in flips]
        return _dev_index(x, y, c), peers

    def _copy(self, ins, outs, sems, i, k, peer, me, sending):
        src = ins[i] if self.mode == "gather" else ins[i].at[_dev_index(*peer)]
        dst = outs[i].at[me if sending else _dev_index(*peer)]
        return pltpu.make_async_remote_copy(src_ref=src, dst_ref=dst, send_sem=sems[0].at[i, k],
                                            recv_sem=sems[1].at[i, k], device_id=peer, device_id_type=MESH)

    def _own(self, ins, outs, sems, i, me):
        return pltpu.make_async_copy(ins[i], outs[i].at[me], sems[2].at[i])

    def start(self, ins, outs, sems):
        me, peers = self._peers()
        for i in range(len(ins)):
            if self.mode == "gather":
                self._own(ins, outs, sems, i, me).start()
            for k, peer in enumerate(peers):
                self._copy(ins, outs, sems, i, k, peer, me, True).start()

    def wait(self, ins, outs, sems):
        me, peers = self._peers()
        for i in range(len(ins)):
            for k, peer in enumerate(peers):
                self._copy(ins, outs, sems, i, k, peer, me, False).wait_recv()
            for k, peer in enumerate(peers):
                self._copy(ins, outs, sems, i, k, peer, me, True).wait_send()
            if self.mode == "gather":
                self._own(ins, outs, sems, i, me).wait()


def _call(body, *, name, grid, in_specs, out_specs, out_shape, args, semantics, carry=None):
    if carry is None:
        return pl.pallas_call(body, name=name, grid=grid, in_specs=in_specs, out_specs=out_specs,
                              out_shape=out_shape, compiler_params=_params(*semantics))(*args)
    n_in, n_out, n_x = len(in_specs), len(out_specs), len(carry.arrays)

    def carried(*refs):
        ins, x_ins = refs[:n_in], refs[n_in:n_in + n_x]
        outs = refs[n_in + n_x:n_in + n_x + n_out]
        x_outs = refs[n_in + n_x + n_out:n_in + 2 * n_x + n_out]
        sems = refs[n_in + 2 * n_x + n_out:]
        first = functools.reduce(jnp.logical_and, [pl.program_id(a) == 0 for a in range(len(grid))])
        last = functools.reduce(jnp.logical_and, [pl.program_id(a) == grid[a] - 1 for a in range(len(grid))])

        @pl.when(first)
        def _():
            carry.start(x_ins, x_outs, sems)

        body(*ins, *outs)

        @pl.when(last)
        def _():
            carry.wait(x_ins, x_outs, sems)

    res = pl.pallas_call(
        carried, name=name, grid=grid, in_specs=list(in_specs) + [ANY] * n_x,
        out_specs=list(out_specs) + [ANY] * n_x, out_shape=list(out_shape) + carry.out_shape,
        scratch_shapes=carry.scratch, compiler_params=_params(*["arbitrary"] * len(grid)),
    )(*args, *carry.arrays)
    return list(res[:n_out]), list(res[n_out:])


def _rms_matmul(x, g, w, widths, name, carry=None):
    t, d = x.shape
    n = w.shape[1]
    assert sum(widths) == n
    tm = min(256, t)
    spans, lo = [], 0
    for wd in widths:
        spans.append((lo, wd))
        lo += wd

    def body(x_ref, g_ref, w_ref, *o_refs):
        h, _ = _rms_fwd(x_ref[...], g_ref[...])
        h = h.astype(BF16)
        for o_ref, (c0, wd) in zip(o_refs, spans):
            o_ref[...] = _dot(h, w_ref[:, c0:c0 + wd]).astype(BF16)

    return _call(
        body, name=name, grid=(t // tm,),
        in_specs=[_rows(tm, d), _resident((1, d)), _resident((d, n))],
        out_specs=[_rows(tm, wd) for wd in widths],
        out_shape=[jax.ShapeDtypeStruct((t, wd), BF16) for wd in widths],
        args=(x, g, w), semantics=("parallel",), carry=carry)


def _in_proj(x, g, w, cw, carry=None):
    t, d = x.shape
    n = w.shape[1]
    tm = min(256, t)
    qkv0 = 3 * cw

    def body(x_ref, g_ref, w_ref, abcv_ref, gates_ref, *s_refs):
        h, _ = _rms_fwd(x_ref[...], g_ref[...])
        h = h.astype(BF16)
        abcv_ref[...] = _dot(h, w_ref[:, 0:qkv0]).astype(BF16)
        gates_ref[...] = _dot(h, w_ref[:, qkv0 + 3 * ATTN_W:n]).astype(BF16)
        for gi, s_ref in enumerate(s_refs):
            cols = [_dot(h, w_ref[:, qkv0 + j * ATTN_W + gi * GROUP_W:qkv0 + j * ATTN_W + (gi + 1) * GROUP_W])
                    for j in range(3)]
            _store_streams(s_ref, DILATIONS[gi], tm, jnp.concatenate(cols, axis=1).astype(BF16))

    return _call(
        body, name="in_proj", grid=(t // tm,),
        in_specs=[_rows(tm, d), _resident((1, d)), _resident((d, n))],
        out_specs=[_rows(tm, qkv0), _rows(tm, 2 * d)] + [_stream_spec(dil, tm, 3 * GROUP_W) for dil in DILATIONS],
        out_shape=[jax.ShapeDtypeStruct((t, qkv0), BF16), jax.ShapeDtypeStruct((t, 2 * d), BF16)]
        + [jax.ShapeDtypeStruct((dil, t // dil, 3 * GROUP_W), BF16) for dil in DILATIONS],
        args=(x, g, w), semantics=("parallel",), carry=carry)


def _head_masks():
    lane = lax.broadcasted_iota(jnp.int32, (1, GROUP_W), 1)
    return lane, [(lane // HEAD_DIM) == h for h in range(HEADS_PER_GROUP)]


def _qkv_block(col, shift=0, nb=None):
    if shift == 0:
        return pl.BlockSpec((QBLK, GROUP_W), lambda b: (b, col))
    return pl.BlockSpec((QBLK, GROUP_W), lambda b: (jnp.clip(b + shift, 0, nb - 1), col))


def _band_masks(has_other):
    row = lax.broadcasted_iota(jnp.int32, (QBLK, QBLK), 0)
    col = lax.broadcasted_iota(jnp.int32, (QBLK, QBLK), 1)
    return col <= row, (col >= row) & has_other


def _attn_fwd(s, dil, carry=None):
    t = s.shape[0] * s.shape[1]
    nb = t // QBLK
    per_stream = nb // dil

    def body(q_ref, kc_ref, kp_ref, vc_ref, vp_ref, o_ref, lse_ref):
        b = pl.program_id(0)
        has_prev = lax.rem(b, per_stream) != 0
        mask_c, mask_p = _band_masks(has_prev)
        _, heads = _head_masks()
        q, kc, kp, vc, vp = q_ref[...], kc_ref[...], kp_ref[...], vc_ref[...], vp_ref[...]
        o_acc = jnp.zeros((QBLK, GROUP_W), F32)
        lse_acc = jnp.zeros((QBLK, GROUP_W), F32)
        for hm in heads:
            qh = jnp.where(hm, q, jnp.zeros_like(q))
            sc = jnp.where(mask_c, _dot_nt(qh, kc) * ATTN_SCALE, NEG_INF)
            sp = jnp.where(mask_p, _dot_nt(qh, kp) * ATTN_SCALE, NEG_INF)
            mx = jnp.maximum(jnp.max(sc, axis=1, keepdims=True), jnp.max(sp, axis=1, keepdims=True))
            pc = jnp.exp(sc - mx)
            pp = jnp.exp(sp - mx)
            den = jnp.sum(pc, axis=1, keepdims=True) + jnp.sum(pp, axis=1, keepdims=True)
            oh = _dot(pc.astype(BF16), vc) + _dot(pp.astype(BF16), vp)
            o_acc = jnp.where(hm, oh / den, o_acc)
            lse_acc = jnp.where(hm, mx + jnp.log(den), lse_acc)
        o_ref[...] = o_acc.astype(BF16)
        lse_ref[...] = lse_acc

    sv = s.reshape(t, 3 * GROUP_W)
    return _call(
        body, name=f"attn_fwd_d{dil}", grid=(nb,),
        in_specs=[_qkv_block(0), _qkv_block(1), _qkv_block(1, -1, nb), _qkv_block(2), _qkv_block(2, -1, nb)],
        out_specs=[_qkv_block(0), _qkv_block(0)],
        out_shape=[jax.ShapeDtypeStruct((t, GROUP_W), BF16), jax.ShapeDtypeStruct((t, GROUP_W), F32)],
        args=(sv, sv, sv, sv, sv), semantics=("parallel",), carry=carry)


def _group_softmax(parts):
    mx = jnp.maximum(jnp.maximum(parts[0], parts[1]), parts[2])
    es = [jnp.exp(p - mx) for p in parts]
    den = es[0] + es[1] + es[2]
    return [e / den for e in es]


def _mixer_out(x, abcv, gates, os, lses, conv_w, conv_b, b_gate, w_pa, w_pb, w_o):
    t, d = x.shape
    cw = conv_w.shape[1]
    tm = min(256, t)

    def body(x_ref, abcv_ref, halo_ref, gates_ref, o0_ref, o1_ref, o2_ref, l0_ref, l1_ref, l2_ref, cw_ref, cb_ref,
             bg_ref, wpa_ref, wpb_ref, wo_ref, x1_ref, ya_ref, yb_ref, yap_ref, ybp_ref, mg_ref):
        m = pl.program_id(0)
        ab = abcv_ref[:, 0:cw].astype(F32)
        u = abcv_ref[:, cw:2 * cw].astype(F32) * abcv_ref[:, 2 * cw:3 * cw].astype(F32)
        hu = halo_ref[:, cw:2 * cw].astype(F32) * halo_ref[:, 2 * cw:3 * cw].astype(F32)
        hu = jnp.where(m > 0, hu, 0.0)
        cv = (cw_ref[0:1, :] * _shift_down(u, hu, 2) + cw_ref[1:2, :] * _shift_down(u, hu, 1)
              + cw_ref[2:3, :] * u + cb_ref[...])
        ya = (ab * cv).astype(BF16)
        ya_ref[...] = ya
        alphas = _group_softmax([_load_streams(r, dil, tm) for r, dil in zip((l0_ref, l1_ref, l2_ref), DILATIONS)])
        for i, (o_ref, dil) in enumerate(zip((o0_ref, o1_ref, o2_ref), DILATIONS)):
            sl = slice(i * GROUP_W, (i + 1) * GROUP_W)
            yb_ref[:, sl] = (alphas[i] * _load_streams(o_ref, dil, tm).astype(F32)).astype(BF16)
        yap = _dot(ya, wpa_ref[...])
        ybp = _dot(yb_ref[...], wpb_ref[...])
        yap_ref[...] = yap.astype(BF16)
        ybp_ref[...] = ybp.astype(BF16)
        sa = _sigmoid(gates_ref[:, 0:d].astype(F32) + bg_ref[0:1, :])
        sb = _sigmoid(gates_ref[:, d:2 * d].astype(F32) + bg_ref[1:2, :])
        merged = (sa * yap + sb * ybp).astype(BF16)
        mg_ref[...] = merged
        x1_ref[...] = x_ref[...] + _dot(merged, wo_ref[...])

    return pl.pallas_call(
        body, name="mixer_out", grid=(t // tm,),
        in_specs=[_rows(tm, d), _rows(tm, 3 * cw), _prev_halo(tm, 3 * cw), _rows(tm, 2 * d)]
        + [_stream_spec(dil, tm, GROUP_W) for dil in DILATIONS] * 2
        + [_resident((3, cw)), _resident((1, cw)), _resident((2, d)),
           _resident((cw, d)), _resident((ATTN_W, d)), _resident((d, d))],
        out_specs=[_rows(tm, d), _rows(tm, cw), _rows(tm, ATTN_W), _rows(tm, d), _rows(tm, d), _rows(tm, d)],
        out_shape=[jax.ShapeDtypeStruct((t, d), F32), jax.ShapeDtypeStruct((t, cw), BF16),
                   jax.ShapeDtypeStruct((t, ATTN_W), BF16), jax.ShapeDtypeStruct((t, d), BF16),
                   jax.ShapeDtypeStruct((t, d), BF16), jax.ShapeDtypeStruct((t, d), BF16)],
        compiler_params=_params("parallel"),
    )(x, abcv, abcv, gates, *[_stream_view(a, dil) for a, dil in zip(os, DILATIONS)],
      *[_stream_view(a, dil) for a, dil in zip(lses, DILATIONS)], conv_w, conv_b, b_gate, w_pa, w_pb, w_o)


def _ffn_conv(p_ref, halo_ref, w_ref, b_ref, m, c0, wd):
    p = p_ref[:, c0:c0 + wd].astype(F32)
    hp = jnp.where(m > 0, halo_ref[:, c0:c0 + wd].astype(F32), 0.0)
    p1 = _shift_down(p, hp, 1)
    p2 = _shift_down(p, hp, 2)
    up = (w_ref[0:1, c0:c0 + wd] * p2 + w_ref[1:2, c0:c0 + wd] * p1 + w_ref[2:3, c0:c0 + wd] * p
          + b_ref[:, c0:c0 + wd])
    return up, p, p1, p2


def _ffn_loss(x1, up_pre, target, conv_w, conv_b, w_d, g_f):
    t, d = x1.shape
    dff = w_d.shape[0]
    tm = min(256, t)
    ck = _pick_tile(dff, 1408)

    def body(x1_ref, up_ref, halo_ref, tg_ref, cw_ref, cb_ref, wd_ref, gf_ref, act_ref, dx2_ref, acc_ref, loss_ref):
        m = pl.program_id(0)

        @pl.when(m == 0)
        def _():
            acc_ref[...] = jnp.zeros_like(acc_ref)
            loss_ref[...] = jnp.zeros_like(loss_ref)

        x2 = x1_ref[...]
        for c0 in range(0, dff, ck):
            gate = _ffn_conv(up_ref, halo_ref, cw_ref, cb_ref, m, c0, ck)[0]
            val = _ffn_conv(up_ref, halo_ref, cw_ref, cb_ref, m, dff + c0, ck)[0]
            act = (gate * _sigmoid(gate) * val).astype(BF16)
            act_ref[:, c0:c0 + ck] = act
            x2 = x2 + _dot(act, wd_ref[c0:c0 + ck, :])
        y, _ = _rms_fwd(x2, gf_ref[...])
        diff = y - tg_ref[...]
        loss_ref[...] += 0.5 * jnp.sum(jnp.mean(diff * diff, axis=-1, keepdims=True))
        dx2, dg = _rms_bwd(x2, gf_ref[...], diff * (1.0 / d))
        dx2_ref[...] = dx2
        acc_ref[...] += _stack_rows([_colsum(dg)], d)

    return pl.pallas_call(
        body, name="ffn_loss", grid=(t // tm,),
        in_specs=[_rows(tm, d), _rows(tm, 2 * dff), _prev_halo(tm, 2 * dff), _rows(tm, d),
                  _resident((3, 2 * dff)), _resident((1, 2 * dff)), _resident((dff, d)), _resident((1, d))],
        out_specs=[_rows(tm, dff), _rows(tm, d), _acc_spec(d), _acc_spec(LANES)],
        out_shape=[jax.ShapeDtypeStruct((t, dff), BF16), jax.ShapeDtypeStruct((t, d), F32),
                   jax.ShapeDtypeStruct((SUBLANES, d), F32), jax.ShapeDtypeStruct((SUBLANES, LANES), F32)],
        compiler_params=_params("arbitrary"),
    )(x1, up_pre, up_pre, target, conv_w, conv_b, w_d, g_f)


def _ffn_act_bwd(dx2, up_pre, conv_w, conv_b, w_d):
    t, d = dx2.shape
    dff = w_d.shape[0]
    tm = min(256, t)
    ck = _pick_tile(dff, 1408)

    def body(dx2_ref, up_ref, halo_ref, cw_ref, cb_ref, wd_ref, dup_ref, acc_ref):
        m = pl.program_id(0)

        @pl.when(m == 0)
        def _():
            acc_ref[...] = jnp.zeros_like(acc_ref)

        dx2v = dx2_ref[...].astype(BF16)
        for c0 in range(0, dff, ck):
            dact = _dot_nt(dx2v, wd_ref[c0:c0 + ck, :])
            gate, gp, gp1, gp2 = _ffn_conv(up_ref, halo_ref, cw_ref, cb_ref, m, c0, ck)
            val, vp, vp1, vp2 = _ffn_conv(up_ref, halo_ref, cw_ref, cb_ref, m, dff + c0, ck)
            sg = _sigmoid(gate)
            dval = dact * gate * sg
            dgate = dact * val * sg * (1.0 + gate * (1.0 - sg))
            dup_ref[:, c0:c0 + ck] = dgate.astype(BF16)
            dup_ref[:, dff + c0:dff + c0 + ck] = dval.astype(BF16)
            for base, dv, p0, p1, p2 in ((c0, dgate, gp, gp1, gp2), (dff + c0, dval, vp, vp1, vp2)):
                acc_ref[:, base:base + ck] += _stack_rows(
                    [_colsum(dv * p2), _colsum(dv * p1), _colsum(dv * p0), _colsum(dv)], ck)

    return pl.pallas_call(
        body, name="ffn_act_bwd", grid=(t // tm,),
        in_specs=[_rows(tm, d), _rows(tm, 2 * dff), _prev_halo(tm, 2 * dff), _resident((3, 2 * dff)),
                  _resident((1, 2 * dff)), _resident((dff, d))],
        out_specs=[_rows(tm, 2 * dff), _acc_spec(2 * dff)],
        out_shape=[jax.ShapeDtypeStruct((t, 2 * dff), BF16), jax.ShapeDtypeStruct((SUBLANES, 2 * dff), F32)],
        compiler_params=_params("arbitrary"),
    )(dx2, up_pre, up_pre, conv_w, conv_b, w_d)


def _ffn_up_bwd(dup, x1, dx2, conv_w, w_u, g2, carry=None):
    t, d = x1.shape
    n = dup.shape[1]
    tm = min(256, t)
    ck = _pick_tile(n, 1408)
    last = t // tm - 1

    def body(dup_ref, nxt_ref, x1_ref, dx2_ref, cw_ref, wu_ref, g2_ref, dpre_ref, dx1_ref, acc_ref):
        m = pl.program_id(0)

        @pl.when(m == 0)
        def _():
            acc_ref[...] = jnp.zeros_like(acc_ref)

        dh = jnp.zeros((tm, d), F32)
        for c0 in range(0, n, ck):
            du = dup_ref[:, c0:c0 + ck].astype(F32)
            hn = jnp.where(m < last, nxt_ref[:, c0:c0 + ck].astype(F32), 0.0)
            dpre = (cw_ref[2:3, c0:c0 + ck] * du + cw_ref[1:2, c0:c0 + ck] * _shift_up(du, hn, 1)
                    + cw_ref[0:1, c0:c0 + ck] * _shift_up(du, hn, 2)).astype(BF16)
            dpre_ref[:, c0:c0 + ck] = dpre
            dh = dh + _dot_nt(dpre, wu_ref[:, c0:c0 + ck])
        dx, dg = _rms_bwd(x1_ref[...], g2_ref[...], dh)
        dx1_ref[...] = dx2_ref[...] + dx
        acc_ref[...] += _stack_rows([_colsum(dg)], d)

    return _call(
        body, name="ffn_up_bwd", grid=(t // tm,),
        in_specs=[_rows(tm, n), _next_halo(tm, n, t), _rows(tm, d), _rows(tm, d), _resident((3, n)),
                  _resident((d, n)), _resident((1, d))],
        out_specs=[_rows(tm, n), _rows(tm, d), _acc_spec(d)],
        out_shape=[jax.ShapeDtypeStruct((t, n), BF16), jax.ShapeDtypeStruct((t, d), F32),
                   jax.ShapeDtypeStruct((SUBLANES, d), F32)],
        args=(dup, dup, x1, dx2, conv_w, w_u, g2), semantics=("arbitrary",), carry=carry)


def _tn_matmul(a, b, name, gain=None):
    t, mdim = a.shape
    n = b.shape[1]
    tk = min(512, t)
    tmm = mdim if gain is not None else _pick_tile(mdim, 1408)
    tn = _pick_tile(n, 1536)

    def body(*refs):
        if gain is not None:
            a_ref, b_ref, g_ref, o_ref, acc_ref = refs
        else:
            a_ref, b_ref, o_ref, acc_ref = refs
        k = pl.program_id(2)

        @pl.when(k == 0)
        def _():
            acc_ref[...] = jnp.zeros_like(acc_ref)

        av = a_ref[...]
        if gain is not None:
            av = _rms_fwd(av, g_ref[...])[0]
        acc_ref[...] += _dot_tn(av.astype(BF16), b_ref[...].astype(BF16))

        @pl.when(k == t // tk - 1)
        def _():
            o_ref[...] = acc_ref[...].astype(BF16)

    in_specs = [pl.BlockSpec((tk, tmm), lambda i, j, k: (k, i)), pl.BlockSpec((tk, tn), lambda i, j, k: (k, j))]
    args = [a, b]
    if gain is not None:
        in_specs.append(pl.BlockSpec((1, mdim), lambda i, j, k: (0, 0)))
        args.append(gain)
    return pl.pallas_call(
        body, name=name, grid=(mdim // tmm, n // tn, t // tk),
        in_specs=in_specs,
        out_specs=pl.BlockSpec((tmm, tn), lambda i, j, k: (i, j)),
        out_shape=jax.ShapeDtypeStruct((mdim, n), BF16),
        scratch_shapes=[pltpu.VMEM((tmm, tn), F32)],
        compiler_params=_params("parallel", "parallel", "arbitrary"),
    )(*args)


def _mixer_bwd(dx1, gates, yap, ybp, os, lses, b_gate, w_o, w_pa, w_pb):
    t, d = dx1.shape
    cw = w_pa.shape[0]
    tm = min(256, t)

    def body(dx1_ref, gates_ref, yap_ref, ybp_ref, o0_ref, o1_ref, o2_ref, l0_ref, l1_ref, l2_ref, bg_ref, wo_ref,
             wpa_ref, wpb_ref, dgates_ref, dyap_ref, dybp_ref, dya_ref, do0_ref, do1_ref, do2_ref, dl0_ref, dl1_ref,
             dl2_ref, acc_ref):
        m = pl.program_id(0)

        @pl.when(m == 0)
        def _():
            acc_ref[...] = jnp.zeros_like(acc_ref)

        dmg = _dot_nt(dx1_ref[...].astype(BF16), wo_ref[...])
        sa = _sigmoid(gates_ref[:, 0:d].astype(F32) + bg_ref[0:1, :])
        sb = _sigmoid(gates_ref[:, d:2 * d].astype(F32) + bg_ref[1:2, :])
        dyap = (dmg * sa).astype(BF16)
        dybp = (dmg * sb).astype(BF16)
        dga = dmg * yap_ref[...].astype(F32) * sa * (1.0 - sa)
        dgb = dmg * ybp_ref[...].astype(F32) * sb * (1.0 - sb)
        dyap_ref[...] = dyap
        dybp_ref[...] = dybp
        dgates_ref[:, 0:d] = dga.astype(BF16)
        dgates_ref[:, d:2 * d] = dgb.astype(BF16)
        acc_ref[...] += _stack_rows([_colsum(dga), _colsum(dgb)], d)
        dya_ref[...] = _dot_nt(dyap, wpa_ref[...]).astype(BF16)
        dyb = _dot_nt(dybp, wpb_ref[...])

        ri = lax.broadcasted_iota(jnp.int32, (GROUP_W, GROUP_W), 0) // HEAD_DIM
        ci = lax.broadcasted_iota(jnp.int32, (GROUP_W, GROUP_W), 1) // HEAD_DIM
        same_head = (ri == ci).astype(BF16)
        alphas = _group_softmax([_load_streams(r, dil, tm) for r, dil in zip((l0_ref, l1_ref, l2_ref), DILATIONS)])
        dtot = jnp.zeros((tm, GROUP_W), F32)
        for i, (o_ref, do_ref, dil) in enumerate(zip((o0_ref, o1_ref, o2_ref), (do0_ref, do1_ref, do2_ref), DILATIONS)):
            dov = alphas[i] * dyb[:, i * GROUP_W:(i + 1) * GROUP_W]
            _store_streams(do_ref, dil, tm, dov.astype(BF16))
            prod = dov * _load_streams(o_ref, dil, tm).astype(F32)
            hi = prod.astype(BF16)
            lo = (prod - hi.astype(F32)).astype(BF16)
            dtot = dtot + _dot(hi, same_head) + _dot(lo, same_head)
        for alpha, dl_ref, dil in zip(alphas, (dl0_ref, dl1_ref, dl2_ref), DILATIONS):
            _store_streams(dl_ref, dil, tm, alpha * dtot)

    streams = [_stream_spec(dil, tm, GROUP_W) for dil in DILATIONS]
    res = pl.pallas_call(
        body, name="mixer_bwd", grid=(t // tm,),
        in_specs=[_rows(tm, d), _rows(tm, 2 * d), _rows(tm, d), _rows(tm, d)] + streams * 2
        + [_resident((2, d)), _resident((d, d)), _resident((cw, d)), _resident((ATTN_W, d))],
        out_specs=[_rows(tm, 2 * d), _rows(tm, d), _rows(tm, d), _rows(tm, cw)] + streams * 2 + [_acc_spec(d)],
        out_shape=[jax.ShapeDtypeStruct((t, 2 * d), BF16), jax.ShapeDtypeStruct((t, d), BF16),
                   jax.ShapeDtypeStruct((t, d), BF16), jax.ShapeDtypeStruct((t, cw), BF16)]
        + [jax.ShapeDtypeStruct((dil, t // dil, GROUP_W), BF16) for dil in DILATIONS]
        + [jax.ShapeDtypeStruct((dil, t // dil, GROUP_W), F32) for dil in DILATIONS]
        + [jax.ShapeDtypeStruct((SUBLANES, d), F32)],
        compiler_params=_params("arbitrary"),
    )(dx1, gates, yap, ybp, *[_stream_view(a, dil) for a, dil in zip(os, DILATIONS)],
      *[_stream_view(a, dil) for a, dil in zip(lses, DILATIONS)], b_gate, w_o, w_pa, w_pb)
    dgates, dyap, dybp, dya = res[:4]
    dos = [a.reshape(t, GROUP_W) for a in res[4:7]]
    dls = [a.reshape(t, GROUP_W) for a in res[7:10]]
    return dgates, dyap, dybp, dya, dos, dls, res[10]


def _attn_bwd(s, do, lse, dl, dil, carry=None):
    t = s.shape[0] * s.shape[1]
    nb = t // QBLK
    per_stream = nb // dil

    def body(q_ref, qn_ref, kc_ref, kp_ref, vc_ref, vp_ref, do_ref, don_ref, lse_ref, lsen_ref, dl_ref, dln_ref,
             ds_ref):
        b = pl.program_id(0)
        has_prev = lax.rem(b, per_stream) != 0
        has_next = lax.rem(b + 1, per_stream) != 0
        mask_c, mask_p = _band_masks(has_prev)
        _, mask_n = _band_masks(has_next)
        lane, heads = _head_masks()
        q, qn, kc, kp, vc, vp = q_ref[...], qn_ref[...], kc_ref[...], kp_ref[...], vc_ref[...], vp_ref[...]
        do, don = do_ref[...], don_ref[...]
        lse, lsen, dl, dln = lse_ref[...], lsen_ref[...], dl_ref[...], dln_ref[...]
        dq_acc = jnp.zeros((QBLK, GROUP_W), F32)
        dk_acc = jnp.zeros((QBLK, GROUP_W), F32)
        dv_acc = jnp.zeros((QBLK, GROUP_W), F32)

        def pair(qh, doh, k, v, mask, lse_h, dl_h):
            s = jnp.where(mask, _dot_nt(qh, k) * ATTN_SCALE, NEG_INF)
            p = jnp.exp(s - lse_h)
            ds = p * (_dot_nt(doh, v) - dl_h)
            return p.astype(BF16), ds.astype(BF16)

        for h, hm in enumerate(heads):
            first = lane == h * HEAD_DIM

            def col(v):
                return jnp.sum(jnp.where(first, v, 0.0), axis=1, keepdims=True)

            zero = jnp.zeros_like(q)
            qh, qnh = jnp.where(hm, q, zero), jnp.where(hm, qn, zero)
            doh, donh = jnp.where(hm, do, zero), jnp.where(hm, don, zero)
            lse_h, dl_h, lsen_h, dln_h = col(lse), col(dl), col(lsen), col(dln)
            p_c, ds_c = pair(qh, doh, kc, vc, mask_c, lse_h, dl_h)
            _, ds_p = pair(qh, doh, kp, vp, mask_p, lse_h, dl_h)
            p_n, ds_n = pair(qnh, donh, kc, vc, mask_n, lsen_h, dln_h)
            dq_h = _dot(ds_c, kc) + _dot(ds_p, kp)
            dk_h = _dot_tn(ds_c, q) + _dot_tn(ds_n, qn)
            dv_h = _dot_tn(p_c, do) + _dot_tn(p_n, don)
            dq_acc = jnp.where(hm, dq_h, dq_acc)
            dk_acc = jnp.where(hm, dk_h, dk_acc)
            dv_acc = jnp.where(hm, dv_h, dv_acc)
        ds_ref[:, 0:GROUP_W] = (dq_acc * ATTN_SCALE).astype(BF16)
        ds_ref[:, GROUP_W:2 * GROUP_W] = (dk_acc * ATTN_SCALE).astype(BF16)
        ds_ref[:, 2 * GROUP_W:3 * GROUP_W] = dv_acc.astype(BF16)

    sv = s.reshape(t, 3 * GROUP_W)
    cur, nxt = _qkv_block(0), _qkv_block(0, 1, nb)
    return _call(
        body, name=f"attn_bwd_d{dil}", grid=(nb,),
        in_specs=[cur, nxt, _qkv_block(1), _qkv_block(1, -1, nb), _qkv_block(2), _qkv_block(2, -1, nb),
                  cur, nxt, cur, nxt, cur, nxt],
        out_specs=[pl.BlockSpec((QBLK, 3 * GROUP_W), lambda b: (b, 0))],
        out_shape=[jax.ShapeDtypeStruct((t, 3 * GROUP_W), BF16)],
        args=(sv, sv, sv, sv, sv, sv, do, do, lse, lse, dl, dl), semantics=("parallel",), carry=carry)


def _conv_mixer_bwd(abcv, dya, conv_w, conv_b):
    t = abcv.shape[0]
    cw = conv_w.shape[1]
    tm = min(256, t)
    last = t // tm - 1

    def body(a_ref, ap_ref, an_ref, dya_ref, dyan_ref, cw_ref, cb_ref, d_ref, acc_ref):
        m = pl.program_id(0)

        @pl.when(m == 0)
        def _():
            acc_ref[...] = jnp.zeros_like(acc_ref)

        ab = a_ref[:, 0:cw].astype(F32)
        ac = a_ref[:, cw:2 * cw].astype(F32)
        av = a_ref[:, 2 * cw:3 * cw].astype(F32)
        u = ac * av
        hu = ap_ref[:, cw:2 * cw].astype(F32) * ap_ref[:, 2 * cw:3 * cw].astype(F32)
        hu = jnp.where(m > 0, hu, 0.0)
        u1 = _shift_down(u, hu, 1)
        u2 = _shift_down(u, hu, 2)
        cv = cw_ref[0:1, :] * u2 + cw_ref[1:2, :] * u1 + cw_ref[2:3, :] * u + cb_ref[...]
        dya_v = dya_ref[...].astype(F32)
        dcv = dya_v * ab
        ndcv = jnp.where(m < last, dyan_ref[...].astype(F32) * an_ref[:, 0:cw].astype(F32), 0.0)
        du = (cw_ref[2:3, :] * dcv + cw_ref[1:2, :] * _shift_up(dcv, ndcv, 1)
              + cw_ref[0:1, :] * _shift_up(dcv, ndcv, 2))
        d_ref[:, 0:cw] = (dya_v * cv).astype(BF16)
        d_ref[:, cw:2 * cw] = (du * av).astype(BF16)
        d_ref[:, 2 * cw:3 * cw] = (du * ac).astype(BF16)
        acc_ref[...] += _stack_rows([_colsum(dcv * u2), _colsum(dcv * u1), _colsum(dcv * u), _colsum(dcv)], cw)

    return pl.pallas_call(
        body, name="conv_mixer_bwd", grid=(t // tm,),
        in_specs=[_rows(tm, 3 * cw), _prev_halo(tm, 3 * cw), _next_halo(tm, 3 * cw, t), _rows(tm, cw),
                  _next_halo(tm, cw, t), _resident((3, cw)), _resident((1, cw))],
        out_specs=[_rows(tm, 3 * cw), _acc_spec(cw)],
        out_shape=[jax.ShapeDtypeStruct((t, 3 * cw), BF16), jax.ShapeDtypeStruct((SUBLANES, cw), F32)],
        compiler_params=_params("arbitrary"),
    )(abcv, abcv, abcv, dya, dya, conv_w, conv_b)


def _in_proj_bwd(x, dx1, dabcv, dss, dgates, w_in, g1, carry=None):
    t, d = x.shape
    qkv0 = dabcv.shape[1]
    n = w_in.shape[1]
    tm = min(256, t)

    def body(x_ref, dx1_ref, da_ref, ds0_ref, ds1_ref, ds2_ref, dg_ref, w_ref, g_ref, dx_ref, acc_ref):
        m = pl.program_id(0)

        @pl.when(m == 0)
        def _():
            acc_ref[...] = jnp.zeros_like(acc_ref)

        dh = _dot_nt(da_ref[...], w_ref[:, 0:qkv0]) + _dot_nt(dg_ref[...], w_ref[:, qkv0 + 3 * ATTN_W:n])
        for gi, (ds_ref, dil) in enumerate(zip((ds0_ref, ds1_ref, ds2_ref), DILATIONS)):
            ds = _load_streams(ds_ref, dil, tm)
            for j in range(3):
                c0 = qkv0 + j * ATTN_W + gi * GROUP_W
                dh = dh + _dot_nt(ds[:, j * GROUP_W:(j + 1) * GROUP_W], w_ref[:, c0:c0 + GROUP_W])
        dx, dg = _rms_bwd(x_ref[...], g_ref[...], dh)
        dx_ref[...] = dx1_ref[...] + dx
        acc_ref[...] += _stack_rows([_colsum(dg)], d)

    return _call(
        body, name="in_proj_bwd", grid=(t // tm,),
        in_specs=[_rows(tm, d), _rows(tm, d), _rows(tm, qkv0)]
        + [_stream_spec(dil, tm, 3 * GROUP_W) for dil in DILATIONS]
        + [_rows(tm, 2 * d), _resident((d, n)), _resident((1, d))],
        out_specs=[_rows(tm, d), _acc_spec(d)],
        out_shape=[jax.ShapeDtypeStruct((t, d), F32), jax.ShapeDtypeStruct((SUBLANES, d), F32)],
        args=(x, dx1, dabcv, *[_stream_view(a, dil) for a, dil in zip(dss, DILATIONS)], dgates, w_in, g1),
        semantics=("arbitrary",), carry=carry)


def _dw_in_qkv(x, gain, ds, dil):
    t, d = x.shape
    tk = min(256, t)
    width = 3 * GROUP_W

    def body(x_ref, g_ref, ds_ref, o_ref, acc_ref):
        k = pl.program_id(0)

        @pl.when(k == 0)
        def _():
            acc_ref[...] = jnp.zeros_like(acc_ref)

        h = _rms_fwd(x_ref[...], g_ref[...])[0].astype(BF16)
        acc_ref[...] += _dot_tn(h, _load_streams(ds_ref, dil, tk))

        @pl.when(k == t // tk - 1)
        def _():
            o_ref[...] = acc_ref[...].astype(BF16)

    return pl.pallas_call(
        body, name=f"dw_in_qkv_d{dil}", grid=(t // tk,),
        in_specs=[_rows(tk, d), _resident((1, d)), _stream_spec(dil, tk, width)],
        out_specs=pl.BlockSpec((d, width), lambda k: (0, 0)),
        out_shape=jax.ShapeDtypeStruct((d, width), BF16),
        scratch_shapes=[pltpu.VMEM((d, width), F32)],
        compiler_params=_params("arbitrary"),
    )(x, gain, _stream_view(ds, dil))


def _local_step(x, target, p, late):
    cw = p["conv_a_w"].shape[1]
    d = x.shape[1]
    (abcv, gates, *ss), (g_up,) = _in_proj(x, p["norm_mix_g"], p["w_in"], cw,
                                           carry=_Exchange("gather", [late["w_up"]]))
    w_up = _full_from_gathered("w_up", g_up)
    os, lses = zip(*[_attn_fwd(s, dil) for s, dil in zip(ss, DILATIONS)])
    x1, ya, yb, yap, ybp, merged = _mixer_out(x, abcv, gates, os, lses, p["conv_a_w"], p["conv_a_b"], p["b_gate"],
                                              p["w_proj_a"], p["w_proj_b"], p["w_out"])
    (up_pre,), (g_down,) = _rms_matmul(x1, p["norm_ffn_g"], w_up, (w_up.shape[1],), "up_proj",
                                       carry=_Exchange("gather", [late["w_down"]]))
    w_down = _full_from_gathered("w_down", g_down)
    act, dx2, acc_gf, loss = _ffn_loss(x1, up_pre, target, p["ffn_conv_w"], p["ffn_conv_b"], w_down,
                                       p["final_norm_g"])

    parts, got = {}, {}
    dup, acc_fc = _ffn_act_bwd(dx2, up_pre, p["ffn_conv_w"], p["ffn_conv_b"], w_down)
    parts["w_down"] = _by_destination("w_down", _tn_matmul(act, dx2, "dw_down"))
    (dpre, dx1, acc_g2), (got["w_down"],) = _ffn_up_bwd(dup, x1, dx2, p["ffn_conv_w"], w_up, p["norm_ffn_g"],
                                                        carry=_Exchange("scatter", [parts["w_down"]]))
    parts["w_up"] = _by_destination("w_up", _tn_matmul(x1, dpre, "dw_up", gain=p["norm_ffn_g"]))
    dgates, dyap, dybp, dya, dos, dls, acc_bg = _mixer_bwd(dx1, gates, yap, ybp, os, lses, p["b_gate"], p["w_out"],
                                                           p["w_proj_a"], p["w_proj_b"])
    parts["w_out"] = _by_destination("w_out", _tn_matmul(merged, dx1, "dw_out"))
    parts["w_proj_a"] = _by_destination("w_proj_a", _tn_matmul(ya, dyap, "dw_proj_a"))
    parts["w_proj_b"] = _by_destination("w_proj_b", _tn_matmul(yb, dybp, "dw_proj_b"))
    riders = (("w_up",), ("w_out", "w_proj_a", "w_proj_b"), ())
    dss = []
    for s, do, lse, dl, dil, names in zip(ss, dos, lses, dls, DILATIONS, riders):
        if names:
            (ds,), received = _attn_bwd(s, do, lse, dl, dil, carry=_Exchange("scatter", [parts[n] for n in names]))
            got.update(zip(names, received))
        else:
            (ds,) = _attn_bwd(s, do, lse, dl, dil)
        dss.append(ds)
    dabcv, acc_ca = _conv_mixer_bwd(abcv, dya, p["conv_a_w"], p["conv_a_b"])
    dw_s = [_dw_in_qkv(x, p["norm_mix_g"], ds, dil) for ds, dil in zip(dss, DILATIONS)]
    dw_qkv = [w[:, j * GROUP_W:(j + 1) * GROUP_W] for j in range(3) for w in dw_s]
    g_w_in = jnp.concatenate([_tn_matmul(x, dabcv, "dw_in_a", gain=p["norm_mix_g"]), *dw_qkv,
                              _tn_matmul(x, dgates, "dw_in_g", gain=p["norm_mix_g"])], axis=1)
    parts["w_in"] = _by_destination("w_in", g_w_in)
    (dx, acc_g1), (got["w_in"],) = _in_proj_bwd(x, dx1, dabcv, dss, dgates, p["w_in"], p["norm_mix_g"],
                                                carry=_Exchange("scatter", [parts["w_in"]]))
    small = dict(norm_mix_g=acc_g1[0:1], b_gate=acc_bg[0:2], conv_a_w=acc_ca[0:3], conv_a_b=acc_ca[3:4],
                 norm_ffn_g=acc_g2[0:1], ffn_conv_w=acc_fc[0:3], ffn_conv_b=acc_fc[3:4], final_norm_g=acc_gf[0:1])
    return loss[0, 0], dx, parts, got, small


def _all_gather(shards):
    n = len(shards)

    def body(*refs):
        ins, outs = refs[:n], refs[n:2 * n]
        send_sems, recv_sems, local_sems = refs[2 * n:]
        x, y, c = _mesh_pos()
        me, sibling = (x, y, c), (x, y, 1 - c)
        chips = [(1 - x, y), (x, 1 - y), (1 - x, 1 - y)]

        def copy(i, k, block, to, src=None):
            rows = outs[i].at[_dev_index(*block)]
            return pltpu.make_async_remote_copy(
                src_ref=rows if src is None else src, dst_ref=rows, send_sem=send_sems.at[i, k],
                recv_sem=recv_sems.at[i, k], device_id=to, device_id_type=MESH)

        mine, first, passed = [], [], []
        for i in range(n):
            cp = pltpu.make_async_copy(ins[i], outs[i].at[_dev_index(*me)], local_sems.at[i])
            cp.start()
            mine.append(cp)
            first.append(copy(i, 0, me, sibling, src=ins[i]))
            first += [copy(i, 1 + j, me, (*chip, c), src=ins[i]) for j, chip in enumerate(chips)]
        for cp in first:
            cp.start()
        for i in range(n):
            for j, chip in enumerate(chips):
                copy(i, 1 + j, (*chip, c), me).wait_recv()
                fw = copy(i, 4 + j, (*chip, c), sibling)
                fw.start()
                passed.append(fw)
        for i in range(n):
            copy(i, 0, sibling, me).wait_recv()
            for j, chip in enumerate(chips):
                copy(i, 4 + j, (*chip, 1 - c), me).wait_recv()
        for cp in first + passed:
            cp.wait_send()
        for cp in mine:
            cp.wait()

    return pl.pallas_call(
        body, name="all_gather_weights",
        in_specs=[ANY] * n, out_specs=[ANY] * n,
        out_shape=[jax.ShapeDtypeStruct((N_DEV,) + s.shape, s.dtype) for s in shards],
        scratch_shapes=[pltpu.SemaphoreType.DMA((n, 7)), pltpu.SemaphoreType.DMA((n, 7)),
                        pltpu.SemaphoreType.DMA((n,))],
    )(*shards)


def _all_reduce_small(v):
    r = v.shape[0]

    def body(v_ref, o_ref, gath, send_sems, recv_sems):
        x, y, c = _mesh_pos()
        me = _dev_index(x, y, c)
        gath[me] = v_ref[...]
        flips = [(kx, ky, kc) for kx in (0, 1) for ky in (0, 1) for kc in (0, 1)][1:]
        copies = []
        for k, (kx, ky, kc) in enumerate(flips):
            px = 1 - x if kx else x
            py = 1 - y if ky else y
            pc = 1 - c if kc else c
            cp = pltpu.make_async_remote_copy(
                src_ref=v_ref, dst_ref=gath.at[me], send_sem=send_sems.at[k], recv_sem=recv_sems.at[k],
                device_id=(px, py, pc), device_id_type=MESH)
            cp.start()
            copies.append((cp, _dev_index(px, py, pc)))
        for k, (cp, peer) in enumerate(copies):
            pltpu.make_async_remote_copy(
                src_ref=v_ref, dst_ref=gath.at[peer], send_sem=send_sems.at[k], recv_sem=recv_sems.at[k],
                device_id=(x, y, c), device_id_type=MESH).wait_recv()
        for cp, _ in copies:
            cp.wait_send()
        total = gath[0]
        for j in range(1, N_DEV):
            total = total + gath[j]
        o_ref[...] = total

    return pl.pallas_call(
        body, name="all_reduce_small",
        in_specs=[pl.BlockSpec(memory_space=pltpu.VMEM)], out_specs=pl.BlockSpec(memory_space=pltpu.VMEM),
        out_shape=jax.ShapeDtypeStruct((r, LANES), F32),
        scratch_shapes=[pltpu.VMEM((N_DEV, r, LANES), F32), pltpu.SemaphoreType.DMA((7,)),
                        pltpu.SemaphoreType.DMA((7,))],
    )(v)


def _adamw_math(w, g, m, v):
    m2 = ADAM_B1 * m + (1.0 - ADAM_B1) * g
    v2 = ADAM_B2 * v + (1.0 - ADAM_B2) * (g * g)
    m_hat = m2 / (1.0 - ADAM_B1 ** ADAM_STEP)
    v_hat = v2 / (1.0 - ADAM_B2 ** ADAM_STEP)
    delta = -ADAM_LR * (m_hat / (jnp.sqrt(v_hat) + ADAM_EPS) + ADAM_WD * w)
    return delta, m2, v2


def _adamw_big(w, m, v, part, got, me):
    r, c = w.shape
    tr = r if r <= 512 else 256

    def body(me_ref, w_ref, m_ref, v_ref, own_ref, *rest):
        del me_ref
        got_refs, (g_out, d_out, m_out, v_out) = rest[:N_DEV - 1], rest[N_DEV - 1:]
        g = own_ref[...].astype(F32)
        for ref in got_refs:
            g = g + ref[...].astype(F32)
        delta, m2, v2 = _adamw_math(w_ref[...], g, m_ref[...], v_ref[...])
        g_out[...] = g
        d_out[...] = delta
        m_out[...] = m2
        v_out[...] = v2

    def peer_block(k):
        return pl.BlockSpec((None, tr, c), lambda i, me_ref: (jnp.bitwise_xor(me_ref[0], k), i, 0))

    plain = pl.BlockSpec((tr, c), lambda i, me_ref: (i, 0))
    out = jax.ShapeDtypeStruct((r, c), F32)
    return pl.pallas_call(
        body, name="adamw_big",
        grid_spec=pltpu.PrefetchScalarGridSpec(
            num_scalar_prefetch=1, grid=(r // tr,),
            in_specs=[plain, plain, plain] + [peer_block(k) for k in range(N_DEV)],
            out_specs=[plain] * 4),
        out_shape=[out] * 4,
        compiler_params=_params("parallel"),
    )(me, w, m, v, part, *([got] * (N_DEV - 1)))


def _adamw_small(w, g, m, v):
    def body(w_ref, g_ref, m_ref, v_ref, d_out, m_out, v_out):
        delta, m2, v2 = _adamw_math(w_ref[...], g_ref[...], m_ref[...], v_ref[...])
        d_out[...] = delta
        m_out[...] = m2
        v_out[...] = v2

    out = jax.ShapeDtypeStruct(w.shape, F32)
    return pl.pallas_call(body, name="adamw_small", out_shape=[out] * 3)(w, g, m, v)


BIG = ("w_in", "w_proj_a", "w_proj_b", "w_out", "w_up", "w_down")
EARLY = ("w_in", "w_proj_a", "w_proj_b", "w_out")
LATE = ("w_up", "w_down")
ROW_SHARDED = ("w_out", "w_down")
SMALL = ("norm_mix_g", "b_gate", "conv_a_w", "conv_a_b", "norm_ffn_g", "ffn_conv_w", "ffn_conv_b", "final_norm_g")
SMALL_SHARDED = ("b_gate", "conv_a_w", "ffn_conv_w")
WEIGHTS = ("norm_mix_g", "w_in", "b_gate", "conv_a_w", "conv_a_b", "w_proj_a", "w_proj_b", "w_out", "norm_ffn_g",
           "w_up", "ffn_conv_w", "ffn_conv_b", "w_down", "final_norm_g")


def _pack(vectors, rows):
    flat = jnp.concatenate([v.reshape(-1) for v in vectors])
    return jnp.pad(flat, (0, rows * LANES - flat.shape[0])).reshape(rows, LANES)


def _packed_rows(count):
    rows = -(-count // LANES)
    return -(-rows // SUBLANES) * SUBLANES


def _unpack(packed, shapes):
    flat = packed.reshape(-1)
    out, lo = [], 0
    for s in shapes:
        size = 1
        for dim in s:
            size *= dim
        out.append(flat[lo:lo + size].reshape(s))
        lo += size
    return out


def _full_from_gathered(name, gathered):
    _, r, c = gathered.shape
    if name in ROW_SHARDED:
        return gathered.reshape(N_DEV * r, c)
    return gathered.transpose(1, 0, 2).reshape(r, N_DEV * c)


def _by_destination(name, grad):
    rr, cc = grad.shape
    g = grad.astype(BF16)
    if name in ROW_SHARDED:
        return g.reshape(N_DEV, rr // N_DEV, cc)
    return g.reshape(rr, N_DEV, cc // N_DEV).transpose(1, 0, 2)


def kernel(x, norm_mix_g, w_in, b_gate, conv_a_w, conv_a_b, w_proj_a, w_proj_b, w_out, norm_ffn_g, w_up, ffn_conv_w, ffn_conv_b, w_down, final_norm_g, loss_target, m_norm_mix_g, m_w_in, m_b_gate, m_conv_a_w, m_conv_a_b, m_w_proj_a, m_w_proj_b, m_w_out, m_norm_ffn_g, m_w_up, m_ffn_conv_w, m_ffn_conv_b, m_w_down, m_final_norm_g, v_norm_mix_g, v_w_in, v_b_gate, v_conv_a_w, v_conv_a_b, v_w_proj_a, v_w_proj_b, v_w_out, v_norm_ffn_g, v_w_up, v_ffn_conv_w, v_ffn_conv_b, v_w_down, v_final_norm_g):
    given = dict(locals())
    shard = {n: given[n] for n in WEIGHTS}
    mom_m = {n: given["m_" + n] for n in WEIGHTS}
    mom_v = {n: given["v_" + n] for n in WEIGHTS}
    xi, yi, ci = _mesh_pos()
    me = _dev_index(xi, yi, ci)
    me1 = me.astype(jnp.int32).reshape(1)

    big2d = {n: shard[n].reshape(shard[n].shape[-2:]) for n in BIG}
    small_shapes = [shard[n].shape[1:] for n in SMALL_SHARDED]
    n_small = sum(s[0] * s[1] for s in small_shapes)
    packed_small = _pack([shard[n] for n in SMALL_SHARDED], _packed_rows(n_small))
    gathered = _all_gather([big2d[n].astype(BF16) for n in EARLY] + [packed_small])
    p = {n: _full_from_gathered(n, g) for n, g in zip(EARLY, gathered[:-1])}
    per_dev = [_unpack(gathered[-1][j], small_shapes) for j in range(N_DEV)]
    for i, n in enumerate(SMALL_SHARDED):
        p[n] = jnp.concatenate([per_dev[j][i] for j in range(N_DEV)], axis=1)
    p["norm_mix_g"], p["norm_ffn_g"] = shard["norm_mix_g"], shard["norm_ffn_g"]
    p["conv_a_b"], p["ffn_conv_b"] = shard["conv_a_b"], shard["ffn_conv_b"]
    p["final_norm_g"] = shard["final_norm_g"].reshape(1, -1)
    late = {n: big2d[n].astype(BF16) for n in LATE}

    loss_part, dx, parts, got, g_small = _local_step(x[0], loss_target[0], p, late)

    results = {}
    for n in BIG:
        lead = shard[n].shape
        outs = _adamw_big(big2d[n], mom_m[n].reshape(lead[-2:]), mom_v[n].reshape(lead[-2:]), parts[n], got[n], me1)
        results[n] = [o.reshape(lead) for o in outs]

    small_full_shapes = [g_small[n].shape for n in SMALL]
    n_vec = sum(s[0] * s[1] for s in small_full_shapes) + 1
    packed = _pack([g_small[n] for n in SMALL] + [loss_part.reshape(1)], _packed_rows(n_vec))
    reduced = _all_reduce_small(packed)
    *g_full, loss_vec = _unpack(reduced, small_full_shapes + [(1,)])
    loss = loss_vec[0]
    own_g = []
    for n, g in zip(SMALL, g_full):
        if n in SMALL_SHARDED:
            width = shard[n].shape[-1]
            g = lax.dynamic_slice_in_dim(g, me * width, width, axis=1)
        own_g.append(g.reshape(shard[n].shape))
    own_shapes = [shard[n].shape for n in SMALL]
    rows = _packed_rows(sum(g.size for g in own_g))
    small_out = _adamw_small(_pack([shard[n] for n in SMALL], rows), _pack(own_g, rows),
                             _pack([mom_m[n] for n in SMALL], rows), _pack([mom_v[n] for n in SMALL], rows))
    deltas, new_ms, new_vs = (_unpack(o, own_shapes) for o in small_out)
    for i, n in enumerate(SMALL):
        results[n] = [own_g[i], deltas[i], new_ms[i], new_vs[i]]

    grad_x = dx.reshape(x.shape)
    return (loss, grad_x, *[results[n][0] for n in WEIGHTS], *[results[n][1] for n in WEIGHTS],
            *[results[n][2] for n in WEIGHTS], *[results[n][3] for n in WEIGHTS])
```

```python
import functools

import jax
import jax.numpy as jnp
from jax import lax
from jax.experimental import pallas as pl
from jax.experimental.pallas import tpu as pltpu

F32 = jnp.float32
BF16 = jnp.bfloat16
MESH = pl.DeviceIdType.MESH

N_DEV = 8
RMS_EPS = 1e-6
NEG_INF = -1e30
N_GROUPS = 3
DILATIONS = (1, 4, 16)
HEADS_PER_GROUP = 4
HEAD_DIM = 64
GROUP_W = HEADS_PER_GROUP * HEAD_DIM
ATTN_W = N_GROUPS * GROUP_W
QBLK = 128
ATTN_SCALE = HEAD_DIM ** -0.5

ADAM_LR = 0.001
ADAM_B1 = 0.9
ADAM_B2 = 0.999
ADAM_EPS = 1e-08
ADAM_WD = 0.01
ADAM_STEP = 10

HALO = 16
LANES = 128
SUBLANES = 8
VMEM_LIMIT_BYTES = 56 * 1024 * 1024


def _params(*sem):
    return pltpu.CompilerParams(dimension_semantics=sem, vmem_limit_bytes=VMEM_LIMIT_BYTES)


def _pick_tile(n, cap):
    if n <= cap:
        return n
    best = None
    for t in range(LANES, cap + 1, LANES):
        if n % t == 0:
            best = t
    assert best is not None, (n, cap)
    return best


def _rows(tm, c, j=0):
    return pl.BlockSpec((tm, c), lambda m: (m, j))


def _prev_halo(tm, c):
    return pl.BlockSpec((HALO, c), lambda m: (jnp.maximum(m * (tm // HALO) - 1, 0), 0))


def _next_halo(tm, c, t_total):
    last = t_total // HALO - 1
    return pl.BlockSpec((HALO, c), lambda m: (jnp.minimum((m + 1) * (tm // HALO), last), 0))


def _resident(shape):
    nd = len(shape)
    return pl.BlockSpec(shape, lambda *_: (0,) * nd, pipeline_mode=pl.Buffered(1))


def _acc_spec(c):
    return pl.BlockSpec((SUBLANES, c), lambda *_: (0, 0))


def _shift_down(u, halo, k):
    ext = jnp.concatenate([halo, u], axis=0)
    return pltpu.roll(ext, k, 0)[HALO:, :]


def _shift_up(u, halo, k):
    ext = jnp.concatenate([u, halo], axis=0)
    return pltpu.roll(ext, ext.shape[0] - k, 0)[: u.shape[0], :]


def _stack_rows(rows, c):
    idx = lax.broadcasted_iota(jnp.int32, (SUBLANES, c), 0)
    out = jnp.zeros((SUBLANES, c), F32)
    for i, r in enumerate(rows):
        out = out + jnp.where(idx == i, r, 0.0)
    return out


def _colsum(v):
    return jnp.sum(v, axis=0, keepdims=True)


def _sigmoid(v):
    return 1.0 / (1.0 + jnp.exp(-v))


def _rms_fwd(xv, g):
    r = lax.rsqrt(jnp.mean(xv * xv, axis=-1, keepdims=True) + RMS_EPS)
    return xv * r * g, r


def _rms_bwd(xv, g, dy):
    r = lax.rsqrt(jnp.mean(xv * xv, axis=-1, keepdims=True) + RMS_EPS)
    xn = xv * r
    dxn = dy * g
    dx = r * (dxn - xn * jnp.mean(dxn * xn, axis=-1, keepdims=True))
    return dx, dy * xn


def _dot(a, b):
    return jnp.dot(a, b, preferred_element_type=F32)


def _dot_nt(a, b):
    return lax.dot_general(a, b, (((1,), (1,)), ((), ())), preferred_element_type=F32)


def _dot_tn(a, b):
    return lax.dot_general(a, b, (((0,), (0,)), ((), ())), preferred_element_type=F32)


def _perm(dil, n, inverse=False):
    i = lax.broadcasted_iota(jnp.int32, (n, n), 0)
    j = lax.broadcasted_iota(jnp.int32, (n, n), 1)
    if inverse:
        i, j = j, i
    per = n // dil
    return (j == (i % per) * dil + i // per).astype(BF16)


def _permute_rows(pm, v):
    if v.dtype == BF16:
        return _dot(pm, v).astype(BF16)
    h1 = v.astype(BF16)
    r1 = v - h1.astype(F32)
    h2 = r1.astype(BF16)
    h3 = (r1 - h2.astype(F32)).astype(BF16)
    return _dot(pm, h1) + _dot(pm, h2) + _dot(pm, h3)


def _stream_view(a, dil):
    t, c = a.shape
    return a.reshape(dil, t // dil, c)


def _stream_spec(dil, tm, c):
    return pl.BlockSpec((dil, tm // dil, c), lambda m: (0, m, 0))


def _load_streams(ref, dil, tm):
    v = ref[...].reshape(tm, ref.shape[-1])
    return v if dil == 1 else _permute_rows(_perm(dil, tm, inverse=True), v)


def _store_streams(ref, dil, tm, v):
    if dil > 1:
        v = _permute_rows(_perm(dil, tm), v)
    ref[...] = v.reshape(ref.shape).astype(ref.dtype)


ANY = pl.BlockSpec(memory_space=pl.ANY)


def _mesh_pos():
    return lax.axis_index("x"), lax.axis_index("y"), lax.axis_index("c")


def _dev_index(px, py, pc):
    return 4 * px + 2 * py + pc


class _Exchange:
    def __init__(self, mode, arrays):
        self.mode, self.arrays = mode, list(arrays)
        n = len(self.arrays)
        if mode == "gather":
            self.out_shape = [jax.ShapeDtypeStruct((N_DEV,) + a.shape, a.dtype) for a in self.arrays]
        else:
            self.out_shape = [jax.ShapeDtypeStruct(a.shape, a.dtype) for a in self.arrays]
        self.scratch = [pltpu.SemaphoreType.DMA((n, N_DEV - 1)), pltpu.SemaphoreType.DMA((n, N_DEV - 1)),
                        pltpu.SemaphoreType.DMA((n,))]

    def _peers(self):
        x, y, c = _mesh_pos()
        flips = [(kx, ky, kc) for kx in (0, 1) for ky in (0, 1) for kc in (0, 1)][1:]
        peers = [(1 - x if kx else x, 1 - y if ky else y, 1 - c if kc else c) for kx, ky, kc in flips]
        return _dev_index(x, y, c), peers

    def _copy(self, ins, outs, sems, i, k, peer, me, sending):
        src = ins[i] if self.mode == "gather" else ins[i].at[_dev_index(*peer)]
        dst = outs[i].at[me if sending else _dev_index(*peer)]
        return pltpu.make_async_remote_copy(src_ref=src, dst_ref=dst, send_sem=sems[0].at[i, k],
                                            recv_sem=sems[1].at[i, k], device_id=peer, device_id_type=MESH)

    def _own(self, ins, outs, sems, i, me):
        return pltpu.make_async_copy(ins[i], outs[i].at[me], sems[2].at[i])

    def start(self, ins, outs, sems):
        me, peers = self._peers()
        for i in range(len(ins)):
            if self.mode == "gather":
                self._own(ins, outs, sems, i, me).start()
            for k, peer in enumerate(peers):
                self._copy(ins, outs, sems, i, k, peer, me, True).start()

    def wait(self, ins, outs, sems):
        me, peers = self._peers()
        for i in range(len(ins)):
            for k, peer in enumerate(peers):
                self._copy(ins, outs, sems, i, k, peer, me, False).wait_recv()
            for k, peer in enumerate(peers):
                self._copy(ins, outs, sems, i, k, peer, me, True).wait_send()
            if self.mode == "gather":
                self._own(ins, outs, sems, i, me).wait()


def _call(body, *, name, grid, in_specs, out_specs, out_shape, args, semantics, carry=None):
    if carry is None:
        return pl.pallas_call(body, name=name, grid=grid, in_specs=in_specs, out_specs=out_specs,
                              out_shape=out_shape, compiler_params=_params(*semantics))(*args)
    n_in, n_out, n_x = len(in_specs), len(out_specs), len(carry.arrays)

    def carried(*refs):
        ins, x_ins = refs[:n_in], refs[n_in:n_in + n_x]
        outs = refs[n_in + n_x:n_in + n_x + n_out]
        x_outs = refs[n_in + n_x + n_out:n_in + 2 * n_x + n_out]
        sems = refs[n_in + 2 * n_x + n_out:]
        first = functools.reduce(jnp.logical_and, [pl.program_id(a) == 0 for a in range(len(grid))])
        last = functools.reduce(jnp.logical_and, [pl.program_id(a) == grid[a] - 1 for a in range(len(grid))])

        @pl.when(first)
        def _():
            carry.start(x_ins, x_outs, sems)

        body(*ins, *outs)

        @pl.when(last)
        def _():
            carry.wait(x_ins, x_outs, sems)

    res = pl.pallas_call(
        carried, name=name, grid=grid, in_specs=list(in_specs) + [ANY] * n_x,
        out_specs=list(out_specs) + [ANY] * n_x, out_shape=list(out_shape) + carry.out_shape,
        scratch_shapes=carry.scratch, compiler_params=_params(*["arbitrary"] * len(grid)),
    )(*args, *carry.arrays)
    return list(res[:n_out]), list(res[n_out:])


def _rms_matmul(x, g, w, widths, name, carry=None):
    t, d = x.shape
    n = w.shape[1]
    assert sum(widths) == n
    tm = min(256, t)
    spans, lo = [], 0
    for wd in widths:
        spans.append((lo, wd))
        lo += wd

    def body(x_ref, g_ref, w_ref, *o_refs):
        h, _ = _rms_fwd(x_ref[...], g_ref[...])
        h = h.astype(BF16)
        for o_ref, (c0, wd) in zip(o_refs, spans):
            o_ref[...] = _dot(h, w_ref[:, c0:c0 + wd]).astype(BF16)

    return _call(
        body, name=name, grid=(t // tm,),
        in_specs=[_rows(tm, d), _resident((1, d)), _resident((d, n))],
        out_specs=[_rows(tm, wd) for wd in widths],
        out_shape=[jax.ShapeDtypeStruct((t, wd), BF16) for wd in widths],
        args=(x, g, w), semantics=("parallel",), carry=carry)


def _in_proj(x, g, w, cw, carry=None):
    t, d = x.shape
    n = w.shape[1]
    tm = min(256, t)
    qkv0 = 3 * cw

    def body(x_ref, g_ref, w_ref, abcv_ref, gates_ref, *s_refs):
        h, _ = _rms_fwd(x_ref[...], g_ref[...])
        h = h.astype(BF16)
        abcv_ref[...] = _dot(h, w_ref[:, 0:qkv0]).astype(BF16)
        gates_ref[...] = _dot(h, w_ref[:, qkv0 + 3 * ATTN_W:n]).astype(BF16)
        for gi, s_ref in enumerate(s_refs):
            cols = [_dot(h, w_ref[:, qkv0 + j * ATTN_W + gi * GROUP_W:qkv0 + j * ATTN_W + (gi + 1) * GROUP_W])
                    for j in range(3)]
            _store_streams(s_ref, DILATIONS[gi], tm, jnp.concatenate(cols, axis=1).astype(BF16))

    return _call(
        body, name="in_proj", grid=(t // tm,),
        in_specs=[_rows(tm, d), _resident((1, d)), _resident((d, n))],
        out_specs=[_rows(tm, qkv0), _rows(tm, 2 * d)] + [_stream_spec(dil, tm, 3 * GROUP_W) for dil in DILATIONS],
        out_shape=[jax.ShapeDtypeStruct((t, qkv0), BF16), jax.ShapeDtypeStruct((t, 2 * d), BF16)]
        + [jax.ShapeDtypeStruct((dil, t // dil, 3 * GROUP_W), BF16) for dil in DILATIONS],
        args=(x, g, w), semantics=("parallel",), carry=carry)


def _head_masks():
    lane = lax.broadcasted_iota(jnp.int32, (1, GROUP_W), 1)
    return lane, [(lane // HEAD_DIM) == h for h in range(HEADS_PER_GROUP)]


def _qkv_block(col, shift=0, nb=None):
    if shift == 0:
        return pl.BlockSpec((QBLK, GROUP_W), lambda b: (b, col))
    return pl.BlockSpec((QBLK, GROUP_W), lambda b: (jnp.clip(b + shift, 0, nb - 1), col))


def _band_masks(has_other):
    row = lax.broadcasted_iota(jnp.int32, (QBLK, QBLK), 0)
    col = lax.broadcasted_iota(jnp.int32, (QBLK, QBLK), 1)
    return col <= row, (col >= row) & has_other


def _attn_fwd(s, dil, carry=None):
    t = s.shape[0] * s.shape[1]
    nb = t // QBLK
    per_stream = nb // dil

    def body(q_ref, kc_ref, kp_ref, vc_ref, vp_ref, o_ref, lse_ref):
        b = pl.program_id(0)
        has_prev = lax.rem(b, per_stream) != 0
        mask_c, mask_p = _band_masks(has_prev)
        _, heads = _head_masks()
        q, kc, kp, vc, vp = q_ref[...], kc_ref[...], kp_ref[...], vc_ref[...], vp_ref[...]
        o_acc = jnp.zeros((QBLK, GROUP_W), F32)
        lse_acc = jnp.zeros((QBLK, GROUP_W), F32)
        for hm in heads:
            qh = jnp.where(hm, q, jnp.zeros_like(q))
            sc = jnp.where(mask_c, _dot_nt(qh, kc) * ATTN_SCALE, NEG_INF)
            sp = jnp.where(mask_p, _dot_nt(qh, kp) * ATTN_SCALE, NEG_INF)
            mx = jnp.maximum(jnp.max(sc, axis=1, keepdims=True), jnp.max(sp, axis=1, keepdims=True))
            pc = jnp.exp(sc - mx)
            pp = jnp.exp(sp - mx)
            den = jnp.sum(pc, axis=1, keepdims=True) + jnp.sum(pp, axis=1, keepdims=True)
            oh = _dot(pc.astype(BF16), vc) + _dot(pp.astype(BF16), vp)
            o_acc = jnp.where(hm, oh / den, o_acc)
            lse_acc = jnp.where(hm, mx + jnp.log(den), lse_acc)
        o_ref[...] = o_acc.astype(BF16)
        lse_ref[...] = lse_acc

    sv = s.reshape(t, 3 * GROUP_W)
    return _call(
        body, name=f"attn_fwd_d{dil}", grid=(nb,),
        in_specs=[_qkv_block(0), _qkv_block(1), _qkv_block(1, -1, nb), _qkv_block(2), _qkv_block(2, -1, nb)],
        out_specs=[_qkv_block(0), _qkv_block(0)],
        out_shape=[jax.ShapeDtypeStruct((t, GROUP_W), BF16), jax.ShapeDtypeStruct((t, GROUP_W), F32)],
        args=(sv, sv, sv, sv, sv), semantics=("parallel",), carry=carry)


def _group_softmax(parts):
    mx = jnp.maximum(jnp.maximum(parts[0], parts[1]), parts[2])
    es = [jnp.exp(p - mx) for p in parts]
    den = es[0] + es[1] + es[2]
    return [e / den for e in es]


def _mixer_out(x, abcv, gates, os, lses, conv_w, conv_b, b_gate, w_pa, w_pb, w_o):
    t, d = x.shape
    cw = conv_w.shape[1]
    tm = min(256, t)

    def body(x_ref, abcv_ref, halo_ref, gates_ref, o0_ref, o1_ref, o2_ref, l0_ref, l1_ref, l2_ref, cw_ref, cb_ref,
             bg_ref, wpa_ref, wpb_ref, wo_ref, x1_ref, ya_ref, yb_ref, yap_ref, ybp_ref, mg_ref):
        m = pl.program_id(0)
        ab = abcv_ref[:, 0:cw].astype(F32)
        u = abcv_ref[:, cw:2 * cw].astype(F32) * abcv_ref[:, 2 * cw:3 * cw].astype(F32)
        hu = halo_ref[:, cw:2 * cw].astype(F32) * halo_ref[:, 2 * cw:3 * cw].astype(F32)
        hu = jnp.where(m > 0, hu, 0.0)
        cv = (cw_ref[0:1, :] * _shift_down(u, hu, 2) + cw_ref[1:2, :] * _shift_down(u, hu, 1)
              + cw_ref[2:3, :] * u + cb_ref[...])
        ya = (ab * cv).astype(BF16)
        ya_ref[...] = ya
        alphas = _group_softmax([_load_streams(r, dil, tm) for r, dil in zip((l0_ref, l1_ref, l2_ref), DILATIONS)])
        for i, (o_ref, dil) in enumerate(zip((o0_ref, o1_ref, o2_ref), DILATIONS)):
            sl = slice(i * GROUP_W, (i + 1) * GROUP_W)
            yb_ref[:, sl] = (alphas[i] * _load_streams(o_ref, dil, tm).astype(F32)).astype(BF16)
        yap = _dot(ya, wpa_ref[...])
        ybp = _dot(yb_ref[...], wpb_ref[...])
        yap_ref[...] = yap.astype(BF16)
        ybp_ref[...] = ybp.astype(BF16)
        sa = _sigmoid(gates_ref[:, 0:d].astype(F32) + bg_ref[0:1, :])
        sb = _sigmoid(gates_ref[:, d:2 * d].astype(F32) + bg_ref[1:2, :])
        merged = (sa * yap + sb * ybp).astype(BF16)
        mg_ref[...] = merged
        x1_ref[...] = x_ref[...] + _dot(merged, wo_ref[...])

    return pl.pallas_call(
        body, name="mixer_out", grid=(t // tm,),
        in_specs=[_rows(tm, d), _rows(tm, 3 * cw), _prev_halo(tm, 3 * cw), _rows(tm, 2 * d)]
        + [_stream_spec(dil, tm, GROUP_W) for dil in DILATIONS] * 2
        + [_resident((3, cw)), _resident((1, cw)), _resident((2, d)),
           _resident((cw, d)), _resident((ATTN_W, d)), _resident((d, d))],
        out_specs=[_rows(tm, d), _rows(tm, cw), _rows(tm, ATTN_W), _rows(tm, d), _rows(tm, d), _rows(tm, d)],
        out_shape=[jax.ShapeDtypeStruct((t, d), F32), jax.ShapeDtypeStruct((t, cw), BF16),
                   jax.ShapeDtypeStruct((t, ATTN_W), BF16), jax.ShapeDtypeStruct((t, d), BF16),
                   jax.ShapeDtypeStruct((t, d), BF16), jax.ShapeDtypeStruct((t, d), BF16)],
        compiler_params=_params("parallel"),
    )(x, abcv, abcv, gates, *[_stream_view(a, dil) for a, dil in zip(os, DILATIONS)],
      *[_stream_view(a, dil) for a, dil in zip(lses, DILATIONS)], conv_w, conv_b, b_gate, w_pa, w_pb, w_o)


def _ffn_conv(p_ref, halo_ref, w_ref, b_ref, m, c0, wd):
    p = p_ref[:, c0:c0 + wd].astype(F32)
    hp = jnp.where(m > 0, halo_ref[:, c0:c0 + wd].astype(F32), 0.0)
    return (w_ref[0:1, c0:c0 + wd] * _shift_down(p, hp, 2) + w_ref[1:2, c0:c0 + wd] * _shift_down(p, hp, 1)
            + w_ref[2:3, c0:c0 + wd] * p + b_ref[:, c0:c0 + wd])


def _ffn_loss(x1, up_pre, target, conv_w, conv_b, w_d, g_f):
    t, d = x1.shape
    dff = w_d.shape[0]
    tm = min(256, t)
    ck = _pick_tile(dff, 1408)

    def body(x1_ref, up_ref, halo_ref, tg_ref, cw_ref, cb_ref, wd_ref, gf_ref, act_ref, conv_ref, dx2_ref, acc_ref,
             loss_ref):
        m = pl.program_id(0)

        @pl.when(m == 0)
        def _():
            acc_ref[...] = jnp.zeros_like(acc_ref)
            loss_ref[...] = jnp.zeros_like(loss_ref)

        x2 = x1_ref[...]
        for c0 in range(0, dff, ck):
            gate = _ffn_conv(up_ref, halo_ref, cw_ref, cb_ref, m, c0, ck)
            val = _ffn_conv(up_ref, halo_ref, cw_ref, cb_ref, m, dff + c0, ck)
            conv_ref[:, c0:c0 + ck] = gate.astype(BF16)
            conv_ref[:, dff + c0:dff + c0 + ck] = val.astype(BF16)
            act = (gate * _sigmoid(gate) * val).astype(BF16)
            act_ref[:, c0:c0 + ck] = act
            x2 = x2 + _dot(act, wd_ref[c0:c0 + ck, :])
        y, _ = _rms_fwd(x2, gf_ref[...])
        diff = y - tg_ref[...]
        loss_ref[...] += 0.5 * jnp.sum(jnp.mean(diff * diff, axis=-1, keepdims=True))
        dx2, dg = _rms_bwd(x2, gf_ref[...], diff * (1.0 / d))
        dx2_ref[...] = dx2
        acc_ref[...] += _stack_rows([_colsum(dg)], d)

    return pl.pallas_call(
        body, name="ffn_loss", grid=(t // tm,),
        in_specs=[_rows(tm, d), _rows(tm, 2 * dff), _prev_halo(tm, 2 * dff), _rows(tm, d),
                  _resident((3, 2 * dff)), _resident((1, 2 * dff)), _resident((dff, d)), _resident((1, d))],
        out_specs=[_rows(tm, dff), _rows(tm, 2 * dff), _rows(tm, d), _acc_spec(d), _acc_spec(LANES)],
        out_shape=[jax.ShapeDtypeStruct((t, dff), BF16), jax.ShapeDtypeStruct((t, 2 * dff), BF16),
                   jax.ShapeDtypeStruct((t, d), F32), jax.ShapeDtypeStruct((SUBLANES, d), F32),
                   jax.ShapeDtypeStruct((SUBLANES, LANES), F32)],
        compiler_params=_params("arbitrary"),
    )(x1, up_pre, up_pre, target, conv_w, conv_b, w_d, g_f)


def _ffn_act_bwd(dx2, conv, w_d):
    t, d = dx2.shape
    dff = w_d.shape[0]
    tm = min(256, t)
    ck = _pick_tile(dff, 1408)

    def body(dx2_ref, conv_ref, wd_ref, dup_ref, acc_ref):
        m = pl.program_id(0)

        @pl.when(m == 0)
        def _():
            acc_ref[...] = jnp.zeros_like(acc_ref)

        dx2v = dx2_ref[...].astype(BF16)
        for c0 in range(0, dff, ck):
            dact = _dot_nt(dx2v, wd_ref[c0:c0 + ck, :])
            gate = conv_ref[:, c0:c0 + ck].astype(F32)
            val = conv_ref[:, dff + c0:dff + c0 + ck].astype(F32)
            sg = _sigmoid(gate)
            dval = dact * gate * sg
            dgate = dact * val * sg * (1.0 + gate * (1.0 - sg))
            dup_ref[:, c0:c0 + ck] = dgate.astype(BF16)
            dup_ref[:, dff + c0:dff + c0 + ck] = dval.astype(BF16)
            acc_ref[:, c0:c0 + ck] += _stack_rows([_colsum(dgate)], ck)
            acc_ref[:, dff + c0:dff + c0 + ck] += _stack_rows([_colsum(dval)], ck)

    return pl.pallas_call(
        body, name="ffn_act_bwd", grid=(t // tm,),
        in_specs=[_rows(tm, d), _rows(tm, 2 * dff), _resident((dff, d))],
        out_specs=[_rows(tm, 2 * dff), _acc_spec(2 * dff)],
        out_shape=[jax.ShapeDtypeStruct((t, 2 * dff), BF16), jax.ShapeDtypeStruct((SUBLANES, 2 * dff), F32)],
        compiler_params=_params("arbitrary"),
    )(dx2, conv, w_d)


def _ffn_up_bwd(dup, up_pre, x1, dx2, conv_w, w_u, g2, carry=None):
    t, d = x1.shape
    n = dup.shape[1]
    tm = min(256, t)
    ck = _pick_tile(n, 1408)
    last = t // tm - 1

    def body(dup_ref, nxt_ref, up_ref, x1_ref, dx2_ref, cw_ref, wu_ref, g2_ref, dpre_ref, dx1_ref, acc_ref, accw_ref):
        m = pl.program_id(0)

        @pl.when(m == 0)
        def _():
            acc_ref[...] = jnp.zeros_like(acc_ref)
            accw_ref[...] = jnp.zeros_like(accw_ref)

        dh = jnp.zeros((tm, d), F32)
        for c0 in range(0, n, ck):
            du = dup_ref[:, c0:c0 + ck].astype(F32)
            hn = jnp.where(m < last, nxt_ref[:, c0:c0 + ck].astype(F32), 0.0)
            du1 = _shift_up(du, hn, 1)
            du2 = _shift_up(du, hn, 2)
            dpre = (cw_ref[2:3, c0:c0 + ck] * du + cw_ref[1:2, c0:c0 + ck] * du1
                    + cw_ref[0:1, c0:c0 + ck] * du2).astype(BF16)
            dpre_ref[:, c0:c0 + ck] = dpre
            dh = dh + _dot_nt(dpre, wu_ref[:, c0:c0 + ck])
            p = up_ref[:, c0:c0 + ck].astype(F32)
            accw_ref[:, c0:c0 + ck] += _stack_rows([_colsum(du2 * p), _colsum(du1 * p), _colsum(du * p)], ck)
        dx, dg = _rms_bwd(x1_ref[...], g2_ref[...], dh)
        dx1_ref[...] = dx2_ref[...] + dx
        acc_ref[...] += _stack_rows([_colsum(dg)], d)

    return _call(
        body, name="ffn_up_bwd", grid=(t // tm,),
        in_specs=[_rows(tm, n), _next_halo(tm, n, t), _rows(tm, n), _rows(tm, d), _rows(tm, d), _resident((3, n)),
                  _resident((d, n)), _resident((1, d))],
        out_specs=[_rows(tm, n), _rows(tm, d), _acc_spec(d), _acc_spec(n)],
        out_shape=[jax.ShapeDtypeStruct((t, n), BF16), jax.ShapeDtypeStruct((t, d), F32),
                   jax.ShapeDtypeStruct((SUBLANES, d), F32), jax.ShapeDtypeStruct((SUBLANES, n), F32)],
        args=(dup, dup, up_pre, x1, dx2, conv_w, w_u, g2), semantics=("arbitrary",), carry=carry)


def _tn_matmul(a, b, name, gain=None):
    t, mdim = a.shape
    n = b.shape[1]
    tk = min(512, t)
    tmm = mdim if gain is not None else _pick_tile(mdim, 1408)
    tn = _pick_tile(n, 1536)

    def body(*refs):
        if gain is not None:
            a_ref, b_ref, g_ref, o_ref, acc_ref = refs
        else:
            a_ref, b_ref, o_ref, acc_ref = refs
        k = pl.program_id(2)

        @pl.when(k == 0)
        def _():
            acc_ref[...] = jnp.zeros_like(acc_ref)

        av = a_ref[...]
        if gain is not None:
            av = _rms_fwd(av, g_ref[...])[0]
        acc_ref[...] += _dot_tn(av.astype(BF16), b_ref[...].astype(BF16))

        @pl.when(k == t // tk - 1)
        def _():
            o_ref[...] = acc_ref[...].astype(BF16)

    in_specs = [pl.BlockSpec((tk, tmm), lambda i, j, k: (k, i)), pl.BlockSpec((tk, tn), lambda i, j, k: (k, j))]
    args = [a, b]
    if gain is not None:
        in_specs.append(pl.BlockSpec((1, mdim), lambda i, j, k: (0, 0)))
        args.append(gain)
    return pl.pallas_call(
        body, name=name, grid=(mdim // tmm, n // tn, t // tk),
        in_specs=in_specs,
        out_specs=pl.BlockSpec((tmm, tn), lambda i, j, k: (i, j)),
        out_shape=jax.ShapeDtypeStruct((mdim, n), BF16),
        scratch_shapes=[pltpu.VMEM((tmm, tn), F32)],
        compiler_params=_params("parallel", "parallel", "arbitrary"),
    )(*args)


def _mixer_bwd(dx1, gates, yap, ybp, os, lses, b_gate, w_o, w_pa, w_pb):
    t, d = dx1.shape
    cw = w_pa.shape[0]
    tm = min(256, t)

    def body(dx1_ref, gates_ref, yap_ref, ybp_ref, o0_ref, o1_ref, o2_ref, l0_ref, l1_ref, l2_ref, bg_ref, wo_ref,
             wpa_ref, wpb_ref, dgates_ref, dyap_ref, dybp_ref, dya_ref, do0_ref, do1_ref, do2_ref, dl0_ref, dl1_ref,
             dl2_ref, acc_ref):
        m = pl.program_id(0)

        @pl.when(m == 0)
        def _():
            acc_ref[...] = jnp.zeros_like(acc_ref)

        dmg = _dot_nt(dx1_ref[...].astype(BF16), wo_ref[...])
        sa = _sigmoid(gates_ref[:, 0:d].astype(F32) + bg_ref[0:1, :])
        sb = _sigmoid(gates_ref[:, d:2 * d].astype(F32) + bg_ref[1:2, :])
        dyap = (dmg * sa).astype(BF16)
        dybp = (dmg * sb).astype(BF16)
        dga = dmg * yap_ref[...].astype(F32) * sa * (1.0 - sa)
        dgb = dmg * ybp_ref[...].astype(F32) * sb * (1.0 - sb)
        dyap_ref[...] = dyap
        dybp_ref[...] = dybp
        dgates_ref[:, 0:d] = dga.astype(BF16)
        dgates_ref[:, d:2 * d] = dgb.astype(BF16)
        acc_ref[...] += _stack_rows([_colsum(dga), _colsum(dgb)], d)
        dya_ref[...] = _dot_nt(dyap, wpa_ref[...]).astype(BF16)
        dyb = _dot_nt(dybp, wpb_ref[...])

        ri = lax.broadcasted_iota(jnp.int32, (GROUP_W, GROUP_W), 0) // HEAD_DIM
        ci = lax.broadcasted_iota(jnp.int32, (GROUP_W, GROUP_W), 1) // HEAD_DIM
        same_head = (ri == ci).astype(BF16)
        alphas = _group_softmax([_load_streams(r, dil, tm) for r, dil in zip((l0_ref, l1_ref, l2_ref), DILATIONS)])
        dtot = jnp.zeros((tm, GROUP_W), F32)
        for i, (o_ref, do_ref, dil) in enumerate(zip((o0_ref, o1_ref, o2_ref), (do0_ref, do1_ref, do2_ref), DILATIONS)):
            dov = alphas[i] * dyb[:, i * GROUP_W:(i + 1) * GROUP_W]
            _store_streams(do_ref, dil, tm, dov.astype(BF16))
            prod = dov * _load_streams(o_ref, dil, tm).astype(F32)
            hi = prod.astype(BF16)
            lo = (prod - hi.astype(F32)).astype(BF16)
            dtot = dtot + _dot(hi, same_head) + _dot(lo, same_head)
        for alpha, dl_ref, dil in zip(alphas, (dl0_ref, dl1_ref, dl2_ref), DILATIONS):
            _store_streams(dl_ref, dil, tm, alpha * dtot)

    streams = [_stream_spec(dil, tm, GROUP_W) for dil in DILATIONS]
    res = pl.pallas_call(
        body, name="mixer_bwd", grid=(t // tm,),
        in_specs=[_rows(tm, d), _rows(tm, 2 * d), _rows(tm, d), _rows(tm, d)] + streams * 2
        + [_resident((2, d)), _resident((d, d)), _resident((cw, d)), _resident((ATTN_W, d))],
        out_specs=[_rows(tm, 2 * d), _rows(tm, d), _rows(tm, d), _rows(tm, cw)] + streams * 2 + [_acc_spec(d)],
        out_shape=[jax.ShapeDtypeStruct((t, 2 * d), BF16), jax.ShapeDtypeStruct((t, d), BF16),
                   jax.ShapeDtypeStruct((t, d), BF16), jax.ShapeDtypeStruct((t, cw), BF16)]
        + [jax.ShapeDtypeStruct((dil, t // dil, GROUP_W), BF16) for dil in DILATIONS]
        + [jax.ShapeDtypeStruct((dil, t // dil, GROUP_W), F32) for dil in DILATIONS]
        + [jax.ShapeDtypeStruct((SUBLANES, d), F32)],
        compiler_params=_params("arbitrary"),
    )(dx1, gates, yap, ybp, *[_stream_view(a, dil) for a, dil in zip(os, DILATIONS)],
      *[_stream_view(a, dil) for a, dil in zip(lses, DILATIONS)], b_gate, w_o, w_pa, w_pb)
    dgates, dyap, dybp, dya = res[:4]
    dos = [a.reshape(t, GROUP_W) for a in res[4:7]]
    dls = [a.reshape(t, GROUP_W) for a in res[7:10]]
    return dgates, dyap, dybp, dya, dos, dls, res[10]


def _attn_bwd(s, do, lse, dl, dil, carry=None):
    t = s.shape[0] * s.shape[1]
    nb = t // QBLK
    per_stream = nb // dil

    def body(q_ref, qn_ref, kc_ref, kp_ref, vc_ref, vp_ref, do_ref, don_ref, lse_ref, lsen_ref, dl_ref, dln_ref,
             ds_ref):
        b = pl.program_id(0)
        has_prev = lax.rem(b, per_stream) != 0
        has_next = lax.rem(b + 1, per_stream) != 0
        mask_c, mask_p = _band_masks(has_prev)
        _, mask_n = _band_masks(has_next)
        lane, heads = _head_masks()
        q, qn, kc, kp, vc, vp = q_ref[...], qn_ref[...], kc_ref[...], kp_ref[...], vc_ref[...], vp_ref[...]
        do, don = do_ref[...], don_ref[...]
        lse, lsen, dl, dln = lse_ref[...], lsen_ref[...], dl_ref[...], dln_ref[...]
        dq_acc = jnp.zeros((QBLK, GROUP_W), F32)
        dk_acc = jnp.zeros((QBLK, GROUP_W), F32)
        dv_acc = jnp.zeros((QBLK, GROUP_W), F32)

        def pair(qh, doh, k, v, mask, lse_h, dl_h):
            s = jnp.where(mask, _dot_nt(qh, k) * ATTN_SCALE, NEG_INF)
            p = jnp.exp(s - lse_h)
            ds = p * (_dot_nt(doh, v) - dl_h)
            return p.astype(BF16), ds.astype(BF16)

        for h, hm in enumerate(heads):
            first = lane == h * HEAD_DIM

            def col(v):
                return jnp.sum(jnp.where(first, v, 0.0), axis=1, keepdims=True)

            zero = jnp.zeros_like(q)
            qh, qnh = jnp.where(hm, q, zero), jnp.where(hm, qn, zero)
            doh, donh = jnp.where(hm, do, zero), jnp.where(hm, don, zero)
            lse_h, dl_h, lsen_h, dln_h = col(lse), col(dl), col(lsen), col(dln)
            p_c, ds_c = pair(qh, doh, kc, vc, mask_c, lse_h, dl_h)
            _, ds_p = pair(qh, doh, kp, vp, mask_p, lse_h, dl_h)
            p_n, ds_n = pair(qnh, donh, kc, vc, mask_n, lsen_h, dln_h)
            dq_h = _dot(ds_c, kc) + _dot(ds_p, kp)
            dk_h = _dot_tn(ds_c, q) + _dot_tn(ds_n, qn)
            dv_h = _dot_tn(p_c, do) + _dot_tn(p_n, don)
            dq_acc = jnp.where(hm, dq_h, dq_acc)
            dk_acc = jnp.where(hm, dk_h, dk_acc)
            dv_acc = jnp.where(hm, dv_h, dv_acc)
        ds_ref[:, 0:GROUP_W] = (dq_acc * ATTN_SCALE).astype(BF16)
        ds_ref[:, GROUP_W:2 * GROUP_W] = (dk_acc * ATTN_SCALE).astype(BF16)
        ds_ref[:, 2 * GROUP_W:3 * GROUP_W] = dv_acc.astype(BF16)

    sv = s.reshape(t, 3 * GROUP_W)
    cur, nxt = _qkv_block(0), _qkv_block(0, 1, nb)
    return _call(
        body, name=f"attn_bwd_d{dil}", grid=(nb,),
        in_specs=[cur, nxt, _qkv_block(1), _qkv_block(1, -1, nb), _qkv_block(2), _qkv_block(2, -1, nb),
                  cur, nxt, cur, nxt, cur, nxt],
        out_specs=[pl.BlockSpec((QBLK, 3 * GROUP_W), lambda b: (b, 0))],
        out_shape=[jax.ShapeDtypeStruct((t, 3 * GROUP_W), BF16)],
        args=(sv, sv, sv, sv, sv, sv, do, do, lse, lse, dl, dl), semantics=("parallel",), carry=carry)


def _conv_mixer_bwd(abcv, dya, conv_w, conv_b):
    t = abcv.shape[0]
    cw = conv_w.shape[1]
    tm = min(256, t)
    last = t // tm - 1

    def body(a_ref, ap_ref, an_ref, dya_ref, dyan_ref, cw_ref, cb_ref, d_ref, acc_ref):
        m = pl.program_id(0)

        @pl.when(m == 0)
        def _():
            acc_ref[...] = jnp.zeros_like(acc_ref)

        ab = a_ref[:, 0:cw].astype(F32)
        ac = a_ref[:, cw:2 * cw].astype(F32)
        av = a_ref[:, 2 * cw:3 * cw].astype(F32)
        u = ac * av
        hu = ap_ref[:, cw:2 * cw].astype(F32) * ap_ref[:, 2 * cw:3 * cw].astype(F32)
        hu = jnp.where(m > 0, hu, 0.0)
        u1 = _shift_down(u, hu, 1)
        u2 = _shift_down(u, hu, 2)
        cv = cw_ref[0:1, :] * u2 + cw_ref[1:2, :] * u1 + cw_ref[2:3, :] * u + cb_ref[...]
        dya_v = dya_ref[...].astype(F32)
        dcv = dya_v * ab
        ndcv = jnp.where(m < last, dyan_ref[...].astype(F32) * an_ref[:, 0:cw].astype(F32), 0.0)
        du = (cw_ref[2:3, :] * dcv + cw_ref[1:2, :] * _shift_up(dcv, ndcv, 1)
              + cw_ref[0:1, :] * _shift_up(dcv, ndcv, 2))
        d_ref[:, 0:cw] = (dya_v * cv).astype(BF16)
        d_ref[:, cw:2 * cw] = (du * av).astype(BF16)
        d_ref[:, 2 * cw:3 * cw] = (du * ac).astype(BF16)
        acc_ref[...] += _stack_rows([_colsum(dcv * u2), _colsum(dcv * u1), _colsum(dcv * u), _colsum(dcv)], cw)

    return pl.pallas_call(
        body, name="conv_mixer_bwd", grid=(t // tm,),
        in_specs=[_rows(tm, 3 * cw), _prev_halo(tm, 3 * cw), _next_halo(tm, 3 * cw, t), _rows(tm, cw),
                  _next_halo(tm, cw, t), _resident((3, cw)), _resident((1, cw))],
        out_specs=[_rows(tm, 3 * cw), _acc_spec(cw)],
        out_shape=[jax.ShapeDtypeStruct((t, 3 * cw), BF16), jax.ShapeDtypeStruct((SUBLANES, cw), F32)],
        compiler_params=_params("arbitrary"),
    )(abcv, abcv, abcv, dya, dya, conv_w, conv_b)


def _in_proj_bwd(x, dx1, dabcv, dss, dgates, w_in, g1, carry=None):
    t, d = x.shape
    qkv0 = dabcv.shape[1]
    n = w_in.shape[1]
    tm = min(256, t)

    def body(x_ref, dx1_ref, da_ref, ds0_ref, ds1_ref, ds2_ref, dg_ref, w_ref, g_ref, dx_ref, acc_ref):
        m = pl.program_id(0)

        @pl.when(m == 0)
        def _():
            acc_ref[...] = jnp.zeros_like(acc_ref)

        dh = _dot_nt(da_ref[...], w_ref[:, 0:qkv0]) + _dot_nt(dg_ref[...], w_ref[:, qkv0 + 3 * ATTN_W:n])
        for gi, (ds_ref, dil) in enumerate(zip((ds0_ref, ds1_ref, ds2_ref), DILATIONS)):
            ds = _load_streams(ds_ref, dil, tm)
            for j in range(3):
                c0 = qkv0 + j * ATTN_W + gi * GROUP_W
                dh = dh + _dot_nt(ds[:, j * GROUP_W:(j + 1) * GROUP_W], w_ref[:, c0:c0 + GROUP_W])
        dx, dg = _rms_bwd(x_ref[...], g_ref[...], dh)
        dx_ref[...] = dx1_ref[...] + dx
        acc_ref[...] += _stack_rows([_colsum(dg)], d)

    return _call(
        body, name="in_proj_bwd", grid=(t // tm,),
        in_specs=[_rows(tm, d), _rows(tm, d), _rows(tm, qkv0)]
        + [_stream_spec(dil, tm, 3 * GROUP_W) for dil in DILATIONS]
        + [_rows(tm, 2 * d), _resident((d, n)), _resident((1, d))],
        out_specs=[_rows(tm, d), _acc_spec(d)],
        out_shape=[jax.ShapeDtypeStruct((t, d), F32), jax.ShapeDtypeStruct((SUBLANES, d), F32)],
        args=(x, dx1, dabcv, *[_stream_view(a, dil) for a, dil in zip(dss, DILATIONS)], dgates, w_in, g1),
        semantics=("arbitrary",), carry=carry)


def _dw_in_qkv(x, gain, ds, dil):
    t, d = x.shape
    tk = min(256, t)
    width = 3 * GROUP_W

    def body(x_ref, g_ref, ds_ref, o_ref, acc_ref):
        k = pl.program_id(0)

        @pl.when(k == 0)
        def _():
            acc_ref[...] = jnp.zeros_like(acc_ref)

        h = _rms_fwd(x_ref[...], g_ref[...])[0].astype(BF16)
        acc_ref[...] += _dot_tn(h, _load_streams(ds_ref, dil, tk))

        @pl.when(k == t // tk - 1)
        def _():
            o_ref[...] = acc_ref[...].astype(BF16)

    return pl.pallas_call(
        body, name=f"dw_in_qkv_d{dil}", grid=(t // tk,),
        in_specs=[_rows(tk, d), _resident((1, d)), _stream_spec(dil, tk, width)],
        out_specs=pl.BlockSpec((d, width), lambda k: (0, 0)),
        out_shape=jax.ShapeDtypeStruct((d, width), BF16),
        scratch_shapes=[pltpu.VMEM((d, width), F32)],
        compiler_params=_params("arbitrary"),
    )(x, gain, _stream_view(ds, dil))


def _local_step(x, target, p, late):
    cw = p["conv_a_w"].shape[1]
    d = x.shape[1]
    (abcv, gates, *ss), (g_up,) = _in_proj(x, p["norm_mix_g"], p["w_in"], cw,
                                           carry=_Exchange("gather", [late["w_up"]]))
    w_up = _full_from_gathered("w_up", g_up)
    os, lses = zip(*[_attn_fwd(s, dil) for s, dil in zip(ss, DILATIONS)])
    x1, ya, yb, yap, ybp, merged = _mixer_out(x, abcv, gates, os, lses, p["conv_a_w"], p["conv_a_b"], p["b_gate"],
                                              p["w_proj_a"], p["w_proj_b"], p["w_out"])
    (up_pre,), (g_down,) = _rms_matmul(x1, p["norm_ffn_g"], w_up, (w_up.shape[1],), "up_proj",
                                       carry=_Exchange("gather", [late["w_down"]]))
    w_down = _full_from_gathered("w_down", g_down)
    act, conv, dx2, acc_gf, loss = _ffn_loss(x1, up_pre, target, p["ffn_conv_w"], p["ffn_conv_b"], w_down,
                                       p["final_norm_g"])

    parts, got = {}, {}
    dup, acc_fb = _ffn_act_bwd(dx2, conv, w_down)
    parts["w_down"] = _by_destination("w_down", _tn_matmul(act, dx2, "dw_down"))
    (dpre, dx1, acc_g2, acc_fw), (got["w_down"],) = _ffn_up_bwd(dup, up_pre, x1, dx2, p["ffn_conv_w"], w_up,
                                                                p["norm_ffn_g"],
                                                        carry=_Exchange("scatter", [parts["w_down"]]))
    parts["w_up"] = _by_destination("w_up", _tn_matmul(x1, dpre, "dw_up", gain=p["norm_ffn_g"]))
    dgates, dyap, dybp, dya, dos, dls, acc_bg = _mixer_bwd(dx1, gates, yap, ybp, os, lses, p["b_gate"], p["w_out"],
                                                           p["w_proj_a"], p["w_proj_b"])
    parts["w_out"] = _by_destination("w_out", _tn_matmul(merged, dx1, "dw_out"))
    parts["w_proj_a"] = _by_destination("w_proj_a", _tn_matmul(ya, dyap, "dw_proj_a"))
    parts["w_proj_b"] = _by_destination("w_proj_b", _tn_matmul(yb, dybp, "dw_proj_b"))
    riders = (("w_up",), ("w_out", "w_proj_a", "w_proj_b"), ())
    dss = []
    for s, do, lse, dl, dil, names in zip(ss, dos, lses, dls, DILATIONS, riders):
        if names:
            (ds,), received = _attn_bwd(s, do, lse, dl, dil, carry=_Exchange("scatter", [parts[n] for n in names]))
            got.update(zip(names, received))
        else:
            (ds,) = _attn_bwd(s, do, lse, dl, dil)
        dss.append(ds)
    dabcv, acc_ca = _conv_mixer_bwd(abcv, dya, p["conv_a_w"], p["conv_a_b"])
    dw_s = [_dw_in_qkv(x, p["norm_mix_g"], ds, dil) for ds, dil in zip(dss, DILATIONS)]
    dw_qkv = [w[:, j * GROUP_W:(j + 1) * GROUP_W] for j in range(3) for w in dw_s]
    g_w_in = jnp.concatenate([_tn_matmul(x, dabcv, "dw_in_a", gain=p["norm_mix_g"]), *dw_qkv,
                              _tn_matmul(x, dgates, "dw_in_g", gain=p["norm_mix_g"])], axis=1)
    parts["w_in"] = _by_destination("w_in", g_w_in)
    (dx, acc_g1), (got["w_in"],) = _in_proj_bwd(x, dx1, dabcv, dss, dgates, p["w_in"], p["norm_mix_g"],
                                                carry=_Exchange("scatter", [parts["w_in"]]))
    small = dict(norm_mix_g=acc_g1[0:1], b_gate=acc_bg[0:2], conv_a_w=acc_ca[0:3], conv_a_b=acc_ca[3:4],
                 norm_ffn_g=acc_g2[0:1], ffn_conv_w=acc_fw[0:3], ffn_conv_b=acc_fb[0:1], final_norm_g=acc_gf[0:1])
    return loss[0, 0], dx, parts, got, small


def _all_gather(shards):
    n = len(shards)

    def body(*refs):
        ins, outs = refs[:n], refs[n:2 * n]
        send_sems, recv_sems, local_sems = refs[2 * n:]
        x, y, c = _mesh_pos()
        me, sibling = (x, y, c), (x, y, 1 - c)
        chips = [(1 - x, y), (x, 1 - y), (1 - x, 1 - y)]

        def copy(i, k, block, to, src=None):
            rows = outs[i].at[_dev_index(*block)]
            return pltpu.make_async_remote_copy(
                src_ref=rows if src is None else src, dst_ref=rows, send_sem=send_sems.at[i, k],
                recv_sem=recv_sems.at[i, k], device_id=to, device_id_type=MESH)

        mine, first, passed = [], [], []
        for i in range(n):
            cp = pltpu.make_async_copy(ins[i], outs[i].at[_dev_index(*me)], local_sems.at[i])
            cp.start()
            mine.append(cp)
            first.append(copy(i, 0, me, sibling, src=ins[i]))
            first += [copy(i, 1 + j, me, (*chip, c), src=ins[i]) for j, chip in enumerate(chips)]
        for cp in first:
            cp.start()
        for i in range(n):
            for j, chip in enumerate(chips):
                copy(i, 1 + j, (*chip, c), me).wait_recv()
                fw = copy(i, 4 + j, (*chip, c), sibling)
                fw.start()
                passed.append(fw)
        for i in range(n):
            copy(i, 0, sibling, me).wait_recv()
            for j, chip in enumerate(chips):
                copy(i, 4 + j, (*chip, 1 - c), me).wait_recv()
        for cp in first + passed:
            cp.wait_send()
        for cp in mine:
            cp.wait()

    return pl.pallas_call(
        body, name="all_gather_weights",
        in_specs=[ANY] * n, out_specs=[ANY] * n,
        out_shape=[jax.ShapeDtypeStruct((N_DEV,) + s.shape, s.dtype) for s in shards],
        scratch_shapes=[pltpu.SemaphoreType.DMA((n, 7)), pltpu.SemaphoreType.DMA((n, 7)),
                        pltpu.SemaphoreType.DMA((n,))],
    )(*shards)


def _all_reduce_small(v):
    r = v.shape[0]

    def body(v_ref, o_ref, gath, send_sems, recv_sems):
        x, y, c = _mesh_pos()
        me = _dev_index(x, y, c)
        gath[me] = v_ref[...]
        flips = [(kx, ky, kc) for kx in (0, 1) for ky in (0, 1) for kc in (0, 1)][1:]
        copies = []
        for k, (kx, ky, kc) in enumerate(flips):
            px = 1 - x if kx else x
            py = 1 - y if ky else y
            pc = 1 - c if kc else c
            cp = pltpu.make_async_remote_copy(
                src_ref=v_ref, dst_ref=gath.at[me], send_sem=send_sems.at[k], recv_sem=recv_sems.at[k],
                device_id=(px, py, pc), device_id_type=MESH)
            cp.start()
            copies.append((cp, _dev_index(px, py, pc)))
        for k, (cp, peer) in enumerate(copies):
            pltpu.make_async_remote_copy(
                src_ref=v_ref, dst_ref=gath.at[peer], send_sem=send_sems.at[k], recv_sem=recv_sems.at[k],
                device_id=(x, y, c), device_id_type=MESH).wait_recv()
        for cp, _ in copies:
            cp.wait_send()
        total = gath[0]
        for j in range(1, N_DEV):
            total = total + gath[j]
        o_ref[...] = total

    return pl.pallas_call(
        body, name="all_reduce_small",
        in_specs=[pl.BlockSpec(memory_space=pltpu.VMEM)], out_specs=pl.BlockSpec(memory_space=pltpu.VMEM),
        out_shape=jax.ShapeDtypeStruct((r, LANES), F32),
        scratch_shapes=[pltpu.VMEM((N_DEV, r, LANES), F32), pltpu.SemaphoreType.DMA((7,)),
                        pltpu.SemaphoreType.DMA((7,))],
    )(v)


def _adamw_math(w, g, m, v):
    m2 = ADAM_B1 * m + (1.0 - ADAM_B1) * g
    v2 = ADAM_B2 * v + (1.0 - ADAM_B2) * (g * g)
    m_hat = m2 / (1.0 - ADAM_B1 ** ADAM_STEP)
    v_hat = v2 / (1.0 - ADAM_B2 ** ADAM_STEP)
    delta = -ADAM_LR * (m_hat / (jnp.sqrt(v_hat) + ADAM_EPS) + ADAM_WD * w)
    return delta, m2, v2


def _adamw_big(w, m, v, part, got, me):
    r, c = w.shape
    tr = r if r <= 512 else 256

    def body(me_ref, w_ref, m_ref, v_ref, own_ref, *rest):
        del me_ref
        got_refs, (g_out, d_out, m_out, v_out) = rest[:N_DEV - 1], rest[N_DEV - 1:]
        g = own_ref[...].astype(F32)
        for ref in got_refs:
            g = g + ref[...].astype(F32)
        delta, m2, v2 = _adamw_math(w_ref[...], g, m_ref[...], v_ref[...])
        g_out[...] = g
        d_out[...] = delta
        m_out[...] = m2
        v_out[...] = v2

    def peer_block(k):
        return pl.BlockSpec((None, tr, c), lambda i, me_ref: (jnp.bitwise_xor(me_ref[0], k), i, 0))

    plain = pl.BlockSpec((tr, c), lambda i, me_ref: (i, 0))
    out = jax.ShapeDtypeStruct((r, c), F32)
    return pl.pallas_call(
        body, name="adamw_big",
        grid_spec=pltpu.PrefetchScalarGridSpec(
            num_scalar_prefetch=1, grid=(r // tr,),
            in_specs=[plain, plain, plain] + [peer_block(k) for k in range(N_DEV)],
            out_specs=[plain] * 4),
        out_shape=[out] * 4,
        compiler_params=_params("parallel"),
    )(me, w, m, v, part, *([got] * (N_DEV - 1)))


def _adamw_small(w, g, m, v):
    def body(w_ref, g_ref, m_ref, v_ref, d_out, m_out, v_out):
        delta, m2, v2 = _adamw_math(w_ref[...], g_ref[...], m_ref[...], v_ref[...])
        d_out[...] = delta
        m_out[...] = m2
        v_out[...] = v2

    out = jax.ShapeDtypeStruct(w.shape, F32)
    return pl.pallas_call(body, name="adamw_small", out_shape=[out] * 3)(w, g, m, v)


BIG = ("w_in", "w_proj_a", "w_proj_b", "w_out", "w_up", "w_down")
EARLY = ("w_in", "w_proj_a", "w_proj_b", "w_out")
LATE = ("w_up", "w_down")
ROW_SHARDED = ("w_out", "w_down")
SMALL = ("norm_mix_g", "b_gate", "conv_a_w", "conv_a_b", "norm_ffn_g", "ffn_conv_w", "ffn_conv_b", "final_norm_g")
SMALL_SHARDED = ("b_gate", "conv_a_w", "ffn_conv_w")
WEIGHTS = ("norm_mix_g", "w_in", "b_gate", "conv_a_w", "conv_a_b", "w_proj_a", "w_proj_b", "w_out", "norm_ffn_g",
           "w_up", "ffn_conv_w", "ffn_conv_b", "w_down", "final_norm_g")


def _pack(vectors, rows):
    flat = jnp.concatenate([v.reshape(-1) for v in vectors])
    return jnp.pad(flat, (0, rows * LANES - flat.shape[0])).reshape(rows, LANES)


def _packed_rows(count):
    rows = -(-count // LANES)
    return -(-rows // SUBLANES) * SUBLANES


def _unpack(packed, shapes):
    flat = packed.reshape(-1)
    out, lo = [], 0
    for s in shapes:
        size = 1
        for dim in s:
            size *= dim
        out.append(flat[lo:lo + size].reshape(s))
        lo += size
    return out


def _full_from_gathered(name, gathered):
    _, r, c = gathered.shape
    if name in ROW_SHARDED:
        return gathered.reshape(N_DEV * r, c)
    return gathered.transpose(1, 0, 2).reshape(r, N_DEV * c)


def _by_destination(name, grad):
    rr, cc = grad.shape
    g = grad.astype(BF16)
    if name in ROW_SHARDED:
        return g.reshape(N_DEV, rr // N_DEV, cc)
    return g.reshape(rr, N_DEV, cc // N_DEV).transpose(1, 0, 2)


def kernel(x, norm_mix_g, w_in, b_gate, conv_a_w, conv_a_b, w_proj_a, w_proj_b, w_out, norm_ffn_g, w_up, ffn_conv_w, ffn_conv_b, w_down, final_norm_g, loss_target, m_norm_mix_g, m_w_in, m_b_gate, m_conv_a_w, m_conv_a_b, m_w_proj_a, m_w_proj_b, m_w_out, m_norm_ffn_g, m_w_up, m_ffn_conv_w, m_ffn_conv_b, m_w_down, m_final_norm_g, v_norm_mix_g, v_w_in, v_b_gate, v_conv_a_w, v_conv_a_b, v_w_proj_a, v_w_proj_b, v_w_out, v_norm_ffn_g, v_w_up, v_ffn_conv_w, v_ffn_conv_b, v_w_down, v_final_norm_g):
    given = dict(locals())
    shard = {n: given[n] for n in WEIGHTS}
    mom_m = {n: given["m_" + n] for n in WEIGHTS}
    mom_v = {n: given["v_" + n] for n in WEIGHTS}
    xi, yi, ci = _mesh_pos()
    me = _dev_index(xi, yi, ci)
    me1 = me.astype(jnp.int32).reshape(1)

    big2d = {n: shard[n].reshape(shard[n].shape[-2:]) for n in BIG}
    small_shapes = [shard[n].shape[1:] for n in SMALL_SHARDED]
    n_small = sum(s[0] * s[1] for s in small_shapes)
    packed_small = _pack([shard[n] for n in SMALL_SHARDED], _packed_rows(n_small))
    gathered = _all_gather([big2d[n].astype(BF16) for n in EARLY] + [packed_small])
    p = {n: _full_from_gathered(n, g) for n, g in zip(EARLY, gathered[:-1])}
    per_dev = [_unpack(gathered[-1][j], small_shapes) for j in range(N_DEV)]
    for i, n in enumerate(SMALL_SHARDED):
        p[n] = jnp.concatenate([per_dev[j][i] for j in range(N_DEV)], axis=1)
    p["norm_mix_g"], p["norm_ffn_g"] = shard["norm_mix_g"], shard["norm_ffn_g"]
    p["conv_a_b"], p["ffn_conv_b"] = shard["conv_a_b"], shard["ffn_conv_b"]
    p["final_norm_g"] = shard["final_norm_g"].reshape(1, -1)
    late = {n: big2d[n].astype(BF16) for n in LATE}

    loss_part, dx, parts, got, g_small = _local_step(x[0], loss_target[0], p, late)

    results = {}
    for n in BIG:
        lead = shard[n].shape
        outs = _adamw_big(big2d[n], mom_m[n].reshape(lead[-2:]), mom_v[n].reshape(lead[-2:]), parts[n], got[n], me1)
        results[n] = [o.reshape(lead) for o in outs]

    small_full_shapes = [g_small[n].shape for n in SMALL]
    n_vec = sum(s[0] * s[1] for s in small_full_shapes) + 1
    packed = _pack([g_small[n] for n in SMALL] + [loss_part.reshape(1)], _packed_rows(n_vec))
    reduced = _all_reduce_small(packed)
    *g_full, loss_vec = _unpack(reduced, small_full_shapes + [(1,)])
    loss = loss_vec[0]
    own_g = []
    for n, g in zip(SMALL, g_full):
        if n in SMALL_SHARDED:
            width = shard[n].shape[-1]
            g = lax.dynamic_slice_in_dim(g, me * width, width, axis=1)
        own_g.append(g.reshape(shard[n].shape))
    own_shapes = [shard[n].shape for n in SMALL]
    rows = _packed_rows(sum(g.size for g in own_g))
    small_out = _adamw_small(_pack([shard[n] for n in SMALL], rows), _pack(own_g, rows),
                             _pack([mom_m[n] for n in SMALL], rows), _pack([mom_v[n] for n in SMALL], rows))
    deltas, new_ms, new_vs = (_unpack(o, own_shapes) for o in small_out)
    for i, n in enumerate(SMALL):
        results[n] = [own_g[i], deltas[i], new_ms[i], new_vs[i]]

    grad_x = dx.reshape(x.shape)
    return (loss, grad_x, *[results[n][0] for n in WEIGHTS], *[results[n][1] for n in WEIGHTS],
            *[results[n][2] for n in WEIGHTS], *[results[n][3] for n in WEIGHTS])
```

```python
import functools

import jax
import jax.numpy as jnp
from jax import lax
from jax.experimental import pallas as pl
from jax.experimental.pallas import tpu as pltpu

F32 = jnp.float32
BF16 = jnp.bfloat16
MESH = pl.DeviceIdType.MESH

N_DEV = 8
RMS_EPS = 1e-6
NEG_INF = -1e30
N_GROUPS = 3
DILATIONS = (1, 4, 16)
HEADS_PER_GROUP = 4
HEAD_DIM = 64
GROUP_W = HEADS_PER_GROUP * HEAD_DIM
ATTN_W = N_GROUPS * GROUP_W
QBLK = 128
ATTN_SCALE = HEAD_DIM ** -0.5

ADAM_LR = 0.001
ADAM_B1 = 0.9
ADAM_B2 = 0.999
ADAM_EPS = 1e-08
ADAM_WD = 0.01
ADAM_STEP = 10

HALO = 16
LANES = 128
SUBLANES = 8
VMEM_LIMIT_BYTES = 56 * 1024 * 1024


def _params(*sem):
    return pltpu.CompilerParams(dimension_semantics=sem, vmem_limit_bytes=VMEM_LIMIT_BYTES)


def _pick_tile(n, cap):
    if n <= cap:
        return n
    best = None
    for t in range(LANES, cap + 1, LANES):
        if n % t == 0:
            best = t
    assert best is not None, (n, cap)
    return best


def _rows(tm, c, j=0):
    return pl.BlockSpec((tm, c), lambda m: (m, j))


def _prev_halo(tm, c):
    return pl.BlockSpec((HALO, c), lambda m: (jnp.maximum(m * (tm // HALO) - 1, 0), 0))


def _next_halo(tm, c, t_total):
    last = t_total // HALO - 1
    return pl.BlockSpec((HALO, c), lambda m: (jnp.minimum((m + 1) * (tm // HALO), last), 0))


def _resident(shape):
    nd = len(shape)
    return pl.BlockSpec(shape, lambda *_: (0,) * nd, pipeline_mode=pl.Buffered(1))


def _acc_spec(c):
    return pl.BlockSpec((SUBLANES, c), lambda *_: (0, 0))


def _shift_down(u, halo, k):
    ext = jnp.concatenate([halo, u], axis=0)
    return pltpu.roll(ext, k, 0)[HALO:, :]


def _shift_up(u, halo, k):
    ext = jnp.concatenate([u, halo], axis=0)
    return pltpu.roll(ext, ext.shape[0] - k, 0)[: u.shape[0], :]


def _stack_rows(rows, c):
    idx = lax.broadcasted_iota(jnp.int32, (SUBLANES, c), 0)
    out = jnp.zeros((SUBLANES, c), F32)
    for i, r in enumerate(rows):
        out = out + jnp.where(idx == i, r, 0.0)
    return out


def _colsum(v):
    return jnp.sum(v, axis=0, keepdims=True)


def _sigmoid(v):
    return 1.0 / (1.0 + jnp.exp(-v))


def _rms_fwd(xv, g):
    r = lax.rsqrt(jnp.mean(xv * xv, axis=-1, keepdims=True) + RMS_EPS)
    return xv * r * g, r


def _rms_bwd(xv, g, dy):
    r = lax.rsqrt(jnp.mean(xv * xv, axis=-1, keepdims=True) + RMS_EPS)
    xn = xv * r
    dxn = dy * g
    dx = r * (dxn - xn * jnp.mean(dxn * xn, axis=-1, keepdims=True))
    return dx, dy * xn


def _dot(a, b):
    return jnp.dot(a, b, preferred_element_type=F32)


def _dot_nt(a, b):
    return lax.dot_general(a, b, (((1,), (1,)), ((), ())), preferred_element_type=F32)


def _dot_tn(a, b):
    return lax.dot_general(a, b, (((0,), (0,)), ((), ())), preferred_element_type=F32)


def _perm(dil, n, inverse=False):
    i = lax.broadcasted_iota(jnp.int32, (n, n), 0)
    j = lax.broadcasted_iota(jnp.int32, (n, n), 1)
    if inverse:
        i, j = j, i
    per = n // dil
    return (j == (i % per) * dil + i // per).astype(BF16)


def _permute_rows(pm, v):
    if v.dtype == BF16:
        return _dot(pm, v).astype(BF16)
    h1 = v.astype(BF16)
    r1 = v - h1.astype(F32)
    h2 = r1.astype(BF16)
    h3 = (r1 - h2.astype(F32)).astype(BF16)
    return _dot(pm, h1) + _dot(pm, h2) + _dot(pm, h3)


def _stream_view(a, dil):
    t, c = a.shape
    return a.reshape(dil, t // dil, c)


def _stream_spec(dil, tm, c):
    return pl.BlockSpec((dil, tm // dil, c), lambda m: (0, m, 0))


def _load_streams(ref, dil, tm):
    v = ref[...].reshape(tm, ref.shape[-1])
    return v if dil == 1 else _permute_rows(_perm(dil, tm, inverse=True), v)


def _store_streams(ref, dil, tm, v):
    if dil > 1:
        v = _permute_rows(_perm(dil, tm), v)
    ref[...] = v.reshape(ref.shape).astype(ref.dtype)


ANY = pl.BlockSpec(memory_space=pl.ANY)


def _mesh_pos():
    return lax.axis_index("x"), lax.axis_index("y"), lax.axis_index("c")


def _dev_index(px, py, pc):
    return 4 * px + 2 * py + pc


class _Exchange:
    def __init__(self, mode, arrays):
        self.mode, self.arrays = mode, list(arrays)
        n = len(self.arrays)
        if mode == "gather":
            self.out_shape = [jax.ShapeDtypeStruct((N_DEV,) + a.shape, a.dtype) for a in self.arrays]
        else:
            self.out_shape = [jax.ShapeDtypeStruct(a.shape, a.dtype) for a in self.arrays]
        self.scratch = [pltpu.SemaphoreType.DMA((n, N_DEV - 1)), pltpu.SemaphoreType.DMA((n, N_DEV - 1)),
                        pltpu.SemaphoreType.DMA((n,))]

    def _peers(self):
        x, y, c = _mesh_pos()
        flips = [(kx, ky, kc) for kx in (0, 1) for ky in (0, 1) for kc in (0, 1)][1:]
        peers = [(1 - x if kx else x, 1 - y if ky else y, 1 - c if kc else c) for kx, ky, kc in flips]
        return _dev_index(x, y, c), peers

    def _copy(self, ins, outs, sems, i, k, peer, me, sending):
        src = ins[i] if self.mode == "gather" else ins[i].at[_dev_index(*peer)]
        dst = outs[i].at[me if sending else _dev_index(*peer)]
        return pltpu.make_async_remote_copy(src_ref=src, dst_ref=dst, send_sem=sems[0].at[i, k],
                                            recv_sem=sems[1].at[i, k], device_id=peer, device_id_type=MESH)

    def _own(self, ins, outs, sems, i, me):
        return pltpu.make_async_copy(ins[i], outs[i].at[me], sems[2].at[i])

    def start(self, ins, outs, sems):
        me, peers = self._peers()
        for i in range(len(ins)):
            if self.mode == "gather":
                self._own(ins, outs, sems, i, me).start()
            for k, peer in enumerate(peers):
                self._copy(ins, outs, sems, i, k, peer, me, True).start()

    def wait(self, ins, outs, sems):
        me, peers = self._peers()
        for i in range(len(ins)):
            for k, peer in enumerate(peers):
                self._copy(ins, outs, sems, i, k, peer, me, False).wait_recv()
            for k, peer in enumerate(peers):
                self._copy(ins, outs, sems, i, k, peer, me, True).wait_send()
            if self.mode == "gather":
                self._own(ins, outs, sems, i, me).wait()


def _call(body, *, name, grid, in_specs, out_specs, out_shape, args, semantics, carry=None):
    if carry is None:
        return pl.pallas_call(body, name=name, grid=grid, in_specs=in_specs, out_specs=out_specs,
                              out_shape=out_shape, compiler_params=_params(*semantics))(*args)
    n_in, n_out, n_x = len(in_specs), len(out_specs), len(carry.arrays)

    def carried(*refs):
        ins, x_ins = refs[:n_in], refs[n_in:n_in + n_x]
        outs = refs[n_in + n_x:n_in + n_x + n_out]
        x_outs = refs[n_in + n_x + n_out:n_in + 2 * n_x + n_out]
        sems = refs[n_in + 2 * n_x + n_out:]
        first = functools.reduce(jnp.logical_and, [pl.program_id(a) == 0 for a in range(len(grid))])
        last = functools.reduce(jnp.logical_and, [pl.program_id(a) == grid[a] - 1 for a in range(len(grid))])

        @pl.when(first)
        def _():
            carry.start(x_ins, x_outs, sems)

        body(*ins, *outs)

        @pl.when(last)
        def _():
            carry.wait(x_ins, x_outs, sems)

    res = pl.pallas_call(
        carried, name=name, grid=grid, in_specs=list(in_specs) + [ANY] * n_x,
        out_specs=list(out_specs) + [ANY] * n_x, out_shape=list(out_shape) + carry.out_shape,
        scratch_shapes=carry.scratch, compiler_params=_params(*["arbitrary"] * len(grid)),
    )(*args, *carry.arrays)
    return list(res[:n_out]), list(res[n_out:])


def _up_proj(x, g, wt, carry=None):
    t, d = x.shape
    n = wt.shape[0]
    tm = min(256, t)

    def body(x_ref, g_ref, wt_ref, h_ref, o_ref):
        h = _rms_fwd(x_ref[...], g_ref[...])[0].astype(BF16)
        h_ref[...] = h
        o_ref[...] = _dot_nt(h, wt_ref[...]).astype(BF16)

    return _call(
        body, name="up_proj", grid=(t // tm,),
        in_specs=[_rows(tm, d), _resident((1, d)), _resident((n, d))],
        out_specs=[_rows(tm, d), _rows(tm, n)],
        out_shape=[jax.ShapeDtypeStruct((t, d), BF16), jax.ShapeDtypeStruct((t, n), BF16)],
        args=(x, g, wt), semantics=("parallel",), carry=carry)


def _in_proj(x, g, wt, cw, carry=None):
    t, d = x.shape
    n = wt.shape[0]
    tm = min(256, t)
    qkv0 = 3 * cw

    def body(x_ref, g_ref, wt_ref, h_ref, abcv_ref, gates_ref, *s_refs):
        h = _rms_fwd(x_ref[...], g_ref[...])[0].astype(BF16)
        h_ref[...] = h
        abcv_ref[...] = _dot_nt(h, wt_ref[0:qkv0, :]).astype(BF16)
        gates_ref[...] = _dot_nt(h, wt_ref[qkv0 + 3 * ATTN_W:n, :]).astype(BF16)
        for gi, s_ref in enumerate(s_refs):
            cols = [_dot_nt(h, wt_ref[qkv0 + j * ATTN_W + gi * GROUP_W:qkv0 + j * ATTN_W + (gi + 1) * GROUP_W, :])
                    for j in range(3)]
            _store_streams(s_ref, DILATIONS[gi], tm, jnp.concatenate(cols, axis=1).astype(BF16))

    return _call(
        body, name="in_proj", grid=(t // tm,),
        in_specs=[_rows(tm, d), _resident((1, d)), _resident((n, d))],
        out_specs=[_rows(tm, d), _rows(tm, qkv0), _rows(tm, 2 * d)]
        + [_stream_spec(dil, tm, 3 * GROUP_W) for dil in DILATIONS],
        out_shape=[jax.ShapeDtypeStruct((t, d), BF16), jax.ShapeDtypeStruct((t, qkv0), BF16),
                   jax.ShapeDtypeStruct((t, 2 * d), BF16)]
        + [jax.ShapeDtypeStruct((dil, t // dil, 3 * GROUP_W), BF16) for dil in DILATIONS],
        args=(x, g, wt), semantics=("parallel",), carry=carry)


def _head_masks():
    lane = lax.broadcasted_iota(jnp.int32, (1, GROUP_W), 1)
    return lane, [(lane // HEAD_DIM) == h for h in range(HEADS_PER_GROUP)]


def _qkv_block(col, shift=0, nb=None):
    if shift == 0:
        return pl.BlockSpec((QBLK, GROUP_W), lambda b: (b, col))
    return pl.BlockSpec((QBLK, GROUP_W), lambda b: (jnp.clip(b + shift, 0, nb - 1), col))


def _band_masks(has_other):
    row = lax.broadcasted_iota(jnp.int32, (QBLK, QBLK), 0)
    col = lax.broadcasted_iota(jnp.int32, (QBLK, QBLK), 1)
    return col <= row, (col >= row) & has_other


def _attn_fwd(s, dil, carry=None):
    t = s.shape[0] * s.shape[1]
    nb = t // QBLK
    per_stream = nb // dil

    def body(q_ref, kc_ref, kp_ref, vc_ref, vp_ref, o_ref, lse_ref):
        b = pl.program_id(0)
        has_prev = lax.rem(b, per_stream) != 0
        mask_c, mask_p = _band_masks(has_prev)
        _, heads = _head_masks()
        q, kc, kp, vc, vp = q_ref[...], kc_ref[...], kp_ref[...], vc_ref[...], vp_ref[...]
        o_acc = jnp.zeros((QBLK, GROUP_W), F32)
        lse_acc = jnp.zeros((QBLK, GROUP_W), F32)
        for hm in heads:
            qh = jnp.where(hm, q, jnp.zeros_like(q))
            sc = jnp.where(mask_c, _dot_nt(qh, kc) * ATTN_SCALE, NEG_INF)
            sp = jnp.where(mask_p, _dot_nt(qh, kp) * ATTN_SCALE, NEG_INF)
            mx = jnp.maximum(jnp.max(sc, axis=1, keepdims=True), jnp.max(sp, axis=1, keepdims=True))
            pc = jnp.exp(sc - mx)
            pp = jnp.exp(sp - mx)
            den = jnp.sum(pc, axis=1, keepdims=True) + jnp.sum(pp, axis=1, keepdims=True)
            oh = _dot(pc.astype(BF16), vc) + _dot(pp.astype(BF16), vp)
            o_acc = jnp.where(hm, oh / den, o_acc)
            lse_acc = jnp.where(hm, mx + jnp.log(den), lse_acc)
        o_ref[...] = o_acc.astype(BF16)
        lse_ref[...] = lse_acc

    sv = s.reshape(t, 3 * GROUP_W)
    return _call(
        body, name=f"attn_fwd_d{dil}", grid=(nb,),
        in_specs=[_qkv_block(0), _qkv_block(1), _qkv_block(1, -1, nb), _qkv_block(2), _qkv_block(2, -1, nb)],
        out_specs=[_qkv_block(0), _qkv_block(0)],
        out_shape=[jax.ShapeDtypeStruct((t, GROUP_W), BF16), jax.ShapeDtypeStruct((t, GROUP_W), F32)],
        args=(sv, sv, sv, sv, sv), semantics=("parallel",), carry=carry)


def _group_softmax(parts):
    mx = jnp.maximum(jnp.maximum(parts[0], parts[1]), parts[2])
    es = [jnp.exp(p - mx) for p in parts]
    den = es[0] + es[1] + es[2]
    return [e / den for e in es]


def _mixer_out(x, abcv, gates, os, lses, conv_w, conv_b, b_gate, w_pa, w_pb, w_o):
    t, d = x.shape
    cw = conv_w.shape[1]
    tm = min(256, t)

    def body(x_ref, abcv_ref, halo_ref, gates_ref, o0_ref, o1_ref, o2_ref, l0_ref, l1_ref, l2_ref, cw_ref, cb_ref,
             bg_ref, wpa_ref, wpb_ref, wo_ref, x1_ref, ya_ref, yb_ref, yap_ref, ybp_ref, mg_ref):
        m = pl.program_id(0)
        ab = abcv_ref[:, 0:cw].astype(F32)
        u = abcv_ref[:, cw:2 * cw].astype(F32) * abcv_ref[:, 2 * cw:3 * cw].astype(F32)
        hu = halo_ref[:, cw:2 * cw].astype(F32) * halo_ref[:, 2 * cw:3 * cw].astype(F32)
        hu = jnp.where(m > 0, hu, 0.0)
        cv = (cw_ref[0:1, :] * _shift_down(u, hu, 2) + cw_ref[1:2, :] * _shift_down(u, hu, 1)
              + cw_ref[2:3, :] * u + cb_ref[...])
        ya = (ab * cv).astype(BF16)
        ya_ref[...] = ya
        alphas = _group_softmax([_load_streams(r, dil, tm) for r, dil in zip((l0_ref, l1_ref, l2_ref), DILATIONS)])
        for i, (o_ref, dil) in enumerate(zip((o0_ref, o1_ref, o2_ref), DILATIONS)):
            sl = slice(i * GROUP_W, (i + 1) * GROUP_W)
            yb_ref[:, sl] = (alphas[i] * _load_streams(o_ref, dil, tm).astype(F32)).astype(BF16)
        yap = _dot_nt(ya, wpa_ref[...])
        ybp = _dot_nt(yb_ref[...], wpb_ref[...])
        yap_ref[...] = yap.astype(BF16)
        ybp_ref[...] = ybp.astype(BF16)
        sa = _sigmoid(gates_ref[:, 0:d].astype(F32) + bg_ref[0:1, :])
        sb = _sigmoid(gates_ref[:, d:2 * d].astype(F32) + bg_ref[1:2, :])
        merged = (sa * yap + sb * ybp).astype(BF16)
        mg_ref[...] = merged
        x1_ref[...] = x_ref[...] + _dot(merged, wo_ref[...])

    return pl.pallas_call(
        body, name="mixer_out", grid=(t // tm,),
        in_specs=[_rows(tm, d), _rows(tm, 3 * cw), _prev_halo(tm, 3 * cw), _rows(tm, 2 * d)]
        + [_stream_spec(dil, tm, GROUP_W) for dil in DILATIONS] * 2
        + [_resident((3, cw)), _resident((1, cw)), _resident((2, d)),
           _resident((d, cw)), _resident((d, ATTN_W)), _resident((d, d))],
        out_specs=[_rows(tm, d), _rows(tm, cw), _rows(tm, ATTN_W), _rows(tm, d), _rows(tm, d), _rows(tm, d)],
        out_shape=[jax.ShapeDtypeStruct((t, d), F32), jax.ShapeDtypeStruct((t, cw), BF16),
                   jax.ShapeDtypeStruct((t, ATTN_W), BF16), jax.ShapeDtypeStruct((t, d), BF16),
                   jax.ShapeDtypeStruct((t, d), BF16), jax.ShapeDtypeStruct((t, d), BF16)],
        compiler_params=_params("parallel"),
    )(x, abcv, abcv, gates, *[_stream_view(a, dil) for a, dil in zip(os, DILATIONS)],
      *[_stream_view(a, dil) for a, dil in zip(lses, DILATIONS)], conv_w, conv_b, b_gate, w_pa, w_pb, w_o)


def _ffn_conv(p_ref, halo_ref, w_ref, b_ref, m, c0, wd):
    p = p_ref[:, c0:c0 + wd].astype(F32)
    hp = jnp.where(m > 0, halo_ref[:, c0:c0 + wd].astype(F32), 0.0)
    return (w_ref[0:1, c0:c0 + wd] * _shift_down(p, hp, 2) + w_ref[1:2, c0:c0 + wd] * _shift_down(p, hp, 1)
            + w_ref[2:3, c0:c0 + wd] * p + b_ref[:, c0:c0 + wd])


def _ffn_loss(x1, up_pre, target, conv_w, conv_b, w_d, g_f):
    t, d = x1.shape
    dff = w_d.shape[0]
    tm = min(256, t)
    ck = _pick_tile(dff, 1408)

    def body(x1_ref, up_ref, halo_ref, tg_ref, cw_ref, cb_ref, wd_ref, gf_ref, act_ref, conv_ref, dx2_ref, acc_ref,
             loss_ref):
        m = pl.program_id(0)

        @pl.when(m == 0)
        def _():
            acc_ref[...] = jnp.zeros_like(acc_ref)
            loss_ref[...] = jnp.zeros_like(loss_ref)

        x2 = x1_ref[...]
        for c0 in range(0, dff, ck):
            gate = _ffn_conv(up_ref, halo_ref, cw_ref, cb_ref, m, c0, ck)
            val = _ffn_conv(up_ref, halo_ref, cw_ref, cb_ref, m, dff + c0, ck)
            conv_ref[:, c0:c0 + ck] = gate.astype(BF16)
            conv_ref[:, dff + c0:dff + c0 + ck] = val.astype(BF16)
            act = (gate * _sigmoid(gate) * val).astype(BF16)
            act_ref[:, c0:c0 + ck] = act
            x2 = x2 + _dot(act, wd_ref[c0:c0 + ck, :])
        y, _ = _rms_fwd(x2, gf_ref[...])
        diff = y - tg_ref[...]
        loss_ref[...] += 0.5 * jnp.sum(jnp.mean(diff * diff, axis=-1, keepdims=True))
        dx2, dg = _rms_bwd(x2, gf_ref[...], diff * (1.0 / d))
        dx2_ref[...] = dx2
        acc_ref[...] += _stack_rows([_colsum(dg)], d)

    return pl.pallas_call(
        body, name="ffn_loss", grid=(t // tm,),
        in_specs=[_rows(tm, d), _rows(tm, 2 * dff), _prev_halo(tm, 2 * dff), _rows(tm, d),
                  _resident((3, 2 * dff)), _resident((1, 2 * dff)), _resident((dff, d)), _resident((1, d))],
        out_specs=[_rows(tm, dff), _rows(tm, 2 * dff), _rows(tm, d), _acc_spec(d), _acc_spec(LANES)],
        out_shape=[jax.ShapeDtypeStruct((t, dff), BF16), jax.ShapeDtypeStruct((t, 2 * dff), BF16),
                   jax.ShapeDtypeStruct((t, d), F32), jax.ShapeDtypeStruct((SUBLANES, d), F32),
                   jax.ShapeDtypeStruct((SUBLANES, LANES), F32)],
        compiler_params=_params("arbitrary"),
    )(x1, up_pre, up_pre, target, conv_w, conv_b, w_d, g_f)


def _ffn_act_bwd(dx2, conv, w_d):
    t, d = dx2.shape
    dff = w_d.shape[0]
    tm = min(256, t)
    ck = _pick_tile(dff, 1408)

    def body(dx2_ref, conv_ref, wd_ref, dup_ref, acc_ref):
        m = pl.program_id(0)

        @pl.when(m == 0)
        def _():
            acc_ref[...] = jnp.zeros_like(acc_ref)

        dx2v = dx2_ref[...].astype(BF16)
        for c0 in range(0, dff, ck):
            dact = _dot_nt(dx2v, wd_ref[c0:c0 + ck, :])
            gate = conv_ref[:, c0:c0 + ck].astype(F32)
            val = conv_ref[:, dff + c0:dff + c0 + ck].astype(F32)
            sg = _sigmoid(gate)
            dval = dact * gate * sg
            dgate = dact * val * sg * (1.0 + gate * (1.0 - sg))
            dup_ref[:, c0:c0 + ck] = dgate.astype(BF16)
            dup_ref[:, dff + c0:dff + c0 + ck] = dval.astype(BF16)
            acc_ref[:, c0:c0 + ck] += _stack_rows([_colsum(dgate)], ck)
            acc_ref[:, dff + c0:dff + c0 + ck] += _stack_rows([_colsum(dval)], ck)

    return pl.pallas_call(
        body, name="ffn_act_bwd", grid=(t // tm,),
        in_specs=[_rows(tm, d), _rows(tm, 2 * dff), _resident((dff, d))],
        out_specs=[_rows(tm, 2 * dff), _acc_spec(2 * dff)],
        out_shape=[jax.ShapeDtypeStruct((t, 2 * dff), BF16), jax.ShapeDtypeStruct((SUBLANES, 2 * dff), F32)],
        compiler_params=_params("arbitrary"),
    )(dx2, conv, w_d)


def _ffn_up_bwd(dup, up_pre, x1, dx2, conv_w, w_u, g2, carry=None):
    t, d = x1.shape
    n = dup.shape[1]
    tm = min(256, t)
    ck = _pick_tile(n, 1408)
    last = t // tm - 1

    def body(dup_ref, nxt_ref, up_ref, x1_ref, dx2_ref, cw_ref, wu_ref, g2_ref, dpre_ref, dx1_ref, acc_ref, accw_ref):
        m = pl.program_id(0)

        @pl.when(m == 0)
        def _():
            acc_ref[...] = jnp.zeros_like(acc_ref)
            accw_ref[...] = jnp.zeros_like(accw_ref)

        dh = jnp.zeros((tm, d), F32)
        for c0 in range(0, n, ck):
            du = dup_ref[:, c0:c0 + ck].astype(F32)
            hn = jnp.where(m < last, nxt_ref[:, c0:c0 + ck].astype(F32), 0.0)
            du1 = _shift_up(du, hn, 1)
            du2 = _shift_up(du, hn, 2)
            dpre = (cw_ref[2:3, c0:c0 + ck] * du + cw_ref[1:2, c0:c0 + ck] * du1
                    + cw_ref[0:1, c0:c0 + ck] * du2).astype(BF16)
            dpre_ref[:, c0:c0 + ck] = dpre
            dh = dh + _dot(dpre, wu_ref[c0:c0 + ck, :])
            p = up_ref[:, c0:c0 + ck].astype(F32)
            accw_ref[:, c0:c0 + ck] += _stack_rows([_colsum(du2 * p), _colsum(du1 * p), _colsum(du * p)], ck)
        dx, dg = _rms_bwd(x1_ref[...], g2_ref[...], dh)
        dx1_ref[...] = dx2_ref[...] + dx
        acc_ref[...] += _stack_rows([_colsum(dg)], d)

    return _call(
        body, name="ffn_up_bwd", grid=(t // tm,),
        in_specs=[_rows(tm, n), _next_halo(tm, n, t), _rows(tm, n), _rows(tm, d), _rows(tm, d), _resident((3, n)),
                  _resident((n, d)), _resident((1, d))],
        out_specs=[_rows(tm, n), _rows(tm, d), _acc_spec(d), _acc_spec(n)],
        out_shape=[jax.ShapeDtypeStruct((t, n), BF16), jax.ShapeDtypeStruct((t, d), F32),
                   jax.ShapeDtypeStruct((SUBLANES, d), F32), jax.ShapeDtypeStruct((SUBLANES, n), F32)],
        args=(dup, dup, up_pre, x1, dx2, conv_w, w_u, g2), semantics=("arbitrary",), carry=carry)


def _tn_matmul(a, b, name):
    t, mdim = a.shape
    n = b.shape[1]
    tk = min(512, t)
    tmm = _pick_tile(mdim, 1408)
    tn = _pick_tile(n, 1536)

    def body(a_ref, b_ref, o_ref, acc_ref):
        k = pl.program_id(2)

        @pl.when(k == 0)
        def _():
            acc_ref[...] = jnp.zeros_like(acc_ref)

        acc_ref[...] += _dot_tn(a_ref[...].astype(BF16), b_ref[...].astype(BF16))

        @pl.when(k == t // tk - 1)
        def _():
            o_ref[...] = acc_ref[...].astype(BF16)

    return pl.pallas_call(
        body, name=name, grid=(mdim // tmm, n // tn, t // tk),
        in_specs=[pl.BlockSpec((tk, tmm), lambda i, j, k: (k, i)), pl.BlockSpec((tk, tn), lambda i, j, k: (k, j))],
        out_specs=pl.BlockSpec((tmm, tn), lambda i, j, k: (i, j)),
        out_shape=jax.ShapeDtypeStruct((mdim, n), BF16),
        scratch_shapes=[pltpu.VMEM((tmm, tn), F32)],
        compiler_params=_params("parallel", "parallel", "arbitrary"),
    )(a, b)


def _mixer_bwd(dx1, gates, yap, ybp, os, lses, b_gate, w_o, w_pa, w_pb):
    t, d = dx1.shape
    cw = w_pa.shape[1]
    tm = min(256, t)

    def body(dx1_ref, gates_ref, yap_ref, ybp_ref, o0_ref, o1_ref, o2_ref, l0_ref, l1_ref, l2_ref, bg_ref, wo_ref,
             wpa_ref, wpb_ref, dgates_ref, dyap_ref, dybp_ref, dya_ref, do0_ref, do1_ref, do2_ref, dl0_ref, dl1_ref,
             dl2_ref, acc_ref):
        m = pl.program_id(0)

        @pl.when(m == 0)
        def _():
            acc_ref[...] = jnp.zeros_like(acc_ref)

        dmg = _dot_nt(dx1_ref[...].astype(BF16), wo_ref[...])
        sa = _sigmoid(gates_ref[:, 0:d].astype(F32) + bg_ref[0:1, :])
        sb = _sigmoid(gates_ref[:, d:2 * d].astype(F32) + bg_ref[1:2, :])
        dyap = (dmg * sa).astype(BF16)
        dybp = (dmg * sb).astype(BF16)
        dga = dmg * yap_ref[...].astype(F32) * sa * (1.0 - sa)
        dgb = dmg * ybp_ref[...].astype(F32) * sb * (1.0 - sb)
        dyap_ref[...] = dyap
        dybp_ref[...] = dybp
        dgates_ref[:, 0:d] = dga.astype(BF16)
        dgates_ref[:, d:2 * d] = dgb.astype(BF16)
        acc_ref[...] += _stack_rows([_colsum(dga), _colsum(dgb)], d)
        dya_ref[...] = _dot(dyap, wpa_ref[...]).astype(BF16)
        dyb = _dot(dybp, wpb_ref[...])

        ri = lax.broadcasted_iota(jnp.int32, (GROUP_W, GROUP_W), 0) // HEAD_DIM
        ci = lax.broadcasted_iota(jnp.int32, (GROUP_W, GROUP_W), 1) // HEAD_DIM
        same_head = (ri == ci).astype(BF16)
        alphas = _group_softmax([_load_streams(r, dil, tm) for r, dil in zip((l0_ref, l1_ref, l2_ref), DILATIONS)])
        dtot = jnp.zeros((tm, GROUP_W), F32)
        for i, (o_ref, do_ref, dil) in enumerate(zip((o0_ref, o1_ref, o2_ref), (do0_ref, do1_ref, do2_ref), DILATIONS)):
            dov = alphas[i] * dyb[:, i * GROUP_W:(i + 1) * GROUP_W]
            _store_streams(do_ref, dil, tm, dov.astype(BF16))
            prod = dov * _load_streams(o_ref, dil, tm).astype(F32)
            hi = prod.astype(BF16)
            lo = (prod - hi.astype(F32)).astype(BF16)
            dtot = dtot + _dot(hi, same_head) + _dot(lo, same_head)
        for alpha, dl_ref, dil in zip(alphas, (dl0_ref, dl1_ref, dl2_ref), DILATIONS):
            _store_streams(dl_ref, dil, tm, alpha * dtot)

    streams = [_stream_spec(dil, tm, GROUP_W) for dil in DILATIONS]
    res = pl.pallas_call(
        body, name="mixer_bwd", grid=(t // tm,),
        in_specs=[_rows(tm, d), _rows(tm, 2 * d), _rows(tm, d), _rows(tm, d)] + streams * 2
        + [_resident((2, d)), _resident((d, d)), _resident((d, cw)), _resident((d, ATTN_W))],
        out_specs=[_rows(tm, 2 * d), _rows(tm, d), _rows(tm, d), _rows(tm, cw)] + streams * 2 + [_acc_spec(d)],
        out_shape=[jax.ShapeDtypeStruct((t, 2 * d), BF16), jax.ShapeDtypeStruct((t, d), BF16),
                   jax.ShapeDtypeStruct((t, d), BF16), jax.ShapeDtypeStruct((t, cw), BF16)]
        + [jax.ShapeDtypeStruct((dil, t // dil, GROUP_W), BF16) for dil in DILATIONS]
        + [jax.ShapeDtypeStruct((dil, t // dil, GROUP_W), F32) for dil in DILATIONS]
        + [jax.ShapeDtypeStruct((SUBLANES, d), F32)],
        compiler_params=_params("arbitrary"),
    )(dx1, gates, yap, ybp, *[_stream_view(a, dil) for a, dil in zip(os, DILATIONS)],
      *[_stream_view(a, dil) for a, dil in zip(lses, DILATIONS)], b_gate, w_o, w_pa, w_pb)
    dgates, dyap, dybp, dya = res[:4]
    dos = [a.reshape(t, GROUP_W) for a in res[4:7]]
    dls = [a.reshape(t, GROUP_W) for a in res[7:10]]
    return dgates, dyap, dybp, dya, dos, dls, res[10]


def _attn_bwd(s, do, lse, dl, dil, carry=None):
    t = s.shape[0] * s.shape[1]
    nb = t // QBLK
    per_stream = nb // dil

    def body(q_ref, qn_ref, kc_ref, kp_ref, vc_ref, vp_ref, do_ref, don_ref, lse_ref, lsen_ref, dl_ref, dln_ref,
             ds_ref):
        b = pl.program_id(0)
        has_prev = lax.rem(b, per_stream) != 0
        has_next = lax.rem(b + 1, per_stream) != 0
        mask_c, mask_p = _band_masks(has_prev)
        _, mask_n = _band_masks(has_next)
        lane, heads = _head_masks()
        q, qn, kc, kp, vc, vp = q_ref[...], qn_ref[...], kc_ref[...], kp_ref[...], vc_ref[...], vp_ref[...]
        do, don = do_ref[...], don_ref[...]
        lse, lsen, dl, dln = lse_ref[...], lsen_ref[...], dl_ref[...], dln_ref[...]
        dq_acc = jnp.zeros((QBLK, GROUP_W), F32)
        dk_acc = jnp.zeros((QBLK, GROUP_W), F32)
        dv_acc = jnp.zeros((QBLK, GROUP_W), F32)

        def pair(qh, doh, k, v, mask, lse_h, dl_h):
            s = jnp.where(mask, _dot_nt(qh, k) * ATTN_SCALE, NEG_INF)
            p = jnp.exp(s - lse_h)
            ds = p * (_dot_nt(doh, v) - dl_h)
            return p.astype(BF16), ds.astype(BF16)

        for h, hm in enumerate(heads):
            first = lane == h * HEAD_DIM

            def col(v):
                return jnp.sum(jnp.where(first, v, 0.0), axis=1, keepdims=True)

            zero = jnp.zeros_like(q)
            qh, qnh = jnp.where(hm, q, zero), jnp.where(hm, qn, zero)
            doh, donh = jnp.where(hm, do, zero), jnp.where(hm, don, zero)
            lse_h, dl_h, lsen_h, dln_h = col(lse), col(dl), col(lsen), col(dln)
            p_c, ds_c = pair(qh, doh, kc, vc, mask_c, lse_h, dl_h)
            _, ds_p = pair(qh, doh, kp, vp, mask_p, lse_h, dl_h)
            p_n, ds_n = pair(qnh, donh, kc, vc, mask_n, lsen_h, dln_h)
            dq_h = _dot(ds_c, kc) + _dot(ds_p, kp)
            dk_h = _dot_tn(ds_c, q) + _dot_tn(ds_n, qn)
            dv_h = _dot_tn(p_c, do) + _dot_tn(p_n, don)
            dq_acc = jnp.where(hm, dq_h, dq_acc)
            dk_acc = jnp.where(hm, dk_h, dk_acc)
            dv_acc = jnp.where(hm, dv_h, dv_acc)
        ds_ref[:, 0:GROUP_W] = (dq_acc * ATTN_SCALE).astype(BF16)
        ds_ref[:, GROUP_W:2 * GROUP_W] = (dk_acc * ATTN_SCALE).astype(BF16)
        ds_ref[:, 2 * GROUP_W:3 * GROUP_W] = dv_acc.astype(BF16)

    sv = s.reshape(t, 3 * GROUP_W)
    cur, nxt = _qkv_block(0), _qkv_block(0, 1, nb)
    return _call(
        body, name=f"attn_bwd_d{dil}", grid=(nb,),
        in_specs=[cur, nxt, _qkv_block(1), _qkv_block(1, -1, nb), _qkv_block(2), _qkv_block(2, -1, nb),
                  cur, nxt, cur, nxt, cur, nxt],
        out_specs=[pl.BlockSpec((QBLK, 3 * GROUP_W), lambda b: (b, 0))],
        out_shape=[jax.ShapeDtypeStruct((t, 3 * GROUP_W), BF16)],
        args=(sv, sv, sv, sv, sv, sv, do, do, lse, lse, dl, dl), semantics=("parallel",), carry=carry)


def _conv_mixer_bwd(abcv, dya, conv_w, conv_b):
    t = abcv.shape[0]
    cw = conv_w.shape[1]
    tm = min(256, t)
    last = t // tm - 1

    def body(a_ref, ap_ref, an_ref, dya_ref, dyan_ref, cw_ref, cb_ref, d_ref, acc_ref):
        m = pl.program_id(0)

        @pl.when(m == 0)
        def _():
            acc_ref[...] = jnp.zeros_like(acc_ref)

        ab = a_ref[:, 0:cw].astype(F32)
        ac = a_ref[:, cw:2 * cw].astype(F32)
        av = a_ref[:, 2 * cw:3 * cw].astype(F32)
        u = ac * av
        hu = ap_ref[:, cw:2 * cw].astype(F32) * ap_ref[:, 2 * cw:3 * cw].astype(F32)
        hu = jnp.where(m > 0, hu, 0.0)
        u1 = _shift_down(u, hu, 1)
        u2 = _shift_down(u, hu, 2)
        cv = cw_ref[0:1, :] * u2 + cw_ref[1:2, :] * u1 + cw_ref[2:3, :] * u + cb_ref[...]
        dya_v = dya_ref[...].astype(F32)
        dcv = dya_v * ab
        ndcv = jnp.where(m < last, dyan_ref[...].astype(F32) * an_ref[:, 0:cw].astype(F32), 0.0)
        du = (cw_ref[2:3, :] * dcv + cw_ref[1:2, :] * _shift_up(dcv, ndcv, 1)
              + cw_ref[0:1, :] * _shift_up(dcv, ndcv, 2))
        d_ref[:, 0:cw] = (dya_v * cv).astype(BF16)
        d_ref[:, cw:2 * cw] = (du * av).astype(BF16)
        d_ref[:, 2 * cw:3 * cw] = (du * ac).astype(BF16)
        acc_ref[...] += _stack_rows([_colsum(dcv * u2), _colsum(dcv * u1), _colsum(dcv * u), _colsum(dcv)], cw)

    return pl.pallas_call(
        body, name="conv_mixer_bwd", grid=(t // tm,),
        in_specs=[_rows(tm, 3 * cw), _prev_halo(tm, 3 * cw), _next_halo(tm, 3 * cw, t), _rows(tm, cw),
                  _next_halo(tm, cw, t), _resident((3, cw)), _resident((1, cw))],
        out_specs=[_rows(tm, 3 * cw), _acc_spec(cw)],
        out_shape=[jax.ShapeDtypeStruct((t, 3 * cw), BF16), jax.ShapeDtypeStruct((SUBLANES, cw), F32)],
        compiler_params=_params("arbitrary"),
    )(abcv, abcv, abcv, dya, dya, conv_w, conv_b)


def _in_proj_bwd(x, dx1, dabcv, dss, dgates, w_in, g1, carry=None):
    t, d = x.shape
    qkv0 = dabcv.shape[1]
    n = w_in.shape[0]
    tm = min(256, t)

    def body(x_ref, dx1_ref, da_ref, ds0_ref, ds1_ref, ds2_ref, dg_ref, w_ref, g_ref, dx_ref, acc_ref):
        m = pl.program_id(0)

        @pl.when(m == 0)
        def _():
            acc_ref[...] = jnp.zeros_like(acc_ref)

        dh = _dot(da_ref[...], w_ref[0:qkv0, :]) + _dot(dg_ref[...], w_ref[qkv0 + 3 * ATTN_W:n, :])
        for gi, (ds_ref, dil) in enumerate(zip((ds0_ref, ds1_ref, ds2_ref), DILATIONS)):
            ds = _load_streams(ds_ref, dil, tm)
            for j in range(3):
                c0 = qkv0 + j * ATTN_W + gi * GROUP_W
                dh = dh + _dot(ds[:, j * GROUP_W:(j + 1) * GROUP_W], w_ref[c0:c0 + GROUP_W, :])
        dx, dg = _rms_bwd(x_ref[...], g_ref[...], dh)
        dx_ref[...] = dx1_ref[...] + dx
        acc_ref[...] += _stack_rows([_colsum(dg)], d)

    return _call(
        body, name="in_proj_bwd", grid=(t // tm,),
        in_specs=[_rows(tm, d), _rows(tm, d), _rows(tm, qkv0)]
        + [_stream_spec(dil, tm, 3 * GROUP_W) for dil in DILATIONS]
        + [_rows(tm, 2 * d), _resident((n, d)), _resident((1, d))],
        out_specs=[_rows(tm, d), _acc_spec(d)],
        out_shape=[jax.ShapeDtypeStruct((t, d), F32), jax.ShapeDtypeStruct((SUBLANES, d), F32)],
        args=(x, dx1, dabcv, *[_stream_view(a, dil) for a, dil in zip(dss, DILATIONS)], dgates, w_in, g1),
        semantics=("arbitrary",), carry=carry)


def _dw_in_qkv(ds, h, dil):
    t, d = h.shape
    tk = min(256, t)
    width = 3 * GROUP_W

    def body(ds_ref, h_ref, o_ref, acc_ref):
        k = pl.program_id(0)

        @pl.when(k == 0)
        def _():
            acc_ref[...] = jnp.zeros_like(acc_ref)

        acc_ref[...] += _dot_tn(_load_streams(ds_ref, dil, tk), h_ref[...])

        @pl.when(k == t // tk - 1)
        def _():
            o_ref[...] = acc_ref[...].astype(BF16)

    return pl.pallas_call(
        body, name=f"dw_in_qkv_d{dil}", grid=(t // tk,),
        in_specs=[_stream_spec(dil, tk, width), _rows(tk, d)],
        out_specs=pl.BlockSpec((width, d), lambda k: (0, 0)),
        out_shape=jax.ShapeDtypeStruct((width, d), BF16),
        scratch_shapes=[pltpu.VMEM((width, d), F32)],
        compiler_params=_params("arbitrary"),
    )(_stream_view(ds, dil), h)


def _local_step(x, target, p, late):
    cw = p["conv_a_w"].shape[1]
    (h, abcv, gates, *ss), (g_up,) = _in_proj(x, p["norm_mix_g"], p["w_in"], cw,
                                              carry=_Exchange("gather", [late["w_up"]]))
    w_up = _full_from_gathered(g_up)
    mid = ("w_proj_a", "w_proj_b", "w_out")
    (o0, lse0), g_mid = _attn_fwd(ss[0], DILATIONS[0], carry=_Exchange("gather", [late[n] for n in mid]))
    w_pa, w_pb, w_out = [_full_from_gathered(g) for g in g_mid]
    os, lses = zip((o0, lse0), *[_attn_fwd(s, dil) for s, dil in zip(ss[1:], DILATIONS[1:])])
    x1, ya, yb, yap, ybp, merged = _mixer_out(x, abcv, gates, os, lses, p["conv_a_w"], p["conv_a_b"], p["b_gate"],
                                              w_pa, w_pb, w_out)
    (h2, up_pre), (g_down,) = _up_proj(x1, p["norm_ffn_g"], w_up, carry=_Exchange("gather", [late["w_down"]]))
    w_down = _full_from_gathered(g_down)
    act, conv, dx2, acc_gf, loss = _ffn_loss(x1, up_pre, target, p["ffn_conv_w"], p["ffn_conv_b"], w_down,
                                       p["final_norm_g"])

    parts, got = {}, {}
    dup, acc_fb = _ffn_act_bwd(dx2, conv, w_down)
    parts["w_down"] = _by_destination(_tn_matmul(act, dx2, "dw_down"))
    (dpre, dx1, acc_g2, acc_fw), (got["w_down"],) = _ffn_up_bwd(dup, up_pre, x1, dx2, p["ffn_conv_w"], w_up,
                                                                p["norm_ffn_g"],
                                                                carry=_Exchange("scatter", [parts["w_down"]]))
    parts["w_up"] = _by_destination(_tn_matmul(dpre, h2, "dw_up"))
    dgates, dyap, dybp, dya, dos, dls, acc_bg = _mixer_bwd(dx1, gates, yap, ybp, os, lses, p["b_gate"], w_out,
                                                           w_pa, w_pb)
    parts["w_out"] = _by_destination(_tn_matmul(merged, dx1, "dw_out"))
    parts["w_proj_a"] = _by_destination(_tn_matmul(dyap, ya, "dw_proj_a"))
    parts["w_proj_b"] = _by_destination(_tn_matmul(dybp, yb, "dw_proj_b"))
    riders = (("w_up",), ("w_out", "w_proj_a", "w_proj_b"), ())
    dss = []
    for s, do, lse, dl, dil, names in zip(ss, dos, lses, dls, DILATIONS, riders):
        if names:
            (ds,), received = _attn_bwd(s, do, lse, dl, dil, carry=_Exchange("scatter", [parts[n] for n in names]))
            got.update(zip(names, received))
        else:
            (ds,) = _attn_bwd(s, do, lse, dl, dil)
        dss.append(ds)
    dabcv, acc_ca = _conv_mixer_bwd(abcv, dya, p["conv_a_w"], p["conv_a_b"])
    dw_s = [_dw_in_qkv(ds, h, dil) for ds, dil in zip(dss, DILATIONS)]
    dw_qkv = [w[j * GROUP_W:(j + 1) * GROUP_W] for j in range(3) for w in dw_s]
    g_w_in = jnp.concatenate([_tn_matmul(dabcv, h, "dw_in_a"), *dw_qkv, _tn_matmul(dgates, h, "dw_in_g")], axis=0)
    parts["w_in"] = _by_destination(g_w_in)
    (dx, acc_g1), (got["w_in"],) = _in_proj_bwd(x, dx1, dabcv, dss, dgates, p["w_in"], p["norm_mix_g"],
                                                carry=_Exchange("scatter", [parts["w_in"]]))
    small = dict(norm_mix_g=acc_g1[0:1], b_gate=acc_bg[0:2], conv_a_w=acc_ca[0:3], conv_a_b=acc_ca[3:4],
                 norm_ffn_g=acc_g2[0:1], ffn_conv_w=acc_fw[0:3], ffn_conv_b=acc_fb[0:1], final_norm_g=acc_gf[0:1])
    return loss[0, 0], dx, parts, got, small


def _all_gather(shards):
    n = len(shards)

    def body(*refs):
        ins, outs = refs[:n], refs[n:2 * n]
        send_sems, recv_sems, local_sems = refs[2 * n:]
        x, y, c = _mesh_pos()
        me, sibling = (x, y, c), (x, y, 1 - c)
        chips = [(1 - x, y), (x, 1 - y), (1 - x, 1 - y)]

        def copy(i, k, block, to, src=None):
            rows = outs[i].at[_dev_index(*block)]
            return pltpu.make_async_remote_copy(
                src_ref=rows if src is None else src, dst_ref=rows, send_sem=send_sems.at[i, k],
                recv_sem=recv_sems.at[i, k], device_id=to, device_id_type=MESH)

        mine, first, passed = [], [], []
        for i in range(n):
            cp = pltpu.make_async_copy(ins[i], outs[i].at[_dev_index(*me)], local_sems.at[i])
            cp.start()
            mine.append(cp)
            first.append(copy(i, 0, me, sibling, src=ins[i]))
            first += [copy(i, 1 + j, me, (*chip, c), src=ins[i]) for j, chip in enumerate(chips)]
        for cp in first:
            cp.start()
        for i in range(n):
            for j, chip in enumerate(chips):
                copy(i, 1 + j, (*chip, c), me).wait_recv()
                fw = copy(i, 4 + j, (*chip, c), sibling)
                fw.start()
                passed.append(fw)
        for i in range(n):
            copy(i, 0, sibling, me).wait_recv()
            for j, chip in enumerate(chips):
                copy(i, 4 + j, (*chip, 1 - c), me).wait_recv()
        for cp in first + passed:
            cp.wait_send()
        for cp in mine:
            cp.wait()

    return pl.pallas_call(
        body, name="all_gather_weights",
        in_specs=[ANY] * n, out_specs=[ANY] * n,
        out_shape=[jax.ShapeDtypeStruct((N_DEV,) + s.shape, s.dtype) for s in shards],
        scratch_shapes=[pltpu.SemaphoreType.DMA((n, 7)), pltpu.SemaphoreType.DMA((n, 7)),
                        pltpu.SemaphoreType.DMA((n,))],
    )(*shards)


def _all_reduce_small(v):
    r = v.shape[0]

    def body(v_ref, o_ref, gath, send_sems, recv_sems):
        x, y, c = _mesh_pos()
        me = _dev_index(x, y, c)
        gath[me] = v_ref[...]
        flips = [(kx, ky, kc) for kx in (0, 1) for ky in (0, 1) for kc in (0, 1)][1:]
        copies = []
        for k, (kx, ky, kc) in enumerate(flips):
            px = 1 - x if kx else x
            py = 1 - y if ky else y
            pc = 1 - c if kc else c
            cp = pltpu.make_async_remote_copy(
                src_ref=v_ref, dst_ref=gath.at[me], send_sem=send_sems.at[k], recv_sem=recv_sems.at[k],
                device_id=(px, py, pc), device_id_type=MESH)
            cp.start()
            copies.append((cp, _dev_index(px, py, pc)))
        for k, (cp, peer) in enumerate(copies):
            pltpu.make_async_remote_copy(
                src_ref=v_ref, dst_ref=gath.at[peer], send_sem=send_sems.at[k], recv_sem=recv_sems.at[k],
                device_id=(x, y, c), device_id_type=MESH).wait_recv()
        for cp, _ in copies:
            cp.wait_send()
        total = gath[0]
        for j in range(1, N_DEV):
            total = total + gath[j]
        o_ref[...] = total

    return pl.pallas_call(
        body, name="all_reduce_small",
        in_specs=[pl.BlockSpec(memory_space=pltpu.VMEM)], out_specs=pl.BlockSpec(memory_space=pltpu.VMEM),
        out_shape=jax.ShapeDtypeStruct((r, LANES), F32),
        scratch_shapes=[pltpu.VMEM((N_DEV, r, LANES), F32), pltpu.SemaphoreType.DMA((7,)),
                        pltpu.SemaphoreType.DMA((7,))],
    )(v)


def _adamw_math(w, g, m, v):
    m2 = ADAM_B1 * m + (1.0 - ADAM_B1) * g
    v2 = ADAM_B2 * v + (1.0 - ADAM_B2) * (g * g)
    m_hat = m2 / (1.0 - ADAM_B1 ** ADAM_STEP)
    v_hat = v2 / (1.0 - ADAM_B2 ** ADAM_STEP)
    delta = -ADAM_LR * (m_hat / (jnp.sqrt(v_hat) + ADAM_EPS) + ADAM_WD * w)
    return delta, m2, v2


def _adamw_big(w, m, v, part, got, me):
    r, c = w.shape
    tr = max(t for t in range(HALO, min(r, 512) + 1, HALO) if r % t == 0)

    def body(me_ref, w_ref, m_ref, v_ref, own_ref, *rest):
        del me_ref
        got_refs, (g_out, d_out, m_out, v_out) = rest[:N_DEV - 1], rest[N_DEV - 1:]
        g = own_ref[...].astype(F32)
        for ref in got_refs:
            g = g + ref[...].astype(F32)
        delta, m2, v2 = _adamw_math(w_ref[...], g, m_ref[...], v_ref[...])
        g_out[...] = g
        d_out[...] = delta
        m_out[...] = m2
        v_out[...] = v2

    def peer_block(k):
        return pl.BlockSpec((None, tr, c), lambda i, me_ref: (jnp.bitwise_xor(me_ref[0], k), i, 0))

    plain = pl.BlockSpec((tr, c), lambda i, me_ref: (i, 0))
    out = jax.ShapeDtypeStruct((r, c), F32)
    return pl.pallas_call(
        body, name="adamw_big",
        grid_spec=pltpu.PrefetchScalarGridSpec(
            num_scalar_prefetch=1, grid=(r // tr,),
            in_specs=[plain, plain, plain] + [peer_block(k) for k in range(N_DEV)],
            out_specs=[plain] * 4),
        out_shape=[out] * 4,
        compiler_params=_params("parallel"),
    )(me, w, m, v, part, *([got] * (N_DEV - 1)))


def _adamw_small(w, g, m, v):
    def body(w_ref, g_ref, m_ref, v_ref, d_out, m_out, v_out):
        delta, m2, v2 = _adamw_math(w_ref[...], g_ref[...], m_ref[...], v_ref[...])
        d_out[...] = delta
        m_out[...] = m2
        v_out[...] = v2

    out = jax.ShapeDtypeStruct(w.shape, F32)
    return pl.pallas_call(body, name="adamw_small", out_shape=[out] * 3)(w, g, m, v)


BIG = ("w_in", "w_proj_a", "w_proj_b", "w_out", "w_up", "w_down")
LATE = ("w_proj_a", "w_proj_b", "w_out", "w_up", "w_down")
COLUMN_SHARDED = ("w_in", "w_proj_a", "w_proj_b", "w_up")
SMALL = ("norm_mix_g", "b_gate", "conv_a_w", "conv_a_b", "norm_ffn_g", "ffn_conv_w", "ffn_conv_b", "final_norm_g")
SMALL_SHARDED = ("b_gate", "conv_a_w", "ffn_conv_w")
WEIGHTS = ("norm_mix_g", "w_in", "b_gate", "conv_a_w", "conv_a_b", "w_proj_a", "w_proj_b", "w_out", "norm_ffn_g",
           "w_up", "ffn_conv_w", "ffn_conv_b", "w_down", "final_norm_g")


def _pack(vectors, rows):
    flat = jnp.concatenate([v.reshape(-1) for v in vectors])
    return jnp.pad(flat, (0, rows * LANES - flat.shape[0])).reshape(rows, LANES)


def _packed_rows(count):
    rows = -(-count // LANES)
    return -(-rows // SUBLANES) * SUBLANES


def _unpack(packed, shapes):
    flat = packed.reshape(-1)
    out, lo = [], 0
    for s in shapes:
        size = 1
        for dim in s:
            size *= dim
        out.append(flat[lo:lo + size].reshape(s))
        lo += size
    return out


def _full_from_gathered(gathered):
    _, r, c = gathered.shape
    return gathered.reshape(N_DEV * r, c)


def _by_destination(grad):
    rr, cc = grad.shape
    return grad.reshape(N_DEV, rr // N_DEV, cc)


def _block2d(name, a):
    a = a.reshape(a.shape[-2:])
    return a.T if name in COLUMN_SHARDED else a


def kernel(x, norm_mix_g, w_in, b_gate, conv_a_w, conv_a_b, w_proj_a, w_proj_b, w_out, norm_ffn_g, w_up, ffn_conv_w, ffn_conv_b, w_down, final_norm_g, loss_target, m_norm_mix_g, m_w_in, m_b_gate, m_conv_a_w, m_conv_a_b, m_w_proj_a, m_w_proj_b, m_w_out, m_norm_ffn_g, m_w_up, m_ffn_conv_w, m_ffn_conv_b, m_w_down, m_final_norm_g, v_norm_mix_g, v_w_in, v_b_gate, v_conv_a_w, v_conv_a_b, v_w_proj_a, v_w_proj_b, v_w_out, v_norm_ffn_g, v_w_up, v_ffn_conv_w, v_ffn_conv_b, v_w_down, v_final_norm_g):
    given = dict(locals())
    shard = {n: given[n] for n in WEIGHTS}
    mom_m = {n: given["m_" + n] for n in WEIGHTS}
    mom_v = {n: given["v_" + n] for n in WEIGHTS}
    xi, yi, ci = _mesh_pos()
    me = _dev_index(xi, yi, ci)
    me1 = me.astype(jnp.int32).reshape(1)

    big2d = {n: _block2d(n, shard[n]) for n in BIG}
    small_shapes = [shard[n].shape[1:] for n in SMALL_SHARDED]
    n_small = sum(s[0] * s[1] for s in small_shapes)
    packed_small = _pack([shard[n] for n in SMALL_SHARDED], _packed_rows(n_small))
    gathered = _all_gather([big2d["w_in"].astype(BF16), packed_small])
    p = {"w_in": _full_from_gathered(gathered[0])}
    per_dev = [_unpack(gathered[-1][j], small_shapes) for j in range(N_DEV)]
    for i, n in enumerate(SMALL_SHARDED):
        p[n] = jnp.concatenate([per_dev[j][i] for j in range(N_DEV)], axis=1)
    p["norm_mix_g"], p["norm_ffn_g"] = shard["norm_mix_g"], shard["norm_ffn_g"]
    p["conv_a_b"], p["ffn_conv_b"] = shard["conv_a_b"], shard["ffn_conv_b"]
    p["final_norm_g"] = shard["final_norm_g"].reshape(1, -1)
    late = {n: big2d[n].astype(BF16) for n in LATE}

    loss_part, dx, parts, got, g_small = _local_step(x[0], loss_target[0], p, late)

    results = {}
    for n in BIG:
        outs = _adamw_big(big2d[n], _block2d(n, mom_m[n]), _block2d(n, mom_v[n]), parts[n], got[n], me1)
        results[n] = [_block2d(n, o).reshape(shard[n].shape) for o in outs]

    small_full_shapes = [g_small[n].shape for n in SMALL]
    n_vec = sum(s[0] * s[1] for s in small_full_shapes) + 1
    packed = _pack([g_small[n] for n in SMALL] + [loss_part.reshape(1)], _packed_rows(n_vec))
    reduced = _all_reduce_small(packed)
    *g_full, loss_vec = _unpack(reduced, small_full_shapes + [(1,)])
    loss = loss_vec[0]
    own_g = []
    for n, g in zip(SMALL, g_full):
        if n in SMALL_SHARDED:
            width = shard[n].shape[-1]
            g = lax.dynamic_slice_in_dim(g, me * width, width, axis=1)
        own_g.append(g.reshape(shard[n].shape))
    own_shapes = [shard[n].shape for n in SMALL]
    rows = _packed_rows(sum(g.size for g in own_g))
    small_out = _adamw_small(_pack([shard[n] for n in SMALL], rows), _pack(own_g, rows),
                             _pack([mom_m[n] for n in SMALL], rows), _pack([mom_v[n] for n in SMALL], rows))
    deltas, new_ms, new_vs = (_unpack(o, own_shapes) for o in small_out)
    for i, n in enumerate(SMALL):
        results[n] = [own_g[i], deltas[i], new_ms[i], new_vs[i]]

    grad_x = dx.reshape(x.shape)
    return (loss, grad_x, *[results[n][0] for n in WEIGHTS], *[results[n][1] for n in WEIGHTS],
            *[results[n][2] for n in WEIGHTS], *[results[n][3] for n in WEIGHTS])
```

```python
import functools

import jax
import jax.numpy as jnp
from jax import lax
from jax.experimental import pallas as pl
from jax.experimental.pallas import tpu as pltpu

F32 = jnp.float32
BF16 = jnp.bfloat16
MESH = pl.DeviceIdType.MESH

N_DEV = 8
RMS_EPS = 1e-6
NEG_INF = -1e30
N_GROUPS = 3
DILATIONS = (1, 4, 16)
HEADS_PER_GROUP = 4
HEAD_DIM = 64
GROUP_W = HEADS_PER_GROUP * HEAD_DIM
ATTN_W = N_GROUPS * GROUP_W
QBLK = 128
ATTN_SCALE = HEAD_DIM ** -0.5

ADAM_LR = 0.001
ADAM_B1 = 0.9
ADAM_B2 = 0.999
ADAM_EPS = 1e-08
ADAM_WD = 0.01
ADAM_STEP = 10

HALO = 16
LANES = 128
SUBLANES = 8
VMEM_LIMIT_BYTES = 56 * 1024 * 1024


def _params(*sem):
    return pltpu.CompilerParams(dimension_semantics=sem, vmem_limit_bytes=VMEM_LIMIT_BYTES)


def _pick_tile(n, cap):
    if n <= cap:
        return n
    best = None
    for t in range(LANES, cap + 1, LANES):
        if n % t == 0:
            best = t
    assert best is not None, (n, cap)
    return best


def _rows(tm, c, j=0):
    return pl.BlockSpec((tm, c), lambda m: (m, j))


def _prev_halo(tm, c):
    return pl.BlockSpec((HALO, c), lambda m: (jnp.maximum(m * (tm // HALO) - 1, 0), 0))


def _next_halo(tm, c, t_total):
    last = t_total // HALO - 1
    return pl.BlockSpec((HALO, c), lambda m: (jnp.minimum((m + 1) * (tm // HALO), last), 0))


def _resident(shape):
    nd = len(shape)
    return pl.BlockSpec(shape, lambda *_: (0,) * nd, pipeline_mode=pl.Buffered(1))


def _acc_spec(c):
    return pl.BlockSpec((SUBLANES, c), lambda *_: (0, 0))


def _shift_down(u, halo, k):
    edge = jnp.concatenate([halo[HALO - SUBLANES:], u[:SUBLANES]], axis=0)
    head = pltpu.roll(edge, k, 0)[SUBLANES:]
    return jnp.concatenate([head, pltpu.roll(u, k, 0)[SUBLANES:]], axis=0)


def _shift_up(u, halo, k):
    n = u.shape[0]
    edge = jnp.concatenate([u[n - SUBLANES:], halo[:SUBLANES]], axis=0)
    tail = pltpu.roll(edge, 2 * SUBLANES - k, 0)[:SUBLANES]
    return jnp.concatenate([pltpu.roll(u, n - k, 0)[:n - SUBLANES], tail], axis=0)


def _stack_rows(rows, c):
    idx = lax.broadcasted_iota(jnp.int32, (SUBLANES, c), 0)
    out = jnp.zeros((SUBLANES, c), F32)
    for i, r in enumerate(rows):
        out = out + jnp.where(idx == i, r, 0.0)
    return out


def _colsum(v):
    return jnp.sum(v, axis=0, keepdims=True)


def _sigmoid(v):
    return 0.5 * jnp.tanh(0.5 * v) + 0.5


def _rms_fwd(xv, g):
    r = lax.rsqrt(jnp.mean(xv * xv, axis=-1, keepdims=True) + RMS_EPS)
    return xv * r * g, r


def _rms_bwd(xv, g, dy):
    r = lax.rsqrt(jnp.mean(xv * xv, axis=-1, keepdims=True) + RMS_EPS)
    xn = xv * r
    dxn = dy * g
    dx = r * (dxn - xn * jnp.mean(dxn * xn, axis=-1, keepdims=True))
    return dx, dy * xn


def _dot(a, b):
    return jnp.dot(a, b, preferred_element_type=F32)


def _dot_nt(a, b):
    return lax.dot_general(a, b, (((1,), (1,)), ((), ())), preferred_element_type=F32)


def _dot_tn(a, b):
    return lax.dot_general(a, b, (((0,), (0,)), ((), ())), preferred_element_type=F32)


def _perm(dil, n, inverse=False):
    i = lax.broadcasted_iota(jnp.int32, (n, n), 0)
    j = lax.broadcasted_iota(jnp.int32, (n, n), 1)
    if inverse:
        i, j = j, i
    per = n // dil
    return (j == (i % per) * dil + i // per).astype(BF16)


def _permute_rows(pm, v):
    if v.dtype == BF16:
        return _dot(pm, v).astype(BF16)
    h1 = v.astype(BF16)
    r1 = v - h1.astype(F32)
    h2 = r1.astype(BF16)
    h3 = (r1 - h2.astype(F32)).astype(BF16)
    return _dot(pm, h1) + _dot(pm, h2) + _dot(pm, h3)


def _stream_view(a, dil):
    t, c = a.shape
    return a.reshape(dil, t // dil, c)


def _stream_spec(dil, tm, c):
    return pl.BlockSpec((dil, tm // dil, c), lambda m: (0, m, 0))


def _load_streams(ref, dil, tm):
    v = ref[...].reshape(tm, ref.shape[-1])
    return v if dil == 1 else _permute_rows(_perm(dil, tm, inverse=True), v)


def _store_streams(ref, dil, tm, v):
    if dil > 1:
        v = _permute_rows(_perm(dil, tm), v)
    ref[...] = v.reshape(ref.shape).astype(ref.dtype)


ANY = pl.BlockSpec(memory_space=pl.ANY)


def _mesh_pos():
    return lax.axis_index("x"), lax.axis_index("y"), lax.axis_index("c")


def _dev_index(px, py, pc):
    return 4 * px + 2 * py + pc


class _Exchange:
    def __init__(self, mode, arrays):
        self.mode, self.arrays = mode, list(arrays)
        n = len(self.arrays)
        if mode == "gather":
            self.out_shape = [jax.ShapeDtypeStruct((N_DEV,) + a.shape, a.dtype) for a in self.arrays]
        else:
            self.out_shape = [jax.ShapeDtypeStruct(a.shape, a.dtype) for a in self.arrays]
        self.scratch = [pltpu.SemaphoreType.DMA((n, N_DEV - 1)), pltpu.SemaphoreType.DMA((n, N_DEV - 1)),
                        pltpu.SemaphoreType.DMA((n,))]

    def _peers(self):
        x, y, c = _mesh_pos()
        flips = [(kx, ky, kc) for kx in (0, 1) for ky in (0, 1) for kc in (0, 1)][1:]
        peers = [(1 - x if kx else x, 1 - y if ky else y, 1 - c if kc else c) for kx, ky, kc in flips]
        return _dev_index(x, y, c), peers

    def _copy(self, ins, outs, sems, i, k, peer, me, sending):
        src = ins[i] if self.mode == "gather" else ins[i].at[_dev_index(*peer)]
        dst = outs[i].at[me if sending else _dev_index(*peer)]
        return pltpu.make_async_remote_copy(src_ref=src, dst_ref=dst, send_sem=sems[0].at[i, k],
                                            recv_sem=sems[1].at[i, k], device_id=peer, device_id_type=MESH)

    def _own(self, ins, outs, sems, i, me):
        return pltpu.make_async_copy(ins[i], outs[i].at[me], sems[2].at[i])

    def start(self, ins, outs, sems):
        me, peers = self._peers()
        for i in range(len(ins)):
            if self.mode == "gather":
                self._own(ins, outs, sems, i, me).start()
            for k, peer in enumerate(peers):
                self._copy(ins, outs, sems, i, k, peer, me, True).start()

    def wait(self, ins, outs, sems):
        me, peers = self._peers()
        for i in range(len(ins)):
            for k, peer in enumerate(peers):
                self._copy(ins, outs, sems, i, k, peer, me, False).wait_recv()
            for k, peer in enumerate(peers):
                self._copy(ins, outs, sems, i, k, peer, me, True).wait_send()
            if self.mode == "gather":
                self._own(ins, outs, sems, i, me).wait()


def _call(body, *, name, grid, in_specs, out_specs, out_shape, args, semantics, carry=None, scratch=()):
    if carry is None:
        return pl.pallas_call(body, name=name, grid=grid, in_specs=in_specs, out_specs=out_specs,
                              out_shape=out_shape, scratch_shapes=list(scratch),
                              compiler_params=_params(*semantics))(*args)
    n_in, n_out, n_x, n_s = len(in_specs), len(out_specs), len(carry.arrays), len(scratch)

    def carried(*refs):
        ins, x_ins = refs[:n_in], refs[n_in:n_in + n_x]
        outs = refs[n_in + n_x:n_in + n_x + n_out]
        x_outs = refs[n_in + n_x + n_out:n_in + 2 * n_x + n_out]
        own = refs[n_in + 2 * n_x + n_out:n_in + 2 * n_x + n_out + n_s]
        sems = refs[n_in + 2 * n_x + n_out + n_s:]
        first = functools.reduce(jnp.logical_and, [pl.program_id(a) == 0 for a in range(len(grid))])
        last = functools.reduce(jnp.logical_and, [pl.program_id(a) == grid[a] - 1 for a in range(len(grid))])

        @pl.when(first)
        def _():
            carry.start(x_ins, x_outs, sems)

        body(*ins, *outs, *own)

        @pl.when(last)
        def _():
            carry.wait(x_ins, x_outs, sems)

    res = pl.pallas_call(
        carried, name=name, grid=grid, in_specs=list(in_specs) + [ANY] * n_x,
        out_specs=list(out_specs) + [ANY] * n_x, out_shape=list(out_shape) + carry.out_shape,
        scratch_shapes=list(scratch) + carry.scratch, compiler_params=_params(*["arbitrary"] * len(grid)),
    )(*args, *carry.arrays)
    return list(res[:n_out]), list(res[n_out:])


def _up_proj(x, g, wt, carry=None):
    t, d = x.shape
    n = wt.shape[0]
    tm = min(256, t)

    def body(x_ref, g_ref, wt_ref, h_ref, o_ref):
        h = _rms_fwd(x_ref[...], g_ref[...])[0].astype(BF16)
        h_ref[...] = h
        o_ref[...] = _dot_nt(h, wt_ref[...]).astype(BF16)

    return _call(
        body, name="up_proj", grid=(t // tm,),
        in_specs=[_rows(tm, d), _resident((1, d)), _resident((n, d))],
        out_specs=[_rows(tm, d), _rows(tm, n)],
        out_shape=[jax.ShapeDtypeStruct((t, d), BF16), jax.ShapeDtypeStruct((t, n), BF16)],
        args=(x, g, wt), semantics=("parallel",), carry=carry)


def _in_proj(x, g, wt, cw, carry=None):
    t, d = x.shape
    n = wt.shape[0]
    tm = min(256, t)
    qkv0 = 3 * cw

    def body(x_ref, g_ref, wt_ref, h_ref, abcv_ref, gates_ref, *s_refs):
        h = _rms_fwd(x_ref[...], g_ref[...])[0].astype(BF16)
        h_ref[...] = h
        abcv_ref[...] = _dot_nt(h, wt_ref[0:qkv0, :]).astype(BF16)
        gates_ref[...] = _dot_nt(h, wt_ref[qkv0 + 3 * ATTN_W:n, :]).astype(BF16)
        for gi, s_ref in enumerate(s_refs):
            cols = [_dot_nt(h, wt_ref[qkv0 + j * ATTN_W + gi * GROUP_W:qkv0 + j * ATTN_W + (gi + 1) * GROUP_W, :])
                    for j in range(3)]
            _store_streams(s_ref, DILATIONS[gi], tm, jnp.concatenate(cols, axis=1).astype(BF16))

    return _call(
        body, name="in_proj", grid=(t // tm,),
        in_specs=[_rows(tm, d), _resident((1, d)), _resident((n, d))],
        out_specs=[_rows(tm, d), _rows(tm, qkv0), _rows(tm, 2 * d)]
        + [_stream_spec(dil, tm, 3 * GROUP_W) for dil in DILATIONS],
        out_shape=[jax.ShapeDtypeStruct((t, d), BF16), jax.ShapeDtypeStruct((t, qkv0), BF16),
                   jax.ShapeDtypeStruct((t, 2 * d), BF16)]
        + [jax.ShapeDtypeStruct((dil, t // dil, 3 * GROUP_W), BF16) for dil in DILATIONS],
        args=(x, g, wt), semantics=("parallel",), carry=carry)


def _head_masks():
    lane = lax.broadcasted_iota(jnp.int32, (1, GROUP_W), 1)
    return lane, [(lane // HEAD_DIM) == h for h in range(HEADS_PER_GROUP)]


def _stack_heads(v, heads):
    return jnp.concatenate([jnp.where(hm, v, jnp.zeros_like(v)) for hm in heads], axis=0)


def _merge_heads(v, heads):
    out = jnp.zeros((QBLK, GROUP_W), v.dtype)
    for h, hm in enumerate(heads):
        out = jnp.where(hm, v[h * QBLK:(h + 1) * QBLK], out)
    return out


def _qkv_block(col, shift=0, nb=None):
    if shift == 0:
        return pl.BlockSpec((QBLK, GROUP_W), lambda b: (b, col))
    return pl.BlockSpec((QBLK, GROUP_W), lambda b: (jnp.clip(b + shift, 0, nb - 1), col))


def _band_mask(has_prev):
    rows = HEADS_PER_GROUP * QBLK
    row = lax.broadcasted_iota(jnp.int32, (rows, 2 * QBLK), 0) & (QBLK - 1)
    col = lax.broadcasted_iota(jnp.int32, (rows, 2 * QBLK), 1)
    return ((col < QBLK) & (col >= row) & has_prev) | ((col >= QBLK) & (col - QBLK <= row))


def _next_mask(has_next):
    rows = HEADS_PER_GROUP * QBLK
    row = lax.broadcasted_iota(jnp.int32, (rows, QBLK), 0) & (QBLK - 1)
    col = lax.broadcasted_iota(jnp.int32, (rows, QBLK), 1)
    return (col >= row) & has_next


def _attn_fwd(s, dil, carry=None):
    t = s.shape[0] * s.shape[1]
    nb = t // QBLK
    per_stream = nb // dil

    def body(q_ref, kc_ref, kp_ref, vc_ref, vp_ref, o_ref, lse_ref):
        b = pl.program_id(0)
        mask = _band_mask(lax.rem(b, per_stream) != 0)
        _, heads = _head_masks()
        k2 = jnp.concatenate([kp_ref[...], kc_ref[...]], axis=0)
        v2 = jnp.concatenate([vp_ref[...], vc_ref[...]], axis=0)
        sc = jnp.where(mask, _dot_nt(_stack_heads(q_ref[...], heads), k2) * ATTN_SCALE, NEG_INF)
        mx = jnp.max(sc, axis=1, keepdims=True)
        pr = jnp.exp(sc - mx)
        den = jnp.sum(pr, axis=1, keepdims=True)
        o_all = _dot(pr.astype(BF16), v2) / den
        o_ref[...] = _merge_heads(o_all, heads).astype(BF16)
        lse_ref[...] = _merge_heads(jnp.broadcast_to(mx + jnp.log(den), o_all.shape), heads)

    sv = s.reshape(t, 3 * GROUP_W)
    return _call(
        body, name=f"attn_fwd_d{dil}", grid=(nb,),
        in_specs=[_qkv_block(0), _qkv_block(1), _qkv_block(1, -1, nb), _qkv_block(2), _qkv_block(2, -1, nb)],
        out_specs=[_qkv_block(0), _qkv_block(0)],
        out_shape=[jax.ShapeDtypeStruct((t, GROUP_W), BF16), jax.ShapeDtypeStruct((t, GROUP_W), F32)],
        args=(sv, sv, sv, sv, sv), semantics=("parallel",), carry=carry)


def _group_softmax(parts):
    mx = jnp.maximum(jnp.maximum(parts[0], parts[1]), parts[2])
    es = [jnp.exp(p - mx) for p in parts]
    den = es[0] + es[1] + es[2]
    return [e / den for e in es]


def _mixer_out(x, abcv, gates, os, lses, conv_w, conv_b, b_gate, w_pa, w_pb, w_o):
    t, d = x.shape
    cw = conv_w.shape[1]
    tm = min(256, t)

    def body(x_ref, abcv_ref, halo_ref, gates_ref, o0_ref, o1_ref, o2_ref, l0_ref, l1_ref, l2_ref, cw_ref, cb_ref,
             bg_ref, wpa_ref, wpb_ref, wo_ref, x1_ref, ya_ref, yb_ref, yap_ref, ybp_ref, mg_ref):
        m = pl.program_id(0)
        ab = abcv_ref[:, 0:cw].astype(F32)
        u = abcv_ref[:, cw:2 * cw].astype(F32) * abcv_ref[:, 2 * cw:3 * cw].astype(F32)
        hu = halo_ref[:, cw:2 * cw].astype(F32) * halo_ref[:, 2 * cw:3 * cw].astype(F32)
        hu = jnp.where(m > 0, hu, 0.0)
        cv = (cw_ref[0:1, :] * _shift_down(u, hu, 2) + cw_ref[1:2, :] * _shift_down(u, hu, 1)
              + cw_ref[2:3, :] * u + cb_ref[...])
        ya = (ab * cv).astype(BF16)
        ya_ref[...] = ya
        alphas = _group_softmax([_load_streams(r, dil, tm) for r, dil in zip((l0_ref, l1_ref, l2_ref), DILATIONS)])
        for i, (o_ref, dil) in enumerate(zip((o0_ref, o1_ref, o2_ref), DILATIONS)):
            sl = slice(i * GROUP_W, (i + 1) * GROUP_W)
            yb_ref[:, sl] = (alphas[i] * _load_streams(o_ref, dil, tm).astype(F32)).astype(BF16)
        yap = _dot_nt(ya, wpa_ref[...])
        ybp = _dot_nt(yb_ref[...], wpb_ref[...])
        yap_ref[...] = yap.astype(BF16)
        ybp_ref[...] = ybp.astype(BF16)
        sa = _sigmoid(gates_ref[:, 0:d].astype(F32) + bg_ref[0:1, :])
        sb = _sigmoid(gates_ref[:, d:2 * d].astype(F32) + bg_ref[1:2, :])
        merged = (sa * yap + sb * ybp).astype(BF16)
        mg_ref[...] = merged
        x1_ref[...] = x_ref[...] + _dot(merged, wo_ref[...])

    return pl.pallas_call(
        body, name="mixer_out", grid=(t // tm,),
        in_specs=[_rows(tm, d), _rows(tm, 3 * cw), _prev_halo(tm, 3 * cw), _rows(tm, 2 * d)]
        + [_stream_spec(dil, tm, GROUP_W) for dil in DILATIONS] * 2
        + [_resident((3, cw)), _resident((1, cw)), _resident((2, d)),
           _resident((d, cw)), _resident((d, ATTN_W)), _resident((d, d))],
        out_specs=[_rows(tm, d), _rows(tm, cw), _rows(tm, ATTN_W), _rows(tm, d), _rows(tm, d), _rows(tm, d)],
        out_shape=[jax.ShapeDtypeStruct((t, d), F32), jax.ShapeDtypeStruct((t, cw), BF16),
                   jax.ShapeDtypeStruct((t, ATTN_W), BF16), jax.ShapeDtypeStruct((t, d), BF16),
                   jax.ShapeDtypeStruct((t, d), BF16), jax.ShapeDtypeStruct((t, d), BF16)],
        compiler_params=_params("parallel"),
    )(x, abcv, abcv, gates, *[_stream_view(a, dil) for a, dil in zip(os, DILATIONS)],
      *[_stream_view(a, dil) for a, dil in zip(lses, DILATIONS)], conv_w, conv_b, b_gate, w_pa, w_pb, w_o)


def _ffn_conv(p_ref, halo_ref, w_ref, b_ref, m, c0, wd):
    p = p_ref[:, c0:c0 + wd].astype(F32)
    hp = jnp.where(m > 0, halo_ref[:, c0:c0 + wd].astype(F32), 0.0)
    return (w_ref[0:1, c0:c0 + wd] * _shift_down(p, hp, 2) + w_ref[1:2, c0:c0 + wd] * _shift_down(p, hp, 1)
            + w_ref[2:3, c0:c0 + wd] * p + b_ref[:, c0:c0 + wd])


def _ffn_loss(x1, up_pre, target, conv_w, conv_b, w_d, g_f):
    t, d = x1.shape
    dff = w_d.shape[0]
    tm = min(256, t)
    ck = _pick_tile(dff, 1408)

    def body(x1_ref, up_ref, halo_ref, tg_ref, cw_ref, cb_ref, wd_ref, gf_ref, act_ref, conv_ref, dx2_ref, acc_ref,
             loss_ref):
        m = pl.program_id(0)

        @pl.when(m == 0)
        def _():
            acc_ref[...] = jnp.zeros_like(acc_ref)
            loss_ref[...] = jnp.zeros_like(loss_ref)

        x2 = x1_ref[...]
        for c0 in range(0, dff, ck):
            gate = _ffn_conv(up_ref, halo_ref, cw_ref, cb_ref, m, c0, ck)
            val = _ffn_conv(up_ref, halo_ref, cw_ref, cb_ref, m, dff + c0, ck)
            conv_ref[:, c0:c0 + ck] = gate.astype(BF16)
            conv_ref[:, dff + c0:dff + c0 + ck] = val.astype(BF16)
            act = (gate * _sigmoid(gate) * val).astype(BF16)
            act_ref[:, c0:c0 + ck] = act
            x2 = x2 + _dot(act, wd_ref[c0:c0 + ck, :])
        y, _ = _rms_fwd(x2, gf_ref[...])
        diff = y - tg_ref[...]
        loss_ref[...] += 0.5 * jnp.sum(jnp.mean(diff * diff, axis=-1, keepdims=True))
        dx2, dg = _rms_bwd(x2, gf_ref[...], diff * (1.0 / d))
        dx2_ref[...] = dx2
        acc_ref[...] += _stack_rows([_colsum(dg)], d)

    return pl.pallas_call(
        body, name="ffn_loss", grid=(t // tm,),
        in_specs=[_rows(tm, d), _rows(tm, 2 * dff), _prev_halo(tm, 2 * dff), _rows(tm, d),
                  _resident((3, 2 * dff)), _resident((1, 2 * dff)), _resident((dff, d)), _resident((1, d))],
        out_specs=[_rows(tm, dff), _rows(tm, 2 * dff), _rows(tm, d), _acc_spec(d), _acc_spec(LANES)],
        out_shape=[jax.ShapeDtypeStruct((t, dff), BF16), jax.ShapeDtypeStruct((t, 2 * dff), BF16),
                   jax.ShapeDtypeStruct((t, d), F32), jax.ShapeDtypeStruct((SUBLANES, d), F32),
                   jax.ShapeDtypeStruct((SUBLANES, LANES), F32)],
        compiler_params=_params("arbitrary"),
    )(x1, up_pre, up_pre, target, conv_w, conv_b, w_d, g_f)


def _ffn_act_bwd(dx2, conv, w_d):
    t, d = dx2.shape
    dff = w_d.shape[0]
    tm = min(256, t)
    ck = _pick_tile(dff, 1408)

    def body(dx2_ref, conv_ref, wd_ref, dup_ref, acc_ref):
        m = pl.program_id(0)

        @pl.when(m == 0)
        def _():
            acc_ref[...] = jnp.zeros_like(acc_ref)

        dx2v = dx2_ref[...].astype(BF16)
        for c0 in range(0, dff, ck):
            dact = _dot_nt(dx2v, wd_ref[c0:c0 + ck, :])
            gate = conv_ref[:, c0:c0 + ck].astype(F32)
            val = conv_ref[:, dff + c0:dff + c0 + ck].astype(F32)
            sg = _sigmoid(gate)
            dval = dact * gate * sg
            dgate = dact * val * sg * (1.0 + gate * (1.0 - sg))
            dup_ref[:, c0:c0 + ck] = dgate.astype(BF16)
            dup_ref[:, dff + c0:dff + c0 + ck] = dval.astype(BF16)
            acc_ref[:, c0:c0 + ck] += _stack_rows([_colsum(dgate)], ck)
            acc_ref[:, dff + c0:dff + c0 + ck] += _stack_rows([_colsum(dval)], ck)

    return pl.pallas_call(
        body, name="ffn_act_bwd", grid=(t // tm,),
        in_specs=[_rows(tm, d), _rows(tm, 2 * dff), _resident((dff, d))],
        out_specs=[_rows(tm, 2 * dff), _acc_spec(2 * dff)],
        out_shape=[jax.ShapeDtypeStruct((t, 2 * dff), BF16), jax.ShapeDtypeStruct((SUBLANES, 2 * dff), F32)],
        compiler_params=_params("arbitrary"),
    )(dx2, conv, w_d)


def _ffn_up_bwd(dup, up_pre, x1, dx2, conv_w, w_u, g2, carry=None):
    t, d = x1.shape
    n = dup.shape[1]
    tm = min(256, t)
    ck = _pick_tile(n, 1408)
    last = t // tm - 1

    def body(dup_ref, nxt_ref, up_ref, x1_ref, dx2_ref, cw_ref, wu_ref, g2_ref, dpre_ref, dx1_ref, acc_ref, accw_ref):
        m = pl.program_id(0)

        @pl.when(m == 0)
        def _():
            acc_ref[...] = jnp.zeros_like(acc_ref)
            accw_ref[...] = jnp.zeros_like(accw_ref)

        dh = jnp.zeros((tm, d), F32)
        for c0 in range(0, n, ck):
            du = dup_ref[:, c0:c0 + ck].astype(F32)
            hn = jnp.where(m < last, nxt_ref[:, c0:c0 + ck].astype(F32), 0.0)
            du1 = _shift_up(du, hn, 1)
            du2 = _shift_up(du, hn, 2)
            dpre = (cw_ref[2:3, c0:c0 + ck] * du + cw_ref[1:2, c0:c0 + ck] * du1
                    + cw_ref[0:1, c0:c0 + ck] * du2).astype(BF16)
            dpre_ref[:, c0:c0 + ck] = dpre
            dh = dh + _dot(dpre, wu_ref[c0:c0 + ck, :])
            p = up_ref[:, c0:c0 + ck].astype(F32)
            accw_ref[:, c0:c0 + ck] += _stack_rows([_colsum(du2 * p), _colsum(du1 * p), _colsum(du * p)], ck)
        dx, dg = _rms_bwd(x1_ref[...], g2_ref[...], dh)
        dx1_ref[...] = dx2_ref[...] + dx
        acc_ref[...] += _stack_rows([_colsum(dg)], d)

    return _call(
        body, name="ffn_up_bwd", grid=(t // tm,),
        in_specs=[_rows(tm, n), _next_halo(tm, n, t), _rows(tm, n), _rows(tm, d), _rows(tm, d), _resident((3, n)),
                  _resident((n, d)), _resident((1, d))],
        out_specs=[_rows(tm, n), _rows(tm, d), _acc_spec(d), _acc_spec(n)],
        out_shape=[jax.ShapeDtypeStruct((t, n), BF16), jax.ShapeDtypeStruct((t, d), F32),
                   jax.ShapeDtypeStruct((SUBLANES, d), F32), jax.ShapeDtypeStruct((SUBLANES, n), F32)],
        args=(dup, dup, up_pre, x1, dx2, conv_w, w_u, g2), semantics=("arbitrary",), carry=carry)


def _tn_matmul(a, b, name):
    t, mdim = a.shape
    n = b.shape[1]
    tk = min(512, t)
    tmm = _pick_tile(mdim, 1408)
    tn = _pick_tile(n, 1536)

    def body(a_ref, b_ref, o_ref, acc_ref):
        k = pl.program_id(2)

        @pl.when(k == 0)
        def _():
            acc_ref[...] = jnp.zeros_like(acc_ref)

        acc_ref[...] += _dot_tn(a_ref[...].astype(BF16), b_ref[...].astype(BF16))

        @pl.when(k == t // tk - 1)
        def _():
            o_ref[...] = acc_ref[...].astype(BF16)

    return pl.pallas_call(
        body, name=name, grid=(mdim // tmm, n // tn, t // tk),
        in_specs=[pl.BlockSpec((tk, tmm), lambda i, j, k: (k, i)), pl.BlockSpec((tk, tn), lambda i, j, k: (k, j))],
        out_specs=pl.BlockSpec((tmm, tn), lambda i, j, k: (i, j)),
        out_shape=jax.ShapeDtypeStruct((mdim, n), BF16),
        scratch_shapes=[pltpu.VMEM((tmm, tn), F32)],
        compiler_params=_params("parallel", "parallel", "arbitrary"),
    )(a, b)


def _mixer_bwd(dx1, gates, yap, ybp, os, lses, b_gate, w_o, w_pa, w_pb):
    t, d = dx1.shape
    cw = w_pa.shape[1]
    tm = min(256, t)

    def body(dx1_ref, gates_ref, yap_ref, ybp_ref, o0_ref, o1_ref, o2_ref, l0_ref, l1_ref, l2_ref, bg_ref, wo_ref,
             wpa_ref, wpb_ref, dgates_ref, dyap_ref, dybp_ref, dya_ref, do0_ref, do1_ref, do2_ref, dl0_ref, dl1_ref,
             dl2_ref, acc_ref):
        m = pl.program_id(0)

        @pl.when(m == 0)
        def _():
            acc_ref[...] = jnp.zeros_like(acc_ref)

        dmg = _dot_nt(dx1_ref[...].astype(BF16), wo_ref[...])
        sa = _sigmoid(gates_ref[:, 0:d].astype(F32) + bg_ref[0:1, :])
        sb = _sigmoid(gates_ref[:, d:2 * d].astype(F32) + bg_ref[1:2, :])
        dyap = (dmg * sa).astype(BF16)
        dybp = (dmg * sb).astype(BF16)
        dga = dmg * yap_ref[...].astype(F32) * sa * (1.0 - sa)
        dgb = dmg * ybp_ref[...].astype(F32) * sb * (1.0 - sb)
        dyap_ref[...] = dyap
        dybp_ref[...] = dybp
        dgates_ref[:, 0:d] = dga.astype(BF16)
        dgates_ref[:, d:2 * d] = dgb.astype(BF16)
        acc_ref[...] += _stack_rows([_colsum(dga), _colsum(dgb)], d)
        dya_ref[...] = _dot(dyap, wpa_ref[...]).astype(BF16)
        dyb = _dot(dybp, wpb_ref[...])

        ri = lax.broadcasted_iota(jnp.int32, (GROUP_W, GROUP_W), 0) // HEAD_DIM
        ci = lax.broadcasted_iota(jnp.int32, (GROUP_W, GROUP_W), 1) // HEAD_DIM
        same_head = (ri == ci).astype(BF16)
        alphas = _group_softmax([_load_streams(r, dil, tm) for r, dil in zip((l0_ref, l1_ref, l2_ref), DILATIONS)])
        dtot = jnp.zeros((tm, GROUP_W), F32)
        for i, (o_ref, do_ref, dil) in enumerate(zip((o0_ref, o1_ref, o2_ref), (do0_ref, do1_ref, do2_ref), DILATIONS)):
            dov = alphas[i] * dyb[:, i * GROUP_W:(i + 1) * GROUP_W]
            _store_streams(do_ref, dil, tm, dov.astype(BF16))
            prod = dov * _load_streams(o_ref, dil, tm).astype(F32)
            hi = prod.astype(BF16)
            lo = (prod - hi.astype(F32)).astype(BF16)
            dtot = dtot + _dot(hi, same_head) + _dot(lo, same_head)
        for alpha, dl_ref, dil in zip(alphas, (dl0_ref, dl1_ref, dl2_ref), DILATIONS):
            _store_streams(dl_ref, dil, tm, alpha * dtot)

    streams = [_stream_spec(dil, tm, GROUP_W) for dil in DILATIONS]
    res = pl.pallas_call(
        body, name="mixer_bwd", grid=(t // tm,),
        in_specs=[_rows(tm, d), _rows(tm, 2 * d), _rows(tm, d), _rows(tm, d)] + streams * 2
        + [_resident((2, d)), _resident((d, d)), _resident((d, cw)), _resident((d, ATTN_W))],
        out_specs=[_rows(tm, 2 * d), _rows(tm, d), _rows(tm, d), _rows(tm, cw)] + streams * 2 + [_acc_spec(d)],
        out_shape=[jax.ShapeDtypeStruct((t, 2 * d), BF16), jax.ShapeDtypeStruct((t, d), BF16),
                   jax.ShapeDtypeStruct((t, d), BF16), jax.ShapeDtypeStruct((t, cw), BF16)]
        + [jax.ShapeDtypeStruct((dil, t // dil, GROUP_W), BF16) for dil in DILATIONS]
        + [jax.ShapeDtypeStruct((dil, t // dil, GROUP_W), F32) for dil in DILATIONS]
        + [jax.ShapeDtypeStruct((SUBLANES, d), F32)],
        compiler_params=_params("arbitrary"),
    )(dx1, gates, yap, ybp, *[_stream_view(a, dil) for a, dil in zip(os, DILATIONS)],
      *[_stream_view(a, dil) for a, dil in zip(lses, DILATIONS)], b_gate, w_o, w_pa, w_pb)
    dgates, dyap, dybp, dya = res[:4]
    dos = [a.reshape(t, GROUP_W) for a in res[4:7]]
    dls = [a.reshape(t, GROUP_W) for a in res[7:10]]
    return dgates, dyap, dybp, dya, dos, dls, res[10]


def _attn_bwd(s, do, lse, dl, dil, carry=None):
    t = s.shape[0] * s.shape[1]
    nb = t // QBLK
    per_stream = nb // dil

    def body(q_ref, qn_ref, kc_ref, kp_ref, vc_ref, vp_ref, do_ref, don_ref, lse_ref, lsen_ref, dl_ref, dln_ref,
             ds_ref):
        b = pl.program_id(0)
        mask = _band_mask(lax.rem(b, per_stream) != 0)
        mask_n = _next_mask(lax.rem(b + 1, per_stream) != 0)
        lane, heads = _head_masks()
        q, qn, kc, vc = q_ref[...], qn_ref[...], kc_ref[...], vc_ref[...]
        do, don = do_ref[...], don_ref[...]
        lse, lsen, dl, dln = lse_ref[...], lsen_ref[...], dl_ref[...], dln_ref[...]
        k2 = jnp.concatenate([kp_ref[...], kc], axis=0)
        v2 = jnp.concatenate([vp_ref[...], vc], axis=0)

        def cols(v):
            return jnp.concatenate([jnp.sum(jnp.where(lane == h * HEAD_DIM, v, 0.0), axis=1, keepdims=True)
                                    for h in range(HEADS_PER_GROUP)], axis=0)

        def pair(qs, dos, k, v, valid, lse_c, dl_c):
            s = jnp.where(valid, _dot_nt(qs, k) * ATTN_SCALE, NEG_INF)
            p = jnp.exp(s - lse_c)
            ds = p * (_dot_nt(dos, v) - dl_c)
            return p.astype(BF16), ds.astype(BF16)

        qs, qns = _stack_heads(q, heads), _stack_heads(qn, heads)
        dos, dons = _stack_heads(do, heads), _stack_heads(don, heads)
        p_q, ds_q = pair(qs, dos, k2, v2, mask, cols(lse), cols(dl))
        p_n, ds_n = pair(qns, dons, kc, vc, mask_n, cols(lsen), cols(dln))
        dq = _merge_heads(_dot(ds_q, k2), heads)
        dk = _dot_tn(jnp.concatenate([ds_q[:, QBLK:], ds_n], axis=0), jnp.concatenate([qs, qns], axis=0))
        dv = _dot_tn(jnp.concatenate([p_q[:, QBLK:], p_n], axis=0), jnp.concatenate([dos, dons], axis=0))
        ds_ref[:, 0:GROUP_W] = (dq * ATTN_SCALE).astype(BF16)
        ds_ref[:, GROUP_W:2 * GROUP_W] = (dk * ATTN_SCALE).astype(BF16)
        ds_ref[:, 2 * GROUP_W:3 * GROUP_W] = dv.astype(BF16)

    sv = s.reshape(t, 3 * GROUP_W)
    cur, nxt = _qkv_block(0), _qkv_block(0, 1, nb)
    return _call(
        body, name=f"attn_bwd_d{dil}", grid=(nb,),
        in_specs=[cur, nxt, _qkv_block(1), _qkv_block(1, -1, nb), _qkv_block(2), _qkv_block(2, -1, nb),
                  cur, nxt, cur, nxt, cur, nxt],
        out_specs=[pl.BlockSpec((QBLK, 3 * GROUP_W), lambda b: (b, 0))],
        out_shape=[jax.ShapeDtypeStruct((t, 3 * GROUP_W), BF16)],
        args=(sv, sv, sv, sv, sv, sv, do, do, lse, lse, dl, dl), semantics=("parallel",), carry=carry)


def _conv_mixer_bwd(abcv, dya, conv_w, conv_b):
    t = abcv.shape[0]
    cw = conv_w.shape[1]
    tm = min(256, t)
    last = t // tm - 1

    def body(a_ref, ap_ref, an_ref, dya_ref, dyan_ref, cw_ref, cb_ref, d_ref, acc_ref):
        m = pl.program_id(0)

        @pl.when(m == 0)
        def _():
            acc_ref[...] = jnp.zeros_like(acc_ref)

        ab = a_ref[:, 0:cw].astype(F32)
        ac = a_ref[:, cw:2 * cw].astype(F32)
        av = a_ref[:, 2 * cw:3 * cw].astype(F32)
        u = ac * av
        hu = ap_ref[:, cw:2 * cw].astype(F32) * ap_ref[:, 2 * cw:3 * cw].astype(F32)
        hu = jnp.where(m > 0, hu, 0.0)
        u1 = _shift_down(u, hu, 1)
        u2 = _shift_down(u, hu, 2)
        cv = cw_ref[0:1, :] * u2 + cw_ref[1:2, :] * u1 + cw_ref[2:3, :] * u + cb_ref[...]
        dya_v = dya_ref[...].astype(F32)
        dcv = dya_v * ab
        ndcv = jnp.where(m < last, dyan_ref[...].astype(F32) * an_ref[:, 0:cw].astype(F32), 0.0)
        du = (cw_ref[2:3, :] * dcv + cw_ref[1:2, :] * _shift_up(dcv, ndcv, 1)
              + cw_ref[0:1, :] * _shift_up(dcv, ndcv, 2))
        d_ref[:, 0:cw] = (dya_v * cv).astype(BF16)
        d_ref[:, cw:2 * cw] = (du * av).astype(BF16)
        d_ref[:, 2 * cw:3 * cw] = (du * ac).astype(BF16)
        acc_ref[...] += _stack_rows([_colsum(dcv * u2), _colsum(dcv * u1), _colsum(dcv * u), _colsum(dcv)], cw)

    return pl.pallas_call(
        body, name="conv_mixer_bwd", grid=(t // tm,),
        in_specs=[_rows(tm, 3 * cw), _prev_halo(tm, 3 * cw), _next_halo(tm, 3 * cw, t), _rows(tm, cw),
                  _next_halo(tm, cw, t), _resident((3, cw)), _resident((1, cw))],
        out_specs=[_rows(tm, 3 * cw), _acc_spec(cw)],
        out_shape=[jax.ShapeDtypeStruct((t, 3 * cw), BF16), jax.ShapeDtypeStruct((SUBLANES, cw), F32)],
        compiler_params=_params("arbitrary"),
    )(abcv, abcv, abcv, dya, dya, conv_w, conv_b)


def _in_proj_bwd(x, dx1, dabcv, dss, dgates, w_in, g1, carry=None):
    t, d = x.shape
    qkv0 = dabcv.shape[1]
    n = w_in.shape[0]
    tm = min(256, t)

    def body(x_ref, dx1_ref, da_ref, ds0_ref, ds1_ref, ds2_ref, dg_ref, w_ref, g_ref, dx_ref, acc_ref):
        m = pl.program_id(0)

        @pl.when(m == 0)
        def _():
            acc_ref[...] = jnp.zeros_like(acc_ref)

        dh = _dot(da_ref[...], w_ref[0:qkv0, :]) + _dot(dg_ref[...], w_ref[qkv0 + 3 * ATTN_W:n, :])
        for gi, (ds_ref, dil) in enumerate(zip((ds0_ref, ds1_ref, ds2_ref), DILATIONS)):
            ds = _load_streams(ds_ref, dil, tm)
            for j in range(3):
                c0 = qkv0 + j * ATTN_W + gi * GROUP_W
                dh = dh + _dot(ds[:, j * GROUP_W:(j + 1) * GROUP_W], w_ref[c0:c0 + GROUP_W, :])
        dx, dg = _rms_bwd(x_ref[...], g_ref[...], dh)
        dx_ref[...] = dx1_ref[...] + dx
        acc_ref[...] += _stack_rows([_colsum(dg)], d)

    return _call(
        body, name="in_proj_bwd", grid=(t // tm,),
        in_specs=[_rows(tm, d), _rows(tm, d), _rows(tm, qkv0)]
        + [_stream_spec(dil, tm, 3 * GROUP_W) for dil in DILATIONS]
        + [_rows(tm, 2 * d), _resident((n, d)), _resident((1, d))],
        out_specs=[_rows(tm, d), _acc_spec(d)],
        out_shape=[jax.ShapeDtypeStruct((t, d), F32), jax.ShapeDtypeStruct((SUBLANES, d), F32)],
        args=(x, dx1, dabcv, *[_stream_view(a, dil) for a, dil in zip(dss, DILATIONS)], dgates, w_in, g1),
        semantics=("arbitrary",), carry=carry)


def _dw_in_qkv(ds, h, dil):
    t, d = h.shape
    tk = min(256, t)
    width = 3 * GROUP_W

    def body(ds_ref, h_ref, o_ref, acc_ref):
        k = pl.program_id(0)

        @pl.when(k == 0)
        def _():
            acc_ref[...] = jnp.zeros_like(acc_ref)

        acc_ref[...] += _dot_tn(_load_streams(ds_ref, dil, tk), h_ref[...])

        @pl.when(k == t // tk - 1)
        def _():
            o_ref[...] = acc_ref[...].astype(BF16)

    return pl.pallas_call(
        body, name=f"dw_in_qkv_d{dil}", grid=(t // tk,),
        in_specs=[_stream_spec(dil, tk, width), _rows(tk, d)],
        out_specs=pl.BlockSpec((width, d), lambda k: (0, 0)),
        out_shape=jax.ShapeDtypeStruct((width, d), BF16),
        scratch_shapes=[pltpu.VMEM((width, d), F32)],
        compiler_params=_params("arbitrary"),
    )(_stream_view(ds, dil), h)


def _local_step(x, target, p, late):
    cw = p["conv_a_w"].shape[1]
    (h, abcv, gates, *ss), (g_up,) = _in_proj(x, p["norm_mix_g"], p["w_in"], cw,
                                              carry=_Exchange("gather", [late["w_up"]]))
    w_up = _full_from_gathered(g_up)
    mid = ("w_proj_a", "w_proj_b", "w_out")
    (o0, lse0), g_mid = _attn_fwd(ss[0], DILATIONS[0], carry=_Exchange("gather", [late[n] for n in mid]))
    w_pa, w_pb, w_out = [_full_from_gathered(g) for g in g_mid]
    os, lses = zip((o0, lse0), *[_attn_fwd(s, dil) for s, dil in zip(ss[1:], DILATIONS[1:])])
    x1, ya, yb, yap, ybp, merged = _mixer_out(x, abcv, gates, os, lses, p["conv_a_w"], p["conv_a_b"], p["b_gate"],
                                              w_pa, w_pb, w_out)
    (h2, up_pre), (g_down,) = _up_proj(x1, p["norm_ffn_g"], w_up, carry=_Exchange("gather", [late["w_down"]]))
    w_down = _full_from_gathered(g_down)
    act, conv, dx2, acc_gf, loss = _ffn_loss(x1, up_pre, target, p["ffn_conv_w"], p["ffn_conv_b"], w_down,
                                       p["final_norm_g"])

    parts, got = {}, {}
    dup, acc_fb = _ffn_act_bwd(dx2, conv, w_down)
    parts["w_down"] = _by_destination(_tn_matmul(act, dx2, "dw_down"))
    (dpre, dx1, acc_g2, acc_fw), (got["w_down"],) = _ffn_up_bwd(dup, up_pre, x1, dx2, p["ffn_conv_w"], w_up,
                                                                p["norm_ffn_g"],
                                                                carry=_Exchange("scatter", [parts["w_down"]]))
    parts["w_up"] = _by_destination(_tn_matmul(dpre, h2, "dw_up"))
    dgates, dyap, dybp, dya, dos, dls, acc_bg = _mixer_bwd(dx1, gates, yap, ybp, os, lses, p["b_gate"], w_out,
                                                           w_pa, w_pb)
    parts["w_out"] = _by_destination(_tn_matmul(merged, dx1, "dw_out"))
    parts["w_proj_a"] = _by_destination(_tn_matmul(dyap, ya, "dw_proj_a"))
    parts["w_proj_b"] = _by_destination(_tn_matmul(dybp, yb, "dw_proj_b"))
    riders = (("w_up",), ("w_out", "w_proj_a", "w_proj_b"), ())
    dss = []
    for s, do, lse, dl, dil, names in zip(ss, dos, lses, dls, DILATIONS, riders):
        if names:
            (ds,), received = _attn_bwd(s, do, lse, dl, dil, carry=_Exchange("scatter", [parts[n] for n in names]))
            got.update(zip(names, received))
        else:
            (ds,) = _attn_bwd(s, do, lse, dl, dil)
        dss.append(ds)
    dabcv, acc_ca = _conv_mixer_bwd(abcv, dya, p["conv_a_w"], p["conv_a_b"])
    dw_s = [_dw_in_qkv(ds, h, dil) for ds, dil in zip(dss, DILATIONS)]
    dw_qkv = [w[j * GROUP_W:(j + 1) * GROUP_W] for j in range(3) for w in dw_s]
    g_w_in = jnp.concatenate([_tn_matmul(dabcv, h, "dw_in_a"), *dw_qkv, _tn_matmul(dgates, h, "dw_in_g")], axis=0)
    parts["w_in"] = _by_destination(g_w_in)
    (dx, acc_g1), (got["w_in"],) = _in_proj_bwd(x, dx1, dabcv, dss, dgates, p["w_in"], p["norm_mix_g"],
                                                carry=_Exchange("scatter", [parts["w_in"]]))
    small = dict(norm_mix_g=acc_g1[0:1], b_gate=acc_bg[0:2], conv_a_w=acc_ca[0:3], conv_a_b=acc_ca[3:4],
                 norm_ffn_g=acc_g2[0:1], ffn_conv_w=acc_fw[0:3], ffn_conv_b=acc_fb[0:1], final_norm_g=acc_gf[0:1])
    return loss[0, 0], dx, parts, got, small


def _all_gather(shards):
    n = len(shards)

    def body(*refs):
        ins, outs = refs[:n], refs[n:2 * n]
        send_sems, recv_sems, local_sems = refs[2 * n:]
        x, y, c = _mesh_pos()
        me, sibling = (x, y, c), (x, y, 1 - c)
        chips = [(1 - x, y), (x, 1 - y), (1 - x, 1 - y)]

        def copy(i, k, block, to, src=None):
            rows = outs[i].at[_dev_index(*block)]
            return pltpu.make_async_remote_copy(
                src_ref=rows if src is None else src, dst_ref=rows, send_sem=send_sems.at[i, k],
                recv_sem=recv_sems.at[i, k], device_id=to, device_id_type=MESH)

        mine, first, passed = [], [], []
        for i in range(n):
            cp = pltpu.make_async_copy(ins[i], outs[i].at[_dev_index(*me)], local_sems.at[i])
            cp.start()
            mine.append(cp)
            first.append(copy(i, 0, me, sibling, src=ins[i]))
            first += [copy(i, 1 + j, me, (*chip, c), src=ins[i]) for j, chip in enumerate(chips)]
        for cp in first:
            cp.start()
        for i in range(n):
            for j, chip in enumerate(chips):
                copy(i, 1 + j, (*chip, c), me).wait_recv()
                fw = copy(i, 4 + j, (*chip, c), sibling)
                fw.start()
                passed.append(fw)
        for i in range(n):
            copy(i, 0, sibling, me).wait_recv()
            for j, chip in enumerate(chips):
                copy(i, 4 + j, (*chip, 1 - c), me).wait_recv()
        for cp in first + passed:
            cp.wait_send()
        for cp in mine:
            cp.wait()

    return pl.pallas_call(
        body, name="all_gather_weights",
        in_specs=[ANY] * n, out_specs=[ANY] * n,
        out_shape=[jax.ShapeDtypeStruct((N_DEV,) + s.shape, s.dtype) for s in shards],
        scratch_shapes=[pltpu.SemaphoreType.DMA((n, 7)), pltpu.SemaphoreType.DMA((n, 7)),
                        pltpu.SemaphoreType.DMA((n,))],
    )(*shards)


def _all_reduce_small(v):
    r = v.shape[0]

    def body(v_ref, o_ref, gath, send_sems, recv_sems):
        x, y, c = _mesh_pos()
        me = _dev_index(x, y, c)
        gath[me] = v_ref[...]
        flips = [(kx, ky, kc) for kx in (0, 1) for ky in (0, 1) for kc in (0, 1)][1:]
        copies = []
        for k, (kx, ky, kc) in enumerate(flips):
            px = 1 - x if kx else x
            py = 1 - y if ky else y
            pc = 1 - c if kc else c
            cp = pltpu.make_async_remote_copy(
                src_ref=v_ref, dst_ref=gath.at[me], send_sem=send_sems.at[k], recv_sem=recv_sems.at[k],
                device_id=(px, py, pc), device_id_type=MESH)
            cp.start()
            copies.append((cp, _dev_index(px, py, pc)))
        for k, (cp, peer) in enumerate(copies):
            pltpu.make_async_remote_copy(
                src_ref=v_ref, dst_ref=gath.at[peer], send_sem=send_sems.at[k], recv_sem=recv_sems.at[k],
                device_id=(x, y, c), device_id_type=MESH).wait_recv()
        for cp, _ in copies:
            cp.wait_send()
        total = gath[0]
        for j in range(1, N_DEV):
            total = total + gath[j]
        o_ref[...] = total

    return pl.pallas_call(
        body, name="all_reduce_small",
        in_specs=[pl.BlockSpec(memory_space=pltpu.VMEM)], out_specs=pl.BlockSpec(memory_space=pltpu.VMEM),
        out_shape=jax.ShapeDtypeStruct((r, LANES), F32),
        scratch_shapes=[pltpu.VMEM((N_DEV, r, LANES), F32), pltpu.SemaphoreType.DMA((7,)),
                        pltpu.SemaphoreType.DMA((7,))],
    )(v)


def _adamw_math(w, g, m, v):
    m2 = ADAM_B1 * m + (1.0 - ADAM_B1) * g
    v2 = ADAM_B2 * v + (1.0 - ADAM_B2) * (g * g)
    m_hat = m2 / (1.0 - ADAM_B1 ** ADAM_STEP)
    v_hat = v2 / (1.0 - ADAM_B2 ** ADAM_STEP)
    delta = -ADAM_LR * (m_hat / (jnp.sqrt(v_hat) + ADAM_EPS) + ADAM_WD * w)
    return delta, m2, v2


def _adamw_big(w, m, v, part, got, me):
    r, c = w.shape
    tr = max(t for t in range(HALO, min(r, 512) + 1, HALO) if r % t == 0)

    def body(me_ref, w_ref, m_ref, v_ref, own_ref, *rest):
        del me_ref
        got_refs, (g_out, d_out, m_out, v_out) = rest[:N_DEV - 1], rest[N_DEV - 1:]
        g = own_ref[...].astype(F32)
        for ref in got_refs:
            g = g + ref[...].astype(F32)
        delta, m2, v2 = _adamw_math(w_ref[...], g, m_ref[...], v_ref[...])
        g_out[...] = g
        d_out[...] = delta
        m_out[...] = m2
        v_out[...] = v2

    def peer_block(k):
        return pl.BlockSpec((None, tr, c), lambda i, me_ref: (jnp.bitwise_xor(me_ref[0], k), i, 0))

    plain = pl.BlockSpec((tr, c), lambda i, me_ref: (i, 0))
    out = jax.ShapeDtypeStruct((r, c), F32)
    return pl.pallas_call(
        body, name="adamw_big",
        grid_spec=pltpu.PrefetchScalarGridSpec(
            num_scalar_prefetch=1, grid=(r // tr,),
            in_specs=[plain, plain, plain] + [peer_block(k) for k in range(N_DEV)],
            out_specs=[plain] * 4),
        out_shape=[out] * 4,
        compiler_params=_params("parallel"),
    )(me, w, m, v, part, *([got] * (N_DEV - 1)))


def _adamw_small(w, g, m, v):
    def body(w_ref, g_ref, m_ref, v_ref, d_out, m_out, v_out):
        delta, m2, v2 = _adamw_math(w_ref[...], g_ref[...], m_ref[...], v_ref[...])
        d_out[...] = delta
        m_out[...] = m2
        v_out[...] = v2

    out = jax.ShapeDtypeStruct(w.shape, F32)
    return pl.pallas_call(body, name="adamw_small", out_shape=[out] * 3)(w, g, m, v)


BIG = ("w_in", "w_proj_a", "w_proj_b", "w_out", "w_up", "w_down")
LATE = ("w_proj_a", "w_proj_b", "w_out", "w_up", "w_down")
COLUMN_SHARDED = ("w_in", "w_proj_a", "w_proj_b", "w_up")
SMALL = ("norm_mix_g", "b_gate", "conv_a_w", "conv_a_b", "norm_ffn_g", "ffn_conv_w", "ffn_conv_b", "final_norm_g")
SMALL_SHARDED = ("b_gate", "conv_a_w", "ffn_conv_w")
WEIGHTS = ("norm_mix_g", "w_in", "b_gate", "conv_a_w", "conv_a_b", "w_proj_a", "w_proj_b", "w_out", "norm_ffn_g",
           "w_up", "ffn_conv_w", "ffn_conv_b", "w_down", "final_norm_g")


def _pack(vectors, rows):
    flat = jnp.concatenate([v.reshape(-1) for v in vectors])
    return jnp.pad(flat, (0, rows * LANES - flat.shape[0])).reshape(rows, LANES)


def _packed_rows(count):
    rows = -(-count // LANES)
    return -(-rows // SUBLANES) * SUBLANES


def _unpack(packed, shapes):
    flat = packed.reshape(-1)
    out, lo = [], 0
    for s in shapes:
        size = 1
        for dim in s:
            size *= dim
        out.append(flat[lo:lo + size].reshape(s))
        lo += size
    return out


def _full_from_gathered(gathered):
    _, r, c = gathered.shape
    return gathered.reshape(N_DEV * r, c)


def _by_destination(grad):
    rr, cc = grad.shape
    return grad.reshape(N_DEV, rr // N_DEV, cc)


def _block2d(name, a):
    a = a.reshape(a.shape[-2:])
    return a.T if name in COLUMN_SHARDED else a


def kernel(x, norm_mix_g, w_in, b_gate, conv_a_w, conv_a_b, w_proj_a, w_proj_b, w_out, norm_ffn_g, w_up, ffn_conv_w, ffn_conv_b, w_down, final_norm_g, loss_target, m_norm_mix_g, m_w_in, m_b_gate, m_conv_a_w, m_conv_a_b, m_w_proj_a, m_w_proj_b, m_w_out, m_norm_ffn_g, m_w_up, m_ffn_conv_w, m_ffn_conv_b, m_w_down, m_final_norm_g, v_norm_mix_g, v_w_in, v_b_gate, v_conv_a_w, v_conv_a_b, v_w_proj_a, v_w_proj_b, v_w_out, v_norm_ffn_g, v_w_up, v_ffn_conv_w, v_ffn_conv_b, v_w_down, v_final_norm_g):
    given = dict(locals())
    shard = {n: given[n] for n in WEIGHTS}
    mom_m = {n: given["m_" + n] for n in WEIGHTS}
    mom_v = {n: given["v_" + n] for n in WEIGHTS}
    xi, yi, ci = _mesh_pos()
    me = _dev_index(xi, yi, ci)
    me1 = me.astype(jnp.int32).reshape(1)

    big2d = {n: _block2d(n, shard[n]) for n in BIG}
    small_shapes = [shard[n].shape[1:] for n in SMALL_SHARDED]
    n_small = sum(s[0] * s[1] for s in small_shapes)
    packed_small = _pack([shard[n] for n in SMALL_SHARDED], _packed_rows(n_small))
    gathered = _all_gather([big2d["w_in"].astype(BF16), packed_small])
    p = {"w_in": _full_from_gathered(gathered[0])}
    per_dev = [_unpack(gathered[-1][j], small_shapes) for j in range(N_DEV)]
    for i, n in enumerate(SMALL_SHARDED):
        p[n] = jnp.concatenate([per_dev[j][i] for j in range(N_DEV)], axis=1)
    p["norm_mix_g"], p["norm_ffn_g"] = shard["norm_mix_g"], shard["norm_ffn_g"]
    p["conv_a_b"], p["ffn_conv_b"] = shard["conv_a_b"], shard["ffn_conv_b"]
    p["final_norm_g"] = shard["final_norm_g"].reshape(1, -1)
    late = {n: big2d[n].astype(BF16) for n in LATE}

    loss_part, dx, parts, got, g_small = _local_step(x[0], loss_target[0], p, late)

    results = {}
    for n in BIG:
        outs = _adamw_big(big2d[n], _block2d(n, mom_m[n]), _block2d(n, mom_v[n]), parts[n], got[n], me1)
        results[n] = [_block2d(n, o).reshape(shard[n].shape) for o in outs]

    small_full_shapes = [g_small[n].shape for n in SMALL]
    n_vec = sum(s[0] * s[1] for s in small_full_shapes) + 1
    packed = _pack([g_small[n] for n in SMALL] + [loss_part.reshape(1)], _packed_rows(n_vec))
    reduced = _all_reduce_small(packed)
    *g_full, loss_vec = _unpack(reduced, small_full_shapes + [(1,)])
    loss = loss_vec[0]
    own_g = []
    for n, g in zip(SMALL, g_full):
        if n in SMALL_SHARDED:
            width = shard[n].shape[-1]
            g = lax.dynamic_slice_in_dim(g, me * width, width, axis=1)
        own_g.append(g.reshape(shard[n].shape))
    own_shapes = [shard[n].shape for n in SMALL]
    rows = _packed_rows(sum(g.size for g in own_g))
    small_out = _adamw_small(_pack([shard[n] for n in SMALL], rows), _pack(own_g, rows),
                             _pack([mom_m[n] for n in SMALL], rows), _pack([mom_v[n] for n in SMALL], rows))
    deltas, new_ms, new_vs = (_unpack(o, own_shapes) for o in small_out)
    for i, n in enumerate(SMALL):
        results[n] = [own_g[i], deltas[i], new_ms[i], new_vs[i]]

    grad_x = dx.reshape(x.shape)
    return (loss, grad_x, *[results[n][0] for n in WEIGHTS], *[results[n][1] for n in WEIGHTS],
            *[results[n][2] for n in WEIGHTS], *[results[n][3] for n in WEIGHTS])
```

```python
import functools

import jax
import jax.numpy as jnp
from jax import lax
from jax.experimental import pallas as pl
from jax.experimental.pallas import tpu as pltpu

F32 = jnp.float32
BF16 = jnp.bfloat16
MESH = pl.DeviceIdType.MESH

N_DEV = 8
RMS_EPS = 1e-6
NEG_INF = -1e30
N_GROUPS = 3
DILATIONS = (1, 4, 16)
HEADS_PER_GROUP = 4
HEAD_DIM = 64
GROUP_W = HEADS_PER_GROUP * HEAD_DIM
ATTN_W = N_GROUPS * GROUP_W
QBLK = 128
ATTN_SCALE = HEAD_DIM ** -0.5

ADAM_LR = 0.001
ADAM_B1 = 0.9
ADAM_B2 = 0.999
ADAM_EPS = 1e-08
ADAM_WD = 0.01
ADAM_STEP = 10

HALO = 16
LANES = 128
SUBLANES = 8
VMEM_LIMIT_BYTES = 56 * 1024 * 1024


def _params(*sem):
    return pltpu.CompilerParams(dimension_semantics=sem, vmem_limit_bytes=VMEM_LIMIT_BYTES)


def _pick_tile(n, cap):
    if n <= cap:
        return n
    best = None
    for t in range(LANES, cap + 1, LANES):
        if n % t == 0:
            best = t
    assert best is not None, (n, cap)
    return best


def _rows(tm, c, j=0):
    return pl.BlockSpec((tm, c), lambda m: (m, j))


def _prev_halo(tm, c):
    return pl.BlockSpec((HALO, c), lambda m: (jnp.maximum(m * (tm // HALO) - 1, 0), 0))


def _next_halo(tm, c, t_total):
    last = t_total // HALO - 1
    return pl.BlockSpec((HALO, c), lambda m: (jnp.minimum((m + 1) * (tm // HALO), last), 0))


def _resident(shape):
    nd = len(shape)
    return pl.BlockSpec(shape, lambda *_: (0,) * nd, pipeline_mode=pl.Buffered(1))


def _acc_spec(c):
    return pl.BlockSpec((SUBLANES, c), lambda *_: (0, 0))


def _shift_down(u, halo, k):
    edge = jnp.concatenate([halo[HALO - SUBLANES:], u[:SUBLANES]], axis=0)
    head = pltpu.roll(edge, k, 0)[SUBLANES:]
    return jnp.concatenate([head, pltpu.roll(u, k, 0)[SUBLANES:]], axis=0)


def _shift_up(u, halo, k):
    n = u.shape[0]
    edge = jnp.concatenate([u[n - SUBLANES:], halo[:SUBLANES]], axis=0)
    tail = pltpu.roll(edge, 2 * SUBLANES - k, 0)[:SUBLANES]
    return jnp.concatenate([pltpu.roll(u, n - k, 0)[:n - SUBLANES], tail], axis=0)


def _stack_rows(rows, c):
    idx = lax.broadcasted_iota(jnp.int32, (SUBLANES, c), 0)
    out = jnp.zeros((SUBLANES, c), F32)
    for i, r in enumerate(rows):
        out = out + jnp.where(idx == i, r, 0.0)
    return out


def _colsum(v):
    return jnp.sum(v, axis=0, keepdims=True)


def _sigmoid(v):
    return 0.5 * jnp.tanh(0.5 * v) + 0.5


def _rms_fwd(xv, g):
    r = lax.rsqrt(jnp.mean(xv * xv, axis=-1, keepdims=True) + RMS_EPS)
    return xv * r * g, r


def _rms_bwd(xv, g, dy):
    r = lax.rsqrt(jnp.mean(xv * xv, axis=-1, keepdims=True) + RMS_EPS)
    xn = xv * r
    dxn = dy * g
    dx = r * (dxn - xn * jnp.mean(dxn * xn, axis=-1, keepdims=True))
    return dx, dy * xn


def _dot(a, b):
    return jnp.dot(a, b, preferred_element_type=F32)


def _dot_nt(a, b):
    return lax.dot_general(a, b, (((1,), (1,)), ((), ())), preferred_element_type=F32)


def _dot_tn(a, b):
    return lax.dot_general(a, b, (((0,), (0,)), ((), ())), preferred_element_type=F32)


def _perm(dil, n, inverse=False):
    i = lax.broadcasted_iota(jnp.int32, (n, n), 0)
    j = lax.broadcasted_iota(jnp.int32, (n, n), 1)
    if inverse:
        i, j = j, i
    per = n // dil
    return (j == (i % per) * dil + i // per).astype(BF16)


def _permute_rows(pm, v):
    if v.dtype == BF16:
        return _dot(pm, v).astype(BF16)
    h1 = v.astype(BF16)
    r1 = v - h1.astype(F32)
    h2 = r1.astype(BF16)
    h3 = (r1 - h2.astype(F32)).astype(BF16)
    return _dot(pm, h1) + _dot(pm, h2) + _dot(pm, h3)


def _stream_view(a, dil):
    t, c = a.shape
    return a.reshape(dil, t // dil, c)


def _stream_spec(dil, tm, c):
    return pl.BlockSpec((dil, tm // dil, c), lambda m: (0, m, 0))


def _load_streams(ref, dil, tm):
    v = ref[...].reshape(tm, ref.shape[-1])
    return v if dil == 1 else _permute_rows(_perm(dil, tm, inverse=True), v)


def _store_streams(ref, dil, tm, v):
    if dil > 1:
        v = _permute_rows(_perm(dil, tm), v)
    ref[...] = v.reshape(ref.shape).astype(ref.dtype)


ANY = pl.BlockSpec(memory_space=pl.ANY)


def _mesh_pos():
    return lax.axis_index("x"), lax.axis_index("y"), lax.axis_index("c")


def _dev_index(px, py, pc):
    return 4 * px + 2 * py + pc


class _Exchange:
    def __init__(self, mode, arrays):
        self.mode, self.arrays = mode, list(arrays)
        n = len(self.arrays)
        if mode == "gather":
            self.out_shape = [jax.ShapeDtypeStruct((N_DEV,) + a.shape, a.dtype) for a in self.arrays]
        else:
            self.out_shape = [jax.ShapeDtypeStruct(a.shape, a.dtype) for a in self.arrays]
        self.scratch = [pltpu.SemaphoreType.DMA((n, N_DEV - 1)), pltpu.SemaphoreType.DMA((n, N_DEV - 1)),
                        pltpu.SemaphoreType.DMA((n,))]

    def _peers(self):
        x, y, c = _mesh_pos()
        flips = [(kx, ky, kc) for kx in (0, 1) for ky in (0, 1) for kc in (0, 1)][1:]
        peers = [(1 - x if kx else x, 1 - y if ky else y, 1 - c if kc else c) for kx, ky, kc in flips]
        return _dev_index(x, y, c), peers

    def _copy(self, ins, outs, sems, i, k, peer, me, sending):
        src = ins[i] if self.mode == "gather" else ins[i].at[_dev_index(*peer)]
        dst = outs[i].at[me if sending else _dev_index(*peer)]
        return pltpu.make_async_remote_copy(src_ref=src, dst_ref=dst, send_sem=sems[0].at[i, k],
                                            recv_sem=sems[1].at[i, k], device_id=peer, device_id_type=MESH)

    def _own(self, ins, outs, sems, i, me):
        return pltpu.make_async_copy(ins[i], outs[i].at[me], sems[2].at[i])

    def start(self, ins, outs, sems):
        me, peers = self._peers()
        for i in range(len(ins)):
            if self.mode == "gather":
                self._own(ins, outs, sems, i, me).start()
            for k, peer in enumerate(peers):
                self._copy(ins, outs, sems, i, k, peer, me, True).start()

    def wait(self, ins, outs, sems):
        me, peers = self._peers()
        for i in range(len(ins)):
            for k, peer in enumerate(peers):
                self._copy(ins, outs, sems, i, k, peer, me, False).wait_recv()
            for k, peer in enumerate(peers):
                self._copy(ins, outs, sems, i, k, peer, me, True).wait_send()
            if self.mode == "gather":
                self._own(ins, outs, sems, i, me).wait()


def _call(body, *, name, grid, in_specs, out_specs, out_shape, args, semantics, carry=None, scratch=()):
    if carry is None:
        return pl.pallas_call(body, name=name, grid=grid, in_specs=in_specs, out_specs=out_specs,
                              out_shape=out_shape, scratch_shapes=list(scratch),
                              compiler_params=_params(*semantics))(*args)
    n_in, n_out, n_x, n_s = len(in_specs), len(out_specs), len(carry.arrays), len(scratch)

    def carried(*refs):
        ins, x_ins = refs[:n_in], refs[n_in:n_in + n_x]
        outs = refs[n_in + n_x:n_in + n_x + n_out]
        x_outs = refs[n_in + n_x + n_out:n_in + 2 * n_x + n_out]
        own = refs[n_in + 2 * n_x + n_out:n_in + 2 * n_x + n_out + n_s]
        sems = refs[n_in + 2 * n_x + n_out + n_s:]
        first = functools.reduce(jnp.logical_and, [pl.program_id(a) == 0 for a in range(len(grid))])
        last = functools.reduce(jnp.logical_and, [pl.program_id(a) == grid[a] - 1 for a in range(len(grid))])

        @pl.when(first)
        def _():
            carry.start(x_ins, x_outs, sems)

        body(*ins, *outs, *own)

        @pl.when(last)
        def _():
            carry.wait(x_ins, x_outs, sems)

    res = pl.pallas_call(
        carried, name=name, grid=grid, in_specs=list(in_specs) + [ANY] * n_x,
        out_specs=list(out_specs) + [ANY] * n_x, out_shape=list(out_shape) + carry.out_shape,
        scratch_shapes=list(scratch) + carry.scratch, compiler_params=_params(*["arbitrary"] * len(grid)),
    )(*args, *carry.arrays)
    return list(res[:n_out]), list(res[n_out:])


def _up_proj(x, g, wt, carry=None):
    t, d = x.shape
    n = wt.shape[0]
    tm = min(256, t)

    def body(x_ref, g_ref, wt_ref, h_ref, o_ref):
        h = _rms_fwd(x_ref[...], g_ref[...])[0].astype(BF16)
        h_ref[...] = h
        o_ref[...] = _dot_nt(h, wt_ref[...]).astype(BF16)

    return _call(
        body, name="up_proj", grid=(t // tm,),
        in_specs=[_rows(tm, d), _resident((1, d)), _resident((n, d))],
        out_specs=[_rows(tm, d), _rows(tm, n)],
        out_shape=[jax.ShapeDtypeStruct((t, d), BF16), jax.ShapeDtypeStruct((t, n), BF16)],
        args=(x, g, wt), semantics=("parallel",), carry=carry)


def _in_proj(x, g, wt, cw, carry=None):
    t, d = x.shape
    n = wt.shape[0]
    tm = min(256, t)
    qkv0 = 3 * cw

    def body(x_ref, g_ref, wt_ref, h_ref, abcv_ref, gates_ref, *s_refs):
        h = _rms_fwd(x_ref[...], g_ref[...])[0].astype(BF16)
        h_ref[...] = h
        abcv_ref[...] = _dot_nt(h, wt_ref[0:qkv0, :]).astype(BF16)
        gates_ref[...] = _dot_nt(h, wt_ref[qkv0 + 3 * ATTN_W:n, :]).astype(BF16)
        for gi, s_ref in enumerate(s_refs):
            cols = [_dot_nt(h, wt_ref[qkv0 + j * ATTN_W + gi * GROUP_W:qkv0 + j * ATTN_W + (gi + 1) * GROUP_W, :])
                    for j in range(3)]
            _store_streams(s_ref, DILATIONS[gi], tm, jnp.concatenate(cols, axis=1).astype(BF16))

    return _call(
        body, name="in_proj", grid=(t // tm,),
        in_specs=[_rows(tm, d), _resident((1, d)), _resident((n, d))],
        out_specs=[_rows(tm, d), _rows(tm, qkv0), _rows(tm, 2 * d)]
        + [_stream_spec(dil, tm, 3 * GROUP_W) for dil in DILATIONS],
        out_shape=[jax.ShapeDtypeStruct((t, d), BF16), jax.ShapeDtypeStruct((t, qkv0), BF16),
                   jax.ShapeDtypeStruct((t, 2 * d), BF16)]
        + [jax.ShapeDtypeStruct((dil, t // dil, 3 * GROUP_W), BF16) for dil in DILATIONS],
        args=(x, g, wt), semantics=("parallel",), carry=carry)


def _head_masks():
    lane = lax.broadcasted_iota(jnp.int32, (1, GROUP_W), 1)
    return lane, [(lane // HEAD_DIM) == h for h in range(HEADS_PER_GROUP)]


def _stack_heads(v, heads):
    return jnp.concatenate([jnp.where(hm, v, jnp.zeros_like(v)) for hm in heads], axis=0)


def _merge_heads(v, heads):
    out = jnp.zeros((QBLK, GROUP_W), v.dtype)
    for h, hm in enumerate(heads):
        out = jnp.where(hm, v[h * QBLK:(h + 1) * QBLK], out)
    return out


def _qkv_block(col, shift=0, nb=None):
    if shift == 0:
        return pl.BlockSpec((QBLK, GROUP_W), lambda b: (b, col))
    return pl.BlockSpec((QBLK, GROUP_W), lambda b: (jnp.clip(b + shift, 0, nb - 1), col))


def _band_mask(has_prev):
    rows = HEADS_PER_GROUP * QBLK
    row = lax.broadcasted_iota(jnp.int32, (rows, 2 * QBLK), 0) & (QBLK - 1)
    col = lax.broadcasted_iota(jnp.int32, (rows, 2 * QBLK), 1)
    return ((col < QBLK) & (col >= row) & has_prev) | ((col >= QBLK) & (col - QBLK <= row))


def _next_mask(has_next):
    rows = HEADS_PER_GROUP * QBLK
    row = lax.broadcasted_iota(jnp.int32, (rows, QBLK), 0) & (QBLK - 1)
    col = lax.broadcasted_iota(jnp.int32, (rows, QBLK), 1)
    return (col >= row) & has_next


def _attn_fwd(s, dil, carry=None):
    t = s.shape[0] * s.shape[1]
    nb = t // QBLK
    per_stream = nb // dil

    def body(q_ref, kc_ref, kp_ref, vc_ref, vp_ref, o_ref, lse_ref):
        b = pl.program_id(0)
        mask = _band_mask(lax.rem(b, per_stream) != 0)
        _, heads = _head_masks()
        k2 = jnp.concatenate([kp_ref[...], kc_ref[...]], axis=0)
        v2 = jnp.concatenate([vp_ref[...], vc_ref[...]], axis=0)
        sc = jnp.where(mask, _dot_nt(_stack_heads(q_ref[...], heads), k2) * ATTN_SCALE, NEG_INF)
        mx = jnp.max(sc, axis=1, keepdims=True)
        pr = jnp.exp(sc - mx)
        den = jnp.sum(pr, axis=1, keepdims=True)
        o_all = _dot(pr.astype(BF16), v2) / den
        o_ref[...] = _merge_heads(o_all, heads).astype(BF16)
        lse_ref[...] = _merge_heads(jnp.broadcast_to(mx + jnp.log(den), o_all.shape), heads)

    sv = s.reshape(t, 3 * GROUP_W)
    return _call(
        body, name=f"attn_fwd_d{dil}", grid=(nb,),
        in_specs=[_qkv_block(0), _qkv_block(1), _qkv_block(1, -1, nb), _qkv_block(2), _qkv_block(2, -1, nb)],
        out_specs=[_qkv_block(0), _qkv_block(0)],
        out_shape=[jax.ShapeDtypeStruct((t, GROUP_W), BF16), jax.ShapeDtypeStruct((t, GROUP_W), F32)],
        args=(sv, sv, sv, sv, sv), semantics=("parallel",), carry=carry)


def _group_softmax(parts):
    mx = jnp.maximum(jnp.maximum(parts[0], parts[1]), parts[2])
    es = [jnp.exp(p - mx) for p in parts]
    den = es[0] + es[1] + es[2]
    return [e / den for e in es]


def _mixer_out(x, abcv, gates, os, lses, conv_w, conv_b, b_gate, w_pa, w_pb, w_o):
    t, d = x.shape
    cw = conv_w.shape[1]
    tm = min(256, t)

    def body(x_ref, abcv_ref, halo_ref, gates_ref, o0_ref, o1_ref, o2_ref, l0_ref, l1_ref, l2_ref, cw_ref, cb_ref,
             bg_ref, wpa_ref, wpb_ref, wo_ref, x1_ref, ya_ref, yb_ref, yap_ref, ybp_ref, mg_ref):
        m = pl.program_id(0)
        ab = abcv_ref[:, 0:cw].astype(F32)
        u = abcv_ref[:, cw:2 * cw].astype(F32) * abcv_ref[:, 2 * cw:3 * cw].astype(F32)
        hu = halo_ref[:, cw:2 * cw].astype(F32) * halo_ref[:, 2 * cw:3 * cw].astype(F32)
        hu = jnp.where(m > 0, hu, 0.0)
        cv = (cw_ref[0:1, :] * _shift_down(u, hu, 2) + cw_ref[1:2, :] * _shift_down(u, hu, 1)
              + cw_ref[2:3, :] * u + cb_ref[...])
        ya = (ab * cv).astype(BF16)
        ya_ref[...] = ya
        alphas = _group_softmax([_load_streams(r, dil, tm) for r, dil in zip((l0_ref, l1_ref, l2_ref), DILATIONS)])
        for i, (o_ref, dil) in enumerate(zip((o0_ref, o1_ref, o2_ref), DILATIONS)):
            sl = slice(i * GROUP_W, (i + 1) * GROUP_W)
            yb_ref[:, sl] = (alphas[i] * _load_streams(o_ref, dil, tm).astype(F32)).astype(BF16)
        yap = _dot_nt(ya, wpa_ref[...])
        ybp = _dot_nt(yb_ref[...], wpb_ref[...])
        yap_ref[...] = yap.astype(BF16)
        ybp_ref[...] = ybp.astype(BF16)
        sa = _sigmoid(gates_ref[:, 0:d].astype(F32) + bg_ref[0:1, :])
        sb = _sigmoid(gates_ref[:, d:2 * d].astype(F32) + bg_ref[1:2, :])
        merged = (sa * yap + sb * ybp).astype(BF16)
        mg_ref[...] = merged
        x1_ref[...] = x_ref[...] + _dot(merged, wo_ref[...])

    return pl.pallas_call(
        body, name="mixer_out", grid=(t // tm,),
        in_specs=[_rows(tm, d), _rows(tm, 3 * cw), _prev_halo(tm, 3 * cw), _rows(tm, 2 * d)]
        + [_stream_spec(dil, tm, GROUP_W) for dil in DILATIONS] * 2
        + [_resident((3, cw)), _resident((1, cw)), _resident((2, d)),
           _resident((d, cw)), _resident((d, ATTN_W)), _resident((d, d))],
        out_specs=[_rows(tm, d), _rows(tm, cw), _rows(tm, ATTN_W), _rows(tm, d), _rows(tm, d), _rows(tm, d)],
        out_shape=[jax.ShapeDtypeStruct((t, d), F32), jax.ShapeDtypeStruct((t, cw), BF16),
                   jax.ShapeDtypeStruct((t, ATTN_W), BF16), jax.ShapeDtypeStruct((t, d), BF16),
                   jax.ShapeDtypeStruct((t, d), BF16), jax.ShapeDtypeStruct((t, d), BF16)],
        compiler_params=_params("parallel"),
    )(x, abcv, abcv, gates, *[_stream_view(a, dil) for a, dil in zip(os, DILATIONS)],
      *[_stream_view(a, dil) for a, dil in zip(lses, DILATIONS)], conv_w, conv_b, b_gate, w_pa, w_pb, w_o)


def _ffn_conv(p_ref, halo_ref, w_ref, b_ref, m, c0, wd):
    p = p_ref[:, c0:c0 + wd].astype(F32)
    hp = jnp.where(m > 0, halo_ref[:, c0:c0 + wd].astype(F32), 0.0)
    return (w_ref[0:1, c0:c0 + wd] * _shift_down(p, hp, 2) + w_ref[1:2, c0:c0 + wd] * _shift_down(p, hp, 1)
            + w_ref[2:3, c0:c0 + wd] * p + b_ref[:, c0:c0 + wd])


def _ffn_loss(x1, up_pre, target, conv_w, conv_b, w_d, g_f):
    t, d = x1.shape
    dff = w_d.shape[0]
    tm = min(256, t)
    ck = _pick_tile(dff, 1408)

    def body(x1_ref, up_ref, halo_ref, tg_ref, cw_ref, cb_ref, wd_ref, gf_ref, act_ref, conv_ref, dx2_ref, acc_ref,
             loss_ref):
        m = pl.program_id(0)

        @pl.when(m == 0)
        def _():
            acc_ref[...] = jnp.zeros_like(acc_ref)
            loss_ref[...] = jnp.zeros_like(loss_ref)

        x2 = x1_ref[...]
        for c0 in range(0, dff, ck):
            gate = _ffn_conv(up_ref, halo_ref, cw_ref, cb_ref, m, c0, ck)
            val = _ffn_conv(up_ref, halo_ref, cw_ref, cb_ref, m, dff + c0, ck)
            conv_ref[:, c0:c0 + ck] = gate.astype(BF16)
            conv_ref[:, dff + c0:dff + c0 + ck] = val.astype(BF16)
            act = (gate * _sigmoid(gate) * val).astype(BF16)
            act_ref[:, c0:c0 + ck] = act
            x2 = x2 + _dot(act, wd_ref[c0:c0 + ck, :])
        y, _ = _rms_fwd(x2, gf_ref[...])
        diff = y - tg_ref[...]
        loss_ref[...] += 0.5 * jnp.sum(jnp.mean(diff * diff, axis=-1, keepdims=True))
        dx2, dg = _rms_bwd(x2, gf_ref[...], diff * (1.0 / d))
        dx2_ref[...] = dx2
        acc_ref[...] += _stack_rows([_colsum(dg)], d)

    return pl.pallas_call(
        body, name="ffn_loss", grid=(t // tm,),
        in_specs=[_rows(tm, d), _rows(tm, 2 * dff), _prev_halo(tm, 2 * dff), _rows(tm, d),
                  _resident((3, 2 * dff)), _resident((1, 2 * dff)), _resident((dff, d)), _resident((1, d))],
        out_specs=[_rows(tm, dff), _rows(tm, 2 * dff), _rows(tm, d), _acc_spec(d), _acc_spec(LANES)],
        out_shape=[jax.ShapeDtypeStruct((t, dff), BF16), jax.ShapeDtypeStruct((t, 2 * dff), BF16),
                   jax.ShapeDtypeStruct((t, d), F32), jax.ShapeDtypeStruct((SUBLANES, d), F32),
                   jax.ShapeDtypeStruct((SUBLANES, LANES), F32)],
        compiler_params=_params("arbitrary"),
    )(x1, up_pre, up_pre, target, conv_w, conv_b, w_d, g_f)


def _ffn_act_bwd(dx2, conv, w_d):
    t, d = dx2.shape
    dff = w_d.shape[0]
    tm = min(256, t)
    ck = _pick_tile(dff, 1408)

    def body(dx2_ref, conv_ref, wd_ref, dup_ref, acc_ref):
        m = pl.program_id(0)

        @pl.when(m == 0)
        def _():
            acc_ref[...] = jnp.zeros_like(acc_ref)

        dx2v = dx2_ref[...].astype(BF16)
        for c0 in range(0, dff, ck):
            dact = _dot_nt(dx2v, wd_ref[c0:c0 + ck, :])
            gate = conv_ref[:, c0:c0 + ck].astype(F32)
            val = conv_ref[:, dff + c0:dff + c0 + ck].astype(F32)
            sg = _sigmoid(gate)
            dval = dact * gate * sg
            dgate = dact * val * sg * (1.0 + gate * (1.0 - sg))
            dup_ref[:, c0:c0 + ck] = dgate.astype(BF16)
            dup_ref[:, dff + c0:dff + c0 + ck] = dval.astype(BF16)
            acc_ref[:, c0:c0 + ck] += _stack_rows([_colsum(dgate)], ck)
            acc_ref[:, dff + c0:dff + c0 + ck] += _stack_rows([_colsum(dval)], ck)

    return pl.pallas_call(
        body, name="ffn_act_bwd", grid=(t // tm,),
        in_specs=[_rows(tm, d), _rows(tm, 2 * dff), _resident((dff, d))],
        out_specs=[_rows(tm, 2 * dff), _acc_spec(2 * dff)],
        out_shape=[jax.ShapeDtypeStruct((t, 2 * dff), BF16), jax.ShapeDtypeStruct((SUBLANES, 2 * dff), F32)],
        compiler_params=_params("arbitrary"),
    )(dx2, conv, w_d)


def _ffn_up_bwd(dup, up_pre, x1, dx2, conv_w, w_u, g2, carry=None):
    t, d = x1.shape
    n = dup.shape[1]
    tm = min(256, t)
    ck = _pick_tile(n, 1408)
    last = t // tm - 1

    def body(dup_ref, nxt_ref, up_ref, x1_ref, dx2_ref, cw_ref, wu_ref, g2_ref, dpre_ref, dx1_ref, acc_ref, accw_ref):
        m = pl.program_id(0)

        @pl.when(m == 0)
        def _():
            acc_ref[...] = jnp.zeros_like(acc_ref)
            accw_ref[...] = jnp.zeros_like(accw_ref)

        dh = jnp.zeros((tm, d), F32)
        for c0 in range(0, n, ck):
            du = dup_ref[:, c0:c0 + ck].astype(F32)
            hn = jnp.where(m < last, nxt_ref[:, c0:c0 + ck].astype(F32), 0.0)
            du1 = _shift_up(du, hn, 1)
            du2 = _shift_up(du, hn, 2)
            dpre = (cw_ref[2:3, c0:c0 + ck] * du + cw_ref[1:2, c0:c0 + ck] * du1
                    + cw_ref[0:1, c0:c0 + ck] * du2).astype(BF16)
            dpre_ref[:, c0:c0 + ck] = dpre
            dh = dh + _dot(dpre, wu_ref[c0:c0 + ck, :])
            p = up_ref[:, c0:c0 + ck].astype(F32)
            accw_ref[:, c0:c0 + ck] += _stack_rows([_colsum(du2 * p), _colsum(du1 * p), _colsum(du * p)], ck)
        dx, dg = _rms_bwd(x1_ref[...], g2_ref[...], dh)
        dx1_ref[...] = dx2_ref[...] + dx
        acc_ref[...] += _stack_rows([_colsum(dg)], d)

    return _call(
        body, name="ffn_up_bwd", grid=(t // tm,),
        in_specs=[_rows(tm, n), _next_halo(tm, n, t), _rows(tm, n), _rows(tm, d), _rows(tm, d), _resident((3, n)),
                  _resident((n, d)), _resident((1, d))],
        out_specs=[_rows(tm, n), _rows(tm, d), _acc_spec(d), _acc_spec(n)],
        out_shape=[jax.ShapeDtypeStruct((t, n), BF16), jax.ShapeDtypeStruct((t, d), F32),
                   jax.ShapeDtypeStruct((SUBLANES, d), F32), jax.ShapeDtypeStruct((SUBLANES, n), F32)],
        args=(dup, dup, up_pre, x1, dx2, conv_w, w_u, g2), semantics=("arbitrary",), carry=carry)


def _tn_matmul(a, b, name):
    t, mdim = a.shape
    n = b.shape[1]
    tk = min(1024, t)
    tmm = _pick_tile(mdim, 1536)
    tn = _pick_tile(n, 1024)

    def body(a_ref, b_ref, o_ref, acc_ref):
        k = pl.program_id(2)

        @pl.when(k == 0)
        def _():
            acc_ref[...] = jnp.zeros_like(acc_ref)

        acc_ref[...] += _dot_tn(a_ref[...].astype(BF16), b_ref[...].astype(BF16))

        @pl.when(k == t // tk - 1)
        def _():
            o_ref[...] = acc_ref[...].astype(BF16)

    return pl.pallas_call(
        body, name=name, grid=(mdim // tmm, n // tn, t // tk),
        in_specs=[pl.BlockSpec((tk, tmm), lambda i, j, k: (k, i)), pl.BlockSpec((tk, tn), lambda i, j, k: (k, j))],
        out_specs=pl.BlockSpec((tmm, tn), lambda i, j, k: (i, j)),
        out_shape=jax.ShapeDtypeStruct((mdim, n), BF16),
        scratch_shapes=[pltpu.VMEM((tmm, tn), F32)],
        compiler_params=_params("parallel", "parallel", "arbitrary"),
    )(a, b)


def _mixer_bwd(dx1, gates, yap, ybp, os, lses, b_gate, w_o, w_pa, w_pb, carry):
    t, d = dx1.shape
    cw = w_pa.shape[1]
    tm = min(256, t)

    def body(dx1_ref, gates_ref, yap_ref, ybp_ref, o0_ref, o1_ref, o2_ref, l0_ref, l1_ref, l2_ref, bg_ref, wo_ref,
             wpa_ref, wpb_ref, dgates_ref, dyap_ref, dybp_ref, dya_ref, do0_ref, do1_ref, do2_ref, dl0_ref, dl1_ref,
             dl2_ref, acc_ref):
        m = pl.program_id(0)

        @pl.when(m == 0)
        def _():
            acc_ref[...] = jnp.zeros_like(acc_ref)

        dmg = _dot_nt(dx1_ref[...].astype(BF16), wo_ref[...])
        sa = _sigmoid(gates_ref[:, 0:d].astype(F32) + bg_ref[0:1, :])
        sb = _sigmoid(gates_ref[:, d:2 * d].astype(F32) + bg_ref[1:2, :])
        dyap = (dmg * sa).astype(BF16)
        dybp = (dmg * sb).astype(BF16)
        dga = dmg * yap_ref[...].astype(F32) * sa * (1.0 - sa)
        dgb = dmg * ybp_ref[...].astype(F32) * sb * (1.0 - sb)
        dyap_ref[...] = dyap
        dybp_ref[...] = dybp
        dgates_ref[:, 0:d] = dga.astype(BF16)
        dgates_ref[:, d:2 * d] = dgb.astype(BF16)
        acc_ref[...] += _stack_rows([_colsum(dga), _colsum(dgb)], d)
        dya_ref[...] = _dot(dyap, wpa_ref[...]).astype(BF16)
        dyb = _dot(dybp, wpb_ref[...])

        ri = lax.broadcasted_iota(jnp.int32, (GROUP_W, GROUP_W), 0) // HEAD_DIM
        ci = lax.broadcasted_iota(jnp.int32, (GROUP_W, GROUP_W), 1) // HEAD_DIM
        same_head = (ri == ci).astype(BF16)
        alphas = _group_softmax([_load_streams(r, dil, tm) for r, dil in zip((l0_ref, l1_ref, l2_ref), DILATIONS)])
        dtot = jnp.zeros((tm, GROUP_W), F32)
        for i, (o_ref, do_ref, dil) in enumerate(zip((o0_ref, o1_ref, o2_ref), (do0_ref, do1_ref, do2_ref), DILATIONS)):
            dov = alphas[i] * dyb[:, i * GROUP_W:(i + 1) * GROUP_W]
            _store_streams(do_ref, dil, tm, dov.astype(BF16))
            prod = dov * _load_streams(o_ref, dil, tm).astype(F32)
            hi = prod.astype(BF16)
            lo = (prod - hi.astype(F32)).astype(BF16)
            dtot = dtot + _dot(hi, same_head) + _dot(lo, same_head)
        for alpha, dl_ref, dil in zip(alphas, (dl0_ref, dl1_ref, dl2_ref), DILATIONS):
            _store_streams(dl_ref, dil, tm, alpha * dtot)

    streams = [_stream_spec(dil, tm, GROUP_W) for dil in DILATIONS]
    res, exchanged = _call(
        body, name="mixer_bwd", grid=(t // tm,),
        in_specs=[_rows(tm, d), _rows(tm, 2 * d), _rows(tm, d), _rows(tm, d)] + streams * 2
        + [_resident((2, d)), _resident((d, d)), _resident((d, cw)), _resident((d, ATTN_W))],
        out_specs=[_rows(tm, 2 * d), _rows(tm, d), _rows(tm, d), _rows(tm, cw)] + streams * 2 + [_acc_spec(d)],
        out_shape=[jax.ShapeDtypeStruct((t, 2 * d), BF16), jax.ShapeDtypeStruct((t, d), BF16),
                   jax.ShapeDtypeStruct((t, d), BF16), jax.ShapeDtypeStruct((t, cw), BF16)]
        + [jax.ShapeDtypeStruct((dil, t // dil, GROUP_W), BF16) for dil in DILATIONS]
        + [jax.ShapeDtypeStruct((dil, t // dil, GROUP_W), F32) for dil in DILATIONS]
        + [jax.ShapeDtypeStruct((SUBLANES, d), F32)],
        args=(dx1, gates, yap, ybp, *[_stream_view(a, dil) for a, dil in zip(os, DILATIONS)],
              *[_stream_view(a, dil) for a, dil in zip(lses, DILATIONS)], b_gate, w_o, w_pa, w_pb),
        semantics=("arbitrary",), carry=carry)
    dgates, dyap, dybp, dya = res[:4]
    dos = [a.reshape(t, GROUP_W) for a in res[4:7]]
    dls = [a.reshape(t, GROUP_W) for a in res[7:10]]
    return (dgates, dyap, dybp, dya, dos, dls, res[10]), exchanged


def _attn_bwd(s, do, lse, dl, dil, carry=None):
    t = s.shape[0] * s.shape[1]
    nb = t // QBLK
    per_stream = nb // dil

    def body(q_ref, qn_ref, kc_ref, kp_ref, vc_ref, vp_ref, do_ref, don_ref, lse_ref, lsen_ref, dl_ref, dln_ref,
             ds_ref):
        b = pl.program_id(0)
        mask = _band_mask(lax.rem(b, per_stream) != 0)
        mask_n = _next_mask(lax.rem(b + 1, per_stream) != 0)
        lane, heads = _head_masks()
        q, qn, kc, vc = q_ref[...], qn_ref[...], kc_ref[...], vc_ref[...]
        do, don = do_ref[...], don_ref[...]
        lse, lsen, dl, dln = lse_ref[...], lsen_ref[...], dl_ref[...], dln_ref[...]
        k2 = jnp.concatenate([kp_ref[...], kc], axis=0)
        v2 = jnp.concatenate([vp_ref[...], vc], axis=0)

        def cols(v):
            return jnp.concatenate([jnp.sum(jnp.where(lane == h * HEAD_DIM, v, 0.0), axis=1, keepdims=True)
                                    for h in range(HEADS_PER_GROUP)], axis=0)

        def pair(qs, dos, k, v, valid, lse_c, dl_c):
            s = jnp.where(valid, _dot_nt(qs, k) * ATTN_SCALE, NEG_INF)
            p = jnp.exp(s - lse_c)
            ds = p * (_dot_nt(dos, v) - dl_c)
            return p.astype(BF16), ds.astype(BF16)

        qs, qns = _stack_heads(q, heads), _stack_heads(qn, heads)
        dos, dons = _stack_heads(do, heads), _stack_heads(don, heads)
        p_q, ds_q = pair(qs, dos, k2, v2, mask, cols(lse), cols(dl))
        p_n, ds_n = pair(qns, dons, kc, vc, mask_n, cols(lsen), cols(dln))
        dq = _merge_heads(_dot(ds_q, k2), heads)
        dk = _dot_tn(jnp.concatenate([ds_q[:, QBLK:], ds_n], axis=0), jnp.concatenate([qs, qns], axis=0))
        dv = _dot_tn(jnp.concatenate([p_q[:, QBLK:], p_n], axis=0), jnp.concatenate([dos, dons], axis=0))
        ds_ref[:, 0:GROUP_W] = (dq * ATTN_SCALE).astype(BF16)
        ds_ref[:, GROUP_W:2 * GROUP_W] = (dk * ATTN_SCALE).astype(BF16)
        ds_ref[:, 2 * GROUP_W:3 * GROUP_W] = dv.astype(BF16)

    sv = s.reshape(t, 3 * GROUP_W)
    cur, nxt = _qkv_block(0), _qkv_block(0, 1, nb)
    return _call(
        body, name=f"attn_bwd_d{dil}", grid=(nb,),
        in_specs=[cur, nxt, _qkv_block(1), _qkv_block(1, -1, nb), _qkv_block(2), _qkv_block(2, -1, nb),
                  cur, nxt, cur, nxt, cur, nxt],
        out_specs=[pl.BlockSpec((QBLK, 3 * GROUP_W), lambda b: (b, 0))],
        out_shape=[jax.ShapeDtypeStruct((t, 3 * GROUP_W), BF16)],
        args=(sv, sv, sv, sv, sv, sv, do, do, lse, lse, dl, dl), semantics=("parallel",), carry=carry)


def _conv_mixer_bwd(abcv, dya, conv_w, conv_b):
    t = abcv.shape[0]
    cw = conv_w.shape[1]
    tm = min(256, t)
    last = t // tm - 1

    def body(a_ref, ap_ref, an_ref, dya_ref, dyan_ref, cw_ref, cb_ref, d_ref, acc_ref):
        m = pl.program_id(0)

        @pl.when(m == 0)
        def _():
            acc_ref[...] = jnp.zeros_like(acc_ref)

        ab = a_ref[:, 0:cw].astype(F32)
        ac = a_ref[:, cw:2 * cw].astype(F32)
        av = a_ref[:, 2 * cw:3 * cw].astype(F32)
        u = ac * av
        hu = ap_ref[:, cw:2 * cw].astype(F32) * ap_ref[:, 2 * cw:3 * cw].astype(F32)
        hu = jnp.where(m > 0, hu, 0.0)
        u1 = _shift_down(u, hu, 1)
        u2 = _shift_down(u, hu, 2)
        cv = cw_ref[0:1, :] * u2 + cw_ref[1:2, :] * u1 + cw_ref[2:3, :] * u + cb_ref[...]
        dya_v = dya_ref[...].astype(F32)
        dcv = dya_v * ab
        ndcv = jnp.where(m < last, dyan_ref[...].astype(F32) * an_ref[:, 0:cw].astype(F32), 0.0)
        du = (cw_ref[2:3, :] * dcv + cw_ref[1:2, :] * _shift_up(dcv, ndcv, 1)
              + cw_ref[0:1, :] * _shift_up(dcv, ndcv, 2))
        d_ref[:, 0:cw] = (dya_v * cv).astype(BF16)
        d_ref[:, cw:2 * cw] = (du * av).astype(BF16)
        d_ref[:, 2 * cw:3 * cw] = (du * ac).astype(BF16)
        acc_ref[...] += _stack_rows([_colsum(dcv * u2), _colsum(dcv * u1), _colsum(dcv * u), _colsum(dcv)], cw)

    return pl.pallas_call(
        body, name="conv_mixer_bwd", grid=(t // tm,),
        in_specs=[_rows(tm, 3 * cw), _prev_halo(tm, 3 * cw), _next_halo(tm, 3 * cw, t), _rows(tm, cw),
                  _next_halo(tm, cw, t), _resident((3, cw)), _resident((1, cw))],
        out_specs=[_rows(tm, 3 * cw), _acc_spec(cw)],
        out_shape=[jax.ShapeDtypeStruct((t, 3 * cw), BF16), jax.ShapeDtypeStruct((SUBLANES, cw), F32)],
        compiler_params=_params("arbitrary"),
    )(abcv, abcv, abcv, dya, dya, conv_w, conv_b)


def _in_proj_bwd(x, dx1, dabcv, dss, dgates, w_in, g1, carry=None):
    t, d = x.shape
    qkv0 = dabcv.shape[1]
    n = w_in.shape[0]
    tm = min(256, t)

    def body(x_ref, dx1_ref, da_ref, ds0_ref, ds1_ref, ds2_ref, dg_ref, w_ref, g_ref, dx_ref, acc_ref):
        m = pl.program_id(0)

        @pl.when(m == 0)
        def _():
            acc_ref[...] = jnp.zeros_like(acc_ref)

        dh = _dot(da_ref[...], w_ref[0:qkv0, :]) + _dot(dg_ref[...], w_ref[qkv0 + 3 * ATTN_W:n, :])
        for gi, (ds_ref, dil) in enumerate(zip((ds0_ref, ds1_ref, ds2_ref), DILATIONS)):
            ds = _load_streams(ds_ref, dil, tm)
            for j in range(3):
                c0 = qkv0 + j * ATTN_W + gi * GROUP_W
                dh = dh + _dot(ds[:, j * GROUP_W:(j + 1) * GROUP_W], w_ref[c0:c0 + GROUP_W, :])
        dx, dg = _rms_bwd(x_ref[...], g_ref[...], dh)
        dx_ref[...] = dx1_ref[...] + dx
        acc_ref[...] += _stack_rows([_colsum(dg)], d)

    return _call(
        body, name="in_proj_bwd", grid=(t // tm,),
        in_specs=[_rows(tm, d), _rows(tm, d), _rows(tm, qkv0)]
        + [_stream_spec(dil, tm, 3 * GROUP_W) for dil in DILATIONS]
        + [_rows(tm, 2 * d), _resident((n, d)), _resident((1, d))],
        out_specs=[_rows(tm, d), _acc_spec(d)],
        out_shape=[jax.ShapeDtypeStruct((t, d), F32), jax.ShapeDtypeStruct((SUBLANES, d), F32)],
        args=(x, dx1, dabcv, *[_stream_view(a, dil) for a, dil in zip(dss, DILATIONS)], dgates, w_in, g1),
        semantics=("arbitrary",), carry=carry)


def _dw_in_qkv(ds, h, dil):
    t, d = h.shape
    tk = min(1024, t)
    sub = min(256, t)
    width = 3 * GROUP_W

    def body(ds_ref, h_ref, o_ref, acc_ref):
        k = pl.program_id(0)

        @pl.when(k == 0)
        def _():
            acc_ref[...] = jnp.zeros_like(acc_ref)

        upd = None
        for i in range(tk // sub):
            rows = ds_ref[:, i * (sub // dil):(i + 1) * (sub // dil), :].reshape(sub, width)
            if dil > 1:
                rows = _permute_rows(_perm(dil, sub, inverse=True), rows)
            term = _dot_tn(rows, h_ref[i * sub:(i + 1) * sub, :])
            upd = term if upd is None else upd + term
        acc_ref[...] += upd

        @pl.when(k == t // tk - 1)
        def _():
            o_ref[...] = acc_ref[...].astype(BF16)

    return pl.pallas_call(
        body, name=f"dw_in_qkv_d{dil}", grid=(t // tk,),
        in_specs=[_stream_spec(dil, tk, width), _rows(tk, d)],
        out_specs=pl.BlockSpec((width, d), lambda k: (0, 0)),
        out_shape=jax.ShapeDtypeStruct((width, d), BF16),
        scratch_shapes=[pltpu.VMEM((width, d), F32)],
        compiler_params=_params("arbitrary"),
    )(_stream_view(ds, dil), h)


def _local_step(x, target, p, late):
    cw = p["conv_a_w"].shape[1]
    (h, abcv, gates, *ss), (g_up,) = _in_proj(x, p["norm_mix_g"], p["w_in"], cw,
                                              carry=_Exchange("gather", [late["w_up"]]))
    w_up = _full_from_gathered(g_up)
    mid = ("w_proj_a", "w_proj_b", "w_out")
    (o0, lse0), g_mid = _attn_fwd(ss[0], DILATIONS[0], carry=_Exchange("gather", [late[n] for n in mid]))
    w_pa, w_pb, w_out = [_full_from_gathered(g) for g in g_mid]
    os, lses = zip((o0, lse0), *[_attn_fwd(s, dil) for s, dil in zip(ss[1:], DILATIONS[1:])])
    x1, ya, yb, yap, ybp, merged = _mixer_out(x, abcv, gates, os, lses, p["conv_a_w"], p["conv_a_b"], p["b_gate"],
                                              w_pa, w_pb, w_out)
    (h2, up_pre), (g_down,) = _up_proj(x1, p["norm_ffn_g"], w_up, carry=_Exchange("gather", [late["w_down"]]))
    w_down = _full_from_gathered(g_down)
    act, conv, dx2, acc_gf, loss = _ffn_loss(x1, up_pre, target, p["ffn_conv_w"], p["ffn_conv_b"], w_down,
                                       p["final_norm_g"])

    parts, got = {}, {}
    dup, acc_fb = _ffn_act_bwd(dx2, conv, w_down)
    parts["w_down"] = _by_destination(_tn_matmul(act, dx2, "dw_down"))
    (dpre, dx1, acc_g2, acc_fw), (got["w_down"],) = _ffn_up_bwd(dup, up_pre, x1, dx2, p["ffn_conv_w"], w_up,
                                                                p["norm_ffn_g"],
                                                                carry=_Exchange("scatter", [parts["w_down"]]))
    parts["w_up"] = _by_destination(_tn_matmul(dpre, h2, "dw_up"))
    (dgates, dyap, dybp, dya, dos, dls, acc_bg), (got["w_up"],) = _mixer_bwd(
        dx1, gates, yap, ybp, os, lses, p["b_gate"], w_out, w_pa, w_pb,
        carry=_Exchange("scatter", [parts["w_up"]]))
    parts["w_out"] = _by_destination(_tn_matmul(merged, dx1, "dw_out"))
    parts["w_proj_a"] = _by_destination(_tn_matmul(dyap, ya, "dw_proj_a"))
    parts["w_proj_b"] = _by_destination(_tn_matmul(dybp, yb, "dw_proj_b"))
    riders = (("w_out", "w_proj_a", "w_proj_b"), (), ())
    dss = []
    for s, do, lse, dl, dil, names in zip(ss, dos, lses, dls, DILATIONS, riders):
        if names:
            (ds,), received = _attn_bwd(s, do, lse, dl, dil, carry=_Exchange("scatter", [parts[n] for n in names]))
            got.update(zip(names, received))
        else:
            (ds,) = _attn_bwd(s, do, lse, dl, dil)
        dss.append(ds)
    dabcv, acc_ca = _conv_mixer_bwd(abcv, dya, p["conv_a_w"], p["conv_a_b"])
    dw_s = [_dw_in_qkv(ds, h, dil) for ds, dil in zip(dss, DILATIONS)]
    dw_qkv = [w[j * GROUP_W:(j + 1) * GROUP_W] for j in range(3) for w in dw_s]
    g_w_in = jnp.concatenate([_tn_matmul(dabcv, h, "dw_in_a"), *dw_qkv, _tn_matmul(dgates, h, "dw_in_g")], axis=0)
    parts["w_in"] = _by_destination(g_w_in)
    (dx, acc_g1), (got["w_in"],) = _in_proj_bwd(x, dx1, dabcv, dss, dgates, p["w_in"], p["norm_mix_g"],
                                                carry=_Exchange("scatter", [parts["w_in"]]))
    small = dict(norm_mix_g=acc_g1[0:1], b_gate=acc_bg[0:2], conv_a_w=acc_ca[0:3], conv_a_b=acc_ca[3:4],
                 norm_ffn_g=acc_g2[0:1], ffn_conv_w=acc_fw[0:3], ffn_conv_b=acc_fb[0:1], final_norm_g=acc_gf[0:1])
    return loss[0, 0], dx, parts, got, small


def _all_gather(shards):
    n = len(shards)

    def body(*refs):
        ins, outs = refs[:n], refs[n:2 * n]
        send_sems, recv_sems, local_sems = refs[2 * n:]
        x, y, c = _mesh_pos()
        me, sibling = (x, y, c), (x, y, 1 - c)
        chips = [(1 - x, y), (x, 1 - y), (1 - x, 1 - y)]

        def copy(i, k, block, to, src=None):
            rows = outs[i].at[_dev_index(*block)]
            return pltpu.make_async_remote_copy(
                src_ref=rows if src is None else src, dst_ref=rows, send_sem=send_sems.at[i, k],
                recv_sem=recv_sems.at[i, k], device_id=to, device_id_type=MESH)

        mine, first, passed = [], [], []
        for i in range(n):
            cp = pltpu.make_async_copy(ins[i], outs[i].at[_dev_index(*me)], local_sems.at[i])
            cp.start()
            mine.append(cp)
            first.append(copy(i, 0, me, sibling, src=ins[i]))
            first += [copy(i, 1 + j, me, (*chip, c), src=ins[i]) for j, chip in enumerate(chips)]
        for cp in first:
            cp.start()
        for i in range(n):
            for j, chip in enumerate(chips):
                copy(i, 1 + j, (*chip, c), me).wait_recv()
                fw = copy(i, 4 + j, (*chip, c), sibling)
                fw.start()
                passed.append(fw)
        for i in range(n):
            copy(i, 0, sibling, me).wait_recv()
            for j, chip in enumerate(chips):
                copy(i, 4 + j, (*chip, 1 - c), me).wait_recv()
        for cp in first + passed:
            cp.wait_send()
        for cp in mine:
            cp.wait()

    return pl.pallas_call(
        body, name="all_gather_weights",
        in_specs=[ANY] * n, out_specs=[ANY] * n,
        out_shape=[jax.ShapeDtypeStruct((N_DEV,) + s.shape, s.dtype) for s in shards],
        scratch_shapes=[pltpu.SemaphoreType.DMA((n, 7)), pltpu.SemaphoreType.DMA((n, 7)),
                        pltpu.SemaphoreType.DMA((n,))],
    )(*shards)


def _all_reduce_small(v):
    r = v.shape[0]

    def body(v_ref, o_ref, gath, send_sems, recv_sems):
        x, y, c = _mesh_pos()
        me = _dev_index(x, y, c)
        gath[me] = v_ref[...]
        flips = [(kx, ky, kc) for kx in (0, 1) for ky in (0, 1) for kc in (0, 1)][1:]
        copies = []
        for k, (kx, ky, kc) in enumerate(flips):
            px = 1 - x if kx else x
            py = 1 - y if ky else y
            pc = 1 - c if kc else c
            cp = pltpu.make_async_remote_copy(
                src_ref=v_ref, dst_ref=gath.at[me], send_sem=send_sems.at[k], recv_sem=recv_sems.at[k],
                device_id=(px, py, pc), device_id_type=MESH)
            cp.start()
            copies.append((cp, _dev_index(px, py, pc)))
        for k, (cp, peer) in enumerate(copies):
            pltpu.make_async_remote_copy(
                src_ref=v_ref, dst_ref=gath.at[peer], send_sem=send_sems.at[k], recv_sem=recv_sems.at[k],
                device_id=(x, y, c), device_id_type=MESH).wait_recv()
        for cp, _ in copies:
            cp.wait_send()
        total = gath[0]
        for j in range(1, N_DEV):
            total = total + gath[j]
        o_ref[...] = total

    return pl.pallas_call(
        body, name="all_reduce_small",
        in_specs=[pl.BlockSpec(memory_space=pltpu.VMEM)], out_specs=pl.BlockSpec(memory_space=pltpu.VMEM),
        out_shape=jax.ShapeDtypeStruct((r, LANES), F32),
        scratch_shapes=[pltpu.VMEM((N_DEV, r, LANES), F32), pltpu.SemaphoreType.DMA((7,)),
                        pltpu.SemaphoreType.DMA((7,))],
    )(v)


def _adamw_math(w, g, m, v):
    m2 = ADAM_B1 * m + (1.0 - ADAM_B1) * g
    v2 = ADAM_B2 * v + (1.0 - ADAM_B2) * (g * g)
    m_hat = m2 / (1.0 - ADAM_B1 ** ADAM_STEP)
    v_hat = v2 / (1.0 - ADAM_B2 ** ADAM_STEP)
    delta = -ADAM_LR * (m_hat / (jnp.sqrt(v_hat) + ADAM_EPS) + ADAM_WD * w)
    return delta, m2, v2


def _adamw_big(w, m, v, part, got, me):
    r, c = w.shape
    tr = max(t for t in range(HALO, min(r, 512) + 1, HALO) if r % t == 0)

    def body(me_ref, w_ref, m_ref, v_ref, own_ref, *rest):
        del me_ref
        got_refs, (g_out, d_out, m_out, v_out) = rest[:N_DEV - 1], rest[N_DEV - 1:]
        g = own_ref[...].astype(F32)
        for ref in got_refs:
            g = g + ref[...].astype(F32)
        delta, m2, v2 = _adamw_math(w_ref[...], g, m_ref[...], v_ref[...])
        g_out[...] = g
        d_out[...] = delta
        m_out[...] = m2
        v_out[...] = v2

    def peer_block(k):
        return pl.BlockSpec((None, tr, c), lambda i, me_ref: (jnp.bitwise_xor(me_ref[0], k), i, 0))

    plain = pl.BlockSpec((tr, c), lambda i, me_ref: (i, 0))
    out = jax.ShapeDtypeStruct((r, c), F32)
    return pl.pallas_call(
        body, name="adamw_big",
        grid_spec=pltpu.PrefetchScalarGridSpec(
            num_scalar_prefetch=1, grid=(r // tr,),
            in_specs=[plain, plain, plain] + [peer_block(k) for k in range(N_DEV)],
            out_specs=[plain] * 4),
        out_shape=[out] * 4,
        compiler_params=_params("parallel"),
    )(me, w, m, v, part, *([got] * (N_DEV - 1)))


def _adamw_small(w, g, m, v):
    def body(w_ref, g_ref, m_ref, v_ref, d_out, m_out, v_out):
        delta, m2, v2 = _adamw_math(w_ref[...], g_ref[...], m_ref[...], v_ref[...])
        d_out[...] = delta
        m_out[...] = m2
        v_out[...] = v2

    out = jax.ShapeDtypeStruct(w.shape, F32)
    return pl.pallas_call(body, name="adamw_small", out_shape=[out] * 3)(w, g, m, v)


BIG = ("w_in", "w_proj_a", "w_proj_b", "w_out", "w_up", "w_down")
LATE = ("w_proj_a", "w_proj_b", "w_out", "w_up", "w_down")
COLUMN_SHARDED = ("w_in", "w_proj_a", "w_proj_b", "w_up")
SMALL = ("norm_mix_g", "b_gate", "conv_a_w", "conv_a_b", "norm_ffn_g", "ffn_conv_w", "ffn_conv_b", "final_norm_g")
SMALL_SHARDED = ("b_gate", "conv_a_w", "ffn_conv_w")
WEIGHTS = ("norm_mix_g", "w_in", "b_gate", "conv_a_w", "conv_a_b", "w_proj_a", "w_proj_b", "w_out", "norm_ffn_g",
           "w_up", "ffn_conv_w", "ffn_conv_b", "w_down", "final_norm_g")


def _pack(vectors, rows):
    flat = jnp.concatenate([v.reshape(-1) for v in vectors])
    return jnp.pad(flat, (0, rows * LANES - flat.shape[0])).reshape(rows, LANES)


def _packed_rows(count):
    rows = -(-count // LANES)
    return -(-rows // SUBLANES) * SUBLANES


def _unpack(packed, shapes):
    flat = packed.reshape(-1)
    out, lo = [], 0
    for s in shapes:
        size = 1
        for dim in s:
            size *= dim
        out.append(flat[lo:lo + size].reshape(s))
        lo += size
    return out


def _full_from_gathered(gathered):
    _, r, c = gathered.shape
    return gathered.reshape(N_DEV * r, c)


def _by_destination(grad):
    rr, cc = grad.shape
    return grad.reshape(N_DEV, rr // N_DEV, cc)


def _block2d(name, a):
    a = a.reshape(a.shape[-2:])
    return a.T if name in COLUMN_SHARDED else a


def kernel(x, norm_mix_g, w_in, b_gate, conv_a_w, conv_a_b, w_proj_a, w_proj_b, w_out, norm_ffn_g, w_up, ffn_conv_w, ffn_conv_b, w_down, final_norm_g, loss_target, m_norm_mix_g, m_w_in, m_b_gate, m_conv_a_w, m_conv_a_b, m_w_proj_a, m_w_proj_b, m_w_out, m_norm_ffn_g, m_w_up, m_ffn_conv_w, m_ffn_conv_b, m_w_down, m_final_norm_g, v_norm_mix_g, v_w_in, v_b_gate, v_conv_a_w, v_conv_a_b, v_w_proj_a, v_w_proj_b, v_w_out, v_norm_ffn_g, v_w_up, v_ffn_conv_w, v_ffn_conv_b, v_w_down, v_final_norm_g):
    given = dict(locals())
    shard = {n: given[n] for n in WEIGHTS}
    mom_m = {n: given["m_" + n] for n in WEIGHTS}
    mom_v = {n: given["v_" + n] for n in WEIGHTS}
    xi, yi, ci = _mesh_pos()
    me = _dev_index(xi, yi, ci)
    me1 = me.astype(jnp.int32).reshape(1)

    big2d = {n: _block2d(n, shard[n]) for n in BIG}
    small_shapes = [shard[n].shape[1:] for n in SMALL_SHARDED]
    n_small = sum(s[0] * s[1] for s in small_shapes)
    packed_small = _pack([shard[n] for n in SMALL_SHARDED], _packed_rows(n_small))
    gathered = _all_gather([big2d["w_in"].astype(BF16), packed_small])
    p = {"w_in": _full_from_gathered(gathered[0])}
    per_dev = [_unpack(gathered[-1][j], small_shapes) for j in range(N_DEV)]
    for i, n in enumerate(SMALL_SHARDED):
        p[n] = jnp.concatenate([per_dev[j][i] for j in range(N_DEV)], axis=1)
    p["norm_mix_g"], p["norm_ffn_g"] = shard["norm_mix_g"], shard["norm_ffn_g"]
    p["conv_a_b"], p["ffn_conv_b"] = shard["conv_a_b"], shard["ffn_conv_b"]
    p["final_norm_g"] = shard["final_norm_g"].reshape(1, -1)
    late = {n: big2d[n].astype(BF16) for n in LATE}

    loss_part, dx, parts, got, g_small = _local_step(x[0], loss_target[0], p, late)

    results = {}
    for n in BIG:
        outs = _adamw_big(big2d[n], _block2d(n, mom_m[n]), _block2d(n, mom_v[n]), parts[n], got[n], me1)
        results[n] = [_block2d(n, o).reshape(shard[n].shape) for o in outs]

    small_full_shapes = [g_small[n].shape for n in SMALL]
    n_vec = sum(s[0] * s[1] for s in small_full_shapes) + 1
    packed = _pack([g_small[n] for n in SMALL] + [loss_part.reshape(1)], _packed_rows(n_vec))
    reduced = _all_reduce_small(packed)
    *g_full, loss_vec = _unpack(reduced, small_full_shapes + [(1,)])
    loss = loss_vec[0]
    own_g = []
    for n, g in zip(SMALL, g_full):
        if n in SMALL_SHARDED:
            width = shard[n].shape[-1]
            g = lax.dynamic_slice_in_dim(g, me * width, width, axis=1)
        own_g.append(g.reshape(shard[n].shape))
    own_shapes = [shard[n].shape for n in SMALL]
    rows = _packed_rows(sum(g.size for g in own_g))
    small_out = _adamw_small(_pack([shard[n] for n in SMALL], rows), _pack(own_g, rows),
                             _pack([mom_m[n] for n in SMALL], rows), _pack([mom_v[n] for n in SMALL], rows))
    deltas, new_ms, new_vs = (_unpack(o, own_shapes) for o in small_out)
    for i, n in enumerate(SMALL):
        results[n] = [own_g[i], deltas[i], new_ms[i], new_vs[i]]

    grad_x = dx.reshape(x.shape)
    return (loss, grad_x, *[results[n][0] for n in WEIGHTS], *[results[n][1] for n in WEIGHTS],
            *[results[n][2] for n in WEIGHTS], *[results[n][3] for n in WEIGHTS])
```

```python
import functools

import jax
import jax.numpy as jnp
from jax import lax
from jax.experimental import pallas as pl
from jax.experimental.pallas import tpu as pltpu

F32 = jnp.float32
BF16 = jnp.bfloat16
MESH = pl.DeviceIdType.MESH

N_DEV = 8
RMS_EPS = 1e-6
NEG_INF = -1e30
N_GROUPS = 3
DILATIONS = (1, 4, 16)
HEADS_PER_GROUP = 4
HEAD_DIM = 64
GROUP_W = HEADS_PER_GROUP * HEAD_DIM
ATTN_W = N_GROUPS * GROUP_W
QBLK = 128
ATTN_SCALE = HEAD_DIM ** -0.5

ADAM_LR = 0.001
ADAM_B1 = 0.9
ADAM_B2 = 0.999
ADAM_EPS = 1e-08
ADAM_WD = 0.01
ADAM_STEP = 10

PERM_TOKENS = 256
ROWS_MATMUL = 512
HALO = 16
LANES = 128
SUBLANES = 8
VMEM_LIMIT_BYTES = 56 * 1024 * 1024


def _params(*sem):
    return pltpu.CompilerParams(dimension_semantics=sem, vmem_limit_bytes=VMEM_LIMIT_BYTES)


def _pick_tile(n, cap):
    if n <= cap:
        return n
    best = None
    for t in range(LANES, cap + 1, LANES):
        if n % t == 0:
            best = t
    assert best is not None, (n, cap)
    return best


def _rows(tm, c, j=0):
    return pl.BlockSpec((tm, c), lambda m: (m, j))


def _prev_halo(tm, c):
    return pl.BlockSpec((HALO, c), lambda m: (jnp.maximum(m * (tm // HALO) - 1, 0), 0))


def _next_halo(tm, c, t_total):
    last = t_total // HALO - 1
    return pl.BlockSpec((HALO, c), lambda m: (jnp.minimum((m + 1) * (tm // HALO), last), 0))


def _resident(shape):
    nd = len(shape)
    return pl.BlockSpec(shape, lambda *_: (0,) * nd, pipeline_mode=pl.Buffered(1))


def _acc_spec(c):
    return pl.BlockSpec((SUBLANES, c), lambda *_: (0, 0))


def _shift_down(u, halo, k):
    edge = jnp.concatenate([halo[HALO - SUBLANES:], u[:SUBLANES]], axis=0)
    head = pltpu.roll(edge, k, 0)[SUBLANES:]
    return jnp.concatenate([head, pltpu.roll(u, k, 0)[SUBLANES:]], axis=0)


def _shift_up(u, halo, k):
    n = u.shape[0]
    edge = jnp.concatenate([u[n - SUBLANES:], halo[:SUBLANES]], axis=0)
    tail = pltpu.roll(edge, 2 * SUBLANES - k, 0)[:SUBLANES]
    return jnp.concatenate([pltpu.roll(u, n - k, 0)[:n - SUBLANES], tail], axis=0)


def _stack_rows(rows, c):
    idx = lax.broadcasted_iota(jnp.int32, (SUBLANES, c), 0)
    out = jnp.zeros((SUBLANES, c), F32)
    for i, r in enumerate(rows):
        out = out + jnp.where(idx == i, r, 0.0)
    return out


def _colsum(v):
    return jnp.sum(v, axis=0, keepdims=True)


def _sigmoid(v):
    return 0.5 * jnp.tanh(0.5 * v) + 0.5


def _rms_fwd(xv, g):
    r = lax.rsqrt(jnp.mean(xv * xv, axis=-1, keepdims=True) + RMS_EPS)
    return xv * r * g, r


def _rms_bwd(xv, g, dy):
    r = lax.rsqrt(jnp.mean(xv * xv, axis=-1, keepdims=True) + RMS_EPS)
    xn = xv * r
    dxn = dy * g
    dx = r * (dxn - xn * jnp.mean(dxn * xn, axis=-1, keepdims=True))
    return dx, dy * xn


def _dot(a, b):
    return jnp.dot(a, b, preferred_element_type=F32)


def _dot_nt(a, b):
    return lax.dot_general(a, b, (((1,), (1,)), ((), ())), preferred_element_type=F32)


def _dot_tn(a, b):
    return lax.dot_general(a, b, (((0,), (0,)), ((), ())), preferred_element_type=F32)


def _perm(dil, n, inverse=False):
    i = lax.broadcasted_iota(jnp.int32, (n, n), 0)
    j = lax.broadcasted_iota(jnp.int32, (n, n), 1)
    if inverse:
        i, j = j, i
    per = n // dil
    return (j == (i % per) * dil + i // per).astype(BF16)


def _permute_rows(pm, v):
    if v.dtype == BF16:
        return _dot(pm, v).astype(BF16)
    hi = v.astype(BF16)
    lo = (v - hi.astype(F32)).astype(BF16)
    return _dot(pm, hi) + _dot(pm, lo)


def _stream_view(a, dil):
    t, c = a.shape
    return a.reshape(dil, t // dil, c)


def _stream_spec(dil, tm, c):
    return pl.BlockSpec((dil, tm // dil, c), lambda m: (0, m, 0))


def _load_streams(ref, dil, tm):
    c = ref.shape[-1]
    if dil == 1:
        return ref[...].reshape(tm, c)
    sub = min(PERM_TOKENS, tm)
    pm = _perm(dil, sub, inverse=True)
    parts = [_permute_rows(pm, ref[:, i * (sub // dil):(i + 1) * (sub // dil), :].reshape(sub, c))
             for i in range(tm // sub)]
    return parts[0] if len(parts) == 1 else jnp.concatenate(parts, axis=0)


def _store_streams(ref, dil, tm, v):
    if dil == 1:
        ref[...] = v.reshape(ref.shape).astype(ref.dtype)
        return
    sub = min(PERM_TOKENS, tm)
    pm = _perm(dil, sub)
    for i in range(tm // sub):
        piece = _permute_rows(pm, v[i * sub:(i + 1) * sub])
        ref[:, i * (sub // dil):(i + 1) * (sub // dil), :] = piece.reshape(dil, sub // dil, -1).astype(ref.dtype)


ANY = pl.BlockSpec(memory_space=pl.ANY)


def _mesh_pos():
    return lax.axis_index("x"), lax.axis_index("y"), lax.axis_index("c")


def _dev_index(px, py, pc):
    return 4 * px + 2 * py + pc


class _Exchange:
    def __init__(self, mode, arrays):
        self.mode, self.arrays = mode, list(arrays)
        n = len(self.arrays)
        if mode == "gather":
            self.out_shape = [jax.ShapeDtypeStruct((N_DEV,) + a.shape, a.dtype) for a in self.arrays]
        else:
            self.out_shape = [jax.ShapeDtypeStruct(a.shape, a.dtype) for a in self.arrays]
        self.scratch = [pltpu.SemaphoreType.DMA((n, N_DEV - 1)), pltpu.SemaphoreType.DMA((n, N_DEV - 1)),
                        pltpu.SemaphoreType.DMA((n,))]

    def _peers(self):
        x, y, c = _mesh_pos()
        flips = [(kx, ky, kc) for kx in (0, 1) for ky in (0, 1) for kc in (0, 1)][1:]
        peers = [(1 - x if kx else x, 1 - y if ky else y, 1 - c if kc else c) for kx, ky, kc in flips]
        return _dev_index(x, y, c), peers

    def _copy(self, ins, outs, sems, i, k, peer, me, sending):
        src = ins[i] if self.mode == "gather" else ins[i].at[_dev_index(*peer)]
        dst = outs[i].at[me if sending else _dev_index(*peer)]
        return pltpu.make_async_remote_copy(src_ref=src, dst_ref=dst, send_sem=sems[0].at[i, k],
                                            recv_sem=sems[1].at[i, k], device_id=peer, device_id_type=MESH)

    def _own(self, ins, outs, sems, i, me):
        return pltpu.make_async_copy(ins[i], outs[i].at[me], sems[2].at[i])

    def start(self, ins, outs, sems):
        me, peers = self._peers()
        for i in range(len(ins)):
            if self.mode == "gather":
                self._own(ins, outs, sems, i, me).start()
            for k, peer in enumerate(peers):
                self._copy(ins, outs, sems, i, k, peer, me, True).start()

    def wait(self, ins, outs, sems):
        me, peers = self._peers()
        for i in range(len(ins)):
            for k, peer in enumerate(peers):
                self._copy(ins, outs, sems, i, k, peer, me, False).wait_recv()
            for k, peer in enumerate(peers):
                self._copy(ins, outs, sems, i, k, peer, me, True).wait_send()
            if self.mode == "gather":
                self._own(ins, outs, sems, i, me).wait()


def _call(body, *, name, grid, in_specs, out_specs, out_shape, args, semantics, carry=None, scratch=()):
    if carry is None:
        return pl.pallas_call(body, name=name, grid=grid, in_specs=in_specs, out_specs=out_specs,
                              out_shape=out_shape, scratch_shapes=list(scratch),
                              compiler_params=_params(*semantics))(*args)
    n_in, n_out, n_x, n_s = len(in_specs), len(out_specs), len(carry.arrays), len(scratch)

    def carried(*refs):
        ins, x_ins = refs[:n_in], refs[n_in:n_in + n_x]
        outs = refs[n_in + n_x:n_in + n_x + n_out]
        x_outs = refs[n_in + n_x + n_out:n_in + 2 * n_x + n_out]
        own = refs[n_in + 2 * n_x + n_out:n_in + 2 * n_x + n_out + n_s]
        sems = refs[n_in + 2 * n_x + n_out + n_s:]
        first = functools.reduce(jnp.logical_and, [pl.program_id(a) == 0 for a in range(len(grid))])
        last = functools.reduce(jnp.logical_and, [pl.program_id(a) == grid[a] - 1 for a in range(len(grid))])

        @pl.when(first)
        def _():
            carry.start(x_ins, x_outs, sems)

        body(*ins, *outs, *own)

        @pl.when(last)
        def _():
            carry.wait(x_ins, x_outs, sems)

    res = pl.pallas_call(
        carried, name=name, grid=grid, in_specs=list(in_specs) + [ANY] * n_x,
        out_specs=list(out_specs) + [ANY] * n_x, out_shape=list(out_shape) + carry.out_shape,
        scratch_shapes=list(scratch) + carry.scratch, compiler_params=_params(*["arbitrary"] * len(grid)),
    )(*args, *carry.arrays)
    return list(res[:n_out]), list(res[n_out:])


def _up_proj(x, g, wt, carry=None):
    t, d = x.shape
    n = wt.shape[0]
    tm = min(256, t)

    def body(x_ref, g_ref, wt_ref, h_ref, o_ref):
        h = _rms_fwd(x_ref[...], g_ref[...])[0].astype(BF16)
        h_ref[...] = h
        o_ref[...] = _dot_nt(h, wt_ref[...]).astype(BF16)

    return _call(
        body, name="up_proj", grid=(t // tm,),
        in_specs=[_rows(tm, d), _resident((1, d)), _resident((n, d))],
        out_specs=[_rows(tm, d), _rows(tm, n)],
        out_shape=[jax.ShapeDtypeStruct((t, d), BF16), jax.ShapeDtypeStruct((t, n), BF16)],
        args=(x, g, wt), semantics=("parallel",), carry=carry)


def _in_proj(x, g, wt, cw, carry=None):
    t, d = x.shape
    n = wt.shape[0]
    tm = min(ROWS_MATMUL, t)
    qkv0 = 3 * cw

    def body(x_ref, g_ref, wt_ref, h_ref, abcv_ref, gates_ref, *s_refs):
        h = _rms_fwd(x_ref[...], g_ref[...])[0].astype(BF16)
        h_ref[...] = h
        abcv_ref[...] = _dot_nt(h, wt_ref[0:qkv0, :]).astype(BF16)
        gates_ref[...] = _dot_nt(h, wt_ref[qkv0 + 3 * ATTN_W:n, :]).astype(BF16)
        for gi, s_ref in enumerate(s_refs):
            cols = [_dot_nt(h, wt_ref[qkv0 + j * ATTN_W + gi * GROUP_W:qkv0 + j * ATTN_W + (gi + 1) * GROUP_W, :])
                    for j in range(3)]
            _store_streams(s_ref, DILATIONS[gi], tm, jnp.concatenate(cols, axis=1).astype(BF16))

    return _call(
        body, name="in_proj", grid=(t // tm,),
        in_specs=[_rows(tm, d), _resident((1, d)), _resident((n, d))],
        out_specs=[_rows(tm, d), _rows(tm, qkv0), _rows(tm, 2 * d)]
        + [_stream_spec(dil, tm, 3 * GROUP_W) for dil in DILATIONS],
        out_shape=[jax.ShapeDtypeStruct((t, d), BF16), jax.ShapeDtypeStruct((t, qkv0), BF16),
                   jax.ShapeDtypeStruct((t, 2 * d), BF16)]
        + [jax.ShapeDtypeStruct((dil, t // dil, 3 * GROUP_W), BF16) for dil in DILATIONS],
        args=(x, g, wt), semantics=("parallel",), carry=carry)


def _head_masks():
    lane = lax.broadcasted_iota(jnp.int32, (1, GROUP_W), 1)
    return lane, [(lane // HEAD_DIM) == h for h in range(HEADS_PER_GROUP)]


def _stack_heads(v, heads):
    return jnp.concatenate([jnp.where(hm, v, jnp.zeros_like(v)) for hm in heads], axis=0)


def _merge_heads(v, heads):
    out = jnp.zeros((QBLK, GROUP_W), v.dtype)
    for h, hm in enumerate(heads):
        out = jnp.where(hm, v[h * QBLK:(h + 1) * QBLK], out)
    return out


def _qkv_block(col, shift=0, nb=None):
    if shift == 0:
        return pl.BlockSpec((QBLK, GROUP_W), lambda b: (b, col))
    return pl.BlockSpec((QBLK, GROUP_W), lambda b: (jnp.clip(b + shift, 0, nb - 1), col))


def _band_mask(has_prev):
    rows = HEADS_PER_GROUP * QBLK
    row = lax.broadcasted_iota(jnp.int32, (rows, 2 * QBLK), 0) & (QBLK - 1)
    col = lax.broadcasted_iota(jnp.int32, (rows, 2 * QBLK), 1)
    return ((col < QBLK) & (col >= row) & has_prev) | ((col >= QBLK) & (col - QBLK <= row))


def _next_mask(has_next):
    rows = HEADS_PER_GROUP * QBLK
    row = lax.broadcasted_iota(jnp.int32, (rows, QBLK), 0) & (QBLK - 1)
    col = lax.broadcasted_iota(jnp.int32, (rows, QBLK), 1)
    return (col >= row) & has_next


def _attn_fwd(s, dil, carry=None):
    t = s.shape[0] * s.shape[1]
    nb = t // QBLK
    per_stream = nb // dil

    def body(q_ref, kc_ref, kp_ref, vc_ref, vp_ref, o_ref, lse_ref):
        b = pl.program_id(0)
        mask = _band_mask(lax.rem(b, per_stream) != 0)
        _, heads = _head_masks()
        k2 = jnp.concatenate([kp_ref[...], kc_ref[...]], axis=0)
        v2 = jnp.concatenate([vp_ref[...], vc_ref[...]], axis=0)
        sc = jnp.where(mask, _dot_nt(_stack_heads(q_ref[...], heads), k2) * ATTN_SCALE, NEG_INF)
        mx = jnp.max(sc, axis=1, keepdims=True)
        pr = jnp.exp(sc - mx)
        den = jnp.sum(pr, axis=1, keepdims=True)
        o_all = _dot(pr.astype(BF16), v2) / den
        o_ref[...] = _merge_heads(o_all, heads).astype(BF16)
        lse_ref[...] = _merge_heads(jnp.broadcast_to(mx + jnp.log(den), o_all.shape), heads)

    sv = s.reshape(t, 3 * GROUP_W)
    return _call(
        body, name=f"attn_fwd_d{dil}", grid=(nb,),
        in_specs=[_qkv_block(0), _qkv_block(1), _qkv_block(1, -1, nb), _qkv_block(2), _qkv_block(2, -1, nb)],
        out_specs=[_qkv_block(0), _qkv_block(0)],
        out_shape=[jax.ShapeDtypeStruct((t, GROUP_W), BF16), jax.ShapeDtypeStruct((t, GROUP_W), F32)],
        args=(sv, sv, sv, sv, sv), semantics=("parallel",), carry=carry)


def _group_softmax(parts):
    mx = jnp.maximum(jnp.maximum(parts[0], parts[1]), parts[2])
    es = [jnp.exp(p - mx) for p in parts]
    den = es[0] + es[1] + es[2]
    return [e / den for e in es]


def _mixer_out(x, abcv, gates, os, lses, conv_w, conv_b, b_gate, w_pa, w_pb, w_o):
    t, d = x.shape
    cw = conv_w.shape[1]
    tm = min(256, t)

    def body(x_ref, abcv_ref, halo_ref, gates_ref, o0_ref, o1_ref, o2_ref, l0_ref, l1_ref, l2_ref, cw_ref, cb_ref,
             bg_ref, wpa_ref, wpb_ref, wo_ref, x1_ref, ya_ref, yb_ref, yap_ref, ybp_ref, mg_ref):
        m = pl.program_id(0)
        ab = abcv_ref[:, 0:cw].astype(F32)
        u = abcv_ref[:, cw:2 * cw].astype(F32) * abcv_ref[:, 2 * cw:3 * cw].astype(F32)
        hu = halo_ref[:, cw:2 * cw].astype(F32) * halo_ref[:, 2 * cw:3 * cw].astype(F32)
        hu = jnp.where(m > 0, hu, 0.0)
        cv = (cw_ref[0:1, :] * _shift_down(u, hu, 2) + cw_ref[1:2, :] * _shift_down(u, hu, 1)
              + cw_ref[2:3, :] * u + cb_ref[...])
        ya = (ab * cv).astype(BF16)
        ya_ref[...] = ya
        alphas = _group_softmax([_load_streams(r, dil, tm) for r, dil in zip((l0_ref, l1_ref, l2_ref), DILATIONS)])
        for i, (o_ref, dil) in enumerate(zip((o0_ref, o1_ref, o2_ref), DILATIONS)):
            sl = slice(i * GROUP_W, (i + 1) * GROUP_W)
            yb_ref[:, sl] = (alphas[i] * _load_streams(o_ref, dil, tm).astype(F32)).astype(BF16)
        yap = _dot_nt(ya, wpa_ref[...])
        ybp = _dot_nt(yb_ref[...], wpb_ref[...])
        yap_ref[...] = yap.astype(BF16)
        ybp_ref[...] = ybp.astype(BF16)
        sa = _sigmoid(gates_ref[:, 0:d].astype(F32) + bg_ref[0:1, :])
        sb = _sigmoid(gates_ref[:, d:2 * d].astype(F32) + bg_ref[1:2, :])
        merged = (sa * yap + sb * ybp).astype(BF16)
        mg_ref[...] = merged
        x1_ref[...] = x_ref[...] + _dot(merged, wo_ref[...])

    return pl.pallas_call(
        body, name="mixer_out", grid=(t // tm,),
        in_specs=[_rows(tm, d), _rows(tm, 3 * cw), _prev_halo(tm, 3 * cw), _rows(tm, 2 * d)]
        + [_stream_spec(dil, tm, GROUP_W) for dil in DILATIONS] * 2
        + [_resident((3, cw)), _resident((1, cw)), _resident((2, d)),
           _resident((d, cw)), _resident((d, ATTN_W)), _resident((d, d))],
        out_specs=[_rows(tm, d), _rows(tm, cw), _rows(tm, ATTN_W), _rows(tm, d), _rows(tm, d), _rows(tm, d)],
        out_shape=[jax.ShapeDtypeStruct((t, d), F32), jax.ShapeDtypeStruct((t, cw), BF16),
                   jax.ShapeDtypeStruct((t, ATTN_W), BF16), jax.ShapeDtypeStruct((t, d), BF16),
                   jax.ShapeDtypeStruct((t, d), BF16), jax.ShapeDtypeStruct((t, d), BF16)],
        compiler_params=_params("parallel"),
    )(x, abcv, abcv, gates, *[_stream_view(a, dil) for a, dil in zip(os, DILATIONS)],
      *[_stream_view(a, dil) for a, dil in zip(lses, DILATIONS)], conv_w, conv_b, b_gate, w_pa, w_pb, w_o)


def _ffn_conv(p_ref, halo_ref, w_ref, b_ref, m, c0, wd):
    p = p_ref[:, c0:c0 + wd].astype(F32)
    hp = jnp.where(m > 0, halo_ref[:, c0:c0 + wd].astype(F32), 0.0)
    return (w_ref[0:1, c0:c0 + wd] * _shift_down(p, hp, 2) + w_ref[1:2, c0:c0 + wd] * _shift_down(p, hp, 1)
            + w_ref[2:3, c0:c0 + wd] * p + b_ref[:, c0:c0 + wd])


def _ffn_loss(x1, up_pre, target, conv_w, conv_b, w_d, g_f):
    t, d = x1.shape
    dff = w_d.shape[0]
    tm = min(256, t)
    ck = _pick_tile(dff, 1408)

    def body(x1_ref, up_ref, halo_ref, tg_ref, cw_ref, cb_ref, wd_ref, gf_ref, act_ref, conv_ref, dx2_ref, acc_ref,
             loss_ref):
        m = pl.program_id(0)

        @pl.when(m == 0)
        def _():
            acc_ref[...] = jnp.zeros_like(acc_ref)
            loss_ref[...] = jnp.zeros_like(loss_ref)

        x2 = x1_ref[...]
        for c0 in range(0, dff, ck):
            gate = _ffn_conv(up_ref, halo_ref, cw_ref, cb_ref, m, c0, ck)
            val = _ffn_conv(up_ref, halo_ref, cw_ref, cb_ref, m, dff + c0, ck)
            conv_ref[:, c0:c0 + ck] = gate.astype(BF16)
            conv_ref[:, dff + c0:dff + c0 + ck] = val.astype(BF16)
            act = (gate * _sigmoid(gate) * val).astype(BF16)
            act_ref[:, c0:c0 + ck] = act
            x2 = x2 + _dot(act, wd_ref[c0:c0 + ck, :])
        y, _ = _rms_fwd(x2, gf_ref[...])
        diff = y - tg_ref[...]
        loss_ref[...] += 0.5 * jnp.sum(jnp.mean(diff * diff, axis=-1, keepdims=True))
        dx2, dg = _rms_bwd(x2, gf_ref[...], diff * (1.0 / d))
        dx2_ref[...] = dx2
        acc_ref[...] += _stack_rows([_colsum(dg)], d)

    return pl.pallas_call(
        body, name="ffn_loss", grid=(t // tm,),
        in_specs=[_rows(tm, d), _rows(tm, 2 * dff), _prev_halo(tm, 2 * dff), _rows(tm, d),
                  _resident((3, 2 * dff)), _resident((1, 2 * dff)), _resident((dff, d)), _resident((1, d))],
        out_specs=[_rows(tm, dff), _rows(tm, 2 * dff), _rows(tm, d), _acc_spec(d), _acc_spec(LANES)],
        out_shape=[jax.ShapeDtypeStruct((t, dff), BF16), jax.ShapeDtypeStruct((t, 2 * dff), BF16),
                   jax.ShapeDtypeStruct((t, d), F32), jax.ShapeDtypeStruct((SUBLANES, d), F32),
                   jax.ShapeDtypeStruct((SUBLANES, LANES), F32)],
        compiler_params=_params("arbitrary"),
    )(x1, up_pre, up_pre, target, conv_w, conv_b, w_d, g_f)


def _ffn_act_bwd(dx2, conv, w_d):
    t, d = dx2.shape
    dff = w_d.shape[0]
    tm = min(256, t)
    ck = _pick_tile(dff, 1408)

    def body(dx2_ref, conv_ref, wd_ref, dup_ref, acc_ref):
        m = pl.program_id(0)

        @pl.when(m == 0)
        def _():
            acc_ref[...] = jnp.zeros_like(acc_ref)

        dx2v = dx2_ref[...].astype(BF16)
        for c0 in range(0, dff, ck):
            dact = _dot_nt(dx2v, wd_ref[c0:c0 + ck, :])
            gate = conv_ref[:, c0:c0 + ck].astype(F32)
            val = conv_ref[:, dff + c0:dff + c0 + ck].astype(F32)
            sg = _sigmoid(gate)
            dval = dact * gate * sg
            dgate = dact * val * sg * (1.0 + gate * (1.0 - sg))
            dup_ref[:, c0:c0 + ck] = dgate.astype(BF16)
            dup_ref[:, dff + c0:dff + c0 + ck] = dval.astype(BF16)
            acc_ref[:, c0:c0 + ck] += _stack_rows([_colsum(dgate)], ck)
            acc_ref[:, dff + c0:dff + c0 + ck] += _stack_rows([_colsum(dval)], ck)

    return pl.pallas_call(
        body, name="ffn_act_bwd", grid=(t // tm,),
        in_specs=[_rows(tm, d), _rows(tm, 2 * dff), _resident((dff, d))],
        out_specs=[_rows(tm, 2 * dff), _acc_spec(2 * dff)],
        out_shape=[jax.ShapeDtypeStruct((t, 2 * dff), BF16), jax.ShapeDtypeStruct((SUBLANES, 2 * dff), F32)],
        compiler_params=_params("arbitrary"),
    )(dx2, conv, w_d)


def _ffn_up_bwd(dup, up_pre, x1, dx2, conv_w, w_u, g2, carry=None):
    t, d = x1.shape
    n = dup.shape[1]
    tm = min(256, t)
    ck = _pick_tile(n, 1408)
    last = t // tm - 1

    def body(dup_ref, nxt_ref, up_ref, x1_ref, dx2_ref, cw_ref, wu_ref, g2_ref, dpre_ref, dx1_ref, acc_ref, accw_ref):
        m = pl.program_id(0)

        @pl.when(m == 0)
        def _():
            acc_ref[...] = jnp.zeros_like(acc_ref)
            accw_ref[...] = jnp.zeros_like(accw_ref)

        dh = jnp.zeros((tm, d), F32)
        for c0 in range(0, n, ck):
            du = dup_ref[:, c0:c0 + ck].astype(F32)
            hn = jnp.where(m < last, nxt_ref[:, c0:c0 + ck].astype(F32), 0.0)
            du1 = _shift_up(du, hn, 1)
            du2 = _shift_up(du, hn, 2)
            dpre = (cw_ref[2:3, c0:c0 + ck] * du + cw_ref[1:2, c0:c0 + ck] * du1
                    + cw_ref[0:1, c0:c0 + ck] * du2).astype(BF16)
            dpre_ref[:, c0:c0 + ck] = dpre
            dh = dh + _dot(dpre, wu_ref[c0:c0 + ck, :])
            p = up_ref[:, c0:c0 + ck].astype(F32)
            accw_ref[:, c0:c0 + ck] += _stack_rows([_colsum(du2 * p), _colsum(du1 * p), _colsum(du * p)], ck)
        dx, dg = _rms_bwd(x1_ref[...], g2_ref[...], dh)
        dx1_ref[...] = dx2_ref[...] + dx
        acc_ref[...] += _stack_rows([_colsum(dg)], d)

    return _call(
        body, name="ffn_up_bwd", grid=(t // tm,),
        in_specs=[_rows(tm, n), _next_halo(tm, n, t), _rows(tm, n), _rows(tm, d), _rows(tm, d), _resident((3, n)),
                  _resident((n, d)), _resident((1, d))],
        out_specs=[_rows(tm, n), _rows(tm, d), _acc_spec(d), _acc_spec(n)],
        out_shape=[jax.ShapeDtypeStruct((t, n), BF16), jax.ShapeDtypeStruct((t, d), F32),
                   jax.ShapeDtypeStruct((SUBLANES, d), F32), jax.ShapeDtypeStruct((SUBLANES, n), F32)],
        args=(dup, dup, up_pre, x1, dx2, conv_w, w_u, g2), semantics=("arbitrary",), carry=carry)


def _tn_matmul(a, b, name):
    t, mdim = a.shape
    n = b.shape[1]
    tk = min(1024, t)
    tmm = _pick_tile(mdim, 1536)
    tn = _pick_tile(n, 1024)

    def body(a_ref, b_ref, o_ref, acc_ref):
        k = pl.program_id(2)

        @pl.when(k == 0)
        def _():
            acc_ref[...] = jnp.zeros_like(acc_ref)

        acc_ref[...] += _dot_tn(a_ref[...].astype(BF16), b_ref[...].astype(BF16))

        @pl.when(k == t // tk - 1)
        def _():
            o_ref[...] = acc_ref[...].astype(BF16)

    return pl.pallas_call(
        body, name=name, grid=(mdim // tmm, n // tn, t // tk),
        in_specs=[pl.BlockSpec((tk, tmm), lambda i, j, k: (k, i)), pl.BlockSpec((tk, tn), lambda i, j, k: (k, j))],
        out_specs=pl.BlockSpec((tmm, tn), lambda i, j, k: (i, j)),
        out_shape=jax.ShapeDtypeStruct((mdim, n), BF16),
        scratch_shapes=[pltpu.VMEM((tmm, tn), F32)],
        compiler_params=_params("parallel", "parallel", "arbitrary"),
    )(a, b)


def _mixer_bwd(dx1, gates, yap, ybp, os, lses, b_gate, w_o, w_pa, w_pb, carry):
    t, d = dx1.shape
    cw = w_pa.shape[1]
    tm = min(256, t)

    def body(dx1_ref, gates_ref, yap_ref, ybp_ref, o0_ref, o1_ref, o2_ref, l0_ref, l1_ref, l2_ref, bg_ref, wo_ref,
             wpa_ref, wpb_ref, dgates_ref, dyap_ref, dybp_ref, dya_ref, do0_ref, do1_ref, do2_ref, dl0_ref, dl1_ref,
             dl2_ref, acc_ref):
        m = pl.program_id(0)

        @pl.when(m == 0)
        def _():
            acc_ref[...] = jnp.zeros_like(acc_ref)

        dmg = _dot_nt(dx1_ref[...].astype(BF16), wo_ref[...])
        sa = _sigmoid(gates_ref[:, 0:d].astype(F32) + bg_ref[0:1, :])
        sb = _sigmoid(gates_ref[:, d:2 * d].astype(F32) + bg_ref[1:2, :])
        dyap = (dmg * sa).astype(BF16)
        dybp = (dmg * sb).astype(BF16)
        dga = dmg * yap_ref[...].astype(F32) * sa * (1.0 - sa)
        dgb = dmg * ybp_ref[...].astype(F32) * sb * (1.0 - sb)
        dyap_ref[...] = dyap
        dybp_ref[...] = dybp
        dgates_ref[:, 0:d] = dga.astype(BF16)
        dgates_ref[:, d:2 * d] = dgb.astype(BF16)
        acc_ref[...] += _stack_rows([_colsum(dga), _colsum(dgb)], d)
        dya_ref[...] = _dot(dyap, wpa_ref[...]).astype(BF16)
        dyb = _dot(dybp, wpb_ref[...])

        ri = lax.broadcasted_iota(jnp.int32, (GROUP_W, GROUP_W), 0) // HEAD_DIM
        ci = lax.broadcasted_iota(jnp.int32, (GROUP_W, GROUP_W), 1) // HEAD_DIM
        same_head = (ri == ci).astype(BF16)
        alphas = _group_softmax([_load_streams(r, dil, tm) for r, dil in zip((l0_ref, l1_ref, l2_ref), DILATIONS)])
        prod = jnp.zeros((tm, GROUP_W), F32)
        for i, (o_ref, do_ref, dil) in enumerate(zip((o0_ref, o1_ref, o2_ref), (do0_ref, do1_ref, do2_ref), DILATIONS)):
            dov = alphas[i] * dyb[:, i * GROUP_W:(i + 1) * GROUP_W]
            _store_streams(do_ref, dil, tm, dov.astype(BF16))
            prod = prod + dov * _load_streams(o_ref, dil, tm).astype(F32)
        hi = prod.astype(BF16)
        lo = (prod - hi.astype(F32)).astype(BF16)
        dtot = _dot(hi, same_head) + _dot(lo, same_head)
        for alpha, dl_ref, dil in zip(alphas, (dl0_ref, dl1_ref, dl2_ref), DILATIONS):
            _store_streams(dl_ref, dil, tm, alpha * dtot)

    streams = [_stream_spec(dil, tm, GROUP_W) for dil in DILATIONS]
    res, exchanged = _call(
        body, name="mixer_bwd", grid=(t // tm,),
        in_specs=[_rows(tm, d), _rows(tm, 2 * d), _rows(tm, d), _rows(tm, d)] + streams * 2
        + [_resident((2, d)), _resident((d, d)), _resident((d, cw)), _resident((d, ATTN_W))],
        out_specs=[_rows(tm, 2 * d), _rows(tm, d), _rows(tm, d), _rows(tm, cw)] + streams * 2 + [_acc_spec(d)],
        out_shape=[jax.ShapeDtypeStruct((t, 2 * d), BF16), jax.ShapeDtypeStruct((t, d), BF16),
                   jax.ShapeDtypeStruct((t, d), BF16), jax.ShapeDtypeStruct((t, cw), BF16)]
        + [jax.ShapeDtypeStruct((dil, t // dil, GROUP_W), BF16) for dil in DILATIONS]
        + [jax.ShapeDtypeStruct((dil, t // dil, GROUP_W), F32) for dil in DILATIONS]
        + [jax.ShapeDtypeStruct((SUBLANES, d), F32)],
        args=(dx1, gates, yap, ybp, *[_stream_view(a, dil) for a, dil in zip(os, DILATIONS)],
              *[_stream_view(a, dil) for a, dil in zip(lses, DILATIONS)], b_gate, w_o, w_pa, w_pb),
        semantics=("arbitrary",), carry=carry)
    dgates, dyap, dybp, dya = res[:4]
    dos = [a.reshape(t, GROUP_W) for a in res[4:7]]
    dls = [a.reshape(t, GROUP_W) for a in res[7:10]]
    return (dgates, dyap, dybp, dya, dos, dls, res[10]), exchanged


def _attn_bwd(s, do, lse, dl, dil, carry=None):
    t = s.shape[0] * s.shape[1]
    nb = t // QBLK
    per_stream = nb // dil

    def body(q_ref, qn_ref, kc_ref, kp_ref, vc_ref, vp_ref, do_ref, don_ref, lse_ref, lsen_ref, dl_ref, dln_ref,
             ds_ref):
        b = pl.program_id(0)
        mask = _band_mask(lax.rem(b, per_stream) != 0)
        mask_n = _next_mask(lax.rem(b + 1, per_stream) != 0)
        lane, heads = _head_masks()
        q, qn, kc, vc = q_ref[...], qn_ref[...], kc_ref[...], vc_ref[...]
        do, don = do_ref[...], don_ref[...]
        lse, lsen, dl, dln = lse_ref[...], lsen_ref[...], dl_ref[...], dln_ref[...]
        k2 = jnp.concatenate([kp_ref[...], kc], axis=0)
        v2 = jnp.concatenate([vp_ref[...], vc], axis=0)

        def cols(v):
            return jnp.concatenate([jnp.sum(jnp.where(lane == h * HEAD_DIM, v, 0.0), axis=1, keepdims=True)
                                    for h in range(HEADS_PER_GROUP)], axis=0)

        def pair(qs, dos, k, v, valid, lse_c, dl_c):
            s = jnp.where(valid, _dot_nt(qs, k) * ATTN_SCALE, NEG_INF)
            p = jnp.exp(s - lse_c)
            ds = p * (_dot_nt(dos, v) - dl_c)
            return p.astype(BF16), ds.astype(BF16)

        qs, qns = _stack_heads(q, heads), _stack_heads(qn, heads)
        dos, dons = _stack_heads(do, heads), _stack_heads(don, heads)
        p_q, ds_q = pair(qs, dos, k2, v2, mask, cols(lse), cols(dl))
        p_n, ds_n = pair(qns, dons, kc, vc, mask_n, cols(lsen), cols(dln))
        dq = _merge_heads(_dot(ds_q, k2), heads)
        dk = _dot_tn(jnp.concatenate([ds_q[:, QBLK:], ds_n], axis=0), jnp.concatenate([qs, qns], axis=0))
        dv = _dot_tn(jnp.concatenate([p_q[:, QBLK:], p_n], axis=0), jnp.concatenate([dos, dons], axis=0))
        ds_ref[:, 0:GROUP_W] = (dq * ATTN_SCALE).astype(BF16)
        ds_ref[:, GROUP_W:2 * GROUP_W] = (dk * ATTN_SCALE).astype(BF16)
        ds_ref[:, 2 * GROUP_W:3 * GROUP_W] = dv.astype(BF16)

    sv = s.reshape(t, 3 * GROUP_W)
    cur, nxt = _qkv_block(0), _qkv_block(0, 1, nb)
    return _call(
        body, name=f"attn_bwd_d{dil}", grid=(nb,),
        in_specs=[cur, nxt, _qkv_block(1), _qkv_block(1, -1, nb), _qkv_block(2), _qkv_block(2, -1, nb),
                  cur, nxt, cur, nxt, cur, nxt],
        out_specs=[pl.BlockSpec((QBLK, 3 * GROUP_W), lambda b: (b, 0))],
        out_shape=[jax.ShapeDtypeStruct((t, 3 * GROUP_W), BF16)],
        args=(sv, sv, sv, sv, sv, sv, do, do, lse, lse, dl, dl), semantics=("parallel",), carry=carry)


def _conv_mixer_bwd(abcv, dya, conv_w, conv_b):
    t = abcv.shape[0]
    cw = conv_w.shape[1]
    tm = min(256, t)
    last = t // tm - 1

    def body(a_ref, ap_ref, an_ref, dya_ref, dyan_ref, cw_ref, cb_ref, d_ref, acc_ref):
        m = pl.program_id(0)

        @pl.when(m == 0)
        def _():
            acc_ref[...] = jnp.zeros_like(acc_ref)

        ab = a_ref[:, 0:cw].astype(F32)
        ac = a_ref[:, cw:2 * cw].astype(F32)
        av = a_ref[:, 2 * cw:3 * cw].astype(F32)
        u = ac * av
        hu = ap_ref[:, cw:2 * cw].astype(F32) * ap_ref[:, 2 * cw:3 * cw].astype(F32)
        hu = jnp.where(m > 0, hu, 0.0)
        u1 = _shift_down(u, hu, 1)
        u2 = _shift_down(u, hu, 2)
        cv = cw_ref[0:1, :] * u2 + cw_ref[1:2, :] * u1 + cw_ref[2:3, :] * u + cb_ref[...]
        dya_v = dya_ref[...].astype(F32)
        dcv = dya_v * ab
        ndcv = jnp.where(m < last, dyan_ref[...].astype(F32) * an_ref[:, 0:cw].astype(F32), 0.0)
        du = (cw_ref[2:3, :] * dcv + cw_ref[1:2, :] * _shift_up(dcv, ndcv, 1)
              + cw_ref[0:1, :] * _shift_up(dcv, ndcv, 2))
        d_ref[:, 0:cw] = (dya_v * cv).astype(BF16)
        d_ref[:, cw:2 * cw] = (du * av).astype(BF16)
        d_ref[:, 2 * cw:3 * cw] = (du * ac).astype(BF16)
        acc_ref[...] += _stack_rows([_colsum(dcv * u2), _colsum(dcv * u1), _colsum(dcv * u), _colsum(dcv)], cw)

    return pl.pallas_call(
        body, name="conv_mixer_bwd", grid=(t // tm,),
        in_specs=[_rows(tm, 3 * cw), _prev_halo(tm, 3 * cw), _next_halo(tm, 3 * cw, t), _rows(tm, cw),
                  _next_halo(tm, cw, t), _resident((3, cw)), _resident((1, cw))],
        out_specs=[_rows(tm, 3 * cw), _acc_spec(cw)],
        out_shape=[jax.ShapeDtypeStruct((t, 3 * cw), BF16), jax.ShapeDtypeStruct((SUBLANES, cw), F32)],
        compiler_params=_params("arbitrary"),
    )(abcv, abcv, abcv, dya, dya, conv_w, conv_b)


def _in_proj_bwd(x, dx1, dabcv, dss, dgates, w_in, g1, carry=None):
    t, d = x.shape
    qkv0 = dabcv.shape[1]
    n = w_in.shape[0]
    tm = min(ROWS_MATMUL, t)

    def body(x_ref, dx1_ref, da_ref, ds0_ref, ds1_ref, ds2_ref, dg_ref, w_ref, g_ref, dx_ref, acc_ref):
        m = pl.program_id(0)

        @pl.when(m == 0)
        def _():
            acc_ref[...] = jnp.zeros_like(acc_ref)

        dh = _dot(da_ref[...], w_ref[0:qkv0, :]) + _dot(dg_ref[...], w_ref[qkv0 + 3 * ATTN_W:n, :])
        for gi, (ds_ref, dil) in enumerate(zip((ds0_ref, ds1_ref, ds2_ref), DILATIONS)):
            ds = _load_streams(ds_ref, dil, tm)
            for j in range(3):
                c0 = qkv0 + j * ATTN_W + gi * GROUP_W
                dh = dh + _dot(ds[:, j * GROUP_W:(j + 1) * GROUP_W], w_ref[c0:c0 + GROUP_W, :])
        dx, dg = _rms_bwd(x_ref[...], g_ref[...], dh)
        dx_ref[...] = dx1_ref[...] + dx
        acc_ref[...] += _stack_rows([_colsum(dg)], d)

    return _call(
        body, name="in_proj_bwd", grid=(t // tm,),
        in_specs=[_rows(tm, d), _rows(tm, d), _rows(tm, qkv0)]
        + [_stream_spec(dil, tm, 3 * GROUP_W) for dil in DILATIONS]
        + [_rows(tm, 2 * d), _resident((n, d)), _resident((1, d))],
        out_specs=[_rows(tm, d), _acc_spec(d)],
        out_shape=[jax.ShapeDtypeStruct((t, d), F32), jax.ShapeDtypeStruct((SUBLANES, d), F32)],
        args=(x, dx1, dabcv, *[_stream_view(a, dil) for a, dil in zip(dss, DILATIONS)], dgates, w_in, g1),
        semantics=("arbitrary",), carry=carry)


def _dw_in_qkv(ds, h, dil):
    t, d = h.shape
    tk = min(1024, t)
    sub = min(256, t)
    width = 3 * GROUP_W

    def body(ds_ref, h_ref, o_ref, acc_ref):
        k = pl.program_id(0)

        @pl.when(k == 0)
        def _():
            acc_ref[...] = jnp.zeros_like(acc_ref)

        upd = None
        for i in range(tk // sub):
            rows = ds_ref[:, i * (sub // dil):(i + 1) * (sub // dil), :].reshape(sub, width)
            if dil > 1:
                rows = _permute_rows(_perm(dil, sub, inverse=True), rows)
            term = _dot_tn(rows, h_ref[i * sub:(i + 1) * sub, :])
            upd = term if upd is None else upd + term
        acc_ref[...] += upd

        @pl.when(k == t // tk - 1)
        def _():
            o_ref[...] = acc_ref[...].astype(BF16)

    return pl.pallas_call(
        body, name=f"dw_in_qkv_d{dil}", grid=(t // tk,),
        in_specs=[_stream_spec(dil, tk, width), _rows(tk, d)],
        out_specs=pl.BlockSpec((width, d), lambda k: (0, 0)),
        out_shape=jax.ShapeDtypeStruct((width, d), BF16),
        scratch_shapes=[pltpu.VMEM((width, d), F32)],
        compiler_params=_params("arbitrary"),
    )(_stream_view(ds, dil), h)


def _local_step(x, target, p, late):
    cw = p["conv_a_w"].shape[1]
    (h, abcv, gates, *ss), (g_up,) = _in_proj(x, p["norm_mix_g"], p["w_in"], cw,
                                              carry=_Exchange("gather", [late["w_up"]]))
    w_up = _full_from_gathered(g_up)
    mid = ("w_proj_a", "w_proj_b", "w_out")
    (o0, lse0), g_mid = _attn_fwd(ss[0], DILATIONS[0], carry=_Exchange("gather", [late[n] for n in mid]))
    w_pa, w_pb, w_out = [_full_from_gathered(g) for g in g_mid]
    os, lses = zip((o0, lse0), *[_attn_fwd(s, dil) for s, dil in zip(ss[1:], DILATIONS[1:])])
    x1, ya, yb, yap, ybp, merged = _mixer_out(x, abcv, gates, os, lses, p["conv_a_w"], p["conv_a_b"], p["b_gate"],
                                              w_pa, w_pb, w_out)
    (h2, up_pre), (g_down,) = _up_proj(x1, p["norm_ffn_g"], w_up, carry=_Exchange("gather", [late["w_down"]]))
    w_down = _full_from_gathered(g_down)
    act, conv, dx2, acc_gf, loss = _ffn_loss(x1, up_pre, target, p["ffn_conv_w"], p["ffn_conv_b"], w_down,
                                       p["final_norm_g"])

    parts, got = {}, {}
    dup, acc_fb = _ffn_act_bwd(dx2, conv, w_down)
    parts["w_down"] = _by_destination(_tn_matmul(act, dx2, "dw_down"))
    (dpre, dx1, acc_g2, acc_fw), (got["w_down"],) = _ffn_up_bwd(dup, up_pre, x1, dx2, p["ffn_conv_w"], w_up,
                                                                p["norm_ffn_g"],
                                                                carry=_Exchange("scatter", [parts["w_down"]]))
    parts["w_up"] = _by_destination(_tn_matmul(dpre, h2, "dw_up"))
    (dgates, dyap, dybp, dya, dos, dls, acc_bg), (got["w_up"],) = _mixer_bwd(
        dx1, gates, yap, ybp, os, lses, p["b_gate"], w_out, w_pa, w_pb,
        carry=_Exchange("scatter", [parts["w_up"]]))
    parts["w_out"] = _by_destination(_tn_matmul(merged, dx1, "dw_out"))
    parts["w_proj_a"] = _by_destination(_tn_matmul(dyap, ya, "dw_proj_a"))
    parts["w_proj_b"] = _by_destination(_tn_matmul(dybp, yb, "dw_proj_b"))
    riders = (("w_out", "w_proj_a", "w_proj_b"), (), ())
    dss = []
    for s, do, lse, dl, dil, names in zip(ss, dos, lses, dls, DILATIONS, riders):
        if names:
            (ds,), received = _attn_bwd(s, do, lse, dl, dil, carry=_Exchange("scatter", [parts[n] for n in names]))
            got.update(zip(names, received))
        else:
            (ds,) = _attn_bwd(s, do, lse, dl, dil)
        dss.append(ds)
    dabcv, acc_ca = _conv_mixer_bwd(abcv, dya, p["conv_a_w"], p["conv_a_b"])
    dw_s = [_dw_in_qkv(ds, h, dil) for ds, dil in zip(dss, DILATIONS)]
    dw_qkv = [w[j * GROUP_W:(j + 1) * GROUP_W] for j in range(3) for w in dw_s]
    g_w_in = jnp.concatenate([_tn_matmul(dabcv, h, "dw_in_a"), *dw_qkv, _tn_matmul(dgates, h, "dw_in_g")], axis=0)
    parts["w_in"] = _by_destination(g_w_in)
    (dx, acc_g1), (got["w_in"],) = _in_proj_bwd(x, dx1, dabcv, dss, dgates, p["w_in"], p["norm_mix_g"],
                                                carry=_Exchange("scatter", [parts["w_in"]]))
    small = dict(norm_mix_g=acc_g1[0:1], b_gate=acc_bg[0:2], conv_a_w=acc_ca[0:3], conv_a_b=acc_ca[3:4],
                 norm_ffn_g=acc_g2[0:1], ffn_conv_w=acc_fw[0:3], ffn_conv_b=acc_fb[0:1], final_norm_g=acc_gf[0:1])
    return loss[0, 0], dx, parts, got, small


def _all_gather(shards):
    n = len(shards)

    def body(*refs):
        ins, outs = refs[:n], refs[n:2 * n]
        send_sems, recv_sems, local_sems = refs[2 * n:]
        x, y, c = _mesh_pos()
        me, sibling = (x, y, c), (x, y, 1 - c)
        chips = [(1 - x, y), (x, 1 - y), (1 - x, 1 - y)]

        def copy(i, k, block, to, src=None):
            rows = outs[i].at[_dev_index(*block)]
            return pltpu.make_async_remote_copy(
                src_ref=rows if src is None else src, dst_ref=rows, send_sem=send_sems.at[i, k],
                recv_sem=recv_sems.at[i, k], device_id=to, device_id_type=MESH)

        mine, first, passed = [], [], []
        for i in range(n):
            cp = pltpu.make_async_copy(ins[i], outs[i].at[_dev_index(*me)], local_sems.at[i])
            cp.start()
            mine.append(cp)
            first.append(copy(i, 0, me, sibling, src=ins[i]))
            first += [copy(i, 1 + j, me, (*chip, c), src=ins[i]) for j, chip in enumerate(chips)]
        for cp in first:
            cp.start()
        for i in range(n):
            for j, chip in enumerate(chips):
                copy(i, 1 + j, (*chip, c), me).wait_recv()
                fw = copy(i, 4 + j, (*chip, c), sibling)
                fw.start()
                passed.append(fw)
        for i in range(n):
            copy(i, 0, sibling, me).wait_recv()
            for j, chip in enumerate(chips):
                copy(i, 4 + j, (*chip, 1 - c), me).wait_recv()
        for cp in first + passed:
            cp.wait_send()
        for cp in mine:
            cp.wait()

    return pl.pallas_call(
        body, name="all_gather_weights",
        in_specs=[ANY] * n, out_specs=[ANY] * n,
        out_shape=[jax.ShapeDtypeStruct((N_DEV,) + s.shape, s.dtype) for s in shards],
        scratch_shapes=[pltpu.SemaphoreType.DMA((n, 7)), pltpu.SemaphoreType.DMA((n, 7)),
                        pltpu.SemaphoreType.DMA((n,))],
    )(*shards)


def _all_reduce_small(v):
    r = v.shape[0]

    def body(v_ref, o_ref, gath, send_sems, recv_sems):
        x, y, c = _mesh_pos()
        me = _dev_index(x, y, c)
        gath[me] = v_ref[...]
        flips = [(kx, ky, kc) for kx in (0, 1) for ky in (0, 1) for kc in (0, 1)][1:]
        copies = []
        for k, (kx, ky, kc) in enumerate(flips):
            px = 1 - x if kx else x
            py = 1 - y if ky else y
            pc = 1 - c if kc else c
            cp = pltpu.make_async_remote_copy(
                src_ref=v_ref, dst_ref=gath.at[me], send_sem=send_sems.at[k], recv_sem=recv_sems.at[k],
                device_id=(px, py, pc), device_id_type=MESH)
            cp.start()
            copies.append((cp, _dev_index(px, py, pc)))
        for k, (cp, peer) in enumerate(copies):
            pltpu.make_async_remote_copy(
                src_ref=v_ref, dst_ref=gath.at[peer], send_sem=send_sems.at[k], recv_sem=recv_sems.at[k],
                device_id=(x, y, c), device_id_type=MESH).wait_recv()
        for cp, _ in copies:
            cp.wait_send()
        total = gath[0]
        for j in range(1, N_DEV):
            total = total + gath[j]
        o_ref[...] = total

    return pl.pallas_call(
        body, name="all_reduce_small",
        in_specs=[pl.BlockSpec(memory_space=pltpu.VMEM)], out_specs=pl.BlockSpec(memory_space=pltpu.VMEM),
        out_shape=jax.ShapeDtypeStruct((r, LANES), F32),
        scratch_shapes=[pltpu.VMEM((N_DEV, r, LANES), F32), pltpu.SemaphoreType.DMA((7,)),
                        pltpu.SemaphoreType.DMA((7,))],
    )(v)


def _adamw_math(w, g, m, v):
    m2 = ADAM_B1 * m + (1.0 - ADAM_B1) * g
    v2 = ADAM_B2 * v + (1.0 - ADAM_B2) * (g * g)
    m_hat = m2 / (1.0 - ADAM_B1 ** ADAM_STEP)
    v_hat = v2 / (1.0 - ADAM_B2 ** ADAM_STEP)
    delta = -ADAM_LR * (m_hat / (jnp.sqrt(v_hat) + ADAM_EPS) + ADAM_WD * w)
    return delta, m2, v2


def _adamw_big(w, m, v, part, got, me):
    r, c = w.shape
    tr = max(t for t in range(HALO, min(r, 512) + 1, HALO) if r % t == 0)

    def body(me_ref, w_ref, m_ref, v_ref, own_ref, *rest):
        del me_ref
        got_refs, (g_out, d_out, m_out, v_out) = rest[:N_DEV - 1], rest[N_DEV - 1:]
        g = own_ref[...].astype(F32)
        for ref in got_refs:
            g = g + ref[...].astype(F32)
        delta, m2, v2 = _adamw_math(w_ref[...], g, m_ref[...], v_ref[...])
        g_out[...] = g
        d_out[...] = delta
        m_out[...] = m2
        v_out[...] = v2

    def peer_block(k):
        return pl.BlockSpec((None, tr, c), lambda i, me_ref: (jnp.bitwise_xor(me_ref[0], k), i, 0))

    plain = pl.BlockSpec((tr, c), lambda i, me_ref: (i, 0))
    out = jax.ShapeDtypeStruct((r, c), F32)
    return pl.pallas_call(
        body, name="adamw_big",
        grid_spec=pltpu.PrefetchScalarGridSpec(
            num_scalar_prefetch=1, grid=(r // tr,),
            in_specs=[plain, plain, plain] + [peer_block(k) for k in range(N_DEV)],
            out_specs=[plain] * 4),
        out_shape=[out] * 4,
        compiler_params=_params("parallel"),
    )(me, w, m, v, part, *([got] * (N_DEV - 1)))


def _adamw_small(w, g, m, v):
    def body(w_ref, g_ref, m_ref, v_ref, d_out, m_out, v_out):
        delta, m2, v2 = _adamw_math(w_ref[...], g_ref[...], m_ref[...], v_ref[...])
        d_out[...] = delta
        m_out[...] = m2
        v_out[...] = v2

    out = jax.ShapeDtypeStruct(w.shape, F32)
    return pl.pallas_call(body, name="adamw_small", out_shape=[out] * 3)(w, g, m, v)


BIG = ("w_in", "w_proj_a", "w_proj_b", "w_out", "w_up", "w_down")
LATE = ("w_proj_a", "w_proj_b", "w_out", "w_up", "w_down")
COLUMN_SHARDED = ("w_in", "w_proj_a", "w_proj_b", "w_up")
SMALL = ("norm_mix_g", "b_gate", "conv_a_w", "conv_a_b", "norm_ffn_g", "ffn_conv_w", "ffn_conv_b", "final_norm_g")
SMALL_SHARDED = ("b_gate", "conv_a_w", "ffn_conv_w")
WEIGHTS = ("norm_mix_g", "w_in", "b_gate", "conv_a_w", "conv_a_b", "w_proj_a", "w_proj_b", "w_out", "norm_ffn_g",
           "w_up", "ffn_conv_w", "ffn_conv_b", "w_down", "final_norm_g")


def _pack(vectors, rows):
    flat = jnp.concatenate([v.reshape(-1) for v in vectors])
    return jnp.pad(flat, (0, rows * LANES - flat.shape[0])).reshape(rows, LANES)


def _packed_rows(count):
    rows = -(-count // LANES)
    return -(-rows // SUBLANES) * SUBLANES


def _unpack(packed, shapes):
    flat = packed.reshape(-1)
    out, lo = [], 0
    for s in shapes:
        size = 1
        for dim in s:
            size *= dim
        out.append(flat[lo:lo + size].reshape(s))
        lo += size
    return out


def _full_from_gathered(gathered):
    _, r, c = gathered.shape
    return gathered.reshape(N_DEV * r, c)


def _by_destination(grad):
    rr, cc = grad.shape
    return grad.reshape(N_DEV, rr // N_DEV, cc)


def _block2d(name, a):
    a = a.reshape(a.shape[-2:])
    return a.T if name in COLUMN_SHARDED else a


def kernel(x, norm_mix_g, w_in, b_gate, conv_a_w, conv_a_b, w_proj_a, w_proj_b, w_out, norm_ffn_g, w_up, ffn_conv_w, ffn_conv_b, w_down, final_norm_g, loss_target, m_norm_mix_g, m_w_in, m_b_gate, m_conv_a_w, m_conv_a_b, m_w_proj_a, m_w_proj_b, m_w_out, m_norm_ffn_g, m_w_up, m_ffn_conv_w, m_ffn_conv_b, m_w_down, m_final_norm_g, v_norm_mix_g, v_w_in, v_b_gate, v_conv_a_w, v_conv_a_b, v_w_proj_a, v_w_proj_b, v_w_out, v_norm_ffn_g, v_w_up, v_ffn_conv_w, v_ffn_conv_b, v_w_down, v_final_norm_g):
    given = dict(locals())
    shard = {n: given[n] for n in WEIGHTS}
    mom_m = {n: given["m_" + n] for n in WEIGHTS}
    mom_v = {n: given["v_" + n] for n in WEIGHTS}
    xi, yi, ci = _mesh_pos()
    me = _dev_index(xi, yi, ci)
    me1 = me.astype(jnp.int32).reshape(1)

    big2d = {n: _block2d(n, shard[n]) for n in BIG}
    small_shapes = [shard[n].shape[1:] for n in SMALL_SHARDED]
    n_small = sum(s[0] * s[1] for s in small_shapes)
    packed_small = _pack([shard[n] for n in SMALL_SHARDED], _packed_rows(n_small))
    gathered = _all_gather([big2d["w_in"].astype(BF16), packed_small])
    p = {"w_in": _full_from_gathered(gathered[0])}
    per_dev = [_unpack(gathered[-1][j], small_shapes) for j in range(N_DEV)]
    for i, n in enumerate(SMALL_SHARDED):
        p[n] = jnp.concatenate([per_dev[j][i] for j in range(N_DEV)], axis=1)
    p["norm_mix_g"], p["norm_ffn_g"] = shard["norm_mix_g"], shard["norm_ffn_g"]
    p["conv_a_b"], p["ffn_conv_b"] = shard["conv_a_b"], shard["ffn_conv_b"]
    p["final_norm_g"] = shard["final_norm_g"].reshape(1, -1)
    late = {n: big2d[n].astype(BF16) for n in LATE}

    loss_part, dx, parts, got, g_small = _local_step(x[0], loss_target[0], p, late)

    results = {}
    for n in BIG:
        outs = _adamw_big(big2d[n], _block2d(n, mom_m[n]), _block2d(n, mom_v[n]), parts[n], got[n], me1)
        results[n] = [_block2d(n, o).reshape(shard[n].shape) for o in outs]

    small_full_shapes = [g_small[n].shape for n in SMALL]
    n_vec = sum(s[0] * s[1] for s in small_full_shapes) + 1
    packed = _pack([g_small[n] for n in SMALL] + [loss_part.reshape(1)], _packed_rows(n_vec))
    reduced = _all_reduce_small(packed)
    *g_full, loss_vec = _unpack(reduced, small_full_shapes + [(1,)])
    loss = loss_vec[0]
    own_g = []
    for n, g in zip(SMALL, g_full):
        if n in SMALL_SHARDED:
            width = shard[n].shape[-1]
            g = lax.dynamic_slice_in_dim(g, me * width, width, axis=1)
        own_g.append(g.reshape(shard[n].shape))
    own_shapes = [shard[n].shape for n in SMALL]
    rows = _packed_rows(sum(g.size for g in own_g))
    small_out = _adamw_small(_pack([shard[n] for n in SMALL], rows), _pack(own_g, rows),
                             _pack([mom_m[n] for n in SMALL], rows), _pack([mom_v[n] for n in SMALL], rows))
    deltas, new_ms, new_vs = (_unpack(o, own_shapes) for o in small_out)
    for i, n in enumerate(SMALL):
        results[n] = [own_g[i], deltas[i], new_ms[i], new_vs[i]]

    grad_x = dx.reshape(x.shape)
    return (loss, grad_x, *[results[n][0] for n in WEIGHTS], *[results[n][1] for n in WEIGHTS],
            *[results[n][2] for n in WEIGHTS], *[results[n][3] for n in WEIGHTS])
```

```python
import functools

import jax
import jax.numpy as jnp
from jax import lax
from jax.experimental import pallas as pl
from jax.experimental.pallas import tpu as pltpu

F32 = jnp.float32
BF16 = jnp.bfloat16
MESH = pl.DeviceIdType.MESH

N_DEV = 8
RMS_EPS = 1e-6
NEG_INF = -1e30
N_GROUPS = 3
DILATIONS = (1, 4, 16)
HEADS_PER_GROUP = 4
HEAD_DIM = 64
GROUP_W = HEADS_PER_GROUP * HEAD_DIM
ATTN_W = N_GROUPS * GROUP_W
QBLK = 128
ATTN_SCALE = HEAD_DIM ** -0.5

ADAM_LR = 0.001
ADAM_B1 = 0.9
ADAM_B2 = 0.999
ADAM_EPS = 1e-08
ADAM_WD = 0.01
ADAM_STEP = 10

PERM_TOKENS = 256
ROWS_MATMUL = 512
HALO = 16
LANES = 128
SUBLANES = 8
VMEM_LIMIT_BYTES = 56 * 1024 * 1024


def _params(*sem):
    return pltpu.CompilerParams(dimension_semantics=sem, vmem_limit_bytes=VMEM_LIMIT_BYTES)


def _pick_tile(n, cap):
    if n <= cap:
        return n
    best = None
    for t in range(LANES, cap + 1, LANES):
        if n % t == 0:
            best = t
    assert best is not None, (n, cap)
    return best


def _rows(tm, c, j=0):
    return pl.BlockSpec((tm, c), lambda m: (m, j))


def _prev_halo(tm, c):
    return pl.BlockSpec((HALO, c), lambda m: (jnp.maximum(m * (tm // HALO) - 1, 0), 0))


def _next_halo(tm, c, t_total):
    last = t_total // HALO - 1
    return pl.BlockSpec((HALO, c), lambda m: (jnp.minimum((m + 1) * (tm // HALO), last), 0))


def _resident(shape):
    nd = len(shape)
    return pl.BlockSpec(shape, lambda *_: (0,) * nd, pipeline_mode=pl.Buffered(1))


def _acc_spec(c):
    return pl.BlockSpec((SUBLANES, c), lambda *_: (0, 0))


def _shift_down(u, halo, k):
    edge = jnp.concatenate([halo[HALO - SUBLANES:], u[:SUBLANES]], axis=0)
    head = pltpu.roll(edge, k, 0)[SUBLANES:]
    return jnp.concatenate([head, pltpu.roll(u, k, 0)[SUBLANES:]], axis=0)


def _shift_up(u, halo, k):
    n = u.shape[0]
    edge = jnp.concatenate([u[n - SUBLANES:], halo[:SUBLANES]], axis=0)
    tail = pltpu.roll(edge, 2 * SUBLANES - k, 0)[:SUBLANES]
    return jnp.concatenate([pltpu.roll(u, n - k, 0)[:n - SUBLANES], tail], axis=0)


def _stack_rows(rows, c):
    idx = lax.broadcasted_iota(jnp.int32, (SUBLANES, c), 0)
    out = jnp.zeros((SUBLANES, c), F32)
    for i, r in enumerate(rows):
        out = out + jnp.where(idx == i, r, 0.0)
    return out


def _colsum(v):
    return jnp.sum(v, axis=0, keepdims=True)


def _sigmoid(v):
    return 0.5 * jnp.tanh(0.5 * v) + 0.5


def _rms_fwd(xv, g):
    r = lax.rsqrt(jnp.mean(xv * xv, axis=-1, keepdims=True) + RMS_EPS)
    return xv * r * g, r


def _rms_bwd(xv, g, dy):
    r = lax.rsqrt(jnp.mean(xv * xv, axis=-1, keepdims=True) + RMS_EPS)
    xn = xv * r
    dxn = dy * g
    dx = r * (dxn - xn * jnp.mean(dxn * xn, axis=-1, keepdims=True))
    return dx, dy * xn


def _dot(a, b):
    return jnp.dot(a, b, preferred_element_type=F32)


def _dot_nt(a, b):
    return lax.dot_general(a, b, (((1,), (1,)), ((), ())), preferred_element_type=F32)


def _dot_tn(a, b):
    return lax.dot_general(a, b, (((0,), (0,)), ((), ())), preferred_element_type=F32)


def _perm(dil, n, inverse=False):
    i = lax.broadcasted_iota(jnp.int32, (n, n), 0)
    j = lax.broadcasted_iota(jnp.int32, (n, n), 1)
    if inverse:
        i, j = j, i
    per = n // dil
    return (j == (i % per) * dil + i // per).astype(BF16)


def _permute_rows(pm, v):
    if v.dtype == BF16:
        return _dot(pm, v).astype(BF16)
    hi = v.astype(BF16)
    lo = (v - hi.astype(F32)).astype(BF16)
    return _dot(pm, hi) + _dot(pm, lo)


def _stream_view(a, dil):
    t, c = a.shape
    return a.reshape(dil, t // dil, c)


def _stream_spec(dil, tm, c):
    return pl.BlockSpec((dil, tm // dil, c), lambda m: (0, m, 0))


def _load_streams(ref, dil, tm):
    c = ref.shape[-1]
    if dil == 1:
        return ref[...].reshape(tm, c)
    sub = min(PERM_TOKENS, tm)
    pm = _perm(dil, sub, inverse=True)
    parts = [_permute_rows(pm, ref[:, i * (sub // dil):(i + 1) * (sub // dil), :].reshape(sub, c))
             for i in range(tm // sub)]
    return parts[0] if len(parts) == 1 else jnp.concatenate(parts, axis=0)


def _store_streams(ref, dil, tm, v):
    if dil == 1:
        ref[...] = v.reshape(ref.shape).astype(ref.dtype)
        return
    sub = min(PERM_TOKENS, tm)
    pm = _perm(dil, sub)
    for i in range(tm // sub):
        piece = _permute_rows(pm, v[i * sub:(i + 1) * sub])
        ref[:, i * (sub // dil):(i + 1) * (sub // dil), :] = piece.reshape(dil, sub // dil, -1).astype(ref.dtype)


ANY = pl.BlockSpec(memory_space=pl.ANY)


def _mesh_pos():
    return lax.axis_index("x"), lax.axis_index("y"), lax.axis_index("c")


def _dev_index(px, py, pc):
    return 4 * px + 2 * py + pc


class _Exchange:
    def __init__(self, mode, arrays, rows=None, into=()):
        self.mode, self.arrays, self.rows, self.into = mode, list(arrays), rows, list(into)
        n = len(self.arrays)
        if mode == "gather":
            self.out_shape = [jax.ShapeDtypeStruct((N_DEV,) + a.shape, a.dtype) for a in self.arrays]
        else:
            self.out_shape = [jax.ShapeDtypeStruct(a.shape, a.dtype) for a in self.arrays]
        self.scratch = [pltpu.SemaphoreType.DMA((n, N_DEV - 1)), pltpu.SemaphoreType.DMA((n, N_DEV - 1)),
                        pltpu.SemaphoreType.DMA((n,))]

    def _peers(self):
        x, y, c = _mesh_pos()
        flips = [(kx, ky, kc) for kx in (0, 1) for ky in (0, 1) for kc in (0, 1)][1:]
        peers = [(1 - x if kx else x, 1 - y if ky else y, 1 - c if kc else c) for kx, ky, kc in flips]
        return _dev_index(x, y, c), peers

    def _copy(self, ins, outs, sems, i, k, peer, me, sending):
        src = ins[i] if self.mode == "gather" else ins[i].at[_dev_index(*peer)]
        dst = outs[i].at[me if sending else _dev_index(*peer)]
        if self.rows is not None:
            src, dst = src.at[pl.ds(*self.rows)], dst.at[pl.ds(*self.rows)]
        return pltpu.make_async_remote_copy(src_ref=src, dst_ref=dst, send_sem=sems[0].at[i, k],
                                            recv_sem=sems[1].at[i, k], device_id=peer, device_id_type=MESH)

    def _own(self, ins, outs, sems, i, me):
        return pltpu.make_async_copy(ins[i], outs[i].at[me], sems[2].at[i])

    def start(self, ins, outs, sems):
        me, peers = self._peers()
        for i in range(len(ins)):
            if self.mode == "gather":
                self._own(ins, outs, sems, i, me).start()
            for k, peer in enumerate(peers):
                self._copy(ins, outs, sems, i, k, peer, me, True).start()

    def wait(self, ins, outs, sems):
        me, peers = self._peers()
        for i in range(len(ins)):
            for k, peer in enumerate(peers):
                self._copy(ins, outs, sems, i, k, peer, me, False).wait_recv()
            for k, peer in enumerate(peers):
                self._copy(ins, outs, sems, i, k, peer, me, True).wait_send()
            if self.mode == "gather":
                self._own(ins, outs, sems, i, me).wait()


def _call(body, *, name, grid, in_specs, out_specs, out_shape, args, semantics, carry=None, scratch=()):
    if carry is None:
        return pl.pallas_call(body, name=name, grid=grid, in_specs=in_specs, out_specs=out_specs,
                              out_shape=out_shape, scratch_shapes=list(scratch),
                              compiler_params=_params(*semantics))(*args)
    n_in, n_out, n_x, n_s = len(in_specs), len(out_specs), len(carry.arrays), len(scratch)
    n_into = len(carry.into)
    all_in = n_in + n_x + n_into

    def carried(*refs):
        ins, x_ins = refs[:n_in], refs[n_in:n_in + n_x]
        outs = refs[all_in:all_in + n_out]
        x_outs = refs[all_in + n_out:all_in + n_out + n_x]
        own = refs[all_in + n_out + n_x:all_in + n_out + n_x + n_s]
        sems = refs[all_in + n_out + n_x + n_s:]
        first = functools.reduce(jnp.logical_and, [pl.program_id(a) == 0 for a in range(len(grid))])
        last = functools.reduce(jnp.logical_and, [pl.program_id(a) == grid[a] - 1 for a in range(len(grid))])

        @pl.when(first)
        def _():
            carry.start(x_ins, x_outs, sems)

        body(*ins, *outs, *own)

        @pl.when(last)
        def _():
            carry.wait(x_ins, x_outs, sems)

    res = pl.pallas_call(
        carried, name=name, grid=grid, in_specs=list(in_specs) + [ANY] * (n_x + n_into),
        out_specs=list(out_specs) + [ANY] * n_x, out_shape=list(out_shape) + carry.out_shape,
        input_output_aliases={n_in + n_x + i: n_out + i for i in range(n_into)},
        scratch_shapes=list(scratch) + carry.scratch, compiler_params=_params(*["arbitrary"] * len(grid)),
    )(*args, *carry.arrays, *carry.into)
    return list(res[:n_out]), list(res[n_out:])


def _up_proj(x, g, wt, carry=None):
    t, d = x.shape
    n = wt.shape[0]
    tm = min(256, t)

    def body(x_ref, g_ref, wt_ref, h_ref, o_ref):
        h = _rms_fwd(x_ref[...], g_ref[...])[0].astype(BF16)
        h_ref[...] = h
        o_ref[...] = _dot_nt(h, wt_ref[...]).astype(BF16)

    return _call(
        body, name="up_proj", grid=(t // tm,),
        in_specs=[_rows(tm, d), _resident((1, d)), _resident((n, d))],
        out_specs=[_rows(tm, d), _rows(tm, n)],
        out_shape=[jax.ShapeDtypeStruct((t, d), BF16), jax.ShapeDtypeStruct((t, n), BF16)],
        args=(x, g, wt), semantics=("parallel",), carry=carry)


def _in_proj(x, g, wt, cw, carry=None):
    t, d = x.shape
    n = wt.shape[0]
    tm = min(ROWS_MATMUL, t)
    qkv0 = 3 * cw

    def body(x_ref, g_ref, wt_ref, h_ref, abcv_ref, gates_ref, *s_refs):
        h = _rms_fwd(x_ref[...], g_ref[...])[0].astype(BF16)
        h_ref[...] = h
        abcv_ref[...] = _dot_nt(h, wt_ref[0:qkv0, :]).astype(BF16)
        gates_ref[...] = _dot_nt(h, wt_ref[qkv0 + 3 * ATTN_W:n, :]).astype(BF16)
        for gi, s_ref in enumerate(s_refs):
            cols = [_dot_nt(h, wt_ref[qkv0 + j * ATTN_W + gi * GROUP_W:qkv0 + j * ATTN_W + (gi + 1) * GROUP_W, :])
                    for j in range(3)]
            _store_streams(s_ref, DILATIONS[gi], tm, jnp.concatenate(cols, axis=1).astype(BF16))

    return _call(
        body, name="in_proj", grid=(t // tm,),
        in_specs=[_rows(tm, d), _resident((1, d)), _resident((n, d))],
        out_specs=[_rows(tm, d), _rows(tm, qkv0), _rows(tm, 2 * d)]
        + [_stream_spec(dil, tm, 3 * GROUP_W) for dil in DILATIONS],
        out_shape=[jax.ShapeDtypeStruct((t, d), BF16), jax.ShapeDtypeStruct((t, qkv0), BF16),
                   jax.ShapeDtypeStruct((t, 2 * d), BF16)]
        + [jax.ShapeDtypeStruct((dil, t // dil, 3 * GROUP_W), BF16) for dil in DILATIONS],
        args=(x, g, wt), semantics=("parallel",), carry=carry)


def _head_masks():
    lane = lax.broadcasted_iota(jnp.int32, (1, GROUP_W), 1)
    return lane, [(lane // HEAD_DIM) == h for h in range(HEADS_PER_GROUP)]


def _stack_heads(v, heads):
    return jnp.concatenate([jnp.where(hm, v, jnp.zeros_like(v)) for hm in heads], axis=0)


def _merge_heads(v, heads):
    out = jnp.zeros((QBLK, GROUP_W), v.dtype)
    for h, hm in enumerate(heads):
        out = jnp.where(hm, v[h * QBLK:(h + 1) * QBLK], out)
    return out


def _pair_block(col):
    return pl.BlockSpec((2 * QBLK, GROUP_W), lambda b: (b, col))


def _edge_block(col, shift, nb):
    return pl.BlockSpec((QBLK, GROUP_W), lambda b: (jnp.clip(2 * b + shift, 0, nb - 1), col))


def _band_mask(has_prev):
    rows = HEADS_PER_GROUP * QBLK
    row = lax.broadcasted_iota(jnp.int32, (rows, 2 * QBLK), 0) & (QBLK - 1)
    col = lax.broadcasted_iota(jnp.int32, (rows, 2 * QBLK), 1)
    return ((col < QBLK) & (col >= row) & has_prev) | ((col >= QBLK) & (col - QBLK <= row))


def _next_mask(has_next):
    rows = HEADS_PER_GROUP * QBLK
    row = lax.broadcasted_iota(jnp.int32, (rows, QBLK), 0) & (QBLK - 1)
    col = lax.broadcasted_iota(jnp.int32, (rows, QBLK), 1)
    return (col >= row) & has_next


def _attn_fwd(s, dil, carry=None):
    t = s.shape[0] * s.shape[1]
    nb = t // QBLK
    per_stream = nb // dil
    assert per_stream % 2 == 0

    def body(q_ref, kc_ref, kp_ref, vc_ref, vp_ref, o_ref, lse_ref):
        b = pl.program_id(0)
        _, heads = _head_masks()
        first_has_prev = lax.rem(2 * b, per_stream) != 0
        for j in range(2):
            rows = slice(j * QBLK, (j + 1) * QBLK)
            if j == 0:
                k2 = jnp.concatenate([kp_ref[...], kc_ref[rows, :]], axis=0)
                v2 = jnp.concatenate([vp_ref[...], vc_ref[rows, :]], axis=0)
            else:
                k2, v2 = kc_ref[...], vc_ref[...]
            mask = _band_mask(first_has_prev if j == 0 else True)
            sc = jnp.where(mask, _dot_nt(_stack_heads(q_ref[rows, :], heads), k2) * ATTN_SCALE, NEG_INF)
            mx = jnp.max(sc, axis=1, keepdims=True)
            pr = jnp.exp(sc - mx)
            den = jnp.sum(pr, axis=1, keepdims=True)
            o_all = _dot(pr.astype(BF16), v2) / den
            o_ref[rows, :] = _merge_heads(o_all, heads).astype(BF16)
            lse_ref[rows, :] = _merge_heads(jnp.broadcast_to(mx + jnp.log(den), o_all.shape), heads)

    sv = s.reshape(t, 3 * GROUP_W)
    return _call(
        body, name=f"attn_fwd_d{dil}", grid=(nb // 2,),
        in_specs=[_pair_block(0), _pair_block(1), _edge_block(1, -1, nb), _pair_block(2), _edge_block(2, -1, nb)],
        out_specs=[_pair_block(0), _pair_block(0)],
        out_shape=[jax.ShapeDtypeStruct((t, GROUP_W), BF16), jax.ShapeDtypeStruct((t, GROUP_W), F32)],
        args=(sv, sv, sv, sv, sv), semantics=("parallel",), carry=carry)


def _group_softmax(parts):
    mx = jnp.maximum(jnp.maximum(parts[0], parts[1]), parts[2])
    es = [jnp.exp(p - mx) for p in parts]
    den = es[0] + es[1] + es[2]
    return [e / den for e in es]


def _mixer_out(x, abcv, gates, os, lses, conv_w, conv_b, b_gate, w_pa, w_pb, w_o):
    t, d = x.shape
    cw = conv_w.shape[1]
    tm = min(256, t)

    def body(x_ref, abcv_ref, halo_ref, gates_ref, o0_ref, o1_ref, o2_ref, l0_ref, l1_ref, l2_ref, cw_ref, cb_ref,
             bg_ref, wpa_ref, wpb_ref, wo_ref, x1_ref, ya_ref, yb_ref, yap_ref, ybp_ref, mg_ref):
        m = pl.program_id(0)
        ab = abcv_ref[:, 0:cw].astype(F32)
        u = abcv_ref[:, cw:2 * cw].astype(F32) * abcv_ref[:, 2 * cw:3 * cw].astype(F32)
        hu = halo_ref[:, cw:2 * cw].astype(F32) * halo_ref[:, 2 * cw:3 * cw].astype(F32)
        hu = jnp.where(m > 0, hu, 0.0)
        cv = (cw_ref[0:1, :] * _shift_down(u, hu, 2) + cw_ref[1:2, :] * _shift_down(u, hu, 1)
              + cw_ref[2:3, :] * u + cb_ref[...])
        ya = (ab * cv).astype(BF16)
        ya_ref[...] = ya
        alphas = _group_softmax([_load_streams(r, dil, tm) for r, dil in zip((l0_ref, l1_ref, l2_ref), DILATIONS)])
        for i, (o_ref, dil) in enumerate(zip((o0_ref, o1_ref, o2_ref), DILATIONS)):
            sl = slice(i * GROUP_W, (i + 1) * GROUP_W)
            yb_ref[:, sl] = (alphas[i] * _load_streams(o_ref, dil, tm).astype(F32)).astype(BF16)
        yap = _dot_nt(ya, wpa_ref[...])
        ybp = _dot_nt(yb_ref[...], wpb_ref[...])
        yap_ref[...] = yap.astype(BF16)
        ybp_ref[...] = ybp.astype(BF16)
        sa = _sigmoid(gates_ref[:, 0:d].astype(F32) + bg_ref[0:1, :])
        sb = _sigmoid(gates_ref[:, d:2 * d].astype(F32) + bg_ref[1:2, :])
        merged = (sa * yap + sb * ybp).astype(BF16)
        mg_ref[...] = merged
        x1_ref[...] = x_ref[...] + _dot(merged, wo_ref[...])

    return pl.pallas_call(
        body, name="mixer_out", grid=(t // tm,),
        in_specs=[_rows(tm, d), _rows(tm, 3 * cw), _prev_halo(tm, 3 * cw), _rows(tm, 2 * d)]
        + [_stream_spec(dil, tm, GROUP_W) for dil in DILATIONS] * 2
        + [_resident((3, cw)), _resident((1, cw)), _resident((2, d)),
           _resident((d, cw)), _resident((d, ATTN_W)), _resident((d, d))],
        out_specs=[_rows(tm, d), _rows(tm, cw), _rows(tm, ATTN_W), _rows(tm, d), _rows(tm, d), _rows(tm, d)],
        out_shape=[jax.ShapeDtypeStruct((t, d), F32), jax.ShapeDtypeStruct((t, cw), BF16),
                   jax.ShapeDtypeStruct((t, ATTN_W), BF16), jax.ShapeDtypeStruct((t, d), BF16),
                   jax.ShapeDtypeStruct((t, d), BF16), jax.ShapeDtypeStruct((t, d), BF16)],
        compiler_params=_params("parallel"),
    )(x, abcv, abcv, gates, *[_stream_view(a, dil) for a, dil in zip(os, DILATIONS)],
      *[_stream_view(a, dil) for a, dil in zip(lses, DILATIONS)], conv_w, conv_b, b_gate, w_pa, w_pb, w_o)


def _ffn_conv(p_ref, halo_ref, w_ref, b_ref, m, c0, wd):
    p = p_ref[:, c0:c0 + wd].astype(F32)
    hp = jnp.where(m > 0, halo_ref[:, c0:c0 + wd].astype(F32), 0.0)
    return (w_ref[0:1, c0:c0 + wd] * _shift_down(p, hp, 2) + w_ref[1:2, c0:c0 + wd] * _shift_down(p, hp, 1)
            + w_ref[2:3, c0:c0 + wd] * p + b_ref[:, c0:c0 + wd])


def _ffn_loss(x1, up_pre, target, conv_w, conv_b, w_d, g_f):
    t, d = x1.shape
    dff = w_d.shape[0]
    tm = min(256, t)
    ck = _pick_tile(dff, 1408)

    def body(x1_ref, up_ref, halo_ref, tg_ref, cw_ref, cb_ref, wd_ref, gf_ref, act_ref, conv_ref, dx2_ref, acc_ref,
             loss_ref):
        m = pl.program_id(0)

        @pl.when(m == 0)
        def _():
            acc_ref[...] = jnp.zeros_like(acc_ref)
            loss_ref[...] = jnp.zeros_like(loss_ref)

        x2 = x1_ref[...]
        for c0 in range(0, dff, ck):
            gate = _ffn_conv(up_ref, halo_ref, cw_ref, cb_ref, m, c0, ck)
            val = _ffn_conv(up_ref, halo_ref, cw_ref, cb_ref, m, dff + c0, ck)
            conv_ref[:, c0:c0 + ck] = gate.astype(BF16)
            conv_ref[:, dff + c0:dff + c0 + ck] = val.astype(BF16)
            act = (gate * _sigmoid(gate) * val).astype(BF16)
            act_ref[:, c0:c0 + ck] = act
            x2 = x2 + _dot(act, wd_ref[c0:c0 + ck, :])
        y, _ = _rms_fwd(x2, gf_ref[...])
        diff = y - tg_ref[...]
        loss_ref[...] += 0.5 * jnp.sum(jnp.mean(diff * diff, axis=-1, keepdims=True))
        dx2, dg = _rms_bwd(x2, gf_ref[...], diff * (1.0 / d))
        dx2_ref[...] = dx2
        acc_ref[...] += _stack_rows([_colsum(dg)], d)

    return pl.pallas_call(
        body, name="ffn_loss", grid=(t // tm,),
        in_specs=[_rows(tm, d), _rows(tm, 2 * dff), _prev_halo(tm, 2 * dff), _rows(tm, d),
                  _resident((3, 2 * dff)), _resident((1, 2 * dff)), _resident((dff, d)), _resident((1, d))],
        out_specs=[_rows(tm, dff), _rows(tm, 2 * dff), _rows(tm, d), _acc_spec(d), _acc_spec(LANES)],
        out_shape=[jax.ShapeDtypeStruct((t, dff), BF16), jax.ShapeDtypeStruct((t, 2 * dff), BF16),
                   jax.ShapeDtypeStruct((t, d), F32), jax.ShapeDtypeStruct((SUBLANES, d), F32),
                   jax.ShapeDtypeStruct((SUBLANES, LANES), F32)],
        compiler_params=_params("arbitrary"),
    )(x1, up_pre, up_pre, target, conv_w, conv_b, w_d, g_f)


def _ffn_act_bwd(dx2, conv, w_d):
    t, d = dx2.shape
    dff = w_d.shape[0]
    tm = min(256, t)
    ck = _pick_tile(dff, 1408)

    def body(dx2_ref, conv_ref, wd_ref, dup_ref, acc_ref):
        m = pl.program_id(0)

        @pl.when(m == 0)
        def _():
            acc_ref[...] = jnp.zeros_like(acc_ref)

        dx2v = dx2_ref[...].astype(BF16)
        for c0 in range(0, dff, ck):
            dact = _dot_nt(dx2v, wd_ref[c0:c0 + ck, :])
            gate = conv_ref[:, c0:c0 + ck].astype(F32)
            val = conv_ref[:, dff + c0:dff + c0 + ck].astype(F32)
            sg = _sigmoid(gate)
            dval = dact * gate * sg
            dgate = dact * val * sg * (1.0 + gate * (1.0 - sg))
            dup_ref[:, c0:c0 + ck] = dgate.astype(BF16)
            dup_ref[:, dff + c0:dff + c0 + ck] = dval.astype(BF16)
            acc_ref[:, c0:c0 + ck] += _stack_rows([_colsum(dgate)], ck)
            acc_ref[:, dff + c0:dff + c0 + ck] += _stack_rows([_colsum(dval)], ck)

    return pl.pallas_call(
        body, name="ffn_act_bwd", grid=(t // tm,),
        in_specs=[_rows(tm, d), _rows(tm, 2 * dff), _resident((dff, d))],
        out_specs=[_rows(tm, 2 * dff), _acc_spec(2 * dff)],
        out_shape=[jax.ShapeDtypeStruct((t, 2 * dff), BF16), jax.ShapeDtypeStruct((SUBLANES, 2 * dff), F32)],
        compiler_params=_params("arbitrary"),
    )(dx2, conv, w_d)


def _ffn_up_bwd(dup, up_pre, x1, dx2, conv_w, w_u, g2, carry=None):
    t, d = x1.shape
    n = dup.shape[1]
    tm = min(256, t)
    ck = _pick_tile(n, 1408)
    last = t // tm - 1

    def body(dup_ref, nxt_ref, up_ref, x1_ref, dx2_ref, cw_ref, wu_ref, g2_ref, dpre_ref, dx1_ref, acc_ref, accw_ref):
        m = pl.program_id(0)

        @pl.when(m == 0)
        def _():
            acc_ref[...] = jnp.zeros_like(acc_ref)
            accw_ref[...] = jnp.zeros_like(accw_ref)

        dh = jnp.zeros((tm, d), F32)
        for c0 in range(0, n, ck):
            du = dup_ref[:, c0:c0 + ck].astype(F32)
            hn = jnp.where(m < last, nxt_ref[:, c0:c0 + ck].astype(F32), 0.0)
            du1 = _shift_up(du, hn, 1)
            du2 = _shift_up(du, hn, 2)
            dpre = (cw_ref[2:3, c0:c0 + ck] * du + cw_ref[1:2, c0:c0 + ck] * du1
                    + cw_ref[0:1, c0:c0 + ck] * du2).astype(BF16)
            dpre_ref[:, c0:c0 + ck] = dpre
            dh = dh + _dot(dpre, wu_ref[c0:c0 + ck, :])
            p = up_ref[:, c0:c0 + ck].astype(F32)
            accw_ref[:, c0:c0 + ck] += _stack_rows([_colsum(du2 * p), _colsum(du1 * p), _colsum(du * p)], ck)
        dx, dg = _rms_bwd(x1_ref[...], g2_ref[...], dh)
        dx1_ref[...] = dx2_ref[...] + dx
        acc_ref[...] += _stack_rows([_colsum(dg)], d)

    return _call(
        body, name="ffn_up_bwd", grid=(t // tm,),
        in_specs=[_rows(tm, n), _next_halo(tm, n, t), _rows(tm, n), _rows(tm, d), _rows(tm, d), _resident((3, n)),
                  _resident((n, d)), _resident((1, d))],
        out_specs=[_rows(tm, n), _rows(tm, d), _acc_spec(d), _acc_spec(n)],
        out_shape=[jax.ShapeDtypeStruct((t, n), BF16), jax.ShapeDtypeStruct((t, d), F32),
                   jax.ShapeDtypeStruct((SUBLANES, d), F32), jax.ShapeDtypeStruct((SUBLANES, n), F32)],
        args=(dup, dup, up_pre, x1, dx2, conv_w, w_u, g2), semantics=("arbitrary",), carry=carry)


def _tn_matmul(a, b, name):
    t, mdim = a.shape
    n = b.shape[1]
    tk = min(1024, t)
    tmm = _pick_tile(mdim, 1536)
    tn = _pick_tile(n, 1024)

    def body(a_ref, b_ref, o_ref, acc_ref):
        k = pl.program_id(2)

        @pl.when(k == 0)
        def _():
            acc_ref[...] = jnp.zeros_like(acc_ref)

        acc_ref[...] += _dot_tn(a_ref[...].astype(BF16), b_ref[...].astype(BF16))

        @pl.when(k == t // tk - 1)
        def _():
            o_ref[...] = acc_ref[...].astype(BF16)

    return pl.pallas_call(
        body, name=name, grid=(mdim // tmm, n // tn, t // tk),
        in_specs=[pl.BlockSpec((tk, tmm), lambda i, j, k: (k, i)), pl.BlockSpec((tk, tn), lambda i, j, k: (k, j))],
        out_specs=pl.BlockSpec((tmm, tn), lambda i, j, k: (i, j)),
        out_shape=jax.ShapeDtypeStruct((mdim, n), BF16),
        scratch_shapes=[pltpu.VMEM((tmm, tn), F32)],
        compiler_params=_params("parallel", "parallel", "arbitrary"),
    )(a, b)


def _mixer_bwd(dx1, gates, yap, ybp, os, lses, b_gate, w_o, w_pa, w_pb):
    t, d = dx1.shape
    cw = w_pa.shape[1]
    tm = min(256, t)

    def body(dx1_ref, gates_ref, yap_ref, ybp_ref, o0_ref, o1_ref, o2_ref, l0_ref, l1_ref, l2_ref, bg_ref, wo_ref,
             wpa_ref, wpb_ref, dgates_ref, dyap_ref, dybp_ref, dya_ref, do0_ref, do1_ref, do2_ref, dl0_ref, dl1_ref,
             dl2_ref, acc_ref):
        m = pl.program_id(0)

        @pl.when(m == 0)
        def _():
            acc_ref[...] = jnp.zeros_like(acc_ref)

        dmg = _dot_nt(dx1_ref[...].astype(BF16), wo_ref[...])
        sa = _sigmoid(gates_ref[:, 0:d].astype(F32) + bg_ref[0:1, :])
        sb = _sigmoid(gates_ref[:, d:2 * d].astype(F32) + bg_ref[1:2, :])
        dyap = (dmg * sa).astype(BF16)
        dybp = (dmg * sb).astype(BF16)
        dga = dmg * yap_ref[...].astype(F32) * sa * (1.0 - sa)
        dgb = dmg * ybp_ref[...].astype(F32) * sb * (1.0 - sb)
        dyap_ref[...] = dyap
        dybp_ref[...] = dybp
        dgates_ref[:, 0:d] = dga.astype(BF16)
        dgates_ref[:, d:2 * d] = dgb.astype(BF16)
        acc_ref[...] += _stack_rows([_colsum(dga), _colsum(dgb)], d)
        dya_ref[...] = _dot(dyap, wpa_ref[...]).astype(BF16)
        dyb = _dot(dybp, wpb_ref[...])

        ri = lax.broadcasted_iota(jnp.int32, (GROUP_W, GROUP_W), 0) // HEAD_DIM
        ci = lax.broadcasted_iota(jnp.int32, (GROUP_W, GROUP_W), 1) // HEAD_DIM
        same_head = (ri == ci).astype(BF16)
        alphas = _group_softmax([_load_streams(r, dil, tm) for r, dil in zip((l0_ref, l1_ref, l2_ref), DILATIONS)])
        prod = jnp.zeros((tm, GROUP_W), F32)
        for i, (o_ref, do_ref, dil) in enumerate(zip((o0_ref, o1_ref, o2_ref), (do0_ref, do1_ref, do2_ref), DILATIONS)):
            dov = alphas[i] * dyb[:, i * GROUP_W:(i + 1) * GROUP_W]
            _store_streams(do_ref, dil, tm, dov.astype(BF16))
            prod = prod + dov * _load_streams(o_ref, dil, tm).astype(F32)
        hi = prod.astype(BF16)
        lo = (prod - hi.astype(F32)).astype(BF16)
        dtot = _dot(hi, same_head) + _dot(lo, same_head)
        for alpha, dl_ref, dil in zip(alphas, (dl0_ref, dl1_ref, dl2_ref), DILATIONS):
            _store_streams(dl_ref, dil, tm, alpha * dtot)

    streams = [_stream_spec(dil, tm, GROUP_W) for dil in DILATIONS]
    res = _call(
        body, name="mixer_bwd", grid=(t // tm,),
        in_specs=[_rows(tm, d), _rows(tm, 2 * d), _rows(tm, d), _rows(tm, d)] + streams * 2
        + [_resident((2, d)), _resident((d, d)), _resident((d, cw)), _resident((d, ATTN_W))],
        out_specs=[_rows(tm, 2 * d), _rows(tm, d), _rows(tm, d), _rows(tm, cw)] + streams * 2 + [_acc_spec(d)],
        out_shape=[jax.ShapeDtypeStruct((t, 2 * d), BF16), jax.ShapeDtypeStruct((t, d), BF16),
                   jax.ShapeDtypeStruct((t, d), BF16), jax.ShapeDtypeStruct((t, cw), BF16)]
        + [jax.ShapeDtypeStruct((dil, t // dil, GROUP_W), BF16) for dil in DILATIONS]
        + [jax.ShapeDtypeStruct((dil, t // dil, GROUP_W), F32) for dil in DILATIONS]
        + [jax.ShapeDtypeStruct((SUBLANES, d), F32)],
        args=(dx1, gates, yap, ybp, *[_stream_view(a, dil) for a, dil in zip(os, DILATIONS)],
              *[_stream_view(a, dil) for a, dil in zip(lses, DILATIONS)], b_gate, w_o, w_pa, w_pb),
        semantics=("arbitrary",))
    dgates, dyap, dybp, dya = res[:4]
    dos = [a.reshape(t, GROUP_W) for a in res[4:7]]
    dls = [a.reshape(t, GROUP_W) for a in res[7:10]]
    return dgates, dyap, dybp, dya, dos, dls, res[10]


def _attn_bwd(s, do, lse, dl, dil, carry=None):
    t = s.shape[0] * s.shape[1]
    nb = t // QBLK
    per_stream = nb // dil

    def body(q_ref, qn_ref, kc_ref, kp_ref, vc_ref, vp_ref, do_ref, don_ref, lse_ref, lsen_ref, dl_ref, dln_ref,
             ds_ref):
        b = pl.program_id(0)
        lane, heads = _head_masks()
        first_has_prev = lax.rem(2 * b, per_stream) != 0
        last_has_next = lax.rem(2 * b + 2, per_stream) != 0

        def cols(v):
            return jnp.concatenate([jnp.sum(jnp.where(lane == h * HEAD_DIM, v, 0.0), axis=1, keepdims=True)
                                    for h in range(HEADS_PER_GROUP)], axis=0)

        def pair(qs, dos, k, v, valid, lse_c, dl_c):
            s = jnp.where(valid, _dot_nt(qs, k) * ATTN_SCALE, NEG_INF)
            p = jnp.exp(s - lse_c)
            ds = p * (_dot_nt(dos, v) - dl_c)
            return p.astype(BF16), ds.astype(BF16)

        lo, hi = slice(0, QBLK), slice(QBLK, 2 * QBLK)
        for j, rows in enumerate((lo, hi)):
            q, do, lse, dl = q_ref[rows, :], do_ref[rows, :], lse_ref[rows, :], dl_ref[rows, :]
            kc, vc = kc_ref[rows, :], vc_ref[rows, :]
            if j == 0:
                k2 = jnp.concatenate([kp_ref[...], kc], axis=0)
                v2 = jnp.concatenate([vp_ref[...], vc], axis=0)
                qn, don, lsen, dln = q_ref[hi, :], do_ref[hi, :], lse_ref[hi, :], dl_ref[hi, :]
                mask, mask_n = _band_mask(first_has_prev), _next_mask(True)
            else:
                k2, v2 = kc_ref[...], vc_ref[...]
                qn, don, lsen, dln = qn_ref[...], don_ref[...], lsen_ref[...], dln_ref[...]
                mask, mask_n = _band_mask(True), _next_mask(last_has_next)
            qs, qns = _stack_heads(q, heads), _stack_heads(qn, heads)
            dos, dons = _stack_heads(do, heads), _stack_heads(don, heads)
            p_q, ds_q = pair(qs, dos, k2, v2, mask, cols(lse), cols(dl))
            p_n, ds_n = pair(qns, dons, kc, vc, mask_n, cols(lsen), cols(dln))
            dq = _merge_heads(_dot(ds_q, k2), heads)
            dk = _dot_tn(jnp.concatenate([ds_q[:, QBLK:], ds_n], axis=0), jnp.concatenate([qs, qns], axis=0))
            dv = _dot_tn(jnp.concatenate([p_q[:, QBLK:], p_n], axis=0), jnp.concatenate([dos, dons], axis=0))
            ds_ref[rows, 0:GROUP_W] = (dq * ATTN_SCALE).astype(BF16)
            ds_ref[rows, GROUP_W:2 * GROUP_W] = (dk * ATTN_SCALE).astype(BF16)
            ds_ref[rows, 2 * GROUP_W:3 * GROUP_W] = dv.astype(BF16)

    sv = s.reshape(t, 3 * GROUP_W)
    cur, nxt = _pair_block(0), _edge_block(0, 2, nb)
    return _call(
        body, name=f"attn_bwd_d{dil}", grid=(nb // 2,),
        in_specs=[cur, nxt, _pair_block(1), _edge_block(1, -1, nb), _pair_block(2), _edge_block(2, -1, nb),
                  cur, nxt, cur, nxt, cur, nxt],
        out_specs=[pl.BlockSpec((2 * QBLK, 3 * GROUP_W), lambda b: (b, 0))],
        out_shape=[jax.ShapeDtypeStruct((t, 3 * GROUP_W), BF16)],
        args=(sv, sv, sv, sv, sv, sv, do, do, lse, lse, dl, dl), semantics=("parallel",), carry=carry)


def _conv_mixer_bwd(abcv, dya, conv_w, conv_b):
    t = abcv.shape[0]
    cw = conv_w.shape[1]
    tm = min(256, t)
    last = t // tm - 1

    def body(a_ref, ap_ref, an_ref, dya_ref, dyan_ref, cw_ref, cb_ref, d_ref, acc_ref):
        m = pl.program_id(0)

        @pl.when(m == 0)
        def _():
            acc_ref[...] = jnp.zeros_like(acc_ref)

        ab = a_ref[:, 0:cw].astype(F32)
        ac = a_ref[:, cw:2 * cw].astype(F32)
        av = a_ref[:, 2 * cw:3 * cw].astype(F32)
        u = ac * av
        hu = ap_ref[:, cw:2 * cw].astype(F32) * ap_ref[:, 2 * cw:3 * cw].astype(F32)
        hu = jnp.where(m > 0, hu, 0.0)
        u1 = _shift_down(u, hu, 1)
        u2 = _shift_down(u, hu, 2)
        cv = cw_ref[0:1, :] * u2 + cw_ref[1:2, :] * u1 + cw_ref[2:3, :] * u + cb_ref[...]
        dya_v = dya_ref[...].astype(F32)
        dcv = dya_v * ab
        ndcv = jnp.where(m < last, dyan_ref[...].astype(F32) * an_ref[:, 0:cw].astype(F32), 0.0)
        du = (cw_ref[2:3, :] * dcv + cw_ref[1:2, :] * _shift_up(dcv, ndcv, 1)
              + cw_ref[0:1, :] * _shift_up(dcv, ndcv, 2))
        d_ref[:, 0:cw] = (dya_v * cv).astype(BF16)
        d_ref[:, cw:2 * cw] = (du * av).astype(BF16)
        d_ref[:, 2 * cw:3 * cw] = (du * ac).astype(BF16)
        acc_ref[...] += _stack_rows([_colsum(dcv * u2), _colsum(dcv * u1), _colsum(dcv * u), _colsum(dcv)], cw)

    return pl.pallas_call(
        body, name="conv_mixer_bwd", grid=(t // tm,),
        in_specs=[_rows(tm, 3 * cw), _prev_halo(tm, 3 * cw), _next_halo(tm, 3 * cw, t), _rows(tm, cw),
                  _next_halo(tm, cw, t), _resident((3, cw)), _resident((1, cw))],
        out_specs=[_rows(tm, 3 * cw), _acc_spec(cw)],
        out_shape=[jax.ShapeDtypeStruct((t, 3 * cw), BF16), jax.ShapeDtypeStruct((SUBLANES, cw), F32)],
        compiler_params=_params("arbitrary"),
    )(abcv, abcv, abcv, dya, dya, conv_w, conv_b)


def _in_proj_bwd(x, dx1, dabcv, dss, dgates, w_in, g1, carry=None):
    t, d = x.shape
    qkv0 = dabcv.shape[1]
    n = w_in.shape[0]
    tm = min(ROWS_MATMUL, t)

    def body(x_ref, dx1_ref, da_ref, ds0_ref, ds1_ref, ds2_ref, dg_ref, w_ref, g_ref, dx_ref, acc_ref):
        m = pl.program_id(0)

        @pl.when(m == 0)
        def _():
            acc_ref[...] = jnp.zeros_like(acc_ref)

        dh = _dot(da_ref[...], w_ref[0:qkv0, :]) + _dot(dg_ref[...], w_ref[qkv0 + 3 * ATTN_W:n, :])
        for gi, (ds_ref, dil) in enumerate(zip((ds0_ref, ds1_ref, ds2_ref), DILATIONS)):
            ds = _load_streams(ds_ref, dil, tm)
            for j in range(3):
                c0 = qkv0 + j * ATTN_W + gi * GROUP_W
                dh = dh + _dot(ds[:, j * GROUP_W:(j + 1) * GROUP_W], w_ref[c0:c0 + GROUP_W, :])
        dx, dg = _rms_bwd(x_ref[...], g_ref[...], dh)
        dx_ref[...] = dx1_ref[...] + dx
        acc_ref[...] += _stack_rows([_colsum(dg)], d)

    return _call(
        body, name="in_proj_bwd", grid=(t // tm,),
        in_specs=[_rows(tm, d), _rows(tm, d), _rows(tm, qkv0)]
        + [_stream_spec(dil, tm, 3 * GROUP_W) for dil in DILATIONS]
        + [_rows(tm, 2 * d), _resident((n, d)), _resident((1, d))],
        out_specs=[_rows(tm, d), _acc_spec(d)],
        out_shape=[jax.ShapeDtypeStruct((t, d), F32), jax.ShapeDtypeStruct((SUBLANES, d), F32)],
        args=(x, dx1, dabcv, *[_stream_view(a, dil) for a, dil in zip(dss, DILATIONS)], dgates, w_in, g1),
        semantics=("arbitrary",), carry=carry)


def _dw_in_qkv(ds, h, dil):
    t, d = h.shape
    tk = min(1024, t)
    sub = min(256, t)
    width = 3 * GROUP_W

    def body(ds_ref, h_ref, o_ref, acc_ref):
        k = pl.program_id(0)

        @pl.when(k == 0)
        def _():
            acc_ref[...] = jnp.zeros_like(acc_ref)

        upd = None
        for i in range(tk // sub):
            rows = ds_ref[:, i * (sub // dil):(i + 1) * (sub // dil), :].reshape(sub, width)
            if dil > 1:
                rows = _permute_rows(_perm(dil, sub, inverse=True), rows)
            term = _dot_tn(rows, h_ref[i * sub:(i + 1) * sub, :])
            upd = term if upd is None else upd + term
        acc_ref[...] += upd

        @pl.when(k == t // tk - 1)
        def _():
            o_ref[...] = acc_ref[...].astype(BF16)

    return pl.pallas_call(
        body, name=f"dw_in_qkv_d{dil}", grid=(t // tk,),
        in_specs=[_stream_spec(dil, tk, width), _rows(tk, d)],
        out_specs=pl.BlockSpec((width, d), lambda k: (0, 0)),
        out_shape=jax.ShapeDtypeStruct((width, d), BF16),
        scratch_shapes=[pltpu.VMEM((width, d), F32)],
        compiler_params=_params("arbitrary"),
    )(_stream_view(ds, dil), h)


def _local_step(x, target, p, late):
    cw = p["conv_a_w"].shape[1]
    (h, abcv, gates, *ss), (g_up,) = _in_proj(x, p["norm_mix_g"], p["w_in"], cw,
                                              carry=_Exchange("gather", [late["w_up"]]))
    w_up = _full_from_gathered(g_up)
    mid = ("w_proj_a", "w_proj_b", "w_out")
    (o0, lse0), g_mid = _attn_fwd(ss[0], DILATIONS[0], carry=_Exchange("gather", [late[n] for n in mid]))
    w_pa, w_pb, w_out = [_full_from_gathered(g) for g in g_mid]
    os, lses = zip((o0, lse0), *[_attn_fwd(s, dil) for s, dil in zip(ss[1:], DILATIONS[1:])])
    x1, ya, yb, yap, ybp, merged = _mixer_out(x, abcv, gates, os, lses, p["conv_a_w"], p["conv_a_b"], p["b_gate"],
                                              w_pa, w_pb, w_out)
    (h2, up_pre), (g_down,) = _up_proj(x1, p["norm_ffn_g"], w_up, carry=_Exchange("gather", [late["w_down"]]))
    w_down = _full_from_gathered(g_down)
    act, conv, dx2, acc_gf, loss = _ffn_loss(x1, up_pre, target, p["ffn_conv_w"], p["ffn_conv_b"], w_down,
                                       p["final_norm_g"])

    parts, got = {}, {}
    dup, acc_fb = _ffn_act_bwd(dx2, conv, w_down)
    parts["w_down"] = _by_destination(_tn_matmul(act, dx2, "dw_down"))
    (dpre, dx1, acc_g2, acc_fw), (got["w_down"],) = _ffn_up_bwd(dup, up_pre, x1, dx2, p["ffn_conv_w"], w_up,
                                                                p["norm_ffn_g"],
                                                                carry=_Exchange("scatter", [parts["w_down"]]))
    parts["w_up"] = _by_destination(_tn_matmul(dpre, h2, "dw_up"))
    dgates, dyap, dybp, dya, dos, dls, acc_bg = _mixer_bwd(dx1, gates, yap, ybp, os, lses, p["b_gate"], w_out,
                                                           w_pa, w_pb)
    parts["w_out"] = _by_destination(_tn_matmul(merged, dx1, "dw_out"))
    parts["w_proj_a"] = _by_destination(_tn_matmul(dyap, ya, "dw_proj_a"))
    parts["w_proj_b"] = _by_destination(_tn_matmul(dybp, yb, "dw_proj_b"))
    minor = ("w_out", "w_proj_a", "w_proj_b")
    half = parts["w_up"].shape[1] // 2
    (ds0,), received = _attn_bwd(ss[0], dos[0], lses[0], dls[0], DILATIONS[0],
                                 carry=_Exchange("scatter", [parts[n] for n in minor]))
    got.update(zip(minor, received))
    (ds1,), first_half = _attn_bwd(ss[1], dos[1], lses[1], dls[1], DILATIONS[1],
                                   carry=_Exchange("scatter", [parts["w_up"]], rows=(0, half)))
    (ds2,), (got["w_up"],) = _attn_bwd(ss[2], dos[2], lses[2], dls[2], DILATIONS[2],
                                       carry=_Exchange("scatter", [parts["w_up"]], rows=(half, half),
                                                       into=first_half))
    dss = [ds0, ds1, ds2]
    dabcv, acc_ca = _conv_mixer_bwd(abcv, dya, p["conv_a_w"], p["conv_a_b"])
    dw_s = [_dw_in_qkv(ds, h, dil) for ds, dil in zip(dss, DILATIONS)]
    dw_qkv = [w[j * GROUP_W:(j + 1) * GROUP_W] for j in range(3) for w in dw_s]
    g_w_in = jnp.concatenate([_tn_matmul(dabcv, h, "dw_in_a"), *dw_qkv, _tn_matmul(dgates, h, "dw_in_g")], axis=0)
    parts["w_in"] = _by_destination(g_w_in)
    (dx, acc_g1), (got["w_in"],) = _in_proj_bwd(x, dx1, dabcv, dss, dgates, p["w_in"], p["norm_mix_g"],
                                                carry=_Exchange("scatter", [parts["w_in"]]))
    small = dict(norm_mix_g=acc_g1[0:1], b_gate=acc_bg[0:2], conv_a_w=acc_ca[0:3], conv_a_b=acc_ca[3:4],
                 norm_ffn_g=acc_g2[0:1], ffn_conv_w=acc_fw[0:3], ffn_conv_b=acc_fb[0:1], final_norm_g=acc_gf[0:1])
    return loss[0, 0], dx, parts, got, small


def _all_gather(shards):
    n = len(shards)

    def body(*refs):
        ins, outs = refs[:n], refs[n:2 * n]
        send_sems, recv_sems, local_sems = refs[2 * n:]
        x, y, c = _mesh_pos()
        me, sibling = (x, y, c), (x, y, 1 - c)
        chips = [(1 - x, y), (x, 1 - y), (1 - x, 1 - y)]

        def copy(i, k, block, to, src=None):
            rows = outs[i].at[_dev_index(*block)]
            return pltpu.make_async_remote_copy(
                src_ref=rows if src is None else src, dst_ref=rows, send_sem=send_sems.at[i, k],
                recv_sem=recv_sems.at[i, k], device_id=to, device_id_type=MESH)

        mine, first, passed = [], [], []
        for i in range(n):
            cp = pltpu.make_async_copy(ins[i], outs[i].at[_dev_index(*me)], local_sems.at[i])
            cp.start()
            mine.append(cp)
            first.append(copy(i, 0, me, sibling, src=ins[i]))
            first += [copy(i, 1 + j, me, (*chip, c), src=ins[i]) for j, chip in enumerate(chips)]
        for cp in first:
            cp.start()
        for i in range(n):
            for j, chip in enumerate(chips):
                copy(i, 1 + j, (*chip, c), me).wait_recv()
                fw = copy(i, 4 + j, (*chip, c), sibling)
                fw.start()
                passed.append(fw)
        for i in range(n):
            copy(i, 0, sibling, me).wait_recv()
            for j, chip in enumerate(chips):
                copy(i, 4 + j, (*chip, 1 - c), me).wait_recv()
        for cp in first + passed:
            cp.wait_send()
        for cp in mine:
            cp.wait()

    return pl.pallas_call(
        body, name="all_gather_weights",
        in_specs=[ANY] * n, out_specs=[ANY] * n,
        out_shape=[jax.ShapeDtypeStruct((N_DEV,) + s.shape, s.dtype) for s in shards],
        scratch_shapes=[pltpu.SemaphoreType.DMA((n, 7)), pltpu.SemaphoreType.DMA((n, 7)),
                        pltpu.SemaphoreType.DMA((n,))],
    )(*shards)


def _all_reduce_small(v):
    r = v.shape[0]

    def body(v_ref, o_ref, gath, send_sems, recv_sems):
        x, y, c = _mesh_pos()
        me = _dev_index(x, y, c)
        gath[me] = v_ref[...]
        flips = [(kx, ky, kc) for kx in (0, 1) for ky in (0, 1) for kc in (0, 1)][1:]
        copies = []
        for k, (kx, ky, kc) in enumerate(flips):
            px = 1 - x if kx else x
            py = 1 - y if ky else y
            pc = 1 - c if kc else c
            cp = pltpu.make_async_remote_copy(
                src_ref=v_ref, dst_ref=gath.at[me], send_sem=send_sems.at[k], recv_sem=recv_sems.at[k],
                device_id=(px, py, pc), device_id_type=MESH)
            cp.start()
            copies.append((cp, _dev_index(px, py, pc)))
        for k, (cp, peer) in enumerate(copies):
            pltpu.make_async_remote_copy(
                src_ref=v_ref, dst_ref=gath.at[peer], send_sem=send_sems.at[k], recv_sem=recv_sems.at[k],
                device_id=(x, y, c), device_id_type=MESH).wait_recv()
        for cp, _ in copies:
            cp.wait_send()
        total = gath[0]
        for j in range(1, N_DEV):
            total = total + gath[j]
        o_ref[...] = total

    return pl.pallas_call(
        body, name="all_reduce_small",
        in_specs=[pl.BlockSpec(memory_space=pltpu.VMEM)], out_specs=pl.BlockSpec(memory_space=pltpu.VMEM),
        out_shape=jax.ShapeDtypeStruct((r, LANES), F32),
        scratch_shapes=[pltpu.VMEM((N_DEV, r, LANES), F32), pltpu.SemaphoreType.DMA((7,)),
                        pltpu.SemaphoreType.DMA((7,))],
    )(v)


def _adamw_math(w, g, m, v):
    m2 = ADAM_B1 * m + (1.0 - ADAM_B1) * g
    v2 = ADAM_B2 * v + (1.0 - ADAM_B2) * (g * g)
    m_hat = m2 / (1.0 - ADAM_B1 ** ADAM_STEP)
    v_hat = v2 / (1.0 - ADAM_B2 ** ADAM_STEP)
    delta = -ADAM_LR * (m_hat / (jnp.sqrt(v_hat) + ADAM_EPS) + ADAM_WD * w)
    return delta, m2, v2


def _adamw_big(w, m, v, part, got, me):
    r, c = w.shape
    tr = max(t for t in range(HALO, min(r, 512) + 1, HALO) if r % t == 0)

    def body(me_ref, w_ref, m_ref, v_ref, own_ref, *rest):
        del me_ref
        got_refs, (g_out, d_out, m_out, v_out) = rest[:N_DEV - 1], rest[N_DEV - 1:]
        g = own_ref[...].astype(F32)
        for ref in got_refs:
            g = g + ref[...].astype(F32)
        delta, m2, v2 = _adamw_math(w_ref[...], g, m_ref[...], v_ref[...])
        g_out[...] = g
        d_out[...] = delta
        m_out[...] = m2
        v_out[...] = v2

    def peer_block(k):
        return pl.BlockSpec((None, tr, c), lambda i, me_ref: (jnp.bitwise_xor(me_ref[0], k), i, 0))

    plain = pl.BlockSpec((tr, c), lambda i, me_ref: (i, 0))
    out = jax.ShapeDtypeStruct((r, c), F32)
    return pl.pallas_call(
        body, name="adamw_big",
        grid_spec=pltpu.PrefetchScalarGridSpec(
            num_scalar_prefetch=1, grid=(r // tr,),
            in_specs=[plain, plain, plain] + [peer_block(k) for k in range(N_DEV)],
            out_specs=[plain] * 4),
        out_shape=[out] * 4,
        compiler_params=_params("parallel"),
    )(me, w, m, v, part, *([got] * (N_DEV - 1)))


def _adamw_small(w, g, m, v):
    def body(w_ref, g_ref, m_ref, v_ref, d_out, m_out, v_out):
        delta, m2, v2 = _adamw_math(w_ref[...], g_ref[...], m_ref[...], v_ref[...])
        d_out[...] = delta
        m_out[...] = m2
        v_out[...] = v2

    out = jax.ShapeDtypeStruct(w.shape, F32)
    return pl.pallas_call(body, name="adamw_small", out_shape=[out] * 3)(w, g, m, v)


BIG = ("w_in", "w_proj_a", "w_proj_b", "w_out", "w_up", "w_down")
LATE = ("w_proj_a", "w_proj_b", "w_out", "w_up", "w_down")
COLUMN_SHARDED = ("w_in", "w_proj_a", "w_proj_b", "w_up")
SMALL = ("norm_mix_g", "b_gate", "conv_a_w", "conv_a_b", "norm_ffn_g", "ffn_conv_w", "ffn_conv_b", "final_norm_g")
SMALL_SHARDED = ("b_gate", "conv_a_w", "ffn_conv_w")
WEIGHTS = ("norm_mix_g", "w_in", "b_gate", "conv_a_w", "conv_a_b", "w_proj_a", "w_proj_b", "w_out", "norm_ffn_g",
           "w_up", "ffn_conv_w", "ffn_conv_b", "w_down", "final_norm_g")


def _pack(vectors, rows):
    flat = jnp.concatenate([v.reshape(-1) for v in vectors])
    return jnp.pad(flat, (0, rows * LANES - flat.shape[0])).reshape(rows, LANES)


def _packed_rows(count):
    rows = -(-count // LANES)
    return -(-rows // SUBLANES) * SUBLANES


def _unpack(packed, shapes):
    flat = packed.reshape(-1)
    out, lo = [], 0
    for s in shapes:
        size = 1
        for dim in s:
            size *= dim
        out.append(flat[lo:lo + size].reshape(s))
        lo += size
    return out


def _full_from_gathered(gathered):
    _, r, c = gathered.shape
    return gathered.reshape(N_DEV * r, c)


def _by_destination(grad):
    rr, cc = grad.shape
    return grad.reshape(N_DEV, rr // N_DEV, cc)


def _block2d(name, a):
    a = a.reshape(a.shape[-2:])
    return a.T if name in COLUMN_SHARDED else a


def kernel(x, norm_mix_g, w_in, b_gate, conv_a_w, conv_a_b, w_proj_a, w_proj_b, w_out, norm_ffn_g, w_up, ffn_conv_w, ffn_conv_b, w_down, final_norm_g, loss_target, m_norm_mix_g, m_w_in, m_b_gate, m_conv_a_w, m_conv_a_b, m_w_proj_a, m_w_proj_b, m_w_out, m_norm_ffn_g, m_w_up, m_ffn_conv_w, m_ffn_conv_b, m_w_down, m_final_norm_g, v_norm_mix_g, v_w_in, v_b_gate, v_conv_a_w, v_conv_a_b, v_w_proj_a, v_w_proj_b, v_w_out, v_norm_ffn_g, v_w_up, v_ffn_conv_w, v_ffn_conv_b, v_w_down, v_final_norm_g):
    given = dict(locals())
    shard = {n: given[n] for n in WEIGHTS}
    mom_m = {n: given["m_" + n] for n in WEIGHTS}
    mom_v = {n: given["v_" + n] for n in WEIGHTS}
    xi, yi, ci = _mesh_pos()
    me = _dev_index(xi, yi, ci)
    me1 = me.astype(jnp.int32).reshape(1)

    big2d = {n: _block2d(n, shard[n]) for n in BIG}
    small_shapes = [shard[n].shape[1:] for n in SMALL_SHARDED]
    n_small = sum(s[0] * s[1] for s in small_shapes)
    packed_small = _pack([shard[n] for n in SMALL_SHARDED], _packed_rows(n_small))
    gathered = _all_gather([big2d["w_in"].astype(BF16), packed_small])
    p = {"w_in": _full_from_gathered(gathered[0])}
    per_dev = [_unpack(gathered[-1][j], small_shapes) for j in range(N_DEV)]
    for i, n in enumerate(SMALL_SHARDED):
        p[n] = jnp.concatenate([per_dev[j][i] for j in range(N_DEV)], axis=1)
    p["norm_mix_g"], p["norm_ffn_g"] = shard["norm_mix_g"], shard["norm_ffn_g"]
    p["conv_a_b"], p["ffn_conv_b"] = shard["conv_a_b"], shard["ffn_conv_b"]
    p["final_norm_g"] = shard["final_norm_g"].reshape(1, -1)
    late = {n: big2d[n].astype(BF16) for n in LATE}

    loss_part, dx, parts, got, g_small = _local_step(x[0], loss_target[0], p, late)

    results = {}
    for n in BIG:
        outs = _adamw_big(big2d[n], _block2d(n, mom_m[n]), _block2d(n, mom_v[n]), parts[n], got[n], me1)
        results[n] = [_block2d(n, o).reshape(shard[n].shape) for o in outs]

    small_full_shapes = [g_small[n].shape for n in SMALL]
    n_vec = sum(s[0] * s[1] for s in small_full_shapes) + 1
    packed = _pack([g_small[n] for n in SMALL] + [loss_part.reshape(1)], _packed_rows(n_vec))
    reduced = _all_reduce_small(packed)
    *g_full, loss_vec = _unpack(reduced, small_full_shapes + [(1,)])
    loss = loss_vec[0]
    own_g = []
    for n, g in zip(SMALL, g_full):
        if n in SMALL_SHARDED:
            width = shard[n].shape[-1]
            g = lax.dynamic_slice_in_dim(g, me * width, width, axis=1)
        own_g.append(g.reshape(shard[n].shape))
    own_shapes = [shard[n].shape for n in SMALL]
    rows = _packed_rows(sum(g.size for g in own_g))
    small_out = _adamw_small(_pack([shard[n] for n in SMALL], rows), _pack(own_g, rows),
                             _pack([mom_m[n] for n in SMALL], rows), _pack([mom_v[n] for n in SMALL], rows))
    deltas, new_ms, new_vs = (_unpack(o, own_shapes) for o in small_out)
    for i, n in enumerate(SMALL):
        results[n] = [own_g[i], deltas[i], new_ms[i], new_vs[i]]

    grad_x = dx.reshape(x.shape)
    return (loss, grad_x, *[results[n][0] for n in WEIGHTS], *[results[n][1] for n in WEIGHTS],
            *[results[n][2] for n in WEIGHTS], *[results[n][3] for n in WEIGHTS])
```

```python
import functools

import jax
import jax.numpy as jnp
from jax import lax
from jax.experimental import pallas as pl
from jax.experimental.pallas import tpu as pltpu

F32 = jnp.float32
BF16 = jnp.bfloat16
MESH = pl.DeviceIdType.MESH

N_DEV = 8
RMS_EPS = 1e-6
NEG_INF = -1e30
N_GROUPS = 3
DILATIONS = (1, 4, 16)
HEADS_PER_GROUP = 4
HEAD_DIM = 64
GROUP_W = HEADS_PER_GROUP * HEAD_DIM
ATTN_W = N_GROUPS * GROUP_W
QBLK = 128
ATTN_SCALE = HEAD_DIM ** -0.5

ADAM_LR = 0.001
ADAM_B1 = 0.9
ADAM_B2 = 0.999
ADAM_EPS = 1e-08
ADAM_WD = 0.01
ADAM_STEP = 10

PERM_TOKENS = 256
ROWS_MATMUL = 512
HALO = 16
LANES = 128
SUBLANES = 8
VMEM_LIMIT_BYTES = 56 * 1024 * 1024


def _params(*sem):
    return pltpu.CompilerParams(dimension_semantics=sem, vmem_limit_bytes=VMEM_LIMIT_BYTES)


def _pick_tile(n, cap):
    if n <= cap:
        return n
    best = None
    for t in range(LANES, cap + 1, LANES):
        if n % t == 0:
            best = t
    assert best is not None, (n, cap)
    return best


def _rows(tm, c, j=0):
    return pl.BlockSpec((tm, c), lambda m: (m, j))


def _prev_halo(tm, c):
    return pl.BlockSpec((HALO, c), lambda m: (jnp.maximum(m * (tm // HALO) - 1, 0), 0))


def _next_halo(tm, c, t_total):
    last = t_total // HALO - 1
    return pl.BlockSpec((HALO, c), lambda m: (jnp.minimum((m + 1) * (tm // HALO), last), 0))


def _resident(shape):
    nd = len(shape)
    return pl.BlockSpec(shape, lambda *_: (0,) * nd, pipeline_mode=pl.Buffered(1))


def _acc_spec(c):
    return pl.BlockSpec((SUBLANES, c), lambda *_: (0, 0))


def _shift_down(u, halo, k):
    edge = jnp.concatenate([halo[HALO - SUBLANES:], u[:SUBLANES]], axis=0)
    head = pltpu.roll(edge, k, 0)[SUBLANES:]
    return jnp.concatenate([head, pltpu.roll(u, k, 0)[SUBLANES:]], axis=0)


def _shift_up(u, halo, k):
    n = u.shape[0]
    edge = jnp.concatenate([u[n - SUBLANES:], halo[:SUBLANES]], axis=0)
    tail = pltpu.roll(edge, 2 * SUBLANES - k, 0)[:SUBLANES]
    return jnp.concatenate([pltpu.roll(u, n - k, 0)[:n - SUBLANES], tail], axis=0)


def _stack_rows(rows, c):
    idx = lax.broadcasted_iota(jnp.int32, (SUBLANES, c), 0)
    out = jnp.zeros((SUBLANES, c), F32)
    for i, r in enumerate(rows):
        out = out + jnp.where(idx == i, r, 0.0)
    return out


def _colsum(v):
    return jnp.sum(v, axis=0, keepdims=True)


def _sigmoid(v):
    return 0.5 * jnp.tanh(0.5 * v) + 0.5


def _rms_fwd(xv, g):
    r = lax.rsqrt(jnp.mean(xv * xv, axis=-1, keepdims=True) + RMS_EPS)
    return xv * r * g, r


def _rms_bwd(xv, g, dy):
    r = lax.rsqrt(jnp.mean(xv * xv, axis=-1, keepdims=True) + RMS_EPS)
    xn = xv * r
    dxn = dy * g
    dx = r * (dxn - xn * jnp.mean(dxn * xn, axis=-1, keepdims=True))
    return dx, dy * xn


def _dot(a, b):
    return jnp.dot(a, b, preferred_element_type=F32)


def _dot_nt(a, b):
    return lax.dot_general(a, b, (((1,), (1,)), ((), ())), preferred_element_type=F32)


def _dot_tn(a, b):
    return lax.dot_general(a, b, (((0,), (0,)), ((), ())), preferred_element_type=F32)


def _perm(dil, n, inverse=False):
    i = lax.broadcasted_iota(jnp.int32, (n, n), 0)
    j = lax.broadcasted_iota(jnp.int32, (n, n), 1)
    if inverse:
        i, j = j, i
    per = n // dil
    return (j == (i % per) * dil + i // per).astype(BF16)


def _permute_rows(pm, v):
    if v.dtype == BF16:
        return _dot(pm, v).astype(BF16)
    hi = v.astype(BF16)
    lo = (v - hi.astype(F32)).astype(BF16)
    return _dot(pm, hi) + _dot(pm, lo)


def _stream_view(a, dil):
    t, c = a.shape
    return a.reshape(dil, t // dil, c)


def _stream_spec(dil, tm, c):
    return pl.BlockSpec((dil, tm // dil, c), lambda m: (0, m, 0))


def _load_streams(ref, dil, tm):
    c = ref.shape[-1]
    if dil == 1:
        return ref[...].reshape(tm, c)
    sub = min(PERM_TOKENS, tm)
    pm = _perm(dil, sub, inverse=True)
    parts = [_permute_rows(pm, ref[:, i * (sub // dil):(i + 1) * (sub // dil), :].reshape(sub, c))
             for i in range(tm // sub)]
    return parts[0] if len(parts) == 1 else jnp.concatenate(parts, axis=0)


def _store_streams(ref, dil, tm, v):
    if dil == 1:
        ref[...] = v.reshape(ref.shape).astype(ref.dtype)
        return
    sub = min(PERM_TOKENS, tm)
    pm = _perm(dil, sub)
    for i in range(tm // sub):
        piece = _permute_rows(pm, v[i * sub:(i + 1) * sub])
        ref[:, i * (sub // dil):(i + 1) * (sub // dil), :] = piece.reshape(dil, sub // dil, -1).astype(ref.dtype)


ANY = pl.BlockSpec(memory_space=pl.ANY)


def _mesh_pos():
    return lax.axis_index("x"), lax.axis_index("y"), lax.axis_index("c")


def _dev_index(px, py, pc):
    return 4 * px + 2 * py + pc


class _Exchange:
    def __init__(self, mode, arrays, rows=None, into=()):
        self.mode, self.arrays, self.rows, self.into = mode, list(arrays), rows, list(into)
        n = len(self.arrays)
        if mode == "gather":
            self.out_shape = [jax.ShapeDtypeStruct((N_DEV,) + a.shape, a.dtype) for a in self.arrays]
        else:
            self.out_shape = [jax.ShapeDtypeStruct(a.shape, a.dtype) for a in self.arrays]
        self.scratch = [pltpu.SemaphoreType.DMA((n, N_DEV - 1)), pltpu.SemaphoreType.DMA((n, N_DEV - 1)),
                        pltpu.SemaphoreType.DMA((n,))]

    def _peers(self):
        x, y, c = _mesh_pos()
        flips = [(kx, ky, kc) for kx in (0, 1) for ky in (0, 1) for kc in (0, 1)][1:]
        peers = [(1 - x if kx else x, 1 - y if ky else y, 1 - c if kc else c) for kx, ky, kc in flips]
        return _dev_index(x, y, c), peers

    def _copy(self, ins, outs, sems, i, k, peer, me, sending):
        src = ins[i] if self.mode == "gather" else ins[i].at[_dev_index(*peer)]
        dst = outs[i].at[me if sending else _dev_index(*peer)]
        if self.rows is not None:
            src, dst = src.at[pl.ds(*self.rows)], dst.at[pl.ds(*self.rows)]
        return pltpu.make_async_remote_copy(src_ref=src, dst_ref=dst, send_sem=sems[0].at[i, k],
                                            recv_sem=sems[1].at[i, k], device_id=peer, device_id_type=MESH)

    def _own(self, ins, outs, sems, i, me):
        return pltpu.make_async_copy(ins[i], outs[i].at[me], sems[2].at[i])

    def start(self, ins, outs, sems):
        me, peers = self._peers()
        for i in range(len(ins)):
            if self.mode == "gather":
                self._own(ins, outs, sems, i, me).start()
            for k, peer in enumerate(peers):
                self._copy(ins, outs, sems, i, k, peer, me, True).start()

    def wait(self, ins, outs, sems):
        me, peers = self._peers()
        for i in range(len(ins)):
            for k, peer in enumerate(peers):
                self._copy(ins, outs, sems, i, k, peer, me, False).wait_recv()
            for k, peer in enumerate(peers):
                self._copy(ins, outs, sems, i, k, peer, me, True).wait_send()
            if self.mode == "gather":
                self._own(ins, outs, sems, i, me).wait()


def _call(body, *, name, grid, in_specs, out_specs, out_shape, args, semantics, carry=None, scratch=()):
    if carry is None:
        return pl.pallas_call(body, name=name, grid=grid, in_specs=in_specs, out_specs=out_specs,
                              out_shape=out_shape, scratch_shapes=list(scratch),
                              compiler_params=_params(*semantics))(*args)
    n_in, n_out, n_x, n_s = len(in_specs), len(out_specs), len(carry.arrays), len(scratch)
    n_into = len(carry.into)
    all_in = n_in + n_x + n_into

    def carried(*refs):
        ins, x_ins = refs[:n_in], refs[n_in:n_in + n_x]
        outs = refs[all_in:all_in + n_out]
        x_outs = refs[all_in + n_out:all_in + n_out + n_x]
        own = refs[all_in + n_out + n_x:all_in + n_out + n_x + n_s]
        sems = refs[all_in + n_out + n_x + n_s:]
        first = functools.reduce(jnp.logical_and, [pl.program_id(a) == 0 for a in range(len(grid))])
        last = functools.reduce(jnp.logical_and, [pl.program_id(a) == grid[a] - 1 for a in range(len(grid))])

        @pl.when(first)
        def _():
            carry.start(x_ins, x_outs, sems)

        body(*ins, *outs, *own)

        @pl.when(last)
        def _():
            carry.wait(x_ins, x_outs, sems)

    res = pl.pallas_call(
        carried, name=name, grid=grid, in_specs=list(in_specs) + [ANY] * (n_x + n_into),
        out_specs=list(out_specs) + [ANY] * n_x, out_shape=list(out_shape) + carry.out_shape,
        input_output_aliases={n_in + n_x + i: n_out + i for i in range(n_into)},
        scratch_shapes=list(scratch) + carry.scratch, compiler_params=_params(*["arbitrary"] * len(grid)),
    )(*args, *carry.arrays, *carry.into)
    return list(res[:n_out]), list(res[n_out:])


def _in_proj(x, g, wt, cw, carry=None):
    t, d = x.shape
    n = wt.shape[0]
    tm = min(ROWS_MATMUL, t)
    qkv0 = 3 * cw

    def body(x_ref, g_ref, wt_ref, h_ref, abcv_ref, gates_ref, *s_refs):
        h = _rms_fwd(x_ref[...], g_ref[...])[0].astype(BF16)
        h_ref[...] = h
        abcv_ref[...] = _dot_nt(h, wt_ref[0:qkv0, :]).astype(BF16)
        gates_ref[...] = _dot_nt(h, wt_ref[qkv0 + 3 * ATTN_W:n, :]).astype(BF16)
        for gi, s_ref in enumerate(s_refs):
            cols = [_dot_nt(h, wt_ref[qkv0 + j * ATTN_W + gi * GROUP_W:qkv0 + j * ATTN_W + (gi + 1) * GROUP_W, :])
                    for j in range(3)]
            _store_streams(s_ref, DILATIONS[gi], tm, jnp.concatenate(cols, axis=1).astype(BF16))

    return _call(
        body, name="in_proj", grid=(t // tm,),
        in_specs=[_rows(tm, d), _resident((1, d)), _resident((n, d))],
        out_specs=[_rows(tm, d), _rows(tm, qkv0), _rows(tm, 2 * d)]
        + [_stream_spec(dil, tm, 3 * GROUP_W) for dil in DILATIONS],
        out_shape=[jax.ShapeDtypeStruct((t, d), BF16), jax.ShapeDtypeStruct((t, qkv0), BF16),
                   jax.ShapeDtypeStruct((t, 2 * d), BF16)]
        + [jax.ShapeDtypeStruct((dil, t // dil, 3 * GROUP_W), BF16) for dil in DILATIONS],
        args=(x, g, wt), semantics=("parallel",), carry=carry)


def _head_masks():
    lane = lax.broadcasted_iota(jnp.int32, (1, GROUP_W), 1)
    return lane, [(lane // HEAD_DIM) == h for h in range(HEADS_PER_GROUP)]


def _stack_heads(v, heads):
    return jnp.concatenate([jnp.where(hm, v, jnp.zeros_like(v)) for hm in heads], axis=0)


def _merge_heads(v, heads):
    out = jnp.zeros((QBLK, GROUP_W), v.dtype)
    for h, hm in enumerate(heads):
        out = jnp.where(hm, v[h * QBLK:(h + 1) * QBLK], out)
    return out


def _pair_block(col):
    return pl.BlockSpec((2 * QBLK, GROUP_W), lambda b: (b, col))


def _edge_block(col, shift, nb):
    return pl.BlockSpec((QBLK, GROUP_W), lambda b: (jnp.clip(2 * b + shift, 0, nb - 1), col))


def _band_mask(has_prev):
    rows = HEADS_PER_GROUP * QBLK
    row = lax.broadcasted_iota(jnp.int32, (rows, 2 * QBLK), 0) & (QBLK - 1)
    col = lax.broadcasted_iota(jnp.int32, (rows, 2 * QBLK), 1)
    return ((col < QBLK) & (col >= row) & has_prev) | ((col >= QBLK) & (col - QBLK <= row))


def _next_mask(has_next):
    rows = HEADS_PER_GROUP * QBLK
    row = lax.broadcasted_iota(jnp.int32, (rows, QBLK), 0) & (QBLK - 1)
    col = lax.broadcasted_iota(jnp.int32, (rows, QBLK), 1)
    return (col >= row) & has_next


def _attn_fwd(s, dil, carry=None):
    t = s.shape[0] * s.shape[1]
    nb = t // QBLK
    per_stream = nb // dil
    assert per_stream % 2 == 0

    def body(q_ref, kc_ref, kp_ref, vc_ref, vp_ref, o_ref, lse_ref):
        b = pl.program_id(0)
        _, heads = _head_masks()
        first_has_prev = lax.rem(2 * b, per_stream) != 0
        for j in range(2):
            rows = slice(j * QBLK, (j + 1) * QBLK)
            if j == 0:
                k2 = jnp.concatenate([kp_ref[...], kc_ref[rows, :]], axis=0)
                v2 = jnp.concatenate([vp_ref[...], vc_ref[rows, :]], axis=0)
            else:
                k2, v2 = kc_ref[...], vc_ref[...]
            mask = _band_mask(first_has_prev if j == 0 else True)
            sc = jnp.where(mask, _dot_nt(_stack_heads(q_ref[rows, :], heads), k2) * ATTN_SCALE, NEG_INF)
            mx = jnp.max(sc, axis=1, keepdims=True)
            pr = jnp.exp(sc - mx)
            den = jnp.sum(pr, axis=1, keepdims=True)
            o_all = _dot(pr.astype(BF16), v2) / den
            o_ref[rows, :] = _merge_heads(o_all, heads).astype(BF16)
            lse_ref[rows, :] = _merge_heads(jnp.broadcast_to(mx + jnp.log(den), o_all.shape), heads)

    sv = s.reshape(t, 3 * GROUP_W)
    return _call(
        body, name=f"attn_fwd_d{dil}", grid=(nb // 2,),
        in_specs=[_pair_block(0), _pair_block(1), _edge_block(1, -1, nb), _pair_block(2), _edge_block(2, -1, nb)],
        out_specs=[_pair_block(0), _pair_block(0)],
        out_shape=[jax.ShapeDtypeStruct((t, GROUP_W), BF16), jax.ShapeDtypeStruct((t, GROUP_W), F32)],
        args=(sv, sv, sv, sv, sv), semantics=("parallel",), carry=carry)


def _group_softmax(parts):
    mx = jnp.maximum(jnp.maximum(parts[0], parts[1]), parts[2])
    es = [jnp.exp(p - mx) for p in parts]
    den = es[0] + es[1] + es[2]
    return [e / den for e in es]


def _mixer_out(x, abcv, gates, os, lses, conv_w, conv_b, b_gate, w_pa, w_pb, w_o, carry=None):
    t, d = x.shape
    cw = conv_w.shape[1]
    tm = min(256, t)

    def body(x_ref, abcv_ref, halo_ref, gates_ref, o0_ref, o1_ref, o2_ref, l0_ref, l1_ref, l2_ref, cw_ref, cb_ref,
             bg_ref, wpa_ref, wpb_ref, wo_ref, x1_ref, ya_ref, yb_ref, yap_ref, ybp_ref, mg_ref):
        m = pl.program_id(0)
        ab = abcv_ref[:, 0:cw].astype(F32)
        u = abcv_ref[:, cw:2 * cw].astype(F32) * abcv_ref[:, 2 * cw:3 * cw].astype(F32)
        hu = halo_ref[:, cw:2 * cw].astype(F32) * halo_ref[:, 2 * cw:3 * cw].astype(F32)
        hu = jnp.where(m > 0, hu, 0.0)
        cv = (cw_ref[0:1, :] * _shift_down(u, hu, 2) + cw_ref[1:2, :] * _shift_down(u, hu, 1)
              + cw_ref[2:3, :] * u + cb_ref[...])
        ya = (ab * cv).astype(BF16)
        ya_ref[...] = ya
        alphas = _group_softmax([_load_streams(r, dil, tm) for r, dil in zip((l0_ref, l1_ref, l2_ref), DILATIONS)])
        for i, (o_ref, dil) in enumerate(zip((o0_ref, o1_ref, o2_ref), DILATIONS)):
            sl = slice(i * GROUP_W, (i + 1) * GROUP_W)
            yb_ref[:, sl] = (alphas[i] * _load_streams(o_ref, dil, tm).astype(F32)).astype(BF16)
        yap = _dot_nt(ya, wpa_ref[...])
        ybp = _dot_nt(yb_ref[...], wpb_ref[...])
        yap_ref[...] = yap.astype(BF16)
        ybp_ref[...] = ybp.astype(BF16)
        sa = _sigmoid(gates_ref[:, 0:d].astype(F32) + bg_ref[0:1, :])
        sb = _sigmoid(gates_ref[:, d:2 * d].astype(F32) + bg_ref[1:2, :])
        merged = (sa * yap + sb * ybp).astype(BF16)
        mg_ref[...] = merged
        x1_ref[...] = x_ref[...] + _dot(merged, wo_ref[...])

    return _call(
        body, name="mixer_out", grid=(t // tm,),
        in_specs=[_rows(tm, d), _rows(tm, 3 * cw), _prev_halo(tm, 3 * cw), _rows(tm, 2 * d)]
        + [_stream_spec(dil, tm, GROUP_W) for dil in DILATIONS] * 2
        + [_resident((3, cw)), _resident((1, cw)), _resident((2, d)),
           _resident((d, cw)), _resident((d, ATTN_W)), _resident((d, d))],
        out_specs=[_rows(tm, d), _rows(tm, cw), _rows(tm, ATTN_W), _rows(tm, d), _rows(tm, d), _rows(tm, d)],
        out_shape=[jax.ShapeDtypeStruct((t, d), F32), jax.ShapeDtypeStruct((t, cw), BF16),
                   jax.ShapeDtypeStruct((t, ATTN_W), BF16), jax.ShapeDtypeStruct((t, d), BF16),
                   jax.ShapeDtypeStruct((t, d), BF16), jax.ShapeDtypeStruct((t, d), BF16)],
        args=(x, abcv, abcv, gates, *[_stream_view(a, dil) for a, dil in zip(os, DILATIONS)],
              *[_stream_view(a, dil) for a, dil in zip(lses, DILATIONS)], conv_w, conv_b, b_gate, w_pa, w_pb, w_o),
        semantics=("parallel",), carry=carry)


def _ffn_fwd(x1, target, g2, w_ut, conv_w, conv_b, w_d, g_f, carry=None):
    t, d = x1.shape
    dff = w_d.shape[0]
    tm = min(256, t)
    ck = _pick_tile(dff, 1408)

    def body(x1_ref, tg_ref, g2_ref, wut_ref, cw_ref, cb_ref, wd_ref, gf_ref, h2_ref, up_ref, act_ref, conv_ref,
             dx2_ref, acc_ref, loss_ref, halo_ref):
        m = pl.program_id(0)

        @pl.when(m == 0)
        def _():
            acc_ref[...] = jnp.zeros_like(acc_ref)
            loss_ref[...] = jnp.zeros_like(loss_ref)
            halo_ref[...] = jnp.zeros_like(halo_ref)

        h2 = _rms_fwd(x1_ref[...], g2_ref[...])[0].astype(BF16)
        h2_ref[...] = h2

        def conv(c0):
            up = _dot_nt(h2, wut_ref[c0:c0 + ck, :]).astype(BF16)
            up_ref[:, c0:c0 + ck] = up
            p = up.astype(F32)
            hp = halo_ref[:, c0:c0 + ck]
            halo_ref[:, c0:c0 + ck] = p[tm - HALO:, :]
            return (cw_ref[0:1, c0:c0 + ck] * _shift_down(p, hp, 2) + cw_ref[1:2, c0:c0 + ck] * _shift_down(p, hp, 1)
                    + cw_ref[2:3, c0:c0 + ck] * p + cb_ref[:, c0:c0 + ck])

        x2 = x1_ref[...]
        for c0 in range(0, dff, ck):
            gate = conv(c0)
            val = conv(dff + c0)
            conv_ref[:, c0:c0 + ck] = gate.astype(BF16)
            conv_ref[:, dff + c0:dff + c0 + ck] = val.astype(BF16)
            act = (gate * _sigmoid(gate) * val).astype(BF16)
            act_ref[:, c0:c0 + ck] = act
            x2 = x2 + _dot(act, wd_ref[c0:c0 + ck, :])
        y, _ = _rms_fwd(x2, gf_ref[...])
        diff = y - tg_ref[...]
        loss_ref[...] += 0.5 * jnp.sum(jnp.mean(diff * diff, axis=-1, keepdims=True))
        dx2, dg = _rms_bwd(x2, gf_ref[...], diff * (1.0 / d))
        dx2_ref[...] = dx2
        acc_ref[...] += _stack_rows([_colsum(dg)], d)

    return _call(
        body, name="ffn_fwd", grid=(t // tm,),
        in_specs=[_rows(tm, d), _rows(tm, d), _resident((1, d)), _resident((2 * dff, d)), _resident((3, 2 * dff)),
                  _resident((1, 2 * dff)), _resident((dff, d)), _resident((1, d))],
        out_specs=[_rows(tm, d), _rows(tm, 2 * dff), _rows(tm, dff), _rows(tm, 2 * dff), _rows(tm, d), _acc_spec(d),
                   _acc_spec(LANES)],
        out_shape=[jax.ShapeDtypeStruct((t, d), BF16), jax.ShapeDtypeStruct((t, 2 * dff), BF16),
                   jax.ShapeDtypeStruct((t, dff), BF16), jax.ShapeDtypeStruct((t, 2 * dff), BF16),
                   jax.ShapeDtypeStruct((t, d), F32), jax.ShapeDtypeStruct((SUBLANES, d), F32),
                   jax.ShapeDtypeStruct((SUBLANES, LANES), F32)],
        args=(x1, target, g2, w_ut, conv_w, conv_b, w_d, g_f), semantics=("arbitrary",), carry=carry,
        scratch=[pltpu.VMEM((HALO, 2 * dff), F32)])


def _ffn_act_bwd(dx2, conv, w_d):
    t, d = dx2.shape
    dff = w_d.shape[0]
    tm = min(256, t)
    ck = _pick_tile(dff, 1408)

    def body(dx2_ref, conv_ref, wd_ref, dup_ref, acc_ref):
        m = pl.program_id(0)

        @pl.when(m == 0)
        def _():
            acc_ref[...] = jnp.zeros_like(acc_ref)

        dx2v = dx2_ref[...].astype(BF16)
        for c0 in range(0, dff, ck):
            dact = _dot_nt(dx2v, wd_ref[c0:c0 + ck, :])
            gate = conv_ref[:, c0:c0 + ck].astype(F32)
            val = conv_ref[:, dff + c0:dff + c0 + ck].astype(F32)
            sg = _sigmoid(gate)
            dval = dact * gate * sg
            dgate = dact * val * sg * (1.0 + gate * (1.0 - sg))
            dup_ref[:, c0:c0 + ck] = dgate.astype(BF16)
            dup_ref[:, dff + c0:dff + c0 + ck] = dval.astype(BF16)
            acc_ref[:, c0:c0 + ck] += _stack_rows([_colsum(dgate)], ck)
            acc_ref[:, dff + c0:dff + c0 + ck] += _stack_rows([_colsum(dval)], ck)

    return pl.pallas_call(
        body, name="ffn_act_bwd", grid=(t // tm,),
        in_specs=[_rows(tm, d), _rows(tm, 2 * dff), _resident((dff, d))],
        out_specs=[_rows(tm, 2 * dff), _acc_spec(2 * dff)],
        out_shape=[jax.ShapeDtypeStruct((t, 2 * dff), BF16), jax.ShapeDtypeStruct((SUBLANES, 2 * dff), F32)],
        compiler_params=_params("arbitrary"),
    )(dx2, conv, w_d)


def _ffn_up_bwd(dup, up_pre, x1, dx2, conv_w, w_u, g2, carry=None):
    t, d = x1.shape
    n = dup.shape[1]
    tm = min(256, t)
    ck = _pick_tile(n, 1408)
    last = t // tm - 1

    def body(dup_ref, nxt_ref, up_ref, x1_ref, dx2_ref, cw_ref, wu_ref, g2_ref, dpre_ref, dx1_ref, acc_ref, accw_ref):
        m = pl.program_id(0)

        @pl.when(m == 0)
        def _():
            acc_ref[...] = jnp.zeros_like(acc_ref)
            accw_ref[...] = jnp.zeros_like(accw_ref)

        dh = jnp.zeros((tm, d), F32)
        for c0 in range(0, n, ck):
            du = dup_ref[:, c0:c0 + ck].astype(F32)
            hn = jnp.where(m < last, nxt_ref[:, c0:c0 + ck].astype(F32), 0.0)
            du1 = _shift_up(du, hn, 1)
            du2 = _shift_up(du, hn, 2)
            dpre = (cw_ref[2:3, c0:c0 + ck] * du + cw_ref[1:2, c0:c0 + ck] * du1
                    + cw_ref[0:1, c0:c0 + ck] * du2).astype(BF16)
            dpre_ref[:, c0:c0 + ck] = dpre
            dh = dh + _dot(dpre, wu_ref[c0:c0 + ck, :])
            p = up_ref[:, c0:c0 + ck].astype(F32)
            accw_ref[:, c0:c0 + ck] += _stack_rows([_colsum(du2 * p), _colsum(du1 * p), _colsum(du * p)], ck)
        dx, dg = _rms_bwd(x1_ref[...], g2_ref[...], dh)
        dx1_ref[...] = dx2_ref[...] + dx
        acc_ref[...] += _stack_rows([_colsum(dg)], d)

    return _call(
        body, name="ffn_up_bwd", grid=(t // tm,),
        in_specs=[_rows(tm, n), _next_halo(tm, n, t), _rows(tm, n), _rows(tm, d), _rows(tm, d), _resident((3, n)),
                  _resident((n, d)), _resident((1, d))],
        out_specs=[_rows(tm, n), _rows(tm, d), _acc_spec(d), _acc_spec(n)],
        out_shape=[jax.ShapeDtypeStruct((t, n), BF16), jax.ShapeDtypeStruct((t, d), F32),
                   jax.ShapeDtypeStruct((SUBLANES, d), F32), jax.ShapeDtypeStruct((SUBLANES, n), F32)],
        args=(dup, dup, up_pre, x1, dx2, conv_w, w_u, g2), semantics=("arbitrary",), carry=carry)


def _tn_matmul(a, b, name):
    t, mdim = a.shape
    n = b.shape[1]
    tk = min(1024, t)
    tmm = _pick_tile(mdim, 1536)
    tn = _pick_tile(n, 1024)

    def body(a_ref, b_ref, o_ref, acc_ref):
        k = pl.program_id(2)

        @pl.when(k == 0)
        def _():
            acc_ref[...] = jnp.zeros_like(acc_ref)

        acc_ref[...] += _dot_tn(a_ref[...].astype(BF16), b_ref[...].astype(BF16))

        @pl.when(k == t // tk - 1)
        def _():
            o_ref[...] = acc_ref[...].astype(BF16)

    return pl.pallas_call(
        body, name=name, grid=(mdim // tmm, n // tn, t // tk),
        in_specs=[pl.BlockSpec((tk, tmm), lambda i, j, k: (k, i)), pl.BlockSpec((tk, tn), lambda i, j, k: (k, j))],
        out_specs=pl.BlockSpec((tmm, tn), lambda i, j, k: (i, j)),
        out_shape=jax.ShapeDtypeStruct((mdim, n), BF16),
        scratch_shapes=[pltpu.VMEM((tmm, tn), F32)],
        compiler_params=_params("parallel", "parallel", "arbitrary"),
    )(a, b)


def _mixer_bwd(dx1, gates, yap, ybp, os, lses, b_gate, w_o, w_pa, w_pb):
    t, d = dx1.shape
    cw = w_pa.shape[1]
    tm = min(256, t)

    def body(dx1_ref, gates_ref, yap_ref, ybp_ref, o0_ref, o1_ref, o2_ref, l0_ref, l1_ref, l2_ref, bg_ref, wo_ref,
             wpa_ref, wpb_ref, dgates_ref, dyap_ref, dybp_ref, dya_ref, do0_ref, do1_ref, do2_ref, dl0_ref, dl1_ref,
             dl2_ref, acc_ref):
        m = pl.program_id(0)

        @pl.when(m == 0)
        def _():
            acc_ref[...] = jnp.zeros_like(acc_ref)

        dmg = _dot_nt(dx1_ref[...].astype(BF16), wo_ref[...])
        sa = _sigmoid(gates_ref[:, 0:d].astype(F32) + bg_ref[0:1, :])
        sb = _sigmoid(gates_ref[:, d:2 * d].astype(F32) + bg_ref[1:2, :])
        dyap = (dmg * sa).astype(BF16)
        dybp = (dmg * sb).astype(BF16)
        dga = dmg * yap_ref[...].astype(F32) * sa * (1.0 - sa)
        dgb = dmg * ybp_ref[...].astype(F32) * sb * (1.0 - sb)
        dyap_ref[...] = dyap
        dybp_ref[...] = dybp
        dgates_ref[:, 0:d] = dga.astype(BF16)
        dgates_ref[:, d:2 * d] = dgb.astype(BF16)
        acc_ref[...] += _stack_rows([_colsum(dga), _colsum(dgb)], d)
        dya_ref[...] = _dot(dyap, wpa_ref[...]).astype(BF16)
        dyb = _dot(dybp, wpb_ref[...])

        ri = lax.broadcasted_iota(jnp.int32, (GROUP_W, GROUP_W), 0) // HEAD_DIM
        ci = lax.broadcasted_iota(jnp.int32, (GROUP_W, GROUP_W), 1) // HEAD_DIM
        same_head = (ri == ci).astype(BF16)
        alphas = _group_softmax([_load_streams(r, dil, tm) for r, dil in zip((l0_ref, l1_ref, l2_ref), DILATIONS)])
        prod = jnp.zeros((tm, GROUP_W), F32)
        for i, (o_ref, do_ref, dil) in enumerate(zip((o0_ref, o1_ref, o2_ref), (do0_ref, do1_ref, do2_ref), DILATIONS)):
            dov = alphas[i] * dyb[:, i * GROUP_W:(i + 1) * GROUP_W]
            _store_streams(do_ref, dil, tm, dov.astype(BF16))
            prod = prod + dov * _load_streams(o_ref, dil, tm).astype(F32)
        hi = prod.astype(BF16)
        lo = (prod - hi.astype(F32)).astype(BF16)
        dtot = _dot(hi, same_head) + _dot(lo, same_head)
        for alpha, dl_ref, dil in zip(alphas, (dl0_ref, dl1_ref, dl2_ref), DILATIONS):
            _store_streams(dl_ref, dil, tm, alpha * dtot)

    streams = [_stream_spec(dil, tm, GROUP_W) for dil in DILATIONS]
    res = _call(
        body, name="mixer_bwd", grid=(t // tm,),
        in_specs=[_rows(tm, d), _rows(tm, 2 * d), _rows(tm, d), _rows(tm, d)] + streams * 2
        + [_resident((2, d)), _resident((d, d)), _resident((d, cw)), _resident((d, ATTN_W))],
        out_specs=[_rows(tm, 2 * d), _rows(tm, d), _rows(tm, d), _rows(tm, cw)] + streams * 2 + [_acc_spec(d)],
        out_shape=[jax.ShapeDtypeStruct((t, 2 * d), BF16), jax.ShapeDtypeStruct((t, d), BF16),
                   jax.ShapeDtypeStruct((t, d), BF16), jax.ShapeDtypeStruct((t, cw), BF16)]
        + [jax.ShapeDtypeStruct((dil, t // dil, GROUP_W), BF16) for dil in DILATIONS]
        + [jax.ShapeDtypeStruct((dil, t // dil, GROUP_W), F32) for dil in DILATIONS]
        + [jax.ShapeDtypeStruct((SUBLANES, d), F32)],
        args=(dx1, gates, yap, ybp, *[_stream_view(a, dil) for a, dil in zip(os, DILATIONS)],
              *[_stream_view(a, dil) for a, dil in zip(lses, DILATIONS)], b_gate, w_o, w_pa, w_pb),
        semantics=("arbitrary",))
    dgates, dyap, dybp, dya = res[:4]
    dos = [a.reshape(t, GROUP_W) for a in res[4:7]]
    dls = [a.reshape(t, GROUP_W) for a in res[7:10]]
    return dgates, dyap, dybp, dya, dos, dls, res[10]


def _attn_bwd(s, do, lse, dl, dil, carry=None):
    t = s.shape[0] * s.shape[1]
    nb = t // QBLK
    per_stream = nb // dil

    def body(q_ref, qn_ref, kc_ref, kp_ref, vc_ref, vp_ref, do_ref, don_ref, lse_ref, lsen_ref, dl_ref, dln_ref,
             ds_ref):
        b = pl.program_id(0)
        lane, heads = _head_masks()
        first_has_prev = lax.rem(2 * b, per_stream) != 0
        last_has_next = lax.rem(2 * b + 2, per_stream) != 0

        def cols(v):
            return jnp.concatenate([jnp.sum(jnp.where(lane == h * HEAD_DIM, v, 0.0), axis=1, keepdims=True)
                                    for h in range(HEADS_PER_GROUP)], axis=0)

        def pair(qs, dos, k, v, valid, lse_c, dl_c):
            s = jnp.where(valid, _dot_nt(qs, k) * ATTN_SCALE, NEG_INF)
            p = jnp.exp(s - lse_c)
            ds = p * (_dot_nt(dos, v) - dl_c)
            return p.astype(BF16), ds.astype(BF16)

        lo, hi = slice(0, QBLK), slice(QBLK, 2 * QBLK)
        for j, rows in enumerate((lo, hi)):
            q, do, lse, dl = q_ref[rows, :], do_ref[rows, :], lse_ref[rows, :], dl_ref[rows, :]
            kc, vc = kc_ref[rows, :], vc_ref[rows, :]
            if j == 0:
                k2 = jnp.concatenate([kp_ref[...], kc], axis=0)
                v2 = jnp.concatenate([vp_ref[...], vc], axis=0)
                qn, don, lsen, dln = q_ref[hi, :], do_ref[hi, :], lse_ref[hi, :], dl_ref[hi, :]
                mask, mask_n = _band_mask(first_has_prev), _next_mask(True)
            else:
                k2, v2 = kc_ref[...], vc_ref[...]
                qn, don, lsen, dln = qn_ref[...], don_ref[...], lsen_ref[...], dln_ref[...]
                mask, mask_n = _band_mask(True), _next_mask(last_has_next)
            qs, qns = _stack_heads(q, heads), _stack_heads(qn, heads)
            dos, dons = _stack_heads(do, heads), _stack_heads(don, heads)
            p_q, ds_q = pair(qs, dos, k2, v2, mask, cols(lse), cols(dl))
            p_n, ds_n = pair(qns, dons, kc, vc, mask_n, cols(lsen), cols(dln))
            dq = _merge_heads(_dot(ds_q, k2), heads)
            dk = _dot_tn(jnp.concatenate([ds_q[:, QBLK:], ds_n], axis=0), jnp.concatenate([qs, qns], axis=0))
            dv = _dot_tn(jnp.concatenate([p_q[:, QBLK:], p_n], axis=0), jnp.concatenate([dos, dons], axis=0))
            ds_ref[rows, 0:GROUP_W] = (dq * ATTN_SCALE).astype(BF16)
            ds_ref[rows, GROUP_W:2 * GROUP_W] = (dk * ATTN_SCALE).astype(BF16)
            ds_ref[rows, 2 * GROUP_W:3 * GROUP_W] = dv.astype(BF16)

    sv = s.reshape(t, 3 * GROUP_W)
    cur, nxt = _pair_block(0), _edge_block(0, 2, nb)
    return _call(
        body, name=f"attn_bwd_d{dil}", grid=(nb // 2,),
        in_specs=[cur, nxt, _pair_block(1), _edge_block(1, -1, nb), _pair_block(2), _edge_block(2, -1, nb),
                  cur, nxt, cur, nxt, cur, nxt],
        out_specs=[pl.BlockSpec((2 * QBLK, 3 * GROUP_W), lambda b: (b, 0))],
        out_shape=[jax.ShapeDtypeStruct((t, 3 * GROUP_W), BF16)],
        args=(sv, sv, sv, sv, sv, sv, do, do, lse, lse, dl, dl), semantics=("parallel",), carry=carry)


def _conv_mixer_bwd(abcv, dya, conv_w, conv_b):
    t = abcv.shape[0]
    cw = conv_w.shape[1]
    tm = min(256, t)
    last = t // tm - 1

    def body(a_ref, ap_ref, an_ref, dya_ref, dyan_ref, cw_ref, cb_ref, d_ref, acc_ref):
        m = pl.program_id(0)

        @pl.when(m == 0)
        def _():
            acc_ref[...] = jnp.zeros_like(acc_ref)

        ab = a_ref[:, 0:cw].astype(F32)
        ac = a_ref[:, cw:2 * cw].astype(F32)
        av = a_ref[:, 2 * cw:3 * cw].astype(F32)
        u = ac * av
        hu = ap_ref[:, cw:2 * cw].astype(F32) * ap_ref[:, 2 * cw:3 * cw].astype(F32)
        hu = jnp.where(m > 0, hu, 0.0)
        u1 = _shift_down(u, hu, 1)
        u2 = _shift_down(u, hu, 2)
        cv = cw_ref[0:1, :] * u2 + cw_ref[1:2, :] * u1 + cw_ref[2:3, :] * u + cb_ref[...]
        dya_v = dya_ref[...].astype(F32)
        dcv = dya_v * ab
        ndcv = jnp.where(m < last, dyan_ref[...].astype(F32) * an_ref[:, 0:cw].astype(F32), 0.0)
        du = (cw_ref[2:3, :] * dcv + cw_ref[1:2, :] * _shift_up(dcv, ndcv, 1)
              + cw_ref[0:1, :] * _shift_up(dcv, ndcv, 2))
        d_ref[:, 0:cw] = (dya_v * cv).astype(BF16)
        d_ref[:, cw:2 * cw] = (du * av).astype(BF16)
        d_ref[:, 2 * cw:3 * cw] = (du * ac).astype(BF16)
        acc_ref[...] += _stack_rows([_colsum(dcv * u2), _colsum(dcv * u1), _colsum(dcv * u), _colsum(dcv)], cw)

    return pl.pallas_call(
        body, name="conv_mixer_bwd", grid=(t // tm,),
        in_specs=[_rows(tm, 3 * cw), _prev_halo(tm, 3 * cw), _next_halo(tm, 3 * cw, t), _rows(tm, cw),
                  _next_halo(tm, cw, t), _resident((3, cw)), _resident((1, cw))],
        out_specs=[_rows(tm, 3 * cw), _acc_spec(cw)],
        out_shape=[jax.ShapeDtypeStruct((t, 3 * cw), BF16), jax.ShapeDtypeStruct((SUBLANES, cw), F32)],
        compiler_params=_params("arbitrary"),
    )(abcv, abcv, abcv, dya, dya, conv_w, conv_b)


def _in_proj_bwd(x, dx1, dabcv, dss, dgates, w_in, g1, carry=None):
    t, d = x.shape
    qkv0 = dabcv.shape[1]
    n = w_in.shape[0]
    tm = min(ROWS_MATMUL, t)

    def body(x_ref, dx1_ref, da_ref, ds0_ref, ds1_ref, ds2_ref, dg_ref, w_ref, g_ref, dx_ref, acc_ref):
        m = pl.program_id(0)

        @pl.when(m == 0)
        def _():
            acc_ref[...] = jnp.zeros_like(acc_ref)

        dh = _dot(da_ref[...], w_ref[0:qkv0, :]) + _dot(dg_ref[...], w_ref[qkv0 + 3 * ATTN_W:n, :])
        for gi, (ds_ref, dil) in enumerate(zip((ds0_ref, ds1_ref, ds2_ref), DILATIONS)):
            ds = _load_streams(ds_ref, dil, tm)
            for j in range(3):
                c0 = qkv0 + j * ATTN_W + gi * GROUP_W
                dh = dh + _dot(ds[:, j * GROUP_W:(j + 1) * GROUP_W], w_ref[c0:c0 + GROUP_W, :])
        dx, dg = _rms_bwd(x_ref[...], g_ref[...], dh)
        dx_ref[...] = dx1_ref[...] + dx
        acc_ref[...] += _stack_rows([_colsum(dg)], d)

    return _call(
        body, name="in_proj_bwd", grid=(t // tm,),
        in_specs=[_rows(tm, d), _rows(tm, d), _rows(tm, qkv0)]
        + [_stream_spec(dil, tm, 3 * GROUP_W) for dil in DILATIONS]
        + [_rows(tm, 2 * d), _resident((n, d)), _resident((1, d))],
        out_specs=[_rows(tm, d), _acc_spec(d)],
        out_shape=[jax.ShapeDtypeStruct((t, d), F32), jax.ShapeDtypeStruct((SUBLANES, d), F32)],
        args=(x, dx1, dabcv, *[_stream_view(a, dil) for a, dil in zip(dss, DILATIONS)], dgates, w_in, g1),
        semantics=("arbitrary",), carry=carry)


def _dw_in_qkv(ds, h, dil):
    t, d = h.shape
    tk = min(1024, t)
    sub = min(256, t)
    width = 3 * GROUP_W

    def body(ds_ref, h_ref, o_ref, acc_ref):
        k = pl.program_id(0)

        @pl.when(k == 0)
        def _():
            acc_ref[...] = jnp.zeros_like(acc_ref)

        upd = None
        for i in range(tk // sub):
            rows = ds_ref[:, i * (sub // dil):(i + 1) * (sub // dil), :].reshape(sub, width)
            if dil > 1:
                rows = _permute_rows(_perm(dil, sub, inverse=True), rows)
            term = _dot_tn(rows, h_ref[i * sub:(i + 1) * sub, :])
            upd = term if upd is None else upd + term
        acc_ref[...] += upd

        @pl.when(k == t // tk - 1)
        def _():
            o_ref[...] = acc_ref[...].astype(BF16)

    return pl.pallas_call(
        body, name=f"dw_in_qkv_d{dil}", grid=(t // tk,),
        in_specs=[_stream_spec(dil, tk, width), _rows(tk, d)],
        out_specs=pl.BlockSpec((width, d), lambda k: (0, 0)),
        out_shape=jax.ShapeDtypeStruct((width, d), BF16),
        scratch_shapes=[pltpu.VMEM((width, d), F32)],
        compiler_params=_params("arbitrary"),
    )(_stream_view(ds, dil), h)


def _local_step(x, target, p, late):
    cw = p["conv_a_w"].shape[1]
    (h, abcv, gates, *ss), (g_up,) = _in_proj(x, p["norm_mix_g"], p["w_in"], cw,
                                              carry=_Exchange("gather", [late["w_up"]]))
    w_up = _full_from_gathered(g_up)
    (o0, lse0), g_proj = _attn_fwd(ss[0], DILATIONS[0],
                                   carry=_Exchange("gather", [late["w_proj_a"], late["w_proj_b"]]))
    (o1, lse1), (g_out,) = _attn_fwd(ss[1], DILATIONS[1], carry=_Exchange("gather", [late["w_out"]]))
    o2, lse2 = _attn_fwd(ss[2], DILATIONS[2])
    w_pa, w_pb, w_out = [_full_from_gathered(g) for g in (*g_proj, g_out)]
    os, lses = (o0, o1, o2), (lse0, lse1, lse2)
    (x1, ya, yb, yap, ybp, merged), (g_down,) = _mixer_out(
        x, abcv, gates, os, lses, p["conv_a_w"], p["conv_a_b"], p["b_gate"], w_pa, w_pb, w_out,
        carry=_Exchange("gather", [late["w_down"]]))
    w_down = _full_from_gathered(g_down)
    h2, up_pre, act, conv, dx2, acc_gf, loss = _ffn_fwd(x1, target, p["norm_ffn_g"], w_up, p["ffn_conv_w"],
                                                        p["ffn_conv_b"], w_down, p["final_norm_g"])

    parts, got = {}, {}
    dup, acc_fb = _ffn_act_bwd(dx2, conv, w_down)
    parts["w_down"] = _by_destination(_tn_matmul(act, dx2, "dw_down"))
    (dpre, dx1, acc_g2, acc_fw), (got["w_down"],) = _ffn_up_bwd(dup, up_pre, x1, dx2, p["ffn_conv_w"], w_up,
                                                                p["norm_ffn_g"],
                                                                carry=_Exchange("scatter", [parts["w_down"]]))
    parts["w_up"] = _by_destination(_tn_matmul(dpre, h2, "dw_up"))
    dgates, dyap, dybp, dya, dos, dls, acc_bg = _mixer_bwd(dx1, gates, yap, ybp, os, lses, p["b_gate"], w_out,
                                                           w_pa, w_pb)
    parts["w_out"] = _by_destination(_tn_matmul(merged, dx1, "dw_out"))
    parts["w_proj_a"] = _by_destination(_tn_matmul(dyap, ya, "dw_proj_a"))
    parts["w_proj_b"] = _by_destination(_tn_matmul(dybp, yb, "dw_proj_b"))
    minor = ("w_out", "w_proj_a", "w_proj_b")
    half = parts["w_up"].shape[1] // 2
    (ds0,), received = _attn_bwd(ss[0], dos[0], lses[0], dls[0], DILATIONS[0],
                                 carry=_Exchange("scatter", [parts[n] for n in minor]))
    got.update(zip(minor, received))
    (ds1,), first_half = _attn_bwd(ss[1], dos[1], lses[1], dls[1], DILATIONS[1],
                                   carry=_Exchange("scatter", [parts["w_up"]], rows=(0, half)))
    (ds2,), (got["w_up"],) = _attn_bwd(ss[2], dos[2], lses[2], dls[2], DILATIONS[2],
                                       carry=_Exchange("scatter", [parts["w_up"]], rows=(half, half),
                                                       into=first_half))
    dss = [ds0, ds1, ds2]
    dabcv, acc_ca = _conv_mixer_bwd(abcv, dya, p["conv_a_w"], p["conv_a_b"])
    dw_s = [_dw_in_qkv(ds, h, dil) for ds, dil in zip(dss, DILATIONS)]
    dw_qkv = [w[j * GROUP_W:(j + 1) * GROUP_W] for j in range(3) for w in dw_s]
    g_w_in = jnp.concatenate([_tn_matmul(dabcv, h, "dw_in_a"), *dw_qkv, _tn_matmul(dgates, h, "dw_in_g")], axis=0)
    parts["w_in"] = _by_destination(g_w_in)
    (dx, acc_g1), (got["w_in"],) = _in_proj_bwd(x, dx1, dabcv, dss, dgates, p["w_in"], p["norm_mix_g"],
                                                carry=_Exchange("scatter", [parts["w_in"]]))
    small = dict(norm_mix_g=acc_g1[0:1], b_gate=acc_bg[0:2], conv_a_w=acc_ca[0:3], conv_a_b=acc_ca[3:4],
                 norm_ffn_g=acc_g2[0:1], ffn_conv_w=acc_fw[0:3], ffn_conv_b=acc_fb[0:1], final_norm_g=acc_gf[0:1])
    return loss[0, 0], dx, parts, got, small


def _all_gather(shards):
    n = len(shards)

    def body(*refs):
        ins, outs = refs[:n], refs[n:2 * n]
        send_sems, recv_sems, local_sems = refs[2 * n:]
        x, y, c = _mesh_pos()
        me, sibling = (x, y, c), (x, y, 1 - c)
        chips = [(1 - x, y), (x, 1 - y), (1 - x, 1 - y)]

        def copy(i, k, block, to, src=None):
            rows = outs[i].at[_dev_index(*block)]
            return pltpu.make_async_remote_copy(
                src_ref=rows if src is None else src, dst_ref=rows, send_sem=send_sems.at[i, k],
                recv_sem=recv_sems.at[i, k], device_id=to, device_id_type=MESH)

        mine, first, passed = [], [], []
        for i in range(n):
            cp = pltpu.make_async_copy(ins[i], outs[i].at[_dev_index(*me)], local_sems.at[i])
            cp.start()
            mine.append(cp)
            first.append(copy(i, 0, me, sibling, src=ins[i]))
            first += [copy(i, 1 + j, me, (*chip, c), src=ins[i]) for j, chip in enumerate(chips)]
        for cp in first:
            cp.start()
        for i in range(n):
            for j, chip in enumerate(chips):
                copy(i, 1 + j, (*chip, c), me).wait_recv()
                fw = copy(i, 4 + j, (*chip, c), sibling)
                fw.start()
                passed.append(fw)
        for i in range(n):
            copy(i, 0, sibling, me).wait_recv()
            for j, chip in enumerate(chips):
                copy(i, 4 + j, (*chip, 1 - c), me).wait_recv()
        for cp in first + passed:
            cp.wait_send()
        for cp in mine:
            cp.wait()

    return pl.pallas_call(
        body, name="all_gather_weights",
        in_specs=[ANY] * n, out_specs=[ANY] * n,
        out_shape=[jax.ShapeDtypeStruct((N_DEV,) + s.shape, s.dtype) for s in shards],
        scratch_shapes=[pltpu.SemaphoreType.DMA((n, 7)), pltpu.SemaphoreType.DMA((n, 7)),
                        pltpu.SemaphoreType.DMA((n,))],
    )(*shards)


def _all_reduce_small(v):
    r = v.shape[0]

    def body(v_ref, o_ref, gath, send_sems, recv_sems):
        x, y, c = _mesh_pos()
        me = _dev_index(x, y, c)
        gath[me] = v_ref[...]
        flips = [(kx, ky, kc) for kx in (0, 1) for ky in (0, 1) for kc in (0, 1)][1:]
        copies = []
        for k, (kx, ky, kc) in enumerate(flips):
            px = 1 - x if kx else x
            py = 1 - y if ky else y
            pc = 1 - c if kc else c
            cp = pltpu.make_async_remote_copy(
                src_ref=v_ref, dst_ref=gath.at[me], send_sem=send_sems.at[k], recv_sem=recv_sems.at[k],
                device_id=(px, py, pc), device_id_type=MESH)
            cp.start()
            copies.append((cp, _dev_index(px, py, pc)))
        for k, (cp, peer) in enumerate(copies):
            pltpu.make_async_remote_copy(
                src_ref=v_ref, dst_ref=gath.at[peer], send_sem=send_sems.at[k], recv_sem=recv_sems.at[k],
                device_id=(x, y, c), device_id_type=MESH).wait_recv()
        for cp, _ in copies:
            cp.wait_send()
        total = gath[0]
        for j in range(1, N_DEV):
            total = total + gath[j]
        o_ref[...] = total

    return pl.pallas_call(
        body, name="all_reduce_small",
        in_specs=[pl.BlockSpec(memory_space=pltpu.VMEM)], out_specs=pl.BlockSpec(memory_space=pltpu.VMEM),
        out_shape=jax.ShapeDtypeStruct((r, LANES), F32),
        scratch_shapes=[pltpu.VMEM((N_DEV, r, LANES), F32), pltpu.SemaphoreType.DMA((7,)),
                        pltpu.SemaphoreType.DMA((7,))],
    )(v)


def _adamw_math(w, g, m, v):
    m2 = ADAM_B1 * m + (1.0 - ADAM_B1) * g
    v2 = ADAM_B2 * v + (1.0 - ADAM_B2) * (g * g)
    m_hat = m2 / (1.0 - ADAM_B1 ** ADAM_STEP)
    v_hat = v2 / (1.0 - ADAM_B2 ** ADAM_STEP)
    delta = -ADAM_LR * (m_hat / (jnp.sqrt(v_hat) + ADAM_EPS) + ADAM_WD * w)
    return delta, m2, v2


def _adamw_big(w, m, v, part, got, me):
    r, c = w.shape
    tr = max(t for t in range(HALO, min(r, 512) + 1, HALO) if r % t == 0)

    def body(me_ref, w_ref, m_ref, v_ref, own_ref, *rest):
        del me_ref
        got_refs, (g_out, d_out, m_out, v_out) = rest[:N_DEV - 1], rest[N_DEV - 1:]
        g = own_ref[...].astype(F32)
        for ref in got_refs:
            g = g + ref[...].astype(F32)
        delta, m2, v2 = _adamw_math(w_ref[...], g, m_ref[...], v_ref[...])
        g_out[...] = g
        d_out[...] = delta
        m_out[...] = m2
        v_out[...] = v2

    def peer_block(k):
        return pl.BlockSpec((None, tr, c), lambda i, me_ref: (jnp.bitwise_xor(me_ref[0], k), i, 0))

    plain = pl.BlockSpec((tr, c), lambda i, me_ref: (i, 0))
    out = jax.ShapeDtypeStruct((r, c), F32)
    return pl.pallas_call(
        body, name="adamw_big",
        grid_spec=pltpu.PrefetchScalarGridSpec(
            num_scalar_prefetch=1, grid=(r // tr,),
            in_specs=[plain, plain, plain] + [peer_block(k) for k in range(N_DEV)],
            out_specs=[plain] * 4),
        out_shape=[out] * 4,
        compiler_params=_params("parallel"),
    )(me, w, m, v, part, *([got] * (N_DEV - 1)))


def _adamw_small(w, g, m, v):
    def body(w_ref, g_ref, m_ref, v_ref, d_out, m_out, v_out):
        delta, m2, v2 = _adamw_math(w_ref[...], g_ref[...], m_ref[...], v_ref[...])
        d_out[...] = delta
        m_out[...] = m2
        v_out[...] = v2

    out = jax.ShapeDtypeStruct(w.shape, F32)
    return pl.pallas_call(body, name="adamw_small", out_shape=[out] * 3)(w, g, m, v)


BIG = ("w_in", "w_proj_a", "w_proj_b", "w_out", "w_up", "w_down")
LATE = ("w_proj_a", "w_proj_b", "w_out", "w_up", "w_down")
COLUMN_SHARDED = ("w_in", "w_proj_a", "w_proj_b", "w_up")
SMALL = ("norm_mix_g", "b_gate", "conv_a_w", "conv_a_b", "norm_ffn_g", "ffn_conv_w", "ffn_conv_b", "final_norm_g")
SMALL_SHARDED = ("b_gate", "conv_a_w", "ffn_conv_w")
WEIGHTS = ("norm_mix_g", "w_in", "b_gate", "conv_a_w", "conv_a_b", "w_proj_a", "w_proj_b", "w_out", "norm_ffn_g",
           "w_up", "ffn_conv_w", "ffn_conv_b", "w_down", "final_norm_g")


def _pack(vectors, rows):
    flat = jnp.concatenate([v.reshape(-1) for v in vectors])
    return jnp.pad(flat, (0, rows * LANES - flat.shape[0])).reshape(rows, LANES)


def _packed_rows(count):
    rows = -(-count // LANES)
    return -(-rows // SUBLANES) * SUBLANES


def _unpack(packed, shapes):
    flat = packed.reshape(-1)
    out, lo = [], 0
    for s in shapes:
        size = 1
        for dim in s:
            size *= dim
        out.append(flat[lo:lo + size].reshape(s))
        lo += size
    return out


def _full_from_gathered(gathered):
    _, r, c = gathered.shape
    return gathered.reshape(N_DEV * r, c)


def _by_destination(grad):
    rr, cc = grad.shape
    return grad.reshape(N_DEV, rr // N_DEV, cc)


def _block2d(name, a):
    a = a.reshape(a.shape[-2:])
    return a.T if name in COLUMN_SHARDED else a


def kernel(x, norm_mix_g, w_in, b_gate, conv_a_w, conv_a_b, w_proj_a, w_proj_b, w_out, norm_ffn_g, w_up, ffn_conv_w, ffn_conv_b, w_down, final_norm_g, loss_target, m_norm_mix_g, m_w_in, m_b_gate, m_conv_a_w, m_conv_a_b, m_w_proj_a, m_w_proj_b, m_w_out, m_norm_ffn_g, m_w_up, m_ffn_conv_w, m_ffn_conv_b, m_w_down, m_final_norm_g, v_norm_mix_g, v_w_in, v_b_gate, v_conv_a_w, v_conv_a_b, v_w_proj_a, v_w_proj_b, v_w_out, v_norm_ffn_g, v_w_up, v_ffn_conv_w, v_ffn_conv_b, v_w_down, v_final_norm_g):
    given = dict(locals())
    shard = {n: given[n] for n in WEIGHTS}
    mom_m = {n: given["m_" + n] for n in WEIGHTS}
    mom_v = {n: given["v_" + n] for n in WEIGHTS}
    xi, yi, ci = _mesh_pos()
    me = _dev_index(xi, yi, ci)
    me1 = me.astype(jnp.int32).reshape(1)

    big2d = {n: _block2d(n, shard[n]) for n in BIG}
    small_shapes = [shard[n].shape[1:] for n in SMALL_SHARDED]
    n_small = sum(s[0] * s[1] for s in small_shapes)
    packed_small = _pack([shard[n] for n in SMALL_SHARDED], _packed_rows(n_small))
    gathered = _all_gather([big2d["w_in"].astype(BF16), packed_small])
    p = {"w_in": _full_from_gathered(gathered[0])}
    per_dev = [_unpack(gathered[-1][j], small_shapes) for j in range(N_DEV)]
    for i, n in enumerate(SMALL_SHARDED):
        p[n] = jnp.concatenate([per_dev[j][i] for j in range(N_DEV)], axis=1)
    p["norm_mix_g"], p["norm_ffn_g"] = shard["norm_mix_g"], shard["norm_ffn_g"]
    p["conv_a_b"], p["ffn_conv_b"] = shard["conv_a_b"], shard["ffn_conv_b"]
    p["final_norm_g"] = shard["final_norm_g"].reshape(1, -1)
    late = {n: big2d[n].astype(BF16) for n in LATE}

    loss_part, dx, parts, got, g_small = _local_step(x[0], loss_target[0], p, late)

    results = {}
    for n in BIG:
        outs = _adamw_big(big2d[n], _block2d(n, mom_m[n]), _block2d(n, mom_v[n]), parts[n], got[n], me1)
        results[n] = [_block2d(n, o).reshape(shard[n].shape) for o in outs]

    small_full_shapes = [g_small[n].shape for n in SMALL]
    n_vec = sum(s[0] * s[1] for s in small_full_shapes) + 1
    packed = _pack([g_small[n] for n in SMALL] + [loss_part.reshape(1)], _packed_rows(n_vec))
    reduced = _all_reduce_small(packed)
    *g_full, loss_vec = _unpack(reduced, small_full_shapes + [(1,)])
    loss = loss_vec[0]
    own_g = []
    for n, g in zip(SMALL, g_full):
        if n in SMALL_SHARDED:
            width = shard[n].shape[-1]
            g = lax.dynamic_slice_in_dim(g, me * width, width, axis=1)
        own_g.append(g.reshape(shard[n].shape))
    own_shapes = [shard[n].shape for n in SMALL]
    rows = _packed_rows(sum(g.size for g in own_g))
    small_out = _adamw_small(_pack([shard[n] for n in SMALL], rows), _pack(own_g, rows),
                             _pack([mom_m[n] for n in SMALL], rows), _pack([mom_v[n] for n in SMALL], rows))
    deltas, new_ms, new_vs = (_unpack(o, own_shapes) for o in small_out)
    for i, n in enumerate(SMALL):
        results[n] = [own_g[i], deltas[i], new_ms[i], new_vs[i]]

    grad_x = dx.reshape(x.shape)
    return (loss, grad_x, *[results[n][0] for n in WEIGHTS], *[results[n][1] for n in WEIGHTS],
            *[results[n][2] for n in WEIGHTS], *[results[n][3] for n in WEIGHTS])
```

```python
import functools

import jax
import jax.numpy as jnp
from jax import lax
from jax.experimental import pallas as pl
from jax.experimental.pallas import tpu as pltpu

F32 = jnp.float32
BF16 = jnp.bfloat16
MESH = pl.DeviceIdType.MESH

N_DEV = 8
RMS_EPS = 1e-6
NEG_INF = -1e30
N_GROUPS = 3
DILATIONS = (1, 4, 16)
HEADS_PER_GROUP = 4
HEAD_DIM = 64
GROUP_W = HEADS_PER_GROUP * HEAD_DIM
ATTN_W = N_GROUPS * GROUP_W
QBLK = 128
ATTN_SCALE = HEAD_DIM ** -0.5

ADAM_LR = 0.001
ADAM_B1 = 0.9
ADAM_B2 = 0.999
ADAM_EPS = 1e-08
ADAM_WD = 0.01
ADAM_STEP = 10

PERM_TOKENS = 256
ROWS_MATMUL = 512
HALO = 16
LANES = 128
SUBLANES = 8
VMEM_LIMIT_BYTES = 56 * 1024 * 1024


def _params(*sem):
    return pltpu.CompilerParams(dimension_semantics=sem, vmem_limit_bytes=VMEM_LIMIT_BYTES)


def _pick_tile(n, cap):
    if n <= cap:
        return n
    best = None
    for t in range(LANES, cap + 1, LANES):
        if n % t == 0:
            best = t
    assert best is not None, (n, cap)
    return best


def _rows(tm, c, j=0):
    return pl.BlockSpec((tm, c), lambda m: (m, j))


def _prev_halo(tm, c):
    return pl.BlockSpec((HALO, c), lambda m: (jnp.maximum(m * (tm // HALO) - 1, 0), 0))


def _next_halo(tm, c, t_total):
    last = t_total // HALO - 1
    return pl.BlockSpec((HALO, c), lambda m: (jnp.minimum((m + 1) * (tm // HALO), last), 0))


def _resident(shape):
    nd = len(shape)
    return pl.BlockSpec(shape, lambda *_: (0,) * nd, pipeline_mode=pl.Buffered(1))


def _acc_spec(c):
    return pl.BlockSpec((SUBLANES, c), lambda *_: (0, 0))


def _shift_down(u, halo, k):
    edge = jnp.concatenate([halo[HALO - SUBLANES:], u[:SUBLANES]], axis=0)
    head = pltpu.roll(edge, k, 0)[SUBLANES:]
    return jnp.concatenate([head, pltpu.roll(u, k, 0)[SUBLANES:]], axis=0)


def _shift_up(u, halo, k):
    n = u.shape[0]
    edge = jnp.concatenate([u[n - SUBLANES:], halo[:SUBLANES]], axis=0)
    tail = pltpu.roll(edge, 2 * SUBLANES - k, 0)[:SUBLANES]
    return jnp.concatenate([pltpu.roll(u, n - k, 0)[:n - SUBLANES], tail], axis=0)


def _stack_rows(rows, c):
    idx = lax.broadcasted_iota(jnp.int32, (SUBLANES, c), 0)
    out = jnp.zeros((SUBLANES, c), F32)
    for i, r in enumerate(rows):
        out = out + jnp.where(idx == i, r, 0.0)
    return out


def _colsum(v):
    return jnp.sum(v, axis=0, keepdims=True)


def _sigmoid(v):
    return 0.5 * jnp.tanh(0.5 * v) + 0.5


def _rms_fwd(xv, g):
    r = lax.rsqrt(jnp.mean(xv * xv, axis=-1, keepdims=True) + RMS_EPS)
    return xv * r * g, r


def _rms_bwd(xv, g, dy):
    r = lax.rsqrt(jnp.mean(xv * xv, axis=-1, keepdims=True) + RMS_EPS)
    xn = xv * r
    dxn = dy * g
    dx = r * (dxn - xn * jnp.mean(dxn * xn, axis=-1, keepdims=True))
    return dx, dy * xn


def _dot(a, b):
    return jnp.dot(a, b, preferred_element_type=F32)


def _dot_nt(a, b):
    return lax.dot_general(a, b, (((1,), (1,)), ((), ())), preferred_element_type=F32)


def _dot_tn(a, b):
    return lax.dot_general(a, b, (((0,), (0,)), ((), ())), preferred_element_type=F32)


def _perm(dil, n, inverse=False):
    i = lax.broadcasted_iota(jnp.int32, (n, n), 0)
    j = lax.broadcasted_iota(jnp.int32, (n, n), 1)
    if inverse:
        i, j = j, i
    per = n // dil
    return (j == (i % per) * dil + i // per).astype(BF16)


def _permute_rows(pm, v):
    if v.dtype == BF16:
        return _dot(pm, v).astype(BF16)
    hi = v.astype(BF16)
    lo = (v - hi.astype(F32)).astype(BF16)
    return _dot(pm, hi) + _dot(pm, lo)


def _stream_view(a, dil):
    t, c = a.shape
    return a.reshape(dil, t // dil, c)


def _stream_spec(dil, tm, c):
    return pl.BlockSpec((dil, tm // dil, c), lambda m: (0, m, 0))


def _load_streams(ref, dil, tm):
    c = ref.shape[-1]
    if dil == 1:
        return ref[...].reshape(tm, c)
    sub = min(PERM_TOKENS, tm)
    pm = _perm(dil, sub, inverse=True)
    parts = [_permute_rows(pm, ref[:, i * (sub // dil):(i + 1) * (sub // dil), :].reshape(sub, c))
             for i in range(tm // sub)]
    return parts[0] if len(parts) == 1 else jnp.concatenate(parts, axis=0)


def _store_streams(ref, dil, tm, v):
    if dil == 1:
        ref[...] = v.reshape(ref.shape).astype(ref.dtype)
        return
    sub = min(PERM_TOKENS, tm)
    pm = _perm(dil, sub)
    for i in range(tm // sub):
        piece = _permute_rows(pm, v[i * sub:(i + 1) * sub])
        ref[:, i * (sub // dil):(i + 1) * (sub // dil), :] = piece.reshape(dil, sub // dil, -1).astype(ref.dtype)


ANY = pl.BlockSpec(memory_space=pl.ANY)


def _mesh_pos():
    return lax.axis_index("x"), lax.axis_index("y"), lax.axis_index("c")


def _dev_index(px, py, pc):
    return 4 * px + 2 * py + pc


class _Exchange:
    def __init__(self, mode, arrays, rows=None, into=()):
        self.mode, self.arrays, self.rows, self.into = mode, list(arrays), rows, list(into)
        n = len(self.arrays)
        if mode == "gather":
            self.out_shape = [jax.ShapeDtypeStruct((N_DEV,) + a.shape, a.dtype) for a in self.arrays]
        else:
            self.out_shape = [jax.ShapeDtypeStruct(a.shape, a.dtype) for a in self.arrays]
        self.scratch = [pltpu.SemaphoreType.DMA((n, N_DEV - 1)), pltpu.SemaphoreType.DMA((n, N_DEV - 1)),
                        pltpu.SemaphoreType.DMA((n,))]

    def _peers(self):
        x, y, c = _mesh_pos()
        flips = [(kx, ky, kc) for kx in (0, 1) for ky in (0, 1) for kc in (0, 1)][1:]
        peers = [(1 - x if kx else x, 1 - y if ky else y, 1 - c if kc else c) for kx, ky, kc in flips]
        return _dev_index(x, y, c), peers

    def _copy(self, ins, outs, sems, i, k, peer, me, sending):
        src = ins[i] if self.mode == "gather" else ins[i].at[_dev_index(*peer)]
        dst = outs[i].at[me if sending else _dev_index(*peer)]
        if self.rows is not None:
            src, dst = src.at[pl.ds(*self.rows)], dst.at[pl.ds(*self.rows)]
        return pltpu.make_async_remote_copy(src_ref=src, dst_ref=dst, send_sem=sems[0].at[i, k],
                                            recv_sem=sems[1].at[i, k], device_id=peer, device_id_type=MESH)

    def _own(self, ins, outs, sems, i, me):
        return pltpu.make_async_copy(ins[i], outs[i].at[me], sems[2].at[i])

    def start(self, ins, outs, sems):
        me, peers = self._peers()
        for i in range(len(ins)):
            if self.mode == "gather":
                self._own(ins, outs, sems, i, me).start()
            for k, peer in enumerate(peers):
                self._copy(ins, outs, sems, i, k, peer, me, True).start()

    def wait(self, ins, outs, sems):
        me, peers = self._peers()
        for i in range(len(ins)):
            for k, peer in enumerate(peers):
                self._copy(ins, outs, sems, i, k, peer, me, False).wait_recv()
            for k, peer in enumerate(peers):
                self._copy(ins, outs, sems, i, k, peer, me, True).wait_send()
            if self.mode == "gather":
                self._own(ins, outs, sems, i, me).wait()


def _call(body, *, name, grid, in_specs, out_specs, out_shape, args, semantics, carry=None, scratch=()):
    if carry is None:
        return pl.pallas_call(body, name=name, grid=grid, in_specs=in_specs, out_specs=out_specs,
                              out_shape=out_shape, scratch_shapes=list(scratch),
                              compiler_params=_params(*semantics))(*args)
    n_in, n_out, n_x, n_s = len(in_specs), len(out_specs), len(carry.arrays), len(scratch)
    n_into = len(carry.into)
    all_in = n_in + n_x + n_into

    def carried(*refs):
        ins, x_ins = refs[:n_in], refs[n_in:n_in + n_x]
        outs = refs[all_in:all_in + n_out]
        x_outs = refs[all_in + n_out:all_in + n_out + n_x]
        own = refs[all_in + n_out + n_x:all_in + n_out + n_x + n_s]
        sems = refs[all_in + n_out + n_x + n_s:]
        first = functools.reduce(jnp.logical_and, [pl.program_id(a) == 0 for a in range(len(grid))])
        last = functools.reduce(jnp.logical_and, [pl.program_id(a) == grid[a] - 1 for a in range(len(grid))])

        @pl.when(first)
        def _():
            carry.start(x_ins, x_outs, sems)

        body(*ins, *outs, *own)

        @pl.when(last)
        def _():
            carry.wait(x_ins, x_outs, sems)

    res = pl.pallas_call(
        carried, name=name, grid=grid, in_specs=list(in_specs) + [ANY] * (n_x + n_into),
        out_specs=list(out_specs) + [ANY] * n_x, out_shape=list(out_shape) + carry.out_shape,
        input_output_aliases={n_in + n_x + i: n_out + i for i in range(n_into)},
        scratch_shapes=list(scratch) + carry.scratch, compiler_params=_params(*["arbitrary"] * len(grid)),
    )(*args, *carry.arrays, *carry.into)
    return list(res[:n_out]), list(res[n_out:])


def _in_proj(x, g, wt, cw, carry=None):
    t, d = x.shape
    n = wt.shape[0]
    tm = min(ROWS_MATMUL, t)
    qkv0 = 3 * cw

    def body(x_ref, g_ref, wt_ref, h_ref, abcv_ref, gates_ref, *s_refs):
        h = _rms_fwd(x_ref[...], g_ref[...])[0].astype(BF16)
        h_ref[...] = h
        abcv_ref[...] = _dot_nt(h, wt_ref[0:qkv0, :]).astype(BF16)
        gates_ref[...] = _dot_nt(h, wt_ref[qkv0 + 3 * ATTN_W:n, :]).astype(BF16)
        for gi, s_ref in enumerate(s_refs):
            cols = [_dot_nt(h, wt_ref[qkv0 + j * ATTN_W + gi * GROUP_W:qkv0 + j * ATTN_W + (gi + 1) * GROUP_W, :])
                    for j in range(3)]
            _store_streams(s_ref, DILATIONS[gi], tm, jnp.concatenate(cols, axis=1).astype(BF16))

    return _call(
        body, name="in_proj", grid=(t // tm,),
        in_specs=[_rows(tm, d), _resident((1, d)), _resident((n, d))],
        out_specs=[_rows(tm, d), _rows(tm, qkv0), _rows(tm, 2 * d)]
        + [_stream_spec(dil, tm, 3 * GROUP_W) for dil in DILATIONS],
        out_shape=[jax.ShapeDtypeStruct((t, d), BF16), jax.ShapeDtypeStruct((t, qkv0), BF16),
                   jax.ShapeDtypeStruct((t, 2 * d), BF16)]
        + [jax.ShapeDtypeStruct((dil, t // dil, 3 * GROUP_W), BF16) for dil in DILATIONS],
        args=(x, g, wt), semantics=("parallel",), carry=carry)


def _head_masks():
    lane = lax.broadcasted_iota(jnp.int32, (1, GROUP_W), 1)
    return lane, [(lane // HEAD_DIM) == h for h in range(HEADS_PER_GROUP)]


def _stack_heads(v, heads):
    return jnp.concatenate([jnp.where(hm, v, jnp.zeros_like(v)) for hm in heads], axis=0)


def _merge_heads(v, heads):
    out = jnp.zeros((QBLK, GROUP_W), v.dtype)
    for h, hm in enumerate(heads):
        out = jnp.where(hm, v[h * QBLK:(h + 1) * QBLK], out)
    return out


def _pair_block(col):
    return pl.BlockSpec((2 * QBLK, GROUP_W), lambda b: (b, col))


def _edge_block(col, shift, nb):
    return pl.BlockSpec((QBLK, GROUP_W), lambda b: (jnp.clip(2 * b + shift, 0, nb - 1), col))


def _band_mask(has_prev):
    rows = HEADS_PER_GROUP * QBLK
    row = lax.broadcasted_iota(jnp.int32, (rows, 2 * QBLK), 0) & (QBLK - 1)
    col = lax.broadcasted_iota(jnp.int32, (rows, 2 * QBLK), 1)
    return ((col < QBLK) & (col >= row) & has_prev) | ((col >= QBLK) & (col - QBLK <= row))


def _next_mask(has_next):
    rows = HEADS_PER_GROUP * QBLK
    row = lax.broadcasted_iota(jnp.int32, (rows, QBLK), 0) & (QBLK - 1)
    col = lax.broadcasted_iota(jnp.int32, (rows, QBLK), 1)
    return (col >= row) & has_next


def _attn_fwd(s, dil, carry=None):
    t = s.shape[0] * s.shape[1]
    nb = t // QBLK
    per_stream = nb // dil
    assert per_stream % 2 == 0

    def body(q_ref, kc_ref, kp_ref, vc_ref, vp_ref, o_ref, lse_ref):
        b = pl.program_id(0)
        _, heads = _head_masks()
        first_has_prev = lax.rem(2 * b, per_stream) != 0
        for j in range(2):
            rows = slice(j * QBLK, (j + 1) * QBLK)
            if j == 0:
                k2 = jnp.concatenate([kp_ref[...], kc_ref[rows, :]], axis=0)
                v2 = jnp.concatenate([vp_ref[...], vc_ref[rows, :]], axis=0)
            else:
                k2, v2 = kc_ref[...], vc_ref[...]
            mask = _band_mask(first_has_prev if j == 0 else True)
            sc = jnp.where(mask, _dot_nt(_stack_heads(q_ref[rows, :], heads), k2) * ATTN_SCALE, NEG_INF)
            mx = jnp.max(sc, axis=1, keepdims=True)
            pr = jnp.exp(sc - mx)
            den = jnp.sum(pr, axis=1, keepdims=True)
            o_all = _dot(pr.astype(BF16), v2) / den
            o_ref[rows, :] = _merge_heads(o_all, heads).astype(BF16)
            lse_ref[rows, :] = _merge_heads(jnp.broadcast_to(mx + jnp.log(den), o_all.shape), heads)

    sv = s.reshape(t, 3 * GROUP_W)
    return _call(
        body, name=f"attn_fwd_d{dil}", grid=(nb // 2,),
        in_specs=[_pair_block(0), _pair_block(1), _edge_block(1, -1, nb), _pair_block(2), _edge_block(2, -1, nb)],
        out_specs=[_pair_block(0), _pair_block(0)],
        out_shape=[jax.ShapeDtypeStruct((t, GROUP_W), BF16), jax.ShapeDtypeStruct((t, GROUP_W), F32)],
        args=(sv, sv, sv, sv, sv), semantics=("parallel",), carry=carry)


def _group_softmax(parts):
    mx = jnp.maximum(jnp.maximum(parts[0], parts[1]), parts[2])
    es = [jnp.exp(p - mx) for p in parts]
    den = es[0] + es[1] + es[2]
    return [e / den for e in es]


def _mixer_out(x, abcv, gates, os, lses, conv_w, conv_b, b_gate, w_pa, w_pb, w_o, carry=None):
    t, d = x.shape
    cw = conv_w.shape[1]
    tm = min(256, t)

    def body(x_ref, abcv_ref, halo_ref, gates_ref, o0_ref, o1_ref, o2_ref, l0_ref, l1_ref, l2_ref, cw_ref, cb_ref,
             bg_ref, wpa_ref, wpb_ref, wo_ref, x1_ref, ya_ref, yb_ref, yap_ref, ybp_ref, mg_ref):
        m = pl.program_id(0)
        ab = abcv_ref[:, 0:cw].astype(F32)
        u = abcv_ref[:, cw:2 * cw].astype(F32) * abcv_ref[:, 2 * cw:3 * cw].astype(F32)
        hu = halo_ref[:, cw:2 * cw].astype(F32) * halo_ref[:, 2 * cw:3 * cw].astype(F32)
        hu = jnp.where(m > 0, hu, 0.0)
        cv = (cw_ref[0:1, :] * _shift_down(u, hu, 2) + cw_ref[1:2, :] * _shift_down(u, hu, 1)
              + cw_ref[2:3, :] * u + cb_ref[...])
        ya = (ab * cv).astype(BF16)
        ya_ref[...] = ya
        alphas = _group_softmax([_load_streams(r, dil, tm) for r, dil in zip((l0_ref, l1_ref, l2_ref), DILATIONS)])
        for i, (o_ref, dil) in enumerate(zip((o0_ref, o1_ref, o2_ref), DILATIONS)):
            sl = slice(i * GROUP_W, (i + 1) * GROUP_W)
            yb_ref[:, sl] = (alphas[i] * _load_streams(o_ref, dil, tm).astype(F32)).astype(BF16)
        yap = _dot_nt(ya, wpa_ref[...])
        ybp = _dot_nt(yb_ref[...], wpb_ref[...])
        yap_ref[...] = yap.astype(BF16)
        ybp_ref[...] = ybp.astype(BF16)
        sa = _sigmoid(gates_ref[:, 0:d].astype(F32) + bg_ref[0:1, :])
        sb = _sigmoid(gates_ref[:, d:2 * d].astype(F32) + bg_ref[1:2, :])
        merged = (sa * yap + sb * ybp).astype(BF16)
        mg_ref[...] = merged
        x1_ref[...] = x_ref[...] + _dot(merged, wo_ref[...])

    return _call(
        body, name="mixer_out", grid=(t // tm,),
        in_specs=[_rows(tm, d), _rows(tm, 3 * cw), _prev_halo(tm, 3 * cw), _rows(tm, 2 * d)]
        + [_stream_spec(dil, tm, GROUP_W) for dil in DILATIONS] * 2
        + [_resident((3, cw)), _resident((1, cw)), _resident((2, d)),
           _resident((d, cw)), _resident((d, ATTN_W)), _resident((d, d))],
        out_specs=[_rows(tm, d), _rows(tm, cw), _rows(tm, ATTN_W), _rows(tm, d), _rows(tm, d), _rows(tm, d)],
        out_shape=[jax.ShapeDtypeStruct((t, d), F32), jax.ShapeDtypeStruct((t, cw), BF16),
                   jax.ShapeDtypeStruct((t, ATTN_W), BF16), jax.ShapeDtypeStruct((t, d), BF16),
                   jax.ShapeDtypeStruct((t, d), BF16), jax.ShapeDtypeStruct((t, d), BF16)],
        args=(x, abcv, abcv, gates, *[_stream_view(a, dil) for a, dil in zip(os, DILATIONS)],
              *[_stream_view(a, dil) for a, dil in zip(lses, DILATIONS)], conv_w, conv_b, b_gate, w_pa, w_pb, w_o),
        semantics=("parallel",), carry=carry)


def _ffn_fwd(x1, target, g2, w_ut, conv_w, conv_b, w_d, g_f, carry=None):
    t, d = x1.shape
    dff = w_d.shape[0]
    tm = min(256, t)
    ck = _pick_tile(dff, 1408)

    def body(x1_ref, tg_ref, g2_ref, wut_ref, cw_ref, cb_ref, wd_ref, gf_ref, h2_ref, up_ref, act_ref, conv_ref,
             dx2_ref, acc_ref, loss_ref, halo_ref):
        m = pl.program_id(0)

        @pl.when(m == 0)
        def _():
            acc_ref[...] = jnp.zeros_like(acc_ref)
            loss_ref[...] = jnp.zeros_like(loss_ref)
            halo_ref[...] = jnp.zeros_like(halo_ref)

        h2 = _rms_fwd(x1_ref[...], g2_ref[...])[0].astype(BF16)
        h2_ref[...] = h2

        def conv(c0):
            p = _dot_nt(h2, wut_ref[c0:c0 + ck, :])
            up_ref[:, c0:c0 + ck] = p.astype(BF16)
            hp = halo_ref[:, c0:c0 + ck]
            halo_ref[:, c0:c0 + ck] = p[tm - HALO:, :]
            return (cw_ref[0:1, c0:c0 + ck] * _shift_down(p, hp, 2) + cw_ref[1:2, c0:c0 + ck] * _shift_down(p, hp, 1)
                    + cw_ref[2:3, c0:c0 + ck] * p + cb_ref[:, c0:c0 + ck])

        x2 = x1_ref[...]
        for c0 in range(0, dff, ck):
            gate = conv(c0)
            val = conv(dff + c0)
            conv_ref[:, c0:c0 + ck] = gate.astype(BF16)
            conv_ref[:, dff + c0:dff + c0 + ck] = val.astype(BF16)
            act = (gate * _sigmoid(gate) * val).astype(BF16)
            act_ref[:, c0:c0 + ck] = act
            x2 = x2 + _dot(act, wd_ref[c0:c0 + ck, :])
        y, _ = _rms_fwd(x2, gf_ref[...])
        diff = y - tg_ref[...]
        loss_ref[...] += 0.5 * jnp.sum(jnp.mean(diff * diff, axis=-1, keepdims=True))
        dx2, dg = _rms_bwd(x2, gf_ref[...], diff * (1.0 / d))
        dx2_ref[...] = dx2
        acc_ref[...] += _stack_rows([_colsum(dg)], d)

    return _call(
        body, name="ffn_fwd", grid=(t // tm,),
        in_specs=[_rows(tm, d), _rows(tm, d), _resident((1, d)), _resident((2 * dff, d)), _resident((3, 2 * dff)),
                  _resident((1, 2 * dff)), _resident((dff, d)), _resident((1, d))],
        out_specs=[_rows(tm, d), _rows(tm, 2 * dff), _rows(tm, dff), _rows(tm, 2 * dff), _rows(tm, d), _acc_spec(d),
                   _acc_spec(LANES)],
        out_shape=[jax.ShapeDtypeStruct((t, d), BF16), jax.ShapeDtypeStruct((t, 2 * dff), BF16),
                   jax.ShapeDtypeStruct((t, dff), BF16), jax.ShapeDtypeStruct((t, 2 * dff), BF16),
                   jax.ShapeDtypeStruct((t, d), F32), jax.ShapeDtypeStruct((SUBLANES, d), F32),
                   jax.ShapeDtypeStruct((SUBLANES, LANES), F32)],
        args=(x1, target, g2, w_ut, conv_w, conv_b, w_d, g_f), semantics=("arbitrary",), carry=carry,
        scratch=[pltpu.VMEM((HALO, 2 * dff), F32)])


def _ffn_act_bwd(dx2, conv, w_d):
    t, d = dx2.shape
    dff = w_d.shape[0]
    tm = min(256, t)
    ck = _pick_tile(dff, 1408)

    def body(dx2_ref, conv_ref, wd_ref, dup_ref, acc_ref):
        m = pl.program_id(0)

        @pl.when(m == 0)
        def _():
            acc_ref[...] = jnp.zeros_like(acc_ref)

        dx2v = dx2_ref[...].astype(BF16)
        for c0 in range(0, dff, ck):
            dact = _dot_nt(dx2v, wd_ref[c0:c0 + ck, :])
            gate = conv_ref[:, c0:c0 + ck].astype(F32)
            val = conv_ref[:, dff + c0:dff + c0 + ck].astype(F32)
            sg = _sigmoid(gate)
            dval = dact * gate * sg
            dgate = dact * val * sg * (1.0 + gate * (1.0 - sg))
            dup_ref[:, c0:c0 + ck] = dgate.astype(BF16)
            dup_ref[:, dff + c0:dff + c0 + ck] = dval.astype(BF16)
            acc_ref[:, c0:c0 + ck] += _stack_rows([_colsum(dgate)], ck)
            acc_ref[:, dff + c0:dff + c0 + ck] += _stack_rows([_colsum(dval)], ck)

    return pl.pallas_call(
        body, name="ffn_act_bwd", grid=(t // tm,),
        in_specs=[_rows(tm, d), _rows(tm, 2 * dff), _resident((dff, d))],
        out_specs=[_rows(tm, 2 * dff), _acc_spec(2 * dff)],
        out_shape=[jax.ShapeDtypeStruct((t, 2 * dff), BF16), jax.ShapeDtypeStruct((SUBLANES, 2 * dff), F32)],
        compiler_params=_params("arbitrary"),
    )(dx2, conv, w_d)


def _ffn_up_bwd(dup, up_pre, x1, dx2, conv_w, w_u, g2, carry=None):
    t, d = x1.shape
    n = dup.shape[1]
    tm = min(256, t)
    ck = _pick_tile(n, 1408)
    last = t // tm - 1

    def body(dup_ref, nxt_ref, up_ref, x1_ref, dx2_ref, cw_ref, wu_ref, g2_ref, dpre_ref, dx1_ref, acc_ref, accw_ref):
        m = pl.program_id(0)

        @pl.when(m == 0)
        def _():
            acc_ref[...] = jnp.zeros_like(acc_ref)
            accw_ref[...] = jnp.zeros_like(accw_ref)

        dh = jnp.zeros((tm, d), F32)
        for c0 in range(0, n, ck):
            du = dup_ref[:, c0:c0 + ck].astype(F32)
            hn = jnp.where(m < last, nxt_ref[:, c0:c0 + ck].astype(F32), 0.0)
            du1 = _shift_up(du, hn, 1)
            du2 = _shift_up(du, hn, 2)
            dpre = (cw_ref[2:3, c0:c0 + ck] * du + cw_ref[1:2, c0:c0 + ck] * du1
                    + cw_ref[0:1, c0:c0 + ck] * du2).astype(BF16)
            dpre_ref[:, c0:c0 + ck] = dpre
            dh = dh + _dot(dpre, wu_ref[c0:c0 + ck, :])
            p = up_ref[:, c0:c0 + ck].astype(F32)
            accw_ref[:, c0:c0 + ck] += _stack_rows([_colsum(du2 * p), _colsum(du1 * p), _colsum(du * p)], ck)
        dx, dg = _rms_bwd(x1_ref[...], g2_ref[...], dh)
        dx1_ref[...] = dx2_ref[...] + dx
        acc_ref[...] += _stack_rows([_colsum(dg)], d)

    return _call(
        body, name="ffn_up_bwd", grid=(t // tm,),
        in_specs=[_rows(tm, n), _next_halo(tm, n, t), _rows(tm, n), _rows(tm, d), _rows(tm, d), _resident((3, n)),
                  _resident((n, d)), _resident((1, d))],
        out_specs=[_rows(tm, n), _rows(tm, d), _acc_spec(d), _acc_spec(n)],
        out_shape=[jax.ShapeDtypeStruct((t, n), BF16), jax.ShapeDtypeStruct((t, d), F32),
                   jax.ShapeDtypeStruct((SUBLANES, d), F32), jax.ShapeDtypeStruct((SUBLANES, n), F32)],
        args=(dup, dup, up_pre, x1, dx2, conv_w, w_u, g2), semantics=("arbitrary",), carry=carry)


def _tn_matmul(a, b, name):
    t, mdim = a.shape
    n = b.shape[1]
    tk = min(1024, t)
    tmm = _pick_tile(mdim, 1536)
    tn = _pick_tile(n, 1024)

    def body(a_ref, b_ref, o_ref, acc_ref):
        k = pl.program_id(2)

        @pl.when(k == 0)
        def _():
            acc_ref[...] = jnp.zeros_like(acc_ref)

        acc_ref[...] += _dot_tn(a_ref[...].astype(BF16), b_ref[...].astype(BF16))

        @pl.when(k == t // tk - 1)
        def _():
            o_ref[...] = acc_ref[...].astype(BF16)

    return pl.pallas_call(
        body, name=name, grid=(mdim // tmm, n // tn, t // tk),
        in_specs=[pl.BlockSpec((tk, tmm), lambda i, j, k: (k, i)), pl.BlockSpec((tk, tn), lambda i, j, k: (k, j))],
        out_specs=pl.BlockSpec((tmm, tn), lambda i, j, k: (i, j)),
        out_shape=jax.ShapeDtypeStruct((mdim, n), BF16),
        scratch_shapes=[pltpu.VMEM((tmm, tn), F32)],
        compiler_params=_params("parallel", "parallel", "arbitrary"),
    )(a, b)


def _mixer_bwd(dx1, gates, yap, ybp, os, lses, b_gate, w_o, w_pa, w_pb):
    t, d = dx1.shape
    cw = w_pa.shape[1]
    tm = min(256, t)

    def body(dx1_ref, gates_ref, yap_ref, ybp_ref, o0_ref, o1_ref, o2_ref, l0_ref, l1_ref, l2_ref, bg_ref, wo_ref,
             wpa_ref, wpb_ref, dgates_ref, dyap_ref, dybp_ref, dya_ref, do0_ref, do1_ref, do2_ref, dl0_ref, dl1_ref,
             dl2_ref, acc_ref):
        m = pl.program_id(0)

        @pl.when(m == 0)
        def _():
            acc_ref[...] = jnp.zeros_like(acc_ref)

        dmg = _dot_nt(dx1_ref[...].astype(BF16), wo_ref[...])
        sa = _sigmoid(gates_ref[:, 0:d].astype(F32) + bg_ref[0:1, :])
        sb = _sigmoid(gates_ref[:, d:2 * d].astype(F32) + bg_ref[1:2, :])
        dyap = (dmg * sa).astype(BF16)
        dybp = (dmg * sb).astype(BF16)
        dga = dmg * yap_ref[...].astype(F32) * sa * (1.0 - sa)
        dgb = dmg * ybp_ref[...].astype(F32) * sb * (1.0 - sb)
        dyap_ref[...] = dyap
        dybp_ref[...] = dybp
        dgates_ref[:, 0:d] = dga.astype(BF16)
        dgates_ref[:, d:2 * d] = dgb.astype(BF16)
        acc_ref[...] += _stack_rows([_colsum(dga), _colsum(dgb)], d)
        dya_ref[...] = _dot(dyap, wpa_ref[...]).astype(BF16)
        dyb = _dot(dybp, wpb_ref[...])

        ri = lax.broadcasted_iota(jnp.int32, (GROUP_W, GROUP_W), 0) // HEAD_DIM
        ci = lax.broadcasted_iota(jnp.int32, (GROUP_W, GROUP_W), 1) // HEAD_DIM
        same_head = (ri == ci).astype(BF16)
        alphas = _group_softmax([_load_streams(r, dil, tm) for r, dil in zip((l0_ref, l1_ref, l2_ref), DILATIONS)])
        prod = jnp.zeros((tm, GROUP_W), F32)
        for i, (o_ref, do_ref, dil) in enumerate(zip((o0_ref, o1_ref, o2_ref), (do0_ref, do1_ref, do2_ref), DILATIONS)):
            dov = alphas[i] * dyb[:, i * GROUP_W:(i + 1) * GROUP_W]
            _store_streams(do_ref, dil, tm, dov.astype(BF16))
            prod = prod + dov * _load_streams(o_ref, dil, tm).astype(F32)
        hi = prod.astype(BF16)
        lo = (prod - hi.astype(F32)).astype(BF16)
        dtot = _dot(hi, same_head) + _dot(lo, same_head)
        for alpha, dl_ref, dil in zip(alphas, (dl0_ref, dl1_ref, dl2_ref), DILATIONS):
            _store_streams(dl_ref, dil, tm, alpha * dtot)

    streams = [_stream_spec(dil, tm, GROUP_W) for dil in DILATIONS]
    res = _call(
        body, name="mixer_bwd", grid=(t // tm,),
        in_specs=[_rows(tm, d), _rows(tm, 2 * d), _rows(tm, d), _rows(tm, d)] + streams * 2
        + [_resident((2, d)), _resident((d, d)), _resident((d, cw)), _resident((d, ATTN_W))],
        out_specs=[_rows(tm, 2 * d), _rows(tm, d), _rows(tm, d), _rows(tm, cw)] + streams * 2 + [_acc_spec(d)],
        out_shape=[jax.ShapeDtypeStruct((t, 2 * d), BF16), jax.ShapeDtypeStruct((t, d), BF16),
                   jax.ShapeDtypeStruct((t, d), BF16), jax.ShapeDtypeStruct((t, cw), BF16)]
        + [jax.ShapeDtypeStruct((dil, t // dil, GROUP_W), BF16) for dil in DILATIONS]
        + [jax.ShapeDtypeStruct((dil, t // dil, GROUP_W), F32) for dil in DILATIONS]
        + [jax.ShapeDtypeStruct((SUBLANES, d), F32)],
        args=(dx1, gates, yap, ybp, *[_stream_view(a, dil) for a, dil in zip(os, DILATIONS)],
              *[_stream_view(a, dil) for a, dil in zip(lses, DILATIONS)], b_gate, w_o, w_pa, w_pb),
        semantics=("arbitrary",))
    dgates, dyap, dybp, dya = res[:4]
    dos = [a.reshape(t, GROUP_W) for a in res[4:7]]
    dls = [a.reshape(t, GROUP_W) for a in res[7:10]]
    return dgates, dyap, dybp, dya, dos, dls, res[10]


def _attn_bwd(s, do, lse, dl, dil, carry=None):
    t = s.shape[0] * s.shape[1]
    nb = t // QBLK
    per_stream = nb // dil

    def body(q_ref, qn_ref, kc_ref, kp_ref, vc_ref, vp_ref, do_ref, don_ref, lse_ref, lsen_ref, dl_ref, dln_ref,
             ds_ref):
        b = pl.program_id(0)
        lane, heads = _head_masks()
        first_has_prev = lax.rem(2 * b, per_stream) != 0
        last_has_next = lax.rem(2 * b + 2, per_stream) != 0

        def cols(v):
            return jnp.concatenate([jnp.sum(jnp.where(lane == h * HEAD_DIM, v, 0.0), axis=1, keepdims=True)
                                    for h in range(HEADS_PER_GROUP)], axis=0)

        def pair(qs, dos, k, v, valid, lse_c, dl_c):
            s = jnp.where(valid, _dot_nt(qs, k) * ATTN_SCALE, NEG_INF)
            p = jnp.exp(s - lse_c)
            ds = p * (_dot_nt(dos, v) - dl_c)
            return p.astype(BF16), ds.astype(BF16)

        lo, hi = slice(0, QBLK), slice(QBLK, 2 * QBLK)
        for j, rows in enumerate((lo, hi)):
            q, do, lse, dl = q_ref[rows, :], do_ref[rows, :], lse_ref[rows, :], dl_ref[rows, :]
            kc, vc = kc_ref[rows, :], vc_ref[rows, :]
            if j == 0:
                k2 = jnp.concatenate([kp_ref[...], kc], axis=0)
                v2 = jnp.concatenate([vp_ref[...], vc], axis=0)
                qn, don, lsen, dln = q_ref[hi, :], do_ref[hi, :], lse_ref[hi, :], dl_ref[hi, :]
                mask, mask_n = _band_mask(first_has_prev), _next_mask(True)
            else:
                k2, v2 = kc_ref[...], vc_ref[...]
                qn, don, lsen, dln = qn_ref[...], don_ref[...], lsen_ref[...], dln_ref[...]
                mask, mask_n = _band_mask(True), _next_mask(last_has_next)
            qs, qns = _stack_heads(q, heads), _stack_heads(qn, heads)
            dos, dons = _stack_heads(do, heads), _stack_heads(don, heads)
            p_q, ds_q = pair(qs, dos, k2, v2, mask, cols(lse), cols(dl))
            p_n, ds_n = pair(qns, dons, kc, vc, mask_n, cols(lsen), cols(dln))
            dq = _merge_heads(_dot(ds_q, k2), heads)
            dk = _dot_tn(jnp.concatenate([ds_q[:, QBLK:], ds_n], axis=0), jnp.concatenate([qs, qns], axis=0))
            dv = _dot_tn(jnp.concatenate([p_q[:, QBLK:], p_n], axis=0), jnp.concatenate([dos, dons], axis=0))
            ds_ref[rows, 0:GROUP_W] = (dq * ATTN_SCALE).astype(BF16)
            ds_ref[rows, GROUP_W:2 * GROUP_W] = (dk * ATTN_SCALE).astype(BF16)
            ds_ref[rows, 2 * GROUP_W:3 * GROUP_W] = dv.astype(BF16)

    sv = s.reshape(t, 3 * GROUP_W)
    cur, nxt = _pair_block(0), _edge_block(0, 2, nb)
    return _call(
        body, name=f"attn_bwd_d{dil}", grid=(nb // 2,),
        in_specs=[cur, nxt, _pair_block(1), _edge_block(1, -1, nb), _pair_block(2), _edge_block(2, -1, nb),
                  cur, nxt, cur, nxt, cur, nxt],
        out_specs=[pl.BlockSpec((2 * QBLK, 3 * GROUP_W), lambda b: (b, 0))],
        out_shape=[jax.ShapeDtypeStruct((t, 3 * GROUP_W), BF16)],
        args=(sv, sv, sv, sv, sv, sv, do, do, lse, lse, dl, dl), semantics=("parallel",), carry=carry)


def _conv_mixer_bwd(abcv, dya, conv_w, conv_b):
    t = abcv.shape[0]
    cw = conv_w.shape[1]
    tm = min(256, t)
    last = t // tm - 1

    def body(a_ref, ap_ref, an_ref, dya_ref, dyan_ref, cw_ref, cb_ref, d_ref, acc_ref):
        m = pl.program_id(0)

        @pl.when(m == 0)
        def _():
            acc_ref[...] = jnp.zeros_like(acc_ref)

        ab = a_ref[:, 0:cw].astype(F32)
        ac = a_ref[:, cw:2 * cw].astype(F32)
        av = a_ref[:, 2 * cw:3 * cw].astype(F32)
        u = ac * av
        hu = ap_ref[:, cw:2 * cw].astype(F32) * ap_ref[:, 2 * cw:3 * cw].astype(F32)
        hu = jnp.where(m > 0, hu, 0.0)
        u1 = _shift_down(u, hu, 1)
        u2 = _shift_down(u, hu, 2)
        cv = cw_ref[0:1, :] * u2 + cw_ref[1:2, :] * u1 + cw_ref[2:3, :] * u + cb_ref[...]
        dya_v = dya_ref[...].astype(F32)
        dcv = dya_v * ab
        ndcv = jnp.where(m < last, dyan_ref[...].astype(F32) * an_ref[:, 0:cw].astype(F32), 0.0)
        du = (cw_ref[2:3, :] * dcv + cw_ref[1:2, :] * _shift_up(dcv, ndcv, 1)
              + cw_ref[0:1, :] * _shift_up(dcv, ndcv, 2))
        d_ref[:, 0:cw] = (dya_v * cv).astype(BF16)
        d_ref[:, cw:2 * cw] = (du * av).astype(BF16)
        d_ref[:, 2 * cw:3 * cw] = (du * ac).astype(BF16)
        acc_ref[...] += _stack_rows([_colsum(dcv * u2), _colsum(dcv * u1), _colsum(dcv * u), _colsum(dcv)], cw)

    return pl.pallas_call(
        body, name="conv_mixer_bwd", grid=(t // tm,),
        in_specs=[_rows(tm, 3 * cw), _prev_halo(tm, 3 * cw), _next_halo(tm, 3 * cw, t), _rows(tm, cw),
                  _next_halo(tm, cw, t), _resident((3, cw)), _resident((1, cw))],
        out_specs=[_rows(tm, 3 * cw), _acc_spec(cw)],
        out_shape=[jax.ShapeDtypeStruct((t, 3 * cw), BF16), jax.ShapeDtypeStruct((SUBLANES, cw), F32)],
        compiler_params=_params("arbitrary"),
    )(abcv, abcv, abcv, dya, dya, conv_w, conv_b)


def _in_proj_bwd(x, dx1, dabcv, dss, dgates, w_in, g1, carry=None):
    t, d = x.shape
    qkv0 = dabcv.shape[1]
    n = w_in.shape[0]
    tm = min(ROWS_MATMUL, t)

    def body(x_ref, dx1_ref, da_ref, ds0_ref, ds1_ref, ds2_ref, dg_ref, w_ref, g_ref, dx_ref, acc_ref):
        m = pl.program_id(0)

        @pl.when(m == 0)
        def _():
            acc_ref[...] = jnp.zeros_like(acc_ref)

        dh = _dot(da_ref[...], w_ref[0:qkv0, :]) + _dot(dg_ref[...], w_ref[qkv0 + 3 * ATTN_W:n, :])
        for gi, (ds_ref, dil) in enumerate(zip((ds0_ref, ds1_ref, ds2_ref), DILATIONS)):
            ds = _load_streams(ds_ref, dil, tm)
            for j in range(3):
                c0 = qkv0 + j * ATTN_W + gi * GROUP_W
                dh = dh + _dot(ds[:, j * GROUP_W:(j + 1) * GROUP_W], w_ref[c0:c0 + GROUP_W, :])
        dx, dg = _rms_bwd(x_ref[...], g_ref[...], dh)
        dx_ref[...] = dx1_ref[...] + dx
        acc_ref[...] += _stack_rows([_colsum(dg)], d)

    return _call(
        body, name="in_proj_bwd", grid=(t // tm,),
        in_specs=[_rows(tm, d), _rows(tm, d), _rows(tm, qkv0)]
        + [_stream_spec(dil, tm, 3 * GROUP_W) for dil in DILATIONS]
        + [_rows(tm, 2 * d), _resident((n, d)), _resident((1, d))],
        out_specs=[_rows(tm, d), _acc_spec(d)],
        out_shape=[jax.ShapeDtypeStruct((t, d), F32), jax.ShapeDtypeStruct((SUBLANES, d), F32)],
        args=(x, dx1, dabcv, *[_stream_view(a, dil) for a, dil in zip(dss, DILATIONS)], dgates, w_in, g1),
        semantics=("arbitrary",), carry=carry)


def _dw_in_qkv(ds, h, dil):
    t, d = h.shape
    tk = min(1024, t)
    sub = min(256, t)
    width = 3 * GROUP_W

    def body(ds_ref, h_ref, o_ref, acc_ref):
        k = pl.program_id(0)

        @pl.when(k == 0)
        def _():
            acc_ref[...] = jnp.zeros_like(acc_ref)

        upd = None
        for i in range(tk // sub):
            rows = ds_ref[:, i * (sub // dil):(i + 1) * (sub // dil), :].reshape(sub, width)
            if dil > 1:
                rows = _permute_rows(_perm(dil, sub, inverse=True), rows)
            term = _dot_tn(rows, h_ref[i * sub:(i + 1) * sub, :])
            upd = term if upd is None else upd + term
        acc_ref[...] += upd

        @pl.when(k == t // tk - 1)
        def _():
            o_ref[...] = acc_ref[...].astype(BF16)

    return pl.pallas_call(
        body, name=f"dw_in_qkv_d{dil}", grid=(t // tk,),
        in_specs=[_stream_spec(dil, tk, width), _rows(tk, d)],
        out_specs=pl.BlockSpec((width, d), lambda k: (0, 0)),
        out_shape=jax.ShapeDtypeStruct((width, d), BF16),
        scratch_shapes=[pltpu.VMEM((width, d), F32)],
        compiler_params=_params("arbitrary"),
    )(_stream_view(ds, dil), h)


def _local_step(x, target, p, late):
    cw = p["conv_a_w"].shape[1]
    (h, abcv, gates, *ss), (g_up,) = _in_proj(x, p["norm_mix_g"], p["w_in"], cw,
                                              carry=_Exchange("gather", [late["w_up"]]))
    w_up = _full_from_gathered(g_up)
    (o0, lse0), g_proj = _attn_fwd(ss[0], DILATIONS[0],
                                   carry=_Exchange("gather", [late["w_proj_a"], late["w_proj_b"]]))
    (o1, lse1), (g_out,) = _attn_fwd(ss[1], DILATIONS[1], carry=_Exchange("gather", [late["w_out"]]))
    o2, lse2 = _attn_fwd(ss[2], DILATIONS[2])
    w_pa, w_pb, w_out = [_full_from_gathered(g) for g in (*g_proj, g_out)]
    os, lses = (o0, o1, o2), (lse0, lse1, lse2)
    (x1, ya, yb, yap, ybp, merged), (g_down,) = _mixer_out(
        x, abcv, gates, os, lses, p["conv_a_w"], p["conv_a_b"], p["b_gate"], w_pa, w_pb, w_out,
        carry=_Exchange("gather", [late["w_down"]]))
    w_down = _full_from_gathered(g_down)
    h2, up_pre, act, conv, dx2, acc_gf, loss = _ffn_fwd(x1, target, p["norm_ffn_g"], w_up, p["ffn_conv_w"],
                                                        p["ffn_conv_b"], w_down, p["final_norm_g"])

    parts, got = {}, {}
    dup, acc_fb = _ffn_act_bwd(dx2, conv, w_down)
    parts["w_down"] = _by_destination(_tn_matmul(act, dx2, "dw_down"))
    (dpre, dx1, acc_g2, acc_fw), (got["w_down"],) = _ffn_up_bwd(dup, up_pre, x1, dx2, p["ffn_conv_w"], w_up,
                                                                p["norm_ffn_g"],
                                                                carry=_Exchange("scatter", [parts["w_down"]]))
    parts["w_up"] = _by_destination(_tn_matmul(dpre, h2, "dw_up"))
    dgates, dyap, dybp, dya, dos, dls, acc_bg = _mixer_bwd(dx1, gates, yap, ybp, os, lses, p["b_gate"], w_out,
                                                           w_pa, w_pb)
    parts["w_out"] = _by_destination(_tn_matmul(merged, dx1, "dw_out"))
    parts["w_proj_a"] = _by_destination(_tn_matmul(dyap, ya, "dw_proj_a"))
    parts["w_proj_b"] = _by_destination(_tn_matmul(dybp, yb, "dw_proj_b"))
    minor = ("w_out", "w_proj_a", "w_proj_b")
    half = parts["w_up"].shape[1] // 2
    (ds0,), received = _attn_bwd(ss[0], dos[0], lses[0], dls[0], DILATIONS[0],
                                 carry=_Exchange("scatter", [parts[n] for n in minor]))
    got.update(zip(minor, received))
    (ds1,), first_half = _attn_bwd(ss[1], dos[1], lses[1], dls[1], DILATIONS[1],
                                   carry=_Exchange("scatter", [parts["w_up"]], rows=(0, half)))
    (ds2,), (got["w_up"],) = _attn_bwd(ss[2], dos[2], lses[2], dls[2], DILATIONS[2],
                                       carry=_Exchange("scatter", [parts["w_up"]], rows=(half, half),
                                                       into=first_half))
    dss = [ds0, ds1, ds2]
    dabcv, acc_ca = _conv_mixer_bwd(abcv, dya, p["conv_a_w"], p["conv_a_b"])
    dw_s = [_dw_in_qkv(ds, h, dil) for ds, dil in zip(dss, DILATIONS)]
    dw_qkv = [w[j * GROUP_W:(j + 1) * GROUP_W] for j in range(3) for w in dw_s]
    g_w_in = jnp.concatenate([_tn_matmul(dabcv, h, "dw_in_a"), *dw_qkv, _tn_matmul(dgates, h, "dw_in_g")], axis=0)
    parts["w_in"] = _by_destination(g_w_in)
    (dx, acc_g1), (got["w_in"],) = _in_proj_bwd(x, dx1, dabcv, dss, dgates, p["w_in"], p["norm_mix_g"],
                                                carry=_Exchange("scatter", [parts["w_in"]]))
    small = dict(norm_mix_g=acc_g1[0:1], b_gate=acc_bg[0:2], conv_a_w=acc_ca[0:3], conv_a_b=acc_ca[3:4],
                 norm_ffn_g=acc_g2[0:1], ffn_conv_w=acc_fw[0:3], ffn_conv_b=acc_fb[0:1], final_norm_g=acc_gf[0:1])
    return loss[0, 0], dx, parts, got, small


def _all_gather(shards):
    n = len(shards)

    def body(*refs):
        ins, outs = refs[:n], refs[n:2 * n]
        send_sems, recv_sems, local_sems = refs[2 * n:]
        x, y, c = _mesh_pos()
        me, sibling = (x, y, c), (x, y, 1 - c)
        chips = [(1 - x, y), (x, 1 - y), (1 - x, 1 - y)]

        def copy(i, k, block, to, src=None):
            rows = outs[i].at[_dev_index(*block)]
            return pltpu.make_async_remote_copy(
                src_ref=rows if src is None else src, dst_ref=rows, send_sem=send_sems.at[i, k],
                recv_sem=recv_sems.at[i, k], device_id=to, device_id_type=MESH)

        mine, first, passed = [], [], []
        for i in range(n):
            cp = pltpu.make_async_copy(ins[i], outs[i].at[_dev_index(*me)], local_sems.at[i])
            cp.start()
            mine.append(cp)
            first.append(copy(i, 0, me, sibling, src=ins[i]))
            first += [copy(i, 1 + j, me, (*chip, c), src=ins[i]) for j, chip in enumerate(chips)]
        for cp in first:
            cp.start()
        for i in range(n):
            for j, chip in enumerate(chips):
                copy(i, 1 + j, (*chip, c), me).wait_recv()
                fw = copy(i, 4 + j, (*chip, c), sibling)
                fw.start()
                passed.append(fw)
        for i in range(n):
            copy(i, 0, sibling, me).wait_recv()
            for j, chip in enumerate(chips):
                copy(i, 4 + j, (*chip, 1 - c), me).wait_recv()
        for cp in first + passed:
            cp.wait_send()
        for cp in mine:
            cp.wait()

    return pl.pallas_call(
        body, name="all_gather_weights",
        in_specs=[ANY] * n, out_specs=[ANY] * n,
        out_shape=[jax.ShapeDtypeStruct((N_DEV,) + s.shape, s.dtype) for s in shards],
        scratch_shapes=[pltpu.SemaphoreType.DMA((n, 7)), pltpu.SemaphoreType.DMA((n, 7)),
                        pltpu.SemaphoreType.DMA((n,))],
    )(*shards)


def _all_reduce_small(v):
    r = v.shape[0]

    def body(v_ref, o_ref, gath, send_sems, recv_sems):
        x, y, c = _mesh_pos()
        me = _dev_index(x, y, c)
        gath[me] = v_ref[...]
        flips = [(kx, ky, kc) for kx in (0, 1) for ky in (0, 1) for kc in (0, 1)][1:]
        copies = []
        for k, (kx, ky, kc) in enumerate(flips):
            px = 1 - x if kx else x
            py = 1 - y if ky else y
            pc = 1 - c if kc else c
            cp = pltpu.make_async_remote_copy(
                src_ref=v_ref, dst_ref=gath.at[me], send_sem=send_sems.at[k], recv_sem=recv_sems.at[k],
                device_id=(px, py, pc), device_id_type=MESH)
            cp.start()
            copies.append((cp, _dev_index(px, py, pc)))
        for k, (cp, peer) in enumerate(copies):
            pltpu.make_async_remote_copy(
                src_ref=v_ref, dst_ref=gath.at[peer], send_sem=send_sems.at[k], recv_sem=recv_sems.at[k],
                device_id=(x, y, c), device_id_type=MESH).wait_recv()
        for cp, _ in copies:
            cp.wait_send()
        total = gath[0]
        for j in range(1, N_DEV):
            total = total + gath[j]
        o_ref[...] = total

    return pl.pallas_call(
        body, name="all_reduce_small",
        in_specs=[pl.BlockSpec(memory_space=pltpu.VMEM)], out_specs=pl.BlockSpec(memory_space=pltpu.VMEM),
        out_shape=jax.ShapeDtypeStruct((r, LANES), F32),
        scratch_shapes=[pltpu.VMEM((N_DEV, r, LANES), F32), pltpu.SemaphoreType.DMA((7,)),
                        pltpu.SemaphoreType.DMA((7,))],
    )(v)


def _adamw_math(w, g, m, v):
    m2 = ADAM_B1 * m + (1.0 - ADAM_B1) * g
    v2 = ADAM_B2 * v + (1.0 - ADAM_B2) * (g * g)
    m_hat = m2 / (1.0 - ADAM_B1 ** ADAM_STEP)
    v_hat = v2 / (1.0 - ADAM_B2 ** ADAM_STEP)
    delta = -ADAM_LR * (m_hat / (jnp.sqrt(v_hat) + ADAM_EPS) + ADAM_WD * w)
    return delta, m2, v2


def _adamw_big(w, m, v, part, got, me):
    r, c = w.shape
    tr = max(t for t in range(HALO, min(r, 512) + 1, HALO) if r % t == 0)

    def body(me_ref, w_ref, m_ref, v_ref, own_ref, *rest):
        del me_ref
        got_refs, (g_out, d_out, m_out, v_out) = rest[:N_DEV - 1], rest[N_DEV - 1:]
        g = own_ref[...].astype(F32)
        for ref in got_refs:
            g = g + ref[...].astype(F32)
        delta, m2, v2 = _adamw_math(w_ref[...], g, m_ref[...], v_ref[...])
        g_out[...] = g
        d_out[...] = delta
        m_out[...] = m2
        v_out[...] = v2

    def peer_block(k):
        return pl.BlockSpec((None, tr, c), lambda i, me_ref: (jnp.bitwise_xor(me_ref[0], k), i, 0))

    plain = pl.BlockSpec((tr, c), lambda i, me_ref: (i, 0))
    out = jax.ShapeDtypeStruct((r, c), F32)
    return pl.pallas_call(
        body, name="adamw_big",
        grid_spec=pltpu.PrefetchScalarGridSpec(
            num_scalar_prefetch=1, grid=(r // tr,),
            in_specs=[plain, plain, plain] + [peer_block(k) for k in range(N_DEV)],
            out_specs=[plain] * 4),
        out_shape=[out] * 4,
        compiler_params=_params("parallel"),
    )(me, w, m, v, part, *([got] * (N_DEV - 1)))


def _adamw_small(ws, gs, ms, vs):
    n = len(ws)

    def body(*refs):
        ins, outs = refs[:4 * n], refs[4 * n:]
        for i in range(n):
            delta, m2, v2 = _adamw_math(ins[i][...], ins[n + i][...], ins[2 * n + i][...], ins[3 * n + i][...])
            outs[i][...] = delta
            outs[n + i][...] = m2
            outs[2 * n + i][...] = v2

    out = [jax.ShapeDtypeStruct(w.shape, F32) for w in ws]
    res = pl.pallas_call(body, name="adamw_small", out_shape=out * 3)(*ws, *gs, *ms, *vs)
    return res[:n], res[n:2 * n], res[2 * n:]


BIG = ("w_in", "w_proj_a", "w_proj_b", "w_out", "w_up", "w_down")
LATE = ("w_proj_a", "w_proj_b", "w_out", "w_up", "w_down")
COLUMN_SHARDED = ("w_in", "w_proj_a", "w_proj_b", "w_up")
SMALL = ("norm_mix_g", "b_gate", "conv_a_w", "conv_a_b", "norm_ffn_g", "ffn_conv_w", "ffn_conv_b", "final_norm_g")
SMALL_SHARDED = ("b_gate", "conv_a_w", "ffn_conv_w")
WEIGHTS = ("norm_mix_g", "w_in", "b_gate", "conv_a_w", "conv_a_b", "w_proj_a", "w_proj_b", "w_out", "norm_ffn_g",
           "w_up", "ffn_conv_w", "ffn_conv_b", "w_down", "final_norm_g")


def _pack(vectors, rows):
    flat = jnp.concatenate([v.reshape(-1) for v in vectors])
    return jnp.pad(flat, (0, rows * LANES - flat.shape[0])).reshape(rows, LANES)


def _packed_rows(count):
    rows = -(-count // LANES)
    return -(-rows // SUBLANES) * SUBLANES


def _unpack(packed, shapes):
    flat = packed.reshape(-1)
    out, lo = [], 0
    for s in shapes:
        size = 1
        for dim in s:
            size *= dim
        out.append(flat[lo:lo + size].reshape(s))
        lo += size
    return out


def _full_from_gathered(gathered):
    _, r, c = gathered.shape
    return gathered.reshape(N_DEV * r, c)


def _by_destination(grad):
    rr, cc = grad.shape
    return grad.reshape(N_DEV, rr // N_DEV, cc)


def _block2d(name, a):
    a = a.reshape(a.shape[-2:])
    return a.T if name in COLUMN_SHARDED else a


def kernel(x, norm_mix_g, w_in, b_gate, conv_a_w, conv_a_b, w_proj_a, w_proj_b, w_out, norm_ffn_g, w_up, ffn_conv_w, ffn_conv_b, w_down, final_norm_g, loss_target, m_norm_mix_g, m_w_in, m_b_gate, m_conv_a_w, m_conv_a_b, m_w_proj_a, m_w_proj_b, m_w_out, m_norm_ffn_g, m_w_up, m_ffn_conv_w, m_ffn_conv_b, m_w_down, m_final_norm_g, v_norm_mix_g, v_w_in, v_b_gate, v_conv_a_w, v_conv_a_b, v_w_proj_a, v_w_proj_b, v_w_out, v_norm_ffn_g, v_w_up, v_ffn_conv_w, v_ffn_conv_b, v_w_down, v_final_norm_g):
    given = dict(locals())
    shard = {n: given[n] for n in WEIGHTS}
    mom_m = {n: given["m_" + n] for n in WEIGHTS}
    mom_v = {n: given["v_" + n] for n in WEIGHTS}
    xi, yi, ci = _mesh_pos()
    me = _dev_index(xi, yi, ci)
    me1 = me.astype(jnp.int32).reshape(1)

    big2d = {n: _block2d(n, shard[n]) for n in BIG}
    small_shapes = [shard[n].shape[1:] for n in SMALL_SHARDED]
    n_small = sum(s[0] * s[1] for s in small_shapes)
    packed_small = _pack([shard[n] for n in SMALL_SHARDED], _packed_rows(n_small))
    gathered = _all_gather([big2d["w_in"].astype(BF16), packed_small])
    p = {"w_in": _full_from_gathered(gathered[0])}
    flat_small = gathered[-1].reshape(N_DEV, -1)
    lo = 0
    for n, (rows, width) in zip(SMALL_SHARDED, small_shapes):
        blocks = flat_small[:, lo:lo + rows * width].reshape(N_DEV, rows, width)
        p[n] = blocks.transpose(1, 0, 2).reshape(rows, N_DEV * width)
        lo += rows * width
    p["norm_mix_g"], p["norm_ffn_g"] = shard["norm_mix_g"], shard["norm_ffn_g"]
    p["conv_a_b"], p["ffn_conv_b"] = shard["conv_a_b"], shard["ffn_conv_b"]
    p["final_norm_g"] = shard["final_norm_g"].reshape(1, -1)
    late = {n: big2d[n].astype(BF16) for n in LATE}

    loss_part, dx, parts, got, g_small = _local_step(x[0], loss_target[0], p, late)

    results = {}
    for n in BIG:
        outs = _adamw_big(big2d[n], _block2d(n, mom_m[n]), _block2d(n, mom_v[n]), parts[n], got[n], me1)
        results[n] = [_block2d(n, o).reshape(shard[n].shape) for o in outs]

    small_full_shapes = [g_small[n].shape for n in SMALL]
    n_vec = sum(s[0] * s[1] for s in small_full_shapes) + 1
    packed = _pack([g_small[n] for n in SMALL] + [loss_part.reshape(1)], _packed_rows(n_vec))
    reduced = _all_reduce_small(packed)
    *g_full, loss_vec = _unpack(reduced, small_full_shapes + [(1,)])
    loss = loss_vec[0]
    own_g = []
    for n, g in zip(SMALL, g_full):
        if n in SMALL_SHARDED:
            width = shard[n].shape[-1]
            g = lax.dynamic_slice_in_dim(g, me * width, width, axis=1)
        own_g.append(g.reshape(shard[n].shape))
    def rows2d(a):
        return a.reshape(-1, a.shape[-1])

    deltas, new_ms, new_vs = _adamw_small([rows2d(shard[n]) for n in SMALL], [rows2d(g) for g in own_g],
                                          [rows2d(mom_m[n]) for n in SMALL], [rows2d(mom_v[n]) for n in SMALL])
    for i, n in enumerate(SMALL):
        results[n] = [own_g[i]] + [a.reshape(shard[n].shape) for a in (deltas[i], new_ms[i], new_vs[i])]

    grad_x = dx.reshape(x.shape)
    return (loss, grad_x, *[results[n][0] for n in WEIGHTS], *[results[n][1] for n in WEIGHTS],
            *[results[n][2] for n in WEIGHTS], *[results[n][3] for n in WEIGHTS])
```

```python
import functools

import jax
import jax.numpy as jnp
from jax import lax
from jax.experimental import pallas as pl
from jax.experimental.pallas import tpu as pltpu

F32 = jnp.float32
BF16 = jnp.bfloat16
MESH = pl.DeviceIdType.MESH

N_DEV = 8
RMS_EPS = 1e-6
NEG_INF = -1e30
N_GROUPS = 3
DILATIONS = (1, 4, 16)
HEADS_PER_GROUP = 4
HEAD_DIM = 64
GROUP_W = HEADS_PER_GROUP * HEAD_DIM
ATTN_W = N_GROUPS * GROUP_W
QBLK = 128
ATTN_SCALE = HEAD_DIM ** -0.5

ADAM_LR = 0.001
ADAM_B1 = 0.9
ADAM_B2 = 0.999
ADAM_EPS = 1e-08
ADAM_WD = 0.01
ADAM_STEP = 10

PERM_TOKENS = 256
ROWS_MATMUL = 512
HALO = 16
LANES = 128
SUBLANES = 8
VMEM_LIMIT_BYTES = 56 * 1024 * 1024


def _params(*sem):
    return pltpu.CompilerParams(dimension_semantics=sem, vmem_limit_bytes=VMEM_LIMIT_BYTES)


def _pick_tile(n, cap):
    if n <= cap:
        return n
    best = None
    for t in range(LANES, cap + 1, LANES):
        if n % t == 0:
            best = t
    assert best is not None, (n, cap)
    return best


def _rows(tm, c, j=0):
    return pl.BlockSpec((tm, c), lambda m: (m, j))


def _prev_halo(tm, c):
    return pl.BlockSpec((HALO, c), lambda m: (jnp.maximum(m * (tm // HALO) - 1, 0), 0))


def _next_halo(tm, c, t_total):
    last = t_total // HALO - 1
    return pl.BlockSpec((HALO, c), lambda m: (jnp.minimum((m + 1) * (tm // HALO), last), 0))


def _resident(shape):
    nd = len(shape)
    return pl.BlockSpec(shape, lambda *_: (0,) * nd, pipeline_mode=pl.Buffered(1))


def _acc_spec(c):
    return pl.BlockSpec((SUBLANES, c), lambda *_: (0, 0))


def _shift_down(u, halo, k):
    edge = jnp.concatenate([halo[HALO - SUBLANES:], u[:SUBLANES]], axis=0)
    head = pltpu.roll(edge, k, 0)[SUBLANES:]
    return jnp.concatenate([head, pltpu.roll(u, k, 0)[SUBLANES:]], axis=0)


def _shift_up(u, halo, k):
    n = u.shape[0]
    edge = jnp.concatenate([u[n - SUBLANES:], halo[:SUBLANES]], axis=0)
    tail = pltpu.roll(edge, 2 * SUBLANES - k, 0)[:SUBLANES]
    return jnp.concatenate([pltpu.roll(u, n - k, 0)[:n - SUBLANES], tail], axis=0)


def _interleave(tm, inverse=False):
    return _perm(tm // SUBLANES, tm, inverse)


def _edge_groups(u, halo, k, from_end):
    n = u.shape[0]
    sub = lax.broadcasted_iota(jnp.int32, (SUBLANES, u.shape[1]), 0)
    out = []
    for j in range(2 - k, 2):
        lo = n - HALO + j * SUBLANES if from_end else j * SUBLANES
        own, other = u[lo:lo + SUBLANES], halo[j * SUBLANES:(j + 1) * SUBLANES]
        if from_end:
            out.append(pltpu.roll(jnp.where(sub == SUBLANES - 1, other, own), 1, 0))
        else:
            out.append(pltpu.roll(jnp.where(sub == 0, other, own), SUBLANES - 1, 0))
    return out


def _shift_down_il(u, halo, k):
    return jnp.concatenate(_edge_groups(u, halo, k, True) + [u[:u.shape[0] - k * SUBLANES]], axis=0)


def _shift_up_il(u, halo, k):
    if k == 1:
        edge = _edge_groups(u, halo, 2, False)[:1]
    else:
        edge = _edge_groups(u, halo, 2, False)
    return jnp.concatenate([u[k * SUBLANES:]] + edge, axis=0)


def _stack_rows(rows, c):
    idx = lax.broadcasted_iota(jnp.int32, (SUBLANES, c), 0)
    out = jnp.zeros((SUBLANES, c), F32)
    for i, r in enumerate(rows):
        out = out + jnp.where(idx == i, r, 0.0)
    return out


def _colsum(v):
    return jnp.sum(v, axis=0, keepdims=True)


def _sigmoid(v):
    return 0.5 * jnp.tanh(0.5 * v) + 0.5


def _rms_fwd(xv, g):
    r = lax.rsqrt(jnp.mean(xv * xv, axis=-1, keepdims=True) + RMS_EPS)
    return xv * r * g, r


def _rms_bwd(xv, g, dy):
    r = lax.rsqrt(jnp.mean(xv * xv, axis=-1, keepdims=True) + RMS_EPS)
    xn = xv * r
    dxn = dy * g
    dx = r * (dxn - xn * jnp.mean(dxn * xn, axis=-1, keepdims=True))
    return dx, dy * xn


def _dot(a, b):
    return jnp.dot(a, b, preferred_element_type=F32)


def _dot_nt(a, b):
    return lax.dot_general(a, b, (((1,), (1,)), ((), ())), preferred_element_type=F32)


def _dot_tn(a, b):
    return lax.dot_general(a, b, (((0,), (0,)), ((), ())), preferred_element_type=F32)


def _perm(dil, n, inverse=False):
    i = lax.broadcasted_iota(jnp.int32, (n, n), 0)
    j = lax.broadcasted_iota(jnp.int32, (n, n), 1)
    if inverse:
        i, j = j, i
    per = n // dil
    return (j == (i % per) * dil + i // per).astype(BF16)


def _permute_rows(pm, v):
    if v.dtype == BF16:
        return _dot(pm, v).astype(BF16)
    hi = v.astype(BF16)
    lo = (v - hi.astype(F32)).astype(BF16)
    return _dot(pm, hi) + _dot(pm, lo)


def _stream_view(a, dil):
    t, c = a.shape
    return a.reshape(dil, t // dil, c)


def _stream_spec(dil, tm, c):
    return pl.BlockSpec((dil, tm // dil, c), lambda m: (0, m, 0))


def _load_streams(ref, dil, tm):
    c = ref.shape[-1]
    if dil == 1:
        return ref[...].reshape(tm, c)
    sub = min(PERM_TOKENS, tm)
    pm = _perm(dil, sub, inverse=True)
    parts = [_permute_rows(pm, ref[:, i * (sub // dil):(i + 1) * (sub // dil), :].reshape(sub, c))
             for i in range(tm // sub)]
    return parts[0] if len(parts) == 1 else jnp.concatenate(parts, axis=0)


def _store_streams(ref, dil, tm, v):
    if dil == 1:
        ref[...] = v.reshape(ref.shape).astype(ref.dtype)
        return
    sub = min(PERM_TOKENS, tm)
    pm = _perm(dil, sub)
    for i in range(tm // sub):
        piece = _permute_rows(pm, v[i * sub:(i + 1) * sub])
        ref[:, i * (sub // dil):(i + 1) * (sub // dil), :] = piece.reshape(dil, sub // dil, -1).astype(ref.dtype)


ANY = pl.BlockSpec(memory_space=pl.ANY)


def _mesh_pos():
    return lax.axis_index("x"), lax.axis_index("y"), lax.axis_index("c")


def _dev_index(px, py, pc):
    return 4 * px + 2 * py + pc


class _Exchange:
    def __init__(self, mode, arrays, rows=None, into=()):
        self.mode, self.arrays, self.rows, self.into = mode, list(arrays), rows, list(into)
        n = len(self.arrays)
        if mode == "gather":
            self.out_shape = [jax.ShapeDtypeStruct((N_DEV,) + a.shape, a.dtype) for a in self.arrays]
        else:
            self.out_shape = [jax.ShapeDtypeStruct(a.shape, a.dtype) for a in self.arrays]
        self.scratch = [pltpu.SemaphoreType.DMA((n, N_DEV - 1)), pltpu.SemaphoreType.DMA((n, N_DEV - 1)),
                        pltpu.SemaphoreType.DMA((n,))]

    def _peers(self):
        x, y, c = _mesh_pos()
        flips = [(kx, ky, kc) for kx in (0, 1) for ky in (0, 1) for kc in (0, 1)][1:]
        peers = [(1 - x if kx else x, 1 - y if ky else y, 1 - c if kc else c) for kx, ky, kc in flips]
        return _dev_index(x, y, c), peers

    def _copy(self, ins, outs, sems, i, k, peer, me, sending):
        src = ins[i] if self.mode == "gather" else ins[i].at[_dev_index(*peer)]
        dst = outs[i].at[me if sending else _dev_index(*peer)]
        if self.rows is not None:
            src, dst = src.at[pl.ds(*self.rows)], dst.at[pl.ds(*self.rows)]
        return pltpu.make_async_remote_copy(src_ref=src, dst_ref=dst, send_sem=sems[0].at[i, k],
                                            recv_sem=sems[1].at[i, k], device_id=peer, device_id_type=MESH)

    def _own(self, ins, outs, sems, i, me):
        return pltpu.make_async_copy(ins[i], outs[i].at[me], sems[2].at[i])

    def start(self, ins, outs, sems):
        me, peers = self._peers()
        for i in range(len(ins)):
            if self.mode == "gather":
                self._own(ins, outs, sems, i, me).start()
            for k, peer in enumerate(peers):
                self._copy(ins, outs, sems, i, k, peer, me, True).start()

    def wait(self, ins, outs, sems):
        me, peers = self._peers()
        for i in range(len(ins)):
            for k, peer in enumerate(peers):
                self._copy(ins, outs, sems, i, k, peer, me, False).wait_recv()
            for k, peer in enumerate(peers):
                self._copy(ins, outs, sems, i, k, peer, me, True).wait_send()
            if self.mode == "gather":
                self._own(ins, outs, sems, i, me).wait()


def _call(body, *, name, grid, in_specs, out_specs, out_shape, args, semantics, carry=None, scratch=()):
    if carry is None:
        return pl.pallas_call(body, name=name, grid=grid, in_specs=in_specs, out_specs=out_specs,
                              out_shape=out_shape, scratch_shapes=list(scratch),
                              compiler_params=_params(*semantics))(*args)
    n_in, n_out, n_x, n_s = len(in_specs), len(out_specs), len(carry.arrays), len(scratch)
    n_into = len(carry.into)
    all_in = n_in + n_x + n_into

    def carried(*refs):
        ins, x_ins = refs[:n_in], refs[n_in:n_in + n_x]
        outs = refs[all_in:all_in + n_out]
        x_outs = refs[all_in + n_out:all_in + n_out + n_x]
        own = refs[all_in + n_out + n_x:all_in + n_out + n_x + n_s]
        sems = refs[all_in + n_out + n_x + n_s:]
        first = functools.reduce(jnp.logical_and, [pl.program_id(a) == 0 for a in range(len(grid))])
        last = functools.reduce(jnp.logical_and, [pl.program_id(a) == grid[a] - 1 for a in range(len(grid))])

        @pl.when(first)
        def _():
            carry.start(x_ins, x_outs, sems)

        body(*ins, *outs, *own)

        @pl.when(last)
        def _():
            carry.wait(x_ins, x_outs, sems)

    res = pl.pallas_call(
        carried, name=name, grid=grid, in_specs=list(in_specs) + [ANY] * (n_x + n_into),
        out_specs=list(out_specs) + [ANY] * n_x, out_shape=list(out_shape) + carry.out_shape,
        input_output_aliases={n_in + n_x + i: n_out + i for i in range(n_into)},
        scratch_shapes=list(scratch) + carry.scratch, compiler_params=_params(*["arbitrary"] * len(grid)),
    )(*args, *carry.arrays, *carry.into)
    return list(res[:n_out]), list(res[n_out:])


def _in_proj(x, g, wt, cw, carry=None):
    t, d = x.shape
    n = wt.shape[0]
    tm = min(ROWS_MATMUL, t)
    qkv0 = 3 * cw

    def body(x_ref, g_ref, wt_ref, h_ref, abcv_ref, gates_ref, *s_refs):
        h = _rms_fwd(x_ref[...], g_ref[...])[0].astype(BF16)
        h_ref[...] = h
        abcv_ref[...] = _dot_nt(h, wt_ref[0:qkv0, :]).astype(BF16)
        gates_ref[...] = _dot_nt(h, wt_ref[qkv0 + 3 * ATTN_W:n, :]).astype(BF16)
        for gi, s_ref in enumerate(s_refs):
            cols = [_dot_nt(h, wt_ref[qkv0 + j * ATTN_W + gi * GROUP_W:qkv0 + j * ATTN_W + (gi + 1) * GROUP_W, :])
                    for j in range(3)]
            _store_streams(s_ref, DILATIONS[gi], tm, jnp.concatenate(cols, axis=1).astype(BF16))

    return _call(
        body, name="in_proj", grid=(t // tm,),
        in_specs=[_rows(tm, d), _resident((1, d)), _resident((n, d))],
        out_specs=[_rows(tm, d), _rows(tm, qkv0), _rows(tm, 2 * d)]
        + [_stream_spec(dil, tm, 3 * GROUP_W) for dil in DILATIONS],
        out_shape=[jax.ShapeDtypeStruct((t, d), BF16), jax.ShapeDtypeStruct((t, qkv0), BF16),
                   jax.ShapeDtypeStruct((t, 2 * d), BF16)]
        + [jax.ShapeDtypeStruct((dil, t // dil, 3 * GROUP_W), BF16) for dil in DILATIONS],
        args=(x, g, wt), semantics=("parallel",), carry=carry)


def _head_masks():
    lane = lax.broadcasted_iota(jnp.int32, (1, GROUP_W), 1)
    return lane, [(lane // HEAD_DIM) == h for h in range(HEADS_PER_GROUP)]


def _stack_heads(v, heads):
    return jnp.concatenate([jnp.where(hm, v, jnp.zeros_like(v)) for hm in heads], axis=0)


def _merge_heads(v, heads):
    out = jnp.zeros((QBLK, GROUP_W), v.dtype)
    for h, hm in enumerate(heads):
        out = jnp.where(hm, v[h * QBLK:(h + 1) * QBLK], out)
    return out


def _pair_block(col):
    return pl.BlockSpec((2 * QBLK, GROUP_W), lambda b: (b, col))


def _edge_block(col, shift, nb):
    return pl.BlockSpec((QBLK, GROUP_W), lambda b: (jnp.clip(2 * b + shift, 0, nb - 1), col))


def _band_mask(has_prev):
    rows = HEADS_PER_GROUP * QBLK
    row = lax.broadcasted_iota(jnp.int32, (rows, 2 * QBLK), 0) & (QBLK - 1)
    col = lax.broadcasted_iota(jnp.int32, (rows, 2 * QBLK), 1)
    return ((col < QBLK) & (col >= row) & has_prev) | ((col >= QBLK) & (col - QBLK <= row))


def _next_mask(has_next):
    rows = HEADS_PER_GROUP * QBLK
    row = lax.broadcasted_iota(jnp.int32, (rows, QBLK), 0) & (QBLK - 1)
    col = lax.broadcasted_iota(jnp.int32, (rows, QBLK), 1)
    return (col >= row) & has_next


def _attn_fwd(s, dil, carry=None):
    t = s.shape[0] * s.shape[1]
    nb = t // QBLK
    per_stream = nb // dil
    assert per_stream % 2 == 0

    def body(q_ref, kc_ref, kp_ref, vc_ref, vp_ref, o_ref, lse_ref):
        b = pl.program_id(0)
        _, heads = _head_masks()
        first_has_prev = lax.rem(2 * b, per_stream) != 0
        for j in range(2):
            rows = slice(j * QBLK, (j + 1) * QBLK)
            if j == 0:
                k2 = jnp.concatenate([kp_ref[...], kc_ref[rows, :]], axis=0)
                v2 = jnp.concatenate([vp_ref[...], vc_ref[rows, :]], axis=0)
            else:
                k2, v2 = kc_ref[...], vc_ref[...]
            mask = _band_mask(first_has_prev if j == 0 else True)
            sc = jnp.where(mask, _dot_nt(_stack_heads(q_ref[rows, :], heads), k2) * ATTN_SCALE, NEG_INF)
            mx = jnp.max(sc, axis=1, keepdims=True)
            pr = jnp.exp(sc - mx)
            den = jnp.sum(pr, axis=1, keepdims=True)
            o_all = _dot(pr.astype(BF16), v2) / den
            o_ref[rows, :] = _merge_heads(o_all, heads).astype(BF16)
            lse_ref[rows, :] = _merge_heads(jnp.broadcast_to(mx + jnp.log(den), o_all.shape), heads)

    sv = s.reshape(t, 3 * GROUP_W)
    return _call(
        body, name=f"attn_fwd_d{dil}", grid=(nb // 2,),
        in_specs=[_pair_block(0), _pair_block(1), _edge_block(1, -1, nb), _pair_block(2), _edge_block(2, -1, nb)],
        out_specs=[_pair_block(0), _pair_block(0)],
        out_shape=[jax.ShapeDtypeStruct((t, GROUP_W), BF16), jax.ShapeDtypeStruct((t, GROUP_W), F32)],
        args=(sv, sv, sv, sv, sv), semantics=("parallel",), carry=carry)


def _group_softmax(parts):
    mx = jnp.maximum(jnp.maximum(parts[0], parts[1]), parts[2])
    es = [jnp.exp(p - mx) for p in parts]
    den = es[0] + es[1] + es[2]
    return [e / den for e in es]


def _mixer_out(x, abcv, gates, os, lses, conv_w, conv_b, b_gate, w_pa, w_pb, w_o, carry=None):
    t, d = x.shape
    cw = conv_w.shape[1]
    tm = min(ROWS_MATMUL, t)

    def body(x_ref, abcv_ref, halo_ref, gates_ref, o0_ref, o1_ref, o2_ref, l0_ref, l1_ref, l2_ref, cw_ref, cb_ref,
             bg_ref, wpa_ref, wpb_ref, wo_ref, x1_ref, ya_ref, yb_ref, yap_ref, ybp_ref, mg_ref):
        m = pl.program_id(0)
        ab = abcv_ref[:, 0:cw].astype(F32)
        u = abcv_ref[:, cw:2 * cw].astype(F32) * abcv_ref[:, 2 * cw:3 * cw].astype(F32)
        hu = halo_ref[:, cw:2 * cw].astype(F32) * halo_ref[:, 2 * cw:3 * cw].astype(F32)
        hu = jnp.where(m > 0, hu, 0.0)
        cv = (cw_ref[0:1, :] * _shift_down(u, hu, 2) + cw_ref[1:2, :] * _shift_down(u, hu, 1)
              + cw_ref[2:3, :] * u + cb_ref[...])
        ya = (ab * cv).astype(BF16)
        ya_ref[...] = ya
        alphas = _group_softmax([_load_streams(r, dil, tm) for r, dil in zip((l0_ref, l1_ref, l2_ref), DILATIONS)])
        for i, (o_ref, dil) in enumerate(zip((o0_ref, o1_ref, o2_ref), DILATIONS)):
            sl = slice(i * GROUP_W, (i + 1) * GROUP_W)
            yb_ref[:, sl] = (alphas[i] * _load_streams(o_ref, dil, tm).astype(F32)).astype(BF16)
        yap = _dot_nt(ya, wpa_ref[...])
        ybp = _dot_nt(yb_ref[...], wpb_ref[...])
        yap_ref[...] = yap.astype(BF16)
        ybp_ref[...] = ybp.astype(BF16)
        sa = _sigmoid(gates_ref[:, 0:d].astype(F32) + bg_ref[0:1, :])
        sb = _sigmoid(gates_ref[:, d:2 * d].astype(F32) + bg_ref[1:2, :])
        merged = (sa * yap + sb * ybp).astype(BF16)
        mg_ref[...] = merged
        x1_ref[...] = x_ref[...] + _dot(merged, wo_ref[...])

    return _call(
        body, name="mixer_out", grid=(t // tm,),
        in_specs=[_rows(tm, d), _rows(tm, 3 * cw), _prev_halo(tm, 3 * cw), _rows(tm, 2 * d)]
        + [_stream_spec(dil, tm, GROUP_W) for dil in DILATIONS] * 2
        + [_resident((3, cw)), _resident((1, cw)), _resident((2, d)),
           _resident((d, cw)), _resident((d, ATTN_W)), _resident((d, d))],
        out_specs=[_rows(tm, d), _rows(tm, cw), _rows(tm, ATTN_W), _rows(tm, d), _rows(tm, d), _rows(tm, d)],
        out_shape=[jax.ShapeDtypeStruct((t, d), F32), jax.ShapeDtypeStruct((t, cw), BF16),
                   jax.ShapeDtypeStruct((t, ATTN_W), BF16), jax.ShapeDtypeStruct((t, d), BF16),
                   jax.ShapeDtypeStruct((t, d), BF16), jax.ShapeDtypeStruct((t, d), BF16)],
        args=(x, abcv, abcv, gates, *[_stream_view(a, dil) for a, dil in zip(os, DILATIONS)],
              *[_stream_view(a, dil) for a, dil in zip(lses, DILATIONS)], conv_w, conv_b, b_gate, w_pa, w_pb, w_o),
        semantics=("parallel",), carry=carry)


def _ffn_fwd(x1, target, g2, w_ut, conv_w, conv_b, w_d, g_f, carry=None):
    t, d = x1.shape
    dff = w_d.shape[0]
    tm = min(256, t)
    ck = _pick_tile(dff, 1408)

    def body(x1_ref, tg_ref, g2_ref, wut_ref, cw_ref, cb_ref, wd_ref, gf_ref, h2_ref, up_ref, act_ref, conv_ref,
             dx2_ref, dx2i_ref, acc_ref, loss_ref, halo_ref):
        m = pl.program_id(0)

        @pl.when(m == 0)
        def _():
            acc_ref[...] = jnp.zeros_like(acc_ref)
            loss_ref[...] = jnp.zeros_like(loss_ref)
            halo_ref[...] = jnp.zeros_like(halo_ref)

        h2 = _permute_rows(_interleave(tm), _rms_fwd(x1_ref[...], g2_ref[...])[0].astype(BF16))
        h2_ref[...] = h2

        def conv(c0):
            p = _dot_nt(h2, wut_ref[c0:c0 + ck, :])
            up_ref[:, c0:c0 + ck] = p.astype(BF16)
            hp = halo_ref[:, c0:c0 + ck]
            halo_ref[:, c0:c0 + ck] = p[tm - HALO:, :]
            return (cw_ref[0:1, c0:c0 + ck] * _shift_down_il(p, hp, 2)
                    + cw_ref[1:2, c0:c0 + ck] * _shift_down_il(p, hp, 1)
                    + cw_ref[2:3, c0:c0 + ck] * p + cb_ref[:, c0:c0 + ck])

        down = jnp.zeros((tm, d), F32)
        for c0 in range(0, dff, ck):
            gate = conv(c0)
            val = conv(dff + c0)
            conv_ref[:, c0:c0 + ck] = gate.astype(BF16)
            conv_ref[:, dff + c0:dff + c0 + ck] = val.astype(BF16)
            act = (gate * _sigmoid(gate) * val).astype(BF16)
            act_ref[:, c0:c0 + ck] = act
            down = down + _dot(act, wd_ref[c0:c0 + ck, :])
        x2 = x1_ref[...] + _permute_rows(_interleave(tm, inverse=True), down)
        y, _ = _rms_fwd(x2, gf_ref[...])
        diff = y - tg_ref[...]
        loss_ref[...] += 0.5 * jnp.sum(jnp.mean(diff * diff, axis=-1, keepdims=True))
        dx2, dg = _rms_bwd(x2, gf_ref[...], diff * (1.0 / d))
        dx2_ref[...] = dx2
        dx2i_ref[...] = _permute_rows(_interleave(tm), dx2.astype(BF16))
        acc_ref[...] += _stack_rows([_colsum(dg)], d)

    return _call(
        body, name="ffn_fwd", grid=(t // tm,),
        in_specs=[_rows(tm, d), _rows(tm, d), _resident((1, d)), _resident((2 * dff, d)), _resident((3, 2 * dff)),
                  _resident((1, 2 * dff)), _resident((dff, d)), _resident((1, d))],
        out_specs=[_rows(tm, d), _rows(tm, 2 * dff), _rows(tm, dff), _rows(tm, 2 * dff), _rows(tm, d), _rows(tm, d),
                   _acc_spec(d), _acc_spec(LANES)],
        out_shape=[jax.ShapeDtypeStruct((t, d), BF16), jax.ShapeDtypeStruct((t, 2 * dff), BF16),
                   jax.ShapeDtypeStruct((t, dff), BF16), jax.ShapeDtypeStruct((t, 2 * dff), BF16),
                   jax.ShapeDtypeStruct((t, d), F32), jax.ShapeDtypeStruct((t, d), BF16),
                   jax.ShapeDtypeStruct((SUBLANES, d), F32), jax.ShapeDtypeStruct((SUBLANES, LANES), F32)],
        args=(x1, target, g2, w_ut, conv_w, conv_b, w_d, g_f), semantics=("arbitrary",), carry=carry,
        scratch=[pltpu.VMEM((HALO, 2 * dff), F32)])


def _ffn_act_bwd(dx2, conv, w_d):
    t, d = dx2.shape
    dff = w_d.shape[0]
    tm = min(256, t)
    ck = _pick_tile(dff, 1408)

    def body(dx2_ref, conv_ref, wd_ref, dup_ref, acc_ref):
        m = pl.program_id(0)

        @pl.when(m == 0)
        def _():
            acc_ref[...] = jnp.zeros_like(acc_ref)

        dx2v = dx2_ref[...]
        for c0 in range(0, dff, ck):
            dact = _dot_nt(dx2v, wd_ref[c0:c0 + ck, :])
            gate = conv_ref[:, c0:c0 + ck].astype(F32)
            val = conv_ref[:, dff + c0:dff + c0 + ck].astype(F32)
            sg = _sigmoid(gate)
            dval = dact * gate * sg
            dgate = dact * val * sg * (1.0 + gate * (1.0 - sg))
            dup_ref[:, c0:c0 + ck] = dgate.astype(BF16)
            dup_ref[:, dff + c0:dff + c0 + ck] = dval.astype(BF16)
            acc_ref[:, c0:c0 + ck] += _stack_rows([_colsum(dgate)], ck)
            acc_ref[:, dff + c0:dff + c0 + ck] += _stack_rows([_colsum(dval)], ck)

    return pl.pallas_call(
        body, name="ffn_act_bwd", grid=(t // tm,),
        in_specs=[_rows(tm, d), _rows(tm, 2 * dff), _resident((dff, d))],
        out_specs=[_rows(tm, 2 * dff), _acc_spec(2 * dff)],
        out_shape=[jax.ShapeDtypeStruct((t, 2 * dff), BF16), jax.ShapeDtypeStruct((SUBLANES, 2 * dff), F32)],
        compiler_params=_params("arbitrary"),
    )(dx2, conv, w_d)


def _ffn_up_bwd(dup, up_pre, x1, dx2, conv_w, w_u, g2, carry=None):
    t, d = x1.shape
    n = dup.shape[1]
    tm = min(256, t)
    ck = _pick_tile(n, 1408)
    last = t // tm - 1

    def body(dup_ref, nxt_ref, up_ref, x1_ref, dx2_ref, cw_ref, wu_ref, g2_ref, dpre_ref, dx1_ref, acc_ref, accw_ref):
        m = pl.program_id(0)

        @pl.when(m == 0)
        def _():
            acc_ref[...] = jnp.zeros_like(acc_ref)
            accw_ref[...] = jnp.zeros_like(accw_ref)

        dh = jnp.zeros((tm, d), F32)
        for c0 in range(0, n, ck):
            du = dup_ref[:, c0:c0 + ck].astype(F32)
            hn = jnp.where(m < last, nxt_ref[:, c0:c0 + ck].astype(F32), 0.0)
            du1 = _shift_up_il(du, hn, 1)
            du2 = _shift_up_il(du, hn, 2)
            dpre = (cw_ref[2:3, c0:c0 + ck] * du + cw_ref[1:2, c0:c0 + ck] * du1
                    + cw_ref[0:1, c0:c0 + ck] * du2).astype(BF16)
            dpre_ref[:, c0:c0 + ck] = dpre
            dh = dh + _dot(dpre, wu_ref[c0:c0 + ck, :])
            p = up_ref[:, c0:c0 + ck].astype(F32)
            accw_ref[:, c0:c0 + ck] += _stack_rows([_colsum(du2 * p), _colsum(du1 * p), _colsum(du * p)], ck)
        dh = _permute_rows(_interleave(tm, inverse=True), dh)
        dx, dg = _rms_bwd(x1_ref[...], g2_ref[...], dh)
        dx1_ref[...] = dx2_ref[...] + dx
        acc_ref[...] += _stack_rows([_colsum(dg)], d)

    return _call(
        body, name="ffn_up_bwd", grid=(t // tm,),
        in_specs=[_rows(tm, n), _next_halo(tm, n, t), _rows(tm, n), _rows(tm, d), _rows(tm, d), _resident((3, n)),
                  _resident((n, d)), _resident((1, d))],
        out_specs=[_rows(tm, n), _rows(tm, d), _acc_spec(d), _acc_spec(n)],
        out_shape=[jax.ShapeDtypeStruct((t, n), BF16), jax.ShapeDtypeStruct((t, d), F32),
                   jax.ShapeDtypeStruct((SUBLANES, d), F32), jax.ShapeDtypeStruct((SUBLANES, n), F32)],
        args=(dup, dup, up_pre, x1, dx2, conv_w, w_u, g2), semantics=("arbitrary",), carry=carry)


def _tn_matmul(a, b, name):
    t, mdim = a.shape
    n = b.shape[1]
    tk = min(1024, t)
    tmm = _pick_tile(mdim, 1536)
    tn = _pick_tile(n, 1024)

    def body(a_ref, b_ref, o_ref, acc_ref):
        k = pl.program_id(2)

        @pl.when(k == 0)
        def _():
            acc_ref[...] = jnp.zeros_like(acc_ref)

        acc_ref[...] += _dot_tn(a_ref[...].astype(BF16), b_ref[...].astype(BF16))

        @pl.when(k == t // tk - 1)
        def _():
            o_ref[...] = acc_ref[...].astype(BF16)

    return pl.pallas_call(
        body, name=name, grid=(mdim // tmm, n // tn, t // tk),
        in_specs=[pl.BlockSpec((tk, tmm), lambda i, j, k: (k, i)), pl.BlockSpec((tk, tn), lambda i, j, k: (k, j))],
        out_specs=pl.BlockSpec((tmm, tn), lambda i, j, k: (i, j)),
        out_shape=jax.ShapeDtypeStruct((mdim, n), BF16),
        scratch_shapes=[pltpu.VMEM((tmm, tn), F32)],
        compiler_params=_params("parallel", "parallel", "arbitrary"),
    )(a, b)


def _mixer_bwd(dx1, gates, yap, ybp, os, lses, b_gate, w_o, w_pa, w_pb):
    t, d = dx1.shape
    cw = w_pa.shape[1]
    tm = min(ROWS_MATMUL, t)

    def body(dx1_ref, gates_ref, yap_ref, ybp_ref, o0_ref, o1_ref, o2_ref, l0_ref, l1_ref, l2_ref, bg_ref, wo_ref,
             wpa_ref, wpb_ref, dgates_ref, dyap_ref, dybp_ref, dya_ref, do0_ref, do1_ref, do2_ref, dl0_ref, dl1_ref,
             dl2_ref, acc_ref):
        m = pl.program_id(0)

        @pl.when(m == 0)
        def _():
            acc_ref[...] = jnp.zeros_like(acc_ref)

        dmg = _dot_nt(dx1_ref[...].astype(BF16), wo_ref[...])
        sa = _sigmoid(gates_ref[:, 0:d].astype(F32) + bg_ref[0:1, :])
        sb = _sigmoid(gates_ref[:, d:2 * d].astype(F32) + bg_ref[1:2, :])
        dyap = (dmg * sa).astype(BF16)
        dybp = (dmg * sb).astype(BF16)
        dga = dmg * yap_ref[...].astype(F32) * sa * (1.0 - sa)
        dgb = dmg * ybp_ref[...].astype(F32) * sb * (1.0 - sb)
        dyap_ref[...] = dyap
        dybp_ref[...] = dybp
        dgates_ref[:, 0:d] = dga.astype(BF16)
        dgates_ref[:, d:2 * d] = dgb.astype(BF16)
        acc_ref[...] += _stack_rows([_colsum(dga), _colsum(dgb)], d)
        dya_ref[...] = _dot(dyap, wpa_ref[...]).astype(BF16)
        dyb = _dot(dybp, wpb_ref[...])

        ri = lax.broadcasted_iota(jnp.int32, (GROUP_W, GROUP_W), 0) // HEAD_DIM
        ci = lax.broadcasted_iota(jnp.int32, (GROUP_W, GROUP_W), 1) // HEAD_DIM
        same_head = (ri == ci).astype(BF16)
        alphas = _group_softmax([_load_streams(r, dil, tm) for r, dil in zip((l0_ref, l1_ref, l2_ref), DILATIONS)])
        prod = jnp.zeros((tm, GROUP_W), F32)
        for i, (o_ref, do_ref, dil) in enumerate(zip((o0_ref, o1_ref, o2_ref), (do0_ref, do1_ref, do2_ref), DILATIONS)):
            dov = alphas[i] * dyb[:, i * GROUP_W:(i + 1) * GROUP_W]
            _store_streams(do_ref, dil, tm, dov.astype(BF16))
            prod = prod + dov * _load_streams(o_ref, dil, tm).astype(F32)
        hi = prod.astype(BF16)
        lo = (prod - hi.astype(F32)).astype(BF16)
        dtot = _dot(hi, same_head) + _dot(lo, same_head)
        for alpha, dl_ref, dil in zip(alphas, (dl0_ref, dl1_ref, dl2_ref), DILATIONS):
            _store_streams(dl_ref, dil, tm, alpha * dtot)

    streams = [_stream_spec(dil, tm, GROUP_W) for dil in DILATIONS]
    res = _call(
        body, name="mixer_bwd", grid=(t // tm,),
        in_specs=[_rows(tm, d), _rows(tm, 2 * d), _rows(tm, d), _rows(tm, d)] + streams * 2
        + [_resident((2, d)), _resident((d, d)), _resident((d, cw)), _resident((d, ATTN_W))],
        out_specs=[_rows(tm, 2 * d), _rows(tm, d), _rows(tm, d), _rows(tm, cw)] + streams * 2 + [_acc_spec(d)],
        out_shape=[jax.ShapeDtypeStruct((t, 2 * d), BF16), jax.ShapeDtypeStruct((t, d), BF16),
                   jax.ShapeDtypeStruct((t, d), BF16), jax.ShapeDtypeStruct((t, cw), BF16)]
        + [jax.ShapeDtypeStruct((dil, t // dil, GROUP_W), BF16) for dil in DILATIONS]
        + [jax.ShapeDtypeStruct((dil, t // dil, GROUP_W), F32) for dil in DILATIONS]
        + [jax.ShapeDtypeStruct((SUBLANES, d), F32)],
        args=(dx1, gates, yap, ybp, *[_stream_view(a, dil) for a, dil in zip(os, DILATIONS)],
              *[_stream_view(a, dil) for a, dil in zip(lses, DILATIONS)], b_gate, w_o, w_pa, w_pb),
        semantics=("arbitrary",))
    dgates, dyap, dybp, dya = res[:4]
    dos = [a.reshape(t, GROUP_W) for a in res[4:7]]
    dls = [a.reshape(t, GROUP_W) for a in res[7:10]]
    return dgates, dyap, dybp, dya, dos, dls, res[10]


def _attn_bwd(s, do, lse, dl, dil, carry=None):
    t = s.shape[0] * s.shape[1]
    nb = t // QBLK
    per_stream = nb // dil

    def body(q_ref, qn_ref, kc_ref, kp_ref, vc_ref, vp_ref, do_ref, don_ref, lse_ref, lsen_ref, dl_ref, dln_ref,
             ds_ref):
        b = pl.program_id(0)
        lane, heads = _head_masks()
        first_has_prev = lax.rem(2 * b, per_stream) != 0
        last_has_next = lax.rem(2 * b + 2, per_stream) != 0

        def cols(v):
            return jnp.concatenate([jnp.sum(jnp.where(lane == h * HEAD_DIM, v, 0.0), axis=1, keepdims=True)
                                    for h in range(HEADS_PER_GROUP)], axis=0)

        def pair(qs, dos, k, v, valid, lse_c, dl_c):
            s = jnp.where(valid, _dot_nt(qs, k) * ATTN_SCALE, NEG_INF)
            p = jnp.exp(s - lse_c)
            ds = p * (_dot_nt(dos, v) - dl_c)
            return p.astype(BF16), ds.astype(BF16)

        lo, hi = slice(0, QBLK), slice(QBLK, 2 * QBLK)
        for j, rows in enumerate((lo, hi)):
            q, do, lse, dl = q_ref[rows, :], do_ref[rows, :], lse_ref[rows, :], dl_ref[rows, :]
            kc, vc = kc_ref[rows, :], vc_ref[rows, :]
            if j == 0:
                k2 = jnp.concatenate([kp_ref[...], kc], axis=0)
                v2 = jnp.concatenate([vp_ref[...], vc], axis=0)
                qn, don, lsen, dln = q_ref[hi, :], do_ref[hi, :], lse_ref[hi, :], dl_ref[hi, :]
                mask, mask_n = _band_mask(first_has_prev), _next_mask(True)
            else:
                k2, v2 = kc_ref[...], vc_ref[...]
                qn, don, lsen, dln = qn_ref[...], don_ref[...], lsen_ref[...], dln_ref[...]
                mask, mask_n = _band_mask(True), _next_mask(last_has_next)
            qs, qns = _stack_heads(q, heads), _stack_heads(qn, heads)
            dos, dons = _stack_heads(do, heads), _stack_heads(don, heads)
            p_q, ds_q = pair(qs, dos, k2, v2, mask, cols(lse), cols(dl))
            p_n, ds_n = pair(qns, dons, kc, vc, mask_n, cols(lsen), cols(dln))
            dq = _merge_heads(_dot(ds_q, k2), heads)
            dk = _dot_tn(jnp.concatenate([ds_q[:, QBLK:], ds_n], axis=0), jnp.concatenate([qs, qns], axis=0))
            dv = _dot_tn(jnp.concatenate([p_q[:, QBLK:], p_n], axis=0), jnp.concatenate([dos, dons], axis=0))
            ds_ref[rows, 0:GROUP_W] = (dq * ATTN_SCALE).astype(BF16)
            ds_ref[rows, GROUP_W:2 * GROUP_W] = (dk * ATTN_SCALE).astype(BF16)
            ds_ref[rows, 2 * GROUP_W:3 * GROUP_W] = dv.astype(BF16)

    sv = s.reshape(t, 3 * GROUP_W)
    cur, nxt = _pair_block(0), _edge_block(0, 2, nb)
    return _call(
        body, name=f"attn_bwd_d{dil}", grid=(nb // 2,),
        in_specs=[cur, nxt, _pair_block(1), _edge_block(1, -1, nb), _pair_block(2), _edge_block(2, -1, nb),
                  cur, nxt, cur, nxt, cur, nxt],
        out_specs=[pl.BlockSpec((2 * QBLK, 3 * GROUP_W), lambda b: (b, 0))],
        out_shape=[jax.ShapeDtypeStruct((t, 3 * GROUP_W), BF16)],
        args=(sv, sv, sv, sv, sv, sv, do, do, lse, lse, dl, dl), semantics=("parallel",), carry=carry)


def _conv_mixer_bwd(abcv, dya, conv_w, conv_b):
    t = abcv.shape[0]
    cw = conv_w.shape[1]
    tm = min(256, t)
    last = t // tm - 1

    def body(a_ref, ap_ref, an_ref, dya_ref, dyan_ref, cw_ref, cb_ref, d_ref, acc_ref):
        m = pl.program_id(0)

        @pl.when(m == 0)
        def _():
            acc_ref[...] = jnp.zeros_like(acc_ref)

        ab = a_ref[:, 0:cw].astype(F32)
        ac = a_ref[:, cw:2 * cw].astype(F32)
        av = a_ref[:, 2 * cw:3 * cw].astype(F32)
        u = ac * av
        hu = ap_ref[:, cw:2 * cw].astype(F32) * ap_ref[:, 2 * cw:3 * cw].astype(F32)
        hu = jnp.where(m > 0, hu, 0.0)
        u1 = _shift_down(u, hu, 1)
        u2 = _shift_down(u, hu, 2)
        cv = cw_ref[0:1, :] * u2 + cw_ref[1:2, :] * u1 + cw_ref[2:3, :] * u + cb_ref[...]
        dya_v = dya_ref[...].astype(F32)
        dcv = dya_v * ab
        ndcv = jnp.where(m < last, dyan_ref[...].astype(F32) * an_ref[:, 0:cw].astype(F32), 0.0)
        du = (cw_ref[2:3, :] * dcv + cw_ref[1:2, :] * _shift_up(dcv, ndcv, 1)
              + cw_ref[0:1, :] * _shift_up(dcv, ndcv, 2))
        d_ref[:, 0:cw] = (dya_v * cv).astype(BF16)
        d_ref[:, cw:2 * cw] = (du * av).astype(BF16)
        d_ref[:, 2 * cw:3 * cw] = (du * ac).astype(BF16)
        acc_ref[...] += _stack_rows([_colsum(dcv * u2), _colsum(dcv * u1), _colsum(dcv * u), _colsum(dcv)], cw)

    return pl.pallas_call(
        body, name="conv_mixer_bwd", grid=(t // tm,),
        in_specs=[_rows(tm, 3 * cw), _prev_halo(tm, 3 * cw), _next_halo(tm, 3 * cw, t), _rows(tm, cw),
                  _next_halo(tm, cw, t), _resident((3, cw)), _resident((1, cw))],
        out_specs=[_rows(tm, 3 * cw), _acc_spec(cw)],
        out_shape=[jax.ShapeDtypeStruct((t, 3 * cw), BF16), jax.ShapeDtypeStruct((SUBLANES, cw), F32)],
        compiler_params=_params("arbitrary"),
    )(abcv, abcv, abcv, dya, dya, conv_w, conv_b)


def _in_proj_bwd(x, dx1, dabcv, dss, dgates, w_in, g1, carry=None):
    t, d = x.shape
    qkv0 = dabcv.shape[1]
    n = w_in.shape[0]
    tm = min(ROWS_MATMUL, t)

    def body(x_ref, dx1_ref, da_ref, ds0_ref, ds1_ref, ds2_ref, dg_ref, w_ref, g_ref, dx_ref, acc_ref):
        m = pl.program_id(0)

        @pl.when(m == 0)
        def _():
            acc_ref[...] = jnp.zeros_like(acc_ref)

        dh = _dot(da_ref[...], w_ref[0:qkv0, :]) + _dot(dg_ref[...], w_ref[qkv0 + 3 * ATTN_W:n, :])
        for gi, (ds_ref, dil) in enumerate(zip((ds0_ref, ds1_ref, ds2_ref), DILATIONS)):
            ds = _load_streams(ds_ref, dil, tm)
            for j in range(3):
                c0 = qkv0 + j * ATTN_W + gi * GROUP_W
                dh = dh + _dot(ds[:, j * GROUP_W:(j + 1) * GROUP_W], w_ref[c0:c0 + GROUP_W, :])
        dx, dg = _rms_bwd(x_ref[...], g_ref[...], dh)
        dx_ref[...] = dx1_ref[...] + dx
        acc_ref[...] += _stack_rows([_colsum(dg)], d)

    return _call(
        body, name="in_proj_bwd", grid=(t // tm,),
        in_specs=[_rows(tm, d), _rows(tm, d), _rows(tm, qkv0)]
        + [_stream_spec(dil, tm, 3 * GROUP_W) for dil in DILATIONS]
        + [_rows(tm, 2 * d), _resident((n, d)), _resident((1, d))],
        out_specs=[_rows(tm, d), _acc_spec(d)],
        out_shape=[jax.ShapeDtypeStruct((t, d), F32), jax.ShapeDtypeStruct((SUBLANES, d), F32)],
        args=(x, dx1, dabcv, *[_stream_view(a, dil) for a, dil in zip(dss, DILATIONS)], dgates, w_in, g1),
        semantics=("arbitrary",), carry=carry)


def _dw_in_qkv(ds, h, dil):
    t, d = h.shape
    tk = min(1024, t)
    sub = min(256, t)
    width = 3 * GROUP_W

    def body(ds_ref, h_ref, o_ref, acc_ref):
        k = pl.program_id(0)

        @pl.when(k == 0)
        def _():
            acc_ref[...] = jnp.zeros_like(acc_ref)

        upd = None
        for i in range(tk // sub):
            rows = ds_ref[:, i * (sub // dil):(i + 1) * (sub // dil), :].reshape(sub, width)
            if dil > 1:
                rows = _permute_rows(_perm(dil, sub, inverse=True), rows)
            term = _dot_tn(rows, h_ref[i * sub:(i + 1) * sub, :])
            upd = term if upd is None else upd + term
        acc_ref[...] += upd

        @pl.when(k == t // tk - 1)
        def _():
            o_ref[...] = acc_ref[...].astype(BF16)

    return pl.pallas_call(
        body, name=f"dw_in_qkv_d{dil}", grid=(t // tk,),
        in_specs=[_stream_spec(dil, tk, width), _rows(tk, d)],
        out_specs=pl.BlockSpec((width, d), lambda k: (0, 0)),
        out_shape=jax.ShapeDtypeStruct((width, d), BF16),
        scratch_shapes=[pltpu.VMEM((width, d), F32)],
        compiler_params=_params("arbitrary"),
    )(_stream_view(ds, dil), h)


def _local_step(x, target, p, late):
    cw = p["conv_a_w"].shape[1]
    (h, abcv, gates, *ss), (g_up,) = _in_proj(x, p["norm_mix_g"], p["w_in"], cw,
                                              carry=_Exchange("gather", [late["w_up"]]))
    w_up = _full_from_gathered(g_up)
    (o0, lse0), g_proj = _attn_fwd(ss[0], DILATIONS[0],
                                   carry=_Exchange("gather", [late["w_proj_a"], late["w_proj_b"]]))
    (o1, lse1), (g_out,) = _attn_fwd(ss[1], DILATIONS[1], carry=_Exchange("gather", [late["w_out"]]))
    o2, lse2 = _attn_fwd(ss[2], DILATIONS[2])
    w_pa, w_pb, w_out = [_full_from_gathered(g) for g in (*g_proj, g_out)]
    os, lses = (o0, o1, o2), (lse0, lse1, lse2)
    (x1, ya, yb, yap, ybp, merged), (g_down,) = _mixer_out(
        x, abcv, gates, os, lses, p["conv_a_w"], p["conv_a_b"], p["b_gate"], w_pa, w_pb, w_out,
        carry=_Exchange("gather", [late["w_down"]]))
    w_down = _full_from_gathered(g_down)
    h2, up_pre, act, conv, dx2, dx2i, acc_gf, loss = _ffn_fwd(x1, target, p["norm_ffn_g"], w_up, p["ffn_conv_w"],
                                                              p["ffn_conv_b"], w_down, p["final_norm_g"])

    parts, got = {}, {}
    dup, acc_fb = _ffn_act_bwd(dx2i, conv, w_down)
    parts["w_down"] = _by_destination(_tn_matmul(act, dx2i, "dw_down"))
    (dpre, dx1, acc_g2, acc_fw), (got["w_down"],) = _ffn_up_bwd(dup, up_pre, x1, dx2, p["ffn_conv_w"], w_up,
                                                                p["norm_ffn_g"],
                                                                carry=_Exchange("scatter", [parts["w_down"]]))
    parts["w_up"] = _by_destination(_tn_matmul(dpre, h2, "dw_up"))
    dgates, dyap, dybp, dya, dos, dls, acc_bg = _mixer_bwd(dx1, gates, yap, ybp, os, lses, p["b_gate"], w_out,
                                                           w_pa, w_pb)
    parts["w_out"] = _by_destination(_tn_matmul(merged, dx1, "dw_out"))
    parts["w_proj_a"] = _by_destination(_tn_matmul(dyap, ya, "dw_proj_a"))
    parts["w_proj_b"] = _by_destination(_tn_matmul(dybp, yb, "dw_proj_b"))
    minor = ("w_out", "w_proj_a", "w_proj_b")
    half = parts["w_up"].shape[1] // 2
    (ds0,), received = _attn_bwd(ss[0], dos[0], lses[0], dls[0], DILATIONS[0],
                                 carry=_Exchange("scatter", [parts[n] for n in minor]))
    got.update(zip(minor, received))
    (ds1,), first_half = _attn_bwd(ss[1], dos[1], lses[1], dls[1], DILATIONS[1],
                                   carry=_Exchange("scatter", [parts["w_up"]], rows=(0, half)))
    (ds2,), (got["w_up"],) = _attn_bwd(ss[2], dos[2], lses[2], dls[2], DILATIONS[2],
                                       carry=_Exchange("scatter", [parts["w_up"]], rows=(half, half),
                                                       into=first_half))
    dss = [ds0, ds1, ds2]
    dabcv, acc_ca = _conv_mixer_bwd(abcv, dya, p["conv_a_w"], p["conv_a_b"])
    dw_s = [_dw_in_qkv(ds, h, dil) for ds, dil in zip(dss, DILATIONS)]
    dw_qkv = [w[j * GROUP_W:(j + 1) * GROUP_W] for j in range(3) for w in dw_s]
    g_w_in = jnp.concatenate([_tn_matmul(dabcv, h, "dw_in_a"), *dw_qkv, _tn_matmul(dgates, h, "dw_in_g")], axis=0)
    parts["w_in"] = _by_destination(g_w_in)
    (dx, acc_g1), (got["w_in"],) = _in_proj_bwd(x, dx1, dabcv, dss, dgates, p["w_in"], p["norm_mix_g"],
                                                carry=_Exchange("scatter", [parts["w_in"]]))
    small = dict(norm_mix_g=acc_g1[0:1], b_gate=acc_bg[0:2], conv_a_w=acc_ca[0:3], conv_a_b=acc_ca[3:4],
                 norm_ffn_g=acc_g2[0:1], ffn_conv_w=acc_fw[0:3], ffn_conv_b=acc_fb[0:1], final_norm_g=acc_gf[0:1])
    return loss[0, 0], dx, parts, got, small


def _all_gather(shards):
    n = len(shards)

    def body(*refs):
        ins, outs = refs[:n], refs[n:2 * n]
        send_sems, recv_sems, local_sems = refs[2 * n:]
        x, y, c = _mesh_pos()
        me, sibling = (x, y, c), (x, y, 1 - c)
        chips = [(1 - x, y), (x, 1 - y), (1 - x, 1 - y)]

        def copy(i, k, block, to, src=None):
            rows = outs[i].at[_dev_index(*block)]
            return pltpu.make_async_remote_copy(
                src_ref=rows if src is None else src, dst_ref=rows, send_sem=send_sems.at[i, k],
                recv_sem=recv_sems.at[i, k], device_id=to, device_id_type=MESH)

        mine, first, passed = [], [], []
        for i in range(n):
            cp = pltpu.make_async_copy(ins[i], outs[i].at[_dev_index(*me)], local_sems.at[i])
            cp.start()
            mine.append(cp)
            first.append(copy(i, 0, me, sibling, src=ins[i]))
            first += [copy(i, 1 + j, me, (*chip, c), src=ins[i]) for j, chip in enumerate(chips)]
        for cp in first:
            cp.start()
        for i in range(n):
            for j, chip in enumerate(chips):
                copy(i, 1 + j, (*chip, c), me).wait_recv()
                fw = copy(i, 4 + j, (*chip, c), sibling)
                fw.start()
                passed.append(fw)
        for i in range(n):
            copy(i, 0, sibling, me).wait_recv()
            for j, chip in enumerate(chips):
                copy(i, 4 + j, (*chip, 1 - c), me).wait_recv()
        for cp in first + passed:
            cp.wait_send()
        for cp in mine:
            cp.wait()

    return pl.pallas_call(
        body, name="all_gather_weights",
        in_specs=[ANY] * n, out_specs=[ANY] * n,
        out_shape=[jax.ShapeDtypeStruct((N_DEV,) + s.shape, s.dtype) for s in shards],
        scratch_shapes=[pltpu.SemaphoreType.DMA((n, 7)), pltpu.SemaphoreType.DMA((n, 7)),
                        pltpu.SemaphoreType.DMA((n,))],
    )(*shards)


def _all_reduce_small(v):
    r = v.shape[0]

    def body(v_ref, o_ref, gath, send_sems, recv_sems):
        x, y, c = _mesh_pos()
        me = _dev_index(x, y, c)
        gath[me] = v_ref[...]
        flips = [(kx, ky, kc) for kx in (0, 1) for ky in (0, 1) for kc in (0, 1)][1:]
        copies = []
        for k, (kx, ky, kc) in enumerate(flips):
            px = 1 - x if kx else x
            py = 1 - y if ky else y
            pc = 1 - c if kc else c
            cp = pltpu.make_async_remote_copy(
                src_ref=v_ref, dst_ref=gath.at[me], send_sem=send_sems.at[k], recv_sem=recv_sems.at[k],
                device_id=(px, py, pc), device_id_type=MESH)
            cp.start()
            copies.append((cp, _dev_index(px, py, pc)))
        for k, (cp, peer) in enumerate(copies):
            pltpu.make_async_remote_copy(
                src_ref=v_ref, dst_ref=gath.at[peer], send_sem=send_sems.at[k], recv_sem=recv_sems.at[k],
                device_id=(x, y, c), device_id_type=MESH).wait_recv()
        for cp, _ in copies:
            cp.wait_send()
        total = gath[0]
        for j in range(1, N_DEV):
            total = total + gath[j]
        o_ref[...] = total

    return pl.pallas_call(
        body, name="all_reduce_small",
        in_specs=[pl.BlockSpec(memory_space=pltpu.VMEM)], out_specs=pl.BlockSpec(memory_space=pltpu.VMEM),
        out_shape=jax.ShapeDtypeStruct((r, LANES), F32),
        scratch_shapes=[pltpu.VMEM((N_DEV, r, LANES), F32), pltpu.SemaphoreType.DMA((7,)),
                        pltpu.SemaphoreType.DMA((7,))],
    )(v)


def _adamw_math(w, g, m, v):
    m2 = ADAM_B1 * m + (1.0 - ADAM_B1) * g
    v2 = ADAM_B2 * v + (1.0 - ADAM_B2) * (g * g)
    m_hat = m2 / (1.0 - ADAM_B1 ** ADAM_STEP)
    v_hat = v2 / (1.0 - ADAM_B2 ** ADAM_STEP)
    delta = -ADAM_LR * (m_hat / (jnp.sqrt(v_hat) + ADAM_EPS) + ADAM_WD * w)
    return delta, m2, v2


def _adamw_big(w, m, v, part, got, me):
    r, c = w.shape
    tr = max(t for t in range(HALO, min(r, 512) + 1, HALO) if r % t == 0)

    def body(me_ref, w_ref, m_ref, v_ref, own_ref, *rest):
        del me_ref
        got_refs, (g_out, d_out, m_out, v_out) = rest[:N_DEV - 1], rest[N_DEV - 1:]
        g = own_ref[...].astype(F32)
        for ref in got_refs:
            g = g + ref[...].astype(F32)
        delta, m2, v2 = _adamw_math(w_ref[...], g, m_ref[...], v_ref[...])
        g_out[...] = g
        d_out[...] = delta
        m_out[...] = m2
        v_out[...] = v2

    def peer_block(k):
        return pl.BlockSpec((None, tr, c), lambda i, me_ref: (jnp.bitwise_xor(me_ref[0], k), i, 0))

    plain = pl.BlockSpec((tr, c), lambda i, me_ref: (i, 0))
    out = jax.ShapeDtypeStruct((r, c), F32)
    return pl.pallas_call(
        body, name="adamw_big",
        grid_spec=pltpu.PrefetchScalarGridSpec(
            num_scalar_prefetch=1, grid=(r // tr,),
            in_specs=[plain, plain, plain] + [peer_block(k) for k in range(N_DEV)],
            out_specs=[plain] * 4),
        out_shape=[out] * 4,
        compiler_params=_params("parallel"),
    )(me, w, m, v, part, *([got] * (N_DEV - 1)))


def _adamw_small(ws, gs, ms, vs):
    n = len(ws)

    def body(*refs):
        ins, outs = refs[:4 * n], refs[4 * n:]
        for i in range(n):
            delta, m2, v2 = _adamw_math(ins[i][...], ins[n + i][...], ins[2 * n + i][...], ins[3 * n + i][...])
            outs[i][...] = delta
            outs[n + i][...] = m2
            outs[2 * n + i][...] = v2

    out = [jax.ShapeDtypeStruct(w.shape, F32) for w in ws]
    res = pl.pallas_call(body, name="adamw_small", out_shape=out * 3)(*ws, *gs, *ms, *vs)
    return res[:n], res[n:2 * n], res[2 * n:]


BIG = ("w_in", "w_proj_a", "w_proj_b", "w_out", "w_up", "w_down")
LATE = ("w_proj_a", "w_proj_b", "w_out", "w_up", "w_down")
COLUMN_SHARDED = ("w_in", "w_proj_a", "w_proj_b", "w_up")
SMALL = ("norm_mix_g", "b_gate", "conv_a_w", "conv_a_b", "norm_ffn_g", "ffn_conv_w", "ffn_conv_b", "final_norm_g")
SMALL_SHARDED = ("b_gate", "conv_a_w", "ffn_conv_w")
WEIGHTS = ("norm_mix_g", "w_in", "b_gate", "conv_a_w", "conv_a_b", "w_proj_a", "w_proj_b", "w_out", "norm_ffn_g",
           "w_up", "ffn_conv_w", "ffn_conv_b", "w_down", "final_norm_g")


def _pack(vectors, rows):
    flat = jnp.concatenate([v.reshape(-1) for v in vectors])
    return jnp.pad(flat, (0, rows * LANES - flat.shape[0])).reshape(rows, LANES)


def _packed_rows(count):
    rows = -(-count // LANES)
    return -(-rows // SUBLANES) * SUBLANES


def _unpack(packed, shapes):
    flat = packed.reshape(-1)
    out, lo = [], 0
    for s in shapes:
        size = 1
        for dim in s:
            size *= dim
        out.append(flat[lo:lo + size].reshape(s))
        lo += size
    return out


def _full_from_gathered(gathered):
    _, r, c = gathered.shape
    return gathered.reshape(N_DEV * r, c)


def _by_destination(grad):
    rr, cc = grad.shape
    return grad.reshape(N_DEV, rr // N_DEV, cc)


def _block2d(name, a):
    a = a.reshape(a.shape[-2:])
    return a.T if name in COLUMN_SHARDED else a


def kernel(x, norm_mix_g, w_in, b_gate, conv_a_w, conv_a_b, w_proj_a, w_proj_b, w_out, norm_ffn_g, w_up, ffn_conv_w, ffn_conv_b, w_down, final_norm_g, loss_target, m_norm_mix_g, m_w_in, m_b_gate, m_conv_a_w, m_conv_a_b, m_w_proj_a, m_w_proj_b, m_w_out, m_norm_ffn_g, m_w_up, m_ffn_conv_w, m_ffn_conv_b, m_w_down, m_final_norm_g, v_norm_mix_g, v_w_in, v_b_gate, v_conv_a_w, v_conv_a_b, v_w_proj_a, v_w_proj_b, v_w_out, v_norm_ffn_g, v_w_up, v_ffn_conv_w, v_ffn_conv_b, v_w_down, v_final_norm_g):
    given = dict(locals())
    shard = {n: given[n] for n in WEIGHTS}
    mom_m = {n: given["m_" + n] for n in WEIGHTS}
    mom_v = {n: given["v_" + n] for n in WEIGHTS}
    xi, yi, ci = _mesh_pos()
    me = _dev_index(xi, yi, ci)
    me1 = me.astype(jnp.int32).reshape(1)

    big2d = {n: _block2d(n, shard[n]) for n in BIG}
    small_shapes = [shard[n].shape[1:] for n in SMALL_SHARDED]
    n_small = sum(s[0] * s[1] for s in small_shapes)
    packed_small = _pack([shard[n] for n in SMALL_SHARDED], _packed_rows(n_small))
    gathered = _all_gather([big2d["w_in"].astype(BF16), packed_small])
    p = {"w_in": _full_from_gathered(gathered[0])}
    flat_small = gathered[-1].reshape(N_DEV, -1)
    lo = 0
    for n, (rows, width) in zip(SMALL_SHARDED, small_shapes):
        blocks = flat_small[:, lo:lo + rows * width].reshape(N_DEV, rows, width)
        p[n] = blocks.transpose(1, 0, 2).reshape(rows, N_DEV * width)
        lo += rows * width
    p["norm_mix_g"], p["norm_ffn_g"] = shard["norm_mix_g"], shard["norm_ffn_g"]
    p["conv_a_b"], p["ffn_conv_b"] = shard["conv_a_b"], shard["ffn_conv_b"]
    p["final_norm_g"] = shard["final_norm_g"].reshape(1, -1)
    late = {n: big2d[n].astype(BF16) for n in LATE}

    loss_part, dx, parts, got, g_small = _local_step(x[0], loss_target[0], p, late)

    results = {}
    for n in BIG:
        outs = _adamw_big(big2d[n], _block2d(n, mom_m[n]), _block2d(n, mom_v[n]), parts[n], got[n], me1)
        results[n] = [_block2d(n, o).reshape(shard[n].shape) for o in outs]

    small_full_shapes = [g_small[n].shape for n in SMALL]
    n_vec = sum(s[0] * s[1] for s in small_full_shapes) + 1
    packed = _pack([g_small[n] for n in SMALL] + [loss_part.reshape(1)], _packed_rows(n_vec))
    reduced = _all_reduce_small(packed)
    *g_full, loss_vec = _unpack(reduced, small_full_shapes + [(1,)])
    loss = loss_vec[0]
    own_g = []
    for n, g in zip(SMALL, g_full):
        if n in SMALL_SHARDED:
            width = shard[n].shape[-1]
            g = lax.dynamic_slice_in_dim(g, me * width, width, axis=1)
        own_g.append(g.reshape(shard[n].shape))
    def rows2d(a):
        return a.reshape(-1, a.shape[-1])

    deltas, new_ms, new_vs = _adamw_small([rows2d(shard[n]) for n in SMALL], [rows2d(g) for g in own_g],
                                          [rows2d(mom_m[n]) for n in SMALL], [rows2d(mom_v[n]) for n in SMALL])
    for i, n in enumerate(SMALL):
        results[n] = [own_g[i]] + [a.reshape(shard[n].shape) for a in (deltas[i], new_ms[i], new_vs[i])]

    grad_x = dx.reshape(x.shape)
    return (loss, grad_x, *[results[n][0] for n in WEIGHTS], *[results[n][1] for n in WEIGHTS],
            *[results[n][2] for n in WEIGHTS], *[results[n][3] for n in WEIGHTS])
```

```python
import functools

import jax
import jax.numpy as jnp
from jax import lax
from jax.experimental import pallas as pl
from jax.experimental.pallas import tpu as pltpu

F32 = jnp.float32
BF16 = jnp.bfloat16
MESH = pl.DeviceIdType.MESH

N_DEV = 8
RMS_EPS = 1e-6
NEG_INF = -1e30
N_GROUPS = 3
DILATIONS = (1, 4, 16)
HEADS_PER_GROUP = 4
HEAD_DIM = 64
GROUP_W = HEADS_PER_GROUP * HEAD_DIM
ATTN_W = N_GROUPS * GROUP_W
QBLK = 128
ATTN_SCALE = HEAD_DIM ** -0.5

ADAM_LR = 0.001
ADAM_B1 = 0.9
ADAM_B2 = 0.999
ADAM_EPS = 1e-08
ADAM_WD = 0.01
ADAM_STEP = 10

PERM_TOKENS = 256
ROWS_MATMUL = 512
HALO = 16
LANES = 128
SUBLANES = 8
VMEM_LIMIT_BYTES = 56 * 1024 * 1024


def _params(*sem):
    return pltpu.CompilerParams(dimension_semantics=sem, vmem_limit_bytes=VMEM_LIMIT_BYTES)


def _pick_tile(n, cap):
    if n <= cap:
        return n
    best = None
    for t in range(LANES, cap + 1, LANES):
        if n % t == 0:
            best = t
    assert best is not None, (n, cap)
    return best


def _rows(tm, c, j=0):
    return pl.BlockSpec((tm, c), lambda m: (m, j))


def _prev_halo(tm, c):
    return pl.BlockSpec((HALO, c), lambda m: (jnp.maximum(m * (tm // HALO) - 1, 0), 0))


def _next_halo(tm, c, t_total):
    last = t_total // HALO - 1
    return pl.BlockSpec((HALO, c), lambda m: (jnp.minimum((m + 1) * (tm // HALO), last), 0))


def _resident(shape):
    nd = len(shape)
    return pl.BlockSpec(shape, lambda *_: (0,) * nd, pipeline_mode=pl.Buffered(1))


def _acc_spec(c):
    return pl.BlockSpec((SUBLANES, c), lambda *_: (0, 0))


def _shift_down(u, halo, k):
    edge = jnp.concatenate([halo[HALO - SUBLANES:], u[:SUBLANES]], axis=0)
    head = pltpu.roll(edge, k, 0)[SUBLANES:]
    return jnp.concatenate([head, pltpu.roll(u, k, 0)[SUBLANES:]], axis=0)


def _shift_up(u, halo, k):
    n = u.shape[0]
    edge = jnp.concatenate([u[n - SUBLANES:], halo[:SUBLANES]], axis=0)
    tail = pltpu.roll(edge, 2 * SUBLANES - k, 0)[:SUBLANES]
    return jnp.concatenate([pltpu.roll(u, n - k, 0)[:n - SUBLANES], tail], axis=0)


def _interleave(tm, inverse=False):
    return _perm(tm // SUBLANES, tm, inverse)


def _edge_groups(u, halo, k, from_end):
    n = u.shape[0]
    sub = lax.broadcasted_iota(jnp.int32, (SUBLANES, u.shape[1]), 0)
    out = []
    for j in range(2 - k, 2):
        lo = n - HALO + j * SUBLANES if from_end else j * SUBLANES
        own, other = u[lo:lo + SUBLANES], halo[j * SUBLANES:(j + 1) * SUBLANES]
        if from_end:
            out.append(pltpu.roll(jnp.where(sub == SUBLANES - 1, other, own), 1, 0))
        else:
            out.append(pltpu.roll(jnp.where(sub == 0, other, own), SUBLANES - 1, 0))
    return out


def _shift_down_il(u, halo, k):
    return jnp.concatenate(_edge_groups(u, halo, k, True) + [u[:u.shape[0] - k * SUBLANES]], axis=0)


def _shift_up_il(u, halo, k):
    if k == 1:
        edge = _edge_groups(u, halo, 2, False)[:1]
    else:
        edge = _edge_groups(u, halo, 2, False)
    return jnp.concatenate([u[k * SUBLANES:]] + edge, axis=0)


def _stack_rows(rows, c):
    idx = lax.broadcasted_iota(jnp.int32, (SUBLANES, c), 0)
    out = jnp.zeros((SUBLANES, c), F32)
    for i, r in enumerate(rows):
        out = out + jnp.where(idx == i, r, 0.0)
    return out


def _colsum(v):
    return jnp.sum(v, axis=0, keepdims=True)


def _sigmoid(v):
    return 0.5 * jnp.tanh(0.5 * v) + 0.5


def _rms_fwd(xv, g):
    r = lax.rsqrt(jnp.mean(xv * xv, axis=-1, keepdims=True) + RMS_EPS)
    return xv * r * g, r


def _rms_bwd(xv, g, dy):
    r = lax.rsqrt(jnp.mean(xv * xv, axis=-1, keepdims=True) + RMS_EPS)
    xn = xv * r
    dxn = dy * g
    dx = r * (dxn - xn * jnp.mean(dxn * xn, axis=-1, keepdims=True))
    return dx, dy * xn


def _dot(a, b):
    return jnp.dot(a, b, preferred_element_type=F32)


def _dot_nt(a, b):
    return lax.dot_general(a, b, (((1,), (1,)), ((), ())), preferred_element_type=F32)


def _dot_tn(a, b):
    return lax.dot_general(a, b, (((0,), (0,)), ((), ())), preferred_element_type=F32)


def _perm(dil, n, inverse=False):
    i = lax.broadcasted_iota(jnp.int32, (n, n), 0)
    j = lax.broadcasted_iota(jnp.int32, (n, n), 1)
    if inverse:
        i, j = j, i
    per = n // dil
    return (j == (i % per) * dil + i // per).astype(BF16)


def _permute_rows(pm, v):
    if v.dtype == BF16:
        return _dot(pm, v).astype(BF16)
    hi = v.astype(BF16)
    lo = (v - hi.astype(F32)).astype(BF16)
    return _dot(pm, hi) + _dot(pm, lo)


def _stream_view(a, dil):
    t, c = a.shape
    return a.reshape(dil, t // dil, c)


def _stream_spec(dil, tm, c):
    return pl.BlockSpec((dil, tm // dil, c), lambda m: (0, m, 0))


def _load_streams(ref, dil, tm):
    c = ref.shape[-1]
    if dil == 1:
        return ref[...].reshape(tm, c)
    sub = min(PERM_TOKENS, tm)
    pm = _perm(dil, sub, inverse=True)
    parts = [_permute_rows(pm, ref[:, i * (sub // dil):(i + 1) * (sub // dil), :].reshape(sub, c))
             for i in range(tm // sub)]
    return parts[0] if len(parts) == 1 else jnp.concatenate(parts, axis=0)


def _store_streams(ref, dil, tm, v):
    if dil == 1:
        ref[...] = v.reshape(ref.shape).astype(ref.dtype)
        return
    sub = min(PERM_TOKENS, tm)
    pm = _perm(dil, sub)
    for i in range(tm // sub):
        piece = _permute_rows(pm, v[i * sub:(i + 1) * sub])
        ref[:, i * (sub // dil):(i + 1) * (sub // dil), :] = piece.reshape(dil, sub // dil, -1).astype(ref.dtype)


ANY = pl.BlockSpec(memory_space=pl.ANY)


def _mesh_pos():
    return lax.axis_index("x"), lax.axis_index("y"), lax.axis_index("c")


def _dev_index(px, py, pc):
    return 4 * px + 2 * py + pc


class _Exchange:
    def __init__(self, mode, arrays, rows=None, into=()):
        self.mode, self.arrays, self.rows, self.into = mode, list(arrays), rows, list(into)
        n = len(self.arrays)
        if mode == "gather":
            self.out_shape = [jax.ShapeDtypeStruct((N_DEV,) + a.shape, a.dtype) for a in self.arrays]
        else:
            self.out_shape = [jax.ShapeDtypeStruct(a.shape, a.dtype) for a in self.arrays]
        self.scratch = [pltpu.SemaphoreType.DMA((n, N_DEV - 1)), pltpu.SemaphoreType.DMA((n, N_DEV - 1)),
                        pltpu.SemaphoreType.DMA((n,))]

    def _peers(self):
        x, y, c = _mesh_pos()
        flips = [(kx, ky, kc) for kx in (0, 1) for ky in (0, 1) for kc in (0, 1)][1:]
        peers = [(1 - x if kx else x, 1 - y if ky else y, 1 - c if kc else c) for kx, ky, kc in flips]
        return _dev_index(x, y, c), peers

    def _copy(self, ins, outs, sems, i, k, peer, me, sending):
        src = ins[i] if self.mode == "gather" else ins[i].at[_dev_index(*peer)]
        dst = outs[i].at[me if sending else _dev_index(*peer)]
        if self.rows is not None:
            src, dst = src.at[pl.ds(*self.rows)], dst.at[pl.ds(*self.rows)]
        return pltpu.make_async_remote_copy(src_ref=src, dst_ref=dst, send_sem=sems[0].at[i, k],
                                            recv_sem=sems[1].at[i, k], device_id=peer, device_id_type=MESH)

    def _own(self, ins, outs, sems, i, me):
        return pltpu.make_async_copy(ins[i], outs[i].at[me], sems[2].at[i])

    def start(self, ins, outs, sems):
        me, peers = self._peers()
        for i in range(len(ins)):
            if self.mode == "gather":
                self._own(ins, outs, sems, i, me).start()
            for k, peer in enumerate(peers):
                self._copy(ins, outs, sems, i, k, peer, me, True).start()

    def wait(self, ins, outs, sems):
        me, peers = self._peers()
        for i in range(len(ins)):
            for k, peer in enumerate(peers):
                self._copy(ins, outs, sems, i, k, peer, me, False).wait_recv()
            for k, peer in enumerate(peers):
                self._copy(ins, outs, sems, i, k, peer, me, True).wait_send()
            if self.mode == "gather":
                self._own(ins, outs, sems, i, me).wait()


def _call(body, *, name, grid, in_specs, out_specs, out_shape, args, semantics, carry=None, scratch=()):
    if carry is None:
        return pl.pallas_call(body, name=name, grid=grid, in_specs=in_specs, out_specs=out_specs,
                              out_shape=out_shape, scratch_shapes=list(scratch),
                              compiler_params=_params(*semantics))(*args)
    n_in, n_out, n_x, n_s = len(in_specs), len(out_specs), len(carry.arrays), len(scratch)
    n_into = len(carry.into)
    all_in = n_in + n_x + n_into

    def carried(*refs):
        ins, x_ins = refs[:n_in], refs[n_in:n_in + n_x]
        outs = refs[all_in:all_in + n_out]
        x_outs = refs[all_in + n_out:all_in + n_out + n_x]
        own = refs[all_in + n_out + n_x:all_in + n_out + n_x + n_s]
        sems = refs[all_in + n_out + n_x + n_s:]
        first = functools.reduce(jnp.logical_and, [pl.program_id(a) == 0 for a in range(len(grid))])
        last = functools.reduce(jnp.logical_and, [pl.program_id(a) == grid[a] - 1 for a in range(len(grid))])

        @pl.when(first)
        def _():
            carry.start(x_ins, x_outs, sems)

        body(*ins, *outs, *own)

        @pl.when(last)
        def _():
            carry.wait(x_ins, x_outs, sems)

    res = pl.pallas_call(
        carried, name=name, grid=grid, in_specs=list(in_specs) + [ANY] * (n_x + n_into),
        out_specs=list(out_specs) + [ANY] * n_x, out_shape=list(out_shape) + carry.out_shape,
        input_output_aliases={n_in + n_x + i: n_out + i for i in range(n_into)},
        scratch_shapes=list(scratch) + carry.scratch, compiler_params=_params(*["arbitrary"] * len(grid)),
    )(*args, *carry.arrays, *carry.into)
    return list(res[:n_out]), list(res[n_out:])


def _in_proj(x, g, wt, cw, carry=None):
    t, d = x.shape
    n = wt.shape[0]
    tm = min(ROWS_MATMUL, t)
    qkv0 = 3 * cw

    def body(x_ref, g_ref, wt_ref, h_ref, abcv_ref, gates_ref, *s_refs):
        h = _rms_fwd(x_ref[...], g_ref[...])[0].astype(BF16)
        h_ref[...] = h
        abcv_ref[...] = _dot_nt(h, wt_ref[0:qkv0, :]).astype(BF16)
        gates_ref[...] = _dot_nt(h, wt_ref[qkv0 + 3 * ATTN_W:n, :]).astype(BF16)
        for gi, s_ref in enumerate(s_refs):
            cols = [_dot_nt(h, wt_ref[qkv0 + j * ATTN_W + gi * GROUP_W:qkv0 + j * ATTN_W + (gi + 1) * GROUP_W, :])
                    for j in range(3)]
            _store_streams(s_ref, DILATIONS[gi], tm, jnp.concatenate(cols, axis=1).astype(BF16))

    return _call(
        body, name="in_proj", grid=(t // tm,),
        in_specs=[_rows(tm, d), _resident((1, d)), _resident((n, d))],
        out_specs=[_rows(tm, d), _rows(tm, qkv0), _rows(tm, 2 * d)]
        + [_stream_spec(dil, tm, 3 * GROUP_W) for dil in DILATIONS],
        out_shape=[jax.ShapeDtypeStruct((t, d), BF16), jax.ShapeDtypeStruct((t, qkv0), BF16),
                   jax.ShapeDtypeStruct((t, 2 * d), BF16)]
        + [jax.ShapeDtypeStruct((dil, t // dil, 3 * GROUP_W), BF16) for dil in DILATIONS],
        args=(x, g, wt), semantics=("parallel",), carry=carry)


def _head_masks():
    lane = lax.broadcasted_iota(jnp.int32, (1, GROUP_W), 1)
    return lane, [(lane // HEAD_DIM) == h for h in range(HEADS_PER_GROUP)]


def _stack_heads(v, heads):
    return jnp.concatenate([jnp.where(hm, v, jnp.zeros_like(v)) for hm in heads], axis=0)


def _merge_heads(v, heads):
    out = jnp.zeros((QBLK, GROUP_W), v.dtype)
    for h, hm in enumerate(heads):
        out = jnp.where(hm, v[h * QBLK:(h + 1) * QBLK], out)
    return out


def _pair_block(col):
    return pl.BlockSpec((2 * QBLK, GROUP_W), lambda b: (b, col))


def _edge_block(col, shift, nb):
    return pl.BlockSpec((QBLK, GROUP_W), lambda b: (jnp.clip(2 * b + shift, 0, nb - 1), col))


def _band_mask(has_prev):
    rows = HEADS_PER_GROUP * QBLK
    row = lax.broadcasted_iota(jnp.int32, (rows, 2 * QBLK), 0) & (QBLK - 1)
    col = lax.broadcasted_iota(jnp.int32, (rows, 2 * QBLK), 1)
    return ((col < QBLK) & (col >= row) & has_prev) | ((col >= QBLK) & (col - QBLK <= row))


def _next_mask(has_next):
    rows = HEADS_PER_GROUP * QBLK
    row = lax.broadcasted_iota(jnp.int32, (rows, QBLK), 0) & (QBLK - 1)
    col = lax.broadcasted_iota(jnp.int32, (rows, QBLK), 1)
    return (col >= row) & has_next


def _attn_fwd(s, dil, carry=None):
    t = s.shape[0] * s.shape[1]
    nb = t // QBLK
    per_stream = nb // dil
    assert per_stream % 2 == 0

    def body(q_ref, kc_ref, kp_ref, vc_ref, vp_ref, o_ref, lse_ref):
        b = pl.program_id(0)
        _, heads = _head_masks()
        first_has_prev = lax.rem(2 * b, per_stream) != 0
        for j in range(2):
            rows = slice(j * QBLK, (j + 1) * QBLK)
            if j == 0:
                k2 = jnp.concatenate([kp_ref[...], kc_ref[rows, :]], axis=0)
                v2 = jnp.concatenate([vp_ref[...], vc_ref[rows, :]], axis=0)
            else:
                k2, v2 = kc_ref[...], vc_ref[...]
            mask = _band_mask(first_has_prev if j == 0 else True)
            sc = jnp.where(mask, _dot_nt(_stack_heads(q_ref[rows, :], heads), k2) * ATTN_SCALE, NEG_INF)
            mx = jnp.max(sc, axis=1, keepdims=True)
            pr = jnp.exp(sc - mx)
            den = jnp.sum(pr, axis=1, keepdims=True)
            o_all = _dot(pr.astype(BF16), v2) / den
            o_ref[rows, :] = _merge_heads(o_all, heads).astype(BF16)
            lse_ref[rows, :] = _merge_heads(jnp.broadcast_to(mx + jnp.log(den), o_all.shape), heads)

    sv = s.reshape(t, 3 * GROUP_W)
    return _call(
        body, name=f"attn_fwd_d{dil}", grid=(nb // 2,),
        in_specs=[_pair_block(0), _pair_block(1), _edge_block(1, -1, nb), _pair_block(2), _edge_block(2, -1, nb)],
        out_specs=[_pair_block(0), _pair_block(0)],
        out_shape=[jax.ShapeDtypeStruct((t, GROUP_W), BF16), jax.ShapeDtypeStruct((t, GROUP_W), F32)],
        args=(sv, sv, sv, sv, sv), semantics=("parallel",), carry=carry)


def _group_softmax(parts):
    mx = jnp.maximum(jnp.maximum(parts[0], parts[1]), parts[2])
    es = [jnp.exp(p - mx) for p in parts]
    den = es[0] + es[1] + es[2]
    return [e / den for e in es]


def _mixer_out(x, abcv, gates, os, lses, conv_w, conv_b, b_gate, w_pa, w_pb, w_o, carry=None):
    t, d = x.shape
    cw = conv_w.shape[1]
    tm = min(ROWS_MATMUL, t)

    def body(x_ref, abcv_ref, halo_ref, gates_ref, o0_ref, o1_ref, o2_ref, l0_ref, l1_ref, l2_ref, cw_ref, cb_ref,
             bg_ref, wpa_ref, wpb_ref, wo_ref, x1_ref, ya_ref, yb_ref, yap_ref, ybp_ref, mg_ref):
        m = pl.program_id(0)
        ab = abcv_ref[:, 0:cw].astype(F32)
        u = abcv_ref[:, cw:2 * cw].astype(F32) * abcv_ref[:, 2 * cw:3 * cw].astype(F32)
        hu = halo_ref[:, cw:2 * cw].astype(F32) * halo_ref[:, 2 * cw:3 * cw].astype(F32)
        hu = jnp.where(m > 0, hu, 0.0)
        cv = (cw_ref[0:1, :] * _shift_down(u, hu, 2) + cw_ref[1:2, :] * _shift_down(u, hu, 1)
              + cw_ref[2:3, :] * u + cb_ref[...])
        ya = (ab * cv).astype(BF16)
        ya_ref[...] = ya
        alphas = _group_softmax([_load_streams(r, dil, tm) for r, dil in zip((l0_ref, l1_ref, l2_ref), DILATIONS)])
        for i, (o_ref, dil) in enumerate(zip((o0_ref, o1_ref, o2_ref), DILATIONS)):
            sl = slice(i * GROUP_W, (i + 1) * GROUP_W)
            yb_ref[:, sl] = (alphas[i] * _load_streams(o_ref, dil, tm).astype(F32)).astype(BF16)
        yap = _dot_nt(ya, wpa_ref[...])
        ybp = _dot_nt(yb_ref[...], wpb_ref[...])
        yap_ref[...] = yap.astype(BF16)
        ybp_ref[...] = ybp.astype(BF16)
        sa = _sigmoid(gates_ref[:, 0:d].astype(F32) + bg_ref[0:1, :])
        sb = _sigmoid(gates_ref[:, d:2 * d].astype(F32) + bg_ref[1:2, :])
        merged = (sa * yap + sb * ybp).astype(BF16)
        mg_ref[...] = merged
        x1_ref[...] = x_ref[...] + _dot(merged, wo_ref[...])

    return _call(
        body, name="mixer_out", grid=(t // tm,),
        in_specs=[_rows(tm, d), _rows(tm, 3 * cw), _prev_halo(tm, 3 * cw), _rows(tm, 2 * d)]
        + [_stream_spec(dil, tm, GROUP_W) for dil in DILATIONS] * 2
        + [_resident((3, cw)), _resident((1, cw)), _resident((2, d)),
           _resident((d, cw)), _resident((d, ATTN_W)), _resident((d, d))],
        out_specs=[_rows(tm, d), _rows(tm, cw), _rows(tm, ATTN_W), _rows(tm, d), _rows(tm, d), _rows(tm, d)],
        out_shape=[jax.ShapeDtypeStruct((t, d), F32), jax.ShapeDtypeStruct((t, cw), BF16),
                   jax.ShapeDtypeStruct((t, ATTN_W), BF16), jax.ShapeDtypeStruct((t, d), BF16),
                   jax.ShapeDtypeStruct((t, d), BF16), jax.ShapeDtypeStruct((t, d), BF16)],
        args=(x, abcv, abcv, gates, *[_stream_view(a, dil) for a, dil in zip(os, DILATIONS)],
              *[_stream_view(a, dil) for a, dil in zip(lses, DILATIONS)], conv_w, conv_b, b_gate, w_pa, w_pb, w_o),
        semantics=("parallel",), carry=carry)


def _ffn_fwd(x1, target, g2, w_ut, conv_w, conv_b, w_d, g_f, carry=None):
    t, d = x1.shape
    dff = w_d.shape[0]
    tm = min(256, t)
    ck = _pick_tile(dff, 1408)

    def body(x1_ref, tg_ref, g2_ref, wut_ref, cw_ref, cb_ref, wd_ref, gf_ref, h2_ref, up_ref, act_ref, conv_ref,
             dx2_ref, dx2i_ref, acc_ref, loss_ref, halo_ref):
        m = pl.program_id(0)

        @pl.when(m == 0)
        def _():
            acc_ref[...] = jnp.zeros_like(acc_ref)
            loss_ref[...] = jnp.zeros_like(loss_ref)
            halo_ref[...] = jnp.zeros_like(halo_ref)

        h2 = _permute_rows(_interleave(tm), _rms_fwd(x1_ref[...], g2_ref[...])[0].astype(BF16))
        h2_ref[...] = h2

        def conv(c0):
            p = _dot_nt(h2, wut_ref[c0:c0 + ck, :])
            up_ref[:, c0:c0 + ck] = p.astype(BF16)
            hp = halo_ref[:, c0:c0 + ck]
            halo_ref[:, c0:c0 + ck] = p[tm - HALO:, :]
            return (cw_ref[0:1, c0:c0 + ck] * _shift_down_il(p, hp, 2)
                    + cw_ref[1:2, c0:c0 + ck] * _shift_down_il(p, hp, 1)
                    + cw_ref[2:3, c0:c0 + ck] * p + cb_ref[:, c0:c0 + ck])

        down = jnp.zeros((tm, d), F32)
        for c0 in range(0, dff, ck):
            gate = conv(c0)
            val = conv(dff + c0)
            conv_ref[:, c0:c0 + ck] = gate.astype(BF16)
            conv_ref[:, dff + c0:dff + c0 + ck] = val.astype(BF16)
            act = (gate * _sigmoid(gate) * val).astype(BF16)
            act_ref[:, c0:c0 + ck] = act
            down = down + _dot(act, wd_ref[c0:c0 + ck, :])
        x2 = x1_ref[...] + _permute_rows(_interleave(tm, inverse=True), down)
        y, _ = _rms_fwd(x2, gf_ref[...])
        diff = y - tg_ref[...]
        loss_ref[...] += 0.5 * jnp.sum(jnp.mean(diff * diff, axis=-1, keepdims=True))
        dx2, dg = _rms_bwd(x2, gf_ref[...], diff * (1.0 / d))
        dx2_ref[...] = dx2
        dx2i_ref[...] = _permute_rows(_interleave(tm), dx2.astype(BF16))
        acc_ref[...] += _stack_rows([_colsum(dg)], d)

    return _call(
        body, name="ffn_fwd", grid=(t // tm,),
        in_specs=[_rows(tm, d), _rows(tm, d), _resident((1, d)), _resident((2 * dff, d)), _resident((3, 2 * dff)),
                  _resident((1, 2 * dff)), _resident((dff, d)), _resident((1, d))],
        out_specs=[_rows(tm, d), _rows(tm, 2 * dff), _rows(tm, dff), _rows(tm, 2 * dff), _rows(tm, d), _rows(tm, d),
                   _acc_spec(d), _acc_spec(LANES)],
        out_shape=[jax.ShapeDtypeStruct((t, d), BF16), jax.ShapeDtypeStruct((t, 2 * dff), BF16),
                   jax.ShapeDtypeStruct((t, dff), BF16), jax.ShapeDtypeStruct((t, 2 * dff), BF16),
                   jax.ShapeDtypeStruct((t, d), F32), jax.ShapeDtypeStruct((t, d), BF16),
                   jax.ShapeDtypeStruct((SUBLANES, d), F32), jax.ShapeDtypeStruct((SUBLANES, LANES), F32)],
        args=(x1, target, g2, w_ut, conv_w, conv_b, w_d, g_f), semantics=("arbitrary",), carry=carry,
        scratch=[pltpu.VMEM((HALO, 2 * dff), F32)])


def _ffn_act_bwd(dx2, conv, w_d):
    t, d = dx2.shape
    dff = w_d.shape[0]
    tm = min(256, t)
    ck = _pick_tile(dff, 256)

    def body(dx2_ref, conv_ref, wd_ref, dup_ref, acc_ref):
        m = pl.program_id(0)

        @pl.when(m == 0)
        def _():
            acc_ref[...] = jnp.zeros_like(acc_ref)

        dx2v = dx2_ref[...]
        for c0 in range(0, dff, ck):
            dact = _dot_nt(dx2v, wd_ref[c0:c0 + ck, :])
            gate = conv_ref[:, c0:c0 + ck].astype(F32)
            val = conv_ref[:, dff + c0:dff + c0 + ck].astype(F32)
            sg = _sigmoid(gate)
            dval = dact * gate * sg
            dgate = dact * val * sg * (1.0 + gate * (1.0 - sg))
            dup_ref[:, c0:c0 + ck] = dgate.astype(BF16)
            dup_ref[:, dff + c0:dff + c0 + ck] = dval.astype(BF16)
            acc_ref[:, c0:c0 + ck] += _stack_rows([_colsum(dgate)], ck)
            acc_ref[:, dff + c0:dff + c0 + ck] += _stack_rows([_colsum(dval)], ck)

    return pl.pallas_call(
        body, name="ffn_act_bwd", grid=(t // tm,),
        in_specs=[_rows(tm, d), _rows(tm, 2 * dff), _resident((dff, d))],
        out_specs=[_rows(tm, 2 * dff), _acc_spec(2 * dff)],
        out_shape=[jax.ShapeDtypeStruct((t, 2 * dff), BF16), jax.ShapeDtypeStruct((SUBLANES, 2 * dff), F32)],
        compiler_params=_params("arbitrary"),
    )(dx2, conv, w_d)


def _ffn_up_bwd(dup, up_pre, x1, dx2, conv_w, w_u, g2, carry=None):
    t, d = x1.shape
    n = dup.shape[1]
    tm = min(256, t)
    ck = _pick_tile(n, 256)
    last = t // tm - 1

    def body(dup_ref, nxt_ref, up_ref, x1_ref, dx2_ref, cw_ref, wu_ref, g2_ref, dpre_ref, dx1_ref, acc_ref, accw_ref):
        m = pl.program_id(0)

        @pl.when(m == 0)
        def _():
            acc_ref[...] = jnp.zeros_like(acc_ref)
            accw_ref[...] = jnp.zeros_like(accw_ref)

        dh = jnp.zeros((tm, d), F32)
        for c0 in range(0, n, ck):
            du = dup_ref[:, c0:c0 + ck].astype(F32)
            hn = jnp.where(m < last, nxt_ref[:, c0:c0 + ck].astype(F32), 0.0)
            du1 = _shift_up_il(du, hn, 1)
            du2 = _shift_up_il(du, hn, 2)
            dpre = (cw_ref[2:3, c0:c0 + ck] * du + cw_ref[1:2, c0:c0 + ck] * du1
                    + cw_ref[0:1, c0:c0 + ck] * du2).astype(BF16)
            dpre_ref[:, c0:c0 + ck] = dpre
            dh = dh + _dot(dpre, wu_ref[c0:c0 + ck, :])
            p = up_ref[:, c0:c0 + ck].astype(F32)
            accw_ref[:, c0:c0 + ck] += _stack_rows([_colsum(du2 * p), _colsum(du1 * p), _colsum(du * p)], ck)
        dh = _permute_rows(_interleave(tm, inverse=True), dh)
        dx, dg = _rms_bwd(x1_ref[...], g2_ref[...], dh)
        dx1_ref[...] = dx2_ref[...] + dx
        acc_ref[...] += _stack_rows([_colsum(dg)], d)

    return _call(
        body, name="ffn_up_bwd", grid=(t // tm,),
        in_specs=[_rows(tm, n), _next_halo(tm, n, t), _rows(tm, n), _rows(tm, d), _rows(tm, d), _resident((3, n)),
                  _resident((n, d)), _resident((1, d))],
        out_specs=[_rows(tm, n), _rows(tm, d), _acc_spec(d), _acc_spec(n)],
        out_shape=[jax.ShapeDtypeStruct((t, n), BF16), jax.ShapeDtypeStruct((t, d), F32),
                   jax.ShapeDtypeStruct((SUBLANES, d), F32), jax.ShapeDtypeStruct((SUBLANES, n), F32)],
        args=(dup, dup, up_pre, x1, dx2, conv_w, w_u, g2), semantics=("arbitrary",), carry=carry)


def _tn_matmul(a, b, name):
    t, mdim = a.shape
    n = b.shape[1]
    tk = min(1024, t)
    tmm = _pick_tile(mdim, 1536)
    tn = _pick_tile(n, 1024)

    def body(a_ref, b_ref, o_ref, acc_ref):
        k = pl.program_id(2)

        @pl.when(k == 0)
        def _():
            acc_ref[...] = jnp.zeros_like(acc_ref)

        acc_ref[...] += _dot_tn(a_ref[...].astype(BF16), b_ref[...].astype(BF16))

        @pl.when(k == t // tk - 1)
        def _():
            o_ref[...] = acc_ref[...].astype(BF16)

    return pl.pallas_call(
        body, name=name, grid=(mdim // tmm, n // tn, t // tk),
        in_specs=[pl.BlockSpec((tk, tmm), lambda i, j, k: (k, i)), pl.BlockSpec((tk, tn), lambda i, j, k: (k, j))],
        out_specs=pl.BlockSpec((tmm, tn), lambda i, j, k: (i, j)),
        out_shape=jax.ShapeDtypeStruct((mdim, n), BF16),
        scratch_shapes=[pltpu.VMEM((tmm, tn), F32)],
        compiler_params=_params("parallel", "parallel", "arbitrary"),
    )(a, b)


def _mixer_bwd(dx1, gates, yap, ybp, os, lses, b_gate, w_o, w_pa, w_pb):
    t, d = dx1.shape
    cw = w_pa.shape[1]
    tm = min(ROWS_MATMUL, t)

    def body(dx1_ref, gates_ref, yap_ref, ybp_ref, o0_ref, o1_ref, o2_ref, l0_ref, l1_ref, l2_ref, bg_ref, wo_ref,
             wpa_ref, wpb_ref, dgates_ref, dyap_ref, dybp_ref, dya_ref, do0_ref, do1_ref, do2_ref, dl0_ref, dl1_ref,
             dl2_ref, acc_ref):
        m = pl.program_id(0)

        @pl.when(m == 0)
        def _():
            acc_ref[...] = jnp.zeros_like(acc_ref)

        dmg = _dot_nt(dx1_ref[...].astype(BF16), wo_ref[...])
        sa = _sigmoid(gates_ref[:, 0:d].astype(F32) + bg_ref[0:1, :])
        sb = _sigmoid(gates_ref[:, d:2 * d].astype(F32) + bg_ref[1:2, :])
        dyap = (dmg * sa).astype(BF16)
        dybp = (dmg * sb).astype(BF16)
        dga = dmg * yap_ref[...].astype(F32) * sa * (1.0 - sa)
        dgb = dmg * ybp_ref[...].astype(F32) * sb * (1.0 - sb)
        dyap_ref[...] = dyap
        dybp_ref[...] = dybp
        dgates_ref[:, 0:d] = dga.astype(BF16)
        dgates_ref[:, d:2 * d] = dgb.astype(BF16)
        acc_ref[...] += _stack_rows([_colsum(dga), _colsum(dgb)], d)
        dya_ref[...] = _dot(dyap, wpa_ref[...]).astype(BF16)
        dyb = _dot(dybp, wpb_ref[...])

        ri = lax.broadcasted_iota(jnp.int32, (GROUP_W, GROUP_W), 0) // HEAD_DIM
        ci = lax.broadcasted_iota(jnp.int32, (GROUP_W, GROUP_W), 1) // HEAD_DIM
        same_head = (ri == ci).astype(BF16)
        alphas = _group_softmax([_load_streams(r, dil, tm) for r, dil in zip((l0_ref, l1_ref, l2_ref), DILATIONS)])
        prod = jnp.zeros((tm, GROUP_W), F32)
        for i, (o_ref, do_ref, dil) in enumerate(zip((o0_ref, o1_ref, o2_ref), (do0_ref, do1_ref, do2_ref), DILATIONS)):
            dov = alphas[i] * dyb[:, i * GROUP_W:(i + 1) * GROUP_W]
            _store_streams(do_ref, dil, tm, dov.astype(BF16))
            prod = prod + dov * _load_streams(o_ref, dil, tm).astype(F32)
        hi = prod.astype(BF16)
        lo = (prod - hi.astype(F32)).astype(BF16)
        dtot = _dot(hi, same_head) + _dot(lo, same_head)
        for alpha, dl_ref, dil in zip(alphas, (dl0_ref, dl1_ref, dl2_ref), DILATIONS):
            _store_streams(dl_ref, dil, tm, alpha * dtot)

    streams = [_stream_spec(dil, tm, GROUP_W) for dil in DILATIONS]
    res = _call(
        body, name="mixer_bwd", grid=(t // tm,),
        in_specs=[_rows(tm, d), _rows(tm, 2 * d), _rows(tm, d), _rows(tm, d)] + streams * 2
        + [_resident((2, d)), _resident((d, d)), _resident((d, cw)), _resident((d, ATTN_W))],
        out_specs=[_rows(tm, 2 * d), _rows(tm, d), _rows(tm, d), _rows(tm, cw)] + streams * 2 + [_acc_spec(d)],
        out_shape=[jax.ShapeDtypeStruct((t, 2 * d), BF16), jax.ShapeDtypeStruct((t, d), BF16),
                   jax.ShapeDtypeStruct((t, d), BF16), jax.ShapeDtypeStruct((t, cw), BF16)]
        + [jax.ShapeDtypeStruct((dil, t // dil, GROUP_W), BF16) for dil in DILATIONS]
        + [jax.ShapeDtypeStruct((dil, t // dil, GROUP_W), F32) for dil in DILATIONS]
        + [jax.ShapeDtypeStruct((SUBLANES, d), F32)],
        args=(dx1, gates, yap, ybp, *[_stream_view(a, dil) for a, dil in zip(os, DILATIONS)],
              *[_stream_view(a, dil) for a, dil in zip(lses, DILATIONS)], b_gate, w_o, w_pa, w_pb),
        semantics=("arbitrary",))
    dgates, dyap, dybp, dya = res[:4]
    dos = [a.reshape(t, GROUP_W) for a in res[4:7]]
    dls = [a.reshape(t, GROUP_W) for a in res[7:10]]
    return dgates, dyap, dybp, dya, dos, dls, res[10]


def _attn_bwd(s, do, lse, dl, dil, carry=None):
    t = s.shape[0] * s.shape[1]
    nb = t // QBLK
    per_stream = nb // dil

    def body(q_ref, qn_ref, kc_ref, kp_ref, vc_ref, vp_ref, do_ref, don_ref, lse_ref, lsen_ref, dl_ref, dln_ref,
             ds_ref):
        b = pl.program_id(0)
        lane, heads = _head_masks()
        first_has_prev = lax.rem(2 * b, per_stream) != 0
        last_has_next = lax.rem(2 * b + 2, per_stream) != 0

        def cols(v):
            return jnp.concatenate([jnp.sum(jnp.where(lane == h * HEAD_DIM, v, 0.0), axis=1, keepdims=True)
                                    for h in range(HEADS_PER_GROUP)], axis=0)

        def pair(qs, dos, k, v, valid, lse_c, dl_c):
            s = jnp.where(valid, _dot_nt(qs, k) * ATTN_SCALE, NEG_INF)
            p = jnp.exp(s - lse_c)
            ds = p * (_dot_nt(dos, v) - dl_c)
            return p.astype(BF16), ds.astype(BF16)

        lo, hi = slice(0, QBLK), slice(QBLK, 2 * QBLK)
        for j, rows in enumerate((lo, hi)):
            q, do, lse, dl = q_ref[rows, :], do_ref[rows, :], lse_ref[rows, :], dl_ref[rows, :]
            kc, vc = kc_ref[rows, :], vc_ref[rows, :]
            if j == 0:
                k2 = jnp.concatenate([kp_ref[...], kc], axis=0)
                v2 = jnp.concatenate([vp_ref[...], vc], axis=0)
                qn, don, lsen, dln = q_ref[hi, :], do_ref[hi, :], lse_ref[hi, :], dl_ref[hi, :]
                mask, mask_n = _band_mask(first_has_prev), _next_mask(True)
            else:
                k2, v2 = kc_ref[...], vc_ref[...]
                qn, don, lsen, dln = qn_ref[...], don_ref[...], lsen_ref[...], dln_ref[...]
                mask, mask_n = _band_mask(True), _next_mask(last_has_next)
            qs, qns = _stack_heads(q, heads), _stack_heads(qn, heads)
            dos, dons = _stack_heads(do, heads), _stack_heads(don, heads)
            p_q, ds_q = pair(qs, dos, k2, v2, mask, cols(lse), cols(dl))
            p_n, ds_n = pair(qns, dons, kc, vc, mask_n, cols(lsen), cols(dln))
            dq = _merge_heads(_dot(ds_q, k2), heads)
            dk = _dot_tn(jnp.concatenate([ds_q[:, QBLK:], ds_n], axis=0), jnp.concatenate([qs, qns], axis=0))
            dv = _dot_tn(jnp.concatenate([p_q[:, QBLK:], p_n], axis=0), jnp.concatenate([dos, dons], axis=0))
            ds_ref[rows, 0:GROUP_W] = (dq * ATTN_SCALE).astype(BF16)
            ds_ref[rows, GROUP_W:2 * GROUP_W] = (dk * ATTN_SCALE).astype(BF16)
            ds_ref[rows, 2 * GROUP_W:3 * GROUP_W] = dv.astype(BF16)

    sv = s.reshape(t, 3 * GROUP_W)
    cur, nxt = _pair_block(0), _edge_block(0, 2, nb)
    return _call(
        body, name=f"attn_bwd_d{dil}", grid=(nb // 2,),
        in_specs=[cur, nxt, _pair_block(1), _edge_block(1, -1, nb), _pair_block(2), _edge_block(2, -1, nb),
                  cur, nxt, cur, nxt, cur, nxt],
        out_specs=[pl.BlockSpec((2 * QBLK, 3 * GROUP_W), lambda b: (b, 0))],
        out_shape=[jax.ShapeDtypeStruct((t, 3 * GROUP_W), BF16)],
        args=(sv, sv, sv, sv, sv, sv, do, do, lse, lse, dl, dl), semantics=("parallel",), carry=carry)


def _conv_mixer_bwd(abcv, dya, conv_w, conv_b):
    t = abcv.shape[0]
    cw = conv_w.shape[1]
    tm = min(256, t)
    last = t // tm - 1

    def body(a_ref, ap_ref, an_ref, dya_ref, dyan_ref, cw_ref, cb_ref, d_ref, acc_ref):
        m = pl.program_id(0)

        @pl.when(m == 0)
        def _():
            acc_ref[...] = jnp.zeros_like(acc_ref)

        ab = a_ref[:, 0:cw].astype(F32)
        ac = a_ref[:, cw:2 * cw].astype(F32)
        av = a_ref[:, 2 * cw:3 * cw].astype(F32)
        u = ac * av
        hu = ap_ref[:, cw:2 * cw].astype(F32) * ap_ref[:, 2 * cw:3 * cw].astype(F32)
        hu = jnp.where(m > 0, hu, 0.0)
        u1 = _shift_down(u, hu, 1)
        u2 = _shift_down(u, hu, 2)
        cv = cw_ref[0:1, :] * u2 + cw_ref[1:2, :] * u1 + cw_ref[2:3, :] * u + cb_ref[...]
        dya_v = dya_ref[...].astype(F32)
        dcv = dya_v * ab
        ndcv = jnp.where(m < last, dyan_ref[...].astype(F32) * an_ref[:, 0:cw].astype(F32), 0.0)
        du = (cw_ref[2:3, :] * dcv + cw_ref[1:2, :] * _shift_up(dcv, ndcv, 1)
              + cw_ref[0:1, :] * _shift_up(dcv, ndcv, 2))
        d_ref[:, 0:cw] = (dya_v * cv).astype(BF16)
        d_ref[:, cw:2 * cw] = (du * av).astype(BF16)
        d_ref[:, 2 * cw:3 * cw] = (du * ac).astype(BF16)
        acc_ref[...] += _stack_rows([_colsum(dcv * u2), _colsum(dcv * u1), _colsum(dcv * u), _colsum(dcv)], cw)

    return pl.pallas_call(
        body, name="conv_mixer_bwd", grid=(t // tm,),
        in_specs=[_rows(tm, 3 * cw), _prev_halo(tm, 3 * cw), _next_halo(tm, 3 * cw, t), _rows(tm, cw),
                  _next_halo(tm, cw, t), _resident((3, cw)), _resident((1, cw))],
        out_specs=[_rows(tm, 3 * cw), _acc_spec(cw)],
        out_shape=[jax.ShapeDtypeStruct((t, 3 * cw), BF16), jax.ShapeDtypeStruct((SUBLANES, cw), F32)],
        compiler_params=_params("arbitrary"),
    )(abcv, abcv, abcv, dya, dya, conv_w, conv_b)


def _in_proj_bwd(x, dx1, dabcv, dss, dgates, w_in, g1, carry=None):
    t, d = x.shape
    qkv0 = dabcv.shape[1]
    n = w_in.shape[0]
    tm = min(ROWS_MATMUL, t)

    def body(x_ref, dx1_ref, da_ref, ds0_ref, ds1_ref, ds2_ref, dg_ref, w_ref, g_ref, dx_ref, acc_ref):
        m = pl.program_id(0)

        @pl.when(m == 0)
        def _():
            acc_ref[...] = jnp.zeros_like(acc_ref)

        dh = _dot(da_ref[...], w_ref[0:qkv0, :]) + _dot(dg_ref[...], w_ref[qkv0 + 3 * ATTN_W:n, :])
        for gi, (ds_ref, dil) in enumerate(zip((ds0_ref, ds1_ref, ds2_ref), DILATIONS)):
            ds = _load_streams(ds_ref, dil, tm)
            for j in range(3):
                c0 = qkv0 + j * ATTN_W + gi * GROUP_W
                dh = dh + _dot(ds[:, j * GROUP_W:(j + 1) * GROUP_W], w_ref[c0:c0 + GROUP_W, :])
        dx, dg = _rms_bwd(x_ref[...], g_ref[...], dh)
        dx_ref[...] = dx1_ref[...] + dx
        acc_ref[...] += _stack_rows([_colsum(dg)], d)

    return _call(
        body, name="in_proj_bwd", grid=(t // tm,),
        in_specs=[_rows(tm, d), _rows(tm, d), _rows(tm, qkv0)]
        + [_stream_spec(dil, tm, 3 * GROUP_W) for dil in DILATIONS]
        + [_rows(tm, 2 * d), _resident((n, d)), _resident((1, d))],
        out_specs=[_rows(tm, d), _acc_spec(d)],
        out_shape=[jax.ShapeDtypeStruct((t, d), F32), jax.ShapeDtypeStruct((SUBLANES, d), F32)],
        args=(x, dx1, dabcv, *[_stream_view(a, dil) for a, dil in zip(dss, DILATIONS)], dgates, w_in, g1),
        semantics=("arbitrary",), carry=carry)


def _dw_in_qkv(ds, h, dil):
    t, d = h.shape
    tk = min(1024, t)
    sub = min(256, t)
    width = 3 * GROUP_W

    def body(ds_ref, h_ref, o_ref, acc_ref):
        k = pl.program_id(0)

        @pl.when(k == 0)
        def _():
            acc_ref[...] = jnp.zeros_like(acc_ref)

        upd = None
        for i in range(tk // sub):
            rows = ds_ref[:, i * (sub // dil):(i + 1) * (sub // dil), :].reshape(sub, width)
            if dil > 1:
                rows = _permute_rows(_perm(dil, sub, inverse=True), rows)
            term = _dot_tn(rows, h_ref[i * sub:(i + 1) * sub, :])
            upd = term if upd is None else upd + term
        acc_ref[...] += upd

        @pl.when(k == t // tk - 1)
        def _():
            o_ref[...] = acc_ref[...].astype(BF16)

    return pl.pallas_call(
        body, name=f"dw_in_qkv_d{dil}", grid=(t // tk,),
        in_specs=[_stream_spec(dil, tk, width), _rows(tk, d)],
        out_specs=pl.BlockSpec((width, d), lambda k: (0, 0)),
        out_shape=jax.ShapeDtypeStruct((width, d), BF16),
        scratch_shapes=[pltpu.VMEM((width, d), F32)],
        compiler_params=_params("arbitrary"),
    )(_stream_view(ds, dil), h)


def _local_step(x, target, p, late):
    cw = p["conv_a_w"].shape[1]
    (h, abcv, gates, *ss), (g_up,) = _in_proj(x, p["norm_mix_g"], p["w_in"], cw,
                                              carry=_Exchange("gather", [late["w_up"]]))
    w_up = _full_from_gathered(g_up)
    (o0, lse0), g_proj = _attn_fwd(ss[0], DILATIONS[0],
                                   carry=_Exchange("gather", [late["w_proj_a"], late["w_proj_b"]]))
    (o1, lse1), (g_out,) = _attn_fwd(ss[1], DILATIONS[1], carry=_Exchange("gather", [late["w_out"]]))
    o2, lse2 = _attn_fwd(ss[2], DILATIONS[2])
    w_pa, w_pb, w_out = [_full_from_gathered(g) for g in (*g_proj, g_out)]
    os, lses = (o0, o1, o2), (lse0, lse1, lse2)
    (x1, ya, yb, yap, ybp, merged), (g_down,) = _mixer_out(
        x, abcv, gates, os, lses, p["conv_a_w"], p["conv_a_b"], p["b_gate"], w_pa, w_pb, w_out,
        carry=_Exchange("gather", [late["w_down"]]))
    w_down = _full_from_gathered(g_down)
    h2, up_pre, act, conv, dx2, dx2i, acc_gf, loss = _ffn_fwd(x1, target, p["norm_ffn_g"], w_up, p["ffn_conv_w"],
                                                              p["ffn_conv_b"], w_down, p["final_norm_g"])

    parts, got = {}, {}
    dup, acc_fb = _ffn_act_bwd(dx2i, conv, w_down)
    parts["w_down"] = _by_destination(_tn_matmul(act, dx2i, "dw_down"))
    (dpre, dx1, acc_g2, acc_fw), (got["w_down"],) = _ffn_up_bwd(dup, up_pre, x1, dx2, p["ffn_conv_w"], w_up,
                                                                p["norm_ffn_g"],
                                                                carry=_Exchange("scatter", [parts["w_down"]]))
    parts["w_up"] = _by_destination(_tn_matmul(dpre, h2, "dw_up"))
    dgates, dyap, dybp, dya, dos, dls, acc_bg = _mixer_bwd(dx1, gates, yap, ybp, os, lses, p["b_gate"], w_out,
                                                           w_pa, w_pb)
    parts["w_out"] = _by_destination(_tn_matmul(merged, dx1, "dw_out"))
    parts["w_proj_a"] = _by_destination(_tn_matmul(dyap, ya, "dw_proj_a"))
    parts["w_proj_b"] = _by_destination(_tn_matmul(dybp, yb, "dw_proj_b"))
    minor = ("w_out", "w_proj_a", "w_proj_b")
    half = parts["w_up"].shape[1] // 2
    (ds0,), received = _attn_bwd(ss[0], dos[0], lses[0], dls[0], DILATIONS[0],
                                 carry=_Exchange("scatter", [parts[n] for n in minor]))
    got.update(zip(minor, received))
    (ds1,), first_half = _attn_bwd(ss[1], dos[1], lses[1], dls[1], DILATIONS[1],
                                   carry=_Exchange("scatter", [parts["w_up"]], rows=(0, half)))
    (ds2,), (got["w_up"],) = _attn_bwd(ss[2], dos[2], lses[2], dls[2], DILATIONS[2],
                                       carry=_Exchange("scatter", [parts["w_up"]], rows=(half, half),
                                                       into=first_half))
    dss = [ds0, ds1, ds2]
    dabcv, acc_ca = _conv_mixer_bwd(abcv, dya, p["conv_a_w"], p["conv_a_b"])
    dw_s = [_dw_in_qkv(ds, h, dil) for ds, dil in zip(dss, DILATIONS)]
    dw_qkv = [w[j * GROUP_W:(j + 1) * GROUP_W] for j in range(3) for w in dw_s]
    g_w_in = jnp.concatenate([_tn_matmul(dabcv, h, "dw_in_a"), *dw_qkv, _tn_matmul(dgates, h, "dw_in_g")], axis=0)
    parts["w_in"] = _by_destination(g_w_in)
    (dx, acc_g1), (got["w_in"],) = _in_proj_bwd(x, dx1, dabcv, dss, dgates, p["w_in"], p["norm_mix_g"],
                                                carry=_Exchange("scatter", [parts["w_in"]]))
    small = dict(norm_mix_g=acc_g1[0:1], b_gate=acc_bg[0:2], conv_a_w=acc_ca[0:3], conv_a_b=acc_ca[3:4],
                 norm_ffn_g=acc_g2[0:1], ffn_conv_w=acc_fw[0:3], ffn_conv_b=acc_fb[0:1], final_norm_g=acc_gf[0:1])
    return loss[0, 0], dx, parts, got, small


def _all_gather(shards):
    n = len(shards)

    def body(*refs):
        ins, outs = refs[:n], refs[n:2 * n]
        send_sems, recv_sems, local_sems = refs[2 * n:]
        x, y, c = _mesh_pos()
        me, sibling = (x, y, c), (x, y, 1 - c)
        chips = [(1 - x, y), (x, 1 - y), (1 - x, 1 - y)]

        def copy(i, k, block, to, src=None):
            rows = outs[i].at[_dev_index(*block)]
            return pltpu.make_async_remote_copy(
                src_ref=rows if src is None else src, dst_ref=rows, send_sem=send_sems.at[i, k],
                recv_sem=recv_sems.at[i, k], device_id=to, device_id_type=MESH)

        mine, first, passed = [], [], []
        for i in range(n):
            cp = pltpu.make_async_copy(ins[i], outs[i].at[_dev_index(*me)], local_sems.at[i])
            cp.start()
            mine.append(cp)
            first.append(copy(i, 0, me, sibling, src=ins[i]))
            first += [copy(i, 1 + j, me, (*chip, c), src=ins[i]) for j, chip in enumerate(chips)]
        for cp in first:
            cp.start()
        for i in range(n):
            for j, chip in enumerate(chips):
                copy(i, 1 + j, (*chip, c), me).wait_recv()
                fw = copy(i, 4 + j, (*chip, c), sibling)
                fw.start()
                passed.append(fw)
        for i in range(n):
            copy(i, 0, sibling, me).wait_recv()
            for j, chip in enumerate(chips):
                copy(i, 4 + j, (*chip, 1 - c), me).wait_recv()
        for cp in first + passed:
            cp.wait_send()
        for cp in mine:
            cp.wait()

    return pl.pallas_call(
        body, name="all_gather_weights",
        in_specs=[ANY] * n, out_specs=[ANY] * n,
        out_shape=[jax.ShapeDtypeStruct((N_DEV,) + s.shape, s.dtype) for s in shards],
        scratch_shapes=[pltpu.SemaphoreType.DMA((n, 7)), pltpu.SemaphoreType.DMA((n, 7)),
                        pltpu.SemaphoreType.DMA((n,))],
    )(*shards)


def _all_reduce_small(v):
    r = v.shape[0]

    def body(v_ref, o_ref, gath, send_sems, recv_sems):
        x, y, c = _mesh_pos()
        me = _dev_index(x, y, c)
        gath[me] = v_ref[...]
        flips = [(kx, ky, kc) for kx in (0, 1) for ky in (0, 1) for kc in (0, 1)][1:]
        copies = []
        for k, (kx, ky, kc) in enumerate(flips):
            px = 1 - x if kx else x
            py = 1 - y if ky else y
            pc = 1 - c if kc else c
            cp = pltpu.make_async_remote_copy(
                src_ref=v_ref, dst_ref=gath.at[me], send_sem=send_sems.at[k], recv_sem=recv_sems.at[k],
                device_id=(px, py, pc), device_id_type=MESH)
            cp.start()
            copies.append((cp, _dev_index(px, py, pc)))
        for k, (cp, peer) in enumerate(copies):
            pltpu.make_async_remote_copy(
                src_ref=v_ref, dst_ref=gath.at[peer], send_sem=send_sems.at[k], recv_sem=recv_sems.at[k],
                device_id=(x, y, c), device_id_type=MESH).wait_recv()
        for cp, _ in copies:
            cp.wait_send()
        total = gath[0]
        for j in range(1, N_DEV):
            total = total + gath[j]
        o_ref[...] = total

    return pl.pallas_call(
        body, name="all_reduce_small",
        in_specs=[pl.BlockSpec(memory_space=pltpu.VMEM)], out_specs=pl.BlockSpec(memory_space=pltpu.VMEM),
        out_shape=jax.ShapeDtypeStruct((r, LANES), F32),
        scratch_shapes=[pltpu.VMEM((N_DEV, r, LANES), F32), pltpu.SemaphoreType.DMA((7,)),
                        pltpu.SemaphoreType.DMA((7,))],
    )(v)


def _adamw_math(w, g, m, v):
    m2 = ADAM_B1 * m + (1.0 - ADAM_B1) * g
    v2 = ADAM_B2 * v + (1.0 - ADAM_B2) * (g * g)
    m_hat = m2 / (1.0 - ADAM_B1 ** ADAM_STEP)
    v_hat = v2 / (1.0 - ADAM_B2 ** ADAM_STEP)
    delta = -ADAM_LR * (m_hat / (jnp.sqrt(v_hat) + ADAM_EPS) + ADAM_WD * w)
    return delta, m2, v2


def _adamw_big(w, m, v, part, got, me):
    r, c = w.shape
    tr = max(t for t in range(HALO, min(r, 512) + 1, HALO) if r % t == 0)

    def body(me_ref, w_ref, m_ref, v_ref, own_ref, *rest):
        del me_ref
        got_refs, (g_out, d_out, m_out, v_out) = rest[:N_DEV - 1], rest[N_DEV - 1:]
        g = own_ref[...].astype(F32)
        for ref in got_refs:
            g = g + ref[...].astype(F32)
        delta, m2, v2 = _adamw_math(w_ref[...], g, m_ref[...], v_ref[...])
        g_out[...] = g
        d_out[...] = delta
        m_out[...] = m2
        v_out[...] = v2

    def peer_block(k):
        return pl.BlockSpec((None, tr, c), lambda i, me_ref: (jnp.bitwise_xor(me_ref[0], k), i, 0))

    plain = pl.BlockSpec((tr, c), lambda i, me_ref: (i, 0))
    out = jax.ShapeDtypeStruct((r, c), F32)
    return pl.pallas_call(
        body, name="adamw_big",
        grid_spec=pltpu.PrefetchScalarGridSpec(
            num_scalar_prefetch=1, grid=(r // tr,),
            in_specs=[plain, plain, plain] + [peer_block(k) for k in range(N_DEV)],
            out_specs=[plain] * 4),
        out_shape=[out] * 4,
        compiler_params=_params("parallel"),
    )(me, w, m, v, part, *([got] * (N_DEV - 1)))


def _adamw_small(ws, gs, ms, vs):
    n = len(ws)

    def body(*refs):
        ins, outs = refs[:4 * n], refs[4 * n:]
        for i in range(n):
            delta, m2, v2 = _adamw_math(ins[i][...], ins[n + i][...], ins[2 * n + i][...], ins[3 * n + i][...])
            outs[i][...] = delta
            outs[n + i][...] = m2
            outs[2 * n + i][...] = v2

    out = [jax.ShapeDtypeStruct(w.shape, F32) for w in ws]
    res = pl.pallas_call(body, name="adamw_small", out_shape=out * 3)(*ws, *gs, *ms, *vs)
    return res[:n], res[n:2 * n], res[2 * n:]


BIG = ("w_in", "w_proj_a", "w_proj_b", "w_out", "w_up", "w_down")
LATE = ("w_proj_a", "w_proj_b", "w_out", "w_up", "w_down")
COLUMN_SHARDED = ("w_in", "w_proj_a", "w_proj_b", "w_up")
SMALL = ("norm_mix_g", "b_gate", "conv_a_w", "conv_a_b", "norm_ffn_g", "ffn_conv_w", "ffn_conv_b", "final_norm_g")
SMALL_SHARDED = ("b_gate", "conv_a_w", "ffn_conv_w")
WEIGHTS = ("norm_mix_g", "w_in", "b_gate", "conv_a_w", "conv_a_b", "w_proj_a", "w_proj_b", "w_out", "norm_ffn_g",
           "w_up", "ffn_conv_w", "ffn_conv_b", "w_down", "final_norm_g")


def _pack(vectors, rows):
    flat = jnp.concatenate([v.reshape(-1) for v in vectors])
    return jnp.pad(flat, (0, rows * LANES - flat.shape[0])).reshape(rows, LANES)


def _packed_rows(count):
    rows = -(-count // LANES)
    return -(-rows // SUBLANES) * SUBLANES


def _unpack(packed, shapes):
    flat = packed.reshape(-1)
    out, lo = [], 0
    for s in shapes:
        size = 1
        for dim in s:
            size *= dim
        out.append(flat[lo:lo + size].reshape(s))
        lo += size
    return out


def _full_from_gathered(gathered):
    _, r, c = gathered.shape
    return gathered.reshape(N_DEV * r, c)


def _by_destination(grad):
    rr, cc = grad.shape
    return grad.reshape(N_DEV, rr // N_DEV, cc)


def _block2d(name, a):
    a = a.reshape(a.shape[-2:])
    return a.T if name in COLUMN_SHARDED else a


def kernel(x, norm_mix_g, w_in, b_gate, conv_a_w, conv_a_b, w_proj_a, w_proj_b, w_out, norm_ffn_g, w_up, ffn_conv_w, ffn_conv_b, w_down, final_norm_g, loss_target, m_norm_mix_g, m_w_in, m_b_gate, m_conv_a_w, m_conv_a_b, m_w_proj_a, m_w_proj_b, m_w_out, m_norm_ffn_g, m_w_up, m_ffn_conv_w, m_ffn_conv_b, m_w_down, m_final_norm_g, v_norm_mix_g, v_w_in, v_b_gate, v_conv_a_w, v_conv_a_b, v_w_proj_a, v_w_proj_b, v_w_out, v_norm_ffn_g, v_w_up, v_ffn_conv_w, v_ffn_conv_b, v_w_down, v_final_norm_g):
    given = dict(locals())
    shard = {n: given[n] for n in WEIGHTS}
    mom_m = {n: given["m_" + n] for n in WEIGHTS}
    mom_v = {n: given["v_" + n] for n in WEIGHTS}
    xi, yi, ci = _mesh_pos()
    me = _dev_index(xi, yi, ci)
    me1 = me.astype(jnp.int32).reshape(1)

    big2d = {n: _block2d(n, shard[n]) for n in BIG}
    small_shapes = [shard[n].shape[1:] for n in SMALL_SHARDED]
    n_small = sum(s[0] * s[1] for s in small_shapes)
    packed_small = _pack([shard[n] for n in SMALL_SHARDED], _packed_rows(n_small))
    gathered = _all_gather([big2d["w_in"].astype(BF16), packed_small])
    p = {"w_in": _full_from_gathered(gathered[0])}
    flat_small = gathered[-1].reshape(N_DEV, -1)
    lo = 0
    for n, (rows, width) in zip(SMALL_SHARDED, small_shapes):
        blocks = flat_small[:, lo:lo + rows * width].reshape(N_DEV, rows, width)
        p[n] = blocks.transpose(1, 0, 2).reshape(rows, N_DEV * width)
        lo += rows * width
    p["norm_mix_g"], p["norm_ffn_g"] = shard["norm_mix_g"], shard["norm_ffn_g"]
    p["conv_a_b"], p["ffn_conv_b"] = shard["conv_a_b"], shard["ffn_conv_b"]
    p["final_norm_g"] = shard["final_norm_g"].reshape(1, -1)
    late = {n: big2d[n].astype(BF16) for n in LATE}

    loss_part, dx, parts, got, g_small = _local_step(x[0], loss_target[0], p, late)

    results = {}
    for n in BIG:
        outs = _adamw_big(big2d[n], _block2d(n, mom_m[n]), _block2d(n, mom_v[n]), parts[n], got[n], me1)
        results[n] = [_block2d(n, o).reshape(shard[n].shape) for o in outs]

    small_full_shapes = [g_small[n].shape for n in SMALL]
    n_vec = sum(s[0] * s[1] for s in small_full_shapes) + 1
    packed = _pack([g_small[n] for n in SMALL] + [loss_part.reshape(1)], _packed_rows(n_vec))
    reduced = _all_reduce_small(packed)
    *g_full, loss_vec = _unpack(reduced, small_full_shapes + [(1,)])
    loss = loss_vec[0]
    own_g = []
    for n, g in zip(SMALL, g_full):
        if n in SMALL_SHARDED:
            width = shard[n].shape[-1]
            g = lax.dynamic_slice_in_dim(g, me * width, width, axis=1)
        own_g.append(g.reshape(shard[n].shape))
    def rows2d(a):
        return a.reshape(-1, a.shape[-1])

    deltas, new_ms, new_vs = _adamw_small([rows2d(shard[n]) for n in SMALL], [rows2d(g) for g in own_g],
                                          [rows2d(mom_m[n]) for n in SMALL], [rows2d(mom_v[n]) for n in SMALL])
    for i, n in enumerate(SMALL):
        results[n] = [own_g[i]] + [a.reshape(shard[n].shape) for a in (deltas[i], new_ms[i], new_vs[i])]

    grad_x = dx.reshape(x.shape)
    return (loss, grad_x, *[results[n][0] for n in WEIGHTS], *[results[n][1] for n in WEIGHTS],
            *[results[n][2] for n in WEIGHTS], *[results[n][3] for n in WEIGHTS])
```

```python
import functools

import jax
import jax.numpy as jnp
from jax import lax
from jax.experimental import pallas as pl
from jax.experimental.pallas import tpu as pltpu

F32 = jnp.float32
BF16 = jnp.bfloat16
MESH = pl.DeviceIdType.MESH

N_DEV = 8
RMS_EPS = 1e-6
NEG_INF = -1e30
N_GROUPS = 3
DILATIONS = (1, 4, 16)
HEADS_PER_GROUP = 4
HEAD_DIM = 64
GROUP_W = HEADS_PER_GROUP * HEAD_DIM
ATTN_W = N_GROUPS * GROUP_W
QBLK = 128
ATTN_SCALE = HEAD_DIM ** -0.5

ADAM_LR = 0.001
ADAM_B1 = 0.9
ADAM_B2 = 0.999
ADAM_EPS = 1e-08
ADAM_WD = 0.01
ADAM_STEP = 10

PERM_TOKENS = 256
ROWS_MATMUL = 512
HALO = 16
LANES = 128
SUBLANES = 8
VMEM_LIMIT_BYTES = 56 * 1024 * 1024


def _params(*sem):
    return pltpu.CompilerParams(dimension_semantics=sem, vmem_limit_bytes=VMEM_LIMIT_BYTES)


def _pick_tile(n, cap):
    if n <= cap:
        return n
    best = None
    for t in range(LANES, cap + 1, LANES):
        if n % t == 0:
            best = t
    assert best is not None, (n, cap)
    return best


def _rows(tm, c, j=0):
    return pl.BlockSpec((tm, c), lambda m: (m, j))


def _prev_halo(tm, c):
    return pl.BlockSpec((HALO, c), lambda m: (jnp.maximum(m * (tm // HALO) - 1, 0), 0))


def _next_halo(tm, c, t_total):
    last = t_total // HALO - 1
    return pl.BlockSpec((HALO, c), lambda m: (jnp.minimum((m + 1) * (tm // HALO), last), 0))


def _resident(shape):
    nd = len(shape)
    return pl.BlockSpec(shape, lambda *_: (0,) * nd, pipeline_mode=pl.Buffered(1))


def _acc_spec(c):
    return pl.BlockSpec((SUBLANES, c), lambda *_: (0, 0))


def _shift_down(u, halo, k):
    edge = jnp.concatenate([halo[HALO - SUBLANES:], u[:SUBLANES]], axis=0)
    head = pltpu.roll(edge, k, 0)[SUBLANES:]
    return jnp.concatenate([head, pltpu.roll(u, k, 0)[SUBLANES:]], axis=0)


def _shift_up(u, halo, k):
    n = u.shape[0]
    edge = jnp.concatenate([u[n - SUBLANES:], halo[:SUBLANES]], axis=0)
    tail = pltpu.roll(edge, 2 * SUBLANES - k, 0)[:SUBLANES]
    return jnp.concatenate([pltpu.roll(u, n - k, 0)[:n - SUBLANES], tail], axis=0)


def _interleave(tm, inverse=False):
    return _perm(tm // SUBLANES, tm, inverse)


def _edge_groups(u, halo, k, from_end):
    n = u.shape[0]
    sub = lax.broadcasted_iota(jnp.int32, (SUBLANES, u.shape[1]), 0)
    out = []
    for j in range(2 - k, 2):
        lo = n - HALO + j * SUBLANES if from_end else j * SUBLANES
        own, other = u[lo:lo + SUBLANES], halo[j * SUBLANES:(j + 1) * SUBLANES]
        if from_end:
            out.append(pltpu.roll(jnp.where(sub == SUBLANES - 1, other, own), 1, 0))
        else:
            out.append(pltpu.roll(jnp.where(sub == 0, other, own), SUBLANES - 1, 0))
    return out


def _shift_down_il(u, halo, k):
    return jnp.concatenate(_edge_groups(u, halo, k, True) + [u[:u.shape[0] - k * SUBLANES]], axis=0)


def _shift_up_il(u, halo, k):
    if k == 1:
        edge = _edge_groups(u, halo, 2, False)[:1]
    else:
        edge = _edge_groups(u, halo, 2, False)
    return jnp.concatenate([u[k * SUBLANES:]] + edge, axis=0)


def _stack_rows(rows, c):
    idx = lax.broadcasted_iota(jnp.int32, (SUBLANES, c), 0)
    out = jnp.zeros((SUBLANES, c), F32)
    for i, r in enumerate(rows):
        out = out + jnp.where(idx == i, r, 0.0)
    return out


def _colsum(v):
    return jnp.sum(v, axis=0, keepdims=True)


def _sigmoid(v):
    return 0.5 * jnp.tanh(0.5 * v) + 0.5


def _rms_fwd(xv, g):
    r = lax.rsqrt(jnp.mean(xv * xv, axis=-1, keepdims=True) + RMS_EPS)
    return xv * r * g, r


def _rms_bwd(xv, g, dy):
    r = lax.rsqrt(jnp.mean(xv * xv, axis=-1, keepdims=True) + RMS_EPS)
    xn = xv * r
    dxn = dy * g
    dx = r * (dxn - xn * jnp.mean(dxn * xn, axis=-1, keepdims=True))
    return dx, dy * xn


def _dot(a, b):
    return jnp.dot(a, b, preferred_element_type=F32)


def _dot_nt(a, b):
    return lax.dot_general(a, b, (((1,), (1,)), ((), ())), preferred_element_type=F32)


def _dot_tn(a, b):
    return lax.dot_general(a, b, (((0,), (0,)), ((), ())), preferred_element_type=F32)


def _perm(dil, n, inverse=False):
    i = lax.broadcasted_iota(jnp.int32, (n, n), 0)
    j = lax.broadcasted_iota(jnp.int32, (n, n), 1)
    if inverse:
        i, j = j, i
    per = n // dil
    return (j == (i % per) * dil + i // per).astype(BF16)


def _permute_rows(pm, v):
    if v.dtype == BF16:
        return _dot(pm, v).astype(BF16)
    hi = v.astype(BF16)
    lo = (v - hi.astype(F32)).astype(BF16)
    return _dot(pm, hi) + _dot(pm, lo)


def _stream_view(a, dil):
    t, c = a.shape
    return a.reshape(dil, t // dil, c)


def _stream_spec(dil, tm, c):
    return pl.BlockSpec((dil, tm // dil, c), lambda m: (0, m, 0))


def _load_streams(ref, dil, tm):
    c = ref.shape[-1]
    if dil == 1:
        return ref[...].reshape(tm, c)
    sub = min(PERM_TOKENS, tm)
    pm = _perm(dil, sub, inverse=True)
    parts = [_permute_rows(pm, ref[:, i * (sub // dil):(i + 1) * (sub // dil), :].reshape(sub, c))
             for i in range(tm // sub)]
    return parts[0] if len(parts) == 1 else jnp.concatenate(parts, axis=0)


def _store_streams(ref, dil, tm, v):
    if dil == 1:
        ref[...] = v.reshape(ref.shape).astype(ref.dtype)
        return
    sub = min(PERM_TOKENS, tm)
    pm = _perm(dil, sub)
    for i in range(tm // sub):
        piece = _permute_rows(pm, v[i * sub:(i + 1) * sub])
        ref[:, i * (sub // dil):(i + 1) * (sub // dil), :] = piece.reshape(dil, sub // dil, -1).astype(ref.dtype)


ANY = pl.BlockSpec(memory_space=pl.ANY)


def _mesh_pos():
    return lax.axis_index("x"), lax.axis_index("y"), lax.axis_index("c")


def _dev_index(px, py, pc):
    return 4 * px + 2 * py + pc


class _Exchange:
    def __init__(self, mode, arrays, rows=None, into=()):
        self.mode, self.arrays, self.rows, self.into = mode, list(arrays), rows, list(into)
        n = len(self.arrays)
        if mode == "gather":
            self.out_shape = [jax.ShapeDtypeStruct((N_DEV,) + a.shape, a.dtype) for a in self.arrays]
        else:
            self.out_shape = [jax.ShapeDtypeStruct(a.shape, a.dtype) for a in self.arrays]
        self.scratch = [pltpu.SemaphoreType.DMA((n, N_DEV - 1)), pltpu.SemaphoreType.DMA((n, N_DEV - 1)),
                        pltpu.SemaphoreType.DMA((n,))]

    def _peers(self):
        x, y, c = _mesh_pos()
        flips = [(kx, ky, kc) for kx in (0, 1) for ky in (0, 1) for kc in (0, 1)][1:]
        peers = [(1 - x if kx else x, 1 - y if ky else y, 1 - c if kc else c) for kx, ky, kc in flips]
        return _dev_index(x, y, c), peers

    def _copy(self, ins, outs, sems, i, k, peer, me, sending):
        src = ins[i] if self.mode == "gather" else ins[i].at[_dev_index(*peer)]
        dst = outs[i].at[me if sending else _dev_index(*peer)]
        if self.rows is not None:
            src, dst = src.at[pl.ds(*self.rows)], dst.at[pl.ds(*self.rows)]
        return pltpu.make_async_remote_copy(src_ref=src, dst_ref=dst, send_sem=sems[0].at[i, k],
                                            recv_sem=sems[1].at[i, k], device_id=peer, device_id_type=MESH)

    def _own(self, ins, outs, sems, i, me):
        return pltpu.make_async_copy(ins[i], outs[i].at[me], sems[2].at[i])

    def start(self, ins, outs, sems):
        me, peers = self._peers()
        for i in range(len(ins)):
            if self.mode == "gather":
                self._own(ins, outs, sems, i, me).start()
            for k, peer in enumerate(peers):
                self._copy(ins, outs, sems, i, k, peer, me, True).start()

    def wait(self, ins, outs, sems):
        me, peers = self._peers()
        for i in range(len(ins)):
            for k, peer in enumerate(peers):
                self._copy(ins, outs, sems, i, k, peer, me, False).wait_recv()
            for k, peer in enumerate(peers):
                self._copy(ins, outs, sems, i, k, peer, me, True).wait_send()
            if self.mode == "gather":
                self._own(ins, outs, sems, i, me).wait()


def _call(body, *, name, grid, in_specs, out_specs, out_shape, args, semantics, carry=None, scratch=()):
    if carry is None:
        return pl.pallas_call(body, name=name, grid=grid, in_specs=in_specs, out_specs=out_specs,
                              out_shape=out_shape, scratch_shapes=list(scratch),
                              compiler_params=_params(*semantics))(*args)
    n_in, n_out, n_x, n_s = len(in_specs), len(out_specs), len(carry.arrays), len(scratch)
    n_into = len(carry.into)
    all_in = n_in + n_x + n_into

    def carried(*refs):
        ins, x_ins = refs[:n_in], refs[n_in:n_in + n_x]
        outs = refs[all_in:all_in + n_out]
        x_outs = refs[all_in + n_out:all_in + n_out + n_x]
        own = refs[all_in + n_out + n_x:all_in + n_out + n_x + n_s]
        sems = refs[all_in + n_out + n_x + n_s:]
        first = functools.reduce(jnp.logical_and, [pl.program_id(a) == 0 for a in range(len(grid))])
        last = functools.reduce(jnp.logical_and, [pl.program_id(a) == grid[a] - 1 for a in range(len(grid))])

        @pl.when(first)
        def _():
            carry.start(x_ins, x_outs, sems)

        body(*ins, *outs, *own)

        @pl.when(last)
        def _():
            carry.wait(x_ins, x_outs, sems)

    res = pl.pallas_call(
        carried, name=name, grid=grid, in_specs=list(in_specs) + [ANY] * (n_x + n_into),
        out_specs=list(out_specs) + [ANY] * n_x, out_shape=list(out_shape) + carry.out_shape,
        input_output_aliases={n_in + n_x + i: n_out + i for i in range(n_into)},
        scratch_shapes=list(scratch) + carry.scratch, compiler_params=_params(*["arbitrary"] * len(grid)),
    )(*args, *carry.arrays, *carry.into)
    return list(res[:n_out]), list(res[n_out:])


def _in_proj(x, g, wt, cw, carry=None):
    t, d = x.shape
    n = wt.shape[0]
    tm = min(ROWS_MATMUL, t)
    qkv0 = 3 * cw

    def body(x_ref, g_ref, wt_ref, h_ref, abcv_ref, gates_ref, *s_refs):
        h = _rms_fwd(x_ref[...], g_ref[...])[0].astype(BF16)
        h_ref[...] = h
        abcv_ref[...] = _dot_nt(h, wt_ref[0:qkv0, :]).astype(BF16)
        gates_ref[...] = _dot_nt(h, wt_ref[qkv0 + 3 * ATTN_W:n, :]).astype(BF16)
        for gi, s_ref in enumerate(s_refs):
            cols = [_dot_nt(h, wt_ref[qkv0 + j * ATTN_W + gi * GROUP_W:qkv0 + j * ATTN_W + (gi + 1) * GROUP_W, :])
                    for j in range(3)]
            _store_streams(s_ref, DILATIONS[gi], tm, jnp.concatenate(cols, axis=1).astype(BF16))

    return _call(
        body, name="in_proj", grid=(t // tm,),
        in_specs=[_rows(tm, d), _resident((1, d)), _resident((n, d))],
        out_specs=[_rows(tm, d), _rows(tm, qkv0), _rows(tm, 2 * d)]
        + [_stream_spec(dil, tm, 3 * GROUP_W) for dil in DILATIONS],
        out_shape=[jax.ShapeDtypeStruct((t, d), BF16), jax.ShapeDtypeStruct((t, qkv0), BF16),
                   jax.ShapeDtypeStruct((t, 2 * d), BF16)]
        + [jax.ShapeDtypeStruct((dil, t // dil, 3 * GROUP_W), BF16) for dil in DILATIONS],
        args=(x, g, wt), semantics=("parallel",), carry=carry)


def _head_masks():
    lane = lax.broadcasted_iota(jnp.int32, (1, GROUP_W), 1)
    return lane, [(lane // HEAD_DIM) == h for h in range(HEADS_PER_GROUP)]


def _stack_heads(v, heads):
    return jnp.concatenate([jnp.where(hm, v, jnp.zeros_like(v)) for hm in heads], axis=0)


def _merge_heads(v, heads):
    out = jnp.zeros((QBLK, GROUP_W), v.dtype)
    for h, hm in enumerate(heads):
        out = jnp.where(hm, v[h * QBLK:(h + 1) * QBLK], out)
    return out


def _pair_block(col):
    return pl.BlockSpec((2 * QBLK, GROUP_W), lambda b: (b, col))


def _edge_block(col, shift, nb):
    return pl.BlockSpec((QBLK, GROUP_W), lambda b: (jnp.clip(2 * b + shift, 0, nb - 1), col))


def _band_mask(has_prev):
    rows = HEADS_PER_GROUP * QBLK
    row = lax.broadcasted_iota(jnp.int32, (rows, 2 * QBLK), 0) & (QBLK - 1)
    col = lax.broadcasted_iota(jnp.int32, (rows, 2 * QBLK), 1)
    return ((col < QBLK) & (col >= row) & has_prev) | ((col >= QBLK) & (col - QBLK <= row))


def _next_mask(has_next):
    rows = HEADS_PER_GROUP * QBLK
    row = lax.broadcasted_iota(jnp.int32, (rows, QBLK), 0) & (QBLK - 1)
    col = lax.broadcasted_iota(jnp.int32, (rows, QBLK), 1)
    return (col >= row) & has_next


def _attn_fwd(s, dil, carry=None):
    t = s.shape[0] * s.shape[1]
    nb = t // QBLK
    per_stream = nb // dil
    assert per_stream % 2 == 0

    def body(q_ref, kc_ref, kp_ref, vc_ref, vp_ref, o_ref, lse_ref):
        b = pl.program_id(0)
        _, heads = _head_masks()
        first_has_prev = lax.rem(2 * b, per_stream) != 0
        for j in range(2):
            rows = slice(j * QBLK, (j + 1) * QBLK)
            if j == 0:
                k2 = jnp.concatenate([kp_ref[...], kc_ref[rows, :]], axis=0)
                v2 = jnp.concatenate([vp_ref[...], vc_ref[rows, :]], axis=0)
            else:
                k2, v2 = kc_ref[...], vc_ref[...]
            mask = _band_mask(first_has_prev if j == 0 else True)
            sc = jnp.where(mask, _dot_nt(_stack_heads(q_ref[rows, :], heads), k2) * ATTN_SCALE, NEG_INF)
            mx = jnp.max(sc, axis=1, keepdims=True)
            pr = jnp.exp(sc - mx)
            den = jnp.sum(pr, axis=1, keepdims=True)
            o_all = _dot(pr.astype(BF16), v2) / den
            o_ref[rows, :] = _merge_heads(o_all, heads).astype(BF16)
            lse_ref[rows, :] = _merge_heads(jnp.broadcast_to(mx + jnp.log(den), o_all.shape), heads)

    sv = s.reshape(t, 3 * GROUP_W)
    return _call(
        body, name=f"attn_fwd_d{dil}", grid=(nb // 2,),
        in_specs=[_pair_block(0), _pair_block(1), _edge_block(1, -1, nb), _pair_block(2), _edge_block(2, -1, nb)],
        out_specs=[_pair_block(0), _pair_block(0)],
        out_shape=[jax.ShapeDtypeStruct((t, GROUP_W), BF16), jax.ShapeDtypeStruct((t, GROUP_W), F32)],
        args=(sv, sv, sv, sv, sv), semantics=("parallel",), carry=carry)


def _group_softmax(parts):
    mx = jnp.maximum(jnp.maximum(parts[0], parts[1]), parts[2])
    es = [jnp.exp(p - mx) for p in parts]
    den = es[0] + es[1] + es[2]
    return [e / den for e in es]


def _mixer_out(x, abcv, gates, os, lses, conv_w, conv_b, b_gate, w_pa, w_pb, w_o, carry=None):
    t, d = x.shape
    cw = conv_w.shape[1]
    tm = min(ROWS_MATMUL, t)

    def body(x_ref, abcv_ref, halo_ref, gates_ref, o0_ref, o1_ref, o2_ref, l0_ref, l1_ref, l2_ref, cw_ref, cb_ref,
             bg_ref, wpa_ref, wpb_ref, wo_ref, x1_ref, ya_ref, yb_ref, yap_ref, ybp_ref, mg_ref):
        m = pl.program_id(0)
        ab = abcv_ref[:, 0:cw].astype(F32)
        u = abcv_ref[:, cw:2 * cw].astype(F32) * abcv_ref[:, 2 * cw:3 * cw].astype(F32)
        hu = halo_ref[:, cw:2 * cw].astype(F32) * halo_ref[:, 2 * cw:3 * cw].astype(F32)
        hu = jnp.where(m > 0, hu, 0.0)
        cv = (cw_ref[0:1, :] * _shift_down(u, hu, 2) + cw_ref[1:2, :] * _shift_down(u, hu, 1)
              + cw_ref[2:3, :] * u + cb_ref[...])
        ya = (ab * cv).astype(BF16)
        ya_ref[...] = ya
        alphas = _group_softmax([_load_streams(r, dil, tm) for r, dil in zip((l0_ref, l1_ref, l2_ref), DILATIONS)])
        for i, (o_ref, dil) in enumerate(zip((o0_ref, o1_ref, o2_ref), DILATIONS)):
            sl = slice(i * GROUP_W, (i + 1) * GROUP_W)
            yb_ref[:, sl] = (alphas[i] * _load_streams(o_ref, dil, tm).astype(F32)).astype(BF16)
        yap = _dot_nt(ya, wpa_ref[...])
        ybp = _dot_nt(yb_ref[...], wpb_ref[...])
        yap_ref[...] = yap.astype(BF16)
        ybp_ref[...] = ybp.astype(BF16)
        sa = _sigmoid(gates_ref[:, 0:d].astype(F32) + bg_ref[0:1, :])
        sb = _sigmoid(gates_ref[:, d:2 * d].astype(F32) + bg_ref[1:2, :])
        merged = (sa * yap + sb * ybp).astype(BF16)
        mg_ref[...] = merged
        x1_ref[...] = x_ref[...] + _dot(merged, wo_ref[...])

    return _call(
        body, name="mixer_out", grid=(t // tm,),
        in_specs=[_rows(tm, d), _rows(tm, 3 * cw), _prev_halo(tm, 3 * cw), _rows(tm, 2 * d)]
        + [_stream_spec(dil, tm, GROUP_W) for dil in DILATIONS] * 2
        + [_resident((3, cw)), _resident((1, cw)), _resident((2, d)),
           _resident((d, cw)), _resident((d, ATTN_W)), _resident((d, d))],
        out_specs=[_rows(tm, d), _rows(tm, cw), _rows(tm, ATTN_W), _rows(tm, d), _rows(tm, d), _rows(tm, d)],
        out_shape=[jax.ShapeDtypeStruct((t, d), F32), jax.ShapeDtypeStruct((t, cw), BF16),
                   jax.ShapeDtypeStruct((t, ATTN_W), BF16), jax.ShapeDtypeStruct((t, d), BF16),
                   jax.ShapeDtypeStruct((t, d), BF16), jax.ShapeDtypeStruct((t, d), BF16)],
        args=(x, abcv, abcv, gates, *[_stream_view(a, dil) for a, dil in zip(os, DILATIONS)],
              *[_stream_view(a, dil) for a, dil in zip(lses, DILATIONS)], conv_w, conv_b, b_gate, w_pa, w_pb, w_o),
        semantics=("parallel",), carry=carry)


def _ffn_fwd(x1, target, g2, w_ut, conv_w, conv_b, w_d, g_f, carry=None):
    t, d = x1.shape
    dff = w_d.shape[0]
    tm = min(256, t)
    ck = _pick_tile(dff, 2816)

    def body(x1_ref, tg_ref, g2_ref, wut_ref, cw_ref, cb_ref, wd_ref, gf_ref, h2_ref, up_ref, act_ref, conv_ref,
             dx2_ref, dx2i_ref, acc_ref, loss_ref, halo_ref):
        m = pl.program_id(0)

        @pl.when(m == 0)
        def _():
            acc_ref[...] = jnp.zeros_like(acc_ref)
            loss_ref[...] = jnp.zeros_like(loss_ref)
            halo_ref[...] = jnp.zeros_like(halo_ref)

        h2 = _permute_rows(_interleave(tm), _rms_fwd(x1_ref[...], g2_ref[...])[0].astype(BF16))
        h2_ref[...] = h2

        def conv(c0):
            p = _dot_nt(h2, wut_ref[c0:c0 + ck, :])
            up_ref[:, c0:c0 + ck] = p.astype(BF16)
            hp = halo_ref[:, c0:c0 + ck]
            halo_ref[:, c0:c0 + ck] = p[tm - HALO:, :]
            return (cw_ref[0:1, c0:c0 + ck] * _shift_down_il(p, hp, 2)
                    + cw_ref[1:2, c0:c0 + ck] * _shift_down_il(p, hp, 1)
                    + cw_ref[2:3, c0:c0 + ck] * p + cb_ref[:, c0:c0 + ck])

        down = jnp.zeros((tm, d), F32)
        for c0 in range(0, dff, ck):
            gate = conv(c0)
            val = conv(dff + c0)
            conv_ref[:, c0:c0 + ck] = gate.astype(BF16)
            conv_ref[:, dff + c0:dff + c0 + ck] = val.astype(BF16)
            act = (gate * _sigmoid(gate) * val).astype(BF16)
            act_ref[:, c0:c0 + ck] = act
            down = down + _dot(act, wd_ref[c0:c0 + ck, :])
        x2 = x1_ref[...] + _permute_rows(_interleave(tm, inverse=True), down)
        y, _ = _rms_fwd(x2, gf_ref[...])
        diff = y - tg_ref[...]
        loss_ref[...] += 0.5 * jnp.sum(jnp.mean(diff * diff, axis=-1, keepdims=True))
        dx2, dg = _rms_bwd(x2, gf_ref[...], diff * (1.0 / d))
        dx2_ref[...] = dx2
        dx2i_ref[...] = _permute_rows(_interleave(tm), dx2.astype(BF16))
        acc_ref[...] += _stack_rows([_colsum(dg)], d)

    return _call(
        body, name="ffn_fwd", grid=(t // tm,),
        in_specs=[_rows(tm, d), _rows(tm, d), _resident((1, d)), _resident((2 * dff, d)), _resident((3, 2 * dff)),
                  _resident((1, 2 * dff)), _resident((dff, d)), _resident((1, d))],
        out_specs=[_rows(tm, d), _rows(tm, 2 * dff), _rows(tm, dff), _rows(tm, 2 * dff), _rows(tm, d), _rows(tm, d),
                   _acc_spec(d), _acc_spec(LANES)],
        out_shape=[jax.ShapeDtypeStruct((t, d), BF16), jax.ShapeDtypeStruct((t, 2 * dff), BF16),
                   jax.ShapeDtypeStruct((t, dff), BF16), jax.ShapeDtypeStruct((t, 2 * dff), BF16),
                   jax.ShapeDtypeStruct((t, d), F32), jax.ShapeDtypeStruct((t, d), BF16),
                   jax.ShapeDtypeStruct((SUBLANES, d), F32), jax.ShapeDtypeStruct((SUBLANES, LANES), F32)],
        args=(x1, target, g2, w_ut, conv_w, conv_b, w_d, g_f), semantics=("arbitrary",), carry=carry,
        scratch=[pltpu.VMEM((HALO, 2 * dff), F32)])


def _ffn_act_bwd(dx2, conv, w_d):
    t, d = dx2.shape
    dff = w_d.shape[0]
    tm = min(256, t)
    ck = _pick_tile(dff, 2816)

    def body(dx2_ref, conv_ref, wd_ref, dup_ref, acc_ref):
        m = pl.program_id(0)

        @pl.when(m == 0)
        def _():
            acc_ref[...] = jnp.zeros_like(acc_ref)

        dx2v = dx2_ref[...]
        for c0 in range(0, dff, ck):
            dact = _dot_nt(dx2v, wd_ref[c0:c0 + ck, :])
            gate = conv_ref[:, c0:c0 + ck].astype(F32)
            val = conv_ref[:, dff + c0:dff + c0 + ck].astype(F32)
            sg = _sigmoid(gate)
            dval = dact * gate * sg
            dgate = dact * val * sg * (1.0 + gate * (1.0 - sg))
            dup_ref[:, c0:c0 + ck] = dgate.astype(BF16)
            dup_ref[:, dff + c0:dff + c0 + ck] = dval.astype(BF16)
            acc_ref[:, c0:c0 + ck] += _stack_rows([_colsum(dgate)], ck)
            acc_ref[:, dff + c0:dff + c0 + ck] += _stack_rows([_colsum(dval)], ck)

    return pl.pallas_call(
        body, name="ffn_act_bwd", grid=(t // tm,),
        in_specs=[_rows(tm, d), _rows(tm, 2 * dff), _resident((dff, d))],
        out_specs=[_rows(tm, 2 * dff), _acc_spec(2 * dff)],
        out_shape=[jax.ShapeDtypeStruct((t, 2 * dff), BF16), jax.ShapeDtypeStruct((SUBLANES, 2 * dff), F32)],
        compiler_params=_params("arbitrary"),
    )(dx2, conv, w_d)


def _ffn_up_bwd(dup, up_pre, x1, dx2, conv_w, w_u, g2, carry=None):
    t, d = x1.shape
    n = dup.shape[1]
    tm = min(256, t)
    ck = _pick_tile(n, 256)
    last = t // tm - 1

    def body(dup_ref, nxt_ref, up_ref, x1_ref, dx2_ref, cw_ref, wu_ref, g2_ref, dpre_ref, dx1_ref, acc_ref, accw_ref):
        m = pl.program_id(0)

        @pl.when(m == 0)
        def _():
            acc_ref[...] = jnp.zeros_like(acc_ref)
            accw_ref[...] = jnp.zeros_like(accw_ref)

        dh = jnp.zeros((tm, d), F32)
        for c0 in range(0, n, ck):
            du = dup_ref[:, c0:c0 + ck].astype(F32)
            hn = jnp.where(m < last, nxt_ref[:, c0:c0 + ck].astype(F32), 0.0)
            du1 = _shift_up_il(du, hn, 1)
            du2 = _shift_up_il(du, hn, 2)
            dpre = (cw_ref[2:3, c0:c0 + ck] * du + cw_ref[1:2, c0:c0 + ck] * du1
                    + cw_ref[0:1, c0:c0 + ck] * du2).astype(BF16)
            dpre_ref[:, c0:c0 + ck] = dpre
            dh = dh + _dot(dpre, wu_ref[c0:c0 + ck, :])
            p = up_ref[:, c0:c0 + ck].astype(F32)
            accw_ref[:, c0:c0 + ck] += _stack_rows([_colsum(du2 * p), _colsum(du1 * p), _colsum(du * p)], ck)
        dh = _permute_rows(_interleave(tm, inverse=True), dh)
        dx, dg = _rms_bwd(x1_ref[...], g2_ref[...], dh)
        dx1_ref[...] = dx2_ref[...] + dx
        acc_ref[...] += _stack_rows([_colsum(dg)], d)

    return _call(
        body, name="ffn_up_bwd", grid=(t // tm,),
        in_specs=[_rows(tm, n), _next_halo(tm, n, t), _rows(tm, n), _rows(tm, d), _rows(tm, d), _resident((3, n)),
                  _resident((n, d)), _resident((1, d))],
        out_specs=[_rows(tm, n), _rows(tm, d), _acc_spec(d), _acc_spec(n)],
        out_shape=[jax.ShapeDtypeStruct((t, n), BF16), jax.ShapeDtypeStruct((t, d), F32),
                   jax.ShapeDtypeStruct((SUBLANES, d), F32), jax.ShapeDtypeStruct((SUBLANES, n), F32)],
        args=(dup, dup, up_pre, x1, dx2, conv_w, w_u, g2), semantics=("arbitrary",), carry=carry)


def _tn_matmul(a, b, name):
    t, mdim = a.shape
    n = b.shape[1]
    tk = min(1024, t)
    tmm = _pick_tile(mdim, 1536)
    tn = _pick_tile(n, 1024)

    def body(a_ref, b_ref, o_ref, acc_ref):
        k = pl.program_id(2)

        @pl.when(k == 0)
        def _():
            acc_ref[...] = jnp.zeros_like(acc_ref)

        acc_ref[...] += _dot_tn(a_ref[...].astype(BF16), b_ref[...].astype(BF16))

        @pl.when(k == t // tk - 1)
        def _():
            o_ref[...] = acc_ref[...].astype(BF16)

    return pl.pallas_call(
        body, name=name, grid=(mdim // tmm, n // tn, t // tk),
        in_specs=[pl.BlockSpec((tk, tmm), lambda i, j, k: (k, i)), pl.BlockSpec((tk, tn), lambda i, j, k: (k, j))],
        out_specs=pl.BlockSpec((tmm, tn), lambda i, j, k: (i, j)),
        out_shape=jax.ShapeDtypeStruct((mdim, n), BF16),
        scratch_shapes=[pltpu.VMEM((tmm, tn), F32)],
        compiler_params=_params("parallel", "parallel", "arbitrary"),
    )(a, b)


def _mixer_bwd(dx1, gates, yap, ybp, os, lses, b_gate, w_o, w_pa, w_pb):
    t, d = dx1.shape
    cw = w_pa.shape[1]
    tm = min(ROWS_MATMUL, t)

    def body(dx1_ref, gates_ref, yap_ref, ybp_ref, o0_ref, o1_ref, o2_ref, l0_ref, l1_ref, l2_ref, bg_ref, wo_ref,
             wpa_ref, wpb_ref, dgates_ref, dyap_ref, dybp_ref, dya_ref, do0_ref, do1_ref, do2_ref, dl0_ref, dl1_ref,
             dl2_ref, acc_ref):
        m = pl.program_id(0)

        @pl.when(m == 0)
        def _():
            acc_ref[...] = jnp.zeros_like(acc_ref)

        dmg = _dot_nt(dx1_ref[...].astype(BF16), wo_ref[...])
        sa = _sigmoid(gates_ref[:, 0:d].astype(F32) + bg_ref[0:1, :])
        sb = _sigmoid(gates_ref[:, d:2 * d].astype(F32) + bg_ref[1:2, :])
        dyap = (dmg * sa).astype(BF16)
        dybp = (dmg * sb).astype(BF16)
        dga = dmg * yap_ref[...].astype(F32) * sa * (1.0 - sa)
        dgb = dmg * ybp_ref[...].astype(F32) * sb * (1.0 - sb)
        dyap_ref[...] = dyap
        dybp_ref[...] = dybp
        dgates_ref[:, 0:d] = dga.astype(BF16)
        dgates_ref[:, d:2 * d] = dgb.astype(BF16)
        acc_ref[...] += _stack_rows([_colsum(dga), _colsum(dgb)], d)
        dya_ref[...] = _dot(dyap, wpa_ref[...]).astype(BF16)
        dyb = _dot(dybp, wpb_ref[...])

        ri = lax.broadcasted_iota(jnp.int32, (GROUP_W, GROUP_W), 0) // HEAD_DIM
        ci = lax.broadcasted_iota(jnp.int32, (GROUP_W, GROUP_W), 1) // HEAD_DIM
        same_head = (ri == ci).astype(BF16)
        alphas = _group_softmax([_load_streams(r, dil, tm) for r, dil in zip((l0_ref, l1_ref, l2_ref), DILATIONS)])
        prod = jnp.zeros((tm, GROUP_W), F32)
        for i, (o_ref, do_ref, dil) in enumerate(zip((o0_ref, o1_ref, o2_ref), (do0_ref, do1_ref, do2_ref), DILATIONS)):
            dov = alphas[i] * dyb[:, i * GROUP_W:(i + 1) * GROUP_W]
            _store_streams(do_ref, dil, tm, dov.astype(BF16))
            prod = prod + dov * _load_streams(o_ref, dil, tm).astype(F32)
        hi = prod.astype(BF16)
        lo = (prod - hi.astype(F32)).astype(BF16)
        dtot = _dot(hi, same_head) + _dot(lo, same_head)
        for alpha, dl_ref, dil in zip(alphas, (dl0_ref, dl1_ref, dl2_ref), DILATIONS):
            _store_streams(dl_ref, dil, tm, alpha * dtot)

    streams = [_stream_spec(dil, tm, GROUP_W) for dil in DILATIONS]
    res = _call(
        body, name="mixer_bwd", grid=(t // tm,),
        in_specs=[_rows(tm, d), _rows(tm, 2 * d), _rows(tm, d), _rows(tm, d)] + streams * 2
        + [_resident((2, d)), _resident((d, d)), _resident((d, cw)), _resident((d, ATTN_W))],
        out_specs=[_rows(tm, 2 * d), _rows(tm, d), _rows(tm, d), _rows(tm, cw)] + streams * 2 + [_acc_spec(d)],
        out_shape=[jax.ShapeDtypeStruct((t, 2 * d), BF16), jax.ShapeDtypeStruct((t, d), BF16),
                   jax.ShapeDtypeStruct((t, d), BF16), jax.ShapeDtypeStruct((t, cw), BF16)]
        + [jax.ShapeDtypeStruct((dil, t // dil, GROUP_W), BF16) for dil in DILATIONS]
        + [jax.ShapeDtypeStruct((dil, t // dil, GROUP_W), F32) for dil in DILATIONS]
        + [jax.ShapeDtypeStruct((SUBLANES, d), F32)],
        args=(dx1, gates, yap, ybp, *[_stream_view(a, dil) for a, dil in zip(os, DILATIONS)],
              *[_stream_view(a, dil) for a, dil in zip(lses, DILATIONS)], b_gate, w_o, w_pa, w_pb),
        semantics=("arbitrary",))
    dgates, dyap, dybp, dya = res[:4]
    dos = [a.reshape(t, GROUP_W) for a in res[4:7]]
    dls = [a.reshape(t, GROUP_W) for a in res[7:10]]
    return dgates, dyap, dybp, dya, dos, dls, res[10]


def _attn_bwd(s, do, lse, dl, dil, carry=None):
    t = s.shape[0] * s.shape[1]
    nb = t // QBLK
    per_stream = nb // dil

    def body(q_ref, qn_ref, kc_ref, kp_ref, vc_ref, vp_ref, do_ref, don_ref, lse_ref, lsen_ref, dl_ref, dln_ref,
             ds_ref):
        b = pl.program_id(0)
        lane, heads = _head_masks()
        first_has_prev = lax.rem(2 * b, per_stream) != 0
        last_has_next = lax.rem(2 * b + 2, per_stream) != 0

        def cols(v):
            return jnp.concatenate([jnp.sum(jnp.where(lane == h * HEAD_DIM, v, 0.0), axis=1, keepdims=True)
                                    for h in range(HEADS_PER_GROUP)], axis=0)

        def pair(qs, dos, k, v, valid, lse_c, dl_c):
            s = jnp.where(valid, _dot_nt(qs, k) * ATTN_SCALE, NEG_INF)
            p = jnp.exp(s - lse_c)
            ds = p * (_dot_nt(dos, v) - dl_c)
            return p.astype(BF16), ds.astype(BF16)

        lo, hi = slice(0, QBLK), slice(QBLK, 2 * QBLK)
        for j, rows in enumerate((lo, hi)):
            q, do, lse, dl = q_ref[rows, :], do_ref[rows, :], lse_ref[rows, :], dl_ref[rows, :]
            kc, vc = kc_ref[rows, :], vc_ref[rows, :]
            if j == 0:
                k2 = jnp.concatenate([kp_ref[...], kc], axis=0)
                v2 = jnp.concatenate([vp_ref[...], vc], axis=0)
                qn, don, lsen, dln = q_ref[hi, :], do_ref[hi, :], lse_ref[hi, :], dl_ref[hi, :]
                mask, mask_n = _band_mask(first_has_prev), _next_mask(True)
            else:
                k2, v2 = kc_ref[...], vc_ref[...]
                qn, don, lsen, dln = qn_ref[...], don_ref[...], lsen_ref[...], dln_ref[...]
                mask, mask_n = _band_mask(True), _next_mask(last_has_next)
            qs, qns = _stack_heads(q, heads), _stack_heads(qn, heads)
            dos, dons = _stack_heads(do, heads), _stack_heads(don, heads)
            p_q, ds_q = pair(qs, dos, k2, v2, mask, cols(lse), cols(dl))
            p_n, ds_n = pair(qns, dons, kc, vc, mask_n, cols(lsen), cols(dln))
            dq = _merge_heads(_dot(ds_q, k2), heads)
            dk = _dot_tn(jnp.concatenate([ds_q[:, QBLK:], ds_n], axis=0), jnp.concatenate([qs, qns], axis=0))
            dv = _dot_tn(jnp.concatenate([p_q[:, QBLK:], p_n], axis=0), jnp.concatenate([dos, dons], axis=0))
            ds_ref[rows, 0:GROUP_W] = (dq * ATTN_SCALE).astype(BF16)
            ds_ref[rows, GROUP_W:2 * GROUP_W] = (dk * ATTN_SCALE).astype(BF16)
            ds_ref[rows, 2 * GROUP_W:3 * GROUP_W] = dv.astype(BF16)

    sv = s.reshape(t, 3 * GROUP_W)
    cur, nxt = _pair_block(0), _edge_block(0, 2, nb)
    return _call(
        body, name=f"attn_bwd_d{dil}", grid=(nb // 2,),
        in_specs=[cur, nxt, _pair_block(1), _edge_block(1, -1, nb), _pair_block(2), _edge_block(2, -1, nb),
                  cur, nxt, cur, nxt, cur, nxt],
        out_specs=[pl.BlockSpec((2 * QBLK, 3 * GROUP_W), lambda b: (b, 0))],
        out_shape=[jax.ShapeDtypeStruct((t, 3 * GROUP_W), BF16)],
        args=(sv, sv, sv, sv, sv, sv, do, do, lse, lse, dl, dl), semantics=("parallel",), carry=carry)


def _conv_mixer_bwd(abcv, dya, conv_w, conv_b):
    t = abcv.shape[0]
    cw = conv_w.shape[1]
    tm = min(256, t)
    last = t // tm - 1

    def body(a_ref, ap_ref, an_ref, dya_ref, dyan_ref, cw_ref, cb_ref, d_ref, acc_ref):
        m = pl.program_id(0)

        @pl.when(m == 0)
        def _():
            acc_ref[...] = jnp.zeros_like(acc_ref)

        ab = a_ref[:, 0:cw].astype(F32)
        ac = a_ref[:, cw:2 * cw].astype(F32)
        av = a_ref[:, 2 * cw:3 * cw].astype(F32)
        u = ac * av
        hu = ap_ref[:, cw:2 * cw].astype(F32) * ap_ref[:, 2 * cw:3 * cw].astype(F32)
        hu = jnp.where(m > 0, hu, 0.0)
        u1 = _shift_down(u, hu, 1)
        u2 = _shift_down(u, hu, 2)
        cv = cw_ref[0:1, :] * u2 + cw_ref[1:2, :] * u1 + cw_ref[2:3, :] * u + cb_ref[...]
        dya_v = dya_ref[...].astype(F32)
        dcv = dya_v * ab
        ndcv = jnp.where(m < last, dyan_ref[...].astype(F32) * an_ref[:, 0:cw].astype(F32), 0.0)
        du = (cw_ref[2:3, :] * dcv + cw_ref[1:2, :] * _shift_up(dcv, ndcv, 1)
              + cw_ref[0:1, :] * _shift_up(dcv, ndcv, 2))
        d_ref[:, 0:cw] = (dya_v * cv).astype(BF16)
        d_ref[:, cw:2 * cw] = (du * av).astype(BF16)
        d_ref[:, 2 * cw:3 * cw] = (du * ac).astype(BF16)
        acc_ref[...] += _stack_rows([_colsum(dcv * u2), _colsum(dcv * u1), _colsum(dcv * u), _colsum(dcv)], cw)

    return pl.pallas_call(
        body, name="conv_mixer_bwd", grid=(t // tm,),
        in_specs=[_rows(tm, 3 * cw), _prev_halo(tm, 3 * cw), _next_halo(tm, 3 * cw, t), _rows(tm, cw),
                  _next_halo(tm, cw, t), _resident((3, cw)), _resident((1, cw))],
        out_specs=[_rows(tm, 3 * cw), _acc_spec(cw)],
        out_shape=[jax.ShapeDtypeStruct((t, 3 * cw), BF16), jax.ShapeDtypeStruct((SUBLANES, cw), F32)],
        compiler_params=_params("arbitrary"),
    )(abcv, abcv, abcv, dya, dya, conv_w, conv_b)


def _in_proj_bwd(x, dx1, dabcv, dss, dgates, w_in, g1, carry=None):
    t, d = x.shape
    qkv0 = dabcv.shape[1]
    n = w_in.shape[0]
    tm = min(ROWS_MATMUL, t)

    def body(x_ref, dx1_ref, da_ref, ds0_ref, ds1_ref, ds2_ref, dg_ref, w_ref, g_ref, dx_ref, acc_ref):
        m = pl.program_id(0)

        @pl.when(m == 0)
        def _():
            acc_ref[...] = jnp.zeros_like(acc_ref)

        dh = _dot(da_ref[...], w_ref[0:qkv0, :]) + _dot(dg_ref[...], w_ref[qkv0 + 3 * ATTN_W:n, :])
        for gi, (ds_ref, dil) in enumerate(zip((ds0_ref, ds1_ref, ds2_ref), DILATIONS)):
            ds = _load_streams(ds_ref, dil, tm)
            for j in range(3):
                c0 = qkv0 + j * ATTN_W + gi * GROUP_W
                dh = dh + _dot(ds[:, j * GROUP_W:(j + 1) * GROUP_W], w_ref[c0:c0 + GROUP_W, :])
        dx, dg = _rms_bwd(x_ref[...], g_ref[...], dh)
        dx_ref[...] = dx1_ref[...] + dx
        acc_ref[...] += _stack_rows([_colsum(dg)], d)

    return _call(
        body, name="in_proj_bwd", grid=(t // tm,),
        in_specs=[_rows(tm, d), _rows(tm, d), _rows(tm, qkv0)]
        + [_stream_spec(dil, tm, 3 * GROUP_W) for dil in DILATIONS]
        + [_rows(tm, 2 * d), _resident((n, d)), _resident((1, d))],
        out_specs=[_rows(tm, d), _acc_spec(d)],
        out_shape=[jax.ShapeDtypeStruct((t, d), F32), jax.ShapeDtypeStruct((SUBLANES, d), F32)],
        args=(x, dx1, dabcv, *[_stream_view(a, dil) for a, dil in zip(dss, DILATIONS)], dgates, w_in, g1),
        semantics=("arbitrary",), carry=carry)


def _dw_in_qkv(ds, h, dil):
    t, d = h.shape
    tk = min(1024, t)
    sub = min(256, t)
    width = 3 * GROUP_W

    def body(ds_ref, h_ref, o_ref, acc_ref):
        k = pl.program_id(0)

        @pl.when(k == 0)
        def _():
            acc_ref[...] = jnp.zeros_like(acc_ref)

        upd = None
        for i in range(tk // sub):
            rows = ds_ref[:, i * (sub // dil):(i + 1) * (sub // dil), :].reshape(sub, width)
            if dil > 1:
                rows = _permute_rows(_perm(dil, sub, inverse=True), rows)
            term = _dot_tn(rows, h_ref[i * sub:(i + 1) * sub, :])
            upd = term if upd is None else upd + term
        acc_ref[...] += upd

        @pl.when(k == t // tk - 1)
        def _():
            o_ref[...] = acc_ref[...].astype(BF16)

    return pl.pallas_call(
        body, name=f"dw_in_qkv_d{dil}", grid=(t // tk,),
        in_specs=[_stream_spec(dil, tk, width), _rows(tk, d)],
        out_specs=pl.BlockSpec((width, d), lambda k: (0, 0)),
        out_shape=jax.ShapeDtypeStruct((width, d), BF16),
        scratch_shapes=[pltpu.VMEM((width, d), F32)],
        compiler_params=_params("arbitrary"),
    )(_stream_view(ds, dil), h)


def _local_step(x, target, p, late):
    cw = p["conv_a_w"].shape[1]
    (h, abcv, gates, *ss), (g_up,) = _in_proj(x, p["norm_mix_g"], p["w_in"], cw,
                                              carry=_Exchange("gather", [late["w_up"]]))
    w_up = _full_from_gathered(g_up)
    (o0, lse0), g_proj = _attn_fwd(ss[0], DILATIONS[0],
                                   carry=_Exchange("gather", [late["w_proj_a"], late["w_proj_b"]]))
    (o1, lse1), (g_out,) = _attn_fwd(ss[1], DILATIONS[1], carry=_Exchange("gather", [late["w_out"]]))
    o2, lse2 = _attn_fwd(ss[2], DILATIONS[2])
    w_pa, w_pb, w_out = [_full_from_gathered(g) for g in (*g_proj, g_out)]
    os, lses = (o0, o1, o2), (lse0, lse1, lse2)
    (x1, ya, yb, yap, ybp, merged), (g_down,) = _mixer_out(
        x, abcv, gates, os, lses, p["conv_a_w"], p["conv_a_b"], p["b_gate"], w_pa, w_pb, w_out,
        carry=_Exchange("gather", [late["w_down"]]))
    w_down = _full_from_gathered(g_down)
    h2, up_pre, act, conv, dx2, dx2i, acc_gf, loss = _ffn_fwd(x1, target, p["norm_ffn_g"], w_up, p["ffn_conv_w"],
                                                              p["ffn_conv_b"], w_down, p["final_norm_g"])

    parts, got = {}, {}
    dup, acc_fb = _ffn_act_bwd(dx2i, conv, w_down)
    parts["w_down"] = _by_destination(_tn_matmul(act, dx2i, "dw_down"))
    (dpre, dx1, acc_g2, acc_fw), (got["w_down"],) = _ffn_up_bwd(dup, up_pre, x1, dx2, p["ffn_conv_w"], w_up,
                                                                p["norm_ffn_g"],
                                                                carry=_Exchange("scatter", [parts["w_down"]]))
    parts["w_up"] = _by_destination(_tn_matmul(dpre, h2, "dw_up"))
    dgates, dyap, dybp, dya, dos, dls, acc_bg = _mixer_bwd(dx1, gates, yap, ybp, os, lses, p["b_gate"], w_out,
                                                           w_pa, w_pb)
    parts["w_out"] = _by_destination(_tn_matmul(merged, dx1, "dw_out"))
    parts["w_proj_a"] = _by_destination(_tn_matmul(dyap, ya, "dw_proj_a"))
    parts["w_proj_b"] = _by_destination(_tn_matmul(dybp, yb, "dw_proj_b"))
    minor = ("w_out", "w_proj_a", "w_proj_b")
    half = parts["w_up"].shape[1] // 2
    (ds0,), received = _attn_bwd(ss[0], dos[0], lses[0], dls[0], DILATIONS[0],
                                 carry=_Exchange("scatter", [parts[n] for n in minor]))
    got.update(zip(minor, received))
    (ds1,), first_half = _attn_bwd(ss[1], dos[1], lses[1], dls[1], DILATIONS[1],
                                   carry=_Exchange("scatter", [parts["w_up"]], rows=(0, half)))
    (ds2,), (got["w_up"],) = _attn_bwd(ss[2], dos[2], lses[2], dls[2], DILATIONS[2],
                                       carry=_Exchange("scatter", [parts["w_up"]], rows=(half, half),
                                                       into=first_half))
    dss = [ds0, ds1, ds2]
    dabcv, acc_ca = _conv_mixer_bwd(abcv, dya, p["conv_a_w"], p["conv_a_b"])
    dw_s = [_dw_in_qkv(ds, h, dil) for ds, dil in zip(dss, DILATIONS)]
    dw_qkv = [w[j * GROUP_W:(j + 1) * GROUP_W] for j in range(3) for w in dw_s]
    g_w_in = jnp.concatenate([_tn_matmul(dabcv, h, "dw_in_a"), *dw_qkv, _tn_matmul(dgates, h, "dw_in_g")], axis=0)
    parts["w_in"] = _by_destination(g_w_in)
    (dx, acc_g1), (got["w_in"],) = _in_proj_bwd(x, dx1, dabcv, dss, dgates, p["w_in"], p["norm_mix_g"],
                                                carry=_Exchange("scatter", [parts["w_in"]]))
    small = dict(norm_mix_g=acc_g1[0:1], b_gate=acc_bg[0:2], conv_a_w=acc_ca[0:3], conv_a_b=acc_ca[3:4],
                 norm_ffn_g=acc_g2[0:1], ffn_conv_w=acc_fw[0:3], ffn_conv_b=acc_fb[0:1], final_norm_g=acc_gf[0:1])
    return loss[0, 0], dx, parts, got, small


def _all_gather(shards):
    n = len(shards)

    def body(*refs):
        ins, outs = refs[:n], refs[n:2 * n]
        send_sems, recv_sems, local_sems = refs[2 * n:]
        x, y, c = _mesh_pos()
        me, sibling = (x, y, c), (x, y, 1 - c)
        chips = [(1 - x, y), (x, 1 - y), (1 - x, 1 - y)]

        def copy(i, k, block, to, src=None):
            rows = outs[i].at[_dev_index(*block)]
            return pltpu.make_async_remote_copy(
                src_ref=rows if src is None else src, dst_ref=rows, send_sem=send_sems.at[i, k],
                recv_sem=recv_sems.at[i, k], device_id=to, device_id_type=MESH)

        mine, first, passed = [], [], []
        for i in range(n):
            cp = pltpu.make_async_copy(ins[i], outs[i].at[_dev_index(*me)], local_sems.at[i])
            cp.start()
            mine.append(cp)
            first.append(copy(i, 0, me, sibling, src=ins[i]))
            first += [copy(i, 1 + j, me, (*chip, c), src=ins[i]) for j, chip in enumerate(chips)]
        for cp in first:
            cp.start()
        for i in range(n):
            for j, chip in enumerate(chips):
                copy(i, 1 + j, (*chip, c), me).wait_recv()
                fw = copy(i, 4 + j, (*chip, c), sibling)
                fw.start()
                passed.append(fw)
        for i in range(n):
            copy(i, 0, sibling, me).wait_recv()
            for j, chip in enumerate(chips):
                copy(i, 4 + j, (*chip, 1 - c), me).wait_recv()
        for cp in first + passed:
            cp.wait_send()
        for cp in mine:
            cp.wait()

    return pl.pallas_call(
        body, name="all_gather_weights",
        in_specs=[ANY] * n, out_specs=[ANY] * n,
        out_shape=[jax.ShapeDtypeStruct((N_DEV,) + s.shape, s.dtype) for s in shards],
        scratch_shapes=[pltpu.SemaphoreType.DMA((n, 7)), pltpu.SemaphoreType.DMA((n, 7)),
                        pltpu.SemaphoreType.DMA((n,))],
    )(*shards)


def _all_reduce_small(v):
    r = v.shape[0]

    def body(v_ref, o_ref, gath, send_sems, recv_sems):
        x, y, c = _mesh_pos()
        me = _dev_index(x, y, c)
        gath[me] = v_ref[...]
        flips = [(kx, ky, kc) for kx in (0, 1) for ky in (0, 1) for kc in (0, 1)][1:]
        copies = []
        for k, (kx, ky, kc) in enumerate(flips):
            px = 1 - x if kx else x
            py = 1 - y if ky else y
            pc = 1 - c if kc else c
            cp = pltpu.make_async_remote_copy(
                src_ref=v_ref, dst_ref=gath.at[me], send_sem=send_sems.at[k], recv_sem=recv_sems.at[k],
                device_id=(px, py, pc), device_id_type=MESH)
            cp.start()
            copies.append((cp, _dev_index(px, py, pc)))
        for k, (cp, peer) in enumerate(copies):
            pltpu.make_async_remote_copy(
                src_ref=v_ref, dst_ref=gath.at[peer], send_sem=send_sems.at[k], recv_sem=recv_sems.at[k],
                device_id=(x, y, c), device_id_type=MESH).wait_recv()
        for cp, _ in copies:
            cp.wait_send()
        total = gath[0]
        for j in range(1, N_DEV):
            total = total + gath[j]
        o_ref[...] = total

    return pl.pallas_call(
        body, name="all_reduce_small",
        in_specs=[pl.BlockSpec(memory_space=pltpu.VMEM)], out_specs=pl.BlockSpec(memory_space=pltpu.VMEM),
        out_shape=jax.ShapeDtypeStruct((r, LANES), F32),
        scratch_shapes=[pltpu.VMEM((N_DEV, r, LANES), F32), pltpu.SemaphoreType.DMA((7,)),
                        pltpu.SemaphoreType.DMA((7,))],
    )(v)


def _adamw_math(w, g, m, v):
    m2 = ADAM_B1 * m + (1.0 - ADAM_B1) * g
    v2 = ADAM_B2 * v + (1.0 - ADAM_B2) * (g * g)
    m_hat = m2 / (1.0 - ADAM_B1 ** ADAM_STEP)
    v_hat = v2 / (1.0 - ADAM_B2 ** ADAM_STEP)
    delta = -ADAM_LR * (m_hat / (jnp.sqrt(v_hat) + ADAM_EPS) + ADAM_WD * w)
    return delta, m2, v2


def _adamw_big(w, m, v, part, got, me):
    r, c = w.shape
    tr = max(t for t in range(HALO, min(r, 512) + 1, HALO) if r % t == 0)

    def body(me_ref, w_ref, m_ref, v_ref, own_ref, *rest):
        del me_ref
        got_refs, (g_out, d_out, m_out, v_out) = rest[:N_DEV - 1], rest[N_DEV - 1:]
        g = own_ref[...].astype(F32)
        for ref in got_refs:
            g = g + ref[...].astype(F32)
        delta, m2, v2 = _adamw_math(w_ref[...], g, m_ref[...], v_ref[...])
        g_out[...] = g
        d_out[...] = delta
        m_out[...] = m2
        v_out[...] = v2

    def peer_block(k):
        return pl.BlockSpec((None, tr, c), lambda i, me_ref: (jnp.bitwise_xor(me_ref[0], k), i, 0))

    plain = pl.BlockSpec((tr, c), lambda i, me_ref: (i, 0))
    out = jax.ShapeDtypeStruct((r, c), F32)
    return pl.pallas_call(
        body, name="adamw_big",
        grid_spec=pltpu.PrefetchScalarGridSpec(
            num_scalar_prefetch=1, grid=(r // tr,),
            in_specs=[plain, plain, plain] + [peer_block(k) for k in range(N_DEV)],
            out_specs=[plain] * 4),
        out_shape=[out] * 4,
        compiler_params=_params("parallel"),
    )(me, w, m, v, part, *([got] * (N_DEV - 1)))


def _adamw_small(ws, gs, ms, vs):
    n = len(ws)

    def body(*refs):
        ins, outs = refs[:4 * n], refs[4 * n:]
        for i in range(n):
            delta, m2, v2 = _adamw_math(ins[i][...], ins[n + i][...], ins[2 * n + i][...], ins[3 * n + i][...])
            outs[i][...] = delta
            outs[n + i][...] = m2
            outs[2 * n + i][...] = v2

    out = [jax.ShapeDtypeStruct(w.shape, F32) for w in ws]
    res = pl.pallas_call(body, name="adamw_small", out_shape=out * 3)(*ws, *gs, *ms, *vs)
    return res[:n], res[n:2 * n], res[2 * n:]


BIG = ("w_in", "w_proj_a", "w_proj_b", "w_out", "w_up", "w_down")
LATE = ("w_proj_a", "w_proj_b", "w_out", "w_up", "w_down")
COLUMN_SHARDED = ("w_in", "w_proj_a", "w_proj_b", "w_up")
SMALL = ("norm_mix_g", "b_gate", "conv_a_w", "conv_a_b", "norm_ffn_g", "ffn_conv_w", "ffn_conv_b", "final_norm_g")
SMALL_SHARDED = ("b_gate", "conv_a_w", "ffn_conv_w")
WEIGHTS = ("norm_mix_g", "w_in", "b_gate", "conv_a_w", "conv_a_b", "w_proj_a", "w_proj_b", "w_out", "norm_ffn_g",
           "w_up", "ffn_conv_w", "ffn_conv_b", "w_down", "final_norm_g")


def _pack(vectors, rows):
    flat = jnp.concatenate([v.reshape(-1) for v in vectors])
    return jnp.pad(flat, (0, rows * LANES - flat.shape[0])).reshape(rows, LANES)


def _packed_rows(count):
    rows = -(-count // LANES)
    return -(-rows // SUBLANES) * SUBLANES


def _unpack(packed, shapes):
    flat = packed.reshape(-1)
    out, lo = [], 0
    for s in shapes:
        size = 1
        for dim in s:
            size *= dim
        out.append(flat[lo:lo + size].reshape(s))
        lo += size
    return out


def _full_from_gathered(gathered):
    _, r, c = gathered.shape
    return gathered.reshape(N_DEV * r, c)


def _by_destination(grad):
    rr, cc = grad.shape
    return grad.reshape(N_DEV, rr // N_DEV, cc)


def _block2d(name, a):
    a = a.reshape(a.shape[-2:])
    return a.T if name in COLUMN_SHARDED else a


def kernel(x, norm_mix_g, w_in, b_gate, conv_a_w, conv_a_b, w_proj_a, w_proj_b, w_out, norm_ffn_g, w_up, ffn_conv_w, ffn_conv_b, w_down, final_norm_g, loss_target, m_norm_mix_g, m_w_in, m_b_gate, m_conv_a_w, m_conv_a_b, m_w_proj_a, m_w_proj_b, m_w_out, m_norm_ffn_g, m_w_up, m_ffn_conv_w, m_ffn_conv_b, m_w_down, m_final_norm_g, v_norm_mix_g, v_w_in, v_b_gate, v_conv_a_w, v_conv_a_b, v_w_proj_a, v_w_proj_b, v_w_out, v_norm_ffn_g, v_w_up, v_ffn_conv_w, v_ffn_conv_b, v_w_down, v_final_norm_g):
    given = dict(locals())
    shard = {n: given[n] for n in WEIGHTS}
    mom_m = {n: given["m_" + n] for n in WEIGHTS}
    mom_v = {n: given["v_" + n] for n in WEIGHTS}
    xi, yi, ci = _mesh_pos()
    me = _dev_index(xi, yi, ci)
    me1 = me.astype(jnp.int32).reshape(1)

    big2d = {n: _block2d(n, shard[n]) for n in BIG}
    small_shapes = [shard[n].shape[1:] for n in SMALL_SHARDED]
    n_small = sum(s[0] * s[1] for s in small_shapes)
    packed_small = _pack([shard[n] for n in SMALL_SHARDED], _packed_rows(n_small))
    gathered = _all_gather([big2d["w_in"].astype(BF16), packed_small])
    p = {"w_in": _full_from_gathered(gathered[0])}
    flat_small = gathered[-1].reshape(N_DEV, -1)
    lo = 0
    for n, (rows, width) in zip(SMALL_SHARDED, small_shapes):
        blocks = flat_small[:, lo:lo + rows * width].reshape(N_DEV, rows, width)
        p[n] = blocks.transpose(1, 0, 2).reshape(rows, N_DEV * width)
        lo += rows * width
    p["norm_mix_g"], p["norm_ffn_g"] = shard["norm_mix_g"], shard["norm_ffn_g"]
    p["conv_a_b"], p["ffn_conv_b"] = shard["conv_a_b"], shard["ffn_conv_b"]
    p["final_norm_g"] = shard["final_norm_g"].reshape(1, -1)
    late = {n: big2d[n].astype(BF16) for n in LATE}

    loss_part, dx, parts, got, g_small = _local_step(x[0], loss_target[0], p, late)

    results = {}
    for n in BIG:
        outs = _adamw_big(big2d[n], _block2d(n, mom_m[n]), _block2d(n, mom_v[n]), parts[n], got[n], me1)
        results[n] = [_block2d(n, o).reshape(shard[n].shape) for o in outs]

    small_full_shapes = [g_small[n].shape for n in SMALL]
    n_vec = sum(s[0] * s[1] for s in small_full_shapes) + 1
    packed = _pack([g_small[n] for n in SMALL] + [loss_part.reshape(1)], _packed_rows(n_vec))
    reduced = _all_reduce_small(packed)
    *g_full, loss_vec = _unpack(reduced, small_full_shapes + [(1,)])
    loss = loss_vec[0]
    own_g = []
    for n, g in zip(SMALL, g_full):
        if n in SMALL_SHARDED:
            width = shard[n].shape[-1]
            g = lax.dynamic_slice_in_dim(g, me * width, width, axis=1)
        own_g.append(g.reshape(shard[n].shape))
    def rows2d(a):
        return a.reshape(-1, a.shape[-1])

    deltas, new_ms, new_vs = _adamw_small([rows2d(shard[n]) for n in SMALL], [rows2d(g) for g in own_g],
                                          [rows2d(mom_m[n]) for n in SMALL], [rows2d(mom_v[n]) for n in SMALL])
    for i, n in enumerate(SMALL):
        results[n] = [own_g[i]] + [a.reshape(shard[n].shape) for a in (deltas[i], new_ms[i], new_vs[i])]

    grad_x = dx.reshape(x.shape)
    return (loss, grad_x, *[results[n][0] for n in WEIGHTS], *[results[n][1] for n in WEIGHTS],
            *[results[n][2] for n in WEIGHTS], *[results[n][3] for n in WEIGHTS])
```

```python
import functools

import jax
import jax.numpy as jnp
from jax import lax
from jax.experimental import pallas as pl
from jax.experimental.pallas import tpu as pltpu

F32 = jnp.float32
BF16 = jnp.bfloat16
MESH = pl.DeviceIdType.MESH

N_DEV = 8
RMS_EPS = 1e-6
NEG_INF = -1e30
N_GROUPS = 3
DILATIONS = (1, 4, 16)
HEADS_PER_GROUP = 4
HEAD_DIM = 64
GROUP_W = HEADS_PER_GROUP * HEAD_DIM
ATTN_W = N_GROUPS * GROUP_W
QBLK = 128
ATTN_SCALE = HEAD_DIM ** -0.5

ADAM_LR = 0.001
ADAM_B1 = 0.9
ADAM_B2 = 0.999
ADAM_EPS = 1e-08
ADAM_WD = 0.01
ADAM_STEP = 10

PERM_TOKENS = 256
ROWS_MATMUL = 512
HALO = 16
LANES = 128
SUBLANES = 8
VMEM_LIMIT_BYTES = 56 * 1024 * 1024


def _params(*sem):
    return pltpu.CompilerParams(dimension_semantics=sem, vmem_limit_bytes=VMEM_LIMIT_BYTES)


def _pick_tile(n, cap):
    if n <= cap:
        return n
    best = None
    for t in range(LANES, cap + 1, LANES):
        if n % t == 0:
            best = t
    assert best is not None, (n, cap)
    return best


def _rows(tm, c, j=0):
    return pl.BlockSpec((tm, c), lambda m: (m, j))


def _prev_halo(tm, c):
    return pl.BlockSpec((HALO, c), lambda m: (jnp.maximum(m * (tm // HALO) - 1, 0), 0))


def _next_halo(tm, c, t_total):
    last = t_total // HALO - 1
    return pl.BlockSpec((HALO, c), lambda m: (jnp.minimum((m + 1) * (tm // HALO), last), 0))


def _resident(shape):
    nd = len(shape)
    return pl.BlockSpec(shape, lambda *_: (0,) * nd, pipeline_mode=pl.Buffered(1))


def _acc_spec(c):
    return pl.BlockSpec((SUBLANES, c), lambda *_: (0, 0))


def _shift_down(u, halo, k):
    edge = jnp.concatenate([halo[HALO - SUBLANES:], u[:SUBLANES]], axis=0)
    head = pltpu.roll(edge, k, 0)[SUBLANES:]
    return jnp.concatenate([head, pltpu.roll(u, k, 0)[SUBLANES:]], axis=0)


def _shift_up(u, halo, k):
    n = u.shape[0]
    edge = jnp.concatenate([u[n - SUBLANES:], halo[:SUBLANES]], axis=0)
    tail = pltpu.roll(edge, 2 * SUBLANES - k, 0)[:SUBLANES]
    return jnp.concatenate([pltpu.roll(u, n - k, 0)[:n - SUBLANES], tail], axis=0)


def _interleave(tm, inverse=False):
    return _perm(tm // SUBLANES, tm, inverse)


def _edge_groups(u, halo, k, from_end):
    n = u.shape[0]
    sub = lax.broadcasted_iota(jnp.int32, (SUBLANES, u.shape[1]), 0)
    out = []
    for j in range(2 - k, 2):
        lo = n - HALO + j * SUBLANES if from_end else j * SUBLANES
        own, other = u[lo:lo + SUBLANES], halo[j * SUBLANES:(j + 1) * SUBLANES]
        if from_end:
            out.append(pltpu.roll(jnp.where(sub == SUBLANES - 1, other, own), 1, 0))
        else:
            out.append(pltpu.roll(jnp.where(sub == 0, other, own), SUBLANES - 1, 0))
    return out


def _shift_down_il(u, halo, k):
    return jnp.concatenate(_edge_groups(u, halo, k, True) + [u[:u.shape[0] - k * SUBLANES]], axis=0)


def _shift_up_il(u, halo, k):
    if k == 1:
        edge = _edge_groups(u, halo, 2, False)[:1]
    else:
        edge = _edge_groups(u, halo, 2, False)
    return jnp.concatenate([u[k * SUBLANES:]] + edge, axis=0)


def _stack_rows(rows, c):
    idx = lax.broadcasted_iota(jnp.int32, (SUBLANES, c), 0)
    out = jnp.zeros((SUBLANES, c), F32)
    for i, r in enumerate(rows):
        out = out + jnp.where(idx == i, r, 0.0)
    return out


def _colsum(v):
    return jnp.sum(v, axis=0, keepdims=True)


def _sigmoid(v):
    return 0.5 * jnp.tanh(0.5 * v) + 0.5


def _rms_fwd(xv, g):
    r = lax.rsqrt(jnp.mean(xv * xv, axis=-1, keepdims=True) + RMS_EPS)
    return xv * r * g, r


def _rms_bwd(xv, g, dy):
    r = lax.rsqrt(jnp.mean(xv * xv, axis=-1, keepdims=True) + RMS_EPS)
    xn = xv * r
    dxn = dy * g
    dx = r * (dxn - xn * jnp.mean(dxn * xn, axis=-1, keepdims=True))
    return dx, dy * xn


def _dot(a, b):
    return jnp.dot(a, b, preferred_element_type=F32)


def _dot_nt(a, b):
    return lax.dot_general(a, b, (((1,), (1,)), ((), ())), preferred_element_type=F32)


def _dot_tn(a, b):
    return lax.dot_general(a, b, (((0,), (0,)), ((), ())), preferred_element_type=F32)


def _perm(dil, n, inverse=False):
    i = lax.broadcasted_iota(jnp.int32, (n, n), 0)
    j = lax.broadcasted_iota(jnp.int32, (n, n), 1)
    if inverse:
        i, j = j, i
    per = n // dil
    return (j == (i % per) * dil + i // per).astype(BF16)


def _permute_rows(pm, v):
    if v.dtype == BF16:
        return _dot(pm, v).astype(BF16)
    hi = v.astype(BF16)
    lo = (v - hi.astype(F32)).astype(BF16)
    return _dot(pm, hi) + _dot(pm, lo)


def _stream_view(a, dil):
    t, c = a.shape
    return a.reshape(dil, t // dil, c)


def _stream_spec(dil, tm, c):
    return pl.BlockSpec((dil, tm // dil, c), lambda m: (0, m, 0))


def _load_streams(ref, dil, tm):
    c = ref.shape[-1]
    if dil == 1:
        return ref[...].reshape(tm, c)
    sub = min(PERM_TOKENS, tm)
    pm = _perm(dil, sub, inverse=True)
    parts = [_permute_rows(pm, ref[:, i * (sub // dil):(i + 1) * (sub // dil), :].reshape(sub, c))
             for i in range(tm // sub)]
    return parts[0] if len(parts) == 1 else jnp.concatenate(parts, axis=0)


def _store_streams(ref, dil, tm, v):
    if dil == 1:
        ref[...] = v.reshape(ref.shape).astype(ref.dtype)
        return
    sub = min(PERM_TOKENS, tm)
    pm = _perm(dil, sub)
    for i in range(tm // sub):
        piece = _permute_rows(pm, v[i * sub:(i + 1) * sub])
        ref[:, i * (sub // dil):(i + 1) * (sub // dil), :] = piece.reshape(dil, sub // dil, -1).astype(ref.dtype)


ANY = pl.BlockSpec(memory_space=pl.ANY)


def _mesh_pos():
    return lax.axis_index("x"), lax.axis_index("y"), lax.axis_index("c")


def _dev_index(px, py, pc):
    return 4 * px + 2 * py + pc


class _Exchange:
    def __init__(self, mode, arrays, rows=None, into=()):
        self.mode, self.arrays, self.rows, self.into = mode, list(arrays), rows, list(into)
        n = len(self.arrays)
        if mode == "gather":
            self.out_shape = [jax.ShapeDtypeStruct((N_DEV,) + a.shape, a.dtype) for a in self.arrays]
        else:
            self.out_shape = [jax.ShapeDtypeStruct(a.shape, a.dtype) for a in self.arrays]
        self.scratch = [pltpu.SemaphoreType.DMA((n, N_DEV - 1)), pltpu.SemaphoreType.DMA((n, N_DEV - 1)),
                        pltpu.SemaphoreType.DMA((n,))]

    def _peers(self):
        x, y, c = _mesh_pos()
        flips = [(kx, ky, kc) for kx in (0, 1) for ky in (0, 1) for kc in (0, 1)][1:]
        peers = [(1 - x if kx else x, 1 - y if ky else y, 1 - c if kc else c) for kx, ky, kc in flips]
        return _dev_index(x, y, c), peers

    def _copy(self, ins, outs, sems, i, k, peer, me, sending):
        src = ins[i] if self.mode == "gather" else ins[i].at[_dev_index(*peer)]
        dst = outs[i].at[me if sending else _dev_index(*peer)]
        if self.rows is not None:
            src, dst = src.at[pl.ds(*self.rows)], dst.at[pl.ds(*self.rows)]
        return pltpu.make_async_remote_copy(src_ref=src, dst_ref=dst, send_sem=sems[0].at[i, k],
                                            recv_sem=sems[1].at[i, k], device_id=peer, device_id_type=MESH)

    def _own(self, ins, outs, sems, i, me):
        return pltpu.make_async_copy(ins[i], outs[i].at[me], sems[2].at[i])

    def start(self, ins, outs, sems):
        me, peers = self._peers()
        for i in range(len(ins)):
            if self.mode == "gather":
                self._own(ins, outs, sems, i, me).start()
            for k, peer in enumerate(peers):
                self._copy(ins, outs, sems, i, k, peer, me, True).start()

    def wait(self, ins, outs, sems):
        me, peers = self._peers()
        for i in range(len(ins)):
            for k, peer in enumerate(peers):
                self._copy(ins, outs, sems, i, k, peer, me, False).wait_recv()
            for k, peer in enumerate(peers):
                self._copy(ins, outs, sems, i, k, peer, me, True).wait_send()
            if self.mode == "gather":
                self._own(ins, outs, sems, i, me).wait()


def _call(body, *, name, grid, in_specs, out_specs, out_shape, args, semantics, carry=None, scratch=()):
    if carry is None:
        return pl.pallas_call(body, name=name, grid=grid, in_specs=in_specs, out_specs=out_specs,
                              out_shape=out_shape, scratch_shapes=list(scratch),
                              compiler_params=_params(*semantics))(*args)
    n_in, n_out, n_x, n_s = len(in_specs), len(out_specs), len(carry.arrays), len(scratch)
    n_into = len(carry.into)
    all_in = n_in + n_x + n_into

    def carried(*refs):
        ins, x_ins = refs[:n_in], refs[n_in:n_in + n_x]
        outs = refs[all_in:all_in + n_out]
        x_outs = refs[all_in + n_out:all_in + n_out + n_x]
        own = refs[all_in + n_out + n_x:all_in + n_out + n_x + n_s]
        sems = refs[all_in + n_out + n_x + n_s:]
        first = functools.reduce(jnp.logical_and, [pl.program_id(a) == 0 for a in range(len(grid))])
        last = functools.reduce(jnp.logical_and, [pl.program_id(a) == grid[a] - 1 for a in range(len(grid))])

        @pl.when(first)
        def _():
            carry.start(x_ins, x_outs, sems)

        body(*ins, *outs, *own)

        @pl.when(last)
        def _():
            carry.wait(x_ins, x_outs, sems)

    res = pl.pallas_call(
        carried, name=name, grid=grid, in_specs=list(in_specs) + [ANY] * (n_x + n_into),
        out_specs=list(out_specs) + [ANY] * n_x, out_shape=list(out_shape) + carry.out_shape,
        input_output_aliases={n_in + n_x + i: n_out + i for i in range(n_into)},
        scratch_shapes=list(scratch) + carry.scratch, compiler_params=_params(*["arbitrary"] * len(grid)),
    )(*args, *carry.arrays, *carry.into)
    return list(res[:n_out]), list(res[n_out:])


def _in_proj(x, g, wt, cw, carry=None):
    t, d = x.shape
    n = wt.shape[0]
    tm = min(ROWS_MATMUL, t)
    qkv0 = 3 * cw

    def body(x_ref, g_ref, wt_ref, h_ref, abcv_ref, gates_ref, *s_refs):
        h = _rms_fwd(x_ref[...], g_ref[...])[0].astype(BF16)
        h_ref[...] = h
        abcv_ref[...] = _dot_nt(h, wt_ref[0:qkv0, :]).astype(BF16)
        gates_ref[...] = _dot_nt(h, wt_ref[qkv0 + 3 * ATTN_W:n, :]).astype(BF16)
        qkv = _dot_nt(h, wt_ref[qkv0:qkv0 + 3 * ATTN_W, :]).astype(BF16)
        for gi, s_ref in enumerate(s_refs):
            cols = [qkv[:, j * ATTN_W + gi * GROUP_W:j * ATTN_W + (gi + 1) * GROUP_W] for j in range(3)]
            _store_streams(s_ref, DILATIONS[gi], tm, jnp.concatenate(cols, axis=1))

    return _call(
        body, name="in_proj", grid=(t // tm,),
        in_specs=[_rows(tm, d), _resident((1, d)), _resident((n, d))],
        out_specs=[_rows(tm, d), _rows(tm, qkv0), _rows(tm, 2 * d)]
        + [_stream_spec(dil, tm, 3 * GROUP_W) for dil in DILATIONS],
        out_shape=[jax.ShapeDtypeStruct((t, d), BF16), jax.ShapeDtypeStruct((t, qkv0), BF16),
                   jax.ShapeDtypeStruct((t, 2 * d), BF16)]
        + [jax.ShapeDtypeStruct((dil, t // dil, 3 * GROUP_W), BF16) for dil in DILATIONS],
        args=(x, g, wt), semantics=("parallel",), carry=carry)


def _head_masks():
    lane = lax.broadcasted_iota(jnp.int32, (1, GROUP_W), 1)
    return lane, [(lane // HEAD_DIM) == h for h in range(HEADS_PER_GROUP)]


def _stack_heads(v, heads):
    return jnp.concatenate([jnp.where(hm, v, jnp.zeros_like(v)) for hm in heads], axis=0)


def _merge_heads(v, heads):
    out = jnp.zeros((QBLK, GROUP_W), v.dtype)
    for h, hm in enumerate(heads):
        out = jnp.where(hm, v[h * QBLK:(h + 1) * QBLK], out)
    return out


def _pair_block(col):
    return pl.BlockSpec((2 * QBLK, GROUP_W), lambda b: (b, col))


def _edge_block(col, shift, nb):
    return pl.BlockSpec((QBLK, GROUP_W), lambda b: (jnp.clip(2 * b + shift, 0, nb - 1), col))


def _band_mask(has_prev):
    rows = HEADS_PER_GROUP * QBLK
    row = lax.broadcasted_iota(jnp.int32, (rows, 2 * QBLK), 0) & (QBLK - 1)
    col = lax.broadcasted_iota(jnp.int32, (rows, 2 * QBLK), 1)
    return ((col < QBLK) & (col >= row) & has_prev) | ((col >= QBLK) & (col - QBLK <= row))


def _next_mask(has_next):
    rows = HEADS_PER_GROUP * QBLK
    row = lax.broadcasted_iota(jnp.int32, (rows, QBLK), 0) & (QBLK - 1)
    col = lax.broadcasted_iota(jnp.int32, (rows, QBLK), 1)
    return (col >= row) & has_next


def _attn_fwd(s, dil, carry=None):
    t = s.shape[0] * s.shape[1]
    nb = t // QBLK
    per_stream = nb // dil
    assert per_stream % 2 == 0

    def body(q_ref, kc_ref, kp_ref, vc_ref, vp_ref, o_ref, lse_ref):
        b = pl.program_id(0)
        _, heads = _head_masks()
        first_has_prev = lax.rem(2 * b, per_stream) != 0
        for j in range(2):
            rows = slice(j * QBLK, (j + 1) * QBLK)
            if j == 0:
                k2 = jnp.concatenate([kp_ref[...], kc_ref[rows, :]], axis=0)
                v2 = jnp.concatenate([vp_ref[...], vc_ref[rows, :]], axis=0)
            else:
                k2, v2 = kc_ref[...], vc_ref[...]
            mask = _band_mask(first_has_prev if j == 0 else True)
            sc = jnp.where(mask, _dot_nt(_stack_heads(q_ref[rows, :], heads), k2) * ATTN_SCALE, NEG_INF)
            mx = jnp.max(sc, axis=1, keepdims=True)
            pr = jnp.exp(sc - mx)
            den = jnp.sum(pr, axis=1, keepdims=True)
            o_all = _dot(pr.astype(BF16), v2) / den
            o_ref[rows, :] = _merge_heads(o_all, heads).astype(BF16)
            lse_ref[rows, :] = _merge_heads(jnp.broadcast_to(mx + jnp.log(den), o_all.shape), heads)

    sv = s.reshape(t, 3 * GROUP_W)
    return _call(
        body, name=f"attn_fwd_d{dil}", grid=(nb // 2,),
        in_specs=[_pair_block(0), _pair_block(1), _edge_block(1, -1, nb), _pair_block(2), _edge_block(2, -1, nb)],
        out_specs=[_pair_block(0), _pair_block(0)],
        out_shape=[jax.ShapeDtypeStruct((t, GROUP_W), BF16), jax.ShapeDtypeStruct((t, GROUP_W), F32)],
        args=(sv, sv, sv, sv, sv), semantics=("parallel",), carry=carry)


def _group_softmax(parts):
    mx = jnp.maximum(jnp.maximum(parts[0], parts[1]), parts[2])
    es = [jnp.exp(p - mx) for p in parts]
    den = es[0] + es[1] + es[2]
    return [e / den for e in es]


def _mixer_out(x, abcv, gates, os, lses, conv_w, conv_b, b_gate, w_pa, w_pb, w_o, carry=None):
    t, d = x.shape
    cw = conv_w.shape[1]
    tm = min(ROWS_MATMUL, t)

    def body(x_ref, abcv_ref, halo_ref, gates_ref, o0_ref, o1_ref, o2_ref, l0_ref, l1_ref, l2_ref, cw_ref, cb_ref,
             bg_ref, wpa_ref, wpb_ref, wo_ref, x1_ref, ya_ref, yb_ref, yap_ref, ybp_ref, mg_ref):
        m = pl.program_id(0)
        ab = abcv_ref[:, 0:cw].astype(F32)
        u = abcv_ref[:, cw:2 * cw].astype(F32) * abcv_ref[:, 2 * cw:3 * cw].astype(F32)
        hu = halo_ref[:, cw:2 * cw].astype(F32) * halo_ref[:, 2 * cw:3 * cw].astype(F32)
        hu = jnp.where(m > 0, hu, 0.0)
        cv = (cw_ref[0:1, :] * _shift_down(u, hu, 2) + cw_ref[1:2, :] * _shift_down(u, hu, 1)
              + cw_ref[2:3, :] * u + cb_ref[...])
        ya = (ab * cv).astype(BF16)
        ya_ref[...] = ya
        alphas = _group_softmax([_load_streams(r, dil, tm) for r, dil in zip((l0_ref, l1_ref, l2_ref), DILATIONS)])
        for i, (o_ref, dil) in enumerate(zip((o0_ref, o1_ref, o2_ref), DILATIONS)):
            sl = slice(i * GROUP_W, (i + 1) * GROUP_W)
            yb_ref[:, sl] = (alphas[i] * _load_streams(o_ref, dil, tm).astype(F32)).astype(BF16)
        yap = _dot_nt(ya, wpa_ref[...])
        ybp = _dot_nt(yb_ref[...], wpb_ref[...])
        yap_ref[...] = yap.astype(BF16)
        ybp_ref[...] = ybp.astype(BF16)
        sa = _sigmoid(gates_ref[:, 0:d].astype(F32) + bg_ref[0:1, :])
        sb = _sigmoid(gates_ref[:, d:2 * d].astype(F32) + bg_ref[1:2, :])
        merged = (sa * yap + sb * ybp).astype(BF16)
        mg_ref[...] = merged
        x1_ref[...] = x_ref[...] + _dot(merged, wo_ref[...])

    return _call(
        body, name="mixer_out", grid=(t // tm,),
        in_specs=[_rows(tm, d), _rows(tm, 3 * cw), _prev_halo(tm, 3 * cw), _rows(tm, 2 * d)]
        + [_stream_spec(dil, tm, GROUP_W) for dil in DILATIONS] * 2
        + [_resident((3, cw)), _resident((1, cw)), _resident((2, d)),
           _resident((d, cw)), _resident((d, ATTN_W)), _resident((d, d))],
        out_specs=[_rows(tm, d), _rows(tm, cw), _rows(tm, ATTN_W), _rows(tm, d), _rows(tm, d), _rows(tm, d)],
        out_shape=[jax.ShapeDtypeStruct((t, d), F32), jax.ShapeDtypeStruct((t, cw), BF16),
                   jax.ShapeDtypeStruct((t, ATTN_W), BF16), jax.ShapeDtypeStruct((t, d), BF16),
                   jax.ShapeDtypeStruct((t, d), BF16), jax.ShapeDtypeStruct((t, d), BF16)],
        args=(x, abcv, abcv, gates, *[_stream_view(a, dil) for a, dil in zip(os, DILATIONS)],
              *[_stream_view(a, dil) for a, dil in zip(lses, DILATIONS)], conv_w, conv_b, b_gate, w_pa, w_pb, w_o),
        semantics=("parallel",), carry=carry)


def _ffn_fwd(x1, target, g2, w_ut, conv_w, conv_b, w_d, g_f, carry=None):
    t, d = x1.shape
    dff = w_d.shape[0]
    tm = min(256, t)
    ck = _pick_tile(dff, 2816)

    def body(x1_ref, tg_ref, g2_ref, wut_ref, cw_ref, cb_ref, wd_ref, gf_ref, h2_ref, up_ref, act_ref, conv_ref,
             dx2_ref, dx2i_ref, acc_ref, loss_ref, halo_ref):
        m = pl.program_id(0)

        @pl.when(m == 0)
        def _():
            acc_ref[...] = jnp.zeros_like(acc_ref)
            loss_ref[...] = jnp.zeros_like(loss_ref)
            halo_ref[...] = jnp.zeros_like(halo_ref)

        h2 = _permute_rows(_interleave(tm), _rms_fwd(x1_ref[...], g2_ref[...])[0].astype(BF16))
        h2_ref[...] = h2

        def conv(c0):
            p = _dot_nt(h2, wut_ref[c0:c0 + ck, :])
            up_ref[:, c0:c0 + ck] = p.astype(BF16)
            hp = halo_ref[:, c0:c0 + ck]
            halo_ref[:, c0:c0 + ck] = p[tm - HALO:, :]
            return (cw_ref[0:1, c0:c0 + ck] * _shift_down_il(p, hp, 2)
                    + cw_ref[1:2, c0:c0 + ck] * _shift_down_il(p, hp, 1)
                    + cw_ref[2:3, c0:c0 + ck] * p + cb_ref[:, c0:c0 + ck])

        down = jnp.zeros((tm, d), F32)
        for c0 in range(0, dff, ck):
            gate = conv(c0)
            val = conv(dff + c0)
            conv_ref[:, c0:c0 + ck] = gate.astype(BF16)
            conv_ref[:, dff + c0:dff + c0 + ck] = val.astype(BF16)
            act = (gate * _sigmoid(gate) * val).astype(BF16)
            act_ref[:, c0:c0 + ck] = act
            down = down + _dot(act, wd_ref[c0:c0 + ck, :])
        x2 = x1_ref[...] + _permute_rows(_interleave(tm, inverse=True), down)
        y, _ = _rms_fwd(x2, gf_ref[...])
        diff = y - tg_ref[...]
        loss_ref[...] += 0.5 * jnp.sum(jnp.mean(diff * diff, axis=-1, keepdims=True))
        dx2, dg = _rms_bwd(x2, gf_ref[...], diff * (1.0 / d))
        dx2_ref[...] = dx2
        dx2i_ref[...] = _permute_rows(_interleave(tm), dx2.astype(BF16))
        acc_ref[...] += _stack_rows([_colsum(dg)], d)

    return _call(
        body, name="ffn_fwd", grid=(t // tm,),
        in_specs=[_rows(tm, d), _rows(tm, d), _resident((1, d)), _resident((2 * dff, d)), _resident((3, 2 * dff)),
                  _resident((1, 2 * dff)), _resident((dff, d)), _resident((1, d))],
        out_specs=[_rows(tm, d), _rows(tm, 2 * dff), _rows(tm, dff), _rows(tm, 2 * dff), _rows(tm, d), _rows(tm, d),
                   _acc_spec(d), _acc_spec(LANES)],
        out_shape=[jax.ShapeDtypeStruct((t, d), BF16), jax.ShapeDtypeStruct((t, 2 * dff), BF16),
                   jax.ShapeDtypeStruct((t, dff), BF16), jax.ShapeDtypeStruct((t, 2 * dff), BF16),
                   jax.ShapeDtypeStruct((t, d), F32), jax.ShapeDtypeStruct((t, d), BF16),
                   jax.ShapeDtypeStruct((SUBLANES, d), F32), jax.ShapeDtypeStruct((SUBLANES, LANES), F32)],
        args=(x1, target, g2, w_ut, conv_w, conv_b, w_d, g_f), semantics=("arbitrary",), carry=carry,
        scratch=[pltpu.VMEM((HALO, 2 * dff), F32)])


def _ffn_act_bwd(dx2, conv, w_d):
    t, d = dx2.shape
    dff = w_d.shape[0]
    tm = min(256, t)
    ck = _pick_tile(dff, 2816)

    def body(dx2_ref, conv_ref, wd_ref, dup_ref, acc_ref):
        m = pl.program_id(0)

        @pl.when(m == 0)
        def _():
            acc_ref[...] = jnp.zeros_like(acc_ref)

        dx2v = dx2_ref[...]
        for c0 in range(0, dff, ck):
            dact = _dot_nt(dx2v, wd_ref[c0:c0 + ck, :])
            gate = conv_ref[:, c0:c0 + ck].astype(F32)
            val = conv_ref[:, dff + c0:dff + c0 + ck].astype(F32)
            sg = _sigmoid(gate)
            dval = dact * gate * sg
            dgate = dact * val * sg * (1.0 + gate * (1.0 - sg))
            dup_ref[:, c0:c0 + ck] = dgate.astype(BF16)
            dup_ref[:, dff + c0:dff + c0 + ck] = dval.astype(BF16)
            acc_ref[:, c0:c0 + ck] += _stack_rows([_colsum(dgate)], ck)
            acc_ref[:, dff + c0:dff + c0 + ck] += _stack_rows([_colsum(dval)], ck)

    return pl.pallas_call(
        body, name="ffn_act_bwd", grid=(t // tm,),
        in_specs=[_rows(tm, d), _rows(tm, 2 * dff), _resident((dff, d))],
        out_specs=[_rows(tm, 2 * dff), _acc_spec(2 * dff)],
        out_shape=[jax.ShapeDtypeStruct((t, 2 * dff), BF16), jax.ShapeDtypeStruct((SUBLANES, 2 * dff), F32)],
        compiler_params=_params("arbitrary"),
    )(dx2, conv, w_d)


def _ffn_up_bwd(dup, up_pre, x1, dx2, conv_w, w_u, g2, carry=None):
    t, d = x1.shape
    n = dup.shape[1]
    tm = min(256, t)
    ck = _pick_tile(n, 256)
    last = t // tm - 1

    def body(dup_ref, nxt_ref, up_ref, x1_ref, dx2_ref, cw_ref, wu_ref, g2_ref, dpre_ref, dx1_ref, acc_ref, accw_ref):
        m = pl.program_id(0)

        @pl.when(m == 0)
        def _():
            acc_ref[...] = jnp.zeros_like(acc_ref)
            accw_ref[...] = jnp.zeros_like(accw_ref)

        dh = jnp.zeros((tm, d), F32)
        for c0 in range(0, n, ck):
            du = dup_ref[:, c0:c0 + ck].astype(F32)
            hn = jnp.where(m < last, nxt_ref[:, c0:c0 + ck].astype(F32), 0.0)
            du1 = _shift_up_il(du, hn, 1)
            du2 = _shift_up_il(du, hn, 2)
            dpre = (cw_ref[2:3, c0:c0 + ck] * du + cw_ref[1:2, c0:c0 + ck] * du1
                    + cw_ref[0:1, c0:c0 + ck] * du2).astype(BF16)
            dpre_ref[:, c0:c0 + ck] = dpre
            dh = dh + _dot(dpre, wu_ref[c0:c0 + ck, :])
            p = up_ref[:, c0:c0 + ck].astype(F32)
            accw_ref[:, c0:c0 + ck] += _stack_rows([_colsum(du2 * p), _colsum(du1 * p), _colsum(du * p)], ck)
        dh = _permute_rows(_interleave(tm, inverse=True), dh)
        dx, dg = _rms_bwd(x1_ref[...], g2_ref[...], dh)
        dx1_ref[...] = dx2_ref[...] + dx
        acc_ref[...] += _stack_rows([_colsum(dg)], d)

    return _call(
        body, name="ffn_up_bwd", grid=(t // tm,),
        in_specs=[_rows(tm, n), _next_halo(tm, n, t), _rows(tm, n), _rows(tm, d), _rows(tm, d), _resident((3, n)),
                  _resident((n, d)), _resident((1, d))],
        out_specs=[_rows(tm, n), _rows(tm, d), _acc_spec(d), _acc_spec(n)],
        out_shape=[jax.ShapeDtypeStruct((t, n), BF16), jax.ShapeDtypeStruct((t, d), F32),
                   jax.ShapeDtypeStruct((SUBLANES, d), F32), jax.ShapeDtypeStruct((SUBLANES, n), F32)],
        args=(dup, dup, up_pre, x1, dx2, conv_w, w_u, g2), semantics=("arbitrary",), carry=carry)


def _tn_matmul(a, b, name):
    t, mdim = a.shape
    n = b.shape[1]
    tk = min(1024, t)
    tmm = _pick_tile(mdim, 1536)
    tn = _pick_tile(n, 1024)

    def body(a_ref, b_ref, o_ref, acc_ref):
        k = pl.program_id(2)

        @pl.when(k == 0)
        def _():
            acc_ref[...] = jnp.zeros_like(acc_ref)

        acc_ref[...] += _dot_tn(a_ref[...].astype(BF16), b_ref[...].astype(BF16))

        @pl.when(k == t // tk - 1)
        def _():
            o_ref[...] = acc_ref[...].astype(BF16)

    return pl.pallas_call(
        body, name=name, grid=(mdim // tmm, n // tn, t // tk),
        in_specs=[pl.BlockSpec((tk, tmm), lambda i, j, k: (k, i)), pl.BlockSpec((tk, tn), lambda i, j, k: (k, j))],
        out_specs=pl.BlockSpec((tmm, tn), lambda i, j, k: (i, j)),
        out_shape=jax.ShapeDtypeStruct((mdim, n), BF16),
        scratch_shapes=[pltpu.VMEM((tmm, tn), F32)],
        compiler_params=_params("parallel", "parallel", "arbitrary"),
    )(a, b)


def _mixer_bwd(dx1, gates, yap, ybp, os, lses, b_gate, w_o, w_pa, w_pb):
    t, d = dx1.shape
    cw = w_pa.shape[1]
    tm = min(ROWS_MATMUL, t)

    def body(dx1_ref, gates_ref, yap_ref, ybp_ref, o0_ref, o1_ref, o2_ref, l0_ref, l1_ref, l2_ref, bg_ref, wo_ref,
             wpa_ref, wpb_ref, dgates_ref, dyap_ref, dybp_ref, dya_ref, do0_ref, do1_ref, do2_ref, dl0_ref, dl1_ref,
             dl2_ref, acc_ref):
        m = pl.program_id(0)

        @pl.when(m == 0)
        def _():
            acc_ref[...] = jnp.zeros_like(acc_ref)

        dmg = _dot_nt(dx1_ref[...].astype(BF16), wo_ref[...])
        sa = _sigmoid(gates_ref[:, 0:d].astype(F32) + bg_ref[0:1, :])
        sb = _sigmoid(gates_ref[:, d:2 * d].astype(F32) + bg_ref[1:2, :])
        dyap = (dmg * sa).astype(BF16)
        dybp = (dmg * sb).astype(BF16)
        dga = dmg * yap_ref[...].astype(F32) * sa * (1.0 - sa)
        dgb = dmg * ybp_ref[...].astype(F32) * sb * (1.0 - sb)
        dyap_ref[...] = dyap
        dybp_ref[...] = dybp
        dgates_ref[:, 0:d] = dga.astype(BF16)
        dgates_ref[:, d:2 * d] = dgb.astype(BF16)
        acc_ref[...] += _stack_rows([_colsum(dga), _colsum(dgb)], d)
        dya_ref[...] = _dot(dyap, wpa_ref[...]).astype(BF16)
        dyb = _dot(dybp, wpb_ref[...])

        ri = lax.broadcasted_iota(jnp.int32, (GROUP_W, GROUP_W), 0) // HEAD_DIM
        ci = lax.broadcasted_iota(jnp.int32, (GROUP_W, GROUP_W), 1) // HEAD_DIM
        same_head = (ri == ci).astype(BF16)
        alphas = _group_softmax([_load_streams(r, dil, tm) for r, dil in zip((l0_ref, l1_ref, l2_ref), DILATIONS)])
        prod = jnp.zeros((tm, GROUP_W), F32)
        for i, (o_ref, do_ref, dil) in enumerate(zip((o0_ref, o1_ref, o2_ref), (do0_ref, do1_ref, do2_ref), DILATIONS)):
            dov = alphas[i] * dyb[:, i * GROUP_W:(i + 1) * GROUP_W]
            _store_streams(do_ref, dil, tm, dov.astype(BF16))
            prod = prod + dov * _load_streams(o_ref, dil, tm).astype(F32)
        hi = prod.astype(BF16)
        lo = (prod - hi.astype(F32)).astype(BF16)
        dtot = _dot(hi, same_head) + _dot(lo, same_head)
        for alpha, dl_ref, dil in zip(alphas, (dl0_ref, dl1_ref, dl2_ref), DILATIONS):
            _store_streams(dl_ref, dil, tm, alpha * dtot)

    streams = [_stream_spec(dil, tm, GROUP_W) for dil in DILATIONS]
    res = _call(
        body, name="mixer_bwd", grid=(t // tm,),
        in_specs=[_rows(tm, d), _rows(tm, 2 * d), _rows(tm, d), _rows(tm, d)] + streams * 2
        + [_resident((2, d)), _resident((d, d)), _resident((d, cw)), _resident((d, ATTN_W))],
        out_specs=[_rows(tm, 2 * d), _rows(tm, d), _rows(tm, d), _rows(tm, cw)] + streams * 2 + [_acc_spec(d)],
        out_shape=[jax.ShapeDtypeStruct((t, 2 * d), BF16), jax.ShapeDtypeStruct((t, d), BF16),
                   jax.ShapeDtypeStruct((t, d), BF16), jax.ShapeDtypeStruct((t, cw), BF16)]
        + [jax.ShapeDtypeStruct((dil, t // dil, GROUP_W), BF16) for dil in DILATIONS]
        + [jax.ShapeDtypeStruct((dil, t // dil, GROUP_W), F32) for dil in DILATIONS]
        + [jax.ShapeDtypeStruct((SUBLANES, d), F32)],
        args=(dx1, gates, yap, ybp, *[_stream_view(a, dil) for a, dil in zip(os, DILATIONS)],
              *[_stream_view(a, dil) for a, dil in zip(lses, DILATIONS)], b_gate, w_o, w_pa, w_pb),
        semantics=("arbitrary",))
    dgates, dyap, dybp, dya = res[:4]
    dos = [a.reshape(t, GROUP_W) for a in res[4:7]]
    dls = [a.reshape(t, GROUP_W) for a in res[7:10]]
    return dgates, dyap, dybp, dya, dos, dls, res[10]


def _attn_bwd(s, do, lse, dl, dil, carry=None):
    t = s.shape[0] * s.shape[1]
    nb = t // QBLK
    per_stream = nb // dil

    def body(q_ref, qn_ref, kc_ref, kp_ref, vc_ref, vp_ref, do_ref, don_ref, lse_ref, lsen_ref, dl_ref, dln_ref,
             ds_ref):
        b = pl.program_id(0)
        lane, heads = _head_masks()
        first_has_prev = lax.rem(2 * b, per_stream) != 0
        last_has_next = lax.rem(2 * b + 2, per_stream) != 0

        def cols(v):
            return jnp.concatenate([jnp.sum(jnp.where(lane == h * HEAD_DIM, v, 0.0), axis=1, keepdims=True)
                                    for h in range(HEADS_PER_GROUP)], axis=0)

        def pair(qs, dos, k, v, valid, lse_c, dl_c):
            s = jnp.where(valid, _dot_nt(qs, k) * ATTN_SCALE, NEG_INF)
            p = jnp.exp(s - lse_c)
            ds = p * (_dot_nt(dos, v) - dl_c)
            return p.astype(BF16), ds.astype(BF16)

        lo, hi = slice(0, QBLK), slice(QBLK, 2 * QBLK)
        for j, rows in enumerate((lo, hi)):
            q, do, lse, dl = q_ref[rows, :], do_ref[rows, :], lse_ref[rows, :], dl_ref[rows, :]
            kc, vc = kc_ref[rows, :], vc_ref[rows, :]
            if j == 0:
                k2 = jnp.concatenate([kp_ref[...], kc], axis=0)
                v2 = jnp.concatenate([vp_ref[...], vc], axis=0)
                qn, don, lsen, dln = q_ref[hi, :], do_ref[hi, :], lse_ref[hi, :], dl_ref[hi, :]
                mask, mask_n = _band_mask(first_has_prev), _next_mask(True)
            else:
                k2, v2 = kc_ref[...], vc_ref[...]
                qn, don, lsen, dln = qn_ref[...], don_ref[...], lsen_ref[...], dln_ref[...]
                mask, mask_n = _band_mask(True), _next_mask(last_has_next)
            qs, qns = _stack_heads(q, heads), _stack_heads(qn, heads)
            dos, dons = _stack_heads(do, heads), _stack_heads(don, heads)
            p_q, ds_q = pair(qs, dos, k2, v2, mask, cols(lse), cols(dl))
            p_n, ds_n = pair(qns, dons, kc, vc, mask_n, cols(lsen), cols(dln))
            dq = _merge_heads(_dot(ds_q, k2), heads)
            dk = _dot_tn(jnp.concatenate([ds_q[:, QBLK:], ds_n], axis=0), jnp.concatenate([qs, qns], axis=0))
            dv = _dot_tn(jnp.concatenate([p_q[:, QBLK:], p_n], axis=0), jnp.concatenate([dos, dons], axis=0))
            ds_ref[rows, 0:GROUP_W] = (dq * ATTN_SCALE).astype(BF16)
            ds_ref[rows, GROUP_W:2 * GROUP_W] = (dk * ATTN_SCALE).astype(BF16)
            ds_ref[rows, 2 * GROUP_W:3 * GROUP_W] = dv.astype(BF16)

    sv = s.reshape(t, 3 * GROUP_W)
    cur, nxt = _pair_block(0), _edge_block(0, 2, nb)
    return _call(
        body, name=f"attn_bwd_d{dil}", grid=(nb // 2,),
        in_specs=[cur, nxt, _pair_block(1), _edge_block(1, -1, nb), _pair_block(2), _edge_block(2, -1, nb),
                  cur, nxt, cur, nxt, cur, nxt],
        out_specs=[pl.BlockSpec((2 * QBLK, 3 * GROUP_W), lambda b: (b, 0))],
        out_shape=[jax.ShapeDtypeStruct((t, 3 * GROUP_W), BF16)],
        args=(sv, sv, sv, sv, sv, sv, do, do, lse, lse, dl, dl), semantics=("parallel",), carry=carry)


def _conv_mixer_bwd(abcv, dya, conv_w, conv_b):
    t = abcv.shape[0]
    cw = conv_w.shape[1]
    tm = min(256, t)
    last = t // tm - 1

    def body(a_ref, ap_ref, an_ref, dya_ref, dyan_ref, cw_ref, cb_ref, d_ref, acc_ref):
        m = pl.program_id(0)

        @pl.when(m == 0)
        def _():
            acc_ref[...] = jnp.zeros_like(acc_ref)

        ab = a_ref[:, 0:cw].astype(F32)
        ac = a_ref[:, cw:2 * cw].astype(F32)
        av = a_ref[:, 2 * cw:3 * cw].astype(F32)
        u = ac * av
        hu = ap_ref[:, cw:2 * cw].astype(F32) * ap_ref[:, 2 * cw:3 * cw].astype(F32)
        hu = jnp.where(m > 0, hu, 0.0)
        u1 = _shift_down(u, hu, 1)
        u2 = _shift_down(u, hu, 2)
        cv = cw_ref[0:1, :] * u2 + cw_ref[1:2, :] * u1 + cw_ref[2:3, :] * u + cb_ref[...]
        dya_v = dya_ref[...].astype(F32)
        dcv = dya_v * ab
        ndcv = jnp.where(m < last, dyan_ref[...].astype(F32) * an_ref[:, 0:cw].astype(F32), 0.0)
        du = (cw_ref[2:3, :] * dcv + cw_ref[1:2, :] * _shift_up(dcv, ndcv, 1)
              + cw_ref[0:1, :] * _shift_up(dcv, ndcv, 2))
        d_ref[:, 0:cw] = (dya_v * cv).astype(BF16)
        d_ref[:, cw:2 * cw] = (du * av).astype(BF16)
        d_ref[:, 2 * cw:3 * cw] = (du * ac).astype(BF16)
        acc_ref[...] += _stack_rows([_colsum(dcv * u2), _colsum(dcv * u1), _colsum(dcv * u), _colsum(dcv)], cw)

    return pl.pallas_call(
        body, name="conv_mixer_bwd", grid=(t // tm,),
        in_specs=[_rows(tm, 3 * cw), _prev_halo(tm, 3 * cw), _next_halo(tm, 3 * cw, t), _rows(tm, cw),
                  _next_halo(tm, cw, t), _resident((3, cw)), _resident((1, cw))],
        out_specs=[_rows(tm, 3 * cw), _acc_spec(cw)],
        out_shape=[jax.ShapeDtypeStruct((t, 3 * cw), BF16), jax.ShapeDtypeStruct((SUBLANES, cw), F32)],
        compiler_params=_params("arbitrary"),
    )(abcv, abcv, abcv, dya, dya, conv_w, conv_b)


def _in_proj_bwd(x, dx1, dabcv, dss, dgates, w_in, g1, carry=None):
    t, d = x.shape
    qkv0 = dabcv.shape[1]
    n = w_in.shape[0]
    tm = min(ROWS_MATMUL, t)

    def body(x_ref, dx1_ref, da_ref, ds0_ref, ds1_ref, ds2_ref, dg_ref, w_ref, g_ref, dx_ref, acc_ref):
        m = pl.program_id(0)

        @pl.when(m == 0)
        def _():
            acc_ref[...] = jnp.zeros_like(acc_ref)

        dss_tok = [_load_streams(ds_ref, dil, tm) for ds_ref, dil in zip((ds0_ref, ds1_ref, ds2_ref), DILATIONS)]
        dqkv = jnp.concatenate([ds[:, j * GROUP_W:(j + 1) * GROUP_W] for j in range(3) for ds in dss_tok], axis=1)
        dh = (_dot(da_ref[...], w_ref[0:qkv0, :]) + _dot(dqkv, w_ref[qkv0:qkv0 + 3 * ATTN_W, :])
              + _dot(dg_ref[...], w_ref[qkv0 + 3 * ATTN_W:n, :]))
        dx, dg = _rms_bwd(x_ref[...], g_ref[...], dh)
        dx_ref[...] = dx1_ref[...] + dx
        acc_ref[...] += _stack_rows([_colsum(dg)], d)

    return _call(
        body, name="in_proj_bwd", grid=(t // tm,),
        in_specs=[_rows(tm, d), _rows(tm, d), _rows(tm, qkv0)]
        + [_stream_spec(dil, tm, 3 * GROUP_W) for dil in DILATIONS]
        + [_rows(tm, 2 * d), _resident((n, d)), _resident((1, d))],
        out_specs=[_rows(tm, d), _acc_spec(d)],
        out_shape=[jax.ShapeDtypeStruct((t, d), F32), jax.ShapeDtypeStruct((SUBLANES, d), F32)],
        args=(x, dx1, dabcv, *[_stream_view(a, dil) for a, dil in zip(dss, DILATIONS)], dgates, w_in, g1),
        semantics=("arbitrary",), carry=carry)


def _dw_in_qkv(ds, h, dil):
    t, d = h.shape
    tk = min(1024, t)
    sub = min(256, t)
    width = 3 * GROUP_W

    def body(ds_ref, h_ref, o_ref, acc_ref):
        k = pl.program_id(0)

        @pl.when(k == 0)
        def _():
            acc_ref[...] = jnp.zeros_like(acc_ref)

        upd = None
        for i in range(tk // sub):
            rows = ds_ref[:, i * (sub // dil):(i + 1) * (sub // dil), :].reshape(sub, width)
            if dil > 1:
                rows = _permute_rows(_perm(dil, sub, inverse=True), rows)
            term = _dot_tn(rows, h_ref[i * sub:(i + 1) * sub, :])
            upd = term if upd is None else upd + term
        acc_ref[...] += upd

        @pl.when(k == t // tk - 1)
        def _():
            o_ref[...] = acc_ref[...].astype(BF16)

    return pl.pallas_call(
        body, name=f"dw_in_qkv_d{dil}", grid=(t // tk,),
        in_specs=[_stream_spec(dil, tk, width), _rows(tk, d)],
        out_specs=pl.BlockSpec((width, d), lambda k: (0, 0)),
        out_shape=jax.ShapeDtypeStruct((width, d), BF16),
        scratch_shapes=[pltpu.VMEM((width, d), F32)],
        compiler_params=_params("arbitrary"),
    )(_stream_view(ds, dil), h)


def _local_step(x, target, p, late):
    cw = p["conv_a_w"].shape[1]
    (h, abcv, gates, *ss), (g_up,) = _in_proj(x, p["norm_mix_g"], p["w_in"], cw,
                                              carry=_Exchange("gather", [late["w_up"]]))
    w_up = _full_from_gathered(g_up)
    (o0, lse0), g_proj = _attn_fwd(ss[0], DILATIONS[0],
                                   carry=_Exchange("gather", [late["w_proj_a"], late["w_proj_b"]]))
    (o1, lse1), (g_out,) = _attn_fwd(ss[1], DILATIONS[1], carry=_Exchange("gather", [late["w_out"]]))
    o2, lse2 = _attn_fwd(ss[2], DILATIONS[2])
    w_pa, w_pb, w_out = [_full_from_gathered(g) for g in (*g_proj, g_out)]
    os, lses = (o0, o1, o2), (lse0, lse1, lse2)
    (x1, ya, yb, yap, ybp, merged), (g_down,) = _mixer_out(
        x, abcv, gates, os, lses, p["conv_a_w"], p["conv_a_b"], p["b_gate"], w_pa, w_pb, w_out,
        carry=_Exchange("gather", [late["w_down"]]))
    w_down = _full_from_gathered(g_down)
    h2, up_pre, act, conv, dx2, dx2i, acc_gf, loss = _ffn_fwd(x1, target, p["norm_ffn_g"], w_up, p["ffn_conv_w"],
                                                              p["ffn_conv_b"], w_down, p["final_norm_g"])

    parts, got = {}, {}
    dup, acc_fb = _ffn_act_bwd(dx2i, conv, w_down)
    parts["w_down"] = _by_destination(_tn_matmul(act, dx2i, "dw_down"))
    (dpre, dx1, acc_g2, acc_fw), (got["w_down"],) = _ffn_up_bwd(dup, up_pre, x1, dx2, p["ffn_conv_w"], w_up,
                                                                p["norm_ffn_g"],
                                                                carry=_Exchange("scatter", [parts["w_down"]]))
    parts["w_up"] = _by_destination(_tn_matmul(dpre, h2, "dw_up"))
    dgates, dyap, dybp, dya, dos, dls, acc_bg = _mixer_bwd(dx1, gates, yap, ybp, os, lses, p["b_gate"], w_out,
                                                           w_pa, w_pb)
    parts["w_out"] = _by_destination(_tn_matmul(merged, dx1, "dw_out"))
    parts["w_proj_a"] = _by_destination(_tn_matmul(dyap, ya, "dw_proj_a"))
    parts["w_proj_b"] = _by_destination(_tn_matmul(dybp, yb, "dw_proj_b"))
    minor = ("w_out", "w_proj_a", "w_proj_b")
    half = parts["w_up"].shape[1] // 2
    (ds0,), received = _attn_bwd(ss[0], dos[0], lses[0], dls[0], DILATIONS[0],
                                 carry=_Exchange("scatter", [parts[n] for n in minor]))
    got.update(zip(minor, received))
    (ds1,), first_half = _attn_bwd(ss[1], dos[1], lses[1], dls[1], DILATIONS[1],
                                   carry=_Exchange("scatter", [parts["w_up"]], rows=(0, half)))
    (ds2,), (got["w_up"],) = _attn_bwd(ss[2], dos[2], lses[2], dls[2], DILATIONS[2],
                                       carry=_Exchange("scatter", [parts["w_up"]], rows=(half, half),
                                                       into=first_half))
    dss = [ds0, ds1, ds2]
    dabcv, acc_ca = _conv_mixer_bwd(abcv, dya, p["conv_a_w"], p["conv_a_b"])
    dw_s = [_dw_in_qkv(ds, h, dil) for ds, dil in zip(dss, DILATIONS)]
    dw_qkv = [w[j * GROUP_W:(j + 1) * GROUP_W] for j in range(3) for w in dw_s]
    g_w_in = jnp.concatenate([_tn_matmul(dabcv, h, "dw_in_a"), *dw_qkv, _tn_matmul(dgates, h, "dw_in_g")], axis=0)
    parts["w_in"] = _by_destination(g_w_in)
    (dx, acc_g1), (got["w_in"],) = _in_proj_bwd(x, dx1, dabcv, dss, dgates, p["w_in"], p["norm_mix_g"],
                                                carry=_Exchange("scatter", [parts["w_in"]]))
    small = dict(norm_mix_g=acc_g1[0:1], b_gate=acc_bg[0:2], conv_a_w=acc_ca[0:3], conv_a_b=acc_ca[3:4],
                 norm_ffn_g=acc_g2[0:1], ffn_conv_w=acc_fw[0:3], ffn_conv_b=acc_fb[0:1], final_norm_g=acc_gf[0:1])
    return loss[0, 0], dx, parts, got, small


def _all_gather(shards):
    n = len(shards)

    def body(*refs):
        ins, outs = refs[:n], refs[n:2 * n]
        send_sems, recv_sems, local_sems = refs[2 * n:]
        x, y, c = _mesh_pos()
        me, sibling = (x, y, c), (x, y, 1 - c)
        chips = [(1 - x, y), (x, 1 - y), (1 - x, 1 - y)]

        def copy(i, k, block, to, src=None):
            rows = outs[i].at[_dev_index(*block)]
            return pltpu.make_async_remote_copy(
                src_ref=rows if src is None else src, dst_ref=rows, send_sem=send_sems.at[i, k],
                recv_sem=recv_sems.at[i, k], device_id=to, device_id_type=MESH)

        mine, first, passed = [], [], []
        for i in range(n):
            cp = pltpu.make_async_copy(ins[i], outs[i].at[_dev_index(*me)], local_sems.at[i])
            cp.start()
            mine.append(cp)
            first.append(copy(i, 0, me, sibling, src=ins[i]))
            first += [copy(i, 1 + j, me, (*chip, c), src=ins[i]) for j, chip in enumerate(chips)]
        for cp in first:
            cp.start()
        for i in range(n):
            for j, chip in enumerate(chips):
                copy(i, 1 + j, (*chip, c), me).wait_recv()
                fw = copy(i, 4 + j, (*chip, c), sibling)
                fw.start()
                passed.append(fw)
        for i in range(n):
            copy(i, 0, sibling, me).wait_recv()
            for j, chip in enumerate(chips):
                copy(i, 4 + j, (*chip, 1 - c), me).wait_recv()
        for cp in first + passed:
            cp.wait_send()
        for cp in mine:
            cp.wait()

    return pl.pallas_call(
        body, name="all_gather_weights",
        in_specs=[ANY] * n, out_specs=[ANY] * n,
        out_shape=[jax.ShapeDtypeStruct((N_DEV,) + s.shape, s.dtype) for s in shards],
        scratch_shapes=[pltpu.SemaphoreType.DMA((n, 7)), pltpu.SemaphoreType.DMA((n, 7)),
                        pltpu.SemaphoreType.DMA((n,))],
    )(*shards)


def _all_reduce_small(v):
    r = v.shape[0]

    def body(v_ref, o_ref, gath, send_sems, recv_sems):
        x, y, c = _mesh_pos()
        me = _dev_index(x, y, c)
        gath[me] = v_ref[...]
        flips = [(kx, ky, kc) for kx in (0, 1) for ky in (0, 1) for kc in (0, 1)][1:]
        copies = []
        for k, (kx, ky, kc) in enumerate(flips):
            px = 1 - x if kx else x
            py = 1 - y if ky else y
            pc = 1 - c if kc else c
            cp = pltpu.make_async_remote_copy(
                src_ref=v_ref, dst_ref=gath.at[me], send_sem=send_sems.at[k], recv_sem=recv_sems.at[k],
                device_id=(px, py, pc), device_id_type=MESH)
            cp.start()
            copies.append((cp, _dev_index(px, py, pc)))
        for k, (cp, peer) in enumerate(copies):
            pltpu.make_async_remote_copy(
                src_ref=v_ref, dst_ref=gath.at[peer], send_sem=send_sems.at[k], recv_sem=recv_sems.at[k],
                device_id=(x, y, c), device_id_type=MESH).wait_recv()
        for cp, _ in copies:
            cp.wait_send()
        total = gath[0]
        for j in range(1, N_DEV):
            total = total + gath[j]
        o_ref[...] = total

    return pl.pallas_call(
        body, name="all_reduce_small",
        in_specs=[pl.BlockSpec(memory_space=pltpu.VMEM)], out_specs=pl.BlockSpec(memory_space=pltpu.VMEM),
        out_shape=jax.ShapeDtypeStruct((r, LANES), F32),
        scratch_shapes=[pltpu.VMEM((N_DEV, r, LANES), F32), pltpu.SemaphoreType.DMA((7,)),
                        pltpu.SemaphoreType.DMA((7,))],
    )(v)


def _adamw_math(w, g, m, v):
    m2 = ADAM_B1 * m + (1.0 - ADAM_B1) * g
    v2 = ADAM_B2 * v + (1.0 - ADAM_B2) * (g * g)
    m_hat = m2 / (1.0 - ADAM_B1 ** ADAM_STEP)
    v_hat = v2 / (1.0 - ADAM_B2 ** ADAM_STEP)
    delta = -ADAM_LR * (m_hat / (jnp.sqrt(v_hat) + ADAM_EPS) + ADAM_WD * w)
    return delta, m2, v2


def _adamw_big(w, m, v, part, got, me):
    r, c = part.shape[1:]
    flip = w.shape != (r, c)
    tr = r if flip else max(t for t in range(HALO, min(r, 512) + 1, HALO) if r % t == 0)

    def body(me_ref, w_ref, m_ref, v_ref, own_ref, *rest):
        del me_ref
        got_refs, (g_out, d_out, m_out, v_out) = rest[:N_DEV - 1], rest[N_DEV - 1:]
        g = own_ref[...].astype(F32)
        for ref in got_refs:
            g = g + ref[...].astype(F32)
        if flip:
            g = g.T
        delta, m2, v2 = _adamw_math(w_ref[...], g, m_ref[...], v_ref[...])
        g_out[...] = g
        d_out[...] = delta
        m_out[...] = m2
        v_out[...] = v2

    def peer_block(k):
        return pl.BlockSpec((None, tr, c), lambda i, me_ref: (jnp.bitwise_xor(me_ref[0], k), i, 0))

    plain = pl.BlockSpec(w.shape if flip else (tr, c), lambda i, me_ref: (i, 0))
    out = jax.ShapeDtypeStruct(w.shape, F32)
    return pl.pallas_call(
        body, name="adamw_big",
        grid_spec=pltpu.PrefetchScalarGridSpec(
            num_scalar_prefetch=1, grid=(r // tr,),
            in_specs=[plain, plain, plain] + [peer_block(k) for k in range(N_DEV)],
            out_specs=[plain] * 4),
        out_shape=[out] * 4,
        compiler_params=_params("parallel"),
    )(me, w, m, v, part, *([got] * (N_DEV - 1)))


def _adamw_small(ws, gs, ms, vs):
    n = len(ws)

    def body(*refs):
        ins, outs = refs[:4 * n], refs[4 * n:]
        for i in range(n):
            delta, m2, v2 = _adamw_math(ins[i][...], ins[n + i][...], ins[2 * n + i][...], ins[3 * n + i][...])
            outs[i][...] = delta
            outs[n + i][...] = m2
            outs[2 * n + i][...] = v2

    out = [jax.ShapeDtypeStruct(w.shape, F32) for w in ws]
    res = pl.pallas_call(body, name="adamw_small", out_shape=out * 3)(*ws, *gs, *ms, *vs)
    return res[:n], res[n:2 * n], res[2 * n:]


BIG = ("w_in", "w_proj_a", "w_proj_b", "w_out", "w_up", "w_down")
LATE = ("w_proj_a", "w_proj_b", "w_out", "w_up", "w_down")
COLUMN_SHARDED = ("w_in", "w_proj_a", "w_proj_b", "w_up")
WIDE_COLUMN_SHARDED = ("w_in", "w_up")
SMALL = ("norm_mix_g", "b_gate", "conv_a_w", "conv_a_b", "norm_ffn_g", "ffn_conv_w", "ffn_conv_b", "final_norm_g")
SMALL_SHARDED = ("b_gate", "conv_a_w", "ffn_conv_w")
WEIGHTS = ("norm_mix_g", "w_in", "b_gate", "conv_a_w", "conv_a_b", "w_proj_a", "w_proj_b", "w_out", "norm_ffn_g",
           "w_up", "ffn_conv_w", "ffn_conv_b", "w_down", "final_norm_g")


def _pack(vectors, rows):
    flat = jnp.concatenate([v.reshape(-1) for v in vectors])
    return jnp.pad(flat, (0, rows * LANES - flat.shape[0])).reshape(rows, LANES)


def _packed_rows(count):
    rows = -(-count // LANES)
    return -(-rows // SUBLANES) * SUBLANES


def _unpack(packed, shapes):
    flat = packed.reshape(-1)
    out, lo = [], 0
    for s in shapes:
        size = 1
        for dim in s:
            size *= dim
        out.append(flat[lo:lo + size].reshape(s))
        lo += size
    return out


def _full_from_gathered(gathered):
    _, r, c = gathered.shape
    return gathered.reshape(N_DEV * r, c)


def _by_destination(grad):
    rr, cc = grad.shape
    return grad.reshape(N_DEV, rr // N_DEV, cc)


def _block2d(name, a):
    a = a.reshape(a.shape[-2:])
    return a.T if name in WIDE_COLUMN_SHARDED else a


def kernel(x, norm_mix_g, w_in, b_gate, conv_a_w, conv_a_b, w_proj_a, w_proj_b, w_out, norm_ffn_g, w_up, ffn_conv_w, ffn_conv_b, w_down, final_norm_g, loss_target, m_norm_mix_g, m_w_in, m_b_gate, m_conv_a_w, m_conv_a_b, m_w_proj_a, m_w_proj_b, m_w_out, m_norm_ffn_g, m_w_up, m_ffn_conv_w, m_ffn_conv_b, m_w_down, m_final_norm_g, v_norm_mix_g, v_w_in, v_b_gate, v_conv_a_w, v_conv_a_b, v_w_proj_a, v_w_proj_b, v_w_out, v_norm_ffn_g, v_w_up, v_ffn_conv_w, v_ffn_conv_b, v_w_down, v_final_norm_g):
    given = dict(locals())
    shard = {n: given[n] for n in WEIGHTS}
    mom_m = {n: given["m_" + n] for n in WEIGHTS}
    mom_v = {n: given["v_" + n] for n in WEIGHTS}
    xi, yi, ci = _mesh_pos()
    me = _dev_index(xi, yi, ci)
    me1 = me.astype(jnp.int32).reshape(1)

    big2d = {n: _block2d(n, shard[n]) for n in BIG}
    small_shapes = [shard[n].shape[1:] for n in SMALL_SHARDED]
    n_small = sum(s[0] * s[1] for s in small_shapes)
    packed_small = _pack([shard[n] for n in SMALL_SHARDED], _packed_rows(n_small))
    gathered = _all_gather([big2d["w_in"].astype(BF16), packed_small])
    p = {"w_in": _full_from_gathered(gathered[0])}
    flat_small = gathered[-1].reshape(N_DEV, -1)
    lo = 0
    for n, (rows, width) in zip(SMALL_SHARDED, small_shapes):
        blocks = flat_small[:, lo:lo + rows * width].reshape(N_DEV, rows, width)
        p[n] = blocks.transpose(1, 0, 2).reshape(rows, N_DEV * width)
        lo += rows * width
    p["norm_mix_g"], p["norm_ffn_g"] = shard["norm_mix_g"], shard["norm_ffn_g"]
    p["conv_a_b"], p["ffn_conv_b"] = shard["conv_a_b"], shard["ffn_conv_b"]
    p["final_norm_g"] = shard["final_norm_g"].reshape(1, -1)
    late = {n: (big2d[n].T if n in ("w_proj_a", "w_proj_b") else big2d[n]).astype(BF16) for n in LATE}

    loss_part, dx, parts, got, g_small = _local_step(x[0], loss_target[0], p, late)

    results = {}
    for n in BIG:
        outs = _adamw_big(big2d[n], _block2d(n, mom_m[n]), _block2d(n, mom_v[n]), parts[n], got[n], me1)
        results[n] = [_block2d(n, o).reshape(shard[n].shape) for o in outs]

    small_full_shapes = [g_small[n].shape for n in SMALL]
    n_vec = sum(s[0] * s[1] for s in small_full_shapes) + 1
    packed = _pack([g_small[n] for n in SMALL] + [loss_part.reshape(1)], _packed_rows(n_vec))
    reduced = _all_reduce_small(packed)
    *g_full, loss_vec = _unpack(reduced, small_full_shapes + [(1,)])
    loss = loss_vec[0]
    own_g = []
    for n, g in zip(SMALL, g_full):
        if n in SMALL_SHARDED:
            width = shard[n].shape[-1]
            g = lax.dynamic_slice_in_dim(g, me * width, width, axis=1)
        own_g.append(g.reshape(shard[n].shape))
    def rows2d(a):
        return a.reshape(-1, a.shape[-1])

    deltas, new_ms, new_vs = _adamw_small([rows2d(shard[n]) for n in SMALL], [rows2d(g) for g in own_g],
                                          [rows2d(mom_m[n]) for n in SMALL], [rows2d(mom_v[n]) for n in SMALL])
    for i, n in enumerate(SMALL):
        results[n] = [own_g[i]] + [a.reshape(shard[n].shape) for a in (deltas[i], new_ms[i], new_vs[i])]

    grad_x = dx.reshape(x.shape)
    return (loss, grad_x, *[results[n][0] for n in WEIGHTS], *[results[n][1] for n in WEIGHTS],
            *[results[n][2] for n in WEIGHTS], *[results[n][3] for n in WEIGHTS])
```

```python
import functools

import jax
import jax.numpy as jnp
from jax import lax
from jax.experimental import pallas as pl
from jax.experimental.pallas import tpu as pltpu

F32 = jnp.float32
BF16 = jnp.bfloat16
MESH = pl.DeviceIdType.MESH

N_DEV = 8
RMS_EPS = 1e-6
NEG_INF = -1e30
N_GROUPS = 3
DILATIONS = (1, 4, 16)
HEADS_PER_GROUP = 4
HEAD_DIM = 64
GROUP_W = HEADS_PER_GROUP * HEAD_DIM
ATTN_W = N_GROUPS * GROUP_W
QBLK = 128
STEP_BLOCKS = 4
ATTN_SCALE = HEAD_DIM ** -0.5

ADAM_LR = 0.001
ADAM_B1 = 0.9
ADAM_B2 = 0.999
ADAM_EPS = 1e-08
ADAM_WD = 0.01
ADAM_STEP = 10

PERM_TOKENS = 256
ROWS_MATMUL = 512
HALO = 16
LANES = 128
SUBLANES = 8
VMEM_LIMIT_BYTES = 56 * 1024 * 1024


def _params(*sem):
    return pltpu.CompilerParams(dimension_semantics=sem, vmem_limit_bytes=VMEM_LIMIT_BYTES)


def _pick_tile(n, cap):
    if n <= cap:
        return n
    best = None
    for t in range(LANES, cap + 1, LANES):
        if n % t == 0:
            best = t
    assert best is not None, (n, cap)
    return best


def _rows(tm, c, j=0):
    return pl.BlockSpec((tm, c), lambda m: (m, j))


def _prev_halo(tm, c):
    return pl.BlockSpec((HALO, c), lambda m: (jnp.maximum(m * (tm // HALO) - 1, 0), 0))


def _next_halo(tm, c, t_total):
    last = t_total // HALO - 1
    return pl.BlockSpec((HALO, c), lambda m: (jnp.minimum((m + 1) * (tm // HALO), last), 0))


def _resident(shape):
    nd = len(shape)
    return pl.BlockSpec(shape, lambda *_: (0,) * nd, pipeline_mode=pl.Buffered(1))


def _acc_spec(c):
    return pl.BlockSpec((SUBLANES, c), lambda *_: (0, 0))


def _shift_down(u, halo, k):
    edge = jnp.concatenate([halo[HALO - SUBLANES:], u[:SUBLANES]], axis=0)
    head = pltpu.roll(edge, k, 0)[SUBLANES:]
    return jnp.concatenate([head, pltpu.roll(u, k, 0)[SUBLANES:]], axis=0)


def _shift_up(u, halo, k):
    n = u.shape[0]
    edge = jnp.concatenate([u[n - SUBLANES:], halo[:SUBLANES]], axis=0)
    tail = pltpu.roll(edge, 2 * SUBLANES - k, 0)[:SUBLANES]
    return jnp.concatenate([pltpu.roll(u, n - k, 0)[:n - SUBLANES], tail], axis=0)


def _interleave(tm, inverse=False):
    return _perm(tm // SUBLANES, tm, inverse)


def _edge_groups(u, halo, k, from_end):
    n = u.shape[0]
    sub = lax.broadcasted_iota(jnp.int32, (SUBLANES, u.shape[1]), 0)
    out = []
    for j in range(2 - k, 2):
        lo = n - HALO + j * SUBLANES if from_end else j * SUBLANES
        own, other = u[lo:lo + SUBLANES], halo[j * SUBLANES:(j + 1) * SUBLANES]
        if from_end:
            out.append(pltpu.roll(jnp.where(sub == SUBLANES - 1, other, own), 1, 0))
        else:
            out.append(pltpu.roll(jnp.where(sub == 0, other, own), SUBLANES - 1, 0))
    return out


def _shift_down_il(u, halo, k):
    return jnp.concatenate(_edge_groups(u, halo, k, True) + [u[:u.shape[0] - k * SUBLANES]], axis=0)


def _shift_up_il(u, halo, k):
    if k == 1:
        edge = _edge_groups(u, halo, 2, False)[:1]
    else:
        edge = _edge_groups(u, halo, 2, False)
    return jnp.concatenate([u[k * SUBLANES:]] + edge, axis=0)


def _stack_rows(rows, c):
    idx = lax.broadcasted_iota(jnp.int32, (SUBLANES, c), 0)
    out = jnp.zeros((SUBLANES, c), F32)
    for i, r in enumerate(rows):
        out = out + jnp.where(idx == i, r, 0.0)
    return out


def _colsum(v):
    return jnp.sum(v, axis=0, keepdims=True)


def _sigmoid(v):
    return 0.5 * jnp.tanh(0.5 * v) + 0.5


def _rms_fwd(xv, g):
    r = lax.rsqrt(jnp.mean(xv * xv, axis=-1, keepdims=True) + RMS_EPS)
    return xv * r * g, r


def _rms_bwd(xv, g, dy):
    r = lax.rsqrt(jnp.mean(xv * xv, axis=-1, keepdims=True) + RMS_EPS)
    xn = xv * r
    dxn = dy * g
    dx = r * (dxn - xn * jnp.mean(dxn * xn, axis=-1, keepdims=True))
    return dx, dy * xn


def _dot(a, b):
    return jnp.dot(a, b, preferred_element_type=F32)


def _dot_nt(a, b):
    return lax.dot_general(a, b, (((1,), (1,)), ((), ())), preferred_element_type=F32)


def _dot_tn(a, b):
    return lax.dot_general(a, b, (((0,), (0,)), ((), ())), preferred_element_type=F32)


def _perm(dil, n, inverse=False):
    i = lax.broadcasted_iota(jnp.int32, (n, n), 0)
    j = lax.broadcasted_iota(jnp.int32, (n, n), 1)
    if inverse:
        i, j = j, i
    per = n // dil
    return (j == (i % per) * dil + i // per).astype(BF16)


def _permute_rows(pm, v):
    if v.dtype == BF16:
        return _dot(pm, v).astype(BF16)
    hi = v.astype(BF16)
    lo = (v - hi.astype(F32)).astype(BF16)
    return _dot(pm, hi) + _dot(pm, lo)


def _stream_view(a, dil):
    t, c = a.shape
    return a.reshape(dil, t // dil, c)


def _stream_spec(dil, tm, c):
    return pl.BlockSpec((dil, tm // dil, c), lambda m: (0, m, 0))


def _load_streams(ref, dil, tm):
    c = ref.shape[-1]
    if dil == 1:
        return ref[...].reshape(tm, c)
    sub = min(PERM_TOKENS, tm)
    pm = _perm(dil, sub, inverse=True)
    parts = [_permute_rows(pm, ref[:, i * (sub // dil):(i + 1) * (sub // dil), :].reshape(sub, c))
             for i in range(tm // sub)]
    return parts[0] if len(parts) == 1 else jnp.concatenate(parts, axis=0)


def _store_streams(ref, dil, tm, v):
    if dil == 1:
        ref[...] = v.reshape(ref.shape).astype(ref.dtype)
        return
    sub = min(PERM_TOKENS, tm)
    pm = _perm(dil, sub)
    for i in range(tm // sub):
        piece = _permute_rows(pm, v[i * sub:(i + 1) * sub])
        ref[:, i * (sub // dil):(i + 1) * (sub // dil), :] = piece.reshape(dil, sub // dil, -1).astype(ref.dtype)


ANY = pl.BlockSpec(memory_space=pl.ANY)


def _mesh_pos():
    return lax.axis_index("x"), lax.axis_index("y"), lax.axis_index("c")


def _dev_index(px, py, pc):
    return 4 * px + 2 * py + pc


class _Exchange:
    def __init__(self, mode, arrays, rows=None, into=()):
        self.mode, self.arrays, self.rows, self.into = mode, list(arrays), rows, list(into)
        n = len(self.arrays)
        if mode == "gather":
            self.out_shape = [jax.ShapeDtypeStruct((N_DEV,) + a.shape, a.dtype) for a in self.arrays]
        else:
            self.out_shape = [jax.ShapeDtypeStruct(a.shape, a.dtype) for a in self.arrays]
        self.scratch = [pltpu.SemaphoreType.DMA((n, N_DEV - 1)), pltpu.SemaphoreType.DMA((n, N_DEV - 1)),
                        pltpu.SemaphoreType.DMA((n,))]

    def _peers(self):
        x, y, c = _mesh_pos()
        flips = [(kx, ky, kc) for kx in (0, 1) for ky in (0, 1) for kc in (0, 1)][1:]
        peers = [(1 - x if kx else x, 1 - y if ky else y, 1 - c if kc else c) for kx, ky, kc in flips]
        return _dev_index(x, y, c), peers

    def _copy(self, ins, outs, sems, i, k, peer, me, sending):
        src = ins[i] if self.mode == "gather" else ins[i].at[_dev_index(*peer)]
        dst = outs[i].at[me if sending else _dev_index(*peer)]
        if self.rows is not None:
            src, dst = src.at[pl.ds(*self.rows)], dst.at[pl.ds(*self.rows)]
        return pltpu.make_async_remote_copy(src_ref=src, dst_ref=dst, send_sem=sems[0].at[i, k],
                                            recv_sem=sems[1].at[i, k], device_id=peer, device_id_type=MESH)

    def _own(self, ins, outs, sems, i, me):
        return pltpu.make_async_copy(ins[i], outs[i].at[me], sems[2].at[i])

    def start(self, ins, outs, sems):
        me, peers = self._peers()
        for i in range(len(ins)):
            if self.mode == "gather":
                self._own(ins, outs, sems, i, me).start()
            for k, peer in enumerate(peers):
                self._copy(ins, outs, sems, i, k, peer, me, True).start()

    def wait(self, ins, outs, sems):
        me, peers = self._peers()
        for i in range(len(ins)):
            for k, peer in enumerate(peers):
                self._copy(ins, outs, sems, i, k, peer, me, False).wait_recv()
            for k, peer in enumerate(peers):
                self._copy(ins, outs, sems, i, k, peer, me, True).wait_send()
            if self.mode == "gather":
                self._own(ins, outs, sems, i, me).wait()


def _call(body, *, name, grid, in_specs, out_specs, out_shape, args, semantics, carry=None, scratch=()):
    if carry is None:
        return pl.pallas_call(body, name=name, grid=grid, in_specs=in_specs, out_specs=out_specs,
                              out_shape=out_shape, scratch_shapes=list(scratch),
                              compiler_params=_params(*semantics))(*args)
    n_in, n_out, n_x, n_s = len(in_specs), len(out_specs), len(carry.arrays), len(scratch)
    n_into = len(carry.into)
    all_in = n_in + n_x + n_into

    def carried(*refs):
        ins, x_ins = refs[:n_in], refs[n_in:n_in + n_x]
        outs = refs[all_in:all_in + n_out]
        x_outs = refs[all_in + n_out:all_in + n_out + n_x]
        own = refs[all_in + n_out + n_x:all_in + n_out + n_x + n_s]
        sems = refs[all_in + n_out + n_x + n_s:]
        first = functools.reduce(jnp.logical_and, [pl.program_id(a) == 0 for a in range(len(grid))])
        last = functools.reduce(jnp.logical_and, [pl.program_id(a) == grid[a] - 1 for a in range(len(grid))])

        @pl.when(first)
        def _():
            carry.start(x_ins, x_outs, sems)

        body(*ins, *outs, *own)

        @pl.when(last)
        def _():
            carry.wait(x_ins, x_outs, sems)

    res = pl.pallas_call(
        carried, name=name, grid=grid, in_specs=list(in_specs) + [ANY] * (n_x + n_into),
        out_specs=list(out_specs) + [ANY] * n_x, out_shape=list(out_shape) + carry.out_shape,
        input_output_aliases={n_in + n_x + i: n_out + i for i in range(n_into)},
        scratch_shapes=list(scratch) + carry.scratch, compiler_params=_params(*["arbitrary"] * len(grid)),
    )(*args, *carry.arrays, *carry.into)
    return list(res[:n_out]), list(res[n_out:])


def _in_proj(x, g, wt, cw, carry=None):
    t, d = x.shape
    n = wt.shape[0]
    tm = min(ROWS_MATMUL, t)
    qkv0 = 3 * cw

    def body(x_ref, g_ref, wt_ref, h_ref, abcv_ref, gates_ref, *s_refs):
        h = _rms_fwd(x_ref[...], g_ref[...])[0].astype(BF16)
        h_ref[...] = h
        abcv_ref[...] = _dot_nt(h, wt_ref[0:qkv0, :]).astype(BF16)
        gates_ref[...] = _dot_nt(h, wt_ref[qkv0 + 3 * ATTN_W:n, :]).astype(BF16)
        qkv = _dot_nt(h, wt_ref[qkv0:qkv0 + 3 * ATTN_W, :]).astype(BF16)
        for gi, s_ref in enumerate(s_refs):
            cols = [qkv[:, j * ATTN_W + gi * GROUP_W:j * ATTN_W + (gi + 1) * GROUP_W] for j in range(3)]
            _store_streams(s_ref, DILATIONS[gi], tm, jnp.concatenate(cols, axis=1))

    return _call(
        body, name="in_proj", grid=(t // tm,),
        in_specs=[_rows(tm, d), _resident((1, d)), _resident((n, d))],
        out_specs=[_rows(tm, d), _rows(tm, qkv0), _rows(tm, 2 * d)]
        + [_stream_spec(dil, tm, 3 * GROUP_W) for dil in DILATIONS],
        out_shape=[jax.ShapeDtypeStruct((t, d), BF16), jax.ShapeDtypeStruct((t, qkv0), BF16),
                   jax.ShapeDtypeStruct((t, 2 * d), BF16)]
        + [jax.ShapeDtypeStruct((dil, t // dil, 3 * GROUP_W), BF16) for dil in DILATIONS],
        args=(x, g, wt), semantics=("parallel",), carry=carry)


def _head_masks():
    lane = lax.broadcasted_iota(jnp.int32, (1, GROUP_W), 1)
    return lane, [(lane // HEAD_DIM) == h for h in range(HEADS_PER_GROUP)]


def _stack_heads(v, heads):
    return jnp.concatenate([jnp.where(hm, v, jnp.zeros_like(v)) for hm in heads], axis=0)


def _merge_heads(v, heads):
    out = jnp.zeros((QBLK, GROUP_W), v.dtype)
    for h, hm in enumerate(heads):
        out = jnp.where(hm, v[h * QBLK:(h + 1) * QBLK], out)
    return out


def _pair_block(col):
    return pl.BlockSpec((STEP_BLOCKS * QBLK, GROUP_W), lambda b: (b, col))


def _edge_block(col, shift, nb):
    return pl.BlockSpec((QBLK, GROUP_W), lambda b: (jnp.clip(STEP_BLOCKS * b + shift, 0, nb - 1), col))


def _band_mask(has_prev):
    rows = HEADS_PER_GROUP * QBLK
    row = lax.broadcasted_iota(jnp.int32, (rows, 2 * QBLK), 0) & (QBLK - 1)
    col = lax.broadcasted_iota(jnp.int32, (rows, 2 * QBLK), 1)
    return ((col < QBLK) & (col >= row) & has_prev) | ((col >= QBLK) & (col - QBLK <= row))


def _next_mask(has_next):
    rows = HEADS_PER_GROUP * QBLK
    row = lax.broadcasted_iota(jnp.int32, (rows, QBLK), 0) & (QBLK - 1)
    col = lax.broadcasted_iota(jnp.int32, (rows, QBLK), 1)
    return (col >= row) & has_next


def _attn_fwd(s, dil, carry=None):
    t = s.shape[0] * s.shape[1]
    nb = t // QBLK
    per_stream = nb // dil
    assert per_stream % STEP_BLOCKS == 0

    def body(q_ref, kc_ref, kp_ref, vc_ref, vp_ref, o_ref, lse_ref):
        b = pl.program_id(0)
        _, heads = _head_masks()
        first_has_prev = lax.rem(STEP_BLOCKS * b, per_stream) != 0
        for j in range(STEP_BLOCKS):
            rows = slice(j * QBLK, (j + 1) * QBLK)
            if j == 0:
                k2 = jnp.concatenate([kp_ref[...], kc_ref[rows, :]], axis=0)
                v2 = jnp.concatenate([vp_ref[...], vc_ref[rows, :]], axis=0)
            else:
                both = slice((j - 1) * QBLK, (j + 1) * QBLK)
                k2, v2 = kc_ref[both, :], vc_ref[both, :]
            mask = _band_mask(first_has_prev if j == 0 else True)
            sc = jnp.where(mask, _dot_nt(_stack_heads(q_ref[rows, :], heads), k2) * ATTN_SCALE, NEG_INF)
            mx = jnp.max(sc, axis=1, keepdims=True)
            pr = jnp.exp(sc - mx)
            den = jnp.sum(pr, axis=1, keepdims=True)
            o_all = _dot(pr.astype(BF16), v2) / den
            o_ref[rows, :] = _merge_heads(o_all, heads).astype(BF16)
            lse_ref[rows, :] = _merge_heads(jnp.broadcast_to(mx + jnp.log(den), o_all.shape), heads)

    sv = s.reshape(t, 3 * GROUP_W)
    return _call(
        body, name=f"attn_fwd_d{dil}", grid=(nb // STEP_BLOCKS,),
        in_specs=[_pair_block(0), _pair_block(1), _edge_block(1, -1, nb), _pair_block(2), _edge_block(2, -1, nb)],
        out_specs=[_pair_block(0), _pair_block(0)],
        out_shape=[jax.ShapeDtypeStruct((t, GROUP_W), BF16), jax.ShapeDtypeStruct((t, GROUP_W), F32)],
        args=(sv, sv, sv, sv, sv), semantics=("parallel",), carry=carry)


def _group_softmax(parts):
    mx = jnp.maximum(jnp.maximum(parts[0], parts[1]), parts[2])
    es = [jnp.exp(p - mx) for p in parts]
    den = es[0] + es[1] + es[2]
    return [e / den for e in es]


def _mixer_out(x, abcv, gates, os, lses, conv_w, conv_b, b_gate, w_pa, w_pb, w_o, carry=None):
    t, d = x.shape
    cw = conv_w.shape[1]
    tm = min(ROWS_MATMUL, t)

    def body(x_ref, abcv_ref, halo_ref, gates_ref, o0_ref, o1_ref, o2_ref, l0_ref, l1_ref, l2_ref, cw_ref, cb_ref,
             bg_ref, wpa_ref, wpb_ref, wo_ref, x1_ref, ya_ref, yb_ref, yap_ref, ybp_ref, mg_ref):
        m = pl.program_id(0)
        ab = abcv_ref[:, 0:cw].astype(F32)
        u = abcv_ref[:, cw:2 * cw].astype(F32) * abcv_ref[:, 2 * cw:3 * cw].astype(F32)
        hu = halo_ref[:, cw:2 * cw].astype(F32) * halo_ref[:, 2 * cw:3 * cw].astype(F32)
        hu = jnp.where(m > 0, hu, 0.0)
        cv = (cw_ref[0:1, :] * _shift_down(u, hu, 2) + cw_ref[1:2, :] * _shift_down(u, hu, 1)
              + cw_ref[2:3, :] * u + cb_ref[...])
        ya = (ab * cv).astype(BF16)
        ya_ref[...] = ya
        alphas = _group_softmax([_load_streams(r, dil, tm) for r, dil in zip((l0_ref, l1_ref, l2_ref), DILATIONS)])
        for i, (o_ref, dil) in enumerate(zip((o0_ref, o1_ref, o2_ref), DILATIONS)):
            sl = slice(i * GROUP_W, (i + 1) * GROUP_W)
            yb_ref[:, sl] = (alphas[i] * _load_streams(o_ref, dil, tm).astype(F32)).astype(BF16)
        yap = _dot_nt(ya, wpa_ref[...])
        ybp = _dot_nt(yb_ref[...], wpb_ref[...])
        yap_ref[...] = yap.astype(BF16)
        ybp_ref[...] = ybp.astype(BF16)
        sa = _sigmoid(gates_ref[:, 0:d].astype(F32) + bg_ref[0:1, :])
        sb = _sigmoid(gates_ref[:, d:2 * d].astype(F32) + bg_ref[1:2, :])
        merged = (sa * yap + sb * ybp).astype(BF16)
        mg_ref[...] = merged
        x1_ref[...] = x_ref[...] + _dot(merged, wo_ref[...])

    return _call(
        body, name="mixer_out", grid=(t // tm,),
        in_specs=[_rows(tm, d), _rows(tm, 3 * cw), _prev_halo(tm, 3 * cw), _rows(tm, 2 * d)]
        + [_stream_spec(dil, tm, GROUP_W) for dil in DILATIONS] * 2
        + [_resident((3, cw)), _resident((1, cw)), _resident((2, d)),
           _resident((d, cw)), _resident((d, ATTN_W)), _resident((d, d))],
        out_specs=[_rows(tm, d), _rows(tm, cw), _rows(tm, ATTN_W), _rows(tm, d), _rows(tm, d), _rows(tm, d)],
        out_shape=[jax.ShapeDtypeStruct((t, d), F32), jax.ShapeDtypeStruct((t, cw), BF16),
                   jax.ShapeDtypeStruct((t, ATTN_W), BF16), jax.ShapeDtypeStruct((t, d), BF16),
                   jax.ShapeDtypeStruct((t, d), BF16), jax.ShapeDtypeStruct((t, d), BF16)],
        args=(x, abcv, abcv, gates, *[_stream_view(a, dil) for a, dil in zip(os, DILATIONS)],
              *[_stream_view(a, dil) for a, dil in zip(lses, DILATIONS)], conv_w, conv_b, b_gate, w_pa, w_pb, w_o),
        semantics=("parallel",), carry=carry)


def _ffn_fwd(x1, target, g2, w_ut, conv_w, conv_b, w_d, g_f, carry=None):
    t, d = x1.shape
    dff = w_d.shape[0]
    tm = min(256, t)
    ck = _pick_tile(dff, 2816)

    def body(x1_ref, tg_ref, g2_ref, wut_ref, cw_ref, cb_ref, wd_ref, gf_ref, h2_ref, up_ref, act_ref, conv_ref,
             dx2_ref, dx2i_ref, acc_ref, loss_ref, halo_ref):
        m = pl.program_id(0)

        @pl.when(m == 0)
        def _():
            acc_ref[...] = jnp.zeros_like(acc_ref)
            loss_ref[...] = jnp.zeros_like(loss_ref)
            halo_ref[...] = jnp.zeros_like(halo_ref)

        h2 = _permute_rows(_interleave(tm), _rms_fwd(x1_ref[...], g2_ref[...])[0].astype(BF16))
        h2_ref[...] = h2

        def conv(c0):
            p = _dot_nt(h2, wut_ref[c0:c0 + ck, :])
            up_ref[:, c0:c0 + ck] = p.astype(BF16)
            hp = halo_ref[:, c0:c0 + ck]
            halo_ref[:, c0:c0 + ck] = p[tm - HALO:, :]
            return (cw_ref[0:1, c0:c0 + ck] * _shift_down_il(p, hp, 2)
                    + cw_ref[1:2, c0:c0 + ck] * _shift_down_il(p, hp, 1)
                    + cw_ref[2:3, c0:c0 + ck] * p + cb_ref[:, c0:c0 + ck])

        down = jnp.zeros((tm, d), F32)
        for c0 in range(0, dff, ck):
            gate = conv(c0)
            val = conv(dff + c0)
            conv_ref[:, c0:c0 + ck] = gate.astype(BF16)
            conv_ref[:, dff + c0:dff + c0 + ck] = val.astype(BF16)
            act = (gate * _sigmoid(gate) * val).astype(BF16)
            act_ref[:, c0:c0 + ck] = act
            down = down + _dot(act, wd_ref[c0:c0 + ck, :])
        x2 = x1_ref[...] + _permute_rows(_interleave(tm, inverse=True), down)
        y, _ = _rms_fwd(x2, gf_ref[...])
        diff = y - tg_ref[...]
        loss_ref[...] += 0.5 * jnp.sum(jnp.mean(diff * diff, axis=-1, keepdims=True))
        dx2, dg = _rms_bwd(x2, gf_ref[...], diff * (1.0 / d))
        dx2_ref[...] = dx2
        dx2i_ref[...] = _permute_rows(_interleave(tm), dx2.astype(BF16))
        acc_ref[...] += _stack_rows([_colsum(dg)], d)

    return _call(
        body, name="ffn_fwd", grid=(t // tm,),
        in_specs=[_rows(tm, d), _rows(tm, d), _resident((1, d)), _resident((2 * dff, d)), _resident((3, 2 * dff)),
                  _resident((1, 2 * dff)), _resident((dff, d)), _resident((1, d))],
        out_specs=[_rows(tm, d), _rows(tm, 2 * dff), _rows(tm, dff), _rows(tm, 2 * dff), _rows(tm, d), _rows(tm, d),
                   _acc_spec(d), _acc_spec(LANES)],
        out_shape=[jax.ShapeDtypeStruct((t, d), BF16), jax.ShapeDtypeStruct((t, 2 * dff), BF16),
                   jax.ShapeDtypeStruct((t, dff), BF16), jax.ShapeDtypeStruct((t, 2 * dff), BF16),
                   jax.ShapeDtypeStruct((t, d), F32), jax.ShapeDtypeStruct((t, d), BF16),
                   jax.ShapeDtypeStruct((SUBLANES, d), F32), jax.ShapeDtypeStruct((SUBLANES, LANES), F32)],
        args=(x1, target, g2, w_ut, conv_w, conv_b, w_d, g_f), semantics=("arbitrary",), carry=carry,
        scratch=[pltpu.VMEM((HALO, 2 * dff), F32)])


def _ffn_act_bwd(dx2, conv, w_d):
    t, d = dx2.shape
    dff = w_d.shape[0]
    tm = min(256, t)
    ck = _pick_tile(dff, 2816)

    def body(dx2_ref, conv_ref, wd_ref, dup_ref, acc_ref):
        m = pl.program_id(0)

        @pl.when(m == 0)
        def _():
            acc_ref[...] = jnp.zeros_like(acc_ref)

        dx2v = dx2_ref[...]
        for c0 in range(0, dff, ck):
            dact = _dot_nt(dx2v, wd_ref[c0:c0 + ck, :])
            gate = conv_ref[:, c0:c0 + ck].astype(F32)
            val = conv_ref[:, dff + c0:dff + c0 + ck].astype(F32)
            sg = _sigmoid(gate)
            dval = dact * gate * sg
            dgate = dact * val * sg * (1.0 + gate * (1.0 - sg))
            dup_ref[:, c0:c0 + ck] = dgate.astype(BF16)
            dup_ref[:, dff + c0:dff + c0 + ck] = dval.astype(BF16)
            acc_ref[:, c0:c0 + ck] += _stack_rows([_colsum(dgate)], ck)
            acc_ref[:, dff + c0:dff + c0 + ck] += _stack_rows([_colsum(dval)], ck)

    return pl.pallas_call(
        body, name="ffn_act_bwd", grid=(t // tm,),
        in_specs=[_rows(tm, d), _rows(tm, 2 * dff), _resident((dff, d))],
        out_specs=[_rows(tm, 2 * dff), _acc_spec(2 * dff)],
        out_shape=[jax.ShapeDtypeStruct((t, 2 * dff), BF16), jax.ShapeDtypeStruct((SUBLANES, 2 * dff), F32)],
        compiler_params=_params("arbitrary"),
    )(dx2, conv, w_d)


def _ffn_up_bwd(dup, up_pre, x1, dx2, conv_w, w_u, g2, carry=None):
    t, d = x1.shape
    n = dup.shape[1]
    tm = min(256, t)
    ck = _pick_tile(n, 256)
    last = t // tm - 1

    def body(dup_ref, nxt_ref, up_ref, x1_ref, dx2_ref, cw_ref, wu_ref, g2_ref, dpre_ref, dx1_ref, acc_ref, accw_ref):
        m = pl.program_id(0)

        @pl.when(m == 0)
        def _():
            acc_ref[...] = jnp.zeros_like(acc_ref)
            accw_ref[...] = jnp.zeros_like(accw_ref)

        dh = jnp.zeros((tm, d), F32)
        for c0 in range(0, n, ck):
            du = dup_ref[:, c0:c0 + ck].astype(F32)
            hn = jnp.where(m < last, nxt_ref[:, c0:c0 + ck].astype(F32), 0.0)
            du1 = _shift_up_il(du, hn, 1)
            du2 = _shift_up_il(du, hn, 2)
            dpre = (cw_ref[2:3, c0:c0 + ck] * du + cw_ref[1:2, c0:c0 + ck] * du1
                    + cw_ref[0:1, c0:c0 + ck] * du2).astype(BF16)
            dpre_ref[:, c0:c0 + ck] = dpre
            dh = dh + _dot(dpre, wu_ref[c0:c0 + ck, :])
            p = up_ref[:, c0:c0 + ck].astype(F32)
            accw_ref[:, c0:c0 + ck] += _stack_rows([_colsum(du2 * p), _colsum(du1 * p), _colsum(du * p)], ck)
        dh = _permute_rows(_interleave(tm, inverse=True), dh)
        dx, dg = _rms_bwd(x1_ref[...], g2_ref[...], dh)
        dx1_ref[...] = dx2_ref[...] + dx
        acc_ref[...] += _stack_rows([_colsum(dg)], d)

    return _call(
        body, name="ffn_up_bwd", grid=(t // tm,),
        in_specs=[_rows(tm, n), _next_halo(tm, n, t), _rows(tm, n), _rows(tm, d), _rows(tm, d), _resident((3, n)),
                  _resident((n, d)), _resident((1, d))],
        out_specs=[_rows(tm, n), _rows(tm, d), _acc_spec(d), _acc_spec(n)],
        out_shape=[jax.ShapeDtypeStruct((t, n), BF16), jax.ShapeDtypeStruct((t, d), F32),
                   jax.ShapeDtypeStruct((SUBLANES, d), F32), jax.ShapeDtypeStruct((SUBLANES, n), F32)],
        args=(dup, dup, up_pre, x1, dx2, conv_w, w_u, g2), semantics=("arbitrary",), carry=carry)


def _tn_matmul(a, b, name):
    t, mdim = a.shape
    n = b.shape[1]
    tk = min(1024, t)
    tmm = _pick_tile(mdim, 1536)
    tn = _pick_tile(n, 1024)

    def body(a_ref, b_ref, o_ref, acc_ref):
        k = pl.program_id(2)

        @pl.when(k == 0)
        def _():
            acc_ref[...] = jnp.zeros_like(acc_ref)

        acc_ref[...] += _dot_tn(a_ref[...].astype(BF16), b_ref[...].astype(BF16))

        @pl.when(k == t // tk - 1)
        def _():
            o_ref[...] = acc_ref[...].astype(BF16)

    return pl.pallas_call(
        body, name=name, grid=(mdim // tmm, n // tn, t // tk),
        in_specs=[pl.BlockSpec((tk, tmm), lambda i, j, k: (k, i)), pl.BlockSpec((tk, tn), lambda i, j, k: (k, j))],
        out_specs=pl.BlockSpec((tmm, tn), lambda i, j, k: (i, j)),
        out_shape=jax.ShapeDtypeStruct((mdim, n), BF16),
        scratch_shapes=[pltpu.VMEM((tmm, tn), F32)],
        compiler_params=_params("parallel", "parallel", "arbitrary"),
    )(a, b)


def _mixer_bwd(dx1, gates, yap, ybp, os, lses, b_gate, w_o, w_pa, w_pb):
    t, d = dx1.shape
    cw = w_pa.shape[1]
    tm = min(ROWS_MATMUL, t)

    def body(dx1_ref, gates_ref, yap_ref, ybp_ref, o0_ref, o1_ref, o2_ref, l0_ref, l1_ref, l2_ref, bg_ref, wo_ref,
             wpa_ref, wpb_ref, dgates_ref, dyap_ref, dybp_ref, dya_ref, do0_ref, do1_ref, do2_ref, dl0_ref, dl1_ref,
             dl2_ref, acc_ref):
        m = pl.program_id(0)

        @pl.when(m == 0)
        def _():
            acc_ref[...] = jnp.zeros_like(acc_ref)

        dmg = _dot_nt(dx1_ref[...].astype(BF16), wo_ref[...])
        sa = _sigmoid(gates_ref[:, 0:d].astype(F32) + bg_ref[0:1, :])
        sb = _sigmoid(gates_ref[:, d:2 * d].astype(F32) + bg_ref[1:2, :])
        dyap = (dmg * sa).astype(BF16)
        dybp = (dmg * sb).astype(BF16)
        dga = dmg * yap_ref[...].astype(F32) * sa * (1.0 - sa)
        dgb = dmg * ybp_ref[...].astype(F32) * sb * (1.0 - sb)
        dyap_ref[...] = dyap
        dybp_ref[...] = dybp
        dgates_ref[:, 0:d] = dga.astype(BF16)
        dgates_ref[:, d:2 * d] = dgb.astype(BF16)
        acc_ref[...] += _stack_rows([_colsum(dga), _colsum(dgb)], d)
        dya_ref[...] = _dot(dyap, wpa_ref[...]).astype(BF16)
        dyb = _dot(dybp, wpb_ref[...])

        ri = lax.broadcasted_iota(jnp.int32, (GROUP_W, GROUP_W), 0) // HEAD_DIM
        ci = lax.broadcasted_iota(jnp.int32, (GROUP_W, GROUP_W), 1) // HEAD_DIM
        same_head = (ri == ci).astype(BF16)
        alphas = _group_softmax([_load_streams(r, dil, tm) for r, dil in zip((l0_ref, l1_ref, l2_ref), DILATIONS)])
        prod = jnp.zeros((tm, GROUP_W), F32)
        for i, (o_ref, do_ref, dil) in enumerate(zip((o0_ref, o1_ref, o2_ref), (do0_ref, do1_ref, do2_ref), DILATIONS)):
            dov = alphas[i] * dyb[:, i * GROUP_W:(i + 1) * GROUP_W]
            _store_streams(do_ref, dil, tm, dov.astype(BF16))
            prod = prod + dov * _load_streams(o_ref, dil, tm).astype(F32)
        hi = prod.astype(BF16)
        lo = (prod - hi.astype(F32)).astype(BF16)
        dtot = _dot(hi, same_head) + _dot(lo, same_head)
        for alpha, dl_ref, dil in zip(alphas, (dl0_ref, dl1_ref, dl2_ref), DILATIONS):
            _store_streams(dl_ref, dil, tm, alpha * dtot)

    streams = [_stream_spec(dil, tm, GROUP_W) for dil in DILATIONS]
    res = _call(
        body, name="mixer_bwd", grid=(t // tm,),
        in_specs=[_rows(tm, d), _rows(tm, 2 * d), _rows(tm, d), _rows(tm, d)] + streams * 2
        + [_resident((2, d)), _resident((d, d)), _resident((d, cw)), _resident((d, ATTN_W))],
        out_specs=[_rows(tm, 2 * d), _rows(tm, d), _rows(tm, d), _rows(tm, cw)] + streams * 2 + [_acc_spec(d)],
        out_shape=[jax.ShapeDtypeStruct((t, 2 * d), BF16), jax.ShapeDtypeStruct((t, d), BF16),
                   jax.ShapeDtypeStruct((t, d), BF16), jax.ShapeDtypeStruct((t, cw), BF16)]
        + [jax.ShapeDtypeStruct((dil, t // dil, GROUP_W), BF16) for dil in DILATIONS]
        + [jax.ShapeDtypeStruct((dil, t // dil, GROUP_W), F32) for dil in DILATIONS]
        + [jax.ShapeDtypeStruct((SUBLANES, d), F32)],
        args=(dx1, gates, yap, ybp, *[_stream_view(a, dil) for a, dil in zip(os, DILATIONS)],
              *[_stream_view(a, dil) for a, dil in zip(lses, DILATIONS)], b_gate, w_o, w_pa, w_pb),
        semantics=("arbitrary",))
    dgates, dyap, dybp, dya = res[:4]
    dos = [a.reshape(t, GROUP_W) for a in res[4:7]]
    dls = [a.reshape(t, GROUP_W) for a in res[7:10]]
    return dgates, dyap, dybp, dya, dos, dls, res[10]


def _attn_bwd(s, do, lse, dl, dil, carry=None):
    t = s.shape[0] * s.shape[1]
    nb = t // QBLK
    per_stream = nb // dil

    def body(q_ref, qn_ref, kc_ref, kp_ref, vc_ref, vp_ref, do_ref, don_ref, lse_ref, lsen_ref, dl_ref, dln_ref,
             ds_ref):
        b = pl.program_id(0)
        lane, heads = _head_masks()
        first_has_prev = lax.rem(STEP_BLOCKS * b, per_stream) != 0
        last_has_next = lax.rem(STEP_BLOCKS * (b + 1), per_stream) != 0

        def cols(v):
            return jnp.concatenate([jnp.sum(jnp.where(lane == h * HEAD_DIM, v, 0.0), axis=1, keepdims=True)
                                    for h in range(HEADS_PER_GROUP)], axis=0)

        def pair(qs, dos, k, v, valid, lse_c, dl_c):
            s = jnp.where(valid, _dot_nt(qs, k) * ATTN_SCALE, NEG_INF)
            p = jnp.exp(s - lse_c)
            ds = p * (_dot_nt(dos, v) - dl_c)
            return p.astype(BF16), ds.astype(BF16)

        for j in range(STEP_BLOCKS):
            rows, hi = slice(j * QBLK, (j + 1) * QBLK), slice((j + 1) * QBLK, (j + 2) * QBLK)
            q, do, lse, dl = q_ref[rows, :], do_ref[rows, :], lse_ref[rows, :], dl_ref[rows, :]
            kc, vc = kc_ref[rows, :], vc_ref[rows, :]
            if j == 0:
                k2 = jnp.concatenate([kp_ref[...], kc], axis=0)
                v2 = jnp.concatenate([vp_ref[...], vc], axis=0)
                mask = _band_mask(first_has_prev)
            else:
                both = slice((j - 1) * QBLK, (j + 1) * QBLK)
                k2, v2, mask = kc_ref[both, :], vc_ref[both, :], _band_mask(True)
            if j < STEP_BLOCKS - 1:
                qn, don, lsen, dln = q_ref[hi, :], do_ref[hi, :], lse_ref[hi, :], dl_ref[hi, :]
                mask_n = _next_mask(True)
            else:
                qn, don, lsen, dln = qn_ref[...], don_ref[...], lsen_ref[...], dln_ref[...]
                mask_n = _next_mask(last_has_next)
            qs, qns = _stack_heads(q, heads), _stack_heads(qn, heads)
            dos, dons = _stack_heads(do, heads), _stack_heads(don, heads)
            p_q, ds_q = pair(qs, dos, k2, v2, mask, cols(lse), cols(dl))
            p_n, ds_n = pair(qns, dons, kc, vc, mask_n, cols(lsen), cols(dln))
            dq = _merge_heads(_dot(ds_q, k2), heads)
            dk = _dot_tn(jnp.concatenate([ds_q[:, QBLK:], ds_n], axis=0), jnp.concatenate([qs, qns], axis=0))
            dv = _dot_tn(jnp.concatenate([p_q[:, QBLK:], p_n], axis=0), jnp.concatenate([dos, dons], axis=0))
            ds_ref[rows, 0:GROUP_W] = (dq * ATTN_SCALE).astype(BF16)
            ds_ref[rows, GROUP_W:2 * GROUP_W] = (dk * ATTN_SCALE).astype(BF16)
            ds_ref[rows, 2 * GROUP_W:3 * GROUP_W] = dv.astype(BF16)

    sv = s.reshape(t, 3 * GROUP_W)
    cur, nxt = _pair_block(0), _edge_block(0, STEP_BLOCKS, nb)
    return _call(
        body, name=f"attn_bwd_d{dil}", grid=(nb // STEP_BLOCKS,),
        in_specs=[cur, nxt, _pair_block(1), _edge_block(1, -1, nb), _pair_block(2), _edge_block(2, -1, nb),
                  cur, nxt, cur, nxt, cur, nxt],
        out_specs=[pl.BlockSpec((STEP_BLOCKS * QBLK, 3 * GROUP_W), lambda b: (b, 0))],
        out_shape=[jax.ShapeDtypeStruct((t, 3 * GROUP_W), BF16)],
        args=(sv, sv, sv, sv, sv, sv, do, do, lse, lse, dl, dl), semantics=("parallel",), carry=carry)


def _conv_mixer_bwd(abcv, dya, conv_w, conv_b):
    t = abcv.shape[0]
    cw = conv_w.shape[1]
    tm = min(256, t)
    last = t // tm - 1

    def body(a_ref, ap_ref, an_ref, dya_ref, dyan_ref, cw_ref, cb_ref, d_ref, acc_ref):
        m = pl.program_id(0)

        @pl.when(m == 0)
        def _():
            acc_ref[...] = jnp.zeros_like(acc_ref)

        ab = a_ref[:, 0:cw].astype(F32)
        ac = a_ref[:, cw:2 * cw].astype(F32)
        av = a_ref[:, 2 * cw:3 * cw].astype(F32)
        u = ac * av
        hu = ap_ref[:, cw:2 * cw].astype(F32) * ap_ref[:, 2 * cw:3 * cw].astype(F32)
        hu = jnp.where(m > 0, hu, 0.0)
        u1 = _shift_down(u, hu, 1)
        u2 = _shift_down(u, hu, 2)
        cv = cw_ref[0:1, :] * u2 + cw_ref[1:2, :] * u1 + cw_ref[2:3, :] * u + cb_ref[...]
        dya_v = dya_ref[...].astype(F32)
        dcv = dya_v * ab
        ndcv = jnp.where(m < last, dyan_ref[...].astype(F32) * an_ref[:, 0:cw].astype(F32), 0.0)
        du = (cw_ref[2:3, :] * dcv + cw_ref[1:2, :] * _shift_up(dcv, ndcv, 1)
              + cw_ref[0:1, :] * _shift_up(dcv, ndcv, 2))
        d_ref[:, 0:cw] = (dya_v * cv).astype(BF16)
        d_ref[:, cw:2 * cw] = (du * av).astype(BF16)
        d_ref[:, 2 * cw:3 * cw] = (du * ac).astype(BF16)
        acc_ref[...] += _stack_rows([_colsum(dcv * u2), _colsum(dcv * u1), _colsum(dcv * u), _colsum(dcv)], cw)

    return pl.pallas_call(
        body, name="conv_mixer_bwd", grid=(t // tm,),
        in_specs=[_rows(tm, 3 * cw), _prev_halo(tm, 3 * cw), _next_halo(tm, 3 * cw, t), _rows(tm, cw),
                  _next_halo(tm, cw, t), _resident((3, cw)), _resident((1, cw))],
        out_specs=[_rows(tm, 3 * cw), _acc_spec(cw)],
        out_shape=[jax.ShapeDtypeStruct((t, 3 * cw), BF16), jax.ShapeDtypeStruct((SUBLANES, cw), F32)],
        compiler_params=_params("arbitrary"),
    )(abcv, abcv, abcv, dya, dya, conv_w, conv_b)


def _in_proj_bwd(x, dx1, dabcv, dss, dgates, w_in, g1, carry=None):
    t, d = x.shape
    qkv0 = dabcv.shape[1]
    n = w_in.shape[0]
    tm = min(ROWS_MATMUL, t)

    def body(x_ref, dx1_ref, da_ref, ds0_ref, ds1_ref, ds2_ref, dg_ref, w_ref, g_ref, dx_ref, acc_ref):
        m = pl.program_id(0)

        @pl.when(m == 0)
        def _():
            acc_ref[...] = jnp.zeros_like(acc_ref)

        dss_tok = [_load_streams(ds_ref, dil, tm) for ds_ref, dil in zip((ds0_ref, ds1_ref, ds2_ref), DILATIONS)]
        dqkv = jnp.concatenate([ds[:, j * GROUP_W:(j + 1) * GROUP_W] for j in range(3) for ds in dss_tok], axis=1)
        dh = (_dot(da_ref[...], w_ref[0:qkv0, :]) + _dot(dqkv, w_ref[qkv0:qkv0 + 3 * ATTN_W, :])
              + _dot(dg_ref[...], w_ref[qkv0 + 3 * ATTN_W:n, :]))
        dx, dg = _rms_bwd(x_ref[...], g_ref[...], dh)
        dx_ref[...] = dx1_ref[...] + dx
        acc_ref[...] += _stack_rows([_colsum(dg)], d)

    return _call(
        body, name="in_proj_bwd", grid=(t // tm,),
        in_specs=[_rows(tm, d), _rows(tm, d), _rows(tm, qkv0)]
        + [_stream_spec(dil, tm, 3 * GROUP_W) for dil in DILATIONS]
        + [_rows(tm, 2 * d), _resident((n, d)), _resident((1, d))],
        out_specs=[_rows(tm, d), _acc_spec(d)],
        out_shape=[jax.ShapeDtypeStruct((t, d), F32), jax.ShapeDtypeStruct((SUBLANES, d), F32)],
        args=(x, dx1, dabcv, *[_stream_view(a, dil) for a, dil in zip(dss, DILATIONS)], dgates, w_in, g1),
        semantics=("arbitrary",), carry=carry)


def _dw_in_qkv(ds, h, dil):
    t, d = h.shape
    tk = min(1024, t)
    sub = min(256, t)
    width = 3 * GROUP_W

    def body(ds_ref, h_ref, o_ref, acc_ref):
        k = pl.program_id(0)

        @pl.when(k == 0)
        def _():
            acc_ref[...] = jnp.zeros_like(acc_ref)

        upd = None
        for i in range(tk // sub):
            rows = ds_ref[:, i * (sub // dil):(i + 1) * (sub // dil), :].reshape(sub, width)
            if dil > 1:
                rows = _permute_rows(_perm(dil, sub, inverse=True), rows)
            term = _dot_tn(rows, h_ref[i * sub:(i + 1) * sub, :])
            upd = term if upd is None else upd + term
        acc_ref[...] += upd

        @pl.when(k == t // tk - 1)
        def _():
            o_ref[...] = acc_ref[...].astype(BF16)

    return pl.pallas_call(
        body, name=f"dw_in_qkv_d{dil}", grid=(t // tk,),
        in_specs=[_stream_spec(dil, tk, width), _rows(tk, d)],
        out_specs=pl.BlockSpec((width, d), lambda k: (0, 0)),
        out_shape=jax.ShapeDtypeStruct((width, d), BF16),
        scratch_shapes=[pltpu.VMEM((width, d), F32)],
        compiler_params=_params("arbitrary"),
    )(_stream_view(ds, dil), h)


def _local_step(x, target, p, late):
    cw = p["conv_a_w"].shape[1]
    (h, abcv, gates, *ss), (g_up,) = _in_proj(x, p["norm_mix_g"], p["w_in"], cw,
                                              carry=_Exchange("gather", [late["w_up"]]))
    w_up = _full_from_gathered(g_up)
    (o0, lse0), g_proj = _attn_fwd(ss[0], DILATIONS[0],
                                   carry=_Exchange("gather", [late["w_proj_a"], late["w_proj_b"]]))
    (o1, lse1), (g_out,) = _attn_fwd(ss[1], DILATIONS[1], carry=_Exchange("gather", [late["w_out"]]))
    o2, lse2 = _attn_fwd(ss[2], DILATIONS[2])
    w_pa, w_pb, w_out = [_full_from_gathered(g) for g in (*g_proj, g_out)]
    os, lses = (o0, o1, o2), (lse0, lse1, lse2)
    (x1, ya, yb, yap, ybp, merged), (g_down,) = _mixer_out(
        x, abcv, gates, os, lses, p["conv_a_w"], p["conv_a_b"], p["b_gate"], w_pa, w_pb, w_out,
        carry=_Exchange("gather", [late["w_down"]]))
    w_down = _full_from_gathered(g_down)
    h2, up_pre, act, conv, dx2, dx2i, acc_gf, loss = _ffn_fwd(x1, target, p["norm_ffn_g"], w_up, p["ffn_conv_w"],
                                                              p["ffn_conv_b"], w_down, p["final_norm_g"])

    parts, got = {}, {}
    dup, acc_fb = _ffn_act_bwd(dx2i, conv, w_down)
    parts["w_down"] = _by_destination(_tn_matmul(act, dx2i, "dw_down"))
    (dpre, dx1, acc_g2, acc_fw), (got["w_down"],) = _ffn_up_bwd(dup, up_pre, x1, dx2, p["ffn_conv_w"], w_up,
                                                                p["norm_ffn_g"],
                                                                carry=_Exchange("scatter", [parts["w_down"]]))
    parts["w_up"] = _by_destination(_tn_matmul(dpre, h2, "dw_up"))
    dgates, dyap, dybp, dya, dos, dls, acc_bg = _mixer_bwd(dx1, gates, yap, ybp, os, lses, p["b_gate"], w_out,
                                                           w_pa, w_pb)
    parts["w_out"] = _by_destination(_tn_matmul(merged, dx1, "dw_out"))
    parts["w_proj_a"] = _by_destination(_tn_matmul(dyap, ya, "dw_proj_a"))
    parts["w_proj_b"] = _by_destination(_tn_matmul(dybp, yb, "dw_proj_b"))
    minor = ("w_out", "w_proj_a", "w_proj_b")
    half = parts["w_up"].shape[1] // 2
    (ds0,), received = _attn_bwd(ss[0], dos[0], lses[0], dls[0], DILATIONS[0],
                                 carry=_Exchange("scatter", [parts[n] for n in minor]))
    got.update(zip(minor, received))
    (ds1,), first_half = _attn_bwd(ss[1], dos[1], lses[1], dls[1], DILATIONS[1],
                                   carry=_Exchange("scatter", [parts["w_up"]], rows=(0, half)))
    (ds2,), (got["w_up"],) = _attn_bwd(ss[2], dos[2], lses[2], dls[2], DILATIONS[2],
                                       carry=_Exchange("scatter", [parts["w_up"]], rows=(half, half),
                                                       into=first_half))
    dss = [ds0, ds1, ds2]
    dabcv, acc_ca = _conv_mixer_bwd(abcv, dya, p["conv_a_w"], p["conv_a_b"])
    dw_s = [_dw_in_qkv(ds, h, dil) for ds, dil in zip(dss, DILATIONS)]
    dw_qkv = [w[j * GROUP_W:(j + 1) * GROUP_W] for j in range(3) for w in dw_s]
    g_w_in = jnp.concatenate([_tn_matmul(dabcv, h, "dw_in_a"), *dw_qkv, _tn_matmul(dgates, h, "dw_in_g")], axis=0)
    parts["w_in"] = _by_destination(g_w_in)
    (dx, acc_g1), (got["w_in"],) = _in_proj_bwd(x, dx1, dabcv, dss, dgates, p["w_in"], p["norm_mix_g"],
                                                carry=_Exchange("scatter", [parts["w_in"]]))
    small = dict(norm_mix_g=acc_g1[0:1], b_gate=acc_bg[0:2], conv_a_w=acc_ca[0:3], conv_a_b=acc_ca[3:4],
                 norm_ffn_g=acc_g2[0:1], ffn_conv_w=acc_fw[0:3], ffn_conv_b=acc_fb[0:1], final_norm_g=acc_gf[0:1])
    return loss[0, 0], dx, parts, got, small


def _all_gather(shards):
    n = len(shards)

    def body(*refs):
        ins, outs = refs[:n], refs[n:2 * n]
        send_sems, recv_sems, local_sems = refs[2 * n:]
        x, y, c = _mesh_pos()
        me, sibling = (x, y, c), (x, y, 1 - c)
        chips = [(1 - x, y), (x, 1 - y), (1 - x, 1 - y)]

        def copy(i, k, block, to, src=None):
            rows = outs[i].at[_dev_index(*block)]
            return pltpu.make_async_remote_copy(
                src_ref=rows if src is None else src, dst_ref=rows, send_sem=send_sems.at[i, k],
                recv_sem=recv_sems.at[i, k], device_id=to, device_id_type=MESH)

        mine, first, passed = [], [], []
        for i in range(n):
            cp = pltpu.make_async_copy(ins[i], outs[i].at[_dev_index(*me)], local_sems.at[i])
            cp.start()
            mine.append(cp)
            first.append(copy(i, 0, me, sibling, src=ins[i]))
            first += [copy(i, 1 + j, me, (*chip, c), src=ins[i]) for j, chip in enumerate(chips)]
        for cp in first:
            cp.start()
        for i in range(n):
            for j, chip in enumerate(chips):
                copy(i, 1 + j, (*chip, c), me).wait_recv()
                fw = copy(i, 4 + j, (*chip, c), sibling)
                fw.start()
                passed.append(fw)
        for i in range(n):
            copy(i, 0, sibling, me).wait_recv()
            for j, chip in enumerate(chips):
                copy(i, 4 + j, (*chip, 1 - c), me).wait_recv()
        for cp in first + passed:
            cp.wait_send()
        for cp in mine:
            cp.wait()

    return pl.pallas_call(
        body, name="all_gather_weights",
        in_specs=[ANY] * n, out_specs=[ANY] * n,
        out_shape=[jax.ShapeDtypeStruct((N_DEV,) + s.shape, s.dtype) for s in shards],
        scratch_shapes=[pltpu.SemaphoreType.DMA((n, 7)), pltpu.SemaphoreType.DMA((n, 7)),
                        pltpu.SemaphoreType.DMA((n,))],
    )(*shards)


def _all_reduce_small(v):
    r = v.shape[0]

    def body(v_ref, o_ref, gath, send_sems, recv_sems):
        x, y, c = _mesh_pos()
        me = _dev_index(x, y, c)
        gath[me] = v_ref[...]
        flips = [(kx, ky, kc) for kx in (0, 1) for ky in (0, 1) for kc in (0, 1)][1:]
        copies = []
        for k, (kx, ky, kc) in enumerate(flips):
            px = 1 - x if kx else x
            py = 1 - y if ky else y
            pc = 1 - c if kc else c
            cp = pltpu.make_async_remote_copy(
                src_ref=v_ref, dst_ref=gath.at[me], send_sem=send_sems.at[k], recv_sem=recv_sems.at[k],
                device_id=(px, py, pc), device_id_type=MESH)
            cp.start()
            copies.append((cp, _dev_index(px, py, pc)))
        for k, (cp, peer) in enumerate(copies):
            pltpu.make_async_remote_copy(
                src_ref=v_ref, dst_ref=gath.at[peer], send_sem=send_sems.at[k], recv_sem=recv_sems.at[k],
                device_id=(x, y, c), device_id_type=MESH).wait_recv()
        for cp, _ in copies:
            cp.wait_send()
        total = gath[0]
        for j in range(1, N_DEV):
            total = total + gath[j]
        o_ref[...] = total

    return pl.pallas_call(
        body, name="all_reduce_small",
        in_specs=[pl.BlockSpec(memory_space=pltpu.VMEM)], out_specs=pl.BlockSpec(memory_space=pltpu.VMEM),
        out_shape=jax.ShapeDtypeStruct((r, LANES), F32),
        scratch_shapes=[pltpu.VMEM((N_DEV, r, LANES), F32), pltpu.SemaphoreType.DMA((7,)),
                        pltpu.SemaphoreType.DMA((7,))],
    )(v)


def _adamw_math(w, g, m, v):
    m2 = ADAM_B1 * m + (1.0 - ADAM_B1) * g
    v2 = ADAM_B2 * v + (1.0 - ADAM_B2) * (g * g)
    m_hat = m2 / (1.0 - ADAM_B1 ** ADAM_STEP)
    v_hat = v2 / (1.0 - ADAM_B2 ** ADAM_STEP)
    delta = -ADAM_LR * (m_hat / (jnp.sqrt(v_hat) + ADAM_EPS) + ADAM_WD * w)
    return delta, m2, v2


def _adamw_big(w, m, v, part, got, me):
    r, c = part.shape[1:]
    flip = w.shape != (r, c)
    tr = r if flip else max(t for t in range(HALO, min(r, 512) + 1, HALO) if r % t == 0)

    def body(me_ref, w_ref, m_ref, v_ref, own_ref, *rest):
        del me_ref
        got_refs, (g_out, d_out, m_out, v_out) = rest[:N_DEV - 1], rest[N_DEV - 1:]
        g = own_ref[...].astype(F32)
        for ref in got_refs:
            g = g + ref[...].astype(F32)
        if flip:
            g = g.T
        delta, m2, v2 = _adamw_math(w_ref[...], g, m_ref[...], v_ref[...])
        g_out[...] = g
        d_out[...] = delta
        m_out[...] = m2
        v_out[...] = v2

    def peer_block(k):
        return pl.BlockSpec((None, tr, c), lambda i, me_ref: (jnp.bitwise_xor(me_ref[0], k), i, 0))

    plain = pl.BlockSpec(w.shape if flip else (tr, c), lambda i, me_ref: (i, 0))
    out = jax.ShapeDtypeStruct(w.shape, F32)
    return pl.pallas_call(
        body, name="adamw_big",
        grid_spec=pltpu.PrefetchScalarGridSpec(
            num_scalar_prefetch=1, grid=(r // tr,),
            in_specs=[plain, plain, plain] + [peer_block(k) for k in range(N_DEV)],
            out_specs=[plain] * 4),
        out_shape=[out] * 4,
        compiler_params=_params("parallel"),
    )(me, w, m, v, part, *([got] * (N_DEV - 1)))


def _adamw_small(ws, gs, ms, vs):
    n = len(ws)

    def body(*refs):
        ins, outs = refs[:4 * n], refs[4 * n:]
        for i in range(n):
            delta, m2, v2 = _adamw_math(ins[i][...], ins[n + i][...], ins[2 * n + i][...], ins[3 * n + i][...])
            outs[i][...] = delta
            outs[n + i][...] = m2
            outs[2 * n + i][...] = v2

    out = [jax.ShapeDtypeStruct(w.shape, F32) for w in ws]
    res = pl.pallas_call(body, name="adamw_small", out_shape=out * 3)(*ws, *gs, *ms, *vs)
    return res[:n], res[n:2 * n], res[2 * n:]


BIG = ("w_in", "w_proj_a", "w_proj_b", "w_out", "w_up", "w_down")
LATE = ("w_proj_a", "w_proj_b", "w_out", "w_up", "w_down")
COLUMN_SHARDED = ("w_in", "w_proj_a", "w_proj_b", "w_up")
WIDE_COLUMN_SHARDED = ("w_in", "w_up")
SMALL = ("norm_mix_g", "b_gate", "conv_a_w", "conv_a_b", "norm_ffn_g", "ffn_conv_w", "ffn_conv_b", "final_norm_g")
SMALL_SHARDED = ("b_gate", "conv_a_w", "ffn_conv_w")
WEIGHTS = ("norm_mix_g", "w_in", "b_gate", "conv_a_w", "conv_a_b", "w_proj_a", "w_proj_b", "w_out", "norm_ffn_g",
           "w_up", "ffn_conv_w", "ffn_conv_b", "w_down", "final_norm_g")


def _pack(vectors, rows):
    flat = jnp.concatenate([v.reshape(-1) for v in vectors])
    return jnp.pad(flat, (0, rows * LANES - flat.shape[0])).reshape(rows, LANES)


def _packed_rows(count):
    rows = -(-count // LANES)
    return -(-rows // SUBLANES) * SUBLANES


def _unpack(packed, shapes):
    flat = packed.reshape(-1)
    out, lo = [], 0
    for s in shapes:
        size = 1
        for dim in s:
            size *= dim
        out.append(flat[lo:lo + size].reshape(s))
        lo += size
    return out


def _full_from_gathered(gathered):
    _, r, c = gathered.shape
    return gathered.reshape(N_DEV * r, c)


def _by_destination(grad):
    rr, cc = grad.shape
    return grad.reshape(N_DEV, rr // N_DEV, cc)


def _block2d(name, a):
    a = a.reshape(a.shape[-2:])
    return a.T if name in WIDE_COLUMN_SHARDED else a


def kernel(x, norm_mix_g, w_in, b_gate, conv_a_w, conv_a_b, w_proj_a, w_proj_b, w_out, norm_ffn_g, w_up, ffn_conv_w, ffn_conv_b, w_down, final_norm_g, loss_target, m_norm_mix_g, m_w_in, m_b_gate, m_conv_a_w, m_conv_a_b, m_w_proj_a, m_w_proj_b, m_w_out, m_norm_ffn_g, m_w_up, m_ffn_conv_w, m_ffn_conv_b, m_w_down, m_final_norm_g, v_norm_mix_g, v_w_in, v_b_gate, v_conv_a_w, v_conv_a_b, v_w_proj_a, v_w_proj_b, v_w_out, v_norm_ffn_g, v_w_up, v_ffn_conv_w, v_ffn_conv_b, v_w_down, v_final_norm_g):
    given = dict(locals())
    shard = {n: given[n] for n in WEIGHTS}
    mom_m = {n: given["m_" + n] for n in WEIGHTS}
    mom_v = {n: given["v_" + n] for n in WEIGHTS}
    xi, yi, ci = _mesh_pos()
    me = _dev_index(xi, yi, ci)
    me1 = me.astype(jnp.int32).reshape(1)

    big2d = {n: _block2d(n, shard[n]) for n in BIG}
    small_shapes = [shard[n].shape[1:] for n in SMALL_SHARDED]
    n_small = sum(s[0] * s[1] for s in small_shapes)
    packed_small = _pack([shard[n] for n in SMALL_SHARDED], _packed_rows(n_small))
    gathered = _all_gather([big2d["w_in"].astype(BF16), packed_small])
    p = {"w_in": _full_from_gathered(gathered[0])}
    flat_small = gathered[-1].reshape(N_DEV, -1)
    lo = 0
    for n, (rows, width) in zip(SMALL_SHARDED, small_shapes):
        blocks = flat_small[:, lo:lo + rows * width].reshape(N_DEV, rows, width)
        p[n] = blocks.transpose(1, 0, 2).reshape(rows, N_DEV * width)
        lo += rows * width
    p["norm_mix_g"], p["norm_ffn_g"] = shard["norm_mix_g"], shard["norm_ffn_g"]
    p["conv_a_b"], p["ffn_conv_b"] = shard["conv_a_b"], shard["ffn_conv_b"]
    p["final_norm_g"] = shard["final_norm_g"].reshape(1, -1)
    late = {n: (big2d[n].T if n in ("w_proj_a", "w_proj_b") else big2d[n]).astype(BF16) for n in LATE}

    loss_part, dx, parts, got, g_small = _local_step(x[0], loss_target[0], p, late)

    results = {}
    for n in BIG:
        outs = _adamw_big(big2d[n], _block2d(n, mom_m[n]), _block2d(n, mom_v[n]), parts[n], got[n], me1)
        results[n] = [_block2d(n, o).reshape(shard[n].shape) for o in outs]

    small_full_shapes = [g_small[n].shape for n in SMALL]
    n_vec = sum(s[0] * s[1] for s in small_full_shapes) + 1
    packed = _pack([g_small[n] for n in SMALL] + [loss_part.reshape(1)], _packed_rows(n_vec))
    reduced = _all_reduce_small(packed)
    *g_full, loss_vec = _unpack(reduced, small_full_shapes + [(1,)])
    loss = loss_vec[0]
    own_g = []
    for n, g in zip(SMALL, g_full):
        if n in SMALL_SHARDED:
            width = shard[n].shape[-1]
            g = lax.dynamic_slice_in_dim(g, me * width, width, axis=1)
        own_g.append(g.reshape(shard[n].shape))
    def rows2d(a):
        return a.reshape(-1, a.shape[-1])

    deltas, new_ms, new_vs = _adamw_small([rows2d(shard[n]) for n in SMALL], [rows2d(g) for g in own_g],
                                          [rows2d(mom_m[n]) for n in SMALL], [rows2d(mom_v[n]) for n in SMALL])
    for i, n in enumerate(SMALL):
        results[n] = [own_g[i]] + [a.reshape(shard[n].shape) for a in (deltas[i], new_ms[i], new_vs[i])]

    grad_x = dx.reshape(x.shape)
    return (loss, grad_x, *[results[n][0] for n in WEIGHTS], *[results[n][1] for n in WEIGHTS],
            *[results[n][2] for n in WEIGHTS], *[results[n][3] for n in WEIGHTS])
```

```python
import functools

import jax
import jax.numpy as jnp
from jax import lax
from jax.experimental import pallas as pl
from jax.experimental.pallas import tpu as pltpu

F32 = jnp.float32
BF16 = jnp.bfloat16
MESH = pl.DeviceIdType.MESH

N_DEV = 8
RMS_EPS = 1e-6
NEG_INF = -1e30
N_GROUPS = 3
DILATIONS = (1, 4, 16)
HEADS_PER_GROUP = 4
HEAD_DIM = 64
GROUP_W = HEADS_PER_GROUP * HEAD_DIM
ATTN_W = N_GROUPS * GROUP_W
QBLK = 128
STEP_BLOCKS = 4
ATTN_SCALE = HEAD_DIM ** -0.5

ADAM_LR = 0.001
ADAM_B1 = 0.9
ADAM_B2 = 0.999
ADAM_EPS = 1e-08
ADAM_WD = 0.01
ADAM_STEP = 10

PERM_TOKENS = 256
ROWS_MATMUL = 512
HALO = 16
LANES = 128
SUBLANES = 8
VMEM_LIMIT_BYTES = 56 * 1024 * 1024


def _params(*sem):
    return pltpu.CompilerParams(dimension_semantics=sem, vmem_limit_bytes=VMEM_LIMIT_BYTES)


def _pick_tile(n, cap):
    if n <= cap:
        return n
    best = None
    for t in range(LANES, cap + 1, LANES):
        if n % t == 0:
            best = t
    assert best is not None, (n, cap)
    return best


def _rows(tm, c, j=0):
    return pl.BlockSpec((tm, c), lambda m: (m, j))


def _prev_halo(tm, c):
    return pl.BlockSpec((HALO, c), lambda m: (jnp.maximum(m * (tm // HALO) - 1, 0), 0))


def _next_halo(tm, c, t_total):
    last = t_total // HALO - 1
    return pl.BlockSpec((HALO, c), lambda m: (jnp.minimum((m + 1) * (tm // HALO), last), 0))


def _resident(shape):
    nd = len(shape)
    return pl.BlockSpec(shape, lambda *_: (0,) * nd, pipeline_mode=pl.Buffered(1))


def _acc_spec(c):
    return pl.BlockSpec((SUBLANES, c), lambda *_: (0, 0))


def _shift_down(u, halo, k):
    edge = jnp.concatenate([halo[HALO - SUBLANES:], u[:SUBLANES]], axis=0)
    head = pltpu.roll(edge, k, 0)[SUBLANES:]
    return jnp.concatenate([head, pltpu.roll(u, k, 0)[SUBLANES:]], axis=0)


def _shift_up(u, halo, k):
    n = u.shape[0]
    edge = jnp.concatenate([u[n - SUBLANES:], halo[:SUBLANES]], axis=0)
    tail = pltpu.roll(edge, 2 * SUBLANES - k, 0)[:SUBLANES]
    return jnp.concatenate([pltpu.roll(u, n - k, 0)[:n - SUBLANES], tail], axis=0)


def _interleave(tm, inverse=False):
    return _perm(tm // SUBLANES, tm, inverse)


def _edge_groups(u, halo, k, from_end):
    n = u.shape[0]
    sub = lax.broadcasted_iota(jnp.int32, (SUBLANES, u.shape[1]), 0)
    out = []
    for j in range(2 - k, 2):
        lo = n - HALO + j * SUBLANES if from_end else j * SUBLANES
        own, other = u[lo:lo + SUBLANES], halo[j * SUBLANES:(j + 1) * SUBLANES]
        if from_end:
            out.append(pltpu.roll(jnp.where(sub == SUBLANES - 1, other, own), 1, 0))
        else:
            out.append(pltpu.roll(jnp.where(sub == 0, other, own), SUBLANES - 1, 0))
    return out


def _shift_down_il(u, halo, k):
    return jnp.concatenate(_edge_groups(u, halo, k, True) + [u[:u.shape[0] - k * SUBLANES]], axis=0)


def _shift_up_il(u, halo, k):
    if k == 1:
        edge = _edge_groups(u, halo, 2, False)[:1]
    else:
        edge = _edge_groups(u, halo, 2, False)
    return jnp.concatenate([u[k * SUBLANES:]] + edge, axis=0)


def _stack_rows(rows, c):
    idx = lax.broadcasted_iota(jnp.int32, (SUBLANES, c), 0)
    out = jnp.zeros((SUBLANES, c), F32)
    for i, r in enumerate(rows):
        out = out + jnp.where(idx == i, r, 0.0)
    return out


def _colsum(v):
    return jnp.sum(v, axis=0, keepdims=True)


def _sigmoid(v):
    return 0.5 * jnp.tanh(0.5 * v) + 0.5


def _rms_fwd(xv, g):
    r = lax.rsqrt(jnp.mean(xv * xv, axis=-1, keepdims=True) + RMS_EPS)
    return xv * r * g, r


def _rms_bwd(xv, g, dy):
    r = lax.rsqrt(jnp.mean(xv * xv, axis=-1, keepdims=True) + RMS_EPS)
    xn = xv * r
    dxn = dy * g
    dx = r * (dxn - xn * jnp.mean(dxn * xn, axis=-1, keepdims=True))
    return dx, dy * xn


def _dot(a, b):
    return jnp.dot(a, b, preferred_element_type=F32)


def _dot_nt(a, b):
    return lax.dot_general(a, b, (((1,), (1,)), ((), ())), preferred_element_type=F32)


def _dot_tn(a, b):
    return lax.dot_general(a, b, (((0,), (0,)), ((), ())), preferred_element_type=F32)


def _perm(dil, n, inverse=False):
    i = lax.broadcasted_iota(jnp.int32, (n, n), 0)
    j = lax.broadcasted_iota(jnp.int32, (n, n), 1)
    if inverse:
        i, j = j, i
    per = n // dil
    return (j == (i % per) * dil + i // per).astype(BF16)


def _permute_rows(pm, v):
    if v.dtype == BF16:
        return _dot(pm, v).astype(BF16)
    hi = v.astype(BF16)
    lo = (v - hi.astype(F32)).astype(BF16)
    return _dot(pm, hi) + _dot(pm, lo)


def _stream_view(a, dil):
    t, c = a.shape
    return a.reshape(dil, t // dil, c)


def _stream_spec(dil, tm, c):
    return pl.BlockSpec((dil, tm // dil, c), lambda m: (0, m, 0))


def _load_streams(ref, dil, tm):
    c = ref.shape[-1]
    if dil == 1:
        return ref[...].reshape(tm, c)
    sub = min(PERM_TOKENS, tm)
    pm = _perm(dil, sub, inverse=True)
    parts = [_permute_rows(pm, ref[:, i * (sub // dil):(i + 1) * (sub // dil), :].reshape(sub, c))
             for i in range(tm // sub)]
    return parts[0] if len(parts) == 1 else jnp.concatenate(parts, axis=0)


def _store_streams(ref, dil, tm, v):
    if dil == 1:
        ref[...] = v.reshape(ref.shape).astype(ref.dtype)
        return
    sub = min(PERM_TOKENS, tm)
    pm = _perm(dil, sub)
    for i in range(tm // sub):
        piece = _permute_rows(pm, v[i * sub:(i + 1) * sub])
        ref[:, i * (sub // dil):(i + 1) * (sub // dil), :] = piece.reshape(dil, sub // dil, -1).astype(ref.dtype)


ANY = pl.BlockSpec(memory_space=pl.ANY)


def _mesh_pos():
    return lax.axis_index("x"), lax.axis_index("y"), lax.axis_index("c")


def _dev_index(px, py, pc):
    return 4 * px + 2 * py + pc


class _Exchange:
    def __init__(self, mode, arrays, rows=None, into=()):
        self.mode, self.arrays, self.rows, self.into = mode, list(arrays), rows, list(into)
        n = len(self.arrays)
        if mode == "gather":
            self.out_shape = [jax.ShapeDtypeStruct((N_DEV,) + a.shape, a.dtype) for a in self.arrays]
        else:
            self.out_shape = [jax.ShapeDtypeStruct(a.shape, a.dtype) for a in self.arrays]
        self.scratch = [pltpu.SemaphoreType.DMA((n, N_DEV - 1)), pltpu.SemaphoreType.DMA((n, N_DEV - 1)),
                        pltpu.SemaphoreType.DMA((n,))]

    def _peers(self):
        x, y, c = _mesh_pos()
        flips = [(kx, ky, kc) for kx in (0, 1) for ky in (0, 1) for kc in (0, 1)][1:]
        peers = [(1 - x if kx else x, 1 - y if ky else y, 1 - c if kc else c) for kx, ky, kc in flips]
        return _dev_index(x, y, c), peers

    def _copy(self, ins, outs, sems, i, k, peer, me, sending):
        src = ins[i] if self.mode == "gather" else ins[i].at[_dev_index(*peer)]
        dst = outs[i].at[me if sending else _dev_index(*peer)]
        if self.rows is not None:
            src, dst = src.at[pl.ds(*self.rows)], dst.at[pl.ds(*self.rows)]
        return pltpu.make_async_remote_copy(src_ref=src, dst_ref=dst, send_sem=sems[0].at[i, k],
                                            recv_sem=sems[1].at[i, k], device_id=peer, device_id_type=MESH)

    def _own(self, ins, outs, sems, i, me):
        return pltpu.make_async_copy(ins[i], outs[i].at[me], sems[2].at[i])

    def start(self, ins, outs, sems):
        me, peers = self._peers()
        for i in range(len(ins)):
            if self.mode == "gather":
                self._own(ins, outs, sems, i, me).start()
            for k, peer in enumerate(peers):
                self._copy(ins, outs, sems, i, k, peer, me, True).start()

    def wait(self, ins, outs, sems):
        me, peers = self._peers()
        for i in range(len(ins)):
            for k, peer in enumerate(peers):
                self._copy(ins, outs, sems, i, k, peer, me, False).wait_recv()
            for k, peer in enumerate(peers):
                self._copy(ins, outs, sems, i, k, peer, me, True).wait_send()
            if self.mode == "gather":
                self._own(ins, outs, sems, i, me).wait()


def _call(body, *, name, grid, in_specs, out_specs, out_shape, args, semantics, carry=None, scratch=()):
    if carry is None:
        return pl.pallas_call(body, name=name, grid=grid, in_specs=in_specs, out_specs=out_specs,
                              out_shape=out_shape, scratch_shapes=list(scratch),
                              compiler_params=_params(*semantics))(*args)
    n_in, n_out, n_x, n_s = len(in_specs), len(out_specs), len(carry.arrays), len(scratch)
    n_into = len(carry.into)
    all_in = n_in + n_x + n_into

    def carried(*refs):
        ins, x_ins = refs[:n_in], refs[n_in:n_in + n_x]
        outs = refs[all_in:all_in + n_out]
        x_outs = refs[all_in + n_out:all_in + n_out + n_x]
        own = refs[all_in + n_out + n_x:all_in + n_out + n_x + n_s]
        sems = refs[all_in + n_out + n_x + n_s:]
        first = functools.reduce(jnp.logical_and, [pl.program_id(a) == 0 for a in range(len(grid))])
        last = functools.reduce(jnp.logical_and, [pl.program_id(a) == grid[a] - 1 for a in range(len(grid))])

        @pl.when(first)
        def _():
            carry.start(x_ins, x_outs, sems)

        body(*ins, *outs, *own)

        @pl.when(last)
        def _():
            carry.wait(x_ins, x_outs, sems)

    res = pl.pallas_call(
        carried, name=name, grid=grid, in_specs=list(in_specs) + [ANY] * (n_x + n_into),
        out_specs=list(out_specs) + [ANY] * n_x, out_shape=list(out_shape) + carry.out_shape,
        input_output_aliases={n_in + n_x + i: n_out + i for i in range(n_into)},
        scratch_shapes=list(scratch) + carry.scratch, compiler_params=_params(*["arbitrary"] * len(grid)),
    )(*args, *carry.arrays, *carry.into)
    return list(res[:n_out]), list(res[n_out:])


def _norm_in(x, g, carry):
    t, d = x.shape
    tm = min(ROWS_MATMUL, t)

    def body(x_ref, g_ref, h_ref):
        h_ref[...] = _rms_fwd(x_ref[...], g_ref[...])[0].astype(BF16)

    return _call(body, name="norm_in", grid=(t // tm,), in_specs=[_rows(tm, d), _resident((1, d))],
                 out_specs=[_rows(tm, d)], out_shape=[jax.ShapeDtypeStruct((t, d), BF16)], args=(x, g),
                 semantics=("parallel",), carry=carry)


def _in_proj(h, wt, cw, carry=None):
    t, d = h.shape
    n = wt.shape[0]
    tm = min(ROWS_MATMUL, t)
    qkv0 = 3 * cw

    def body(h_ref, wt_ref, abcv_ref, gates_ref, *s_refs):
        h = h_ref[...]
        abcv_ref[...] = _dot_nt(h, wt_ref[0:qkv0, :]).astype(BF16)
        gates_ref[...] = _dot_nt(h, wt_ref[qkv0 + 3 * ATTN_W:n, :]).astype(BF16)
        qkv = _dot_nt(h, wt_ref[qkv0:qkv0 + 3 * ATTN_W, :]).astype(BF16)
        for gi, s_ref in enumerate(s_refs):
            cols = [qkv[:, j * ATTN_W + gi * GROUP_W:j * ATTN_W + (gi + 1) * GROUP_W] for j in range(3)]
            _store_streams(s_ref, DILATIONS[gi], tm, jnp.concatenate(cols, axis=1))

    return _call(
        body, name="in_proj", grid=(t // tm,),
        in_specs=[_rows(tm, d), _resident((n, d))],
        out_specs=[_rows(tm, qkv0), _rows(tm, 2 * d)] + [_stream_spec(dil, tm, 3 * GROUP_W) for dil in DILATIONS],
        out_shape=[jax.ShapeDtypeStruct((t, qkv0), BF16), jax.ShapeDtypeStruct((t, 2 * d), BF16)]
        + [jax.ShapeDtypeStruct((dil, t // dil, 3 * GROUP_W), BF16) for dil in DILATIONS],
        args=(h, wt), semantics=("parallel",), carry=carry)


def _head_masks():
    lane = lax.broadcasted_iota(jnp.int32, (1, GROUP_W), 1)
    return lane, [(lane // HEAD_DIM) == h for h in range(HEADS_PER_GROUP)]


def _stack_heads(v, heads):
    return jnp.concatenate([jnp.where(hm, v, jnp.zeros_like(v)) for hm in heads], axis=0)


def _merge_heads(v, heads):
    out = jnp.zeros((QBLK, GROUP_W), v.dtype)
    for h, hm in enumerate(heads):
        out = jnp.where(hm, v[h * QBLK:(h + 1) * QBLK], out)
    return out


def _pair_block(col):
    return pl.BlockSpec((STEP_BLOCKS * QBLK, GROUP_W), lambda b: (b, col))


def _edge_block(col, shift, nb):
    return pl.BlockSpec((QBLK, GROUP_W), lambda b: (jnp.clip(STEP_BLOCKS * b + shift, 0, nb - 1), col))


def _band_mask(has_prev):
    rows = HEADS_PER_GROUP * QBLK
    row = lax.broadcasted_iota(jnp.int32, (rows, 2 * QBLK), 0) & (QBLK - 1)
    col = lax.broadcasted_iota(jnp.int32, (rows, 2 * QBLK), 1)
    return ((col < QBLK) & (col >= row) & has_prev) | ((col >= QBLK) & (col - QBLK <= row))


def _next_mask(has_next):
    rows = HEADS_PER_GROUP * QBLK
    row = lax.broadcasted_iota(jnp.int32, (rows, QBLK), 0) & (QBLK - 1)
    col = lax.broadcasted_iota(jnp.int32, (rows, QBLK), 1)
    return (col >= row) & has_next


def _attn_fwd(s, dil, carry=None):
    t = s.shape[0] * s.shape[1]
    nb = t // QBLK
    per_stream = nb // dil
    assert per_stream % STEP_BLOCKS == 0

    def body(q_ref, kc_ref, kp_ref, vc_ref, vp_ref, o_ref, lse_ref):
        b = pl.program_id(0)
        _, heads = _head_masks()
        first_has_prev = lax.rem(STEP_BLOCKS * b, per_stream) != 0
        for j in range(STEP_BLOCKS):
            rows = slice(j * QBLK, (j + 1) * QBLK)
            if j == 0:
                k2 = jnp.concatenate([kp_ref[...], kc_ref[rows, :]], axis=0)
                v2 = jnp.concatenate([vp_ref[...], vc_ref[rows, :]], axis=0)
            else:
                both = slice((j - 1) * QBLK, (j + 1) * QBLK)
                k2, v2 = kc_ref[both, :], vc_ref[both, :]
            mask = _band_mask(first_has_prev if j == 0 else True)
            sc = jnp.where(mask, _dot_nt(_stack_heads(q_ref[rows, :], heads), k2) * ATTN_SCALE, NEG_INF)
            mx = jnp.max(sc, axis=1, keepdims=True)
            pr = jnp.exp(sc - mx)
            den = jnp.sum(pr, axis=1, keepdims=True)
            o_all = _dot(pr.astype(BF16), v2) / den
            o_ref[rows, :] = _merge_heads(o_all, heads).astype(BF16)
            lse_ref[rows, :] = _merge_heads(jnp.broadcast_to(mx + jnp.log(den), o_all.shape), heads)

    sv = s.reshape(t, 3 * GROUP_W)
    return _call(
        body, name=f"attn_fwd_d{dil}", grid=(nb // STEP_BLOCKS,),
        in_specs=[_pair_block(0), _pair_block(1), _edge_block(1, -1, nb), _pair_block(2), _edge_block(2, -1, nb)],
        out_specs=[_pair_block(0), _pair_block(0)],
        out_shape=[jax.ShapeDtypeStruct((t, GROUP_W), BF16), jax.ShapeDtypeStruct((t, GROUP_W), F32)],
        args=(sv, sv, sv, sv, sv), semantics=("parallel",), carry=carry)


def _group_softmax(parts):
    mx = jnp.maximum(jnp.maximum(parts[0], parts[1]), parts[2])
    es = [jnp.exp(p - mx) for p in parts]
    den = es[0] + es[1] + es[2]
    return [e / den for e in es]


def _mixer_out(x, abcv, gates, os, lses, conv_w, conv_b, b_gate, w_pa, w_pb, w_o, carry=None):
    t, d = x.shape
    cw = conv_w.shape[1]
    tm = min(ROWS_MATMUL, t)

    def body(x_ref, abcv_ref, halo_ref, gates_ref, o0_ref, o1_ref, o2_ref, l0_ref, l1_ref, l2_ref, cw_ref, cb_ref,
             bg_ref, wpa_ref, wpb_ref, wo_ref, x1_ref, ya_ref, yb_ref, yap_ref, ybp_ref, mg_ref):
        m = pl.program_id(0)
        ab = abcv_ref[:, 0:cw].astype(F32)
        u = abcv_ref[:, cw:2 * cw].astype(F32) * abcv_ref[:, 2 * cw:3 * cw].astype(F32)
        hu = halo_ref[:, cw:2 * cw].astype(F32) * halo_ref[:, 2 * cw:3 * cw].astype(F32)
        hu = jnp.where(m > 0, hu, 0.0)
        cv = (cw_ref[0:1, :] * _shift_down(u, hu, 2) + cw_ref[1:2, :] * _shift_down(u, hu, 1)
              + cw_ref[2:3, :] * u + cb_ref[...])
        ya = (ab * cv).astype(BF16)
        ya_ref[...] = ya
        alphas = _group_softmax([_load_streams(r, dil, tm) for r, dil in zip((l0_ref, l1_ref, l2_ref), DILATIONS)])
        for i, (o_ref, dil) in enumerate(zip((o0_ref, o1_ref, o2_ref), DILATIONS)):
            sl = slice(i * GROUP_W, (i + 1) * GROUP_W)
            yb_ref[:, sl] = (alphas[i] * _load_streams(o_ref, dil, tm).astype(F32)).astype(BF16)
        yap = _dot_nt(ya, wpa_ref[...])
        ybp = _dot_nt(yb_ref[...], wpb_ref[...])
        yap_ref[...] = yap.astype(BF16)
        ybp_ref[...] = ybp.astype(BF16)
        sa = _sigmoid(gates_ref[:, 0:d].astype(F32) + bg_ref[0:1, :])
        sb = _sigmoid(gates_ref[:, d:2 * d].astype(F32) + bg_ref[1:2, :])
        merged = (sa * yap + sb * ybp).astype(BF16)
        mg_ref[...] = merged
        x1_ref[...] = x_ref[...] + _dot(merged, wo_ref[...])

    return _call(
        body, name="mixer_out", grid=(t // tm,),
        in_specs=[_rows(tm, d), _rows(tm, 3 * cw), _prev_halo(tm, 3 * cw), _rows(tm, 2 * d)]
        + [_stream_spec(dil, tm, GROUP_W) for dil in DILATIONS] * 2
        + [_resident((3, cw)), _resident((1, cw)), _resident((2, d)),
           _resident((d, cw)), _resident((d, ATTN_W)), _resident((d, d))],
        out_specs=[_rows(tm, d), _rows(tm, cw), _rows(tm, ATTN_W), _rows(tm, d), _rows(tm, d), _rows(tm, d)],
        out_shape=[jax.ShapeDtypeStruct((t, d), F32), jax.ShapeDtypeStruct((t, cw), BF16),
                   jax.ShapeDtypeStruct((t, ATTN_W), BF16), jax.ShapeDtypeStruct((t, d), BF16),
                   jax.ShapeDtypeStruct((t, d), BF16), jax.ShapeDtypeStruct((t, d), BF16)],
        args=(x, abcv, abcv, gates, *[_stream_view(a, dil) for a, dil in zip(os, DILATIONS)],
              *[_stream_view(a, dil) for a, dil in zip(lses, DILATIONS)], conv_w, conv_b, b_gate, w_pa, w_pb, w_o),
        semantics=("parallel",), carry=carry)


def _ffn_fwd(x1, target, g2, w_ut, conv_w, conv_b, w_d, g_f, carry=None):
    t, d = x1.shape
    dff = w_d.shape[0]
    tm = min(256, t)
    ck = _pick_tile(dff, 2816)

    def body(x1_ref, tg_ref, g2_ref, wut_ref, cw_ref, cb_ref, wd_ref, gf_ref, h2_ref, up_ref, act_ref, conv_ref,
             dx2_ref, dx2i_ref, acc_ref, loss_ref, halo_ref):
        m = pl.program_id(0)

        @pl.when(m == 0)
        def _():
            acc_ref[...] = jnp.zeros_like(acc_ref)
            loss_ref[...] = jnp.zeros_like(loss_ref)
            halo_ref[...] = jnp.zeros_like(halo_ref)

        h2 = _permute_rows(_interleave(tm), _rms_fwd(x1_ref[...], g2_ref[...])[0].astype(BF16))
        h2_ref[...] = h2

        def conv(c0):
            p = _dot_nt(h2, wut_ref[c0:c0 + ck, :])
            up_ref[:, c0:c0 + ck] = p.astype(BF16)
            hp = halo_ref[:, c0:c0 + ck]
            halo_ref[:, c0:c0 + ck] = p[tm - HALO:, :]
            return (cw_ref[0:1, c0:c0 + ck] * _shift_down_il(p, hp, 2)
                    + cw_ref[1:2, c0:c0 + ck] * _shift_down_il(p, hp, 1)
                    + cw_ref[2:3, c0:c0 + ck] * p + cb_ref[:, c0:c0 + ck])

        down = jnp.zeros((tm, d), F32)
        for c0 in range(0, dff, ck):
            gate = conv(c0)
            val = conv(dff + c0)
            conv_ref[:, c0:c0 + ck] = gate.astype(BF16)
            conv_ref[:, dff + c0:dff + c0 + ck] = val.astype(BF16)
            act = (gate * _sigmoid(gate) * val).astype(BF16)
            act_ref[:, c0:c0 + ck] = act
            down = down + _dot(act, wd_ref[c0:c0 + ck, :])
        x2 = x1_ref[...] + _permute_rows(_interleave(tm, inverse=True), down)
        y, _ = _rms_fwd(x2, gf_ref[...])
        diff = y - tg_ref[...]
        loss_ref[...] += 0.5 * jnp.sum(jnp.mean(diff * diff, axis=-1, keepdims=True))
        dx2, dg = _rms_bwd(x2, gf_ref[...], diff * (1.0 / d))
        dx2_ref[...] = dx2
        dx2i_ref[...] = _permute_rows(_interleave(tm), dx2.astype(BF16))
        acc_ref[...] += _stack_rows([_colsum(dg)], d)

    return _call(
        body, name="ffn_fwd", grid=(t // tm,),
        in_specs=[_rows(tm, d), _rows(tm, d), _resident((1, d)), _resident((2 * dff, d)), _resident((3, 2 * dff)),
                  _resident((1, 2 * dff)), _resident((dff, d)), _resident((1, d))],
        out_specs=[_rows(tm, d), _rows(tm, 2 * dff), _rows(tm, dff), _rows(tm, 2 * dff), _rows(tm, d), _rows(tm, d),
                   _acc_spec(d), _acc_spec(LANES)],
        out_shape=[jax.ShapeDtypeStruct((t, d), BF16), jax.ShapeDtypeStruct((t, 2 * dff), BF16),
                   jax.ShapeDtypeStruct((t, dff), BF16), jax.ShapeDtypeStruct((t, 2 * dff), BF16),
                   jax.ShapeDtypeStruct((t, d), F32), jax.ShapeDtypeStruct((t, d), BF16),
                   jax.ShapeDtypeStruct((SUBLANES, d), F32), jax.ShapeDtypeStruct((SUBLANES, LANES), F32)],
        args=(x1, target, g2, w_ut, conv_w, conv_b, w_d, g_f), semantics=("arbitrary",), carry=carry,
        scratch=[pltpu.VMEM((HALO, 2 * dff), F32)])


def _ffn_act_bwd(dx2, conv, w_d):
    t, d = dx2.shape
    dff = w_d.shape[0]
    tm = min(256, t)
    ck = _pick_tile(dff, 2816)

    def body(dx2_ref, conv_ref, wd_ref, dup_ref, acc_ref):
        m = pl.program_id(0)

        @pl.when(m == 0)
        def _():
            acc_ref[...] = jnp.zeros_like(acc_ref)

        dx2v = dx2_ref[...]
        for c0 in range(0, dff, ck):
            dact = _dot_nt(dx2v, wd_ref[c0:c0 + ck, :])
            gate = conv_ref[:, c0:c0 + ck].astype(F32)
            val = conv_ref[:, dff + c0:dff + c0 + ck].astype(F32)
            sg = _sigmoid(gate)
            dval = dact * gate * sg
            dgate = dact * val * sg * (1.0 + gate * (1.0 - sg))
            dup_ref[:, c0:c0 + ck] = dgate.astype(BF16)
            dup_ref[:, dff + c0:dff + c0 + ck] = dval.astype(BF16)
            acc_ref[:, c0:c0 + ck] += _stack_rows([_colsum(dgate)], ck)
            acc_ref[:, dff + c0:dff + c0 + ck] += _stack_rows([_colsum(dval)], ck)

    return pl.pallas_call(
        body, name="ffn_act_bwd", grid=(t // tm,),
        in_specs=[_rows(tm, d), _rows(tm, 2 * dff), _resident((dff, d))],
        out_specs=[_rows(tm, 2 * dff), _acc_spec(2 * dff)],
        out_shape=[jax.ShapeDtypeStruct((t, 2 * dff), BF16), jax.ShapeDtypeStruct((SUBLANES, 2 * dff), F32)],
        compiler_params=_params("arbitrary"),
    )(dx2, conv, w_d)


def _ffn_up_bwd(dup, up_pre, x1, dx2, conv_w, w_u, g2, carry=None):
    t, d = x1.shape
    n = dup.shape[1]
    tm = min(256, t)
    ck = _pick_tile(n, 256)
    last = t // tm - 1

    def body(dup_ref, nxt_ref, up_ref, x1_ref, dx2_ref, cw_ref, wu_ref, g2_ref, dpre_ref, dx1_ref, acc_ref, accw_ref):
        m = pl.program_id(0)

        @pl.when(m == 0)
        def _():
            acc_ref[...] = jnp.zeros_like(acc_ref)
            accw_ref[...] = jnp.zeros_like(accw_ref)

        dh = jnp.zeros((tm, d), F32)
        for c0 in range(0, n, ck):
            du = dup_ref[:, c0:c0 + ck].astype(F32)
            hn = jnp.where(m < last, nxt_ref[:, c0:c0 + ck].astype(F32), 0.0)
            du1 = _shift_up_il(du, hn, 1)
            du2 = _shift_up_il(du, hn, 2)
            dpre = (cw_ref[2:3, c0:c0 + ck] * du + cw_ref[1:2, c0:c0 + ck] * du1
                    + cw_ref[0:1, c0:c0 + ck] * du2).astype(BF16)
            dpre_ref[:, c0:c0 + ck] = dpre
            dh = dh + _dot(dpre, wu_ref[c0:c0 + ck, :])
            p = up_ref[:, c0:c0 + ck].astype(F32)
            accw_ref[:, c0:c0 + ck] += _stack_rows([_colsum(du2 * p), _colsum(du1 * p), _colsum(du * p)], ck)
        dh = _permute_rows(_interleave(tm, inverse=True), dh)
        dx, dg = _rms_bwd(x1_ref[...], g2_ref[...], dh)
        dx1_ref[...] = dx2_ref[...] + dx
        acc_ref[...] += _stack_rows([_colsum(dg)], d)

    return _call(
        body, name="ffn_up_bwd", grid=(t // tm,),
        in_specs=[_rows(tm, n), _next_halo(tm, n, t), _rows(tm, n), _rows(tm, d), _rows(tm, d), _resident((3, n)),
                  _resident((n, d)), _resident((1, d))],
        out_specs=[_rows(tm, n), _rows(tm, d), _acc_spec(d), _acc_spec(n)],
        out_shape=[jax.ShapeDtypeStruct((t, n), BF16), jax.ShapeDtypeStruct((t, d), F32),
                   jax.ShapeDtypeStruct((SUBLANES, d), F32), jax.ShapeDtypeStruct((SUBLANES, n), F32)],
        args=(dup, dup, up_pre, x1, dx2, conv_w, w_u, g2), semantics=("arbitrary",), carry=carry)


def _tn_matmul(a, b, name):
    t, mdim = a.shape
    n = b.shape[1]
    tk = min(1024, t)
    tmm = _pick_tile(mdim, 1536)
    tn = _pick_tile(n, 1024)

    def body(a_ref, b_ref, o_ref, acc_ref):
        k = pl.program_id(2)

        @pl.when(k == 0)
        def _():
            acc_ref[...] = jnp.zeros_like(acc_ref)

        acc_ref[...] += _dot_tn(a_ref[...].astype(BF16), b_ref[...].astype(BF16))

        @pl.when(k == t // tk - 1)
        def _():
            o_ref[...] = acc_ref[...].astype(BF16)

    return pl.pallas_call(
        body, name=name, grid=(mdim // tmm, n // tn, t // tk),
        in_specs=[pl.BlockSpec((tk, tmm), lambda i, j, k: (k, i)), pl.BlockSpec((tk, tn), lambda i, j, k: (k, j))],
        out_specs=pl.BlockSpec((tmm, tn), lambda i, j, k: (i, j)),
        out_shape=jax.ShapeDtypeStruct((mdim, n), BF16),
        scratch_shapes=[pltpu.VMEM((tmm, tn), F32)],
        compiler_params=_params("parallel", "parallel", "arbitrary"),
    )(a, b)


def _mixer_bwd(dx1, gates, yap, ybp, os, lses, b_gate, w_o, w_pa, w_pb):
    t, d = dx1.shape
    cw = w_pa.shape[1]
    tm = min(ROWS_MATMUL, t)

    def body(dx1_ref, gates_ref, yap_ref, ybp_ref, o0_ref, o1_ref, o2_ref, l0_ref, l1_ref, l2_ref, bg_ref, wo_ref,
             wpa_ref, wpb_ref, dgates_ref, dyap_ref, dybp_ref, dya_ref, do0_ref, do1_ref, do2_ref, dl0_ref, dl1_ref,
             dl2_ref, acc_ref):
        m = pl.program_id(0)

        @pl.when(m == 0)
        def _():
            acc_ref[...] = jnp.zeros_like(acc_ref)

        dmg = _dot_nt(dx1_ref[...].astype(BF16), wo_ref[...])
        sa = _sigmoid(gates_ref[:, 0:d].astype(F32) + bg_ref[0:1, :])
        sb = _sigmoid(gates_ref[:, d:2 * d].astype(F32) + bg_ref[1:2, :])
        dyap = (dmg * sa).astype(BF16)
        dybp = (dmg * sb).astype(BF16)
        dga = dmg * yap_ref[...].astype(F32) * sa * (1.0 - sa)
        dgb = dmg * ybp_ref[...].astype(F32) * sb * (1.0 - sb)
        dyap_ref[...] = dyap
        dybp_ref[...] = dybp
        dgates_ref[:, 0:d] = dga.astype(BF16)
        dgates_ref[:, d:2 * d] = dgb.astype(BF16)
        acc_ref[...] += _stack_rows([_colsum(dga), _colsum(dgb)], d)
        dya_ref[...] = _dot(dyap, wpa_ref[...]).astype(BF16)
        dyb = _dot(dybp, wpb_ref[...])

        ri = lax.broadcasted_iota(jnp.int32, (GROUP_W, GROUP_W), 0) // HEAD_DIM
        ci = lax.broadcasted_iota(jnp.int32, (GROUP_W, GROUP_W), 1) // HEAD_DIM
        same_head = (ri == ci).astype(BF16)
        alphas = _group_softmax([_load_streams(r, dil, tm) for r, dil in zip((l0_ref, l1_ref, l2_ref), DILATIONS)])
        prod = jnp.zeros((tm, GROUP_W), F32)
        for i, (o_ref, do_ref, dil) in enumerate(zip((o0_ref, o1_ref, o2_ref), (do0_ref, do1_ref, do2_ref), DILATIONS)):
            dov = alphas[i] * dyb[:, i * GROUP_W:(i + 1) * GROUP_W]
            _store_streams(do_ref, dil, tm, dov.astype(BF16))
            prod = prod + dov * _load_streams(o_ref, dil, tm).astype(F32)
        hi = prod.astype(BF16)
        lo = (prod - hi.astype(F32)).astype(BF16)
        dtot = _dot(hi, same_head) + _dot(lo, same_head)
        for alpha, dl_ref, dil in zip(alphas, (dl0_ref, dl1_ref, dl2_ref), DILATIONS):
            _store_streams(dl_ref, dil, tm, alpha * dtot)

    streams = [_stream_spec(dil, tm, GROUP_W) for dil in DILATIONS]
    res = _call(
        body, name="mixer_bwd", grid=(t // tm,),
        in_specs=[_rows(tm, d), _rows(tm, 2 * d), _rows(tm, d), _rows(tm, d)] + streams * 2
        + [_resident((2, d)), _resident((d, d)), _resident((d, cw)), _resident((d, ATTN_W))],
        out_specs=[_rows(tm, 2 * d), _rows(tm, d), _rows(tm, d), _rows(tm, cw)] + streams * 2 + [_acc_spec(d)],
        out_shape=[jax.ShapeDtypeStruct((t, 2 * d), BF16), jax.ShapeDtypeStruct((t, d), BF16),
                   jax.ShapeDtypeStruct((t, d), BF16), jax.ShapeDtypeStruct((t, cw), BF16)]
        + [jax.ShapeDtypeStruct((dil, t // dil, GROUP_W), BF16) for dil in DILATIONS]
        + [jax.ShapeDtypeStruct((dil, t // dil, GROUP_W), F32) for dil in DILATIONS]
        + [jax.ShapeDtypeStruct((SUBLANES, d), F32)],
        args=(dx1, gates, yap, ybp, *[_stream_view(a, dil) for a, dil in zip(os, DILATIONS)],
              *[_stream_view(a, dil) for a, dil in zip(lses, DILATIONS)], b_gate, w_o, w_pa, w_pb),
        semantics=("arbitrary",))
    dgates, dyap, dybp, dya = res[:4]
    dos = [a.reshape(t, GROUP_W) for a in res[4:7]]
    dls = [a.reshape(t, GROUP_W) for a in res[7:10]]
    return dgates, dyap, dybp, dya, dos, dls, res[10]


def _attn_bwd(s, do, lse, dl, dil, carry=None):
    t = s.shape[0] * s.shape[1]
    nb = t // QBLK
    per_stream = nb // dil

    def body(q_ref, qn_ref, kc_ref, kp_ref, vc_ref, vp_ref, do_ref, don_ref, lse_ref, lsen_ref, dl_ref, dln_ref,
             ds_ref):
        b = pl.program_id(0)
        lane, heads = _head_masks()
        first_has_prev = lax.rem(STEP_BLOCKS * b, per_stream) != 0
        last_has_next = lax.rem(STEP_BLOCKS * (b + 1), per_stream) != 0

        def cols(v):
            return jnp.concatenate([jnp.sum(jnp.where(lane == h * HEAD_DIM, v, 0.0), axis=1, keepdims=True)
                                    for h in range(HEADS_PER_GROUP)], axis=0)

        def pair(qs, dos, k, v, valid, lse_c, dl_c):
            s = jnp.where(valid, _dot_nt(qs, k) * ATTN_SCALE, NEG_INF)
            p = jnp.exp(s - lse_c)
            ds = p * (_dot_nt(dos, v) - dl_c)
            return p.astype(BF16), ds.astype(BF16)

        for j in range(STEP_BLOCKS):
            rows, hi = slice(j * QBLK, (j + 1) * QBLK), slice((j + 1) * QBLK, (j + 2) * QBLK)
            q, do, lse, dl = q_ref[rows, :], do_ref[rows, :], lse_ref[rows, :], dl_ref[rows, :]
            kc, vc = kc_ref[rows, :], vc_ref[rows, :]
            if j == 0:
                k2 = jnp.concatenate([kp_ref[...], kc], axis=0)
                v2 = jnp.concatenate([vp_ref[...], vc], axis=0)
                mask = _band_mask(first_has_prev)
            else:
                both = slice((j - 1) * QBLK, (j + 1) * QBLK)
                k2, v2, mask = kc_ref[both, :], vc_ref[both, :], _band_mask(True)
            if j < STEP_BLOCKS - 1:
                qn, don, lsen, dln = q_ref[hi, :], do_ref[hi, :], lse_ref[hi, :], dl_ref[hi, :]
                mask_n = _next_mask(True)
            else:
                qn, don, lsen, dln = qn_ref[...], don_ref[...], lsen_ref[...], dln_ref[...]
                mask_n = _next_mask(last_has_next)
            qs, qns = _stack_heads(q, heads), _stack_heads(qn, heads)
            dos, dons = _stack_heads(do, heads), _stack_heads(don, heads)
            p_q, ds_q = pair(qs, dos, k2, v2, mask, cols(lse), cols(dl))
            p_n, ds_n = pair(qns, dons, kc, vc, mask_n, cols(lsen), cols(dln))
            dq = _merge_heads(_dot(ds_q, k2), heads)
            dk = _dot_tn(jnp.concatenate([ds_q[:, QBLK:], ds_n], axis=0), jnp.concatenate([qs, qns], axis=0))
            dv = _dot_tn(jnp.concatenate([p_q[:, QBLK:], p_n], axis=0), jnp.concatenate([dos, dons], axis=0))
            ds_ref[rows, 0:GROUP_W] = (dq * ATTN_SCALE).astype(BF16)
            ds_ref[rows, GROUP_W:2 * GROUP_W] = (dk * ATTN_SCALE).astype(BF16)
            ds_ref[rows, 2 * GROUP_W:3 * GROUP_W] = dv.astype(BF16)

    sv = s.reshape(t, 3 * GROUP_W)
    cur, nxt = _pair_block(0), _edge_block(0, STEP_BLOCKS, nb)
    return _call(
        body, name=f"attn_bwd_d{dil}", grid=(nb // STEP_BLOCKS,),
        in_specs=[cur, nxt, _pair_block(1), _edge_block(1, -1, nb), _pair_block(2), _edge_block(2, -1, nb),
                  cur, nxt, cur, nxt, cur, nxt],
        out_specs=[pl.BlockSpec((STEP_BLOCKS * QBLK, 3 * GROUP_W), lambda b: (b, 0))],
        out_shape=[jax.ShapeDtypeStruct((t, 3 * GROUP_W), BF16)],
        args=(sv, sv, sv, sv, sv, sv, do, do, lse, lse, dl, dl), semantics=("parallel",), carry=carry)


def _conv_mixer_bwd(abcv, dya, conv_w, conv_b):
    t = abcv.shape[0]
    cw = conv_w.shape[1]
    tm = min(256, t)
    last = t // tm - 1

    def body(a_ref, ap_ref, an_ref, dya_ref, dyan_ref, cw_ref, cb_ref, d_ref, acc_ref):
        m = pl.program_id(0)

        @pl.when(m == 0)
        def _():
            acc_ref[...] = jnp.zeros_like(acc_ref)

        ab = a_ref[:, 0:cw].astype(F32)
        ac = a_ref[:, cw:2 * cw].astype(F32)
        av = a_ref[:, 2 * cw:3 * cw].astype(F32)
        u = ac * av
        hu = ap_ref[:, cw:2 * cw].astype(F32) * ap_ref[:, 2 * cw:3 * cw].astype(F32)
        hu = jnp.where(m > 0, hu, 0.0)
        u1 = _shift_down(u, hu, 1)
        u2 = _shift_down(u, hu, 2)
        cv = cw_ref[0:1, :] * u2 + cw_ref[1:2, :] * u1 + cw_ref[2:3, :] * u + cb_ref[...]
        dya_v = dya_ref[...].astype(F32)
        dcv = dya_v * ab
        ndcv = jnp.where(m < last, dyan_ref[...].astype(F32) * an_ref[:, 0:cw].astype(F32), 0.0)
        du = (cw_ref[2:3, :] * dcv + cw_ref[1:2, :] * _shift_up(dcv, ndcv, 1)
              + cw_ref[0:1, :] * _shift_up(dcv, ndcv, 2))
        d_ref[:, 0:cw] = (dya_v * cv).astype(BF16)
        d_ref[:, cw:2 * cw] = (du * av).astype(BF16)
        d_ref[:, 2 * cw:3 * cw] = (du * ac).astype(BF16)
        acc_ref[...] += _stack_rows([_colsum(dcv * u2), _colsum(dcv * u1), _colsum(dcv * u), _colsum(dcv)], cw)

    return pl.pallas_call(
        body, name="conv_mixer_bwd", grid=(t // tm,),
        in_specs=[_rows(tm, 3 * cw), _prev_halo(tm, 3 * cw), _next_halo(tm, 3 * cw, t), _rows(tm, cw),
                  _next_halo(tm, cw, t), _resident((3, cw)), _resident((1, cw))],
        out_specs=[_rows(tm, 3 * cw), _acc_spec(cw)],
        out_shape=[jax.ShapeDtypeStruct((t, 3 * cw), BF16), jax.ShapeDtypeStruct((SUBLANES, cw), F32)],
        compiler_params=_params("arbitrary"),
    )(abcv, abcv, abcv, dya, dya, conv_w, conv_b)


def _in_proj_bwd(x, dx1, dabcv, dss, dgates, w_in, g1, carry=None):
    t, d = x.shape
    qkv0 = dabcv.shape[1]
    n = w_in.shape[0]
    tm = min(ROWS_MATMUL, t)

    def body(x_ref, dx1_ref, da_ref, ds0_ref, ds1_ref, ds2_ref, dg_ref, w_ref, g_ref, dx_ref, acc_ref):
        m = pl.program_id(0)

        @pl.when(m == 0)
        def _():
            acc_ref[...] = jnp.zeros_like(acc_ref)

        dss_tok = [_load_streams(ds_ref, dil, tm) for ds_ref, dil in zip((ds0_ref, ds1_ref, ds2_ref), DILATIONS)]
        dqkv = jnp.concatenate([ds[:, j * GROUP_W:(j + 1) * GROUP_W] for j in range(3) for ds in dss_tok], axis=1)
        dh = (_dot(da_ref[...], w_ref[0:qkv0, :]) + _dot(dqkv, w_ref[qkv0:qkv0 + 3 * ATTN_W, :])
              + _dot(dg_ref[...], w_ref[qkv0 + 3 * ATTN_W:n, :]))
        dx, dg = _rms_bwd(x_ref[...], g_ref[...], dh)
        dx_ref[...] = dx1_ref[...] + dx
        acc_ref[...] += _stack_rows([_colsum(dg)], d)

    return _call(
        body, name="in_proj_bwd", grid=(t // tm,),
        in_specs=[_rows(tm, d), _rows(tm, d), _rows(tm, qkv0)]
        + [_stream_spec(dil, tm, 3 * GROUP_W) for dil in DILATIONS]
        + [_rows(tm, 2 * d), _resident((n, d)), _resident((1, d))],
        out_specs=[_rows(tm, d), _acc_spec(d)],
        out_shape=[jax.ShapeDtypeStruct((t, d), F32), jax.ShapeDtypeStruct((SUBLANES, d), F32)],
        args=(x, dx1, dabcv, *[_stream_view(a, dil) for a, dil in zip(dss, DILATIONS)], dgates, w_in, g1),
        semantics=("arbitrary",), carry=carry)


def _dw_in_qkv(ds, h, dil):
    t, d = h.shape
    tk = min(1024, t)
    sub = min(256, t)
    width = 3 * GROUP_W

    def body(ds_ref, h_ref, o_ref, acc_ref):
        k = pl.program_id(0)

        @pl.when(k == 0)
        def _():
            acc_ref[...] = jnp.zeros_like(acc_ref)

        upd = None
        for i in range(tk // sub):
            rows = ds_ref[:, i * (sub // dil):(i + 1) * (sub // dil), :].reshape(sub, width)
            if dil > 1:
                rows = _permute_rows(_perm(dil, sub, inverse=True), rows)
            term = _dot_tn(rows, h_ref[i * sub:(i + 1) * sub, :])
            upd = term if upd is None else upd + term
        acc_ref[...] += upd

        @pl.when(k == t // tk - 1)
        def _():
            o_ref[...] = acc_ref[...].astype(BF16)

    return pl.pallas_call(
        body, name=f"dw_in_qkv_d{dil}", grid=(t // tk,),
        in_specs=[_stream_spec(dil, tk, width), _rows(tk, d)],
        out_specs=pl.BlockSpec((width, d), lambda k: (0, 0)),
        out_shape=jax.ShapeDtypeStruct((width, d), BF16),
        scratch_shapes=[pltpu.VMEM((width, d), F32)],
        compiler_params=_params("arbitrary"),
    )(_stream_view(ds, dil), h)


def _local_step(x, h, target, p, late):
    cw = p["conv_a_w"].shape[1]
    (abcv, gates, *ss), (g_up,) = _in_proj(h, p["w_in"], cw, carry=_Exchange("gather", [late["w_up"]]))
    w_up = _full_from_gathered(g_up)
    (o0, lse0), g_proj = _attn_fwd(ss[0], DILATIONS[0],
                                   carry=_Exchange("gather", [late["w_proj_a"], late["w_proj_b"]]))
    (o1, lse1), (g_out,) = _attn_fwd(ss[1], DILATIONS[1], carry=_Exchange("gather", [late["w_out"]]))
    o2, lse2 = _attn_fwd(ss[2], DILATIONS[2])
    w_pa, w_pb, w_out = [_full_from_gathered(g) for g in (*g_proj, g_out)]
    os, lses = (o0, o1, o2), (lse0, lse1, lse2)
    (x1, ya, yb, yap, ybp, merged), (g_down,) = _mixer_out(
        x, abcv, gates, os, lses, p["conv_a_w"], p["conv_a_b"], p["b_gate"], w_pa, w_pb, w_out,
        carry=_Exchange("gather", [late["w_down"]]))
    w_down = _full_from_gathered(g_down)
    h2, up_pre, act, conv, dx2, dx2i, acc_gf, loss = _ffn_fwd(x1, target, p["norm_ffn_g"], w_up, p["ffn_conv_w"],
                                                              p["ffn_conv_b"], w_down, p["final_norm_g"])

    parts, got = {}, {}
    dup, acc_fb = _ffn_act_bwd(dx2i, conv, w_down)
    parts["w_down"] = _by_destination(_tn_matmul(act, dx2i, "dw_down"))
    (dpre, dx1, acc_g2, acc_fw), (got["w_down"],) = _ffn_up_bwd(dup, up_pre, x1, dx2, p["ffn_conv_w"], w_up,
                                                                p["norm_ffn_g"],
                                                                carry=_Exchange("scatter", [parts["w_down"]]))
    parts["w_up"] = _by_destination(_tn_matmul(dpre, h2, "dw_up"))
    dgates, dyap, dybp, dya, dos, dls, acc_bg = _mixer_bwd(dx1, gates, yap, ybp, os, lses, p["b_gate"], w_out,
                                                           w_pa, w_pb)
    parts["w_out"] = _by_destination(_tn_matmul(merged, dx1, "dw_out"))
    parts["w_proj_a"] = _by_destination(_tn_matmul(dyap, ya, "dw_proj_a"))
    parts["w_proj_b"] = _by_destination(_tn_matmul(dybp, yb, "dw_proj_b"))
    minor = ("w_out", "w_proj_a", "w_proj_b")
    half = parts["w_up"].shape[1] // 2
    (ds0,), received = _attn_bwd(ss[0], dos[0], lses[0], dls[0], DILATIONS[0],
                                 carry=_Exchange("scatter", [parts[n] for n in minor]))
    got.update(zip(minor, received))
    (ds1,), first_half = _attn_bwd(ss[1], dos[1], lses[1], dls[1], DILATIONS[1],
                                   carry=_Exchange("scatter", [parts["w_up"]], rows=(0, half)))
    (ds2,), (got["w_up"],) = _attn_bwd(ss[2], dos[2], lses[2], dls[2], DILATIONS[2],
                                       carry=_Exchange("scatter", [parts["w_up"]], rows=(half, half),
                                                       into=first_half))
    dss = [ds0, ds1, ds2]
    dabcv, acc_ca = _conv_mixer_bwd(abcv, dya, p["conv_a_w"], p["conv_a_b"])
    dw_s = [_dw_in_qkv(ds, h, dil) for ds, dil in zip(dss, DILATIONS)]
    dw_qkv = [w[j * GROUP_W:(j + 1) * GROUP_W] for j in range(3) for w in dw_s]
    g_w_in = jnp.concatenate([_tn_matmul(dabcv, h, "dw_in_a"), *dw_qkv, _tn_matmul(dgates, h, "dw_in_g")], axis=0)
    parts["w_in"] = _by_destination(g_w_in)
    (dx, acc_g1), (got["w_in"],) = _in_proj_bwd(x, dx1, dabcv, dss, dgates, p["w_in"], p["norm_mix_g"],
                                                carry=_Exchange("scatter", [parts["w_in"]]))
    small = dict(norm_mix_g=acc_g1[0:1], b_gate=acc_bg[0:2], conv_a_w=acc_ca[0:3], conv_a_b=acc_ca[3:4],
                 norm_ffn_g=acc_g2[0:1], ffn_conv_w=acc_fw[0:3], ffn_conv_b=acc_fb[0:1], final_norm_g=acc_gf[0:1])
    return loss[0, 0], dx, parts, got, small


class _TwoLevelGather:
    def __init__(self, arrays):
        self.arrays, self.into = list(arrays), []
        n = len(self.arrays)
        self.out_shape = [jax.ShapeDtypeStruct((N_DEV,) + a.shape, a.dtype) for a in self.arrays]
        self.scratch = [pltpu.SemaphoreType.DMA((n, 7)), pltpu.SemaphoreType.DMA((n, 7)),
                        pltpu.SemaphoreType.DMA((n,))]

    def _parts(self, ins, outs, sems):
        send_sems, recv_sems, local_sems = sems
        x, y, c = _mesh_pos()
        me, sibling = (x, y, c), (x, y, 1 - c)
        chips = [(1 - x, y), (x, 1 - y), (1 - x, 1 - y)]

        def copy(i, k, block, to, src=None):
            rows = outs[i].at[_dev_index(*block)]
            return pltpu.make_async_remote_copy(
                src_ref=rows if src is None else src, dst_ref=rows, send_sem=send_sems.at[i, k],
                recv_sem=recv_sems.at[i, k], device_id=to, device_id_type=MESH)

        mine = [pltpu.make_async_copy(ins[i], outs[i].at[_dev_index(*me)], local_sems.at[i])
                for i in range(len(ins))]
        first = []
        for i in range(len(ins)):
            first.append(copy(i, 0, me, sibling, src=ins[i]))
            first += [copy(i, 1 + j, me, (*chip, c), src=ins[i]) for j, chip in enumerate(chips)]
        return copy, mine, first, me, sibling, chips, c

    def start(self, ins, outs, sems):
        _, mine, first, *_ = self._parts(ins, outs, sems)
        for cp in mine + first:
            cp.start()

    def wait(self, ins, outs, sems):
        copy, mine, first, me, sibling, chips, c = self._parts(ins, outs, sems)
        passed = []
        for i in range(len(ins)):
            for j, chip in enumerate(chips):
                copy(i, 1 + j, (*chip, c), me).wait_recv()
                fw = copy(i, 4 + j, (*chip, c), sibling)
                fw.start()
                passed.append(fw)
        for i in range(len(ins)):
            copy(i, 0, sibling, me).wait_recv()
            for j, chip in enumerate(chips):
                copy(i, 4 + j, (*chip, 1 - c), me).wait_recv()
        for cp in first + passed:
            cp.wait_send()
        for cp in mine:
            cp.wait()


def _all_reduce_small(v):
    r = v.shape[0]

    def body(v_ref, o_ref, gath, send_sems, recv_sems):
        x, y, c = _mesh_pos()
        me = _dev_index(x, y, c)
        gath[me] = v_ref[...]
        flips = [(kx, ky, kc) for kx in (0, 1) for ky in (0, 1) for kc in (0, 1)][1:]
        copies = []
        for k, (kx, ky, kc) in enumerate(flips):
            px = 1 - x if kx else x
            py = 1 - y if ky else y
            pc = 1 - c if kc else c
            cp = pltpu.make_async_remote_copy(
                src_ref=v_ref, dst_ref=gath.at[me], send_sem=send_sems.at[k], recv_sem=recv_sems.at[k],
                device_id=(px, py, pc), device_id_type=MESH)
            cp.start()
            copies.append((cp, _dev_index(px, py, pc)))
        for k, (cp, peer) in enumerate(copies):
            pltpu.make_async_remote_copy(
                src_ref=v_ref, dst_ref=gath.at[peer], send_sem=send_sems.at[k], recv_sem=recv_sems.at[k],
                device_id=(x, y, c), device_id_type=MESH).wait_recv()
        for cp, _ in copies:
            cp.wait_send()
        total = gath[0]
        for j in range(1, N_DEV):
            total = total + gath[j]
        o_ref[...] = total

    return pl.pallas_call(
        body, name="all_reduce_small",
        in_specs=[pl.BlockSpec(memory_space=pltpu.VMEM)], out_specs=pl.BlockSpec(memory_space=pltpu.VMEM),
        out_shape=jax.ShapeDtypeStruct((r, LANES), F32),
        scratch_shapes=[pltpu.VMEM((N_DEV, r, LANES), F32), pltpu.SemaphoreType.DMA((7,)),
                        pltpu.SemaphoreType.DMA((7,))],
    )(v)


def _adamw_math(w, g, m, v):
    m2 = ADAM_B1 * m + (1.0 - ADAM_B1) * g
    v2 = ADAM_B2 * v + (1.0 - ADAM_B2) * (g * g)
    m_hat = m2 / (1.0 - ADAM_B1 ** ADAM_STEP)
    v_hat = v2 / (1.0 - ADAM_B2 ** ADAM_STEP)
    delta = -ADAM_LR * (m_hat / (jnp.sqrt(v_hat) + ADAM_EPS) + ADAM_WD * w)
    return delta, m2, v2


def _adamw_big(w, m, v, part, got, me):
    r, c = part.shape[1:]
    flip = w.shape != (r, c)
    tr = r if flip else max(t for t in range(HALO, min(r, 512) + 1, HALO) if r % t == 0)

    def body(me_ref, w_ref, m_ref, v_ref, own_ref, *rest):
        del me_ref
        got_refs, (g_out, d_out, m_out, v_out) = rest[:N_DEV - 1], rest[N_DEV - 1:]
        g = own_ref[...].astype(F32)
        for ref in got_refs:
            g = g + ref[...].astype(F32)
        if flip:
            g = g.T
        delta, m2, v2 = _adamw_math(w_ref[...], g, m_ref[...], v_ref[...])
        g_out[...] = g
        d_out[...] = delta
        m_out[...] = m2
        v_out[...] = v2

    def peer_block(k):
        return pl.BlockSpec((None, tr, c), lambda i, me_ref: (jnp.bitwise_xor(me_ref[0], k), i, 0))

    plain = pl.BlockSpec(w.shape if flip else (tr, c), lambda i, me_ref: (i, 0))
    out = jax.ShapeDtypeStruct(w.shape, F32)
    return pl.pallas_call(
        body, name="adamw_big",
        grid_spec=pltpu.PrefetchScalarGridSpec(
            num_scalar_prefetch=1, grid=(r // tr,),
            in_specs=[plain, plain, plain] + [peer_block(k) for k in range(N_DEV)],
            out_specs=[plain] * 4),
        out_shape=[out] * 4,
        compiler_params=_params("parallel"),
    )(me, w, m, v, part, *([got] * (N_DEV - 1)))


def _adamw_small(ws, gs, ms, vs):
    n = len(ws)

    def body(*refs):
        ins, outs = refs[:4 * n], refs[4 * n:]
        for i in range(n):
            delta, m2, v2 = _adamw_math(ins[i][...], ins[n + i][...], ins[2 * n + i][...], ins[3 * n + i][...])
            outs[i][...] = delta
            outs[n + i][...] = m2
            outs[2 * n + i][...] = v2

    out = [jax.ShapeDtypeStruct(w.shape, F32) for w in ws]
    res = pl.pallas_call(body, name="adamw_small", out_shape=out * 3)(*ws, *gs, *ms, *vs)
    return res[:n], res[n:2 * n], res[2 * n:]


BIG = ("w_in", "w_proj_a", "w_proj_b", "w_out", "w_up", "w_down")
LATE = ("w_proj_a", "w_proj_b", "w_out", "w_up", "w_down")
COLUMN_SHARDED = ("w_in", "w_proj_a", "w_proj_b", "w_up")
WIDE_COLUMN_SHARDED = ("w_in", "w_up")
SMALL = ("norm_mix_g", "b_gate", "conv_a_w", "conv_a_b", "norm_ffn_g", "ffn_conv_w", "ffn_conv_b", "final_norm_g")
SMALL_SHARDED = ("b_gate", "conv_a_w", "ffn_conv_w")
WEIGHTS = ("norm_mix_g", "w_in", "b_gate", "conv_a_w", "conv_a_b", "w_proj_a", "w_proj_b", "w_out", "norm_ffn_g",
           "w_up", "ffn_conv_w", "ffn_conv_b", "w_down", "final_norm_g")


def _pack(vectors, rows):
    flat = jnp.concatenate([v.reshape(-1) for v in vectors])
    return jnp.pad(flat, (0, rows * LANES - flat.shape[0])).reshape(rows, LANES)


def _packed_rows(count):
    rows = -(-count // LANES)
    return -(-rows // SUBLANES) * SUBLANES


def _unpack(packed, shapes):
    flat = packed.reshape(-1)
    out, lo = [], 0
    for s in shapes:
        size = 1
        for dim in s:
            size *= dim
        out.append(flat[lo:lo + size].reshape(s))
        lo += size
    return out


def _full_from_gathered(gathered):
    _, r, c = gathered.shape
    return gathered.reshape(N_DEV * r, c)


def _by_destination(grad):
    rr, cc = grad.shape
    return grad.reshape(N_DEV, rr // N_DEV, cc)


def _block2d(name, a):
    a = a.reshape(a.shape[-2:])
    return a.T if name in WIDE_COLUMN_SHARDED else a


def kernel(x, norm_mix_g, w_in, b_gate, conv_a_w, conv_a_b, w_proj_a, w_proj_b, w_out, norm_ffn_g, w_up, ffn_conv_w, ffn_conv_b, w_down, final_norm_g, loss_target, m_norm_mix_g, m_w_in, m_b_gate, m_conv_a_w, m_conv_a_b, m_w_proj_a, m_w_proj_b, m_w_out, m_norm_ffn_g, m_w_up, m_ffn_conv_w, m_ffn_conv_b, m_w_down, m_final_norm_g, v_norm_mix_g, v_w_in, v_b_gate, v_conv_a_w, v_conv_a_b, v_w_proj_a, v_w_proj_b, v_w_out, v_norm_ffn_g, v_w_up, v_ffn_conv_w, v_ffn_conv_b, v_w_down, v_final_norm_g):
    given = dict(locals())
    shard = {n: given[n] for n in WEIGHTS}
    mom_m = {n: given["m_" + n] for n in WEIGHTS}
    mom_v = {n: given["v_" + n] for n in WEIGHTS}
    xi, yi, ci = _mesh_pos()
    me = _dev_index(xi, yi, ci)
    me1 = me.astype(jnp.int32).reshape(1)

    big2d = {n: _block2d(n, shard[n]) for n in BIG}
    small_shapes = [shard[n].shape[1:] for n in SMALL_SHARDED]
    n_small = sum(s[0] * s[1] for s in small_shapes)
    packed_small = _pack([shard[n] for n in SMALL_SHARDED], _packed_rows(n_small))
    (h,), gathered = _norm_in(x[0], shard["norm_mix_g"],
                              carry=_TwoLevelGather([big2d["w_in"].astype(BF16), packed_small]))
    p = {"w_in": _full_from_gathered(gathered[0])}
    flat_small = gathered[-1].reshape(N_DEV, -1)
    lo = 0
    for n, (rows, width) in zip(SMALL_SHARDED, small_shapes):
        blocks = flat_small[:, lo:lo + rows * width].reshape(N_DEV, rows, width)
        p[n] = blocks.transpose(1, 0, 2).reshape(rows, N_DEV * width)
        lo += rows * width
    p["norm_mix_g"], p["norm_ffn_g"] = shard["norm_mix_g"], shard["norm_ffn_g"]
    p["conv_a_b"], p["ffn_conv_b"] = shard["conv_a_b"], shard["ffn_conv_b"]
    p["final_norm_g"] = shard["final_norm_g"].reshape(1, -1)
    late = {n: (big2d[n].T if n in ("w_proj_a", "w_proj_b") else big2d[n]).astype(BF16) for n in LATE}

    loss_part, dx, parts, got, g_small = _local_step(x[0], h, loss_target[0], p, late)

    results = {}
    for n in BIG:
        outs = _adamw_big(big2d[n], _block2d(n, mom_m[n]), _block2d(n, mom_v[n]), parts[n], got[n], me1)
        results[n] = [_block2d(n, o).reshape(shard[n].shape) for o in outs]

    small_full_shapes = [g_small[n].shape for n in SMALL]
    n_vec = sum(s[0] * s[1] for s in small_full_shapes) + 1
    packed = _pack([g_small[n] for n in SMALL] + [loss_part.reshape(1)], _packed_rows(n_vec))
    reduced = _all_reduce_small(packed)
    *g_full, loss_vec = _unpack(reduced, small_full_shapes + [(1,)])
    loss = loss_vec[0]
    own_g = []
    for n, g in zip(SMALL, g_full):
        if n in SMALL_SHARDED:
            width = shard[n].shape[-1]
            g = lax.dynamic_slice_in_dim(g, me * width, width, axis=1)
        own_g.append(g.reshape(shard[n].shape))
    def rows2d(a):
        return a.reshape(-1, a.shape[-1])

    deltas, new_ms, new_vs = _adamw_small([rows2d(shard[n]) for n in SMALL], [rows2d(g) for g in own_g],
                                          [rows2d(mom_m[n]) for n in SMALL], [rows2d(mom_v[n]) for n in SMALL])
    for i, n in enumerate(SMALL):
        results[n] = [own_g[i]] + [a.reshape(shard[n].shape) for a in (deltas[i], new_ms[i], new_vs[i])]

    grad_x = dx.reshape(x.shape)
    return (loss, grad_x, *[results[n][0] for n in WEIGHTS], *[results[n][1] for n in WEIGHTS],
            *[results[n][2] for n in WEIGHTS], *[results[n][3] for n in WEIGHTS])
```

```python
import functools

import jax
import jax.numpy as jnp
from jax import lax
from jax.experimental import pallas as pl
from jax.experimental.pallas import tpu as pltpu

F32 = jnp.float32
BF16 = jnp.bfloat16
MESH = pl.DeviceIdType.MESH

N_DEV = 8
RMS_EPS = 1e-6
NEG_INF = -1e30
N_GROUPS = 3
DILATIONS = (1, 4, 16)
HEADS_PER_GROUP = 4
HEAD_DIM = 64
GROUP_W = HEADS_PER_GROUP * HEAD_DIM
ATTN_W = N_GROUPS * GROUP_W
QBLK = 128
STEP_BLOCKS = 4
BWD_STEP_BLOCKS = 2
ATTN_SCALE = HEAD_DIM ** -0.5

ADAM_LR = 0.001
ADAM_B1 = 0.9
ADAM_B2 = 0.999
ADAM_EPS = 1e-08
ADAM_WD = 0.01
ADAM_STEP = 10

PERM_TOKENS = 256
ROWS_MATMUL = 512
HALO = 16
LANES = 128
SUBLANES = 8
VMEM_LIMIT_BYTES = 56 * 1024 * 1024


def _params(*sem):
    return pltpu.CompilerParams(dimension_semantics=sem, vmem_limit_bytes=VMEM_LIMIT_BYTES)


def _pick_tile(n, cap):
    if n <= cap:
        return n
    best = None
    for t in range(LANES, cap + 1, LANES):
        if n % t == 0:
            best = t
    assert best is not None, (n, cap)
    return best


def _rows(tm, c, j=0):
    return pl.BlockSpec((tm, c), lambda m: (m, j))


def _prev_halo(tm, c):
    return pl.BlockSpec((HALO, c), lambda m: (jnp.maximum(m * (tm // HALO) - 1, 0), 0))


def _next_halo(tm, c, t_total):
    last = t_total // HALO - 1
    return pl.BlockSpec((HALO, c), lambda m: (jnp.minimum((m + 1) * (tm // HALO), last), 0))


def _resident(shape):
    nd = len(shape)
    return pl.BlockSpec(shape, lambda *_: (0,) * nd, pipeline_mode=pl.Buffered(1))


def _acc_spec(c):
    return pl.BlockSpec((SUBLANES, c), lambda *_: (0, 0))


def _shift_down(u, halo, k):
    edge = jnp.concatenate([halo[HALO - SUBLANES:], u[:SUBLANES]], axis=0)
    head = pltpu.roll(edge, k, 0)[SUBLANES:]
    return jnp.concatenate([head, pltpu.roll(u, k, 0)[SUBLANES:]], axis=0)


def _shift_up(u, halo, k):
    n = u.shape[0]
    edge = jnp.concatenate([u[n - SUBLANES:], halo[:SUBLANES]], axis=0)
    tail = pltpu.roll(edge, 2 * SUBLANES - k, 0)[:SUBLANES]
    return jnp.concatenate([pltpu.roll(u, n - k, 0)[:n - SUBLANES], tail], axis=0)


def _interleave(tm, inverse=False):
    return _perm(tm // SUBLANES, tm, inverse)


def _edge_groups(u, halo, k, from_end):
    n = u.shape[0]
    sub = lax.broadcasted_iota(jnp.int32, (SUBLANES, u.shape[1]), 0)
    out = []
    for j in range(2 - k, 2):
        lo = n - HALO + j * SUBLANES if from_end else j * SUBLANES
        own, other = u[lo:lo + SUBLANES], halo[j * SUBLANES:(j + 1) * SUBLANES]
        if from_end:
            out.append(pltpu.roll(jnp.where(sub == SUBLANES - 1, other, own), 1, 0))
        else:
            out.append(pltpu.roll(jnp.where(sub == 0, other, own), SUBLANES - 1, 0))
    return out


def _shift_down_il(u, halo, k):
    return jnp.concatenate(_edge_groups(u, halo, k, True) + [u[:u.shape[0] - k * SUBLANES]], axis=0)


def _shift_up_il(u, halo, k):
    if k == 1:
        edge = _edge_groups(u, halo, 2, False)[:1]
    else:
        edge = _edge_groups(u, halo, 2, False)
    return jnp.concatenate([u[k * SUBLANES:]] + edge, axis=0)


def _stack_rows(rows, c):
    idx = lax.broadcasted_iota(jnp.int32, (SUBLANES, c), 0)
    out = jnp.zeros((SUBLANES, c), F32)
    for i, r in enumerate(rows):
        out = out + jnp.where(idx == i, r, 0.0)
    return out


def _colsum(v):
    return jnp.sum(v, axis=0, keepdims=True)


def _sigmoid(v):
    return 0.5 * jnp.tanh(0.5 * v) + 0.5


def _rms_fwd(xv, g):
    r = lax.rsqrt(jnp.mean(xv * xv, axis=-1, keepdims=True) + RMS_EPS)
    return xv * r * g, r


def _rms_bwd(xv, g, dy):
    r = lax.rsqrt(jnp.mean(xv * xv, axis=-1, keepdims=True) + RMS_EPS)
    xn = xv * r
    dxn = dy * g
    dx = r * (dxn - xn * jnp.mean(dxn * xn, axis=-1, keepdims=True))
    return dx, dy * xn


def _dot(a, b):
    return jnp.dot(a, b, preferred_element_type=F32)


def _dot_nt(a, b):
    return lax.dot_general(a, b, (((1,), (1,)), ((), ())), preferred_element_type=F32)


def _dot_tn(a, b):
    return lax.dot_general(a, b, (((0,), (0,)), ((), ())), preferred_element_type=F32)


def _perm(dil, n, inverse=False):
    i = lax.broadcasted_iota(jnp.int32, (n, n), 0)
    j = lax.broadcasted_iota(jnp.int32, (n, n), 1)
    if inverse:
        i, j = j, i
    per = n // dil
    return (j == (i % per) * dil + i // per).astype(BF16)


def _permute_rows(pm, v):
    if v.dtype == BF16:
        return _dot(pm, v).astype(BF16)
    hi = v.astype(BF16)
    lo = (v - hi.astype(F32)).astype(BF16)
    return _dot(pm, hi) + _dot(pm, lo)


def _stream_view(a, dil):
    t, c = a.shape
    return a.reshape(dil, t // dil, c)


def _stream_spec(dil, tm, c):
    return pl.BlockSpec((dil, tm // dil, c), lambda m: (0, m, 0))


def _load_streams(ref, dil, tm):
    c = ref.shape[-1]
    if dil == 1:
        return ref[...].reshape(tm, c)
    sub = min(PERM_TOKENS, tm)
    pm = _perm(dil, sub, inverse=True)
    parts = [_permute_rows(pm, ref[:, i * (sub // dil):(i + 1) * (sub // dil), :].reshape(sub, c))
             for i in range(tm // sub)]
    return parts[0] if len(parts) == 1 else jnp.concatenate(parts, axis=0)


def _store_streams(ref, dil, tm, v):
    if dil == 1:
        ref[...] = v.reshape(ref.shape).astype(ref.dtype)
        return
    sub = min(PERM_TOKENS, tm)
    pm = _perm(dil, sub)
    for i in range(tm // sub):
        piece = _permute_rows(pm, v[i * sub:(i + 1) * sub])
        ref[:, i * (sub // dil):(i + 1) * (sub // dil), :] = piece.reshape(dil, sub // dil, -1).astype(ref.dtype)


ANY = pl.BlockSpec(memory_space=pl.ANY)


def _mesh_pos():
    return lax.axis_index("x"), lax.axis_index("y"), lax.axis_index("c")


def _dev_index(px, py, pc):
    return 4 * px + 2 * py + pc


class _Exchange:
    def __init__(self, mode, arrays, rows=None, into=()):
        self.mode, self.arrays, self.rows, self.into = mode, list(arrays), rows, list(into)
        n = len(self.arrays)
        if mode == "gather":
            self.out_shape = [jax.ShapeDtypeStruct((N_DEV,) + a.shape, a.dtype) for a in self.arrays]
        else:
            self.out_shape = [jax.ShapeDtypeStruct(a.shape, a.dtype) for a in self.arrays]
        self.scratch = [pltpu.SemaphoreType.DMA((n, N_DEV - 1)), pltpu.SemaphoreType.DMA((n, N_DEV - 1)),
                        pltpu.SemaphoreType.DMA((n,))]

    def _peers(self):
        x, y, c = _mesh_pos()
        flips = [(kx, ky, kc) for kx in (0, 1) for ky in (0, 1) for kc in (0, 1)][1:]
        peers = [(1 - x if kx else x, 1 - y if ky else y, 1 - c if kc else c) for kx, ky, kc in flips]
        return _dev_index(x, y, c), peers

    def _copy(self, ins, outs, sems, i, k, peer, me, sending):
        src = ins[i] if self.mode == "gather" else ins[i].at[_dev_index(*peer)]
        dst = outs[i].at[me if sending else _dev_index(*peer)]
        if self.rows is not None:
            src, dst = src.at[pl.ds(*self.rows)], dst.at[pl.ds(*self.rows)]
        return pltpu.make_async_remote_copy(src_ref=src, dst_ref=dst, send_sem=sems[0].at[i, k],
                                            recv_sem=sems[1].at[i, k], device_id=peer, device_id_type=MESH)

    def _own(self, ins, outs, sems, i, me):
        return pltpu.make_async_copy(ins[i], outs[i].at[me], sems[2].at[i])

    def start(self, ins, outs, sems):
        me, peers = self._peers()
        for i in range(len(ins)):
            if self.mode == "gather":
                self._own(ins, outs, sems, i, me).start()
            for k, peer in enumerate(peers):
                self._copy(ins, outs, sems, i, k, peer, me, True).start()

    def wait(self, ins, outs, sems):
        me, peers = self._peers()
        for i in range(len(ins)):
            for k, peer in enumerate(peers):
                self._copy(ins, outs, sems, i, k, peer, me, False).wait_recv()
            for k, peer in enumerate(peers):
                self._copy(ins, outs, sems, i, k, peer, me, True).wait_send()
            if self.mode == "gather":
                self._own(ins, outs, sems, i, me).wait()


def _call(body, *, name, grid, in_specs, out_specs, out_shape, args, semantics, carry=None, scratch=()):
    if carry is None:
        return pl.pallas_call(body, name=name, grid=grid, in_specs=in_specs, out_specs=out_specs,
                              out_shape=out_shape, scratch_shapes=list(scratch),
                              compiler_params=_params(*semantics))(*args)
    n_in, n_out, n_x, n_s = len(in_specs), len(out_specs), len(carry.arrays), len(scratch)
    n_into = len(carry.into)
    all_in = n_in + n_x + n_into

    def carried(*refs):
        ins, x_ins = refs[:n_in], refs[n_in:n_in + n_x]
        outs = refs[all_in:all_in + n_out]
        x_outs = refs[all_in + n_out:all_in + n_out + n_x]
        own = refs[all_in + n_out + n_x:all_in + n_out + n_x + n_s]
        sems = refs[all_in + n_out + n_x + n_s:]
        first = functools.reduce(jnp.logical_and, [pl.program_id(a) == 0 for a in range(len(grid))])
        last = functools.reduce(jnp.logical_and, [pl.program_id(a) == grid[a] - 1 for a in range(len(grid))])

        @pl.when(first)
        def _():
            carry.start(x_ins, x_outs, sems)

        body(*ins, *outs, *own)

        @pl.when(last)
        def _():
            carry.wait(x_ins, x_outs, sems)

    res = pl.pallas_call(
        carried, name=name, grid=grid, in_specs=list(in_specs) + [ANY] * (n_x + n_into),
        out_specs=list(out_specs) + [ANY] * n_x, out_shape=list(out_shape) + carry.out_shape,
        input_output_aliases={n_in + n_x + i: n_out + i for i in range(n_into)},
        scratch_shapes=list(scratch) + carry.scratch, compiler_params=_params(*["arbitrary"] * len(grid)),
    )(*args, *carry.arrays, *carry.into)
    return list(res[:n_out]), list(res[n_out:])


def _in_proj(x, g, wt, cw, carry=None):
    t, d = x.shape
    n = wt.shape[0]
    tm = min(ROWS_MATMUL, t)
    qkv0 = 3 * cw

    def body(x_ref, g_ref, wt_ref, h_ref, abcv_ref, gates_ref, *s_refs):
        h = _rms_fwd(x_ref[...], g_ref[...])[0].astype(BF16)
        h_ref[...] = h
        abcv_ref[...] = _dot_nt(h, wt_ref[0:qkv0, :]).astype(BF16)
        gates_ref[...] = _dot_nt(h, wt_ref[qkv0 + 3 * ATTN_W:n, :]).astype(BF16)
        qkv = _dot_nt(h, wt_ref[qkv0:qkv0 + 3 * ATTN_W, :]).astype(BF16)
        for gi, s_ref in enumerate(s_refs):
            cols = [qkv[:, j * ATTN_W + gi * GROUP_W:j * ATTN_W + (gi + 1) * GROUP_W] for j in range(3)]
            _store_streams(s_ref, DILATIONS[gi], tm, jnp.concatenate(cols, axis=1))

    return _call(
        body, name="in_proj", grid=(t // tm,),
        in_specs=[_rows(tm, d), _resident((1, d)), _resident((n, d))],
        out_specs=[_rows(tm, d), _rows(tm, qkv0), _rows(tm, 2 * d)]
        + [_stream_spec(dil, tm, 3 * GROUP_W) for dil in DILATIONS],
        out_shape=[jax.ShapeDtypeStruct((t, d), BF16), jax.ShapeDtypeStruct((t, qkv0), BF16),
                   jax.ShapeDtypeStruct((t, 2 * d), BF16)]
        + [jax.ShapeDtypeStruct((dil, t // dil, 3 * GROUP_W), BF16) for dil in DILATIONS],
        args=(x, g, wt), semantics=("parallel",), carry=carry)


def _head_masks():
    lane = lax.broadcasted_iota(jnp.int32, (1, GROUP_W), 1)
    return lane, [(lane // HEAD_DIM) == h for h in range(HEADS_PER_GROUP)]


def _stack_heads(v, heads):
    return jnp.concatenate([jnp.where(hm, v, jnp.zeros_like(v)) for hm in heads], axis=0)


def _merge_heads(v, heads):
    out = jnp.zeros((QBLK, GROUP_W), v.dtype)
    for h, hm in enumerate(heads):
        out = jnp.where(hm, v[h * QBLK:(h + 1) * QBLK], out)
    return out


def _pair_block(col, count=STEP_BLOCKS):
    return pl.BlockSpec((count * QBLK, GROUP_W), lambda b: (b, col))


def _edge_block(col, shift, nb, count=STEP_BLOCKS):
    return pl.BlockSpec((QBLK, GROUP_W), lambda b: (jnp.clip(count * b + shift, 0, nb - 1), col))


def _band_mask(has_prev):
    rows = HEADS_PER_GROUP * QBLK
    row = lax.broadcasted_iota(jnp.int32, (rows, 2 * QBLK), 0) & (QBLK - 1)
    col = lax.broadcasted_iota(jnp.int32, (rows, 2 * QBLK), 1)
    return ((col < QBLK) & (col >= row) & has_prev) | ((col >= QBLK) & (col - QBLK <= row))


def _next_mask(has_next):
    rows = HEADS_PER_GROUP * QBLK
    row = lax.broadcasted_iota(jnp.int32, (rows, QBLK), 0) & (QBLK - 1)
    col = lax.broadcasted_iota(jnp.int32, (rows, QBLK), 1)
    return (col >= row) & has_next


def _attn_fwd(s, dil, carry=None):
    t = s.shape[0] * s.shape[1]
    nb = t // QBLK
    per_stream = nb // dil
    assert per_stream % STEP_BLOCKS == 0

    def body(q_ref, kc_ref, kp_ref, vc_ref, vp_ref, o_ref, lse_ref):
        b = pl.program_id(0)
        _, heads = _head_masks()
        first_has_prev = lax.rem(STEP_BLOCKS * b, per_stream) != 0
        for j in range(STEP_BLOCKS):
            rows = slice(j * QBLK, (j + 1) * QBLK)
            if j == 0:
                k2 = jnp.concatenate([kp_ref[...], kc_ref[rows, :]], axis=0)
                v2 = jnp.concatenate([vp_ref[...], vc_ref[rows, :]], axis=0)
            else:
                both = slice((j - 1) * QBLK, (j + 1) * QBLK)
                k2, v2 = kc_ref[both, :], vc_ref[both, :]
            mask = _band_mask(first_has_prev if j == 0 else True)
            sc = jnp.where(mask, _dot_nt(_stack_heads(q_ref[rows, :], heads), k2) * ATTN_SCALE, NEG_INF)
            mx = jnp.max(sc, axis=1, keepdims=True)
            pr = jnp.exp(sc - mx)
            den = jnp.sum(pr, axis=1, keepdims=True)
            o_all = _dot(pr.astype(BF16), v2) / den
            o_ref[rows, :] = _merge_heads(o_all, heads).astype(BF16)
            lse_ref[rows, :] = _merge_heads(jnp.broadcast_to(mx + jnp.log(den), o_all.shape), heads)

    sv = s.reshape(t, 3 * GROUP_W)
    return _call(
        body, name=f"attn_fwd_d{dil}", grid=(nb // STEP_BLOCKS,),
        in_specs=[_pair_block(0), _pair_block(1), _edge_block(1, -1, nb), _pair_block(2), _edge_block(2, -1, nb)],
        out_specs=[_pair_block(0), _pair_block(0)],
        out_shape=[jax.ShapeDtypeStruct((t, GROUP_W), BF16), jax.ShapeDtypeStruct((t, GROUP_W), F32)],
        args=(sv, sv, sv, sv, sv), semantics=("parallel",), carry=carry)


def _group_softmax(parts):
    mx = jnp.maximum(jnp.maximum(parts[0], parts[1]), parts[2])
    es = [jnp.exp(p - mx) for p in parts]
    den = es[0] + es[1] + es[2]
    return [e / den for e in es]


def _mixer_out(x, abcv, gates, os, lses, conv_w, conv_b, b_gate, w_pa, w_pb, w_o, carry=None):
    t, d = x.shape
    cw = conv_w.shape[1]
    tm = min(ROWS_MATMUL, t)

    def body(x_ref, abcv_ref, halo_ref, gates_ref, o0_ref, o1_ref, o2_ref, l0_ref, l1_ref, l2_ref, cw_ref, cb_ref,
             bg_ref, wpa_ref, wpb_ref, wo_ref, x1_ref, ya_ref, yb_ref, yap_ref, ybp_ref, mg_ref):
        m = pl.program_id(0)
        ab = abcv_ref[:, 0:cw].astype(F32)
        u = abcv_ref[:, cw:2 * cw].astype(F32) * abcv_ref[:, 2 * cw:3 * cw].astype(F32)
        hu = halo_ref[:, cw:2 * cw].astype(F32) * halo_ref[:, 2 * cw:3 * cw].astype(F32)
        hu = jnp.where(m > 0, hu, 0.0)
        cv = (cw_ref[0:1, :] * _shift_down(u, hu, 2) + cw_ref[1:2, :] * _shift_down(u, hu, 1)
              + cw_ref[2:3, :] * u + cb_ref[...])
        ya = (ab * cv).astype(BF16)
        ya_ref[...] = ya
        alphas = _group_softmax([_load_streams(r, dil, tm) for r, dil in zip((l0_ref, l1_ref, l2_ref), DILATIONS)])
        for i, (o_ref, dil) in enumerate(zip((o0_ref, o1_ref, o2_ref), DILATIONS)):
            sl = slice(i * GROUP_W, (i + 1) * GROUP_W)
            yb_ref[:, sl] = (alphas[i] * _load_streams(o_ref, dil, tm).astype(F32)).astype(BF16)
        yap = _dot_nt(ya, wpa_ref[...])
        ybp = _dot_nt(yb_ref[...], wpb_ref[...])
        yap_ref[...] = yap.astype(BF16)
        ybp_ref[...] = ybp.astype(BF16)
        sa = _sigmoid(gates_ref[:, 0:d].astype(F32) + bg_ref[0:1, :])
        sb = _sigmoid(gates_ref[:, d:2 * d].astype(F32) + bg_ref[1:2, :])
        merged = (sa * yap + sb * ybp).astype(BF16)
        mg_ref[...] = merged
        x1_ref[...] = x_ref[...] + _dot(merged, wo_ref[...])

    return _call(
        body, name="mixer_out", grid=(t // tm,),
        in_specs=[_rows(tm, d), _rows(tm, 3 * cw), _prev_halo(tm, 3 * cw), _rows(tm, 2 * d)]
        + [_stream_spec(dil, tm, GROUP_W) for dil in DILATIONS] * 2
        + [_resident((3, cw)), _resident((1, cw)), _resident((2, d)),
           _resident((d, cw)), _resident((d, ATTN_W)), _resident((d, d))],
        out_specs=[_rows(tm, d), _rows(tm, cw), _rows(tm, ATTN_W), _rows(tm, d), _rows(tm, d), _rows(tm, d)],
        out_shape=[jax.ShapeDtypeStruct((t, d), F32), jax.ShapeDtypeStruct((t, cw), BF16),
                   jax.ShapeDtypeStruct((t, ATTN_W), BF16), jax.ShapeDtypeStruct((t, d), BF16),
                   jax.ShapeDtypeStruct((t, d), BF16), jax.ShapeDtypeStruct((t, d), BF16)],
        args=(x, abcv, abcv, gates, *[_stream_view(a, dil) for a, dil in zip(os, DILATIONS)],
              *[_stream_view(a, dil) for a, dil in zip(lses, DILATIONS)], conv_w, conv_b, b_gate, w_pa, w_pb, w_o),
        semantics=("parallel",), carry=carry)


def _ffn_fwd(x1, target, g2, w_ut, conv_w, conv_b, w_d, g_f, carry=None):
    t, d = x1.shape
    dff = w_d.shape[0]
    tm = min(256, t)
    ck = _pick_tile(dff, 2816)

    def body(x1_ref, tg_ref, g2_ref, wut_ref, cw_ref, cb_ref, wd_ref, gf_ref, h2_ref, up_ref, act_ref, conv_ref,
             dx2_ref, dx2i_ref, acc_ref, loss_ref, halo_ref):
        m = pl.program_id(0)

        @pl.when(m == 0)
        def _():
            acc_ref[...] = jnp.zeros_like(acc_ref)
            loss_ref[...] = jnp.zeros_like(loss_ref)
            halo_ref[...] = jnp.zeros_like(halo_ref)

        h2 = _permute_rows(_interleave(tm), _rms_fwd(x1_ref[...], g2_ref[...])[0].astype(BF16))
        h2_ref[...] = h2

        def conv(c0):
            p = _dot_nt(h2, wut_ref[c0:c0 + ck, :])
            up_ref[:, c0:c0 + ck] = p.astype(BF16)
            hp = halo_ref[:, c0:c0 + ck]
            halo_ref[:, c0:c0 + ck] = p[tm - HALO:, :]
            return (cw_ref[0:1, c0:c0 + ck] * _shift_down_il(p, hp, 2)
                    + cw_ref[1:2, c0:c0 + ck] * _shift_down_il(p, hp, 1)
                    + cw_ref[2:3, c0:c0 + ck] * p + cb_ref[:, c0:c0 + ck])

        down = jnp.zeros((tm, d), F32)
        for c0 in range(0, dff, ck):
            gate = conv(c0)
            val = conv(dff + c0)
            conv_ref[:, c0:c0 + ck] = gate.astype(BF16)
            conv_ref[:, dff + c0:dff + c0 + ck] = val.astype(BF16)
            act = (gate * _sigmoid(gate) * val).astype(BF16)
            act_ref[:, c0:c0 + ck] = act
            down = down + _dot(act, wd_ref[c0:c0 + ck, :])
        x2 = x1_ref[...] + _permute_rows(_interleave(tm, inverse=True), down)
        y, _ = _rms_fwd(x2, gf_ref[...])
        diff = y - tg_ref[...]
        loss_ref[...] += 0.5 * jnp.sum(jnp.mean(diff * diff, axis=-1, keepdims=True))
        dx2, dg = _rms_bwd(x2, gf_ref[...], diff * (1.0 / d))
        dx2_ref[...] = dx2
        dx2i_ref[...] = _permute_rows(_interleave(tm), dx2.astype(BF16))
        acc_ref[...] += _stack_rows([_colsum(dg)], d)

    return _call(
        body, name="ffn_fwd", grid=(t // tm,),
        in_specs=[_rows(tm, d), _rows(tm, d), _resident((1, d)), _resident((2 * dff, d)), _resident((3, 2 * dff)),
                  _resident((1, 2 * dff)), _resident((dff, d)), _resident((1, d))],
        out_specs=[_rows(tm, d), _rows(tm, 2 * dff), _rows(tm, dff), _rows(tm, 2 * dff), _rows(tm, d), _rows(tm, d),
                   _acc_spec(d), _acc_spec(LANES)],
        out_shape=[jax.ShapeDtypeStruct((t, d), BF16), jax.ShapeDtypeStruct((t, 2 * dff), BF16),
                   jax.ShapeDtypeStruct((t, dff), BF16), jax.ShapeDtypeStruct((t, 2 * dff), BF16),
                   jax.ShapeDtypeStruct((t, d), F32), jax.ShapeDtypeStruct((t, d), BF16),
                   jax.ShapeDtypeStruct((SUBLANES, d), F32), jax.ShapeDtypeStruct((SUBLANES, LANES), F32)],
        args=(x1, target, g2, w_ut, conv_w, conv_b, w_d, g_f), semantics=("arbitrary",), carry=carry,
        scratch=[pltpu.VMEM((HALO, 2 * dff), F32)])


def _ffn_act_bwd(dx2, conv, w_d):
    t, d = dx2.shape
    dff = w_d.shape[0]
    tm = min(256, t)
    ck = _pick_tile(dff, 2816)

    def body(dx2_ref, conv_ref, wd_ref, dup_ref, acc_ref):
        m = pl.program_id(0)

        @pl.when(m == 0)
        def _():
            acc_ref[...] = jnp.zeros_like(acc_ref)

        dx2v = dx2_ref[...]
        for c0 in range(0, dff, ck):
            dact = _dot_nt(dx2v, wd_ref[c0:c0 + ck, :])
            gate = conv_ref[:, c0:c0 + ck].astype(F32)
            val = conv_ref[:, dff + c0:dff + c0 + ck].astype(F32)
            sg = _sigmoid(gate)
            dval = dact * gate * sg
            dgate = dact * val * sg * (1.0 + gate * (1.0 - sg))
            dup_ref[:, c0:c0 + ck] = dgate.astype(BF16)
            dup_ref[:, dff + c0:dff + c0 + ck] = dval.astype(BF16)
            acc_ref[:, c0:c0 + ck] += _stack_rows([_colsum(dgate)], ck)
            acc_ref[:, dff + c0:dff + c0 + ck] += _stack_rows([_colsum(dval)], ck)

    return pl.pallas_call(
        body, name="ffn_act_bwd", grid=(t // tm,),
        in_specs=[_rows(tm, d), _rows(tm, 2 * dff), _resident((dff, d))],
        out_specs=[_rows(tm, 2 * dff), _acc_spec(2 * dff)],
        out_shape=[jax.ShapeDtypeStruct((t, 2 * dff), BF16), jax.ShapeDtypeStruct((SUBLANES, 2 * dff), F32)],
        compiler_params=_params("arbitrary"),
    )(dx2, conv, w_d)


def _ffn_up_bwd(dup, up_pre, x1, dx2, conv_w, w_u, g2, carry=None):
    t, d = x1.shape
    n = dup.shape[1]
    tm = min(256, t)
    ck = _pick_tile(n, 256)
    last = t // tm - 1

    def body(dup_ref, nxt_ref, up_ref, x1_ref, dx2_ref, cw_ref, wu_ref, g2_ref, dpre_ref, dx1_ref, acc_ref, accw_ref):
        m = pl.program_id(0)

        @pl.when(m == 0)
        def _():
            acc_ref[...] = jnp.zeros_like(acc_ref)
            accw_ref[...] = jnp.zeros_like(accw_ref)

        dh = jnp.zeros((tm, d), F32)
        for c0 in range(0, n, ck):
            du = dup_ref[:, c0:c0 + ck].astype(F32)
            hn = jnp.where(m < last, nxt_ref[:, c0:c0 + ck].astype(F32), 0.0)
            du1 = _shift_up_il(du, hn, 1)
            du2 = _shift_up_il(du, hn, 2)
            dpre = (cw_ref[2:3, c0:c0 + ck] * du + cw_ref[1:2, c0:c0 + ck] * du1
                    + cw_ref[0:1, c0:c0 + ck] * du2).astype(BF16)
            dpre_ref[:, c0:c0 + ck] = dpre
            dh = dh + _dot(dpre, wu_ref[c0:c0 + ck, :])
            p = up_ref[:, c0:c0 + ck].astype(F32)
            accw_ref[:, c0:c0 + ck] += _stack_rows([_colsum(du2 * p), _colsum(du1 * p), _colsum(du * p)], ck)
        dh = _permute_rows(_interleave(tm, inverse=True), dh)
        dx, dg = _rms_bwd(x1_ref[...], g2_ref[...], dh)
        dx1_ref[...] = dx2_ref[...] + dx
        acc_ref[...] += _stack_rows([_colsum(dg)], d)

    return _call(
        body, name="ffn_up_bwd", grid=(t // tm,),
        in_specs=[_rows(tm, n), _next_halo(tm, n, t), _rows(tm, n), _rows(tm, d), _rows(tm, d), _resident((3, n)),
                  _resident((n, d)), _resident((1, d))],
        out_specs=[_rows(tm, n), _rows(tm, d), _acc_spec(d), _acc_spec(n)],
        out_shape=[jax.ShapeDtypeStruct((t, n), BF16), jax.ShapeDtypeStruct((t, d), F32),
                   jax.ShapeDtypeStruct((SUBLANES, d), F32), jax.ShapeDtypeStruct((SUBLANES, n), F32)],
        args=(dup, dup, up_pre, x1, dx2, conv_w, w_u, g2), semantics=("arbitrary",), carry=carry)


def _tn_matmul(a, b, name):
    t, mdim = a.shape
    n = b.shape[1]
    tk = min(1024, t)
    tmm = _pick_tile(mdim, 1536)
    tn = _pick_tile(n, 1024)

    def body(a_ref, b_ref, o_ref, acc_ref):
        k = pl.program_id(2)

        @pl.when(k == 0)
        def _():
            acc_ref[...] = jnp.zeros_like(acc_ref)

        acc_ref[...] += _dot_tn(a_ref[...].astype(BF16), b_ref[...].astype(BF16))

        @pl.when(k == t // tk - 1)
        def _():
            o_ref[...] = acc_ref[...].astype(BF16)

    return pl.pallas_call(
        body, name=name, grid=(mdim // tmm, n // tn, t // tk),
        in_specs=[pl.BlockSpec((tk, tmm), lambda i, j, k: (k, i)), pl.BlockSpec((tk, tn), lambda i, j, k: (k, j))],
        out_specs=pl.BlockSpec((tmm, tn), lambda i, j, k: (i, j)),
        out_shape=jax.ShapeDtypeStruct((mdim, n), BF16),
        scratch_shapes=[pltpu.VMEM((tmm, tn), F32)],
        compiler_params=_params("parallel", "parallel", "arbitrary"),
    )(a, b)


def _mixer_bwd(dx1, gates, yap, ybp, os, lses, b_gate, w_o, w_pa, w_pb):
    t, d = dx1.shape
    cw = w_pa.shape[1]
    tm = min(ROWS_MATMUL, t)

    def body(dx1_ref, gates_ref, yap_ref, ybp_ref, o0_ref, o1_ref, o2_ref, l0_ref, l1_ref, l2_ref, bg_ref, wo_ref,
             wpa_ref, wpb_ref, dgates_ref, dyap_ref, dybp_ref, dya_ref, do0_ref, do1_ref, do2_ref, dl0_ref, dl1_ref,
             dl2_ref, acc_ref):
        m = pl.program_id(0)

        @pl.when(m == 0)
        def _():
            acc_ref[...] = jnp.zeros_like(acc_ref)

        dmg = _dot_nt(dx1_ref[...].astype(BF16), wo_ref[...])
        sa = _sigmoid(gates_ref[:, 0:d].astype(F32) + bg_ref[0:1, :])
        sb = _sigmoid(gates_ref[:, d:2 * d].astype(F32) + bg_ref[1:2, :])
        dyap = (dmg * sa).astype(BF16)
        dybp = (dmg * sb).astype(BF16)
        dga = dmg * yap_ref[...].astype(F32) * sa * (1.0 - sa)
        dgb = dmg * ybp_ref[...].astype(F32) * sb * (1.0 - sb)
        dyap_ref[...] = dyap
        dybp_ref[...] = dybp
        dgates_ref[:, 0:d] = dga.astype(BF16)
        dgates_ref[:, d:2 * d] = dgb.astype(BF16)
        acc_ref[...] += _stack_rows([_colsum(dga), _colsum(dgb)], d)
        dya_ref[...] = _dot(dyap, wpa_ref[...]).astype(BF16)
        dyb = _dot(dybp, wpb_ref[...])

        ri = lax.broadcasted_iota(jnp.int32, (GROUP_W, GROUP_W), 0) // HEAD_DIM
        ci = lax.broadcasted_iota(jnp.int32, (GROUP_W, GROUP_W), 1) // HEAD_DIM
        same_head = (ri == ci).astype(BF16)
        alphas = _group_softmax([_load_streams(r, dil, tm) for r, dil in zip((l0_ref, l1_ref, l2_ref), DILATIONS)])
        prod = jnp.zeros((tm, GROUP_W), F32)
        for i, (o_ref, do_ref, dil) in enumerate(zip((o0_ref, o1_ref, o2_ref), (do0_ref, do1_ref, do2_ref), DILATIONS)):
            dov = alphas[i] * dyb[:, i * GROUP_W:(i + 1) * GROUP_W]
            _store_streams(do_ref, dil, tm, dov.astype(BF16))
            prod = prod + dov * _load_streams(o_ref, dil, tm).astype(F32)
        hi = prod.astype(BF16)
        lo = (prod - hi.astype(F32)).astype(BF16)
        dtot = _dot(hi, same_head) + _dot(lo, same_head)
        for alpha, dl_ref, dil in zip(alphas, (dl0_ref, dl1_ref, dl2_ref), DILATIONS):
            _store_streams(dl_ref, dil, tm, alpha * dtot)

    streams = [_stream_spec(dil, tm, GROUP_W) for dil in DILATIONS]
    res = _call(
        body, name="mixer_bwd", grid=(t // tm,),
        in_specs=[_rows(tm, d), _rows(tm, 2 * d), _rows(tm, d), _rows(tm, d)] + streams * 2
        + [_resident((2, d)), _resident((d, d)), _resident((d, cw)), _resident((d, ATTN_W))],
        out_specs=[_rows(tm, 2 * d), _rows(tm, d), _rows(tm, d), _rows(tm, cw)] + streams * 2 + [_acc_spec(d)],
        out_shape=[jax.ShapeDtypeStruct((t, 2 * d), BF16), jax.ShapeDtypeStruct((t, d), BF16),
                   jax.ShapeDtypeStruct((t, d), BF16), jax.ShapeDtypeStruct((t, cw), BF16)]
        + [jax.ShapeDtypeStruct((dil, t // dil, GROUP_W), BF16) for dil in DILATIONS]
        + [jax.ShapeDtypeStruct((dil, t // dil, GROUP_W), F32) for dil in DILATIONS]
        + [jax.ShapeDtypeStruct((SUBLANES, d), F32)],
        args=(dx1, gates, yap, ybp, *[_stream_view(a, dil) for a, dil in zip(os, DILATIONS)],
              *[_stream_view(a, dil) for a, dil in zip(lses, DILATIONS)], b_gate, w_o, w_pa, w_pb),
        semantics=("arbitrary",))
    dgates, dyap, dybp, dya = res[:4]
    dos = [a.reshape(t, GROUP_W) for a in res[4:7]]
    dls = [a.reshape(t, GROUP_W) for a in res[7:10]]
    return dgates, dyap, dybp, dya, dos, dls, res[10]


def _attn_bwd(s, do, lse, dl, dil, carry=None):
    t = s.shape[0] * s.shape[1]
    nb = t // QBLK
    per_stream = nb // dil
    count = BWD_STEP_BLOCKS
    assert per_stream % count == 0

    def body(q_ref, qn_ref, kc_ref, kp_ref, vc_ref, vp_ref, do_ref, don_ref, lse_ref, lsen_ref, dl_ref, dln_ref,
             ds_ref):
        b = pl.program_id(0)
        lane, heads = _head_masks()
        first_has_prev = lax.rem(count * b, per_stream) != 0
        last_has_next = lax.rem(count * (b + 1), per_stream) != 0

        def cols(v):
            return jnp.concatenate([jnp.sum(jnp.where(lane == h * HEAD_DIM, v, 0.0), axis=1, keepdims=True)
                                    for h in range(HEADS_PER_GROUP)], axis=0)

        def pair(qs, dos, k, v, valid, lse_c, dl_c):
            s = jnp.where(valid, _dot_nt(qs, k) * ATTN_SCALE, NEG_INF)
            p = jnp.exp(s - lse_c)
            ds = p * (_dot_nt(dos, v) - dl_c)
            return p.astype(BF16), ds.astype(BF16)

        for j in range(count):
            rows, hi = slice(j * QBLK, (j + 1) * QBLK), slice((j + 1) * QBLK, (j + 2) * QBLK)
            q, do, lse, dl = q_ref[rows, :], do_ref[rows, :], lse_ref[rows, :], dl_ref[rows, :]
            kc, vc = kc_ref[rows, :], vc_ref[rows, :]
            if j == 0:
                k2 = jnp.concatenate([kp_ref[...], kc], axis=0)
                v2 = jnp.concatenate([vp_ref[...], vc], axis=0)
                mask = _band_mask(first_has_prev)
            else:
                both = slice((j - 1) * QBLK, (j + 1) * QBLK)
                k2, v2, mask = kc_ref[both, :], vc_ref[both, :], _band_mask(True)
            if j < count - 1:
                qn, don, lsen, dln = q_ref[hi, :], do_ref[hi, :], lse_ref[hi, :], dl_ref[hi, :]
                mask_n = _next_mask(True)
            else:
                qn, don, lsen, dln = qn_ref[...], don_ref[...], lsen_ref[...], dln_ref[...]
                mask_n = _next_mask(last_has_next)
            qs, qns = _stack_heads(q, heads), _stack_heads(qn, heads)
            dos, dons = _stack_heads(do, heads), _stack_heads(don, heads)
            p_q, ds_q = pair(qs, dos, k2, v2, mask, cols(lse), cols(dl))
            p_n, ds_n = pair(qns, dons, kc, vc, mask_n, cols(lsen), cols(dln))
            dq = _merge_heads(_dot(ds_q, k2), heads)
            dk = _dot_tn(jnp.concatenate([ds_q[:, QBLK:], ds_n], axis=0), jnp.concatenate([qs, qns], axis=0))
            dv = _dot_tn(jnp.concatenate([p_q[:, QBLK:], p_n], axis=0), jnp.concatenate([dos, dons], axis=0))
            ds_ref[rows, 0:GROUP_W] = (dq * ATTN_SCALE).astype(BF16)
            ds_ref[rows, GROUP_W:2 * GROUP_W] = (dk * ATTN_SCALE).astype(BF16)
            ds_ref[rows, 2 * GROUP_W:3 * GROUP_W] = dv.astype(BF16)

    sv = s.reshape(t, 3 * GROUP_W)
    cur, nxt = _pair_block(0, count), _edge_block(0, count, nb, count)
    return _call(
        body, name=f"attn_bwd_d{dil}", grid=(nb // count,),
        in_specs=[cur, nxt, _pair_block(1, count), _edge_block(1, -1, nb, count), _pair_block(2, count),
                  _edge_block(2, -1, nb, count), cur, nxt, cur, nxt, cur, nxt],
        out_specs=[pl.BlockSpec((count * QBLK, 3 * GROUP_W), lambda b: (b, 0))],
        out_shape=[jax.ShapeDtypeStruct((t, 3 * GROUP_W), BF16)],
        args=(sv, sv, sv, sv, sv, sv, do, do, lse, lse, dl, dl), semantics=("parallel",), carry=carry)


def _conv_mixer_bwd(abcv, dya, conv_w, conv_b):
    t = abcv.shape[0]
    cw = conv_w.shape[1]
    tm = min(1024, t)
    last = t // tm - 1

    def body(a_ref, ap_ref, an_ref, dya_ref, dyan_ref, cw_ref, cb_ref, d_ref, acc_ref):
        m = pl.program_id(0)

        @pl.when(m == 0)
        def _():
            acc_ref[...] = jnp.zeros_like(acc_ref)

        ab = a_ref[:, 0:cw].astype(F32)
        ac = a_ref[:, cw:2 * cw].astype(F32)
        av = a_ref[:, 2 * cw:3 * cw].astype(F32)
        u = ac * av
        hu = ap_ref[:, cw:2 * cw].astype(F32) * ap_ref[:, 2 * cw:3 * cw].astype(F32)
        hu = jnp.where(m > 0, hu, 0.0)
        u1 = _shift_down(u, hu, 1)
        u2 = _shift_down(u, hu, 2)
        cv = cw_ref[0:1, :] * u2 + cw_ref[1:2, :] * u1 + cw_ref[2:3, :] * u + cb_ref[...]
        dya_v = dya_ref[...].astype(F32)
        dcv = dya_v * ab
        ndcv = jnp.where(m < last, dyan_ref[...].astype(F32) * an_ref[:, 0:cw].astype(F32), 0.0)
        du = (cw_ref[2:3, :] * dcv + cw_ref[1:2, :] * _shift_up(dcv, ndcv, 1)
              + cw_ref[0:1, :] * _shift_up(dcv, ndcv, 2))
        d_ref[:, 0:cw] = (dya_v * cv).astype(BF16)
        d_ref[:, cw:2 * cw] = (du * av).astype(BF16)
        d_ref[:, 2 * cw:3 * cw] = (du * ac).astype(BF16)
        acc_ref[...] += _stack_rows([_colsum(dcv * u2), _colsum(dcv * u1), _colsum(dcv * u), _colsum(dcv)], cw)

    return pl.pallas_call(
        body, name="conv_mixer_bwd", grid=(t // tm,),
        in_specs=[_rows(tm, 3 * cw), _prev_halo(tm, 3 * cw), _next_halo(tm, 3 * cw, t), _rows(tm, cw),
                  _next_halo(tm, cw, t), _resident((3, cw)), _resident((1, cw))],
        out_specs=[_rows(tm, 3 * cw), _acc_spec(cw)],
        out_shape=[jax.ShapeDtypeStruct((t, 3 * cw), BF16), jax.ShapeDtypeStruct((SUBLANES, cw), F32)],
        compiler_params=_params("arbitrary"),
    )(abcv, abcv, abcv, dya, dya, conv_w, conv_b)


def _in_proj_bwd(x, dx1, dabcv, dss, dgates, w_in, g1, carry=None):
    t, d = x.shape
    qkv0 = dabcv.shape[1]
    n = w_in.shape[0]
    tm = min(ROWS_MATMUL, t)

    def body(x_ref, dx1_ref, da_ref, ds0_ref, ds1_ref, ds2_ref, dg_ref, w_ref, g_ref, dx_ref, acc_ref):
        m = pl.program_id(0)

        @pl.when(m == 0)
        def _():
            acc_ref[...] = jnp.zeros_like(acc_ref)

        dss_tok = [_load_streams(ds_ref, dil, tm) for ds_ref, dil in zip((ds0_ref, ds1_ref, ds2_ref), DILATIONS)]
        dqkv = jnp.concatenate([ds[:, j * GROUP_W:(j + 1) * GROUP_W] for j in range(3) for ds in dss_tok], axis=1)
        dh = (_dot(da_ref[...], w_ref[0:qkv0, :]) + _dot(dqkv, w_ref[qkv0:qkv0 + 3 * ATTN_W, :])
              + _dot(dg_ref[...], w_ref[qkv0 + 3 * ATTN_W:n, :]))
        dx, dg = _rms_bwd(x_ref[...], g_ref[...], dh)
        dx_ref[...] = dx1_ref[...] + dx
        acc_ref[...] += _stack_rows([_colsum(dg)], d)

    return _call(
        body, name="in_proj_bwd", grid=(t // tm,),
        in_specs=[_rows(tm, d), _rows(tm, d), _rows(tm, qkv0)]
        + [_stream_spec(dil, tm, 3 * GROUP_W) for dil in DILATIONS]
        + [_rows(tm, 2 * d), _resident((n, d)), _resident((1, d))],
        out_specs=[_rows(tm, d), _acc_spec(d)],
        out_shape=[jax.ShapeDtypeStruct((t, d), F32), jax.ShapeDtypeStruct((SUBLANES, d), F32)],
        args=(x, dx1, dabcv, *[_stream_view(a, dil) for a, dil in zip(dss, DILATIONS)], dgates, w_in, g1),
        semantics=("arbitrary",), carry=carry)


def _dw_in_qkv(ds, h, dil):
    t, d = h.shape
    tk = min(1024, t)
    sub = min(256, t)
    width = 3 * GROUP_W

    def body(ds_ref, h_ref, o_ref, acc_ref):
        k = pl.program_id(0)

        @pl.when(k == 0)
        def _():
            acc_ref[...] = jnp.zeros_like(acc_ref)

        upd = None
        for i in range(tk // sub):
            rows = ds_ref[:, i * (sub // dil):(i + 1) * (sub // dil), :].reshape(sub, width)
            if dil > 1:
                rows = _permute_rows(_perm(dil, sub, inverse=True), rows)
            term = _dot_tn(rows, h_ref[i * sub:(i + 1) * sub, :])
            upd = term if upd is None else upd + term
        acc_ref[...] += upd

        @pl.when(k == t // tk - 1)
        def _():
            o_ref[...] = acc_ref[...].astype(BF16)

    return pl.pallas_call(
        body, name=f"dw_in_qkv_d{dil}", grid=(t // tk,),
        in_specs=[_stream_spec(dil, tk, width), _rows(tk, d)],
        out_specs=pl.BlockSpec((width, d), lambda k: (0, 0)),
        out_shape=jax.ShapeDtypeStruct((width, d), BF16),
        scratch_shapes=[pltpu.VMEM((width, d), F32)],
        compiler_params=_params("arbitrary"),
    )(_stream_view(ds, dil), h)


def _local_step(x, target, p, late):
    cw = p["conv_a_w"].shape[1]
    (h, abcv, gates, *ss), (g_up,) = _in_proj(x, p["norm_mix_g"], p["w_in"], cw,
                                              carry=_Exchange("gather", [late["w_up"]]))
    w_up = _full_from_gathered(g_up)
    (o0, lse0), g_proj = _attn_fwd(ss[0], DILATIONS[0],
                                   carry=_Exchange("gather", [late["w_proj_a"], late["w_proj_b"]]))
    (o1, lse1), (g_out,) = _attn_fwd(ss[1], DILATIONS[1], carry=_Exchange("gather", [late["w_out"]]))
    o2, lse2 = _attn_fwd(ss[2], DILATIONS[2])
    w_pa, w_pb, w_out = [_full_from_gathered(g) for g in (*g_proj, g_out)]
    os, lses = (o0, o1, o2), (lse0, lse1, lse2)
    (x1, ya, yb, yap, ybp, merged), (g_down,) = _mixer_out(
        x, abcv, gates, os, lses, p["conv_a_w"], p["conv_a_b"], p["b_gate"], w_pa, w_pb, w_out,
        carry=_Exchange("gather", [late["w_down"]]))
    w_down = _full_from_gathered(g_down)
    h2, up_pre, act, conv, dx2, dx2i, acc_gf, loss = _ffn_fwd(x1, target, p["norm_ffn_g"], w_up, p["ffn_conv_w"],
                                                              p["ffn_conv_b"], w_down, p["final_norm_g"])

    parts, got = {}, {}
    dup, acc_fb = _ffn_act_bwd(dx2i, conv, w_down)
    parts["w_down"] = _by_destination(_tn_matmul(act, dx2i, "dw_down"))
    (dpre, dx1, acc_g2, acc_fw), (got["w_down"],) = _ffn_up_bwd(dup, up_pre, x1, dx2, p["ffn_conv_w"], w_up,
                                                                p["norm_ffn_g"],
                                                                carry=_Exchange("scatter", [parts["w_down"]]))
    parts["w_up"] = _by_destination(_tn_matmul(dpre, h2, "dw_up"))
    dgates, dyap, dybp, dya, dos, dls, acc_bg = _mixer_bwd(dx1, gates, yap, ybp, os, lses, p["b_gate"], w_out,
                                                           w_pa, w_pb)
    parts["w_out"] = _by_destination(_tn_matmul(merged, dx1, "dw_out"))
    parts["w_proj_a"] = _by_destination(_tn_matmul(dyap, ya, "dw_proj_a"))
    parts["w_proj_b"] = _by_destination(_tn_matmul(dybp, yb, "dw_proj_b"))
    minor = ("w_out", "w_proj_a", "w_proj_b")
    half = parts["w_up"].shape[1] // 2
    (ds0,), received = _attn_bwd(ss[0], dos[0], lses[0], dls[0], DILATIONS[0],
                                 carry=_Exchange("scatter", [parts[n] for n in minor]))
    got.update(zip(minor, received))
    (ds1,), first_half = _attn_bwd(ss[1], dos[1], lses[1], dls[1], DILATIONS[1],
                                   carry=_Exchange("scatter", [parts["w_up"]], rows=(0, half)))
    (ds2,), (got["w_up"],) = _attn_bwd(ss[2], dos[2], lses[2], dls[2], DILATIONS[2],
                                       carry=_Exchange("scatter", [parts["w_up"]], rows=(half, half),
                                                       into=first_half))
    dss = [ds0, ds1, ds2]
    dabcv, acc_ca = _conv_mixer_bwd(abcv, dya, p["conv_a_w"], p["conv_a_b"])
    dw_s = [_dw_in_qkv(ds, h, dil) for ds, dil in zip(dss, DILATIONS)]
    dw_qkv = [w[j * GROUP_W:(j + 1) * GROUP_W] for j in range(3) for w in dw_s]
    g_w_in = jnp.concatenate([_tn_matmul(dabcv, h, "dw_in_a"), *dw_qkv, _tn_matmul(dgates, h, "dw_in_g")], axis=0)
    parts["w_in"] = _by_destination(g_w_in)
    (dx, acc_g1), (got["w_in"],) = _in_proj_bwd(x, dx1, dabcv, dss, dgates, p["w_in"], p["norm_mix_g"],
                                                carry=_Exchange("scatter", [parts["w_in"]]))
    small = dict(norm_mix_g=acc_g1[0:1], b_gate=acc_bg[0:2], conv_a_w=acc_ca[0:3], conv_a_b=acc_ca[3:4],
                 norm_ffn_g=acc_g2[0:1], ffn_conv_w=acc_fw[0:3], ffn_conv_b=acc_fb[0:1], final_norm_g=acc_gf[0:1])
    return loss[0, 0], dx, parts, got, small


def _all_gather(shards):
    n = len(shards)

    def body(*refs):
        ins, outs = refs[:n], refs[n:2 * n]
        send_sems, recv_sems, local_sems = refs[2 * n:]
        x, y, c = _mesh_pos()
        me, sibling = (x, y, c), (x, y, 1 - c)
        chips = [(1 - x, y), (x, 1 - y), (1 - x, 1 - y)]

        def copy(i, k, block, to, src=None):
            rows = outs[i].at[_dev_index(*block)]
            return pltpu.make_async_remote_copy(
                src_ref=rows if src is None else src, dst_ref=rows, send_sem=send_sems.at[i, k],
                recv_sem=recv_sems.at[i, k], device_id=to, device_id_type=MESH)

        mine, first, passed = [], [], []
        for i in range(n):
            cp = pltpu.make_async_copy(ins[i], outs[i].at[_dev_index(*me)], local_sems.at[i])
            cp.start()
            mine.append(cp)
            first.append(copy(i, 0, me, sibling, src=ins[i]))
            first += [copy(i, 1 + j, me, (*chip, c), src=ins[i]) for j, chip in enumerate(chips)]
        for cp in first:
            cp.start()
        for i in range(n):
            for j, chip in enumerate(chips):
                copy(i, 1 + j, (*chip, c), me).wait_recv()
                fw = copy(i, 4 + j, (*chip, c), sibling)
                fw.start()
                passed.append(fw)
        for i in range(n):
            copy(i, 0, sibling, me).wait_recv()
            for j, chip in enumerate(chips):
                copy(i, 4 + j, (*chip, 1 - c), me).wait_recv()
        for cp in first + passed:
            cp.wait_send()
        for cp in mine:
            cp.wait()

    return pl.pallas_call(
        body, name="all_gather_weights",
        in_specs=[ANY] * n, out_specs=[ANY] * n,
        out_shape=[jax.ShapeDtypeStruct((N_DEV,) + s.shape, s.dtype) for s in shards],
        scratch_shapes=[pltpu.SemaphoreType.DMA((n, 7)), pltpu.SemaphoreType.DMA((n, 7)),
                        pltpu.SemaphoreType.DMA((n,))],
    )(*shards)


def _all_reduce_small(v):
    r = v.shape[0]

    def body(v_ref, o_ref, gath, send_sems, recv_sems):
        x, y, c = _mesh_pos()
        me = _dev_index(x, y, c)
        gath[me] = v_ref[...]
        flips = [(kx, ky, kc) for kx in (0, 1) for ky in (0, 1) for kc in (0, 1)][1:]
        copies = []
        for k, (kx, ky, kc) in enumerate(flips):
            px = 1 - x if kx else x
            py = 1 - y if ky else y
            pc = 1 - c if kc else c
            cp = pltpu.make_async_remote_copy(
                src_ref=v_ref, dst_ref=gath.at[me], send_sem=send_sems.at[k], recv_sem=recv_sems.at[k],
                device_id=(px, py, pc), device_id_type=MESH)
            cp.start()
            copies.append((cp, _dev_index(px, py, pc)))
        for k, (cp, peer) in enumerate(copies):
            pltpu.make_async_remote_copy(
                src_ref=v_ref, dst_ref=gath.at[peer], send_sem=send_sems.at[k], recv_sem=recv_sems.at[k],
                device_id=(x, y, c), device_id_type=MESH).wait_recv()
        for cp, _ in copies:
            cp.wait_send()
        total = gath[0]
        for j in range(1, N_DEV):
            total = total + gath[j]
        o_ref[...] = total

    return pl.pallas_call(
        body, name="all_reduce_small",
        in_specs=[pl.BlockSpec(memory_space=pltpu.VMEM)], out_specs=pl.BlockSpec(memory_space=pltpu.VMEM),
        out_shape=jax.ShapeDtypeStruct((r, LANES), F32),
        scratch_shapes=[pltpu.VMEM((N_DEV, r, LANES), F32), pltpu.SemaphoreType.DMA((7,)),
                        pltpu.SemaphoreType.DMA((7,))],
    )(v)


def _adamw_math(w, g, m, v):
    m2 = ADAM_B1 * m + (1.0 - ADAM_B1) * g
    v2 = ADAM_B2 * v + (1.0 - ADAM_B2) * (g * g)
    m_hat = m2 / (1.0 - ADAM_B1 ** ADAM_STEP)
    v_hat = v2 / (1.0 - ADAM_B2 ** ADAM_STEP)
    delta = -ADAM_LR * (m_hat / (jnp.sqrt(v_hat) + ADAM_EPS) + ADAM_WD * w)
    return delta, m2, v2


def _adamw_big(w, m, v, part, got, me):
    r, c = part.shape[1:]
    flip = w.shape != (r, c)
    tr = r if flip else max(t for t in range(HALO, min(r, 512) + 1, HALO) if r % t == 0)

    def body(me_ref, w_ref, m_ref, v_ref, own_ref, *rest):
        del me_ref
        got_refs, (g_out, d_out, m_out, v_out) = rest[:N_DEV - 1], rest[N_DEV - 1:]
        g = own_ref[...].astype(F32)
        for ref in got_refs:
            g = g + ref[...].astype(F32)
        if flip:
            g = g.T
        delta, m2, v2 = _adamw_math(w_ref[...], g, m_ref[...], v_ref[...])
        g_out[...] = g
        d_out[...] = delta
        m_out[...] = m2
        v_out[...] = v2

    def peer_block(k):
        return pl.BlockSpec((None, tr, c), lambda i, me_ref: (jnp.bitwise_xor(me_ref[0], k), i, 0))

    plain = pl.BlockSpec(w.shape if flip else (tr, c), lambda i, me_ref: (i, 0))
    out = jax.ShapeDtypeStruct(w.shape, F32)
    return pl.pallas_call(
        body, name="adamw_big",
        grid_spec=pltpu.PrefetchScalarGridSpec(
            num_scalar_prefetch=1, grid=(r // tr,),
            in_specs=[plain, plain, plain] + [peer_block(k) for k in range(N_DEV)],
            out_specs=[plain] * 4),
        out_shape=[out] * 4,
        compiler_params=_params("parallel"),
    )(me, w, m, v, part, *([got] * (N_DEV - 1)))


def _adamw_small(ws, gs, ms, vs):
    n = len(ws)

    def body(*refs):
        ins, outs = refs[:4 * n], refs[4 * n:]
        for i in range(n):
            delta, m2, v2 = _adamw_math(ins[i][...], ins[n + i][...], ins[2 * n + i][...], ins[3 * n + i][...])
            outs[i][...] = delta
            outs[n + i][...] = m2
            outs[2 * n + i][...] = v2

    out = [jax.ShapeDtypeStruct(w.shape, F32) for w in ws]
    res = pl.pallas_call(body, name="adamw_small", out_shape=out * 3)(*ws, *gs, *ms, *vs)
    return res[:n], res[n:2 * n], res[2 * n:]


BIG = ("w_in", "w_proj_a", "w_proj_b", "w_out", "w_up", "w_down")
LATE = ("w_proj_a", "w_proj_b", "w_out", "w_up", "w_down")
COLUMN_SHARDED = ("w_in", "w_proj_a", "w_proj_b", "w_up")
WIDE_COLUMN_SHARDED = ("w_in", "w_up")
SMALL = ("norm_mix_g", "b_gate", "conv_a_w", "conv_a_b", "norm_ffn_g", "ffn_conv_w", "ffn_conv_b", "final_norm_g")
SMALL_SHARDED = ("b_gate", "conv_a_w", "ffn_conv_w")
WEIGHTS = ("norm_mix_g", "w_in", "b_gate", "conv_a_w", "conv_a_b", "w_proj_a", "w_proj_b", "w_out", "norm_ffn_g",
           "w_up", "ffn_conv_w", "ffn_conv_b", "w_down", "final_norm_g")


def _pack(vectors, rows):
    flat = jnp.concatenate([v.reshape(-1) for v in vectors])
    return jnp.pad(flat, (0, rows * LANES - flat.shape[0])).reshape(rows, LANES)


def _packed_rows(count):
    rows = -(-count // LANES)
    return -(-rows // SUBLANES) * SUBLANES


def _unpack(packed, shapes):
    flat = packed.reshape(-1)
    out, lo = [], 0
    for s in shapes:
        size = 1
        for dim in s:
            size *= dim
        out.append(flat[lo:lo + size].reshape(s))
        lo += size
    return out


def _full_from_gathered(gathered):
    _, r, c = gathered.shape
    return gathered.reshape(N_DEV * r, c)


def _by_destination(grad):
    rr, cc = grad.shape
    return grad.reshape(N_DEV, rr // N_DEV, cc)


def _block2d(name, a):
    a = a.reshape(a.shape[-2:])
    return a.T if name in WIDE_COLUMN_SHARDED else a


def kernel(x, norm_mix_g, w_in, b_gate, conv_a_w, conv_a_b, w_proj_a, w_proj_b, w_out, norm_ffn_g, w_up, ffn_conv_w, ffn_conv_b, w_down, final_norm_g, loss_target, m_norm_mix_g, m_w_in, m_b_gate, m_conv_a_w, m_conv_a_b, m_w_proj_a, m_w_proj_b, m_w_out, m_norm_ffn_g, m_w_up, m_ffn_conv_w, m_ffn_conv_b, m_w_down, m_final_norm_g, v_norm_mix_g, v_w_in, v_b_gate, v_conv_a_w, v_conv_a_b, v_w_proj_a, v_w_proj_b, v_w_out, v_norm_ffn_g, v_w_up, v_ffn_conv_w, v_ffn_conv_b, v_w_down, v_final_norm_g):
    given = dict(locals())
    shard = {n: given[n] for n in WEIGHTS}
    mom_m = {n: given["m_" + n] for n in WEIGHTS}
    mom_v = {n: given["v_" + n] for n in WEIGHTS}
    xi, yi, ci = _mesh_pos()
    me = _dev_index(xi, yi, ci)
    me1 = me.astype(jnp.int32).reshape(1)

    big2d = {n: _block2d(n, shard[n]) for n in BIG}
    small_shapes = [shard[n].shape[1:] for n in SMALL_SHARDED]
    n_small = sum(s[0] * s[1] for s in small_shapes)
    packed_small = _pack([shard[n] for n in SMALL_SHARDED], _packed_rows(n_small))
    gathered = _all_gather([big2d["w_in"].astype(BF16), packed_small])
    p = {"w_in": _full_from_gathered(gathered[0])}
    flat_small = gathered[-1].reshape(N_DEV, -1)
    lo = 0
    for n, (rows, width) in zip(SMALL_SHARDED, small_shapes):
        blocks = flat_small[:, lo:lo + rows * width].reshape(N_DEV, rows, width)
        p[n] = blocks.transpose(1, 0, 2).reshape(rows, N_DEV * width)
        lo += rows * width
    p["norm_mix_g"], p["norm_ffn_g"] = shard["norm_mix_g"], shard["norm_ffn_g"]
    p["conv_a_b"], p["ffn_conv_b"] = shard["conv_a_b"], shard["ffn_conv_b"]
    p["final_norm_g"] = shard["final_norm_g"].reshape(1, -1)
    late = {n: (big2d[n].T if n in ("w_proj_a", "w_proj_b") else big2d[n]).astype(BF16) for n in LATE}

    loss_part, dx, parts, got, g_small = _local_step(x[0], loss_target[0], p, late)

    results = {}
    for n in BIG:
        outs = _adamw_big(big2d[n], _block2d(n, mom_m[n]), _block2d(n, mom_v[n]), parts[n], got[n], me1)
        results[n] = [_block2d(n, o).reshape(shard[n].shape) for o in outs]

    small_full_shapes = [g_small[n].shape for n in SMALL]
    n_vec = sum(s[0] * s[1] for s in small_full_shapes) + 1
    packed = _pack([g_small[n] for n in SMALL] + [loss_part.reshape(1)], _packed_rows(n_vec))
    reduced = _all_reduce_small(packed)
    *g_full, loss_vec = _unpack(reduced, small_full_shapes + [(1,)])
    loss = loss_vec[0]
    own_g = []
    for n, g in zip(SMALL, g_full):
        if n in SMALL_SHARDED:
            width = shard[n].shape[-1]
            g = lax.dynamic_slice_in_dim(g, me * width, width, axis=1)
        own_g.append(g.reshape(shard[n].shape))
    def rows2d(a):
        return a.reshape(-1, a.shape[-1])

    deltas, new_ms, new_vs = _adamw_small([rows2d(shard[n]) for n in SMALL], [rows2d(g) for g in own_g],
                                          [rows2d(mom_m[n]) for n in SMALL], [rows2d(mom_v[n]) for n in SMALL])
    for i, n in enumerate(SMALL):
        results[n] = [own_g[i]] + [a.reshape(shard[n].shape) for a in (deltas[i], new_ms[i], new_vs[i])]

    grad_x = dx.reshape(x.shape)
    return (loss, grad_x, *[results[n][0] for n in WEIGHTS], *[results[n][1] for n in WEIGHTS],
            *[results[n][2] for n in WEIGHTS], *[results[n][3] for n in WEIGHTS])
```

```python
import functools

import jax
import jax.numpy as jnp
from jax import lax
from jax.experimental import pallas as pl
from jax.experimental.pallas import tpu as pltpu

F32 = jnp.float32
BF16 = jnp.bfloat16
MESH = pl.DeviceIdType.MESH

N_DEV = 8
RMS_EPS = 1e-6
NEG_INF = -1e30
N_GROUPS = 3
DILATIONS = (1, 4, 16)
HEADS_PER_GROUP = 4
HEAD_DIM = 64
GROUP_W = HEADS_PER_GROUP * HEAD_DIM
ATTN_W = N_GROUPS * GROUP_W
QBLK = 128
STEP_BLOCKS = 4
BWD_STEP_BLOCKS = 2
ATTN_SCALE = HEAD_DIM ** -0.5

ADAM_LR = 0.001
ADAM_B1 = 0.9
ADAM_B2 = 0.999
ADAM_EPS = 1e-08
ADAM_WD = 0.01
ADAM_STEP = 10

PERM_TOKENS = 256
ROWS_MATMUL = 512
HALO = 16
LANES = 128
SUBLANES = 8
VMEM_LIMIT_BYTES = 56 * 1024 * 1024


def _params(*sem):
    return pltpu.CompilerParams(dimension_semantics=sem, vmem_limit_bytes=VMEM_LIMIT_BYTES)


def _pick_tile(n, cap):
    if n <= cap:
        return n
    best = None
    for t in range(LANES, cap + 1, LANES):
        if n % t == 0:
            best = t
    assert best is not None, (n, cap)
    return best


def _rows(tm, c, j=0):
    return pl.BlockSpec((tm, c), lambda m: (m, j))


def _prev_halo(tm, c):
    return pl.BlockSpec((HALO, c), lambda m: (jnp.maximum(m * (tm // HALO) - 1, 0), 0))


def _next_halo(tm, c, t_total):
    last = t_total // HALO - 1
    return pl.BlockSpec((HALO, c), lambda m: (jnp.minimum((m + 1) * (tm // HALO), last), 0))


def _resident(shape):
    nd = len(shape)
    return pl.BlockSpec(shape, lambda *_: (0,) * nd, pipeline_mode=pl.Buffered(1))


def _acc_spec(c):
    return pl.BlockSpec((SUBLANES, c), lambda *_: (0, 0))


def _shift_down(u, halo, k):
    edge = jnp.concatenate([halo[HALO - SUBLANES:], u[:SUBLANES]], axis=0)
    head = pltpu.roll(edge, k, 0)[SUBLANES:]
    return jnp.concatenate([head, pltpu.roll(u, k, 0)[SUBLANES:]], axis=0)


def _shift_up(u, halo, k):
    n = u.shape[0]
    edge = jnp.concatenate([u[n - SUBLANES:], halo[:SUBLANES]], axis=0)
    tail = pltpu.roll(edge, 2 * SUBLANES - k, 0)[:SUBLANES]
    return jnp.concatenate([pltpu.roll(u, n - k, 0)[:n - SUBLANES], tail], axis=0)


def _interleave(tm, inverse=False):
    return _perm(tm // SUBLANES, tm, inverse)


def _edge_groups(u, halo, k, from_end):
    n = u.shape[0]
    sub = lax.broadcasted_iota(jnp.int32, (SUBLANES, u.shape[1]), 0)
    out = []
    for j in range(2 - k, 2):
        lo = n - HALO + j * SUBLANES if from_end else j * SUBLANES
        own, other = u[lo:lo + SUBLANES], halo[j * SUBLANES:(j + 1) * SUBLANES]
        if from_end:
            out.append(pltpu.roll(jnp.where(sub == SUBLANES - 1, other, own), 1, 0))
        else:
            out.append(pltpu.roll(jnp.where(sub == 0, other, own), SUBLANES - 1, 0))
    return out


def _shift_down_il(u, halo, k):
    return jnp.concatenate(_edge_groups(u, halo, k, True) + [u[:u.shape[0] - k * SUBLANES]], axis=0)


def _shift_up_il(u, halo, k):
    if k == 1:
        edge = _edge_groups(u, halo, 2, False)[:1]
    else:
        edge = _edge_groups(u, halo, 2, False)
    return jnp.concatenate([u[k * SUBLANES:]] + edge, axis=0)


def _stack_rows(rows, c):
    idx = lax.broadcasted_iota(jnp.int32, (SUBLANES, c), 0)
    out = jnp.zeros((SUBLANES, c), F32)
    for i, r in enumerate(rows):
        out = out + jnp.where(idx == i, r, 0.0)
    return out


def _colsum(v):
    return jnp.sum(v, axis=0, keepdims=True)


def _sigmoid(v):
    return 0.5 * jnp.tanh(0.5 * v) + 0.5


def _rms_fwd(xv, g):
    r = lax.rsqrt(jnp.mean(xv * xv, axis=-1, keepdims=True) + RMS_EPS)
    return xv * r * g, r


def _rms_bwd(xv, g, dy):
    r = lax.rsqrt(jnp.mean(xv * xv, axis=-1, keepdims=True) + RMS_EPS)
    xn = xv * r
    dxn = dy * g
    dx = r * (dxn - xn * jnp.mean(dxn * xn, axis=-1, keepdims=True))
    return dx, dy * xn


def _dot(a, b):
    return jnp.dot(a, b, preferred_element_type=F32)


def _dot_nt(a, b):
    return lax.dot_general(a, b, (((1,), (1,)), ((), ())), preferred_element_type=F32)


def _dot_tn(a, b):
    return lax.dot_general(a, b, (((0,), (0,)), ((), ())), preferred_element_type=F32)


def _perm(dil, n, inverse=False):
    i = lax.broadcasted_iota(jnp.int32, (n, n), 0)
    j = lax.broadcasted_iota(jnp.int32, (n, n), 1)
    if inverse:
        i, j = j, i
    per = n // dil
    return (j == (i % per) * dil + i // per).astype(BF16)


def _permute_rows(pm, v):
    if v.dtype == BF16:
        return _dot(pm, v).astype(BF16)
    hi = v.astype(BF16)
    lo = (v - hi.astype(F32)).astype(BF16)
    return _dot(pm, hi) + _dot(pm, lo)


def _stream_view(a, dil):
    t, c = a.shape
    return a.reshape(dil, t // dil, c)


def _stream_spec(dil, tm, c):
    return pl.BlockSpec((dil, tm // dil, c), lambda m: (0, m, 0))


def _load_streams(ref, dil, tm):
    c = ref.shape[-1]
    if dil == 1:
        return ref[...].reshape(tm, c)
    sub = min(PERM_TOKENS, tm)
    pm = _perm(dil, sub, inverse=True)
    parts = [_permute_rows(pm, ref[:, i * (sub // dil):(i + 1) * (sub // dil), :].reshape(sub, c))
             for i in range(tm // sub)]
    return parts[0] if len(parts) == 1 else jnp.concatenate(parts, axis=0)


def _store_streams(ref, dil, tm, v):
    if dil == 1:
        ref[...] = v.reshape(ref.shape).astype(ref.dtype)
        return
    sub = min(PERM_TOKENS, tm)
    pm = _perm(dil, sub)
    for i in range(tm // sub):
        piece = _permute_rows(pm, v[i * sub:(i + 1) * sub])
        ref[:, i * (sub // dil):(i + 1) * (sub // dil), :] = piece.reshape(dil, sub // dil, -1).astype(ref.dtype)


ANY = pl.BlockSpec(memory_space=pl.ANY)


def _mesh_pos():
    return lax.axis_index("x"), lax.axis_index("y"), lax.axis_index("c")


def _dev_index(px, py, pc):
    return 4 * px + 2 * py + pc


class _Exchange:
    def __init__(self, mode, arrays, rows=None, into=()):
        self.mode, self.arrays, self.rows, self.into = mode, list(arrays), rows, list(into)
        n = len(self.arrays)
        if mode == "gather":
            self.out_shape = [jax.ShapeDtypeStruct((N_DEV,) + a.shape, a.dtype) for a in self.arrays]
        else:
            self.out_shape = [jax.ShapeDtypeStruct(a.shape, a.dtype) for a in self.arrays]
        self.scratch = [pltpu.SemaphoreType.DMA((n, N_DEV - 1)), pltpu.SemaphoreType.DMA((n, N_DEV - 1)),
                        pltpu.SemaphoreType.DMA((n,))]

    def _peers(self):
        x, y, c = _mesh_pos()
        flips = [(kx, ky, kc) for kx in (0, 1) for ky in (0, 1) for kc in (0, 1)][1:]
        peers = [(1 - x if kx else x, 1 - y if ky else y, 1 - c if kc else c) for kx, ky, kc in flips]
        return _dev_index(x, y, c), peers

    def _copy(self, ins, outs, sems, i, k, peer, me, sending):
        src = ins[i] if self.mode == "gather" else ins[i].at[_dev_index(*peer)]
        dst = outs[i].at[me if sending else _dev_index(*peer)]
        if self.rows is not None:
            src, dst = src.at[pl.ds(*self.rows)], dst.at[pl.ds(*self.rows)]
        return pltpu.make_async_remote_copy(src_ref=src, dst_ref=dst, send_sem=sems[0].at[i, k],
                                            recv_sem=sems[1].at[i, k], device_id=peer, device_id_type=MESH)

    def _own(self, ins, outs, sems, i, me):
        return pltpu.make_async_copy(ins[i], outs[i].at[me], sems[2].at[i])

    def start(self, ins, outs, sems):
        me, peers = self._peers()
        for i in range(len(ins)):
            if self.mode == "gather":
                self._own(ins, outs, sems, i, me).start()
            for k, peer in enumerate(peers):
                self._copy(ins, outs, sems, i, k, peer, me, True).start()

    def wait(self, ins, outs, sems):
        me, peers = self._peers()
        for i in range(len(ins)):
            for k, peer in enumerate(peers):
                self._copy(ins, outs, sems, i, k, peer, me, False).wait_recv()
            for k, peer in enumerate(peers):
                self._copy(ins, outs, sems, i, k, peer, me, True).wait_send()
            if self.mode == "gather":
                self._own(ins, outs, sems, i, me).wait()


def _call(body, *, name, grid, in_specs, out_specs, out_shape, args, semantics, carry=None, scratch=()):
    if carry is None:
        return pl.pallas_call(body, name=name, grid=grid, in_specs=in_specs, out_specs=out_specs,
                              out_shape=out_shape, scratch_shapes=list(scratch),
                              compiler_params=_params(*semantics))(*args)
    n_in, n_out, n_x, n_s = len(in_specs), len(out_specs), len(carry.arrays), len(scratch)
    n_into = len(carry.into)
    all_in = n_in + n_x + n_into

    def carried(*refs):
        ins, x_ins = refs[:n_in], refs[n_in:n_in + n_x]
        outs = refs[all_in:all_in + n_out]
        x_outs = refs[all_in + n_out:all_in + n_out + n_x]
        own = refs[all_in + n_out + n_x:all_in + n_out + n_x + n_s]
        sems = refs[all_in + n_out + n_x + n_s:]
        first = functools.reduce(jnp.logical_and, [pl.program_id(a) == 0 for a in range(len(grid))])
        last = functools.reduce(jnp.logical_and, [pl.program_id(a) == grid[a] - 1 for a in range(len(grid))])

        @pl.when(first)
        def _():
            carry.start(x_ins, x_outs, sems)

        body(*ins, *outs, *own)

        @pl.when(last)
        def _():
            carry.wait(x_ins, x_outs, sems)

    res = pl.pallas_call(
        carried, name=name, grid=grid, in_specs=list(in_specs) + [ANY] * (n_x + n_into),
        out_specs=list(out_specs) + [ANY] * n_x, out_shape=list(out_shape) + carry.out_shape,
        input_output_aliases={n_in + n_x + i: n_out + i for i in range(n_into)},
        scratch_shapes=list(scratch) + carry.scratch, compiler_params=_params(*["arbitrary"] * len(grid)),
    )(*args, *carry.arrays, *carry.into)
    return list(res[:n_out]), list(res[n_out:])


def _in_proj(x, g, wt, cw, carry=None):
    t, d = x.shape
    n = wt.shape[0]
    tm = min(ROWS_MATMUL, t)
    qkv0 = 3 * cw

    def body(x_ref, g_ref, wt_ref, h_ref, abcv_ref, gates_ref, *s_refs):
        h = _rms_fwd(x_ref[...], g_ref[...])[0].astype(BF16)
        h_ref[...] = h
        abcv_ref[...] = _dot_nt(h, wt_ref[0:qkv0, :]).astype(BF16)
        gates_ref[...] = _dot_nt(h, wt_ref[qkv0 + 3 * ATTN_W:n, :]).astype(BF16)
        qkv = _dot_nt(h, wt_ref[qkv0:qkv0 + 3 * ATTN_W, :]).astype(BF16)
        for gi, s_ref in enumerate(s_refs):
            cols = [qkv[:, j * ATTN_W + gi * GROUP_W:j * ATTN_W + (gi + 1) * GROUP_W] for j in range(3)]
            _store_streams(s_ref, DILATIONS[gi], tm, jnp.concatenate(cols, axis=1))

    return _call(
        body, name="in_proj", grid=(t // tm,),
        in_specs=[_rows(tm, d), _resident((1, d)), _resident((n, d))],
        out_specs=[_rows(tm, d), _rows(tm, qkv0), _rows(tm, 2 * d)]
        + [_stream_spec(dil, tm, 3 * GROUP_W) for dil in DILATIONS],
        out_shape=[jax.ShapeDtypeStruct((t, d), BF16), jax.ShapeDtypeStruct((t, qkv0), BF16),
                   jax.ShapeDtypeStruct((t, 2 * d), BF16)]
        + [jax.ShapeDtypeStruct((dil, t // dil, 3 * GROUP_W), BF16) for dil in DILATIONS],
        args=(x, g, wt), semantics=("parallel",), carry=carry)


def _head_masks():
    lane = lax.broadcasted_iota(jnp.int32, (1, GROUP_W), 1)
    return lane, [(lane // HEAD_DIM) == h for h in range(HEADS_PER_GROUP)]


def _stack_heads(v, heads):
    return jnp.concatenate([jnp.where(hm, v, jnp.zeros_like(v)) for hm in heads], axis=0)


def _merge_heads(v, heads):
    out = jnp.zeros((QBLK, GROUP_W), v.dtype)
    for h, hm in enumerate(heads):
        out = jnp.where(hm, v[h * QBLK:(h + 1) * QBLK], out)
    return out


def _pair_block(col, count=STEP_BLOCKS):
    return pl.BlockSpec((count * QBLK, GROUP_W), lambda b: (b, col))


def _edge_block(col, shift, nb, count=STEP_BLOCKS):
    return pl.BlockSpec((QBLK, GROUP_W), lambda b: (jnp.clip(count * b + shift, 0, nb - 1), col))


def _band_mask(has_prev):
    rows = HEADS_PER_GROUP * QBLK
    row = lax.broadcasted_iota(jnp.int32, (rows, 2 * QBLK), 0) & (QBLK - 1)
    col = lax.broadcasted_iota(jnp.int32, (rows, 2 * QBLK), 1)
    return ((col < QBLK) & (col >= row) & has_prev) | ((col >= QBLK) & (col - QBLK <= row))


def _next_mask(has_next):
    rows = HEADS_PER_GROUP * QBLK
    row = lax.broadcasted_iota(jnp.int32, (rows, QBLK), 0) & (QBLK - 1)
    col = lax.broadcasted_iota(jnp.int32, (rows, QBLK), 1)
    return (col >= row) & has_next


def _attn_fwd(s, dil, carry=None):
    t = s.shape[0] * s.shape[1]
    nb = t // QBLK
    per_stream = nb // dil
    assert per_stream % STEP_BLOCKS == 0

    def body(q_ref, kc_ref, kp_ref, vc_ref, vp_ref, o_ref, lse_ref):
        b = pl.program_id(0)
        _, heads = _head_masks()
        first_has_prev = lax.rem(STEP_BLOCKS * b, per_stream) != 0
        for j in range(STEP_BLOCKS):
            rows = slice(j * QBLK, (j + 1) * QBLK)
            if j == 0:
                k2 = jnp.concatenate([kp_ref[...], kc_ref[rows, :]], axis=0)
                v2 = jnp.concatenate([vp_ref[...], vc_ref[rows, :]], axis=0)
            else:
                both = slice((j - 1) * QBLK, (j + 1) * QBLK)
                k2, v2 = kc_ref[both, :], vc_ref[both, :]
            mask = _band_mask(first_has_prev if j == 0 else True)
            sc = jnp.where(mask, _dot_nt(_stack_heads(q_ref[rows, :], heads), k2) * ATTN_SCALE, NEG_INF)
            mx = jnp.max(sc, axis=1, keepdims=True)
            pr = jnp.exp(sc - mx)
            den = jnp.sum(pr, axis=1, keepdims=True)
            o_all = _dot(pr.astype(BF16), v2) / den
            o_ref[rows, :] = _merge_heads(o_all, heads).astype(BF16)
            lse_ref[rows, :] = _merge_heads(jnp.broadcast_to(mx + jnp.log(den), o_all.shape), heads)

    sv = s.reshape(t, 3 * GROUP_W)
    return _call(
        body, name=f"attn_fwd_d{dil}", grid=(nb // STEP_BLOCKS,),
        in_specs=[_pair_block(0), _pair_block(1), _edge_block(1, -1, nb), _pair_block(2), _edge_block(2, -1, nb)],
        out_specs=[_pair_block(0), _pair_block(0)],
        out_shape=[jax.ShapeDtypeStruct((t, GROUP_W), BF16), jax.ShapeDtypeStruct((t, GROUP_W), F32)],
        args=(sv, sv, sv, sv, sv), semantics=("parallel",), carry=carry)


def _group_softmax(parts):
    mx = jnp.maximum(jnp.maximum(parts[0], parts[1]), parts[2])
    es = [jnp.exp(p - mx) for p in parts]
    den = es[0] + es[1] + es[2]
    return [e / den for e in es]


def _mixer_out(x, abcv, gates, os, lses, conv_w, conv_b, b_gate, w_pa, w_pb, w_o, carry=None):
    t, d = x.shape
    cw = conv_w.shape[1]
    tm = min(ROWS_MATMUL, t)

    def body(x_ref, abcv_ref, halo_ref, gates_ref, o0_ref, o1_ref, o2_ref, l0_ref, l1_ref, l2_ref, cw_ref, cb_ref,
             bg_ref, wpa_ref, wpb_ref, wo_ref, x1_ref, ya_ref, yb_ref, yap_ref, ybp_ref, mg_ref):
        m = pl.program_id(0)
        ab = abcv_ref[:, 0:cw].astype(F32)
        u = abcv_ref[:, cw:2 * cw].astype(F32) * abcv_ref[:, 2 * cw:3 * cw].astype(F32)
        hu = halo_ref[:, cw:2 * cw].astype(F32) * halo_ref[:, 2 * cw:3 * cw].astype(F32)
        hu = jnp.where(m > 0, hu, 0.0)
        cv = (cw_ref[0:1, :] * _shift_down(u, hu, 2) + cw_ref[1:2, :] * _shift_down(u, hu, 1)
              + cw_ref[2:3, :] * u + cb_ref[...])
        ya = (ab * cv).astype(BF16)
        ya_ref[...] = ya
        alphas = _group_softmax([_load_streams(r, dil, tm) for r, dil in zip((l0_ref, l1_ref, l2_ref), DILATIONS)])
        for i, (o_ref, dil) in enumerate(zip((o0_ref, o1_ref, o2_ref), DILATIONS)):
            sl = slice(i * GROUP_W, (i + 1) * GROUP_W)
            yb_ref[:, sl] = (alphas[i] * _load_streams(o_ref, dil, tm).astype(F32)).astype(BF16)
        yap = _dot_nt(ya, wpa_ref[...])
        ybp = _dot_nt(yb_ref[...], wpb_ref[...])
        yap_ref[...] = yap.astype(BF16)
        ybp_ref[...] = ybp.astype(BF16)
        sa = _sigmoid(gates_ref[:, 0:d].astype(F32) + bg_ref[0:1, :])
        sb = _sigmoid(gates_ref[:, d:2 * d].astype(F32) + bg_ref[1:2, :])
        merged = (sa * yap + sb * ybp).astype(BF16)
        mg_ref[...] = merged
        x1_ref[...] = x_ref[...] + _dot(merged, wo_ref[...])

    return _call(
        body, name="mixer_out", grid=(t // tm,),
        in_specs=[_rows(tm, d), _rows(tm, 3 * cw), _prev_halo(tm, 3 * cw), _rows(tm, 2 * d)]
        + [_stream_spec(dil, tm, GROUP_W) for dil in DILATIONS] * 2
        + [_resident((3, cw)), _resident((1, cw)), _resident((2, d)),
           _resident((d, cw)), _resident((d, ATTN_W)), _resident((d, d))],
        out_specs=[_rows(tm, d), _rows(tm, cw), _rows(tm, ATTN_W), _rows(tm, d), _rows(tm, d), _rows(tm, d)],
        out_shape=[jax.ShapeDtypeStruct((t, d), F32), jax.ShapeDtypeStruct((t, cw), BF16),
                   jax.ShapeDtypeStruct((t, ATTN_W), BF16), jax.ShapeDtypeStruct((t, d), BF16),
                   jax.ShapeDtypeStruct((t, d), BF16), jax.ShapeDtypeStruct((t, d), BF16)],
        args=(x, abcv, abcv, gates, *[_stream_view(a, dil) for a, dil in zip(os, DILATIONS)],
              *[_stream_view(a, dil) for a, dil in zip(lses, DILATIONS)], conv_w, conv_b, b_gate, w_pa, w_pb, w_o),
        semantics=("parallel",), carry=carry)


def _ffn_fwd(x1, target, g2, w_ut, conv_w, conv_b, w_d, g_f, carry=None):
    t, d = x1.shape
    dff = w_d.shape[0]
    tm = min(256, t)
    ck = _pick_tile(dff, 2816)

    def body(x1_ref, tg_ref, g2_ref, wut_ref, cw_ref, cb_ref, wd_ref, gf_ref, h2_ref, up_ref, act_ref, conv_ref,
             dx2_ref, dx2i_ref, acc_ref, loss_ref, halo_ref):
        m = pl.program_id(0)

        @pl.when(m == 0)
        def _():
            acc_ref[...] = jnp.zeros_like(acc_ref)
            loss_ref[...] = jnp.zeros_like(loss_ref)
            halo_ref[...] = jnp.zeros_like(halo_ref)

        h2 = _permute_rows(_interleave(tm), _rms_fwd(x1_ref[...], g2_ref[...])[0].astype(BF16))
        h2_ref[...] = h2

        def conv(c0):
            p = _dot_nt(h2, wut_ref[c0:c0 + ck, :])
            up_ref[:, c0:c0 + ck] = p.astype(BF16)
            hp = halo_ref[:, c0:c0 + ck]
            halo_ref[:, c0:c0 + ck] = p[tm - HALO:, :]
            return (cw_ref[0:1, c0:c0 + ck] * _shift_down_il(p, hp, 2)
                    + cw_ref[1:2, c0:c0 + ck] * _shift_down_il(p, hp, 1)
                    + cw_ref[2:3, c0:c0 + ck] * p + cb_ref[:, c0:c0 + ck])

        down = jnp.zeros((tm, d), F32)
        for c0 in range(0, dff, ck):
            gate = conv(c0)
            val = conv(dff + c0)
            conv_ref[:, c0:c0 + ck] = gate.astype(BF16)
            conv_ref[:, dff + c0:dff + c0 + ck] = val.astype(BF16)
            act = (gate * _sigmoid(gate) * val).astype(BF16)
            act_ref[:, c0:c0 + ck] = act
            down = down + _dot(act, wd_ref[c0:c0 + ck, :])
        x2 = x1_ref[...] + _permute_rows(_interleave(tm, inverse=True), down)
        y, _ = _rms_fwd(x2, gf_ref[...])
        diff = y - tg_ref[...]
        loss_ref[...] += 0.5 * jnp.sum(jnp.mean(diff * diff, axis=-1, keepdims=True))
        dx2, dg = _rms_bwd(x2, gf_ref[...], diff * (1.0 / d))
        dx2_ref[...] = dx2
        dx2i_ref[...] = _permute_rows(_interleave(tm), dx2.astype(BF16))
        acc_ref[...] += _stack_rows([_colsum(dg)], d)

    return _call(
        body, name="ffn_fwd", grid=(t // tm,),
        in_specs=[_rows(tm, d), _rows(tm, d), _resident((1, d)), _resident((2 * dff, d)), _resident((3, 2 * dff)),
                  _resident((1, 2 * dff)), _resident((dff, d)), _resident((1, d))],
        out_specs=[_rows(tm, d), _rows(tm, 2 * dff), _rows(tm, dff), _rows(tm, 2 * dff), _rows(tm, d), _rows(tm, d),
                   _acc_spec(d), _acc_spec(LANES)],
        out_shape=[jax.ShapeDtypeStruct((t, d), BF16), jax.ShapeDtypeStruct((t, 2 * dff), BF16),
                   jax.ShapeDtypeStruct((t, dff), BF16), jax.ShapeDtypeStruct((t, 2 * dff), BF16),
                   jax.ShapeDtypeStruct((t, d), F32), jax.ShapeDtypeStruct((t, d), BF16),
                   jax.ShapeDtypeStruct((SUBLANES, d), F32), jax.ShapeDtypeStruct((SUBLANES, LANES), F32)],
        args=(x1, target, g2, w_ut, conv_w, conv_b, w_d, g_f), semantics=("arbitrary",), carry=carry,
        scratch=[pltpu.VMEM((HALO, 2 * dff), F32)])


def _ffn_act_bwd(dx2, conv, w_d):
    t, d = dx2.shape
    dff = w_d.shape[0]
    tm = min(256, t)
    ck = _pick_tile(dff, 2816)

    def body(dx2_ref, conv_ref, wd_ref, dup_ref, acc_ref):
        m = pl.program_id(0)

        @pl.when(m == 0)
        def _():
            acc_ref[...] = jnp.zeros_like(acc_ref)

        dx2v = dx2_ref[...]
        for c0 in range(0, dff, ck):
            dact = _dot_nt(dx2v, wd_ref[c0:c0 + ck, :])
            gate = conv_ref[:, c0:c0 + ck].astype(F32)
            val = conv_ref[:, dff + c0:dff + c0 + ck].astype(F32)
            sg = _sigmoid(gate)
            dval = dact * gate * sg
            dgate = dact * val * sg * (1.0 + gate * (1.0 - sg))
            dup_ref[:, c0:c0 + ck] = dgate.astype(BF16)
            dup_ref[:, dff + c0:dff + c0 + ck] = dval.astype(BF16)
            acc_ref[:, c0:c0 + ck] += _stack_rows([_colsum(dgate)], ck)
            acc_ref[:, dff + c0:dff + c0 + ck] += _stack_rows([_colsum(dval)], ck)

    return pl.pallas_call(
        body, name="ffn_act_bwd", grid=(t // tm,),
        in_specs=[_rows(tm, d), _rows(tm, 2 * dff), _resident((dff, d))],
        out_specs=[_rows(tm, 2 * dff), _acc_spec(2 * dff)],
        out_shape=[jax.ShapeDtypeStruct((t, 2 * dff), BF16), jax.ShapeDtypeStruct((SUBLANES, 2 * dff), F32)],
        compiler_params=_params("arbitrary"),
    )(dx2, conv, w_d)


def _ffn_up_bwd(dup, up_pre, x1, dx2, conv_w, w_u, g2, carry=None):
    t, d = x1.shape
    n = dup.shape[1]
    tm = min(256, t)
    ck = _pick_tile(n, 256)
    last = t // tm - 1

    def body(dup_ref, nxt_ref, up_ref, x1_ref, dx2_ref, cw_ref, wu_ref, g2_ref, dpre_ref, dx1_ref, acc_ref, accw_ref):
        m = pl.program_id(0)

        @pl.when(m == 0)
        def _():
            acc_ref[...] = jnp.zeros_like(acc_ref)
            accw_ref[...] = jnp.zeros_like(accw_ref)

        dh = jnp.zeros((tm, d), F32)
        for c0 in range(0, n, ck):
            du = dup_ref[:, c0:c0 + ck].astype(F32)
            hn = jnp.where(m < last, nxt_ref[:, c0:c0 + ck].astype(F32), 0.0)
            du1 = _shift_up_il(du, hn, 1)
            du2 = _shift_up_il(du, hn, 2)
            dpre = (cw_ref[2:3, c0:c0 + ck] * du + cw_ref[1:2, c0:c0 + ck] * du1
                    + cw_ref[0:1, c0:c0 + ck] * du2).astype(BF16)
            dpre_ref[:, c0:c0 + ck] = dpre
            dh = dh + _dot(dpre, wu_ref[c0:c0 + ck, :])
            p = up_ref[:, c0:c0 + ck].astype(F32)
            accw_ref[:, c0:c0 + ck] += _stack_rows([_colsum(du2 * p), _colsum(du1 * p), _colsum(du * p)], ck)
        dh = _permute_rows(_interleave(tm, inverse=True), dh)
        dx, dg = _rms_bwd(x1_ref[...], g2_ref[...], dh)
        dx1_ref[...] = dx2_ref[...] + dx
        acc_ref[...] += _stack_rows([_colsum(dg)], d)

    return _call(
        body, name="ffn_up_bwd", grid=(t // tm,),
        in_specs=[_rows(tm, n), _next_halo(tm, n, t), _rows(tm, n), _rows(tm, d), _rows(tm, d), _resident((3, n)),
                  _resident((n, d)), _resident((1, d))],
        out_specs=[_rows(tm, n), _rows(tm, d), _acc_spec(d), _acc_spec(n)],
        out_shape=[jax.ShapeDtypeStruct((t, n), BF16), jax.ShapeDtypeStruct((t, d), F32),
                   jax.ShapeDtypeStruct((SUBLANES, d), F32), jax.ShapeDtypeStruct((SUBLANES, n), F32)],
        args=(dup, dup, up_pre, x1, dx2, conv_w, w_u, g2), semantics=("arbitrary",), carry=carry)


def _tn_matmul(a, b, name):
    t, mdim = a.shape
    n = b.shape[1]
    tk = min(1024, t)
    tmm = _pick_tile(mdim, 1536)
    tn = _pick_tile(n, 1024)

    def body(a_ref, b_ref, o_ref, acc_ref):
        k = pl.program_id(2)

        @pl.when(k == 0)
        def _():
            acc_ref[...] = jnp.zeros_like(acc_ref)

        acc_ref[...] += _dot_tn(a_ref[...].astype(BF16), b_ref[...].astype(BF16))

        @pl.when(k == t // tk - 1)
        def _():
            o_ref[...] = acc_ref[...].astype(BF16)

    return pl.pallas_call(
        body, name=name, grid=(mdim // tmm, n // tn, t // tk),
        in_specs=[pl.BlockSpec((tk, tmm), lambda i, j, k: (k, i)), pl.BlockSpec((tk, tn), lambda i, j, k: (k, j))],
        out_specs=pl.BlockSpec((tmm, tn), lambda i, j, k: (i, j)),
        out_shape=jax.ShapeDtypeStruct((mdim, n), BF16),
        scratch_shapes=[pltpu.VMEM((tmm, tn), F32)],
        compiler_params=_params("parallel", "parallel", "arbitrary"),
    )(a, b)


def _mixer_bwd(dx1, gates, yap, ybp, os, lses, b_gate, w_o, w_pa, w_pb):
    t, d = dx1.shape
    cw = w_pa.shape[1]
    tm = min(ROWS_MATMUL, t)

    def body(dx1_ref, gates_ref, yap_ref, ybp_ref, o0_ref, o1_ref, o2_ref, l0_ref, l1_ref, l2_ref, bg_ref, wo_ref,
             wpa_ref, wpb_ref, dgates_ref, dyap_ref, dybp_ref, dya_ref, do0_ref, do1_ref, do2_ref, dl0_ref, dl1_ref,
             dl2_ref, acc_ref):
        m = pl.program_id(0)

        @pl.when(m == 0)
        def _():
            acc_ref[...] = jnp.zeros_like(acc_ref)

        dmg = _dot_nt(dx1_ref[...].astype(BF16), wo_ref[...])
        sa = _sigmoid(gates_ref[:, 0:d].astype(F32) + bg_ref[0:1, :])
        sb = _sigmoid(gates_ref[:, d:2 * d].astype(F32) + bg_ref[1:2, :])
        dyap = (dmg * sa).astype(BF16)
        dybp = (dmg * sb).astype(BF16)
        dga = dmg * yap_ref[...].astype(F32) * sa * (1.0 - sa)
        dgb = dmg * ybp_ref[...].astype(F32) * sb * (1.0 - sb)
        dyap_ref[...] = dyap
        dybp_ref[...] = dybp
        dgates_ref[:, 0:d] = dga.astype(BF16)
        dgates_ref[:, d:2 * d] = dgb.astype(BF16)
        acc_ref[...] += _stack_rows([_colsum(dga), _colsum(dgb)], d)
        dya_ref[...] = _dot(dyap, wpa_ref[...]).astype(BF16)
        dyb = _dot(dybp, wpb_ref[...])

        ri = lax.broadcasted_iota(jnp.int32, (GROUP_W, GROUP_W), 0) // HEAD_DIM
        ci = lax.broadcasted_iota(jnp.int32, (GROUP_W, GROUP_W), 1) // HEAD_DIM
        same_head = (ri == ci).astype(BF16)
        alphas = _group_softmax([_load_streams(r, dil, tm) for r, dil in zip((l0_ref, l1_ref, l2_ref), DILATIONS)])
        prod = jnp.zeros((tm, GROUP_W), F32)
        for i, (o_ref, do_ref, dil) in enumerate(zip((o0_ref, o1_ref, o2_ref), (do0_ref, do1_ref, do2_ref), DILATIONS)):
            dov = alphas[i] * dyb[:, i * GROUP_W:(i + 1) * GROUP_W]
            _store_streams(do_ref, dil, tm, dov.astype(BF16))
            prod = prod + dov * _load_streams(o_ref, dil, tm).astype(F32)
        hi = prod.astype(BF16)
        lo = (prod - hi.astype(F32)).astype(BF16)
        dtot = _dot(hi, same_head) + _dot(lo, same_head)
        for alpha, dl_ref, dil in zip(alphas, (dl0_ref, dl1_ref, dl2_ref), DILATIONS):
            _store_streams(dl_ref, dil, tm, alpha * dtot)

    streams = [_stream_spec(dil, tm, GROUP_W) for dil in DILATIONS]
    res = _call(
        body, name="mixer_bwd", grid=(t // tm,),
        in_specs=[_rows(tm, d), _rows(tm, 2 * d), _rows(tm, d), _rows(tm, d)] + streams * 2
        + [_resident((2, d)), _resident((d, d)), _resident((d, cw)), _resident((d, ATTN_W))],
        out_specs=[_rows(tm, 2 * d), _rows(tm, d), _rows(tm, d), _rows(tm, cw)] + streams * 2 + [_acc_spec(d)],
        out_shape=[jax.ShapeDtypeStruct((t, 2 * d), BF16), jax.ShapeDtypeStruct((t, d), BF16),
                   jax.ShapeDtypeStruct((t, d), BF16), jax.ShapeDtypeStruct((t, cw), BF16)]
        + [jax.ShapeDtypeStruct((dil, t // dil, GROUP_W), BF16) for dil in DILATIONS]
        + [jax.ShapeDtypeStruct((dil, t // dil, GROUP_W), F32) for dil in DILATIONS]
        + [jax.ShapeDtypeStruct((SUBLANES, d), F32)],
        args=(dx1, gates, yap, ybp, *[_stream_view(a, dil) for a, dil in zip(os, DILATIONS)],
              *[_stream_view(a, dil) for a, dil in zip(lses, DILATIONS)], b_gate, w_o, w_pa, w_pb),
        semantics=("arbitrary",))
    dgates, dyap, dybp, dya = res[:4]
    dos = [a.reshape(t, GROUP_W) for a in res[4:7]]
    dls = [a.reshape(t, GROUP_W) for a in res[7:10]]
    return dgates, dyap, dybp, dya, dos, dls, res[10]


def _attn_bwd(s, do, lse, dl, dil, carry=None):
    t = s.shape[0] * s.shape[1]
    nb = t // QBLK
    per_stream = nb // dil
    count = BWD_STEP_BLOCKS
    assert per_stream % count == 0

    def body(q_ref, qn_ref, kc_ref, kp_ref, vc_ref, vp_ref, do_ref, don_ref, lse_ref, lsen_ref, dl_ref, dln_ref,
             ds_ref):
        b = pl.program_id(0)
        lane, heads = _head_masks()
        first_has_prev = lax.rem(count * b, per_stream) != 0
        last_has_next = lax.rem(count * (b + 1), per_stream) != 0

        def cols(v):
            return jnp.concatenate([jnp.sum(jnp.where(lane == h * HEAD_DIM, v, 0.0), axis=1, keepdims=True)
                                    for h in range(HEADS_PER_GROUP)], axis=0)

        def pair(qs, dos, k, v, valid, lse_c, dl_c):
            s = jnp.where(valid, _dot_nt(qs, k) * ATTN_SCALE, NEG_INF)
            p = jnp.exp(s - lse_c)
            ds = p * (_dot_nt(dos, v) - dl_c)
            return p.astype(BF16), ds.astype(BF16)

        for j in range(count):
            rows, hi = slice(j * QBLK, (j + 1) * QBLK), slice((j + 1) * QBLK, (j + 2) * QBLK)
            q, do, lse, dl = q_ref[rows, :], do_ref[rows, :], lse_ref[rows, :], dl_ref[rows, :]
            kc, vc = kc_ref[rows, :], vc_ref[rows, :]
            if j == 0:
                k2 = jnp.concatenate([kp_ref[...], kc], axis=0)
                v2 = jnp.concatenate([vp_ref[...], vc], axis=0)
                mask = _band_mask(first_has_prev)
            else:
                both = slice((j - 1) * QBLK, (j + 1) * QBLK)
                k2, v2, mask = kc_ref[both, :], vc_ref[both, :], _band_mask(True)
            if j < count - 1:
                qn, don, lsen, dln = q_ref[hi, :], do_ref[hi, :], lse_ref[hi, :], dl_ref[hi, :]
                mask_n = _next_mask(True)
            else:
                qn, don, lsen, dln = qn_ref[...], don_ref[...], lsen_ref[...], dln_ref[...]
                mask_n = _next_mask(last_has_next)
            qs, qns = _stack_heads(q, heads), _stack_heads(qn, heads)
            dos, dons = _stack_heads(do, heads), _stack_heads(don, heads)
            p_q, ds_q = pair(qs, dos, k2, v2, mask, cols(lse), cols(dl))
            p_n, ds_n = pair(qns, dons, kc, vc, mask_n, cols(lsen), cols(dln))
            dq = _merge_heads(_dot(ds_q, k2), heads)
            dk = _dot_tn(jnp.concatenate([ds_q[:, QBLK:], ds_n], axis=0), jnp.concatenate([qs, qns], axis=0))
            dv = _dot_tn(jnp.concatenate([p_q[:, QBLK:], p_n], axis=0), jnp.concatenate([dos, dons], axis=0))
            ds_ref[rows, 0:GROUP_W] = (dq * ATTN_SCALE).astype(BF16)
            ds_ref[rows, GROUP_W:2 * GROUP_W] = (dk * ATTN_SCALE).astype(BF16)
            ds_ref[rows, 2 * GROUP_W:3 * GROUP_W] = dv.astype(BF16)

    sv = s.reshape(t, 3 * GROUP_W)
    cur, nxt = _pair_block(0, count), _edge_block(0, count, nb, count)
    return _call(
        body, name=f"attn_bwd_d{dil}", grid=(nb // count,),
        in_specs=[cur, nxt, _pair_block(1, count), _edge_block(1, -1, nb, count), _pair_block(2, count),
                  _edge_block(2, -1, nb, count), cur, nxt, cur, nxt, cur, nxt],
        out_specs=[pl.BlockSpec((count * QBLK, 3 * GROUP_W), lambda b: (b, 0))],
        out_shape=[jax.ShapeDtypeStruct((t, 3 * GROUP_W), BF16)],
        args=(sv, sv, sv, sv, sv, sv, do, do, lse, lse, dl, dl), semantics=("parallel",), carry=carry)


def _conv_mixer_bwd(abcv, dya, conv_w, conv_b):
    t = abcv.shape[0]
    cw = conv_w.shape[1]
    tm = min(2048, t)
    last = t // tm - 1

    def body(a_ref, ap_ref, an_ref, dya_ref, dyan_ref, cw_ref, cb_ref, d_ref, acc_ref):
        m = pl.program_id(0)

        @pl.when(m == 0)
        def _():
            acc_ref[...] = jnp.zeros_like(acc_ref)

        ab = a_ref[:, 0:cw].astype(F32)
        ac = a_ref[:, cw:2 * cw].astype(F32)
        av = a_ref[:, 2 * cw:3 * cw].astype(F32)
        u = ac * av
        hu = ap_ref[:, cw:2 * cw].astype(F32) * ap_ref[:, 2 * cw:3 * cw].astype(F32)
        hu = jnp.where(m > 0, hu, 0.0)
        u1 = _shift_down(u, hu, 1)
        u2 = _shift_down(u, hu, 2)
        cv = cw_ref[0:1, :] * u2 + cw_ref[1:2, :] * u1 + cw_ref[2:3, :] * u + cb_ref[...]
        dya_v = dya_ref[...].astype(F32)
        dcv = dya_v * ab
        ndcv = jnp.where(m < last, dyan_ref[...].astype(F32) * an_ref[:, 0:cw].astype(F32), 0.0)
        du = (cw_ref[2:3, :] * dcv + cw_ref[1:2, :] * _shift_up(dcv, ndcv, 1)
              + cw_ref[0:1, :] * _shift_up(dcv, ndcv, 2))
        d_ref[:, 0:cw] = (dya_v * cv).astype(BF16)
        d_ref[:, cw:2 * cw] = (du * av).astype(BF16)
        d_ref[:, 2 * cw:3 * cw] = (du * ac).astype(BF16)
        acc_ref[...] += _stack_rows([_colsum(dcv * u2), _colsum(dcv * u1), _colsum(dcv * u), _colsum(dcv)], cw)

    return pl.pallas_call(
        body, name="conv_mixer_bwd", grid=(t // tm,),
        in_specs=[_rows(tm, 3 * cw), _prev_halo(tm, 3 * cw), _next_halo(tm, 3 * cw, t), _rows(tm, cw),
                  _next_halo(tm, cw, t), _resident((3, cw)), _resident((1, cw))],
        out_specs=[_rows(tm, 3 * cw), _acc_spec(cw)],
        out_shape=[jax.ShapeDtypeStruct((t, 3 * cw), BF16), jax.ShapeDtypeStruct((SUBLANES, cw), F32)],
        compiler_params=_params("arbitrary"),
    )(abcv, abcv, abcv, dya, dya, conv_w, conv_b)


def _in_proj_bwd(x, dx1, dabcv, dss, dgates, w_in, g1, carry=None):
    t, d = x.shape
    qkv0 = dabcv.shape[1]
    n = w_in.shape[0]
    tm = min(ROWS_MATMUL, t)

    def body(x_ref, dx1_ref, da_ref, ds0_ref, ds1_ref, ds2_ref, dg_ref, w_ref, g_ref, dx_ref, acc_ref):
        m = pl.program_id(0)

        @pl.when(m == 0)
        def _():
            acc_ref[...] = jnp.zeros_like(acc_ref)

        dss_tok = [_load_streams(ds_ref, dil, tm) for ds_ref, dil in zip((ds0_ref, ds1_ref, ds2_ref), DILATIONS)]
        dqkv = jnp.concatenate([ds[:, j * GROUP_W:(j + 1) * GROUP_W] for j in range(3) for ds in dss_tok], axis=1)
        dh = (_dot(da_ref[...], w_ref[0:qkv0, :]) + _dot(dqkv, w_ref[qkv0:qkv0 + 3 * ATTN_W, :])
              + _dot(dg_ref[...], w_ref[qkv0 + 3 * ATTN_W:n, :]))
        dx, dg = _rms_bwd(x_ref[...], g_ref[...], dh)
        dx_ref[...] = dx1_ref[...] + dx
        acc_ref[...] += _stack_rows([_colsum(dg)], d)

    return _call(
        body, name="in_proj_bwd", grid=(t // tm,),
        in_specs=[_rows(tm, d), _rows(tm, d), _rows(tm, qkv0)]
        + [_stream_spec(dil, tm, 3 * GROUP_W) for dil in DILATIONS]
        + [_rows(tm, 2 * d), _resident((n, d)), _resident((1, d))],
        out_specs=[_rows(tm, d), _acc_spec(d)],
        out_shape=[jax.ShapeDtypeStruct((t, d), F32), jax.ShapeDtypeStruct((SUBLANES, d), F32)],
        args=(x, dx1, dabcv, *[_stream_view(a, dil) for a, dil in zip(dss, DILATIONS)], dgates, w_in, g1),
        semantics=("arbitrary",), carry=carry)


def _dw_in_qkv(ds, h, dil):
    t, d = h.shape
    tk = min(2048, t)
    sub = min(256, t)
    width = 3 * GROUP_W

    def body(ds_ref, h_ref, o_ref, acc_ref):
        k = pl.program_id(0)

        @pl.when(k == 0)
        def _():
            acc_ref[...] = jnp.zeros_like(acc_ref)

        upd = None
        for i in range(tk // sub):
            rows = ds_ref[:, i * (sub // dil):(i + 1) * (sub // dil), :].reshape(sub, width)
            if dil > 1:
                rows = _permute_rows(_perm(dil, sub, inverse=True), rows)
            term = _dot_tn(rows, h_ref[i * sub:(i + 1) * sub, :])
            upd = term if upd is None else upd + term
        acc_ref[...] += upd

        @pl.when(k == t // tk - 1)
        def _():
            o_ref[...] = acc_ref[...].astype(BF16)

    return pl.pallas_call(
        body, name=f"dw_in_qkv_d{dil}", grid=(t // tk,),
        in_specs=[_stream_spec(dil, tk, width), _rows(tk, d)],
        out_specs=pl.BlockSpec((width, d), lambda k: (0, 0)),
        out_shape=jax.ShapeDtypeStruct((width, d), BF16),
        scratch_shapes=[pltpu.VMEM((width, d), F32)],
        compiler_params=_params("arbitrary"),
    )(_stream_view(ds, dil), h)


def _local_step(x, target, p, late):
    cw = p["conv_a_w"].shape[1]
    (h, abcv, gates, *ss), (g_up,) = _in_proj(x, p["norm_mix_g"], p["w_in"], cw,
                                              carry=_Exchange("gather", [late["w_up"]]))
    w_up = _full_from_gathered(g_up)
    (o0, lse0), g_proj = _attn_fwd(ss[0], DILATIONS[0],
                                   carry=_Exchange("gather", [late["w_proj_a"], late["w_proj_b"]]))
    (o1, lse1), (g_out,) = _attn_fwd(ss[1], DILATIONS[1], carry=_Exchange("gather", [late["w_out"]]))
    o2, lse2 = _attn_fwd(ss[2], DILATIONS[2])
    w_pa, w_pb, w_out = [_full_from_gathered(g) for g in (*g_proj, g_out)]
    os, lses = (o0, o1, o2), (lse0, lse1, lse2)
    (x1, ya, yb, yap, ybp, merged), (g_down,) = _mixer_out(
        x, abcv, gates, os, lses, p["conv_a_w"], p["conv_a_b"], p["b_gate"], w_pa, w_pb, w_out,
        carry=_Exchange("gather", [late["w_down"]]))
    w_down = _full_from_gathered(g_down)
    h2, up_pre, act, conv, dx2, dx2i, acc_gf, loss = _ffn_fwd(x1, target, p["norm_ffn_g"], w_up, p["ffn_conv_w"],
                                                              p["ffn_conv_b"], w_down, p["final_norm_g"])

    parts, got = {}, {}
    dup, acc_fb = _ffn_act_bwd(dx2i, conv, w_down)
    parts["w_down"] = _by_destination(_tn_matmul(act, dx2i, "dw_down"))
    (dpre, dx1, acc_g2, acc_fw), (got["w_down"],) = _ffn_up_bwd(dup, up_pre, x1, dx2, p["ffn_conv_w"], w_up,
                                                                p["norm_ffn_g"],
                                                                carry=_Exchange("scatter", [parts["w_down"]]))
    parts["w_up"] = _by_destination(_tn_matmul(dpre, h2, "dw_up"))
    dgates, dyap, dybp, dya, dos, dls, acc_bg = _mixer_bwd(dx1, gates, yap, ybp, os, lses, p["b_gate"], w_out,
                                                           w_pa, w_pb)
    parts["w_out"] = _by_destination(_tn_matmul(merged, dx1, "dw_out"))
    parts["w_proj_a"] = _by_destination(_tn_matmul(dyap, ya, "dw_proj_a"))
    parts["w_proj_b"] = _by_destination(_tn_matmul(dybp, yb, "dw_proj_b"))
    minor = ("w_out", "w_proj_a", "w_proj_b")
    half = parts["w_up"].shape[1] // 2
    (ds0,), received = _attn_bwd(ss[0], dos[0], lses[0], dls[0], DILATIONS[0],
                                 carry=_Exchange("scatter", [parts[n] for n in minor]))
    got.update(zip(minor, received))
    (ds1,), first_half = _attn_bwd(ss[1], dos[1], lses[1], dls[1], DILATIONS[1],
                                   carry=_Exchange("scatter", [parts["w_up"]], rows=(0, half)))
    (ds2,), (got["w_up"],) = _attn_bwd(ss[2], dos[2], lses[2], dls[2], DILATIONS[2],
                                       carry=_Exchange("scatter", [parts["w_up"]], rows=(half, half),
                                                       into=first_half))
    dss = [ds0, ds1, ds2]
    dabcv, acc_ca = _conv_mixer_bwd(abcv, dya, p["conv_a_w"], p["conv_a_b"])
    dw_s = [_dw_in_qkv(ds, h, dil) for ds, dil in zip(dss, DILATIONS)]
    dw_qkv = [w[j * GROUP_W:(j + 1) * GROUP_W] for j in range(3) for w in dw_s]
    g_w_in = jnp.concatenate([_tn_matmul(dabcv, h, "dw_in_a"), *dw_qkv, _tn_matmul(dgates, h, "dw_in_g")], axis=0)
    parts["w_in"] = _by_destination(g_w_in)
    (dx, acc_g1), (got["w_in"],) = _in_proj_bwd(x, dx1, dabcv, dss, dgates, p["w_in"], p["norm_mix_g"],
                                                carry=_Exchange("scatter", [parts["w_in"]]))
    small = dict(norm_mix_g=acc_g1[0:1], b_gate=acc_bg[0:2], conv_a_w=acc_ca[0:3], conv_a_b=acc_ca[3:4],
                 norm_ffn_g=acc_g2[0:1], ffn_conv_w=acc_fw[0:3], ffn_conv_b=acc_fb[0:1], final_norm_g=acc_gf[0:1])
    return loss[0, 0], dx, parts, got, small


def _all_gather(shards):
    n = len(shards)

    def body(*refs):
        ins, outs = refs[:n], refs[n:2 * n]
        send_sems, recv_sems, local_sems = refs[2 * n:]
        x, y, c = _mesh_pos()
        me, sibling = (x, y, c), (x, y, 1 - c)
        chips = [(1 - x, y), (x, 1 - y), (1 - x, 1 - y)]

        def copy(i, k, block, to, src=None):
            rows = outs[i].at[_dev_index(*block)]
            return pltpu.make_async_remote_copy(
                src_ref=rows if src is None else src, dst_ref=rows, send_sem=send_sems.at[i, k],
                recv_sem=recv_sems.at[i, k], device_id=to, device_id_type=MESH)

        mine, first, passed = [], [], []
        for i in range(n):
            cp = pltpu.make_async_copy(ins[i], outs[i].at[_dev_index(*me)], local_sems.at[i])
            cp.start()
            mine.append(cp)
            first.append(copy(i, 0, me, sibling, src=ins[i]))
            first += [copy(i, 1 + j, me, (*chip, c), src=ins[i]) for j, chip in enumerate(chips)]
        for cp in first:
            cp.start()
        for i in range(n):
            for j, chip in enumerate(chips):
                copy(i, 1 + j, (*chip, c), me).wait_recv()
                fw = copy(i, 4 + j, (*chip, c), sibling)
                fw.start()
                passed.append(fw)
        for i in range(n):
            copy(i, 0, sibling, me).wait_recv()
            for j, chip in enumerate(chips):
                copy(i, 4 + j, (*chip, 1 - c), me).wait_recv()
        for cp in first + passed:
            cp.wait_send()
        for cp in mine:
            cp.wait()

    return pl.pallas_call(
        body, name="all_gather_weights",
        in_specs=[ANY] * n, out_specs=[ANY] * n,
        out_shape=[jax.ShapeDtypeStruct((N_DEV,) + s.shape, s.dtype) for s in shards],
        scratch_shapes=[pltpu.SemaphoreType.DMA((n, 7)), pltpu.SemaphoreType.DMA((n, 7)),
                        pltpu.SemaphoreType.DMA((n,))],
    )(*shards)


def _all_reduce_small(v):
    r = v.shape[0]

    def body(v_ref, o_ref, gath, send_sems, recv_sems):
        x, y, c = _mesh_pos()
        me = _dev_index(x, y, c)
        gath[me] = v_ref[...]
        flips = [(kx, ky, kc) for kx in (0, 1) for ky in (0, 1) for kc in (0, 1)][1:]
        copies = []
        for k, (kx, ky, kc) in enumerate(flips):
            px = 1 - x if kx else x
            py = 1 - y if ky else y
            pc = 1 - c if kc else c
            cp = pltpu.make_async_remote_copy(
                src_ref=v_ref, dst_ref=gath.at[me], send_sem=send_sems.at[k], recv_sem=recv_sems.at[k],
                device_id=(px, py, pc), device_id_type=MESH)
            cp.start()
            copies.append((cp, _dev_index(px, py, pc)))
        for k, (cp, peer) in enumerate(copies):
            pltpu.make_async_remote_copy(
                src_ref=v_ref, dst_ref=gath.at[peer], send_sem=send_sems.at[k], recv_sem=recv_sems.at[k],
                device_id=(x, y, c), device_id_type=MESH).wait_recv()
        for cp, _ in copies:
            cp.wait_send()
        total = gath[0]
        for j in range(1, N_DEV):
            total = total + gath[j]
        o_ref[...] = total

    return pl.pallas_call(
        body, name="all_reduce_small",
        in_specs=[pl.BlockSpec(memory_space=pltpu.VMEM)], out_specs=pl.BlockSpec(memory_space=pltpu.VMEM),
        out_shape=jax.ShapeDtypeStruct((r, LANES), F32),
        scratch_shapes=[pltpu.VMEM((N_DEV, r, LANES), F32), pltpu.SemaphoreType.DMA((7,)),
                        pltpu.SemaphoreType.DMA((7,))],
    )(v)


def _adamw_math(w, g, m, v):
    m2 = ADAM_B1 * m + (1.0 - ADAM_B1) * g
    v2 = ADAM_B2 * v + (1.0 - ADAM_B2) * (g * g)
    m_hat = m2 / (1.0 - ADAM_B1 ** ADAM_STEP)
    v_hat = v2 / (1.0 - ADAM_B2 ** ADAM_STEP)
    delta = -ADAM_LR * (m_hat / (jnp.sqrt(v_hat) + ADAM_EPS) + ADAM_WD * w)
    return delta, m2, v2


def _adamw_big(w, m, v, part, got, me):
    r, c = part.shape[1:]
    flip = w.shape != (r, c)
    tr = r if flip else max(t for t in range(HALO, min(r, 512) + 1, HALO) if r % t == 0)

    def body(me_ref, w_ref, m_ref, v_ref, own_ref, *rest):
        del me_ref
        got_refs, (g_out, d_out, m_out, v_out) = rest[:N_DEV - 1], rest[N_DEV - 1:]
        g = own_ref[...].astype(F32)
        for ref in got_refs:
            g = g + ref[...].astype(F32)
        if flip:
            g = g.T
        delta, m2, v2 = _adamw_math(w_ref[...], g, m_ref[...], v_ref[...])
        g_out[...] = g
        d_out[...] = delta
        m_out[...] = m2
        v_out[...] = v2

    def peer_block(k):
        return pl.BlockSpec((None, tr, c), lambda i, me_ref: (jnp.bitwise_xor(me_ref[0], k), i, 0))

    plain = pl.BlockSpec(w.shape if flip else (tr, c), lambda i, me_ref: (i, 0))
    out = jax.ShapeDtypeStruct(w.shape, F32)
    return pl.pallas_call(
        body, name="adamw_big",
        grid_spec=pltpu.PrefetchScalarGridSpec(
            num_scalar_prefetch=1, grid=(r // tr,),
            in_specs=[plain, plain, plain] + [peer_block(k) for k in range(N_DEV)],
            out_specs=[plain] * 4),
        out_shape=[out] * 4,
        compiler_params=_params("parallel"),
    )(me, w, m, v, part, *([got] * (N_DEV - 1)))


def _adamw_small(ws, gs, ms, vs):
    n = len(ws)

    def body(*refs):
        ins, outs = refs[:4 * n], refs[4 * n:]
        for i in range(n):
            delta, m2, v2 = _adamw_math(ins[i][...], ins[n + i][...], ins[2 * n + i][...], ins[3 * n + i][...])
            outs[i][...] = delta
            outs[n + i][...] = m2
            outs[2 * n + i][...] = v2

    out = [jax.ShapeDtypeStruct(w.shape, F32) for w in ws]
    res = pl.pallas_call(body, name="adamw_small", out_shape=out * 3)(*ws, *gs, *ms, *vs)
    return res[:n], res[n:2 * n], res[2 * n:]


BIG = ("w_in", "w_proj_a", "w_proj_b", "w_out", "w_up", "w_down")
LATE = ("w_proj_a", "w_proj_b", "w_out", "w_up", "w_down")
COLUMN_SHARDED = ("w_in", "w_proj_a", "w_proj_b", "w_up")
WIDE_COLUMN_SHARDED = ("w_in", "w_up")
SMALL = ("norm_mix_g", "b_gate", "conv_a_w", "conv_a_b", "norm_ffn_g", "ffn_conv_w", "ffn_conv_b", "final_norm_g")
SMALL_SHARDED = ("b_gate", "conv_a_w", "ffn_conv_w")
WEIGHTS = ("norm_mix_g", "w_in", "b_gate", "conv_a_w", "conv_a_b", "w_proj_a", "w_proj_b", "w_out", "norm_ffn_g",
           "w_up", "ffn_conv_w", "ffn_conv_b", "w_down", "final_norm_g")


def _pack(vectors, rows):
    flat = jnp.concatenate([v.reshape(-1) for v in vectors])
    return jnp.pad(flat, (0, rows * LANES - flat.shape[0])).reshape(rows, LANES)


def _packed_rows(count):
    rows = -(-count // LANES)
    return -(-rows // SUBLANES) * SUBLANES


def _unpack(packed, shapes):
    flat = packed.reshape(-1)
    out, lo = [], 0
    for s in shapes:
        size = 1
        for dim in s:
            size *= dim
        out.append(flat[lo:lo + size].reshape(s))
        lo += size
    return out


def _full_from_gathered(gathered):
    _, r, c = gathered.shape
    return gathered.reshape(N_DEV * r, c)


def _by_destination(grad):
    rr, cc = grad.shape
    return grad.reshape(N_DEV, rr // N_DEV, cc)


def _block2d(name, a):
    a = a.reshape(a.shape[-2:])
    return a.T if name in WIDE_COLUMN_SHARDED else a


def kernel(x, norm_mix_g, w_in, b_gate, conv_a_w, conv_a_b, w_proj_a, w_proj_b, w_out, norm_ffn_g, w_up, ffn_conv_w, ffn_conv_b, w_down, final_norm_g, loss_target, m_norm_mix_g, m_w_in, m_b_gate, m_conv_a_w, m_conv_a_b, m_w_proj_a, m_w_proj_b, m_w_out, m_norm_ffn_g, m_w_up, m_ffn_conv_w, m_ffn_conv_b, m_w_down, m_final_norm_g, v_norm_mix_g, v_w_in, v_b_gate, v_conv_a_w, v_conv_a_b, v_w_proj_a, v_w_proj_b, v_w_out, v_norm_ffn_g, v_w_up, v_ffn_conv_w, v_ffn_conv_b, v_w_down, v_final_norm_g):
    given = dict(locals())
    shard = {n: given[n] for n in WEIGHTS}
    mom_m = {n: given["m_" + n] for n in WEIGHTS}
    mom_v = {n: given["v_" + n] for n in WEIGHTS}
    xi, yi, ci = _mesh_pos()
    me = _dev_index(xi, yi, ci)
    me1 = me.astype(jnp.int32).reshape(1)

    big2d = {n: _block2d(n, shard[n]) for n in BIG}
    small_shapes = [shard[n].shape[1:] for n in SMALL_SHARDED]
    n_small = sum(s[0] * s[1] for s in small_shapes)
    packed_small = _pack([shard[n] for n in SMALL_SHARDED], _packed_rows(n_small))
    gathered = _all_gather([big2d["w_in"].astype(BF16), packed_small])
    p = {"w_in": _full_from_gathered(gathered[0])}
    flat_small = gathered[-1].reshape(N_DEV, -1)
    lo = 0
    for n, (rows, width) in zip(SMALL_SHARDED, small_shapes):
        blocks = flat_small[:, lo:lo + rows * width].reshape(N_DEV, rows, width)
        p[n] = blocks.transpose(1, 0, 2).reshape(rows, N_DEV * width)
        lo += rows * width
    p["norm_mix_g"], p["norm_ffn_g"] = shard["norm_mix_g"], shard["norm_ffn_g"]
    p["conv_a_b"], p["ffn_conv_b"] = shard["conv_a_b"], shard["ffn_conv_b"]
    p["final_norm_g"] = shard["final_norm_g"].reshape(1, -1)
    late = {n: (big2d[n].T if n in ("w_proj_a", "w_proj_b") else big2d[n]).astype(BF16) for n in LATE}

    loss_part, dx, parts, got, g_small = _local_step(x[0], loss_target[0], p, late)

    results = {}
    for n in BIG:
        outs = _adamw_big(big2d[n], _block2d(n, mom_m[n]), _block2d(n, mom_v[n]), parts[n], got[n], me1)
        results[n] = [_block2d(n, o).reshape(shard[n].shape) for o in outs]

    small_full_shapes = [g_small[n].shape for n in SMALL]
    n_vec = sum(s[0] * s[1] for s in small_full_shapes) + 1
    packed = _pack([g_small[n] for n in SMALL] + [loss_part.reshape(1)], _packed_rows(n_vec))
    reduced = _all_reduce_small(packed)
    *g_full, loss_vec = _unpack(reduced, small_full_shapes + [(1,)])
    loss = loss_vec[0]
    own_g = []
    for n, g in zip(SMALL, g_full):
        if n in SMALL_SHARDED:
            width = shard[n].shape[-1]
            g = lax.dynamic_slice_in_dim(g, me * width, width, axis=1)
        own_g.append(g.reshape(shard[n].shape))
    def rows2d(a):
        return a.reshape(-1, a.shape[-1])

    deltas, new_ms, new_vs = _adamw_small([rows2d(shard[n]) for n in SMALL], [rows2d(g) for g in own_g],
                                          [rows2d(mom_m[n]) for n in SMALL], [rows2d(mom_v[n]) for n in SMALL])
    for i, n in enumerate(SMALL):
        results[n] = [own_g[i]] + [a.reshape(shard[n].shape) for a in (deltas[i], new_ms[i], new_vs[i])]

    grad_x = dx.reshape(x.shape)
    return (loss, grad_x, *[results[n][0] for n in WEIGHTS], *[results[n][1] for n in WEIGHTS],
            *[results[n][2] for n in WEIGHTS], *[results[n][3] for n in WEIGHTS])
```

```python
import functools

import jax
import jax.numpy as jnp
from jax import lax
from jax.experimental import pallas as pl
from jax.experimental.pallas import tpu as pltpu

F32 = jnp.float32
BF16 = jnp.bfloat16
MESH = pl.DeviceIdType.MESH

N_DEV = 8
RMS_EPS = 1e-6
NEG_INF = -1e30
N_GROUPS = 3
DILATIONS = (1, 4, 16)
HEADS_PER_GROUP = 4
HEAD_DIM = 64
GROUP_W = HEADS_PER_GROUP * HEAD_DIM
ATTN_W = N_GROUPS * GROUP_W
QBLK = 128
STEP_BLOCKS = 4
BWD_STEP_BLOCKS = 2
ATTN_SCALE = HEAD_DIM ** -0.5

ADAM_LR = 0.001
ADAM_B1 = 0.9
ADAM_B2 = 0.999
ADAM_EPS = 1e-08
ADAM_WD = 0.01
ADAM_STEP = 10

PERM_TOKENS = 256
ROWS_MATMUL = 512
HALO = 16
LANES = 128
SUBLANES = 8
VMEM_LIMIT_BYTES = 56 * 1024 * 1024


def _params(*sem):
    return pltpu.CompilerParams(dimension_semantics=sem, vmem_limit_bytes=VMEM_LIMIT_BYTES)


def _pick_tile(n, cap):
    if n <= cap:
        return n
    best = None
    for t in range(LANES, cap + 1, LANES):
        if n % t == 0:
            best = t
    assert best is not None, (n, cap)
    return best


def _rows(tm, c, j=0):
    return pl.BlockSpec((tm, c), lambda m: (m, j))


def _prev_halo(tm, c):
    return pl.BlockSpec((HALO, c), lambda m: (jnp.maximum(m * (tm // HALO) - 1, 0), 0))


def _next_halo(tm, c, t_total):
    last = t_total // HALO - 1
    return pl.BlockSpec((HALO, c), lambda m: (jnp.minimum((m + 1) * (tm // HALO), last), 0))


def _resident(shape):
    nd = len(shape)
    return pl.BlockSpec(shape, lambda *_: (0,) * nd, pipeline_mode=pl.Buffered(1))


def _acc_spec(c):
    return pl.BlockSpec((SUBLANES, c), lambda *_: (0, 0))


def _shift_down(u, halo, k):
    edge = jnp.concatenate([halo[HALO - SUBLANES:], u[:SUBLANES]], axis=0)
    head = pltpu.roll(edge, k, 0)[SUBLANES:]
    return jnp.concatenate([head, pltpu.roll(u, k, 0)[SUBLANES:]], axis=0)


def _shift_up(u, halo, k):
    n = u.shape[0]
    edge = jnp.concatenate([u[n - SUBLANES:], halo[:SUBLANES]], axis=0)
    tail = pltpu.roll(edge, 2 * SUBLANES - k, 0)[:SUBLANES]
    return jnp.concatenate([pltpu.roll(u, n - k, 0)[:n - SUBLANES], tail], axis=0)


def _interleave(tm, inverse=False):
    return _perm(tm // SUBLANES, tm, inverse)


def _edge_groups(u, halo, k, from_end):
    n = u.shape[0]
    sub = lax.broadcasted_iota(jnp.int32, (SUBLANES, u.shape[1]), 0)
    out = []
    for j in range(2 - k, 2):
        lo = n - HALO + j * SUBLANES if from_end else j * SUBLANES
        own, other = u[lo:lo + SUBLANES], halo[j * SUBLANES:(j + 1) * SUBLANES]
        if from_end:
            out.append(pltpu.roll(jnp.where(sub == SUBLANES - 1, other, own), 1, 0))
        else:
            out.append(pltpu.roll(jnp.where(sub == 0, other, own), SUBLANES - 1, 0))
    return out


def _shift_down_il(u, halo, k):
    return jnp.concatenate(_edge_groups(u, halo, k, True) + [u[:u.shape[0] - k * SUBLANES]], axis=0)


def _shift_up_il(u, halo, k):
    if k == 1:
        edge = _edge_groups(u, halo, 2, False)[:1]
    else:
        edge = _edge_groups(u, halo, 2, False)
    return jnp.concatenate([u[k * SUBLANES:]] + edge, axis=0)


def _stack_rows(rows, c):
    idx = lax.broadcasted_iota(jnp.int32, (SUBLANES, c), 0)
    out = jnp.zeros((SUBLANES, c), F32)
    for i, r in enumerate(rows):
        out = out + jnp.where(idx == i, r, 0.0)
    return out


def _colsum(v):
    return jnp.sum(v, axis=0, keepdims=True)


def _sigmoid(v):
    return 0.5 * jnp.tanh(0.5 * v) + 0.5


def _rms_fwd(xv, g):
    r = lax.rsqrt(jnp.mean(xv * xv, axis=-1, keepdims=True) + RMS_EPS)
    return xv * r * g, r


def _rms_bwd(xv, g, dy):
    r = lax.rsqrt(jnp.mean(xv * xv, axis=-1, keepdims=True) + RMS_EPS)
    xn = xv * r
    dxn = dy * g
    dx = r * (dxn - xn * jnp.mean(dxn * xn, axis=-1, keepdims=True))
    return dx, dy * xn


def _dot(a, b):
    return jnp.dot(a, b, preferred_element_type=F32)


def _dot_nt(a, b):
    return lax.dot_general(a, b, (((1,), (1,)), ((), ())), preferred_element_type=F32)


def _dot_tn(a, b):
    return lax.dot_general(a, b, (((0,), (0,)), ((), ())), preferred_element_type=F32)


def _perm(dil, n, inverse=False):
    i = lax.broadcasted_iota(jnp.int32, (n, n), 0)
    j = lax.broadcasted_iota(jnp.int32, (n, n), 1)
    if inverse:
        i, j = j, i
    per = n // dil
    return (j == (i % per) * dil + i // per).astype(BF16)


def _permute_rows(pm, v):
    if v.dtype == BF16:
        return _dot(pm, v).astype(BF16)
    hi = v.astype(BF16)
    lo = (v - hi.astype(F32)).astype(BF16)
    return _dot(pm, hi) + _dot(pm, lo)


def _stream_view(a, dil):
    t, c = a.shape
    return a.reshape(dil, t // dil, c)


def _stream_spec(dil, tm, c):
    return pl.BlockSpec((dil, tm // dil, c), lambda m: (0, m, 0))


def _load_streams(ref, dil, tm):
    c = ref.shape[-1]
    if dil == 1:
        return ref[...].reshape(tm, c)
    sub = min(PERM_TOKENS, tm)
    pm = _perm(dil, sub, inverse=True)
    parts = [_permute_rows(pm, ref[:, i * (sub // dil):(i + 1) * (sub // dil), :].reshape(sub, c))
             for i in range(tm // sub)]
    return parts[0] if len(parts) == 1 else jnp.concatenate(parts, axis=0)


def _store_streams(ref, dil, tm, v):
    if dil == 1:
        ref[...] = v.reshape(ref.shape).astype(ref.dtype)
        return
    sub = min(PERM_TOKENS, tm)
    pm = _perm(dil, sub)
    for i in range(tm // sub):
        piece = _permute_rows(pm, v[i * sub:(i + 1) * sub])
        ref[:, i * (sub // dil):(i + 1) * (sub // dil), :] = piece.reshape(dil, sub // dil, -1).astype(ref.dtype)


ANY = pl.BlockSpec(memory_space=pl.ANY)


def _mesh_pos():
    return lax.axis_index("x"), lax.axis_index("y"), lax.axis_index("c")


def _dev_index(px, py, pc):
    return 4 * px + 2 * py + pc


class _Exchange:
    def __init__(self, mode, arrays, rows=None, into=()):
        self.mode, self.arrays, self.rows, self.into = mode, list(arrays), rows, list(into)
        n = len(self.arrays)
        if mode == "gather":
            self.out_shape = [jax.ShapeDtypeStruct((N_DEV,) + a.shape, a.dtype) for a in self.arrays]
        else:
            self.out_shape = [jax.ShapeDtypeStruct(a.shape, a.dtype) for a in self.arrays]
        self.scratch = [pltpu.SemaphoreType.DMA((n, N_DEV - 1)), pltpu.SemaphoreType.DMA((n, N_DEV - 1)),
                        pltpu.SemaphoreType.DMA((n,))]

    def _peers(self):
        x, y, c = _mesh_pos()
        flips = [(kx, ky, kc) for kx in (0, 1) for ky in (0, 1) for kc in (0, 1)][1:]
        peers = [(1 - x if kx else x, 1 - y if ky else y, 1 - c if kc else c) for kx, ky, kc in flips]
        return _dev_index(x, y, c), peers

    def _copy(self, ins, outs, sems, i, k, peer, me, sending):
        src = ins[i] if self.mode == "gather" else ins[i].at[_dev_index(*peer)]
        dst = outs[i].at[me if sending else _dev_index(*peer)]
        if self.rows is not None:
            src, dst = src.at[pl.ds(*self.rows)], dst.at[pl.ds(*self.rows)]
        return pltpu.make_async_remote_copy(src_ref=src, dst_ref=dst, send_sem=sems[0].at[i, k],
                                            recv_sem=sems[1].at[i, k], device_id=peer, device_id_type=MESH)

    def _own(self, ins, outs, sems, i, me):
        return pltpu.make_async_copy(ins[i], outs[i].at[me], sems[2].at[i])

    def start(self, ins, outs, sems):
        me, peers = self._peers()
        for i in range(len(ins)):
            if self.mode == "gather":
                self._own(ins, outs, sems, i, me).start()
            for k, peer in enumerate(peers):
                self._copy(ins, outs, sems, i, k, peer, me, True).start()

    def wait(self, ins, outs, sems):
        me, peers = self._peers()
        for i in range(len(ins)):
            for k, peer in enumerate(peers):
                self._copy(ins, outs, sems, i, k, peer, me, False).wait_recv()
            for k, peer in enumerate(peers):
                self._copy(ins, outs, sems, i, k, peer, me, True).wait_send()
            if self.mode == "gather":
                self._own(ins, outs, sems, i, me).wait()


def _call(body, *, name, grid, in_specs, out_specs, out_shape, args, semantics, carry=None, scratch=()):
    if carry is None:
        return pl.pallas_call(body, name=name, grid=grid, in_specs=in_specs, out_specs=out_specs,
                              out_shape=out_shape, scratch_shapes=list(scratch),
                              compiler_params=_params(*semantics))(*args)
    n_in, n_out, n_x, n_s = len(in_specs), len(out_specs), len(carry.arrays), len(scratch)
    n_into = len(carry.into)
    all_in = n_in + n_x + n_into

    def carried(*refs):
        ins, x_ins = refs[:n_in], refs[n_in:n_in + n_x]
        outs = refs[all_in:all_in + n_out]
        x_outs = refs[all_in + n_out:all_in + n_out + n_x]
        own = refs[all_in + n_out + n_x:all_in + n_out + n_x + n_s]
        sems = refs[all_in + n_out + n_x + n_s:]
        first = functools.reduce(jnp.logical_and, [pl.program_id(a) == 0 for a in range(len(grid))])
        last = functools.reduce(jnp.logical_and, [pl.program_id(a) == grid[a] - 1 for a in range(len(grid))])

        @pl.when(first)
        def _():
            carry.start(x_ins, x_outs, sems)

        body(*ins, *outs, *own)

        @pl.when(last)
        def _():
            carry.wait(x_ins, x_outs, sems)

    res = pl.pallas_call(
        carried, name=name, grid=grid, in_specs=list(in_specs) + [ANY] * (n_x + n_into),
        out_specs=list(out_specs) + [ANY] * n_x, out_shape=list(out_shape) + carry.out_shape,
        input_output_aliases={n_in + n_x + i: n_out + i for i in range(n_into)},
        scratch_shapes=list(scratch) + carry.scratch, compiler_params=_params(*["arbitrary"] * len(grid)),
    )(*args, *carry.arrays, *carry.into)
    return list(res[:n_out]), list(res[n_out:])


def _in_proj(x, g, wt, cw, carry=None):
    t, d = x.shape
    n = wt.shape[0]
    tm = min(ROWS_MATMUL, t)
    qkv0 = 3 * cw

    def body(x_ref, g_ref, wt_ref, h_ref, abcv_ref, gates_ref, *s_refs):
        h = _rms_fwd(x_ref[...], g_ref[...])[0].astype(BF16)
        h_ref[...] = h
        abcv_ref[...] = _dot_nt(h, wt_ref[0:qkv0, :]).astype(BF16)
        gates_ref[...] = _dot_nt(h, wt_ref[qkv0 + 3 * ATTN_W:n, :]).astype(BF16)
        qkv = _dot_nt(h, wt_ref[qkv0:qkv0 + 3 * ATTN_W, :]).astype(BF16)
        for gi, s_ref in enumerate(s_refs):
            cols = [qkv[:, j * ATTN_W + gi * GROUP_W:j * ATTN_W + (gi + 1) * GROUP_W] for j in range(3)]
            _store_streams(s_ref, DILATIONS[gi], tm, jnp.concatenate(cols, axis=1))

    return _call(
        body, name="in_proj", grid=(t // tm,),
        in_specs=[_rows(tm, d), _resident((1, d)), _resident((n, d))],
        out_specs=[_rows(tm, d), _rows(tm, qkv0), _rows(tm, 2 * d)]
        + [_stream_spec(dil, tm, 3 * GROUP_W) for dil in DILATIONS],
        out_shape=[jax.ShapeDtypeStruct((t, d), BF16), jax.ShapeDtypeStruct((t, qkv0), BF16),
                   jax.ShapeDtypeStruct((t, 2 * d), BF16)]
        + [jax.ShapeDtypeStruct((dil, t // dil, 3 * GROUP_W), BF16) for dil in DILATIONS],
        args=(x, g, wt), semantics=("parallel",), carry=carry)


def _head_masks():
    lane = lax.broadcasted_iota(jnp.int32, (1, GROUP_W), 1)
    return lane, [(lane // HEAD_DIM) == h for h in range(HEADS_PER_GROUP)]


def _stack_heads(v, heads):
    return jnp.concatenate([jnp.where(hm, v, jnp.zeros_like(v)) for hm in heads], axis=0)


def _merge_heads(v, heads):
    out = jnp.zeros((QBLK, GROUP_W), v.dtype)
    for h, hm in enumerate(heads):
        out = jnp.where(hm, v[h * QBLK:(h + 1) * QBLK], out)
    return out


def _pair_block(col, count=STEP_BLOCKS):
    return pl.BlockSpec((count * QBLK, GROUP_W), lambda b: (b, col))


def _edge_block(col, shift, nb, count=STEP_BLOCKS):
    return pl.BlockSpec((QBLK, GROUP_W), lambda b: (jnp.clip(count * b + shift, 0, nb - 1), col))


def _band_mask(has_prev):
    rows = HEADS_PER_GROUP * QBLK
    row = lax.broadcasted_iota(jnp.int32, (rows, 2 * QBLK), 0) & (QBLK - 1)
    col = lax.broadcasted_iota(jnp.int32, (rows, 2 * QBLK), 1)
    return ((col < QBLK) & (col >= row) & has_prev) | ((col >= QBLK) & (col - QBLK <= row))


def _next_mask(has_next):
    rows = HEADS_PER_GROUP * QBLK
    row = lax.broadcasted_iota(jnp.int32, (rows, QBLK), 0) & (QBLK - 1)
    col = lax.broadcasted_iota(jnp.int32, (rows, QBLK), 1)
    return (col >= row) & has_next


def _attn_fwd(s, dil, carry=None):
    t = s.shape[0] * s.shape[1]
    nb = t // QBLK
    per_stream = nb // dil
    assert per_stream % STEP_BLOCKS == 0

    def body(q_ref, kc_ref, kp_ref, vc_ref, vp_ref, o_ref, lse_ref):
        b = pl.program_id(0)
        _, heads = _head_masks()
        first_has_prev = lax.rem(STEP_BLOCKS * b, per_stream) != 0
        for j in range(STEP_BLOCKS):
            rows = slice(j * QBLK, (j + 1) * QBLK)
            if j == 0:
                k2 = jnp.concatenate([kp_ref[...], kc_ref[rows, :]], axis=0)
                v2 = jnp.concatenate([vp_ref[...], vc_ref[rows, :]], axis=0)
            else:
                both = slice((j - 1) * QBLK, (j + 1) * QBLK)
                k2, v2 = kc_ref[both, :], vc_ref[both, :]
            mask = _band_mask(first_has_prev if j == 0 else True)
            sc = jnp.where(mask, _dot_nt(_stack_heads(q_ref[rows, :], heads), k2) * ATTN_SCALE, NEG_INF)
            mx = jnp.max(sc, axis=1, keepdims=True)
            pr = jnp.exp(sc - mx)
            den = jnp.sum(pr, axis=1, keepdims=True)
            o_all = _dot(pr.astype(BF16), v2) / den
            o_ref[rows, :] = _merge_heads(o_all, heads).astype(BF16)
            lse_ref[rows, :] = _merge_heads(jnp.broadcast_to(mx + jnp.log(den), o_all.shape), heads)

    sv = s.reshape(t, 3 * GROUP_W)
    return _call(
        body, name=f"attn_fwd_d{dil}", grid=(nb // STEP_BLOCKS,),
        in_specs=[_pair_block(0), _pair_block(1), _edge_block(1, -1, nb), _pair_block(2), _edge_block(2, -1, nb)],
        out_specs=[_pair_block(0), _pair_block(0)],
        out_shape=[jax.ShapeDtypeStruct((t, GROUP_W), BF16), jax.ShapeDtypeStruct((t, GROUP_W), F32)],
        args=(sv, sv, sv, sv, sv), semantics=("parallel",), carry=carry)


def _group_softmax(parts):
    mx = jnp.maximum(jnp.maximum(parts[0], parts[1]), parts[2])
    es = [jnp.exp(p - mx) for p in parts]
    den = es[0] + es[1] + es[2]
    return [e / den for e in es]


def _mixer_out(x, abcv, gates, os, lses, conv_w, conv_b, b_gate, w_pa, w_pb, w_o, carry=None):
    t, d = x.shape
    cw = conv_w.shape[1]
    tm = min(ROWS_MATMUL, t)

    def body(x_ref, abcv_ref, halo_ref, gates_ref, o0_ref, o1_ref, o2_ref, l0_ref, l1_ref, l2_ref, cw_ref, cb_ref,
             bg_ref, wpa_ref, wpb_ref, wo_ref, x1_ref, ya_ref, yb_ref, yap_ref, ybp_ref, mg_ref):
        m = pl.program_id(0)
        ab = abcv_ref[:, 0:cw].astype(F32)
        u = abcv_ref[:, cw:2 * cw].astype(F32) * abcv_ref[:, 2 * cw:3 * cw].astype(F32)
        hu = halo_ref[:, cw:2 * cw].astype(F32) * halo_ref[:, 2 * cw:3 * cw].astype(F32)
        hu = jnp.where(m > 0, hu, 0.0)
        cv = (cw_ref[0:1, :] * _shift_down(u, hu, 2) + cw_ref[1:2, :] * _shift_down(u, hu, 1)
              + cw_ref[2:3, :] * u + cb_ref[...])
        ya = (ab * cv).astype(BF16)
        ya_ref[...] = ya
        alphas = _group_softmax([_load_streams(r, dil, tm) for r, dil in zip((l0_ref, l1_ref, l2_ref), DILATIONS)])
        for i, (o_ref, dil) in enumerate(zip((o0_ref, o1_ref, o2_ref), DILATIONS)):
            sl = slice(i * GROUP_W, (i + 1) * GROUP_W)
            yb_ref[:, sl] = (alphas[i] * _load_streams(o_ref, dil, tm).astype(F32)).astype(BF16)
        yap = _dot_nt(ya, wpa_ref[...])
        ybp = _dot_nt(yb_ref[...], wpb_ref[...])
        yap_ref[...] = yap.astype(BF16)
        ybp_ref[...] = ybp.astype(BF16)
        sa = _sigmoid(gates_ref[:, 0:d].astype(F32) + bg_ref[0:1, :])
        sb = _sigmoid(gates_ref[:, d:2 * d].astype(F32) + bg_ref[1:2, :])
        merged = (sa * yap + sb * ybp).astype(BF16)
        mg_ref[...] = merged
        x1_ref[...] = x_ref[...] + _dot(merged, wo_ref[...])

    return _call(
        body, name="mixer_out", grid=(t // tm,),
        in_specs=[_rows(tm, d), _rows(tm, 3 * cw), _prev_halo(tm, 3 * cw), _rows(tm, 2 * d)]
        + [_stream_spec(dil, tm, GROUP_W) for dil in DILATIONS] * 2
        + [_resident((3, cw)), _resident((1, cw)), _resident((2, d)),
           _resident((d, cw)), _resident((d, ATTN_W)), _resident((d, d))],
        out_specs=[_rows(tm, d), _rows(tm, cw), _rows(tm, ATTN_W), _rows(tm, d), _rows(tm, d), _rows(tm, d)],
        out_shape=[jax.ShapeDtypeStruct((t, d), F32), jax.ShapeDtypeStruct((t, cw), BF16),
                   jax.ShapeDtypeStruct((t, ATTN_W), BF16), jax.ShapeDtypeStruct((t, d), BF16),
                   jax.ShapeDtypeStruct((t, d), BF16), jax.ShapeDtypeStruct((t, d), BF16)],
        args=(x, abcv, abcv, gates, *[_stream_view(a, dil) for a, dil in zip(os, DILATIONS)],
              *[_stream_view(a, dil) for a, dil in zip(lses, DILATIONS)], conv_w, conv_b, b_gate, w_pa, w_pb, w_o),
        semantics=("parallel",), carry=carry)


def _ffn_fwd(x1, target, g2, w_ut, conv_w, conv_b, w_d, g_f, carry=None):
    t, d = x1.shape
    dff = w_d.shape[0]
    tm = min(256, t)
    ck = _pick_tile(dff, 2816)

    def body(x1_ref, tg_ref, g2_ref, wut_ref, cw_ref, cb_ref, wd_ref, gf_ref, h2_ref, up_ref, act_ref, conv_ref,
             dx2_ref, dx2i_ref, acc_ref, loss_ref, halo_ref):
        m = pl.program_id(0)

        @pl.when(m == 0)
        def _():
            acc_ref[...] = jnp.zeros_like(acc_ref)
            loss_ref[...] = jnp.zeros_like(loss_ref)
            halo_ref[...] = jnp.zeros_like(halo_ref)

        h2 = _permute_rows(_interleave(tm), _rms_fwd(x1_ref[...], g2_ref[...])[0].astype(BF16))
        h2_ref[...] = h2

        def conv(c0):
            p = _dot_nt(h2, wut_ref[c0:c0 + ck, :])
            up_ref[:, c0:c0 + ck] = p.astype(BF16)
            hp = halo_ref[:, c0:c0 + ck]
            halo_ref[:, c0:c0 + ck] = p[tm - HALO:, :]
            return (cw_ref[0:1, c0:c0 + ck] * _shift_down_il(p, hp, 2)
                    + cw_ref[1:2, c0:c0 + ck] * _shift_down_il(p, hp, 1)
                    + cw_ref[2:3, c0:c0 + ck] * p + cb_ref[:, c0:c0 + ck])

        down = jnp.zeros((tm, d), F32)
        for c0 in range(0, dff, ck):
            gate = conv(c0)
            val = conv(dff + c0)
            conv_ref[:, c0:c0 + ck] = gate.astype(BF16)
            conv_ref[:, dff + c0:dff + c0 + ck] = val.astype(BF16)
            act = (gate * _sigmoid(gate) * val).astype(BF16)
            act_ref[:, c0:c0 + ck] = act
            down = down + _dot(act, wd_ref[c0:c0 + ck, :])
        x2 = x1_ref[...] + _permute_rows(_interleave(tm, inverse=True), down)
        y, _ = _rms_fwd(x2, gf_ref[...])
        diff = y - tg_ref[...]
        loss_ref[...] += 0.5 * jnp.sum(jnp.mean(diff * diff, axis=-1, keepdims=True))
        dx2, dg = _rms_bwd(x2, gf_ref[...], diff * (1.0 / d))
        dx2_ref[...] = dx2
        dx2i_ref[...] = _permute_rows(_interleave(tm), dx2.astype(BF16))
        acc_ref[...] += _stack_rows([_colsum(dg)], d)

    return _call(
        body, name="ffn_fwd", grid=(t // tm,),
        in_specs=[_rows(tm, d), _rows(tm, d), _resident((1, d)), _resident((2 * dff, d)), _resident((3, 2 * dff)),
                  _resident((1, 2 * dff)), _resident((dff, d)), _resident((1, d))],
        out_specs=[_rows(tm, d), _rows(tm, 2 * dff), _rows(tm, dff), _rows(tm, 2 * dff), _rows(tm, d), _rows(tm, d),
                   _acc_spec(d), _acc_spec(LANES)],
        out_shape=[jax.ShapeDtypeStruct((t, d), BF16), jax.ShapeDtypeStruct((t, 2 * dff), BF16),
                   jax.ShapeDtypeStruct((t, dff), BF16), jax.ShapeDtypeStruct((t, 2 * dff), BF16),
                   jax.ShapeDtypeStruct((t, d), F32), jax.ShapeDtypeStruct((t, d), BF16),
                   jax.ShapeDtypeStruct((SUBLANES, d), F32), jax.ShapeDtypeStruct((SUBLANES, LANES), F32)],
        args=(x1, target, g2, w_ut, conv_w, conv_b, w_d, g_f), semantics=("arbitrary",), carry=carry,
        scratch=[pltpu.VMEM((HALO, 2 * dff), F32)])


def _ffn_act_bwd(dx2, conv, w_d):
    t, d = dx2.shape
    dff = w_d.shape[0]
    tm = min(256, t)
    ck = _pick_tile(dff, 2816)

    def body(dx2_ref, conv_ref, wd_ref, dup_ref, acc_ref):
        m = pl.program_id(0)

        @pl.when(m == 0)
        def _():
            acc_ref[...] = jnp.zeros_like(acc_ref)

        dx2v = dx2_ref[...]
        for c0 in range(0, dff, ck):
            dact = _dot_nt(dx2v, wd_ref[c0:c0 + ck, :])
            gate = conv_ref[:, c0:c0 + ck].astype(F32)
            val = conv_ref[:, dff + c0:dff + c0 + ck].astype(F32)
            sg = _sigmoid(gate)
            dval = dact * gate * sg
            dgate = dact * val * sg * (1.0 + gate * (1.0 - sg))
            dup_ref[:, c0:c0 + ck] = dgate.astype(BF16)
            dup_ref[:, dff + c0:dff + c0 + ck] = dval.astype(BF16)
            acc_ref[:, c0:c0 + ck] += _stack_rows([_colsum(dgate)], ck)
            acc_ref[:, dff + c0:dff + c0 + ck] += _stack_rows([_colsum(dval)], ck)

    return pl.pallas_call(
        body, name="ffn_act_bwd", grid=(t // tm,),
        in_specs=[_rows(tm, d), _rows(tm, 2 * dff), _resident((dff, d))],
        out_specs=[_rows(tm, 2 * dff), _acc_spec(2 * dff)],
        out_shape=[jax.ShapeDtypeStruct((t, 2 * dff), BF16), jax.ShapeDtypeStruct((SUBLANES, 2 * dff), F32)],
        compiler_params=_params("arbitrary"),
    )(dx2, conv, w_d)


def _ffn_up_bwd(dup, up_pre, x1, dx2, conv_w, w_u, g2, carry=None):
    t, d = x1.shape
    n = dup.shape[1]
    tm = min(256, t)
    ck = _pick_tile(n, 256)
    last = t // tm - 1

    def body(dup_ref, nxt_ref, up_ref, x1_ref, dx2_ref, cw_ref, wu_ref, g2_ref, dpre_ref, dx1_ref, acc_ref, accw_ref):
        m = pl.program_id(0)

        @pl.when(m == 0)
        def _():
            acc_ref[...] = jnp.zeros_like(acc_ref)
            accw_ref[...] = jnp.zeros_like(accw_ref)

        dh = jnp.zeros((tm, d), F32)
        for c0 in range(0, n, ck):
            du = dup_ref[:, c0:c0 + ck].astype(F32)
            hn = jnp.where(m < last, nxt_ref[:, c0:c0 + ck].astype(F32), 0.0)
            du1 = _shift_up_il(du, hn, 1)
            du2 = _shift_up_il(du, hn, 2)
            dpre = (cw_ref[2:3, c0:c0 + ck] * du + cw_ref[1:2, c0:c0 + ck] * du1
                    + cw_ref[0:1, c0:c0 + ck] * du2).astype(BF16)
            dpre_ref[:, c0:c0 + ck] = dpre
            dh = dh + _dot(dpre, wu_ref[c0:c0 + ck, :])
            p = up_ref[:, c0:c0 + ck].astype(F32)
            accw_ref[:, c0:c0 + ck] += _stack_rows([_colsum(du2 * p), _colsum(du1 * p), _colsum(du * p)], ck)
        dh = _permute_rows(_interleave(tm, inverse=True), dh)
        dx, dg = _rms_bwd(x1_ref[...], g2_ref[...], dh)
        dx1_ref[...] = dx2_ref[...] + dx
        acc_ref[...] += _stack_rows([_colsum(dg)], d)

    return _call(
        body, name="ffn_up_bwd", grid=(t // tm,),
        in_specs=[_rows(tm, n), _next_halo(tm, n, t), _rows(tm, n), _rows(tm, d), _rows(tm, d), _resident((3, n)),
                  _resident((n, d)), _resident((1, d))],
        out_specs=[_rows(tm, n), _rows(tm, d), _acc_spec(d), _acc_spec(n)],
        out_shape=[jax.ShapeDtypeStruct((t, n), BF16), jax.ShapeDtypeStruct((t, d), F32),
                   jax.ShapeDtypeStruct((SUBLANES, d), F32), jax.ShapeDtypeStruct((SUBLANES, n), F32)],
        args=(dup, dup, up_pre, x1, dx2, conv_w, w_u, g2), semantics=("arbitrary",), carry=carry)


def _tn_matmul(a, b, name):
    t, mdim = a.shape
    n = b.shape[1]
    tk = min(1024, t)
    tmm = _pick_tile(mdim, 1536)
    tn = _pick_tile(n, 1024)

    def body(a_ref, b_ref, o_ref, acc_ref):
        k = pl.program_id(2)

        @pl.when(k == 0)
        def _():
            acc_ref[...] = jnp.zeros_like(acc_ref)

        acc_ref[...] += _dot_tn(a_ref[...].astype(BF16), b_ref[...].astype(BF16))

        @pl.when(k == t // tk - 1)
        def _():
            o_ref[...] = acc_ref[...].astype(BF16)

    return pl.pallas_call(
        body, name=name, grid=(mdim // tmm, n // tn, t // tk),
        in_specs=[pl.BlockSpec((tk, tmm), lambda i, j, k: (k, i)), pl.BlockSpec((tk, tn), lambda i, j, k: (k, j))],
        out_specs=pl.BlockSpec((tmm, tn), lambda i, j, k: (i, j)),
        out_shape=jax.ShapeDtypeStruct((mdim, n), BF16),
        scratch_shapes=[pltpu.VMEM((tmm, tn), F32)],
        compiler_params=_params("parallel", "parallel", "arbitrary"),
    )(a, b)


def _mixer_bwd(dx1, gates, yap, ybp, os, lses, b_gate, w_o, w_pa, w_pb):
    t, d = dx1.shape
    cw = w_pa.shape[1]
    tm = min(ROWS_MATMUL, t)

    def body(dx1_ref, gates_ref, yap_ref, ybp_ref, o0_ref, o1_ref, o2_ref, l0_ref, l1_ref, l2_ref, bg_ref, wo_ref,
             wpa_ref, wpb_ref, dgates_ref, dyap_ref, dybp_ref, dya_ref, do0_ref, do1_ref, do2_ref, dl0_ref, dl1_ref,
             dl2_ref, acc_ref):
        m = pl.program_id(0)

        @pl.when(m == 0)
        def _():
            acc_ref[...] = jnp.zeros_like(acc_ref)

        dmg = _dot_nt(dx1_ref[...].astype(BF16), wo_ref[...])
        sa = _sigmoid(gates_ref[:, 0:d].astype(F32) + bg_ref[0:1, :])
        sb = _sigmoid(gates_ref[:, d:2 * d].astype(F32) + bg_ref[1:2, :])
        dyap = (dmg * sa).astype(BF16)
        dybp = (dmg * sb).astype(BF16)
        dga = dmg * yap_ref[...].astype(F32) * sa * (1.0 - sa)
        dgb = dmg * ybp_ref[...].astype(F32) * sb * (1.0 - sb)
        dyap_ref[...] = dyap
        dybp_ref[...] = dybp
        dgates_ref[:, 0:d] = dga.astype(BF16)
        dgates_ref[:, d:2 * d] = dgb.astype(BF16)
        acc_ref[...] += _stack_rows([_colsum(dga), _colsum(dgb)], d)
        dya_ref[...] = _dot(dyap, wpa_ref[...]).astype(BF16)
        dyb = _dot(dybp, wpb_ref[...])

        ri = lax.broadcasted_iota(jnp.int32, (GROUP_W, GROUP_W), 0) // HEAD_DIM
        ci = lax.broadcasted_iota(jnp.int32, (GROUP_W, GROUP_W), 1) // HEAD_DIM
        same_head = (ri == ci).astype(BF16)
        alphas = _group_softmax([_load_streams(r, dil, tm) for r, dil in zip((l0_ref, l1_ref, l2_ref), DILATIONS)])
        prod = jnp.zeros((tm, GROUP_W), F32)
        for i, (o_ref, do_ref, dil) in enumerate(zip((o0_ref, o1_ref, o2_ref), (do0_ref, do1_ref, do2_ref), DILATIONS)):
            dov = alphas[i] * dyb[:, i * GROUP_W:(i + 1) * GROUP_W]
            _store_streams(do_ref, dil, tm, dov.astype(BF16))
            prod = prod + dov * _load_streams(o_ref, dil, tm).astype(F32)
        hi = prod.astype(BF16)
        lo = (prod - hi.astype(F32)).astype(BF16)
        dtot = _dot(hi, same_head) + _dot(lo, same_head)
        for alpha, dl_ref, dil in zip(alphas, (dl0_ref, dl1_ref, dl2_ref), DILATIONS):
            _store_streams(dl_ref, dil, tm, alpha * dtot)

    streams = [_stream_spec(dil, tm, GROUP_W) for dil in DILATIONS]
    res = _call(
        body, name="mixer_bwd", grid=(t // tm,),
        in_specs=[_rows(tm, d), _rows(tm, 2 * d), _rows(tm, d), _rows(tm, d)] + streams * 2
        + [_resident((2, d)), _resident((d, d)), _resident((d, cw)), _resident((d, ATTN_W))],
        out_specs=[_rows(tm, 2 * d), _rows(tm, d), _rows(tm, d), _rows(tm, cw)] + streams * 2 + [_acc_spec(d)],
        out_shape=[jax.ShapeDtypeStruct((t, 2 * d), BF16), jax.ShapeDtypeStruct((t, d), BF16),
                   jax.ShapeDtypeStruct((t, d), BF16), jax.ShapeDtypeStruct((t, cw), BF16)]
        + [jax.ShapeDtypeStruct((dil, t // dil, GROUP_W), BF16) for dil in DILATIONS]
        + [jax.ShapeDtypeStruct((dil, t // dil, GROUP_W), F32) for dil in DILATIONS]
        + [jax.ShapeDtypeStruct((SUBLANES, d), F32)],
        args=(dx1, gates, yap, ybp, *[_stream_view(a, dil) for a, dil in zip(os, DILATIONS)],
              *[_stream_view(a, dil) for a, dil in zip(lses, DILATIONS)], b_gate, w_o, w_pa, w_pb),
        semantics=("arbitrary",))
    dgates, dyap, dybp, dya = res[:4]
    dos = [a.reshape(t, GROUP_W) for a in res[4:7]]
    dls = [a.reshape(t, GROUP_W) for a in res[7:10]]
    return dgates, dyap, dybp, dya, dos, dls, res[10]


def _attn_bwd(s, do, lse, dl, dil, carry=None):
    t = s.shape[0] * s.shape[1]
    nb = t // QBLK
    per_stream = nb // dil
    count = BWD_STEP_BLOCKS
    assert per_stream % count == 0

    def body(q_ref, qn_ref, kc_ref, kp_ref, vc_ref, vp_ref, do_ref, don_ref, lse_ref, lsen_ref, dl_ref, dln_ref,
             ds_ref):
        b = pl.program_id(0)
        lane, heads = _head_masks()
        first_has_prev = lax.rem(count * b, per_stream) != 0
        last_has_next = lax.rem(count * (b + 1), per_stream) != 0

        def cols(v):
            return jnp.concatenate([jnp.sum(jnp.where(lane == h * HEAD_DIM, v, 0.0), axis=1, keepdims=True)
                                    for h in range(HEADS_PER_GROUP)], axis=0)

        def pair(qs, dos, k, v, valid, lse_c, dl_c):
            s = jnp.where(valid, _dot_nt(qs, k) * ATTN_SCALE, NEG_INF)
            p = jnp.exp(s - lse_c)
            ds = p * (_dot_nt(dos, v) - dl_c)
            return p.astype(BF16), ds.astype(BF16)

        for j in range(count):
            rows, hi = slice(j * QBLK, (j + 1) * QBLK), slice((j + 1) * QBLK, (j + 2) * QBLK)
            q, do, lse, dl = q_ref[rows, :], do_ref[rows, :], lse_ref[rows, :], dl_ref[rows, :]
            kc, vc = kc_ref[rows, :], vc_ref[rows, :]
            if j == 0:
                k2 = jnp.concatenate([kp_ref[...], kc], axis=0)
                v2 = jnp.concatenate([vp_ref[...], vc], axis=0)
                mask = _band_mask(first_has_prev)
            else:
                both = slice((j - 1) * QBLK, (j + 1) * QBLK)
                k2, v2, mask = kc_ref[both, :], vc_ref[both, :], _band_mask(True)
            if j < count - 1:
                qn, don, lsen, dln = q_ref[hi, :], do_ref[hi, :], lse_ref[hi, :], dl_ref[hi, :]
                mask_n = _next_mask(True)
            else:
                qn, don, lsen, dln = qn_ref[...], don_ref[...], lsen_ref[...], dln_ref[...]
                mask_n = _next_mask(last_has_next)
            qs, qns = _stack_heads(q, heads), _stack_heads(qn, heads)
            dos, dons = _stack_heads(do, heads), _stack_heads(don, heads)
            p_q, ds_q = pair(qs, dos, k2, v2, mask, cols(lse), cols(dl))
            p_n, ds_n = pair(qns, dons, kc, vc, mask_n, cols(lsen), cols(dln))
            dq = _merge_heads(_dot(ds_q, k2), heads)
            dk = _dot_tn(jnp.concatenate([ds_q[:, QBLK:], ds_n], axis=0), jnp.concatenate([qs, qns], axis=0))
            dv = _dot_tn(jnp.concatenate([p_q[:, QBLK:], p_n], axis=0), jnp.concatenate([dos, dons], axis=0))
            ds_ref[rows, 0:GROUP_W] = (dq * ATTN_SCALE).astype(BF16)
            ds_ref[rows, GROUP_W:2 * GROUP_W] = (dk * ATTN_SCALE).astype(BF16)
            ds_ref[rows, 2 * GROUP_W:3 * GROUP_W] = dv.astype(BF16)

    sv = s.reshape(t, 3 * GROUP_W)
    cur, nxt = _pair_block(0, count), _edge_block(0, count, nb, count)
    return _call(
        body, name=f"attn_bwd_d{dil}", grid=(nb // count,),
        in_specs=[cur, nxt, _pair_block(1, count), _edge_block(1, -1, nb, count), _pair_block(2, count),
                  _edge_block(2, -1, nb, count), cur, nxt, cur, nxt, cur, nxt],
        out_specs=[pl.BlockSpec((count * QBLK, 3 * GROUP_W), lambda b: (b, 0))],
        out_shape=[jax.ShapeDtypeStruct((t, 3 * GROUP_W), BF16)],
        args=(sv, sv, sv, sv, sv, sv, do, do, lse, lse, dl, dl), semantics=("parallel",), carry=carry)


def _conv_mixer_bwd(abcv, dya, conv_w, conv_b):
    t = abcv.shape[0]
    cw = conv_w.shape[1]
    tm = min(1024, t)
    last = t // tm - 1

    def body(a_ref, ap_ref, an_ref, dya_ref, dyan_ref, cw_ref, cb_ref, d_ref, acc_ref):
        m = pl.program_id(0)

        @pl.when(m == 0)
        def _():
            acc_ref[...] = jnp.zeros_like(acc_ref)

        ab = a_ref[:, 0:cw].astype(F32)
        ac = a_ref[:, cw:2 * cw].astype(F32)
        av = a_ref[:, 2 * cw:3 * cw].astype(F32)
        u = ac * av
        hu = ap_ref[:, cw:2 * cw].astype(F32) * ap_ref[:, 2 * cw:3 * cw].astype(F32)
        hu = jnp.where(m > 0, hu, 0.0)
        u1 = _shift_down(u, hu, 1)
        u2 = _shift_down(u, hu, 2)
        cv = cw_ref[0:1, :] * u2 + cw_ref[1:2, :] * u1 + cw_ref[2:3, :] * u + cb_ref[...]
        dya_v = dya_ref[...].astype(F32)
        dcv = dya_v * ab
        ndcv = jnp.where(m < last, dyan_ref[...].astype(F32) * an_ref[:, 0:cw].astype(F32), 0.0)
        du = (cw_ref[2:3, :] * dcv + cw_ref[1:2, :] * _shift_up(dcv, ndcv, 1)
              + cw_ref[0:1, :] * _shift_up(dcv, ndcv, 2))
        d_ref[:, 0:cw] = (dya_v * cv).astype(BF16)
        d_ref[:, cw:2 * cw] = (du * av).astype(BF16)
        d_ref[:, 2 * cw:3 * cw] = (du * ac).astype(BF16)
        acc_ref[...] += _stack_rows([_colsum(dcv * u2), _colsum(dcv * u1), _colsum(dcv * u), _colsum(dcv)], cw)

    return pl.pallas_call(
        body, name="conv_mixer_bwd", grid=(t // tm,),
        in_specs=[_rows(tm, 3 * cw), _prev_halo(tm, 3 * cw), _next_halo(tm, 3 * cw, t), _rows(tm, cw),
                  _next_halo(tm, cw, t), _resident((3, cw)), _resident((1, cw))],
        out_specs=[_rows(tm, 3 * cw), _acc_spec(cw)],
        out_shape=[jax.ShapeDtypeStruct((t, 3 * cw), BF16), jax.ShapeDtypeStruct((SUBLANES, cw), F32)],
        compiler_params=_params("arbitrary"),
    )(abcv, abcv, abcv, dya, dya, conv_w, conv_b)


def _in_proj_bwd(x, dx1, dabcv, dss, dgates, w_in, g1, carry=None):
    t, d = x.shape
    qkv0 = dabcv.shape[1]
    n = w_in.shape[0]
    tm = min(ROWS_MATMUL, t)

    def body(x_ref, dx1_ref, da_ref, ds0_ref, ds1_ref, ds2_ref, dg_ref, w_ref, g_ref, dx_ref, acc_ref):
        m = pl.program_id(0)

        @pl.when(m == 0)
        def _():
            acc_ref[...] = jnp.zeros_like(acc_ref)

        dss_tok = [_load_streams(ds_ref, dil, tm) for ds_ref, dil in zip((ds0_ref, ds1_ref, ds2_ref), DILATIONS)]
        dqkv = jnp.concatenate([ds[:, j * GROUP_W:(j + 1) * GROUP_W] for j in range(3) for ds in dss_tok], axis=1)
        dh = (_dot(da_ref[...], w_ref[0:qkv0, :]) + _dot(dqkv, w_ref[qkv0:qkv0 + 3 * ATTN_W, :])
              + _dot(dg_ref[...], w_ref[qkv0 + 3 * ATTN_W:n, :]))
        dx, dg = _rms_bwd(x_ref[...], g_ref[...], dh)
        dx_ref[...] = dx1_ref[...] + dx
        acc_ref[...] += _stack_rows([_colsum(dg)], d)

    return _call(
        body, name="in_proj_bwd", grid=(t // tm,),
        in_specs=[_rows(tm, d), _rows(tm, d), _rows(tm, qkv0)]
        + [_stream_spec(dil, tm, 3 * GROUP_W) for dil in DILATIONS]
        + [_rows(tm, 2 * d), _resident((n, d)), _resident((1, d))],
        out_specs=[_rows(tm, d), _acc_spec(d)],
        out_shape=[jax.ShapeDtypeStruct((t, d), F32), jax.ShapeDtypeStruct((SUBLANES, d), F32)],
        args=(x, dx1, dabcv, *[_stream_view(a, dil) for a, dil in zip(dss, DILATIONS)], dgates, w_in, g1),
        semantics=("arbitrary",), carry=carry)


def _dw_in_qkv(ds, h, dil):
    t, d = h.shape
    tk = min(1024, t)
    sub = min(256, t)
    width = 3 * GROUP_W

    def body(ds_ref, h_ref, o_ref, acc_ref):
        k = pl.program_id(0)

        @pl.when(k == 0)
        def _():
            acc_ref[...] = jnp.zeros_like(acc_ref)

        upd = None
        for i in range(tk // sub):
            rows = ds_ref[:, i * (sub // dil):(i + 1) * (sub // dil), :].reshape(sub, width)
            if dil > 1:
                rows = _permute_rows(_perm(dil, sub, inverse=True), rows)
            term = _dot_tn(rows, h_ref[i * sub:(i + 1) * sub, :])
            upd = term if upd is None else upd + term
        acc_ref[...] += upd

        @pl.when(k == t // tk - 1)
        def _():
            o_ref[...] = acc_ref[...].astype(BF16)

    return pl.pallas_call(
        body, name=f"dw_in_qkv_d{dil}", grid=(t // tk,),
        in_specs=[_stream_spec(dil, tk, width), _rows(tk, d)],
        out_specs=pl.BlockSpec((width, d), lambda k: (0, 0)),
        out_shape=jax.ShapeDtypeStruct((width, d), BF16),
        scratch_shapes=[pltpu.VMEM((width, d), F32)],
        compiler_params=_params("arbitrary"),
    )(_stream_view(ds, dil), h)


def _local_step(x, target, p, late):
    cw = p["conv_a_w"].shape[1]
    (h, abcv, gates, *ss), (g_up,) = _in_proj(x, p["norm_mix_g"], p["w_in"], cw,
                                              carry=_Exchange("gather", [late["w_up"]]))
    w_up = _full_from_gathered(g_up)
    (o0, lse0), (g_pa,) = _attn_fwd(ss[0], DILATIONS[0], carry=_Exchange("gather", [late["w_proj_a"]]))
    (o1, lse1), (g_pb,) = _attn_fwd(ss[1], DILATIONS[1], carry=_Exchange("gather", [late["w_proj_b"]]))
    (o2, lse2), (g_out,) = _attn_fwd(ss[2], DILATIONS[2], carry=_Exchange("gather", [late["w_out"]]))
    w_pa, w_pb, w_out = [_full_from_gathered(g) for g in (g_pa, g_pb, g_out)]
    os, lses = (o0, o1, o2), (lse0, lse1, lse2)
    (x1, ya, yb, yap, ybp, merged), (g_down,) = _mixer_out(
        x, abcv, gates, os, lses, p["conv_a_w"], p["conv_a_b"], p["b_gate"], w_pa, w_pb, w_out,
        carry=_Exchange("gather", [late["w_down"]]))
    w_down = _full_from_gathered(g_down)
    h2, up_pre, act, conv, dx2, dx2i, acc_gf, loss = _ffn_fwd(x1, target, p["norm_ffn_g"], w_up, p["ffn_conv_w"],
                                                              p["ffn_conv_b"], w_down, p["final_norm_g"])

    parts, got = {}, {}
    dup, acc_fb = _ffn_act_bwd(dx2i, conv, w_down)
    parts["w_down"] = _by_destination(_tn_matmul(act, dx2i, "dw_down"))
    (dpre, dx1, acc_g2, acc_fw), (got["w_down"],) = _ffn_up_bwd(dup, up_pre, x1, dx2, p["ffn_conv_w"], w_up,
                                                                p["norm_ffn_g"],
                                                                carry=_Exchange("scatter", [parts["w_down"]]))
    parts["w_up"] = _by_destination(_tn_matmul(dpre, h2, "dw_up"))
    dgates, dyap, dybp, dya, dos, dls, acc_bg = _mixer_bwd(dx1, gates, yap, ybp, os, lses, p["b_gate"], w_out,
                                                           w_pa, w_pb)
    parts["w_out"] = _by_destination(_tn_matmul(merged, dx1, "dw_out"))
    parts["w_proj_a"] = _by_destination(_tn_matmul(dyap, ya, "dw_proj_a"))
    parts["w_proj_b"] = _by_destination(_tn_matmul(dybp, yb, "dw_proj_b"))
    minor = ("w_out", "w_proj_a", "w_proj_b")
    half = parts["w_up"].shape[1] // 2
    (ds0,), received = _attn_bwd(ss[0], dos[0], lses[0], dls[0], DILATIONS[0],
                                 carry=_Exchange("scatter", [parts[n] for n in minor]))
    got.update(zip(minor, received))
    (ds1,), first_half = _attn_bwd(ss[1], dos[1], lses[1], dls[1], DILATIONS[1],
                                   carry=_Exchange("scatter", [parts["w_up"]], rows=(0, half)))
    (ds2,), (got["w_up"],) = _attn_bwd(ss[2], dos[2], lses[2], dls[2], DILATIONS[2],
                                       carry=_Exchange("scatter", [parts["w_up"]], rows=(half, half),
                                                       into=first_half))
    dss = [ds0, ds1, ds2]
    dabcv, acc_ca = _conv_mixer_bwd(abcv, dya, p["conv_a_w"], p["conv_a_b"])
    dw_s = [_dw_in_qkv(ds, h, dil) for ds, dil in zip(dss, DILATIONS)]
    dw_qkv = [w[j * GROUP_W:(j + 1) * GROUP_W] for j in range(3) for w in dw_s]
    g_w_in = jnp.concatenate([_tn_matmul(dabcv, h, "dw_in_a"), *dw_qkv, _tn_matmul(dgates, h, "dw_in_g")], axis=0)
    parts["w_in"] = _by_destination(g_w_in)
    (dx, acc_g1), (got["w_in"],) = _in_proj_bwd(x, dx1, dabcv, dss, dgates, p["w_in"], p["norm_mix_g"],
                                                carry=_Exchange("scatter", [parts["w_in"]]))
    small = dict(norm_mix_g=acc_g1[0:1], b_gate=acc_bg[0:2], conv_a_w=acc_ca[0:3], conv_a_b=acc_ca[3:4],
                 norm_ffn_g=acc_g2[0:1], ffn_conv_w=acc_fw[0:3], ffn_conv_b=acc_fb[0:1], final_norm_g=acc_gf[0:1])
    return loss[0, 0], dx, parts, got, small


def _all_gather(shards):
    n = len(shards)

    def body(*refs):
        ins, outs = refs[:n], refs[n:2 * n]
        send_sems, recv_sems, local_sems = refs[2 * n:]
        x, y, c = _mesh_pos()
        me, sibling = (x, y, c), (x, y, 1 - c)
        chips = [(1 - x, y), (x, 1 - y), (1 - x, 1 - y)]

        def copy(i, k, block, to, src=None):
            rows = outs[i].at[_dev_index(*block)]
            return pltpu.make_async_remote_copy(
                src_ref=rows if src is None else src, dst_ref=rows, send_sem=send_sems.at[i, k],
                recv_sem=recv_sems.at[i, k], device_id=to, device_id_type=MESH)

        mine, first, passed = [], [], []
        for i in range(n):
            cp = pltpu.make_async_copy(ins[i], outs[i].at[_dev_index(*me)], local_sems.at[i])
            cp.start()
            mine.append(cp)
            first.append(copy(i, 0, me, sibling, src=ins[i]))
            first += [copy(i, 1 + j, me, (*chip, c), src=ins[i]) for j, chip in enumerate(chips)]
        for cp in first:
            cp.start()
        for i in range(n):
            for j, chip in enumerate(chips):
                copy(i, 1 + j, (*chip, c), me).wait_recv()
                fw = copy(i, 4 + j, (*chip, c), sibling)
                fw.start()
                passed.append(fw)
        for i in range(n):
            copy(i, 0, sibling, me).wait_recv()
            for j, chip in enumerate(chips):
                copy(i, 4 + j, (*chip, 1 - c), me).wait_recv()
        for cp in first + passed:
            cp.wait_send()
        for cp in mine:
            cp.wait()

    return pl.pallas_call(
        body, name="all_gather_weights",
        in_specs=[ANY] * n, out_specs=[ANY] * n,
        out_shape=[jax.ShapeDtypeStruct((N_DEV,) + s.shape, s.dtype) for s in shards],
        scratch_shapes=[pltpu.SemaphoreType.DMA((n, 7)), pltpu.SemaphoreType.DMA((n, 7)),
                        pltpu.SemaphoreType.DMA((n,))],
    )(*shards)


def _all_reduce_small(v):
    r = v.shape[0]

    def body(v_ref, o_ref, gath, send_sems, recv_sems):
        x, y, c = _mesh_pos()
        me = _dev_index(x, y, c)
        gath[me] = v_ref[...]
        flips = [(kx, ky, kc) for kx in (0, 1) for ky in (0, 1) for kc in (0, 1)][1:]
        copies = []
        for k, (kx, ky, kc) in enumerate(flips):
            px = 1 - x if kx else x
            py = 1 - y if ky else y
            pc = 1 - c if kc else c
            cp = pltpu.make_async_remote_copy(
                src_ref=v_ref, dst_ref=gath.at[me], send_sem=send_sems.at[k], recv_sem=recv_sems.at[k],
                device_id=(px, py, pc), device_id_type=MESH)
            cp.start()
            copies.append((cp, _dev_index(px, py, pc)))
        for k, (cp, peer) in enumerate(copies):
            pltpu.make_async_remote_copy(
                src_ref=v_ref, dst_ref=gath.at[peer], send_sem=send_sems.at[k], recv_sem=recv_sems.at[k],
                device_id=(x, y, c), device_id_type=MESH).wait_recv()
        for cp, _ in copies:
            cp.wait_send()
        total = gath[0]
        for j in range(1, N_DEV):
            total = total + gath[j]
        o_ref[...] = total

    return pl.pallas_call(
        body, name="all_reduce_small",
        in_specs=[pl.BlockSpec(memory_space=pltpu.VMEM)], out_specs=pl.BlockSpec(memory_space=pltpu.VMEM),
        out_shape=jax.ShapeDtypeStruct((r, LANES), F32),
        scratch_shapes=[pltpu.VMEM((N_DEV, r, LANES), F32), pltpu.SemaphoreType.DMA((7,)),
                        pltpu.SemaphoreType.DMA((7,))],
    )(v)


def _adamw_math(w, g, m, v):
    m2 = ADAM_B1 * m + (1.0 - ADAM_B1) * g
    v2 = ADAM_B2 * v + (1.0 - ADAM_B2) * (g * g)
    m_hat = m2 / (1.0 - ADAM_B1 ** ADAM_STEP)
    v_hat = v2 / (1.0 - ADAM_B2 ** ADAM_STEP)
    delta = -ADAM_LR * (m_hat / (jnp.sqrt(v_hat) + ADAM_EPS) + ADAM_WD * w)
    return delta, m2, v2


def _adamw_big(w, m, v, part, got, me):
    r, c = part.shape[1:]
    flip = w.shape != (r, c)
    tr = r if flip else max(t for t in range(HALO, min(r, 512) + 1, HALO) if r % t == 0)

    def body(me_ref, w_ref, m_ref, v_ref, own_ref, *rest):
        del me_ref
        got_refs, (g_out, d_out, m_out, v_out) = rest[:N_DEV - 1], rest[N_DEV - 1:]
        g = own_ref[...].astype(F32)
        for ref in got_refs:
            g = g + ref[...].astype(F32)
        if flip:
            g = g.T
        delta, m2, v2 = _adamw_math(w_ref[...], g, m_ref[...], v_ref[...])
        g_out[...] = g
        d_out[...] = delta
        m_out[...] = m2
        v_out[...] = v2

    def peer_block(k):
        return pl.BlockSpec((None, tr, c), lambda i, me_ref: (jnp.bitwise_xor(me_ref[0], k), i, 0))

    plain = pl.BlockSpec(w.shape if flip else (tr, c), lambda i, me_ref: (i, 0))
    out = jax.ShapeDtypeStruct(w.shape, F32)
    return pl.pallas_call(
        body, name="adamw_big",
        grid_spec=pltpu.PrefetchScalarGridSpec(
            num_scalar_prefetch=1, grid=(r // tr,),
            in_specs=[plain, plain, plain] + [peer_block(k) for k in range(N_DEV)],
            out_specs=[plain] * 4),
        out_shape=[out] * 4,
        compiler_params=_params("parallel"),
    )(me, w, m, v, part, *([got] * (N_DEV - 1)))


def _adamw_small(ws, gs, ms, vs):
    n = len(ws)

    def body(*refs):
        ins, outs = refs[:4 * n], refs[4 * n:]
        for i in range(n):
            delta, m2, v2 = _adamw_math(ins[i][...], ins[n + i][...], ins[2 * n + i][...], ins[3 * n + i][...])
            outs[i][...] = delta
            outs[n + i][...] = m2
            outs[2 * n + i][...] = v2

    out = [jax.ShapeDtypeStruct(w.shape, F32) for w in ws]
    res = pl.pallas_call(body, name="adamw_small", out_shape=out * 3)(*ws, *gs, *ms, *vs)
    return res[:n], res[n:2 * n], res[2 * n:]


BIG = ("w_in", "w_proj_a", "w_proj_b", "w_out", "w_up", "w_down")
LATE = ("w_proj_a", "w_proj_b", "w_out", "w_up", "w_down")
COLUMN_SHARDED = ("w_in", "w_proj_a", "w_proj_b", "w_up")
WIDE_COLUMN_SHARDED = ("w_in", "w_up")
SMALL = ("norm_mix_g", "b_gate", "conv_a_w", "conv_a_b", "norm_ffn_g", "ffn_conv_w", "ffn_conv_b", "final_norm_g")
SMALL_SHARDED = ("b_gate", "conv_a_w", "ffn_conv_w")
WEIGHTS = ("norm_mix_g", "w_in", "b_gate", "conv_a_w", "conv_a_b", "w_proj_a", "w_proj_b", "w_out", "norm_ffn_g",
           "w_up", "ffn_conv_w", "ffn_conv_b", "w_down", "final_norm_g")


def _pack(vectors, rows):
    flat = jnp.concatenate([v.reshape(-1) for v in vectors])
    return jnp.pad(flat, (0, rows * LANES - flat.shape[0])).reshape(rows, LANES)


def _packed_rows(count):
    rows = -(-count // LANES)
    return -(-rows // SUBLANES) * SUBLANES


def _unpack(packed, shapes):
    flat = packed.reshape(-1)
    out, lo = [], 0
    for s in shapes:
        size = 1
        for dim in s:
            size *= dim
        out.append(flat[lo:lo + size].reshape(s))
        lo += size
    return out


def _full_from_gathered(gathered):
    _, r, c = gathered.shape
    return gathered.reshape(N_DEV * r, c)


def _by_destination(grad):
    rr, cc = grad.shape
    return grad.reshape(N_DEV, rr // N_DEV, cc)


def _block2d(name, a):
    a = a.reshape(a.shape[-2:])
    return a.T if name in WIDE_COLUMN_SHARDED else a


def kernel(x, norm_mix_g, w_in, b_gate, conv_a_w, conv_a_b, w_proj_a, w_proj_b, w_out, norm_ffn_g, w_up, ffn_conv_w, ffn_conv_b, w_down, final_norm_g, loss_target, m_norm_mix_g, m_w_in, m_b_gate, m_conv_a_w, m_conv_a_b, m_w_proj_a, m_w_proj_b, m_w_out, m_norm_ffn_g, m_w_up, m_ffn_conv_w, m_ffn_conv_b, m_w_down, m_final_norm_g, v_norm_mix_g, v_w_in, v_b_gate, v_conv_a_w, v_conv_a_b, v_w_proj_a, v_w_proj_b, v_w_out, v_norm_ffn_g, v_w_up, v_ffn_conv_w, v_ffn_conv_b, v_w_down, v_final_norm_g):
    given = dict(locals())
    shard = {n: given[n] for n in WEIGHTS}
    mom_m = {n: given["m_" + n] for n in WEIGHTS}
    mom_v = {n: given["v_" + n] for n in WEIGHTS}
    xi, yi, ci = _mesh_pos()
    me = _dev_index(xi, yi, ci)
    me1 = me.astype(jnp.int32).reshape(1)

    big2d = {n: _block2d(n, shard[n]) for n in BIG}
    small_shapes = [shard[n].shape[1:] for n in SMALL_SHARDED]
    n_small = sum(s[0] * s[1] for s in small_shapes)
    packed_small = _pack([shard[n] for n in SMALL_SHARDED], _packed_rows(n_small))
    gathered = _all_gather([big2d["w_in"].astype(BF16), packed_small])
    p = {"w_in": _full_from_gathered(gathered[0])}
    flat_small = gathered[-1].reshape(N_DEV, -1)
    lo = 0
    for n, (rows, width) in zip(SMALL_SHARDED, small_shapes):
        blocks = flat_small[:, lo:lo + rows * width].reshape(N_DEV, rows, width)
        p[n] = blocks.transpose(1, 0, 2).reshape(rows, N_DEV * width)
        lo += rows * width
    p["norm_mix_g"], p["norm_ffn_g"] = shard["norm_mix_g"], shard["norm_ffn_g"]
    p["conv_a_b"], p["ffn_conv_b"] = shard["conv_a_b"], shard["ffn_conv_b"]
    p["final_norm_g"] = shard["final_norm_g"].reshape(1, -1)
    late = {n: (big2d[n].T if n in ("w_proj_a", "w_proj_b") else big2d[n]).astype(BF16) for n in LATE}

    loss_part, dx, parts, got, g_small = _local_step(x[0], loss_target[0], p, late)

    results = {}
    for n in BIG:
        outs = _adamw_big(big2d[n], _block2d(n, mom_m[n]), _block2d(n, mom_v[n]), parts[n], got[n], me1)
        results[n] = [_block2d(n, o).reshape(shard[n].shape) for o in outs]

    small_full_shapes = [g_small[n].shape for n in SMALL]
    n_vec = sum(s[0] * s[1] for s in small_full_shapes) + 1
    packed = _pack([g_small[n] for n in SMALL] + [loss_part.reshape(1)], _packed_rows(n_vec))
    reduced = _all_reduce_small(packed)
    *g_full, loss_vec = _unpack(reduced, small_full_shapes + [(1,)])
    loss = loss_vec[0]
    own_g = []
    for n, g in zip(SMALL, g_full):
        if n in SMALL_SHARDED:
            width = shard[n].shape[-1]
            g = lax.dynamic_slice_in_dim(g, me * width, width, axis=1)
        own_g.append(g.reshape(shard[n].shape))
    def rows2d(a):
        return a.reshape(-1, a.shape[-1])

    deltas, new_ms, new_vs = _adamw_small([rows2d(shard[n]) for n in SMALL], [rows2d(g) for g in own_g],
                                          [rows2d(mom_m[n]) for n in SMALL], [rows2d(mom_v[n]) for n in SMALL])
    for i, n in enumerate(SMALL):
        results[n] = [own_g[i]] + [a.reshape(shard[n].shape) for a in (deltas[i], new_ms[i], new_vs[i])]

    grad_x = dx.reshape(x.shape)
    return (loss, grad_x, *[results[n][0] for n in WEIGHTS], *[results[n][1] for n in WEIGHTS],
            *[results[n][2] for n in WEIGHTS], *[results[n][3] for n in WEIGHTS])
```

```python
import functools

import jax
import jax.numpy as jnp
from jax import lax
from jax.experimental import pallas as pl
from jax.experimental.pallas import tpu as pltpu

F32 = jnp.float32
BF16 = jnp.bfloat16
MESH = pl.DeviceIdType.MESH

N_DEV = 8
RMS_EPS = 1e-6
NEG_INF = -1e30
N_GROUPS = 3
DILATIONS = (1, 4, 16)
HEADS_PER_GROUP = 4
HEAD_DIM = 64
GROUP_W = HEADS_PER_GROUP * HEAD_DIM
ATTN_W = N_GROUPS * GROUP_W
QBLK = 128
STEP_BLOCKS = 4
BWD_STEP_BLOCKS = 2
ATTN_SCALE = HEAD_DIM ** -0.5

ADAM_LR = 0.001
ADAM_B1 = 0.9
ADAM_B2 = 0.999
ADAM_EPS = 1e-08
ADAM_WD = 0.01
ADAM_STEP = 10

PERM_TOKENS = 256
ROWS_MATMUL = 512
HALO = 16
LANES = 128
SUBLANES = 8
VMEM_LIMIT_BYTES = 56 * 1024 * 1024


def _params(*sem):
    return pltpu.CompilerParams(dimension_semantics=sem, vmem_limit_bytes=VMEM_LIMIT_BYTES)


def _pick_tile(n, cap):
    if n <= cap:
        return n
    best = None
    for t in range(LANES, cap + 1, LANES):
        if n % t == 0:
            best = t
    assert best is not None, (n, cap)
    return best


def _rows(tm, c, j=0):
    return pl.BlockSpec((tm, c), lambda m: (m, j))


def _prev_halo(tm, c):
    return pl.BlockSpec((HALO, c), lambda m: (jnp.maximum(m * (tm // HALO) - 1, 0), 0))


def _next_halo(tm, c, t_total):
    last = t_total // HALO - 1
    return pl.BlockSpec((HALO, c), lambda m: (jnp.minimum((m + 1) * (tm // HALO), last), 0))


def _resident(shape):
    nd = len(shape)
    return pl.BlockSpec(shape, lambda *_: (0,) * nd, pipeline_mode=pl.Buffered(1))


def _acc_spec(c):
    return pl.BlockSpec((SUBLANES, c), lambda *_: (0, 0))


def _shift_down(u, halo, k):
    edge = jnp.concatenate([halo[HALO - SUBLANES:], u[:SUBLANES]], axis=0)
    head = pltpu.roll(edge, k, 0)[SUBLANES:]
    return jnp.concatenate([head, pltpu.roll(u, k, 0)[SUBLANES:]], axis=0)


def _shift_up(u, halo, k):
    n = u.shape[0]
    edge = jnp.concatenate([u[n - SUBLANES:], halo[:SUBLANES]], axis=0)
    tail = pltpu.roll(edge, 2 * SUBLANES - k, 0)[:SUBLANES]
    return jnp.concatenate([pltpu.roll(u, n - k, 0)[:n - SUBLANES], tail], axis=0)


def _interleave(tm, inverse=False):
    return _perm(tm // SUBLANES, tm, inverse)


def _edge_groups(u, halo, k, from_end):
    n = u.shape[0]
    sub = lax.broadcasted_iota(jnp.int32, (SUBLANES, u.shape[1]), 0)
    out = []
    for j in range(2 - k, 2):
        lo = n - HALO + j * SUBLANES if from_end else j * SUBLANES
        own, other = u[lo:lo + SUBLANES], halo[j * SUBLANES:(j + 1) * SUBLANES]
        if from_end:
            out.append(pltpu.roll(jnp.where(sub == SUBLANES - 1, other, own), 1, 0))
        else:
            out.append(pltpu.roll(jnp.where(sub == 0, other, own), SUBLANES - 1, 0))
    return out


def _shift_down_il(u, halo, k):
    return jnp.concatenate(_edge_groups(u, halo, k, True) + [u[:u.shape[0] - k * SUBLANES]], axis=0)


def _shift_up_il(u, halo, k):
    if k == 1:
        edge = _edge_groups(u, halo, 2, False)[:1]
    else:
        edge = _edge_groups(u, halo, 2, False)
    return jnp.concatenate([u[k * SUBLANES:]] + edge, axis=0)


def _stack_rows(rows, c):
    idx = lax.broadcasted_iota(jnp.int32, (SUBLANES, c), 0)
    out = jnp.zeros((SUBLANES, c), F32)
    for i, r in enumerate(rows):
        out = out + jnp.where(idx == i, r, 0.0)
    return out


def _colsum(v):
    return jnp.sum(v, axis=0, keepdims=True)


def _sigmoid(v):
    return 0.5 * jnp.tanh(0.5 * v) + 0.5


def _rms_fwd(xv, g):
    r = lax.rsqrt(jnp.mean(xv * xv, axis=-1, keepdims=True) + RMS_EPS)
    return xv * r * g, r


def _rms_bwd(xv, g, dy):
    r = lax.rsqrt(jnp.mean(xv * xv, axis=-1, keepdims=True) + RMS_EPS)
    xn = xv * r
    dxn = dy * g
    dx = r * (dxn - xn * jnp.mean(dxn * xn, axis=-1, keepdims=True))
    return dx, dy * xn


def _dot(a, b):
    return jnp.dot(a, b, preferred_element_type=F32)


def _dot_nt(a, b):
    return lax.dot_general(a, b, (((1,), (1,)), ((), ())), preferred_element_type=F32)


def _dot_tn(a, b):
    return lax.dot_general(a, b, (((0,), (0,)), ((), ())), preferred_element_type=F32)


def _perm(dil, n, inverse=False):
    i = lax.broadcasted_iota(jnp.int32, (n, n), 0)
    j = lax.broadcasted_iota(jnp.int32, (n, n), 1)
    if inverse:
        i, j = j, i
    per = n // dil
    return (j == (i % per) * dil + i // per).astype(BF16)


def _permute_rows(pm, v):
    if v.dtype == BF16:
        return _dot(pm, v).astype(BF16)
    hi = v.astype(BF16)
    lo = (v - hi.astype(F32)).astype(BF16)
    return _dot(pm, hi) + _dot(pm, lo)


def _stream_view(a, dil):
    t, c = a.shape
    return a.reshape(dil, t // dil, c)


def _stream_spec(dil, tm, c):
    return pl.BlockSpec((dil, tm // dil, c), lambda m: (0, m, 0))


def _load_streams(ref, dil, tm):
    c = ref.shape[-1]
    if dil == 1:
        return ref[...].reshape(tm, c)
    sub = min(PERM_TOKENS, tm)
    pm = _perm(dil, sub, inverse=True)
    parts = [_permute_rows(pm, ref[:, i * (sub // dil):(i + 1) * (sub // dil), :].reshape(sub, c))
             for i in range(tm // sub)]
    return parts[0] if len(parts) == 1 else jnp.concatenate(parts, axis=0)


def _store_streams(ref, dil, tm, v):
    if dil == 1:
        ref[...] = v.reshape(ref.shape).astype(ref.dtype)
        return
    sub = min(PERM_TOKENS, tm)
    pm = _perm(dil, sub)
    for i in range(tm // sub):
        piece = _permute_rows(pm, v[i * sub:(i + 1) * sub])
        ref[:, i * (sub // dil):(i + 1) * (sub // dil), :] = piece.reshape(dil, sub // dil, -1).astype(ref.dtype)


ANY = pl.BlockSpec(memory_space=pl.ANY)


def _mesh_pos():
    return lax.axis_index("x"), lax.axis_index("y"), lax.axis_index("c")


def _dev_index(px, py, pc):
    return 4 * px + 2 * py + pc


class _Exchange:
    def __init__(self, mode, arrays, rows=None, into=()):
        self.mode, self.arrays, self.rows, self.into = mode, list(arrays), rows, list(into)
        n = len(self.arrays)
        if mode == "gather":
            self.out_shape = [jax.ShapeDtypeStruct((N_DEV,) + a.shape, a.dtype) for a in self.arrays]
        else:
            self.out_shape = [jax.ShapeDtypeStruct(a.shape, a.dtype) for a in self.arrays]
        self.scratch = [pltpu.SemaphoreType.DMA((n, N_DEV - 1)), pltpu.SemaphoreType.DMA((n, N_DEV - 1)),
                        pltpu.SemaphoreType.DMA((n,))]

    def _peers(self):
        x, y, c = _mesh_pos()
        flips = [(kx, ky, kc) for kx in (0, 1) for ky in (0, 1) for kc in (0, 1)][1:]
        peers = [(1 - x if kx else x, 1 - y if ky else y, 1 - c if kc else c) for kx, ky, kc in flips]
        return _dev_index(x, y, c), peers

    def _copy(self, ins, outs, sems, i, k, peer, me, sending):
        src = ins[i] if self.mode == "gather" else ins[i].at[_dev_index(*peer)]
        dst = outs[i].at[me if sending else _dev_index(*peer)]
        if self.rows is not None:
            src, dst = src.at[pl.ds(*self.rows)], dst.at[pl.ds(*self.rows)]
        return pltpu.make_async_remote_copy(src_ref=src, dst_ref=dst, send_sem=sems[0].at[i, k],
                                            recv_sem=sems[1].at[i, k], device_id=peer, device_id_type=MESH)

    def _own(self, ins, outs, sems, i, me):
        return pltpu.make_async_copy(ins[i], outs[i].at[me], sems[2].at[i])

    def start(self, ins, outs, sems):
        me, peers = self._peers()
        for i in range(len(ins)):
            if self.mode == "gather":
                self._own(ins, outs, sems, i, me).start()
            for k, peer in enumerate(peers):
                self._copy(ins, outs, sems, i, k, peer, me, True).start()

    def wait(self, ins, outs, sems):
        me, peers = self._peers()
        for i in range(len(ins)):
            for k, peer in enumerate(peers):
                self._copy(ins, outs, sems, i, k, peer, me, False).wait_recv()
            for k, peer in enumerate(peers):
                self._copy(ins, outs, sems, i, k, peer, me, True).wait_send()
            if self.mode == "gather":
                self._own(ins, outs, sems, i, me).wait()


def _call(body, *, name, grid, in_specs, out_specs, out_shape, args, semantics, carry=None, scratch=()):
    if carry is None:
        return pl.pallas_call(body, name=name, grid=grid, in_specs=in_specs, out_specs=out_specs,
                              out_shape=out_shape, scratch_shapes=list(scratch),
                              compiler_params=_params(*semantics))(*args)
    n_in, n_out, n_x, n_s = len(in_specs), len(out_specs), len(carry.arrays), len(scratch)
    n_into = len(carry.into)
    all_in = n_in + n_x + n_into

    def carried(*refs):
        ins, x_ins = refs[:n_in], refs[n_in:n_in + n_x]
        outs = refs[all_in:all_in + n_out]
        x_outs = refs[all_in + n_out:all_in + n_out + n_x]
        own = refs[all_in + n_out + n_x:all_in + n_out + n_x + n_s]
        sems = refs[all_in + n_out + n_x + n_s:]
        first = functools.reduce(jnp.logical_and, [pl.program_id(a) == 0 for a in range(len(grid))])
        last = functools.reduce(jnp.logical_and, [pl.program_id(a) == grid[a] - 1 for a in range(len(grid))])

        @pl.when(first)
        def _():
            carry.start(x_ins, x_outs, sems)

        body(*ins, *outs, *own)

        @pl.when(last)
        def _():
            carry.wait(x_ins, x_outs, sems)

    res = pl.pallas_call(
        carried, name=name, grid=grid, in_specs=list(in_specs) + [ANY] * (n_x + n_into),
        out_specs=list(out_specs) + [ANY] * n_x, out_shape=list(out_shape) + carry.out_shape,
        input_output_aliases={n_in + n_x + i: n_out + i for i in range(n_into)},
        scratch_shapes=list(scratch) + carry.scratch, compiler_params=_params(*["arbitrary"] * len(grid)),
    )(*args, *carry.arrays, *carry.into)
    return list(res[:n_out]), list(res[n_out:])


def _in_proj(x, g, wt, cw, carry=None):
    t, d = x.shape
    n = wt.shape[0]
    tm = min(ROWS_MATMUL, t)
    qkv0 = 3 * cw

    def body(x_ref, g_ref, wt_ref, h_ref, abcv_ref, gates_ref, *s_refs):
        h = _rms_fwd(x_ref[...], g_ref[...])[0].astype(BF16)
        h_ref[...] = h
        abcv_ref[...] = _dot_nt(h, wt_ref[0:qkv0, :]).astype(BF16)
        gates_ref[...] = _dot_nt(h, wt_ref[qkv0 + 3 * ATTN_W:n, :]).astype(BF16)
        qkv = _dot_nt(h, wt_ref[qkv0:qkv0 + 3 * ATTN_W, :]).astype(BF16)
        for gi, s_ref in enumerate(s_refs):
            cols = [qkv[:, j * ATTN_W + gi * GROUP_W:j * ATTN_W + (gi + 1) * GROUP_W] for j in range(3)]
            _store_streams(s_ref, DILATIONS[gi], tm, jnp.concatenate(cols, axis=1))

    return _call(
        body, name="in_proj", grid=(t // tm,),
        in_specs=[_rows(tm, d), _resident((1, d)), _resident((n, d))],
        out_specs=[_rows(tm, d), _rows(tm, qkv0), _rows(tm, 2 * d)]
        + [_stream_spec(dil, tm, 3 * GROUP_W) for dil in DILATIONS],
        out_shape=[jax.ShapeDtypeStruct((t, d), BF16), jax.ShapeDtypeStruct((t, qkv0), BF16),
                   jax.ShapeDtypeStruct((t, 2 * d), BF16)]
        + [jax.ShapeDtypeStruct((dil, t // dil, 3 * GROUP_W), BF16) for dil in DILATIONS],
        args=(x, g, wt), semantics=("parallel",), carry=carry)


def _head_masks():
    lane = lax.broadcasted_iota(jnp.int32, (1, GROUP_W), 1)
    return lane, [(lane // HEAD_DIM) == h for h in range(HEADS_PER_GROUP)]


def _stack_heads(v, heads):
    return jnp.concatenate([jnp.where(hm, v, jnp.zeros_like(v)) for hm in heads], axis=0)


def _merge_heads(v, heads):
    out = jnp.zeros((QBLK, GROUP_W), v.dtype)
    for h, hm in enumerate(heads):
        out = jnp.where(hm, v[h * QBLK:(h + 1) * QBLK], out)
    return out


def _pair_block(col, count=STEP_BLOCKS):
    return pl.BlockSpec((count * QBLK, GROUP_W), lambda b: (b, col))


def _edge_block(col, shift, nb, count=STEP_BLOCKS):
    return pl.BlockSpec((QBLK, GROUP_W), lambda b: (jnp.clip(count * b + shift, 0, nb - 1), col))


def _band_mask(has_prev):
    rows = HEADS_PER_GROUP * QBLK
    row = lax.broadcasted_iota(jnp.int32, (rows, 2 * QBLK), 0) & (QBLK - 1)
    col = lax.broadcasted_iota(jnp.int32, (rows, 2 * QBLK), 1)
    return ((col < QBLK) & (col >= row) & has_prev) | ((col >= QBLK) & (col - QBLK <= row))


def _next_mask(has_next):
    rows = HEADS_PER_GROUP * QBLK
    row = lax.broadcasted_iota(jnp.int32, (rows, QBLK), 0) & (QBLK - 1)
    col = lax.broadcasted_iota(jnp.int32, (rows, QBLK), 1)
    return (col >= row) & has_next


def _attn_fwd(s, dil, carry=None):
    t = s.shape[0] * s.shape[1]
    nb = t // QBLK
    per_stream = nb // dil
    assert per_stream % STEP_BLOCKS == 0

    def body(q_ref, kc_ref, kp_ref, vc_ref, vp_ref, o_ref, lse_ref):
        b = pl.program_id(0)
        _, heads = _head_masks()
        first_has_prev = lax.rem(STEP_BLOCKS * b, per_stream) != 0
        for j in range(STEP_BLOCKS):
            rows = slice(j * QBLK, (j + 1) * QBLK)
            if j == 0:
                k2 = jnp.concatenate([kp_ref[...], kc_ref[rows, :]], axis=0)
                v2 = jnp.concatenate([vp_ref[...], vc_ref[rows, :]], axis=0)
            else:
                both = slice((j - 1) * QBLK, (j + 1) * QBLK)
                k2, v2 = kc_ref[both, :], vc_ref[both, :]
            mask = _band_mask(first_has_prev if j == 0 else True)
            sc = jnp.where(mask, _dot_nt(_stack_heads(q_ref[rows, :], heads), k2) * ATTN_SCALE, NEG_INF)
            mx = jnp.max(sc, axis=1, keepdims=True)
            pr = jnp.exp(sc - mx)
            den = jnp.sum(pr, axis=1, keepdims=True)
            o_all = _dot(pr.astype(BF16), v2) / den
            o_ref[rows, :] = _merge_heads(o_all, heads).astype(BF16)
            lse_ref[rows, :] = _merge_heads(jnp.broadcast_to(mx + jnp.log(den), o_all.shape), heads)

    sv = s.reshape(t, 3 * GROUP_W)
    return _call(
        body, name=f"attn_fwd_d{dil}", grid=(nb // STEP_BLOCKS,),
        in_specs=[_pair_block(0), _pair_block(1), _edge_block(1, -1, nb), _pair_block(2), _edge_block(2, -1, nb)],
        out_specs=[_pair_block(0), _pair_block(0)],
        out_shape=[jax.ShapeDtypeStruct((t, GROUP_W), BF16), jax.ShapeDtypeStruct((t, GROUP_W), F32)],
        args=(sv, sv, sv, sv, sv), semantics=("parallel",), carry=carry)


def _group_softmax(parts):
    mx = jnp.maximum(jnp.maximum(parts[0], parts[1]), parts[2])
    es = [jnp.exp(p - mx) for p in parts]
    den = es[0] + es[1] + es[2]
    return [e / den for e in es]


def _mixer_out(x, abcv, gates, os, lses, conv_w, conv_b, b_gate, w_pa, w_pb, w_o, carry=None):
    t, d = x.shape
    cw = conv_w.shape[1]
    tm = min(ROWS_MATMUL, t)

    def body(x_ref, abcv_ref, halo_ref, gates_ref, o0_ref, o1_ref, o2_ref, l0_ref, l1_ref, l2_ref, cw_ref, cb_ref,
             bg_ref, wpa_ref, wpb_ref, wo_ref, x1_ref, ya_ref, yb_ref, yap_ref, ybp_ref, mg_ref):
        m = pl.program_id(0)
        ab = abcv_ref[:, 0:cw].astype(F32)
        u = abcv_ref[:, cw:2 * cw].astype(F32) * abcv_ref[:, 2 * cw:3 * cw].astype(F32)
        hu = halo_ref[:, cw:2 * cw].astype(F32) * halo_ref[:, 2 * cw:3 * cw].astype(F32)
        hu = jnp.where(m > 0, hu, 0.0)
        cv = (cw_ref[0:1, :] * _shift_down(u, hu, 2) + cw_ref[1:2, :] * _shift_down(u, hu, 1)
              + cw_ref[2:3, :] * u + cb_ref[...])
        ya = (ab * cv).astype(BF16)
        ya_ref[...] = ya
        alphas = _group_softmax([_load_streams(r, dil, tm) for r, dil in zip((l0_ref, l1_ref, l2_ref), DILATIONS)])
        for i, (o_ref, dil) in enumerate(zip((o0_ref, o1_ref, o2_ref), DILATIONS)):
            sl = slice(i * GROUP_W, (i + 1) * GROUP_W)
            yb_ref[:, sl] = (alphas[i] * _load_streams(o_ref, dil, tm).astype(F32)).astype(BF16)
        yap = _dot_nt(ya, wpa_ref[...])
        ybp = _dot_nt(yb_ref[...], wpb_ref[...])
        yap_ref[...] = yap.astype(BF16)
        ybp_ref[...] = ybp.astype(BF16)
        sa = _sigmoid(gates_ref[:, 0:d].astype(F32) + bg_ref[0:1, :])
        sb = _sigmoid(gates_ref[:, d:2 * d].astype(F32) + bg_ref[1:2, :])
        merged = (sa * yap + sb * ybp).astype(BF16)
        mg_ref[...] = merged
        x1_ref[...] = x_ref[...] + _dot(merged, wo_ref[...])

    return _call(
        body, name="mixer_out", grid=(t // tm,),
        in_specs=[_rows(tm, d), _rows(tm, 3 * cw), _prev_halo(tm, 3 * cw), _rows(tm, 2 * d)]
        + [_stream_spec(dil, tm, GROUP_W) for dil in DILATIONS] * 2
        + [_resident((3, cw)), _resident((1, cw)), _resident((2, d)),
           _resident((d, cw)), _resident((d, ATTN_W)), _resident((d, d))],
        out_specs=[_rows(tm, d), _rows(tm, cw), _rows(tm, ATTN_W), _rows(tm, d), _rows(tm, d), _rows(tm, d)],
        out_shape=[jax.ShapeDtypeStruct((t, d), F32), jax.ShapeDtypeStruct((t, cw), BF16),
                   jax.ShapeDtypeStruct((t, ATTN_W), BF16), jax.ShapeDtypeStruct((t, d), BF16),
                   jax.ShapeDtypeStruct((t, d), BF16), jax.ShapeDtypeStruct((t, d), BF16)],
        args=(x, abcv, abcv, gates, *[_stream_view(a, dil) for a, dil in zip(os, DILATIONS)],
              *[_stream_view(a, dil) for a, dil in zip(lses, DILATIONS)], conv_w, conv_b, b_gate, w_pa, w_pb, w_o),
        semantics=("parallel",), carry=carry)


def _ffn_fwd(x1, target, g2, w_ut, conv_w, conv_b, w_d, g_f, carry=None):
    t, d = x1.shape
    dff = w_d.shape[0]
    tm = min(256, t)
    ck = _pick_tile(dff, 2816)

    def body(x1_ref, tg_ref, g2_ref, wut_ref, cw_ref, cb_ref, wd_ref, gf_ref, h2_ref, up_ref, act_ref, conv_ref,
             dx2_ref, dx2i_ref, acc_ref, loss_ref, halo_ref):
        m = pl.program_id(0)

        @pl.when(m == 0)
        def _():
            acc_ref[...] = jnp.zeros_like(acc_ref)
            loss_ref[...] = jnp.zeros_like(loss_ref)
            halo_ref[...] = jnp.zeros_like(halo_ref)

        h2 = _permute_rows(_interleave(tm), _rms_fwd(x1_ref[...], g2_ref[...])[0].astype(BF16))
        h2_ref[...] = h2

        def conv(c0):
            p = _dot_nt(h2, wut_ref[c0:c0 + ck, :])
            up_ref[:, c0:c0 + ck] = p.astype(BF16)
            hp = halo_ref[:, c0:c0 + ck]
            halo_ref[:, c0:c0 + ck] = p[tm - HALO:, :]
            return (cw_ref[0:1, c0:c0 + ck] * _shift_down_il(p, hp, 2)
                    + cw_ref[1:2, c0:c0 + ck] * _shift_down_il(p, hp, 1)
                    + cw_ref[2:3, c0:c0 + ck] * p + cb_ref[:, c0:c0 + ck])

        down = jnp.zeros((tm, d), F32)
        for c0 in range(0, dff, ck):
            gate = conv(c0)
            val = conv(dff + c0)
            conv_ref[:, c0:c0 + ck] = gate.astype(BF16)
            conv_ref[:, dff + c0:dff + c0 + ck] = val.astype(BF16)
            act = (gate * _sigmoid(gate) * val).astype(BF16)
            act_ref[:, c0:c0 + ck] = act
            down = down + _dot(act, wd_ref[c0:c0 + ck, :])
        x2 = x1_ref[...] + _permute_rows(_interleave(tm, inverse=True), down)
        y, _ = _rms_fwd(x2, gf_ref[...])
        diff = y - tg_ref[...]
        loss_ref[...] += 0.5 * jnp.sum(jnp.mean(diff * diff, axis=-1, keepdims=True))
        dx2, dg = _rms_bwd(x2, gf_ref[...], diff * (1.0 / d))
        dx2_ref[...] = dx2
        dx2i_ref[...] = _permute_rows(_interleave(tm), dx2.astype(BF16))
        acc_ref[...] += _stack_rows([_colsum(dg)], d)

    return _call(
        body, name="ffn_fwd", grid=(t // tm,),
        in_specs=[_rows(tm, d), _rows(tm, d), _resident((1, d)), _resident((2 * dff, d)), _resident((3, 2 * dff)),
                  _resident((1, 2 * dff)), _resident((dff, d)), _resident((1, d))],
        out_specs=[_rows(tm, d), _rows(tm, 2 * dff), _rows(tm, dff), _rows(tm, 2 * dff), _rows(tm, d), _rows(tm, d),
                   _acc_spec(d), _acc_spec(LANES)],
        out_shape=[jax.ShapeDtypeStruct((t, d), BF16), jax.ShapeDtypeStruct((t, 2 * dff), BF16),
                   jax.ShapeDtypeStruct((t, dff), BF16), jax.ShapeDtypeStruct((t, 2 * dff), BF16),
                   jax.ShapeDtypeStruct((t, d), F32), jax.ShapeDtypeStruct((t, d), BF16),
                   jax.ShapeDtypeStruct((SUBLANES, d), F32), jax.ShapeDtypeStruct((SUBLANES, LANES), F32)],
        args=(x1, target, g2, w_ut, conv_w, conv_b, w_d, g_f), semantics=("arbitrary",), carry=carry,
        scratch=[pltpu.VMEM((HALO, 2 * dff), F32)])


def _ffn_act_bwd(dx2, conv, w_d):
    t, d = dx2.shape
    dff = w_d.shape[0]
    tm = min(256, t)
    ck = _pick_tile(dff, 2816)

    def body(dx2_ref, conv_ref, wd_ref, dup_ref, acc_ref):
        m = pl.program_id(0)

        @pl.when(m == 0)
        def _():
            acc_ref[...] = jnp.zeros_like(acc_ref)

        dx2v = dx2_ref[...]
        for c0 in range(0, dff, ck):
            dact = _dot_nt(dx2v, wd_ref[c0:c0 + ck, :])
            gate = conv_ref[:, c0:c0 + ck].astype(F32)
            val = conv_ref[:, dff + c0:dff + c0 + ck].astype(F32)
            sg = _sigmoid(gate)
            dval = dact * gate * sg
            dgate = dact * val * sg * (1.0 + gate * (1.0 - sg))
            dup_ref[:, c0:c0 + ck] = dgate.astype(BF16)
            dup_ref[:, dff + c0:dff + c0 + ck] = dval.astype(BF16)
            acc_ref[:, c0:c0 + ck] += _stack_rows([_colsum(dgate)], ck)
            acc_ref[:, dff + c0:dff + c0 + ck] += _stack_rows([_colsum(dval)], ck)

    return pl.pallas_call(
        body, name="ffn_act_bwd", grid=(t // tm,),
        in_specs=[_rows(tm, d), _rows(tm, 2 * dff), _resident((dff, d))],
        out_specs=[_rows(tm, 2 * dff), _acc_spec(2 * dff)],
        out_shape=[jax.ShapeDtypeStruct((t, 2 * dff), BF16), jax.ShapeDtypeStruct((SUBLANES, 2 * dff), F32)],
        compiler_params=_params("arbitrary"),
    )(dx2, conv, w_d)


def _ffn_up_bwd(dup, up_pre, x1, dx2, conv_w, w_u, g2, carry=None):
    t, d = x1.shape
    n = dup.shape[1]
    tm = min(256, t)
    ck = _pick_tile(n, 256)
    last = t // tm - 1

    def body(dup_ref, nxt_ref, up_ref, x1_ref, dx2_ref, cw_ref, wu_ref, g2_ref, dpre_ref, dx1_ref, acc_ref, accw_ref):
        m = pl.program_id(0)

        @pl.when(m == 0)
        def _():
            acc_ref[...] = jnp.zeros_like(acc_ref)
            accw_ref[...] = jnp.zeros_like(accw_ref)

        dh = jnp.zeros((tm, d), F32)
        for c0 in range(0, n, ck):
            du = dup_ref[:, c0:c0 + ck].astype(F32)
            hn = jnp.where(m < last, nxt_ref[:, c0:c0 + ck].astype(F32), 0.0)
            du1 = _shift_up_il(du, hn, 1)
            du2 = _shift_up_il(du, hn, 2)
            dpre = (cw_ref[2:3, c0:c0 + ck] * du + cw_ref[1:2, c0:c0 + ck] * du1
                    + cw_ref[0:1, c0:c0 + ck] * du2).astype(BF16)
            dpre_ref[:, c0:c0 + ck] = dpre
            dh = dh + _dot(dpre, wu_ref[c0:c0 + ck, :])
            p = up_ref[:, c0:c0 + ck].astype(F32)
            accw_ref[:, c0:c0 + ck] += _stack_rows([_colsum(du2 * p), _colsum(du1 * p), _colsum(du * p)], ck)
        dh = _permute_rows(_interleave(tm, inverse=True), dh)
        dx, dg = _rms_bwd(x1_ref[...], g2_ref[...], dh)
        dx1_ref[...] = dx2_ref[...] + dx
        acc_ref[...] += _stack_rows([_colsum(dg)], d)

    return _call(
        body, name="ffn_up_bwd", grid=(t // tm,),
        in_specs=[_rows(tm, n), _next_halo(tm, n, t), _rows(tm, n), _rows(tm, d), _rows(tm, d), _resident((3, n)),
                  _resident((n, d)), _resident((1, d))],
        out_specs=[_rows(tm, n), _rows(tm, d), _acc_spec(d), _acc_spec(n)],
        out_shape=[jax.ShapeDtypeStruct((t, n), BF16), jax.ShapeDtypeStruct((t, d), F32),
                   jax.ShapeDtypeStruct((SUBLANES, d), F32), jax.ShapeDtypeStruct((SUBLANES, n), F32)],
        args=(dup, dup, up_pre, x1, dx2, conv_w, w_u, g2), semantics=("arbitrary",), carry=carry)


def _tn_matmul(a, b, name):
    t, mdim = a.shape
    n = b.shape[1]
    tk = min(1024, t)
    tmm = _pick_tile(mdim, 1536)
    tn = _pick_tile(n, 1024)

    def body(a_ref, b_ref, o_ref, acc_ref):
        k = pl.program_id(2)

        @pl.when(k == 0)
        def _():
            acc_ref[...] = jnp.zeros_like(acc_ref)

        acc_ref[...] += _dot_tn(a_ref[...].astype(BF16), b_ref[...].astype(BF16))

        @pl.when(k == t // tk - 1)
        def _():
            o_ref[...] = acc_ref[...].astype(BF16)

    return pl.pallas_call(
        body, name=name, grid=(mdim // tmm, n // tn, t // tk),
        in_specs=[pl.BlockSpec((tk, tmm), lambda i, j, k: (k, i)), pl.BlockSpec((tk, tn), lambda i, j, k: (k, j))],
        out_specs=pl.BlockSpec((tmm, tn), lambda i, j, k: (i, j)),
        out_shape=jax.ShapeDtypeStruct((mdim, n), BF16),
        scratch_shapes=[pltpu.VMEM((tmm, tn), F32)],
        compiler_params=_params("parallel", "parallel", "arbitrary"),
    )(a, b)


def _mixer_bwd(dx1, gates, yap, ybp, os, lses, b_gate, w_o, w_pa, w_pb):
    t, d = dx1.shape
    cw = w_pa.shape[1]
    tm = min(ROWS_MATMUL, t)

    def body(dx1_ref, gates_ref, yap_ref, ybp_ref, o0_ref, o1_ref, o2_ref, l0_ref, l1_ref, l2_ref, bg_ref, wo_ref,
             wpa_ref, wpb_ref, dgates_ref, dyap_ref, dybp_ref, dya_ref, do0_ref, do1_ref, do2_ref, dl0_ref, dl1_ref,
             dl2_ref, acc_ref):
        m = pl.program_id(0)

        @pl.when(m == 0)
        def _():
            acc_ref[...] = jnp.zeros_like(acc_ref)

        dmg = _dot_nt(dx1_ref[...].astype(BF16), wo_ref[...])
        sa = _sigmoid(gates_ref[:, 0:d].astype(F32) + bg_ref[0:1, :])
        sb = _sigmoid(gates_ref[:, d:2 * d].astype(F32) + bg_ref[1:2, :])
        dyap = (dmg * sa).astype(BF16)
        dybp = (dmg * sb).astype(BF16)
        dga = dmg * yap_ref[...].astype(F32) * sa * (1.0 - sa)
        dgb = dmg * ybp_ref[...].astype(F32) * sb * (1.0 - sb)
        dyap_ref[...] = dyap
        dybp_ref[...] = dybp
        dgates_ref[:, 0:d] = dga.astype(BF16)
        dgates_ref[:, d:2 * d] = dgb.astype(BF16)
        acc_ref[...] += _stack_rows([_colsum(dga), _colsum(dgb)], d)
        dya_ref[...] = _dot(dyap, wpa_ref[...]).astype(BF16)
        dyb = _dot(dybp, wpb_ref[...])

        ri = lax.broadcasted_iota(jnp.int32, (GROUP_W, GROUP_W), 0) // HEAD_DIM
        ci = lax.broadcasted_iota(jnp.int32, (GROUP_W, GROUP_W), 1) // HEAD_DIM
        same_head = (ri == ci).astype(BF16)
        alphas = _group_softmax([_load_streams(r, dil, tm) for r, dil in zip((l0_ref, l1_ref, l2_ref), DILATIONS)])
        prod = jnp.zeros((tm, GROUP_W), F32)
        for i, (o_ref, do_ref, dil) in enumerate(zip((o0_ref, o1_ref, o2_ref), (do0_ref, do1_ref, do2_ref), DILATIONS)):
            dov = alphas[i] * dyb[:, i * GROUP_W:(i + 1) * GROUP_W]
            _store_streams(do_ref, dil, tm, dov.astype(BF16))
            prod = prod + dov * _load_streams(o_ref, dil, tm).astype(F32)
        hi = prod.astype(BF16)
        lo = (prod - hi.astype(F32)).astype(BF16)
        dtot = _dot(hi, same_head) + _dot(lo, same_head)
        for alpha, dl_ref, dil in zip(alphas, (dl0_ref, dl1_ref, dl2_ref), DILATIONS):
            _store_streams(dl_ref, dil, tm, alpha * dtot)

    streams = [_stream_spec(dil, tm, GROUP_W) for dil in DILATIONS]
    res = _call(
        body, name="mixer_bwd", grid=(t // tm,),
        in_specs=[_rows(tm, d), _rows(tm, 2 * d), _rows(tm, d), _rows(tm, d)] + streams * 2
        + [_resident((2, d)), _resident((d, d)), _resident((d, cw)), _resident((d, ATTN_W))],
        out_specs=[_rows(tm, 2 * d), _rows(tm, d), _rows(tm, d), _rows(tm, cw)] + streams * 2 + [_acc_spec(d)],
        out_shape=[jax.ShapeDtypeStruct((t, 2 * d), BF16), jax.ShapeDtypeStruct((t, d), BF16),
                   jax.ShapeDtypeStruct((t, d), BF16), jax.ShapeDtypeStruct((t, cw), BF16)]
        + [jax.ShapeDtypeStruct((dil, t // dil, GROUP_W), BF16) for dil in DILATIONS]
        + [jax.ShapeDtypeStruct((dil, t // dil, GROUP_W), F32) for dil in DILATIONS]
        + [jax.ShapeDtypeStruct((SUBLANES, d), F32)],
        args=(dx1, gates, yap, ybp, *[_stream_view(a, dil) for a, dil in zip(os, DILATIONS)],
              *[_stream_view(a, dil) for a, dil in zip(lses, DILATIONS)], b_gate, w_o, w_pa, w_pb),
        semantics=("arbitrary",))
    dgates, dyap, dybp, dya = res[:4]
    dos = [a.reshape(t, GROUP_W) for a in res[4:7]]
    dls = [a.reshape(t, GROUP_W) for a in res[7:10]]
    return dgates, dyap, dybp, dya, dos, dls, res[10]


def _attn_bwd(s, do, lse, dl, dil, carry=None):
    t = s.shape[0] * s.shape[1]
    nb = t // QBLK
    per_stream = nb // dil
    count = BWD_STEP_BLOCKS
    assert per_stream % count == 0

    def body(q_ref, qn_ref, kc_ref, kp_ref, vc_ref, vp_ref, do_ref, don_ref, lse_ref, lsen_ref, dl_ref, dln_ref,
             ds_ref):
        b = pl.program_id(0)
        lane, heads = _head_masks()
        first_has_prev = lax.rem(count * b, per_stream) != 0
        last_has_next = lax.rem(count * (b + 1), per_stream) != 0

        def cols(v):
            return jnp.concatenate([jnp.sum(jnp.where(lane == h * HEAD_DIM, v, 0.0), axis=1, keepdims=True)
                                    for h in range(HEADS_PER_GROUP)], axis=0)

        def pair(qs, dos, k, v, valid, lse_c, dl_c):
            s = jnp.where(valid, _dot_nt(qs, k) * ATTN_SCALE, NEG_INF)
            p = jnp.exp(s - lse_c)
            ds = p * (_dot_nt(dos, v) - dl_c)
            return p.astype(BF16), ds.astype(BF16)

        for j in range(count):
            rows, hi = slice(j * QBLK, (j + 1) * QBLK), slice((j + 1) * QBLK, (j + 2) * QBLK)
            q, do, lse, dl = q_ref[rows, :], do_ref[rows, :], lse_ref[rows, :], dl_ref[rows, :]
            kc, vc = kc_ref[rows, :], vc_ref[rows, :]
            if j == 0:
                k2 = jnp.concatenate([kp_ref[...], kc], axis=0)
                v2 = jnp.concatenate([vp_ref[...], vc], axis=0)
                mask = _band_mask(first_has_prev)
            else:
                both = slice((j - 1) * QBLK, (j + 1) * QBLK)
                k2, v2, mask = kc_ref[both, :], vc_ref[both, :], _band_mask(True)
            if j < count - 1:
                qn, don, lsen, dln = q_ref[hi, :], do_ref[hi, :], lse_ref[hi, :], dl_ref[hi, :]
                mask_n = _next_mask(True)
            else:
                qn, don, lsen, dln = qn_ref[...], don_ref[...], lsen_ref[...], dln_ref[...]
                mask_n = _next_mask(last_has_next)
            qs, qns = _stack_heads(q, heads), _stack_heads(qn, heads)
            dos, dons = _stack_heads(do, heads), _stack_heads(don, heads)
            p_q, ds_q = pair(qs, dos, k2, v2, mask, cols(lse), cols(dl))
            p_n, ds_n = pair(qns, dons, kc, vc, mask_n, cols(lsen), cols(dln))
            dq = _merge_heads(_dot(ds_q, k2), heads)
            dk = _dot_tn(jnp.concatenate([ds_q[:, QBLK:], ds_n], axis=0), jnp.concatenate([qs, qns], axis=0))
            dv = _dot_tn(jnp.concatenate([p_q[:, QBLK:], p_n], axis=0), jnp.concatenate([dos, dons], axis=0))
            ds_ref[rows, 0:GROUP_W] = (dq * ATTN_SCALE).astype(BF16)
            ds_ref[rows, GROUP_W:2 * GROUP_W] = (dk * ATTN_SCALE).astype(BF16)
            ds_ref[rows, 2 * GROUP_W:3 * GROUP_W] = dv.astype(BF16)

    sv = s.reshape(t, 3 * GROUP_W)
    cur, nxt = _pair_block(0, count), _edge_block(0, count, nb, count)
    return _call(
        body, name=f"attn_bwd_d{dil}", grid=(nb // count,),
        in_specs=[cur, nxt, _pair_block(1, count), _edge_block(1, -1, nb, count), _pair_block(2, count),
                  _edge_block(2, -1, nb, count), cur, nxt, cur, nxt, cur, nxt],
        out_specs=[pl.BlockSpec((count * QBLK, 3 * GROUP_W), lambda b: (b, 0))],
        out_shape=[jax.ShapeDtypeStruct((t, 3 * GROUP_W), BF16)],
        args=(sv, sv, sv, sv, sv, sv, do, do, lse, lse, dl, dl), semantics=("parallel",), carry=carry)


def _conv_mixer_bwd(abcv, dya, conv_w, conv_b):
    t = abcv.shape[0]
    cw = conv_w.shape[1]
    tm = min(1024, t)
    last = t // tm - 1

    def body(a_ref, ap_ref, an_ref, dya_ref, dyan_ref, cw_ref, cb_ref, d_ref, acc_ref):
        m = pl.program_id(0)

        @pl.when(m == 0)
        def _():
            acc_ref[...] = jnp.zeros_like(acc_ref)

        ab = a_ref[:, 0:cw].astype(F32)
        ac = a_ref[:, cw:2 * cw].astype(F32)
        av = a_ref[:, 2 * cw:3 * cw].astype(F32)
        u = ac * av
        hu = ap_ref[:, cw:2 * cw].astype(F32) * ap_ref[:, 2 * cw:3 * cw].astype(F32)
        hu = jnp.where(m > 0, hu, 0.0)
        u1 = _shift_down(u, hu, 1)
        u2 = _shift_down(u, hu, 2)
        cv = cw_ref[0:1, :] * u2 + cw_ref[1:2, :] * u1 + cw_ref[2:3, :] * u + cb_ref[...]
        dya_v = dya_ref[...].astype(F32)
        dcv = dya_v * ab
        ndcv = jnp.where(m < last, dyan_ref[...].astype(F32) * an_ref[:, 0:cw].astype(F32), 0.0)
        du = (cw_ref[2:3, :] * dcv + cw_ref[1:2, :] * _shift_up(dcv, ndcv, 1)
              + cw_ref[0:1, :] * _shift_up(dcv, ndcv, 2))
        d_ref[:, 0:cw] = (dya_v * cv).astype(BF16)
        d_ref[:, cw:2 * cw] = (du * av).astype(BF16)
        d_ref[:, 2 * cw:3 * cw] = (du * ac).astype(BF16)
        acc_ref[...] += _stack_rows([_colsum(dcv * u2), _colsum(dcv * u1), _colsum(dcv * u), _colsum(dcv)], cw)

    return pl.pallas_call(
        body, name="conv_mixer_bwd", grid=(t // tm,),
        in_specs=[_rows(tm, 3 * cw), _prev_halo(tm, 3 * cw), _next_halo(tm, 3 * cw, t), _rows(tm, cw),
                  _next_halo(tm, cw, t), _resident((3, cw)), _resident((1, cw))],
        out_specs=[_rows(tm, 3 * cw), _acc_spec(cw)],
        out_shape=[jax.ShapeDtypeStruct((t, 3 * cw), BF16), jax.ShapeDtypeStruct((SUBLANES, cw), F32)],
        compiler_params=_params("arbitrary"),
    )(abcv, abcv, abcv, dya, dya, conv_w, conv_b)


def _in_proj_bwd(x, dx1, dabcv, dss, dgates, w_in, g1, carry=None):
    t, d = x.shape
    qkv0 = dabcv.shape[1]
    n = w_in.shape[0]
    tm = min(ROWS_MATMUL, t)

    def body(x_ref, dx1_ref, da_ref, ds0_ref, ds1_ref, ds2_ref, dg_ref, w_ref, g_ref, dx_ref, acc_ref):
        m = pl.program_id(0)

        @pl.when(m == 0)
        def _():
            acc_ref[...] = jnp.zeros_like(acc_ref)

        dss_tok = [_load_streams(ds_ref, dil, tm) for ds_ref, dil in zip((ds0_ref, ds1_ref, ds2_ref), DILATIONS)]
        dqkv = jnp.concatenate([ds[:, j * GROUP_W:(j + 1) * GROUP_W] for j in range(3) for ds in dss_tok], axis=1)
        dh = (_dot(da_ref[...], w_ref[0:qkv0, :]) + _dot(dqkv, w_ref[qkv0:qkv0 + 3 * ATTN_W, :])
              + _dot(dg_ref[...], w_ref[qkv0 + 3 * ATTN_W:n, :]))
        dx, dg = _rms_bwd(x_ref[...], g_ref[...], dh)
        dx_ref[...] = dx1_ref[...] + dx
        acc_ref[...] += _stack_rows([_colsum(dg)], d)

    return _call(
        body, name="in_proj_bwd", grid=(t // tm,),
        in_specs=[_rows(tm, d), _rows(tm, d), _rows(tm, qkv0)]
        + [_stream_spec(dil, tm, 3 * GROUP_W) for dil in DILATIONS]
        + [_rows(tm, 2 * d), _resident((n, d)), _resident((1, d))],
        out_specs=[_rows(tm, d), _acc_spec(d)],
        out_shape=[jax.ShapeDtypeStruct((t, d), F32), jax.ShapeDtypeStruct((SUBLANES, d), F32)],
        args=(x, dx1, dabcv, *[_stream_view(a, dil) for a, dil in zip(dss, DILATIONS)], dgates, w_in, g1),
        semantics=("arbitrary",), carry=carry)


def _dw_in_qkv(ds, h, dil):
    t, d = h.shape
    tk = min(1024, t)
    sub = min(256, t)
    width = 3 * GROUP_W

    def body(ds_ref, h_ref, o_ref, acc_ref):
        k = pl.program_id(0)

        @pl.when(k == 0)
        def _():
            acc_ref[...] = jnp.zeros_like(acc_ref)

        upd = None
        for i in range(tk // sub):
            rows = ds_ref[:, i * (sub // dil):(i + 1) * (sub // dil), :].reshape(sub, width)
            if dil > 1:
                rows = _permute_rows(_perm(dil, sub, inverse=True), rows)
            term = _dot_tn(rows, h_ref[i * sub:(i + 1) * sub, :])
            upd = term if upd is None else upd + term
        acc_ref[...] += upd

        @pl.when(k == t // tk - 1)
        def _():
            o_ref[...] = acc_ref[...].astype(BF16)

    return pl.pallas_call(
        body, name=f"dw_in_qkv_d{dil}", grid=(t // tk,),
        in_specs=[_stream_spec(dil, tk, width), _rows(tk, d)],
        out_specs=pl.BlockSpec((width, d), lambda k: (0, 0)),
        out_shape=jax.ShapeDtypeStruct((width, d), BF16),
        scratch_shapes=[pltpu.VMEM((width, d), F32)],
        compiler_params=_params("arbitrary"),
    )(_stream_view(ds, dil), h)


def _local_step(x, target, p, late):
    cw = p["conv_a_w"].shape[1]
    (h, abcv, gates, *ss), (g_up,) = _in_proj(x, p["norm_mix_g"], p["w_in"], cw,
                                              carry=_TwoLevelGather([late["w_up"]]))
    w_up = _full_from_gathered(g_up)
    (o0, lse0), (g_pa,) = _attn_fwd(ss[0], DILATIONS[0], carry=_Exchange("gather", [late["w_proj_a"]]))
    (o1, lse1), (g_pb,) = _attn_fwd(ss[1], DILATIONS[1], carry=_Exchange("gather", [late["w_proj_b"]]))
    (o2, lse2), (g_out,) = _attn_fwd(ss[2], DILATIONS[2], carry=_Exchange("gather", [late["w_out"]]))
    w_pa, w_pb, w_out = [_full_from_gathered(g) for g in (g_pa, g_pb, g_out)]
    os, lses = (o0, o1, o2), (lse0, lse1, lse2)
    (x1, ya, yb, yap, ybp, merged), (g_down,) = _mixer_out(
        x, abcv, gates, os, lses, p["conv_a_w"], p["conv_a_b"], p["b_gate"], w_pa, w_pb, w_out,
        carry=_TwoLevelGather([late["w_down"]]))
    w_down = _full_from_gathered(g_down)
    h2, up_pre, act, conv, dx2, dx2i, acc_gf, loss = _ffn_fwd(x1, target, p["norm_ffn_g"], w_up, p["ffn_conv_w"],
                                                              p["ffn_conv_b"], w_down, p["final_norm_g"])

    parts, got = {}, {}
    dup, acc_fb = _ffn_act_bwd(dx2i, conv, w_down)
    parts["w_down"] = _by_destination(_tn_matmul(act, dx2i, "dw_down"))
    (dpre, dx1, acc_g2, acc_fw), (got["w_down"],) = _ffn_up_bwd(dup, up_pre, x1, dx2, p["ffn_conv_w"], w_up,
                                                                p["norm_ffn_g"],
                                                                carry=_Exchange("scatter", [parts["w_down"]]))
    parts["w_up"] = _by_destination(_tn_matmul(dpre, h2, "dw_up"))
    dgates, dyap, dybp, dya, dos, dls, acc_bg = _mixer_bwd(dx1, gates, yap, ybp, os, lses, p["b_gate"], w_out,
                                                           w_pa, w_pb)
    parts["w_out"] = _by_destination(_tn_matmul(merged, dx1, "dw_out"))
    parts["w_proj_a"] = _by_destination(_tn_matmul(dyap, ya, "dw_proj_a"))
    parts["w_proj_b"] = _by_destination(_tn_matmul(dybp, yb, "dw_proj_b"))
    minor = ("w_out", "w_proj_a", "w_proj_b")
    half = parts["w_up"].shape[1] // 2
    (ds0,), received = _attn_bwd(ss[0], dos[0], lses[0], dls[0], DILATIONS[0],
                                 carry=_Exchange("scatter", [parts[n] for n in minor]))
    got.update(zip(minor, received))
    (ds1,), first_half = _attn_bwd(ss[1], dos[1], lses[1], dls[1], DILATIONS[1],
                                   carry=_Exchange("scatter", [parts["w_up"]], rows=(0, half)))
    (ds2,), (got["w_up"],) = _attn_bwd(ss[2], dos[2], lses[2], dls[2], DILATIONS[2],
                                       carry=_Exchange("scatter", [parts["w_up"]], rows=(half, half),
                                                       into=first_half))
    dss = [ds0, ds1, ds2]
    dabcv, acc_ca = _conv_mixer_bwd(abcv, dya, p["conv_a_w"], p["conv_a_b"])
    dw_s = [_dw_in_qkv(ds, h, dil) for ds, dil in zip(dss, DILATIONS)]
    dw_qkv = [w[j * GROUP_W:(j + 1) * GROUP_W] for j in range(3) for w in dw_s]
    g_w_in = jnp.concatenate([_tn_matmul(dabcv, h, "dw_in_a"), *dw_qkv, _tn_matmul(dgates, h, "dw_in_g")], axis=0)
    parts["w_in"] = _by_destination(g_w_in)
    (dx, acc_g1), (got["w_in"],) = _in_proj_bwd(x, dx1, dabcv, dss, dgates, p["w_in"], p["norm_mix_g"],
                                                carry=_Exchange("scatter", [parts["w_in"]]))
    small = dict(norm_mix_g=acc_g1[0:1], b_gate=acc_bg[0:2], conv_a_w=acc_ca[0:3], conv_a_b=acc_ca[3:4],
                 norm_ffn_g=acc_g2[0:1], ffn_conv_w=acc_fw[0:3], ffn_conv_b=acc_fb[0:1], final_norm_g=acc_gf[0:1])
    return loss[0, 0], dx, parts, got, small


class _TwoLevelGather:
    def __init__(self, arrays):
        self.arrays, self.into = list(arrays), []
        n = len(self.arrays)
        self.out_shape = [jax.ShapeDtypeStruct((N_DEV,) + a.shape, a.dtype) for a in self.arrays]
        self.scratch = [pltpu.SemaphoreType.DMA((n, 7)), pltpu.SemaphoreType.DMA((n, 7)),
                        pltpu.SemaphoreType.DMA((n,))]

    def _parts(self, ins, outs, sems):
        send_sems, recv_sems, local_sems = sems
        x, y, c = _mesh_pos()
        me, sibling = (x, y, c), (x, y, 1 - c)
        chips = [(1 - x, y), (x, 1 - y), (1 - x, 1 - y)]

        def copy(i, k, block, to, src=None):
            rows = outs[i].at[_dev_index(*block)]
            return pltpu.make_async_remote_copy(
                src_ref=rows if src is None else src, dst_ref=rows, send_sem=send_sems.at[i, k],
                recv_sem=recv_sems.at[i, k], device_id=to, device_id_type=MESH)

        mine = [pltpu.make_async_copy(ins[i], outs[i].at[_dev_index(*me)], local_sems.at[i])
                for i in range(len(ins))]
        first = []
        for i in range(len(ins)):
            first.append(copy(i, 0, me, sibling, src=ins[i]))
            first += [copy(i, 1 + j, me, (*chip, c), src=ins[i]) for j, chip in enumerate(chips)]
        return copy, mine, first, me, sibling, chips, c

    def start(self, ins, outs, sems):
        _, mine, first, *_ = self._parts(ins, outs, sems)
        for cp in mine + first:
            cp.start()

    def wait(self, ins, outs, sems):
        copy, mine, first, me, sibling, chips, c = self._parts(ins, outs, sems)
        passed = []
        for i in range(len(ins)):
            for j, chip in enumerate(chips):
                copy(i, 1 + j, (*chip, c), me).wait_recv()
                fw = copy(i, 4 + j, (*chip, c), sibling)
                fw.start()
                passed.append(fw)
        for i in range(len(ins)):
            copy(i, 0, sibling, me).wait_recv()
            for j, chip in enumerate(chips):
                copy(i, 4 + j, (*chip, 1 - c), me).wait_recv()
        for cp in first + passed:
            cp.wait_send()
        for cp in mine:
            cp.wait()


def _all_gather(shards):
    n = len(shards)

    def body(*refs):
        ins, outs = refs[:n], refs[n:2 * n]
        send_sems, recv_sems, local_sems = refs[2 * n:]
        x, y, c = _mesh_pos()
        me, sibling = (x, y, c), (x, y, 1 - c)
        chips = [(1 - x, y), (x, 1 - y), (1 - x, 1 - y)]

        def copy(i, k, block, to, src=None):
            rows = outs[i].at[_dev_index(*block)]
            return pltpu.make_async_remote_copy(
                src_ref=rows if src is None else src, dst_ref=rows, send_sem=send_sems.at[i, k],
                recv_sem=recv_sems.at[i, k], device_id=to, device_id_type=MESH)

        mine, first, passed = [], [], []
        for i in range(n):
            cp = pltpu.make_async_copy(ins[i], outs[i].at[_dev_index(*me)], local_sems.at[i])
            cp.start()
            mine.append(cp)
            first.append(copy(i, 0, me, sibling, src=ins[i]))
            first += [copy(i, 1 + j, me, (*chip, c), src=ins[i]) for j, chip in enumerate(chips)]
        for cp in first:
            cp.start()
        for i in range(n):
            for j, chip in enumerate(chips):
                copy(i, 1 + j, (*chip, c), me).wait_recv()
                fw = copy(i, 4 + j, (*chip, c), sibling)
                fw.start()
                passed.append(fw)
        for i in range(n):
            copy(i, 0, sibling, me).wait_recv()
            for j, chip in enumerate(chips):
                copy(i, 4 + j, (*chip, 1 - c), me).wait_recv()
        for cp in first + passed:
            cp.wait_send()
        for cp in mine:
            cp.wait()

    return pl.pallas_call(
        body, name="all_gather_weights",
        in_specs=[ANY] * n, out_specs=[ANY] * n,
        out_shape=[jax.ShapeDtypeStruct((N_DEV,) + s.shape, s.dtype) for s in shards],
        scratch_shapes=[pltpu.SemaphoreType.DMA((n, 7)), pltpu.SemaphoreType.DMA((n, 7)),
                        pltpu.SemaphoreType.DMA((n,))],
    )(*shards)


def _all_reduce_small(v):
    r = v.shape[0]

    def body(v_ref, o_ref, gath, send_sems, recv_sems):
        x, y, c = _mesh_pos()
        me = _dev_index(x, y, c)
        gath[me] = v_ref[...]
        flips = [(kx, ky, kc) for kx in (0, 1) for ky in (0, 1) for kc in (0, 1)][1:]
        copies = []
        for k, (kx, ky, kc) in enumerate(flips):
            px = 1 - x if kx else x
            py = 1 - y if ky else y
            pc = 1 - c if kc else c
            cp = pltpu.make_async_remote_copy(
                src_ref=v_ref, dst_ref=gath.at[me], send_sem=send_sems.at[k], recv_sem=recv_sems.at[k],
                device_id=(px, py, pc), device_id_type=MESH)
            cp.start()
            copies.append((cp, _dev_index(px, py, pc)))
        for k, (cp, peer) in enumerate(copies):
            pltpu.make_async_remote_copy(
                src_ref=v_ref, dst_ref=gath.at[peer], send_sem=send_sems.at[k], recv_sem=recv_sems.at[k],
                device_id=(x, y, c), device_id_type=MESH).wait_recv()
        for cp, _ in copies:
            cp.wait_send()
        total = gath[0]
        for j in range(1, N_DEV):
            total = total + gath[j]
        o_ref[...] = total

    return pl.pallas_call(
        body, name="all_reduce_small",
        in_specs=[pl.BlockSpec(memory_space=pltpu.VMEM)], out_specs=pl.BlockSpec(memory_space=pltpu.VMEM),
        out_shape=jax.ShapeDtypeStruct((r, LANES), F32),
        scratch_shapes=[pltpu.VMEM((N_DEV, r, LANES), F32), pltpu.SemaphoreType.DMA((7,)),
                        pltpu.SemaphoreType.DMA((7,))],
    )(v)


def _adamw_math(w, g, m, v):
    m2 = ADAM_B1 * m + (1.0 - ADAM_B1) * g
    v2 = ADAM_B2 * v + (1.0 - ADAM_B2) * (g * g)
    m_hat = m2 / (1.0 - ADAM_B1 ** ADAM_STEP)
    v_hat = v2 / (1.0 - ADAM_B2 ** ADAM_STEP)
    delta = -ADAM_LR * (m_hat / (jnp.sqrt(v_hat) + ADAM_EPS) + ADAM_WD * w)
    return delta, m2, v2


def _adamw_big(w, m, v, part, got, me):
    r, c = part.shape[1:]
    flip = w.shape != (r, c)
    tr = r if flip else max(t for t in range(HALO, min(r, 512) + 1, HALO) if r % t == 0)

    def body(me_ref, w_ref, m_ref, v_ref, own_ref, *rest):
        del me_ref
        got_refs, (g_out, d_out, m_out, v_out) = rest[:N_DEV - 1], rest[N_DEV - 1:]
        g = own_ref[...].astype(F32)
        for ref in got_refs:
            g = g + ref[...].astype(F32)
        if flip:
            g = g.T
        delta, m2, v2 = _adamw_math(w_ref[...], g, m_ref[...], v_ref[...])
        g_out[...] = g
        d_out[...] = delta
        m_out[...] = m2
        v_out[...] = v2

    def peer_block(k):
        return pl.BlockSpec((None, tr, c), lambda i, me_ref: (jnp.bitwise_xor(me_ref[0], k), i, 0))

    plain = pl.BlockSpec(w.shape if flip else (tr, c), lambda i, me_ref: (i, 0))
    out = jax.ShapeDtypeStruct(w.shape, F32)
    return pl.pallas_call(
        body, name="adamw_big",
        grid_spec=pltpu.PrefetchScalarGridSpec(
            num_scalar_prefetch=1, grid=(r // tr,),
            in_specs=[plain, plain, plain] + [peer_block(k) for k in range(N_DEV)],
            out_specs=[plain] * 4),
        out_shape=[out] * 4,
        compiler_params=_params("parallel"),
    )(me, w, m, v, part, *([got] * (N_DEV - 1)))


def _adamw_small(ws, gs, ms, vs):
    n = len(ws)

    def body(*refs):
        ins, outs = refs[:4 * n], refs[4 * n:]
        for i in range(n):
            delta, m2, v2 = _adamw_math(ins[i][...], ins[n + i][...], ins[2 * n + i][...], ins[3 * n + i][...])
            outs[i][...] = delta
            outs[n + i][...] = m2
            outs[2 * n + i][...] = v2

    out = [jax.ShapeDtypeStruct(w.shape, F32) for w in ws]
    res = pl.pallas_call(body, name="adamw_small", out_shape=out * 3)(*ws, *gs, *ms, *vs)
    return res[:n], res[n:2 * n], res[2 * n:]


BIG = ("w_in", "w_proj_a", "w_proj_b", "w_out", "w_up", "w_down")
LATE = ("w_proj_a", "w_proj_b", "w_out", "w_up", "w_down")
COLUMN_SHARDED = ("w_in", "w_proj_a", "w_proj_b", "w_up")
WIDE_COLUMN_SHARDED = ("w_in", "w_up")
SMALL = ("norm_mix_g", "b_gate", "conv_a_w", "conv_a_b", "norm_ffn_g", "ffn_conv_w", "ffn_conv_b", "final_norm_g")
SMALL_SHARDED = ("b_gate", "conv_a_w", "ffn_conv_w")
WEIGHTS = ("norm_mix_g", "w_in", "b_gate", "conv_a_w", "conv_a_b", "w_proj_a", "w_proj_b", "w_out", "norm_ffn_g",
           "w_up", "ffn_conv_w", "ffn_conv_b", "w_down", "final_norm_g")


def _pack(vectors, rows):
    flat = jnp.concatenate([v.reshape(-1) for v in vectors])
    return jnp.pad(flat, (0, rows * LANES - flat.shape[0])).reshape(rows, LANES)


def _packed_rows(count):
    rows = -(-count // LANES)
    return -(-rows // SUBLANES) * SUBLANES


def _unpack(packed, shapes):
    flat = packed.reshape(-1)
    out, lo = [], 0
    for s in shapes:
        size = 1
        for dim in s:
            size *= dim
        out.append(flat[lo:lo + size].reshape(s))
        lo += size
    return out


def _full_from_gathered(gathered):
    _, r, c = gathered.shape
    return gathered.reshape(N_DEV * r, c)


def _by_destination(grad):
    rr, cc = grad.shape
    return grad.reshape(N_DEV, rr // N_DEV, cc)


def _block2d(name, a):
    a = a.reshape(a.shape[-2:])
    return a.T if name in WIDE_COLUMN_SHARDED else a


def kernel(x, norm_mix_g, w_in, b_gate, conv_a_w, conv_a_b, w_proj_a, w_proj_b, w_out, norm_ffn_g, w_up, ffn_conv_w, ffn_conv_b, w_down, final_norm_g, loss_target, m_norm_mix_g, m_w_in, m_b_gate, m_conv_a_w, m_conv_a_b, m_w_proj_a, m_w_proj_b, m_w_out, m_norm_ffn_g, m_w_up, m_ffn_conv_w, m_ffn_conv_b, m_w_down, m_final_norm_g, v_norm_mix_g, v_w_in, v_b_gate, v_conv_a_w, v_conv_a_b, v_w_proj_a, v_w_proj_b, v_w_out, v_norm_ffn_g, v_w_up, v_ffn_conv_w, v_ffn_conv_b, v_w_down, v_final_norm_g):
    given = dict(locals())
    shard = {n: given[n] for n in WEIGHTS}
    mom_m = {n: given["m_" + n] for n in WEIGHTS}
    mom_v = {n: given["v_" + n] for n in WEIGHTS}
    xi, yi, ci = _mesh_pos()
    me = _dev_index(xi, yi, ci)
    me1 = me.astype(jnp.int32).reshape(1)

    big2d = {n: _block2d(n, shard[n]) for n in BIG}
    small_shapes = [shard[n].shape[1:] for n in SMALL_SHARDED]
    n_small = sum(s[0] * s[1] for s in small_shapes)
    packed_small = _pack([shard[n] for n in SMALL_SHARDED], _packed_rows(n_small))
    gathered = _all_gather([big2d["w_in"].astype(BF16), packed_small])
    p = {"w_in": _full_from_gathered(gathered[0])}
    flat_small = gathered[-1].reshape(N_DEV, -1)
    lo = 0
    for n, (rows, width) in zip(SMALL_SHARDED, small_shapes):
        blocks = flat_small[:, lo:lo + rows * width].reshape(N_DEV, rows, width)
        p[n] = blocks.transpose(1, 0, 2).reshape(rows, N_DEV * width)
        lo += rows * width
    p["norm_mix_g"], p["norm_ffn_g"] = shard["norm_mix_g"], shard["norm_ffn_g"]
    p["conv_a_b"], p["ffn_conv_b"] = shard["conv_a_b"], shard["ffn_conv_b"]
    p["final_norm_g"] = shard["final_norm_g"].reshape(1, -1)
    late = {n: (big2d[n].T if n in ("w_proj_a", "w_proj_b") else big2d[n]).astype(BF16) for n in LATE}

    loss_part, dx, parts, got, g_small = _local_step(x[0], loss_target[0], p, late)

    results = {}
    for n in BIG:
        outs = _adamw_big(big2d[n], _block2d(n, mom_m[n]), _block2d(n, mom_v[n]), parts[n], got[n], me1)
        results[n] = [_block2d(n, o).reshape(shard[n].shape) for o in outs]

    small_full_shapes = [g_small[n].shape for n in SMALL]
    n_vec = sum(s[0] * s[1] for s in small_full_shapes) + 1
    packed = _pack([g_small[n] for n in SMALL] + [loss_part.reshape(1)], _packed_rows(n_vec))
    reduced = _all_reduce_small(packed)
    *g_full, loss_vec = _unpack(reduced, small_full_shapes + [(1,)])
    loss = loss_vec[0]
    own_g = []
    for n, g in zip(SMALL, g_full):
        if n in SMALL_SHARDED:
            width = shard[n].shape[-1]
            g = lax.dynamic_slice_in_dim(g, me * width, width, axis=1)
        own_g.append(g.reshape(shard[n].shape))
    def rows2d(a):
        return a.reshape(-1, a.shape[-1])

    deltas, new_ms, new_vs = _adamw_small([rows2d(shard[n]) for n in SMALL], [rows2d(g) for g in own_g],
                                          [rows2d(mom_m[n]) for n in SMALL], [rows2d(mom_v[n]) for n in SMALL])
    for i, n in enumerate(SMALL):
        results[n] = [own_g[i]] + [a.reshape(shard[n].shape) for a in (deltas[i], new_ms[i], new_vs[i])]

    grad_x = dx.reshape(x.shape)
    return (loss, grad_x, *[results[n][0] for n in WEIGHTS], *[results[n][1] for n in WEIGHTS],
            *[results[n][2] for n in WEIGHTS], *[results[n][3] for n in WEIGHTS])
```

```python
import functools

import jax
import jax.numpy as jnp
from jax import lax
from jax.experimental import pallas as pl
from jax.experimental.pallas import tpu as pltpu

F32 = jnp.float32
BF16 = jnp.bfloat16
MESH = pl.DeviceIdType.MESH

N_DEV = 8
RMS_EPS = 1e-6
NEG_INF = -1e30
N_GROUPS = 3
DILATIONS = (1, 4, 16)
HEADS_PER_GROUP = 4
HEAD_DIM = 64
GROUP_W = HEADS_PER_GROUP * HEAD_DIM
ATTN_W = N_GROUPS * GROUP_W
QBLK = 128
STEP_BLOCKS = 4
BWD_STEP_BLOCKS = 2
ATTN_SCALE = HEAD_DIM ** -0.5

ADAM_LR = 0.001
ADAM_B1 = 0.9
ADAM_B2 = 0.999
ADAM_EPS = 1e-08
ADAM_WD = 0.01
ADAM_STEP = 10

PERM_TOKENS = 256
ROWS_MATMUL = 512
HALO = 16
LANES = 128
SUBLANES = 8
VMEM_LIMIT_BYTES = 56 * 1024 * 1024


def _params(*sem):
    return pltpu.CompilerParams(dimension_semantics=sem, vmem_limit_bytes=VMEM_LIMIT_BYTES)


def _pick_tile(n, cap):
    if n <= cap:
        return n
    best = None
    for t in range(LANES, cap + 1, LANES):
        if n % t == 0:
            best = t
    assert best is not None, (n, cap)
    return best


def _rows(tm, c, j=0):
    return pl.BlockSpec((tm, c), lambda m: (m, j))


def _prev_halo(tm, c):
    return pl.BlockSpec((HALO, c), lambda m: (jnp.maximum(m * (tm // HALO) - 1, 0), 0))


def _next_halo(tm, c, t_total):
    last = t_total // HALO - 1
    return pl.BlockSpec((HALO, c), lambda m: (jnp.minimum((m + 1) * (tm // HALO), last), 0))


def _resident(shape):
    nd = len(shape)
    return pl.BlockSpec(shape, lambda *_: (0,) * nd, pipeline_mode=pl.Buffered(1))


def _acc_spec(c):
    return pl.BlockSpec((SUBLANES, c), lambda *_: (0, 0))


def _shift_down(u, halo, k):
    edge = jnp.concatenate([halo[HALO - SUBLANES:], u[:SUBLANES]], axis=0)
    head = pltpu.roll(edge, k, 0)[SUBLANES:]
    return jnp.concatenate([head, pltpu.roll(u, k, 0)[SUBLANES:]], axis=0)


def _shift_up(u, halo, k):
    n = u.shape[0]
    edge = jnp.concatenate([u[n - SUBLANES:], halo[:SUBLANES]], axis=0)
    tail = pltpu.roll(edge, 2 * SUBLANES - k, 0)[:SUBLANES]
    return jnp.concatenate([pltpu.roll(u, n - k, 0)[:n - SUBLANES], tail], axis=0)


def _interleave(tm, inverse=False):
    return _perm(tm // SUBLANES, tm, inverse)


def _edge_groups(u, halo, k, from_end):
    n = u.shape[0]
    sub = lax.broadcasted_iota(jnp.int32, (SUBLANES, u.shape[1]), 0)
    out = []
    for j in range(2 - k, 2):
        lo = n - HALO + j * SUBLANES if from_end else j * SUBLANES
        own, other = u[lo:lo + SUBLANES], halo[j * SUBLANES:(j + 1) * SUBLANES]
        if from_end:
            out.append(pltpu.roll(jnp.where(sub == SUBLANES - 1, other, own), 1, 0))
        else:
            out.append(pltpu.roll(jnp.where(sub == 0, other, own), SUBLANES - 1, 0))
    return out


def _shift_down_il(u, halo, k):
    return jnp.concatenate(_edge_groups(u, halo, k, True) + [u[:u.shape[0] - k * SUBLANES]], axis=0)


def _shift_up_il(u, halo, k):
    if k == 1:
        edge = _edge_groups(u, halo, 2, False)[:1]
    else:
        edge = _edge_groups(u, halo, 2, False)
    return jnp.concatenate([u[k * SUBLANES:]] + edge, axis=0)


def _stack_rows(rows, c):
    idx = lax.broadcasted_iota(jnp.int32, (SUBLANES, c), 0)
    out = jnp.zeros((SUBLANES, c), F32)
    for i, r in enumerate(rows):
        out = out + jnp.where(idx == i, r, 0.0)
    return out


def _colsum(v):
    return jnp.sum(v, axis=0, keepdims=True)


def _sigmoid(v):
    return 0.5 * jnp.tanh(0.5 * v) + 0.5


def _rms_fwd(xv, g):
    r = lax.rsqrt(jnp.mean(xv * xv, axis=-1, keepdims=True) + RMS_EPS)
    return xv * r * g, r


def _rms_bwd(xv, g, dy):
    r = lax.rsqrt(jnp.mean(xv * xv, axis=-1, keepdims=True) + RMS_EPS)
    xn = xv * r
    dxn = dy * g
    dx = r * (dxn - xn * jnp.mean(dxn * xn, axis=-1, keepdims=True))
    return dx, dy * xn


def _dot(a, b):
    return jnp.dot(a, b, preferred_element_type=F32)


def _dot_nt(a, b):
    return lax.dot_general(a, b, (((1,), (1,)), ((), ())), preferred_element_type=F32)


def _dot_tn(a, b):
    return lax.dot_general(a, b, (((0,), (0,)), ((), ())), preferred_element_type=F32)


def _perm(dil, n, inverse=False):
    i = lax.broadcasted_iota(jnp.int32, (n, n), 0)
    j = lax.broadcasted_iota(jnp.int32, (n, n), 1)
    if inverse:
        i, j = j, i
    per = n // dil
    return (j == (i % per) * dil + i // per).astype(BF16)


def _permute_rows(pm, v):
    if v.dtype == BF16:
        return _dot(pm, v).astype(BF16)
    hi = v.astype(BF16)
    lo = (v - hi.astype(F32)).astype(BF16)
    return _dot(pm, hi) + _dot(pm, lo)


def _stream_view(a, dil):
    t, c = a.shape
    return a.reshape(dil, t // dil, c)


def _stream_spec(dil, tm, c):
    return pl.BlockSpec((dil, tm // dil, c), lambda m: (0, m, 0))


def _load_streams(ref, dil, tm):
    c = ref.shape[-1]
    if dil == 1:
        return ref[...].reshape(tm, c)
    sub = min(PERM_TOKENS, tm)
    pm = _perm(dil, sub, inverse=True)
    parts = [_permute_rows(pm, ref[:, i * (sub // dil):(i + 1) * (sub // dil), :].reshape(sub, c))
             for i in range(tm // sub)]
    return parts[0] if len(parts) == 1 else jnp.concatenate(parts, axis=0)


def _store_streams(ref, dil, tm, v):
    if dil == 1:
        ref[...] = v.reshape(ref.shape).astype(ref.dtype)
        return
    sub = min(PERM_TOKENS, tm)
    pm = _perm(dil, sub)
    for i in range(tm // sub):
        piece = _permute_rows(pm, v[i * sub:(i + 1) * sub])
        ref[:, i * (sub // dil):(i + 1) * (sub // dil), :] = piece.reshape(dil, sub // dil, -1).astype(ref.dtype)


ANY = pl.BlockSpec(memory_space=pl.ANY)


def _mesh_pos():
    return lax.axis_index("x"), lax.axis_index("y"), lax.axis_index("c")


def _dev_index(px, py, pc):
    return 4 * px + 2 * py + pc


class _Exchange:
    def __init__(self, mode, arrays, rows=None, into=()):
        self.mode, self.arrays, self.rows, self.into = mode, list(arrays), rows, list(into)
        n = len(self.arrays)
        if mode == "gather":
            self.out_shape = [jax.ShapeDtypeStruct((N_DEV,) + a.shape, a.dtype) for a in self.arrays]
        else:
            self.out_shape = [jax.ShapeDtypeStruct(a.shape, a.dtype) for a in self.arrays]
        self.scratch = [pltpu.SemaphoreType.DMA((n, N_DEV - 1)), pltpu.SemaphoreType.DMA((n, N_DEV - 1)),
                        pltpu.SemaphoreType.DMA((n,))]

    def _peers(self):
        x, y, c = _mesh_pos()
        flips = [(kx, ky, kc) for kx in (0, 1) for ky in (0, 1) for kc in (0, 1)][1:]
        peers = [(1 - x if kx else x, 1 - y if ky else y, 1 - c if kc else c) for kx, ky, kc in flips]
        return _dev_index(x, y, c), peers

    def _copy(self, ins, outs, sems, i, k, peer, me, sending):
        src = ins[i] if self.mode == "gather" else ins[i].at[_dev_index(*peer)]
        dst = outs[i].at[me if sending else _dev_index(*peer)]
        if self.rows is not None:
            src, dst = src.at[pl.ds(*self.rows)], dst.at[pl.ds(*self.rows)]
        return pltpu.make_async_remote_copy(src_ref=src, dst_ref=dst, send_sem=sems[0].at[i, k],
                                            recv_sem=sems[1].at[i, k], device_id=peer, device_id_type=MESH)

    def _own(self, ins, outs, sems, i, me):
        return pltpu.make_async_copy(ins[i], outs[i].at[me], sems[2].at[i])

    def start(self, ins, outs, sems):
        me, peers = self._peers()
        for i in range(len(ins)):
            if self.mode == "gather":
                self._own(ins, outs, sems, i, me).start()
            for k, peer in enumerate(peers):
                self._copy(ins, outs, sems, i, k, peer, me, True).start()

    def wait(self, ins, outs, sems):
        me, peers = self._peers()
        for i in range(len(ins)):
            for k, peer in enumerate(peers):
                self._copy(ins, outs, sems, i, k, peer, me, False).wait_recv()
            for k, peer in enumerate(peers):
                self._copy(ins, outs, sems, i, k, peer, me, True).wait_send()
            if self.mode == "gather":
                self._own(ins, outs, sems, i, me).wait()


def _call(body, *, name, grid, in_specs, out_specs, out_shape, args, semantics, carry=None, scratch=()):
    if carry is None:
        return pl.pallas_call(body, name=name, grid=grid, in_specs=in_specs, out_specs=out_specs,
                              out_shape=out_shape, scratch_shapes=list(scratch),
                              compiler_params=_params(*semantics))(*args)
    n_in, n_out, n_x, n_s = len(in_specs), len(out_specs), len(carry.arrays), len(scratch)
    n_into = len(carry.into)
    all_in = n_in + n_x + n_into

    def carried(*refs):
        ins, x_ins = refs[:n_in], refs[n_in:n_in + n_x]
        outs = refs[all_in:all_in + n_out]
        x_outs = refs[all_in + n_out:all_in + n_out + n_x]
        own = refs[all_in + n_out + n_x:all_in + n_out + n_x + n_s]
        sems = refs[all_in + n_out + n_x + n_s:]
        first = functools.reduce(jnp.logical_and, [pl.program_id(a) == 0 for a in range(len(grid))])
        last = functools.reduce(jnp.logical_and, [pl.program_id(a) == grid[a] - 1 for a in range(len(grid))])

        @pl.when(first)
        def _():
            carry.start(x_ins, x_outs, sems)

        body(*ins, *outs, *own)

        @pl.when(last)
        def _():
            carry.wait(x_ins, x_outs, sems)

    res = pl.pallas_call(
        carried, name=name, grid=grid, in_specs=list(in_specs) + [ANY] * (n_x + n_into),
        out_specs=list(out_specs) + [ANY] * n_x, out_shape=list(out_shape) + carry.out_shape,
        input_output_aliases={n_in + n_x + i: n_out + i for i in range(n_into)},
        scratch_shapes=list(scratch) + carry.scratch, compiler_params=_params(*["arbitrary"] * len(grid)),
    )(*args, *carry.arrays, *carry.into)
    return list(res[:n_out]), list(res[n_out:])


def _in_proj(x, g, wt, cw, carry=None):
    t, d = x.shape
    n = wt.shape[0]
    tm = min(ROWS_MATMUL, t)
    qkv0 = 3 * cw

    def body(x_ref, g_ref, wt_ref, h_ref, abcv_ref, gates_ref, *s_refs):
        h = _rms_fwd(x_ref[...], g_ref[...])[0].astype(BF16)
        h_ref[...] = h
        abcv_ref[...] = _dot_nt(h, wt_ref[0:qkv0, :]).astype(BF16)
        gates_ref[...] = _dot_nt(h, wt_ref[qkv0 + 3 * ATTN_W:n, :]).astype(BF16)
        qkv = _dot_nt(h, wt_ref[qkv0:qkv0 + 3 * ATTN_W, :]).astype(BF16)
        for gi, s_ref in enumerate(s_refs):
            cols = [qkv[:, j * ATTN_W + gi * GROUP_W:j * ATTN_W + (gi + 1) * GROUP_W] for j in range(3)]
            _store_streams(s_ref, DILATIONS[gi], tm, jnp.concatenate(cols, axis=1))

    return _call(
        body, name="in_proj", grid=(t // tm,),
        in_specs=[_rows(tm, d), _resident((1, d)), _resident((n, d))],
        out_specs=[_rows(tm, d), _rows(tm, qkv0), _rows(tm, 2 * d)]
        + [_stream_spec(dil, tm, 3 * GROUP_W) for dil in DILATIONS],
        out_shape=[jax.ShapeDtypeStruct((t, d), BF16), jax.ShapeDtypeStruct((t, qkv0), BF16),
                   jax.ShapeDtypeStruct((t, 2 * d), BF16)]
        + [jax.ShapeDtypeStruct((dil, t // dil, 3 * GROUP_W), BF16) for dil in DILATIONS],
        args=(x, g, wt), semantics=("parallel",), carry=carry)


def _head_masks():
    lane = lax.broadcasted_iota(jnp.int32, (1, GROUP_W), 1)
    return lane, [(lane // HEAD_DIM) == h for h in range(HEADS_PER_GROUP)]


def _stack_heads(v, heads):
    return jnp.concatenate([jnp.where(hm, v, jnp.zeros_like(v)) for hm in heads], axis=0)


def _merge_heads(v, heads):
    out = jnp.zeros((QBLK, GROUP_W), v.dtype)
    for h, hm in enumerate(heads):
        out = jnp.where(hm, v[h * QBLK:(h + 1) * QBLK], out)
    return out


def _pair_block(col, count=STEP_BLOCKS):
    return pl.BlockSpec((count * QBLK, GROUP_W), lambda b: (b, col))


def _edge_block(col, shift, nb, count=STEP_BLOCKS):
    return pl.BlockSpec((QBLK, GROUP_W), lambda b: (jnp.clip(count * b + shift, 0, nb - 1), col))


def _band_mask(has_prev):
    rows = HEADS_PER_GROUP * QBLK
    row = lax.broadcasted_iota(jnp.int32, (rows, 2 * QBLK), 0) & (QBLK - 1)
    col = lax.broadcasted_iota(jnp.int32, (rows, 2 * QBLK), 1)
    return ((col < QBLK) & (col >= row) & has_prev) | ((col >= QBLK) & (col - QBLK <= row))


def _next_mask(has_next):
    rows = HEADS_PER_GROUP * QBLK
    row = lax.broadcasted_iota(jnp.int32, (rows, QBLK), 0) & (QBLK - 1)
    col = lax.broadcasted_iota(jnp.int32, (rows, QBLK), 1)
    return (col >= row) & has_next


def _attn_fwd(s, dil, carry=None):
    t = s.shape[0] * s.shape[1]
    nb = t // QBLK
    per_stream = nb // dil
    assert per_stream % STEP_BLOCKS == 0

    def body(q_ref, kc_ref, kp_ref, vc_ref, vp_ref, o_ref, lse_ref):
        b = pl.program_id(0)
        _, heads = _head_masks()
        first_has_prev = lax.rem(STEP_BLOCKS * b, per_stream) != 0
        for j in range(STEP_BLOCKS):
            rows = slice(j * QBLK, (j + 1) * QBLK)
            if j == 0:
                k2 = jnp.concatenate([kp_ref[...], kc_ref[rows, :]], axis=0)
                v2 = jnp.concatenate([vp_ref[...], vc_ref[rows, :]], axis=0)
            else:
                both = slice((j - 1) * QBLK, (j + 1) * QBLK)
                k2, v2 = kc_ref[both, :], vc_ref[both, :]
            mask = _band_mask(first_has_prev if j == 0 else True)
            sc = jnp.where(mask, _dot_nt(_stack_heads(q_ref[rows, :], heads), k2) * ATTN_SCALE, NEG_INF)
            mx = jnp.max(sc, axis=1, keepdims=True)
            pr = jnp.exp(sc - mx)
            den = jnp.sum(pr, axis=1, keepdims=True)
            o_all = _dot(pr.astype(BF16), v2) / den
            o_ref[rows, :] = _merge_heads(o_all, heads).astype(BF16)
            lse_ref[rows, :] = _merge_heads(jnp.broadcast_to(mx + jnp.log(den), o_all.shape), heads)

    sv = s.reshape(t, 3 * GROUP_W)
    return _call(
        body, name=f"attn_fwd_d{dil}", grid=(nb // STEP_BLOCKS,),
        in_specs=[_pair_block(0), _pair_block(1), _edge_block(1, -1, nb), _pair_block(2), _edge_block(2, -1, nb)],
        out_specs=[_pair_block(0), _pair_block(0)],
        out_shape=[jax.ShapeDtypeStruct((t, GROUP_W), BF16), jax.ShapeDtypeStruct((t, GROUP_W), F32)],
        args=(sv, sv, sv, sv, sv), semantics=("parallel",), carry=carry)


def _group_softmax(parts):
    mx = jnp.maximum(jnp.maximum(parts[0], parts[1]), parts[2])
    es = [jnp.exp(p - mx) for p in parts]
    den = es[0] + es[1] + es[2]
    return [e / den for e in es]


def _mixer_out(x, abcv, gates, os, lses, conv_w, conv_b, b_gate, w_pa, w_pb, w_o, carry=None):
    t, d = x.shape
    cw = conv_w.shape[1]
    tm = min(ROWS_MATMUL, t)

    def body(x_ref, abcv_ref, halo_ref, gates_ref, o0_ref, o1_ref, o2_ref, l0_ref, l1_ref, l2_ref, cw_ref, cb_ref,
             bg_ref, wpa_ref, wpb_ref, wo_ref, x1_ref, ya_ref, yb_ref, yap_ref, ybp_ref, mg_ref):
        m = pl.program_id(0)
        ab = abcv_ref[:, 0:cw].astype(F32)
        u = abcv_ref[:, cw:2 * cw].astype(F32) * abcv_ref[:, 2 * cw:3 * cw].astype(F32)
        hu = halo_ref[:, cw:2 * cw].astype(F32) * halo_ref[:, 2 * cw:3 * cw].astype(F32)
        hu = jnp.where(m > 0, hu, 0.0)
        cv = (cw_ref[0:1, :] * _shift_down(u, hu, 2) + cw_ref[1:2, :] * _shift_down(u, hu, 1)
              + cw_ref[2:3, :] * u + cb_ref[...])
        ya = (ab * cv).astype(BF16)
        ya_ref[...] = ya
        alphas = _group_softmax([_load_streams(r, dil, tm) for r, dil in zip((l0_ref, l1_ref, l2_ref), DILATIONS)])
        for i, (o_ref, dil) in enumerate(zip((o0_ref, o1_ref, o2_ref), DILATIONS)):
            sl = slice(i * GROUP_W, (i + 1) * GROUP_W)
            yb_ref[:, sl] = (alphas[i] * _load_streams(o_ref, dil, tm).astype(F32)).astype(BF16)
        yap = _dot_nt(ya, wpa_ref[...])
        ybp = _dot_nt(yb_ref[...], wpb_ref[...])
        yap_ref[...] = yap.astype(BF16)
        ybp_ref[...] = ybp.astype(BF16)
        sa = _sigmoid(gates_ref[:, 0:d].astype(F32) + bg_ref[0:1, :])
        sb = _sigmoid(gates_ref[:, d:2 * d].astype(F32) + bg_ref[1:2, :])
        merged = (sa * yap + sb * ybp).astype(BF16)
        mg_ref[...] = merged
        x1_ref[...] = x_ref[...] + _dot(merged, wo_ref[...])

    return _call(
        body, name="mixer_out", grid=(t // tm,),
        in_specs=[_rows(tm, d), _rows(tm, 3 * cw), _prev_halo(tm, 3 * cw), _rows(tm, 2 * d)]
        + [_stream_spec(dil, tm, GROUP_W) for dil in DILATIONS] * 2
        + [_resident((3, cw)), _resident((1, cw)), _resident((2, d)),
           _resident((d, cw)), _resident((d, ATTN_W)), _resident((d, d))],
        out_specs=[_rows(tm, d), _rows(tm, cw), _rows(tm, ATTN_W), _rows(tm, d), _rows(tm, d), _rows(tm, d)],
        out_shape=[jax.ShapeDtypeStruct((t, d), F32), jax.ShapeDtypeStruct((t, cw), BF16),
                   jax.ShapeDtypeStruct((t, ATTN_W), BF16), jax.ShapeDtypeStruct((t, d), BF16),
                   jax.ShapeDtypeStruct((t, d), BF16), jax.ShapeDtypeStruct((t, d), BF16)],
        args=(x, abcv, abcv, gates, *[_stream_view(a, dil) for a, dil in zip(os, DILATIONS)],
              *[_stream_view(a, dil) for a, dil in zip(lses, DILATIONS)], conv_w, conv_b, b_gate, w_pa, w_pb, w_o),
        semantics=("parallel",), carry=carry)


def _ffn_fwd(x1, target, g2, w_ut, conv_w, conv_b, w_d, g_f, carry=None):
    t, d = x1.shape
    dff = w_d.shape[0]
    tm = min(256, t)
    ck = _pick_tile(dff, 2816)

    def body(x1_ref, tg_ref, g2_ref, wut_ref, cw_ref, cb_ref, wd_ref, gf_ref, h2_ref, up_ref, act_ref, conv_ref,
             dx2_ref, dx2i_ref, acc_ref, loss_ref, halo_ref):
        m = pl.program_id(0)

        @pl.when(m == 0)
        def _():
            acc_ref[...] = jnp.zeros_like(acc_ref)
            loss_ref[...] = jnp.zeros_like(loss_ref)
            halo_ref[...] = jnp.zeros_like(halo_ref)

        h2 = _permute_rows(_interleave(tm), _rms_fwd(x1_ref[...], g2_ref[...])[0].astype(BF16))
        h2_ref[...] = h2

        def conv(c0):
            p = _dot_nt(h2, wut_ref[c0:c0 + ck, :])
            up_ref[:, c0:c0 + ck] = p.astype(BF16)
            hp = halo_ref[:, c0:c0 + ck]
            halo_ref[:, c0:c0 + ck] = p[tm - HALO:, :]
            return (cw_ref[0:1, c0:c0 + ck] * _shift_down_il(p, hp, 2)
                    + cw_ref[1:2, c0:c0 + ck] * _shift_down_il(p, hp, 1)
                    + cw_ref[2:3, c0:c0 + ck] * p + cb_ref[:, c0:c0 + ck])

        down = jnp.zeros((tm, d), F32)
        for c0 in range(0, dff, ck):
            gate = conv(c0)
            val = conv(dff + c0)
            conv_ref[:, c0:c0 + ck] = gate.astype(BF16)
            conv_ref[:, dff + c0:dff + c0 + ck] = val.astype(BF16)
            act = (gate * _sigmoid(gate) * val).astype(BF16)
            act_ref[:, c0:c0 + ck] = act
            down = down + _dot(act, wd_ref[c0:c0 + ck, :])
        x2 = x1_ref[...] + _permute_rows(_interleave(tm, inverse=True), down)
        y, _ = _rms_fwd(x2, gf_ref[...])
        diff = y - tg_ref[...]
        loss_ref[...] += 0.5 * jnp.sum(jnp.mean(diff * diff, axis=-1, keepdims=True))
        dx2, dg = _rms_bwd(x2, gf_ref[...], diff * (1.0 / d))
        dx2_ref[...] = dx2
        dx2i_ref[...] = _permute_rows(_interleave(tm), dx2.astype(BF16))
        acc_ref[...] += _stack_rows([_colsum(dg)], d)

    return _call(
        body, name="ffn_fwd", grid=(t // tm,),
        in_specs=[_rows(tm, d), _rows(tm, d), _resident((1, d)), _resident((2 * dff, d)), _resident((3, 2 * dff)),
                  _resident((1, 2 * dff)), _resident((dff, d)), _resident((1, d))],
        out_specs=[_rows(tm, d), _rows(tm, 2 * dff), _rows(tm, dff), _rows(tm, 2 * dff), _rows(tm, d), _rows(tm, d),
                   _acc_spec(d), _acc_spec(LANES)],
        out_shape=[jax.ShapeDtypeStruct((t, d), BF16), jax.ShapeDtypeStruct((t, 2 * dff), BF16),
                   jax.ShapeDtypeStruct((t, dff), BF16), jax.ShapeDtypeStruct((t, 2 * dff), BF16),
                   jax.ShapeDtypeStruct((t, d), F32), jax.ShapeDtypeStruct((t, d), BF16),
                   jax.ShapeDtypeStruct((SUBLANES, d), F32), jax.ShapeDtypeStruct((SUBLANES, LANES), F32)],
        args=(x1, target, g2, w_ut, conv_w, conv_b, w_d, g_f), semantics=("arbitrary",), carry=carry,
        scratch=[pltpu.VMEM((HALO, 2 * dff), F32)])


def _ffn_act_bwd(dx2, conv, w_d):
    t, d = dx2.shape
    dff = w_d.shape[0]
    tm = min(256, t)
    ck = _pick_tile(dff, 2816)

    def body(dx2_ref, conv_ref, wd_ref, dup_ref, acc_ref):
        m = pl.program_id(0)

        @pl.when(m == 0)
        def _():
            acc_ref[...] = jnp.zeros_like(acc_ref)

        dx2v = dx2_ref[...]
        for c0 in range(0, dff, ck):
            dact = _dot_nt(dx2v, wd_ref[c0:c0 + ck, :])
            gate = conv_ref[:, c0:c0 + ck].astype(F32)
            val = conv_ref[:, dff + c0:dff + c0 + ck].astype(F32)
            sg = _sigmoid(gate)
            dval = dact * gate * sg
            dgate = dact * val * sg * (1.0 + gate * (1.0 - sg))
            dup_ref[:, c0:c0 + ck] = dgate.astype(BF16)
            dup_ref[:, dff + c0:dff + c0 + ck] = dval.astype(BF16)
            acc_ref[:, c0:c0 + ck] += _stack_rows([_colsum(dgate)], ck)
            acc_ref[:, dff + c0:dff + c0 + ck] += _stack_rows([_colsum(dval)], ck)

    return pl.pallas_call(
        body, name="ffn_act_bwd", grid=(t // tm,),
        in_specs=[_rows(tm, d), _rows(tm, 2 * dff), _resident((dff, d))],
        out_specs=[_rows(tm, 2 * dff), _acc_spec(2 * dff)],
        out_shape=[jax.ShapeDtypeStruct((t, 2 * dff), BF16), jax.ShapeDtypeStruct((SUBLANES, 2 * dff), F32)],
        compiler_params=_params("arbitrary"),
    )(dx2, conv, w_d)


def _ffn_up_bwd(dup, up_pre, x1, dx2, conv_w, w_u, g2, carry=None):
    t, d = x1.shape
    n = dup.shape[1]
    tm = min(256, t)
    ck = _pick_tile(n, 256)
    last = t // tm - 1

    def body(dup_ref, nxt_ref, up_ref, x1_ref, dx2_ref, cw_ref, wu_ref, g2_ref, dpre_ref, dx1_ref, acc_ref, accw_ref):
        m = pl.program_id(0)

        @pl.when(m == 0)
        def _():
            acc_ref[...] = jnp.zeros_like(acc_ref)
            accw_ref[...] = jnp.zeros_like(accw_ref)

        dh = jnp.zeros((tm, d), F32)
        for c0 in range(0, n, ck):
            du = dup_ref[:, c0:c0 + ck].astype(F32)
            hn = jnp.where(m < last, nxt_ref[:, c0:c0 + ck].astype(F32), 0.0)
            du1 = _shift_up_il(du, hn, 1)
            du2 = _shift_up_il(du, hn, 2)
            dpre = (cw_ref[2:3, c0:c0 + ck] * du + cw_ref[1:2, c0:c0 + ck] * du1
                    + cw_ref[0:1, c0:c0 + ck] * du2).astype(BF16)
            dpre_ref[:, c0:c0 + ck] = dpre
            dh = dh + _dot(dpre, wu_ref[c0:c0 + ck, :])
            p = up_ref[:, c0:c0 + ck].astype(F32)
            accw_ref[:, c0:c0 + ck] += _stack_rows([_colsum(du2 * p), _colsum(du1 * p), _colsum(du * p)], ck)
        dh = _permute_rows(_interleave(tm, inverse=True), dh)
        dx, dg = _rms_bwd(x1_ref[...], g2_ref[...], dh)
        dx1_ref[...] = dx2_ref[...] + dx
        acc_ref[...] += _stack_rows([_colsum(dg)], d)

    return _call(
        body, name="ffn_up_bwd", grid=(t // tm,),
        in_specs=[_rows(tm, n), _next_halo(tm, n, t), _rows(tm, n), _rows(tm, d), _rows(tm, d), _resident((3, n)),
                  _resident((n, d)), _resident((1, d))],
        out_specs=[_rows(tm, n), _rows(tm, d), _acc_spec(d), _acc_spec(n)],
        out_shape=[jax.ShapeDtypeStruct((t, n), BF16), jax.ShapeDtypeStruct((t, d), F32),
                   jax.ShapeDtypeStruct((SUBLANES, d), F32), jax.ShapeDtypeStruct((SUBLANES, n), F32)],
        args=(dup, dup, up_pre, x1, dx2, conv_w, w_u, g2), semantics=("arbitrary",), carry=carry)


def _tn_matmul(a, b, name):
    t, mdim = a.shape
    n = b.shape[1]
    tk = min(2048, t)
    tmm = _pick_tile(mdim, 1536)
    tn = _pick_tile(n, 1024)

    def body(a_ref, b_ref, o_ref, acc_ref):
        k = pl.program_id(2)

        @pl.when(k == 0)
        def _():
            acc_ref[...] = jnp.zeros_like(acc_ref)

        acc_ref[...] += _dot_tn(a_ref[...].astype(BF16), b_ref[...].astype(BF16))

        @pl.when(k == t // tk - 1)
        def _():
            o_ref[...] = acc_ref[...].astype(BF16)

    return pl.pallas_call(
        body, name=name, grid=(mdim // tmm, n // tn, t // tk),
        in_specs=[pl.BlockSpec((tk, tmm), lambda i, j, k: (k, i)), pl.BlockSpec((tk, tn), lambda i, j, k: (k, j))],
        out_specs=pl.BlockSpec((tmm, tn), lambda i, j, k: (i, j)),
        out_shape=jax.ShapeDtypeStruct((mdim, n), BF16),
        scratch_shapes=[pltpu.VMEM((tmm, tn), F32)],
        compiler_params=_params("parallel", "parallel", "arbitrary"),
    )(a, b)


def _mixer_bwd(dx1, gates, yap, ybp, os, lses, b_gate, w_o, w_pa, w_pb):
    t, d = dx1.shape
    cw = w_pa.shape[1]
    tm = min(ROWS_MATMUL, t)

    def body(dx1_ref, gates_ref, yap_ref, ybp_ref, o0_ref, o1_ref, o2_ref, l0_ref, l1_ref, l2_ref, bg_ref, wo_ref,
             wpa_ref, wpb_ref, dgates_ref, dyap_ref, dybp_ref, dya_ref, do0_ref, do1_ref, do2_ref, dl0_ref, dl1_ref,
             dl2_ref, acc_ref):
        m = pl.program_id(0)

        @pl.when(m == 0)
        def _():
            acc_ref[...] = jnp.zeros_like(acc_ref)

        dmg = _dot_nt(dx1_ref[...].astype(BF16), wo_ref[...])
        sa = _sigmoid(gates_ref[:, 0:d].astype(F32) + bg_ref[0:1, :])
        sb = _sigmoid(gates_ref[:, d:2 * d].astype(F32) + bg_ref[1:2, :])
        dyap = (dmg * sa).astype(BF16)
        dybp = (dmg * sb).astype(BF16)
        dga = dmg * yap_ref[...].astype(F32) * sa * (1.0 - sa)
        dgb = dmg * ybp_ref[...].astype(F32) * sb * (1.0 - sb)
        dyap_ref[...] = dyap
        dybp_ref[...] = dybp
        dgates_ref[:, 0:d] = dga.astype(BF16)
        dgates_ref[:, d:2 * d] = dgb.astype(BF16)
        acc_ref[...] += _stack_rows([_colsum(dga), _colsum(dgb)], d)
        dya_ref[...] = _dot(dyap, wpa_ref[...]).astype(BF16)
        dyb = _dot(dybp, wpb_ref[...])

        ri = lax.broadcasted_iota(jnp.int32, (GROUP_W, GROUP_W), 0) // HEAD_DIM
        ci = lax.broadcasted_iota(jnp.int32, (GROUP_W, GROUP_W), 1) // HEAD_DIM
        same_head = (ri == ci).astype(BF16)
        alphas = _group_softmax([_load_streams(r, dil, tm) for r, dil in zip((l0_ref, l1_ref, l2_ref), DILATIONS)])
        prod = jnp.zeros((tm, GROUP_W), F32)
        for i, (o_ref, do_ref, dil) in enumerate(zip((o0_ref, o1_ref, o2_ref), (do0_ref, do1_ref, do2_ref), DILATIONS)):
            dov = alphas[i] * dyb[:, i * GROUP_W:(i + 1) * GROUP_W]
            _store_streams(do_ref, dil, tm, dov.astype(BF16))
            prod = prod + dov * _load_streams(o_ref, dil, tm).astype(F32)
        hi = prod.astype(BF16)
        lo = (prod - hi.astype(F32)).astype(BF16)
        dtot = _dot(hi, same_head) + _dot(lo, same_head)
        for alpha, dl_ref, dil in zip(alphas, (dl0_ref, dl1_ref, dl2_ref), DILATIONS):
            _store_streams(dl_ref, dil, tm, alpha * dtot)

    streams = [_stream_spec(dil, tm, GROUP_W) for dil in DILATIONS]
    res = _call(
        body, name="mixer_bwd", grid=(t // tm,),
        in_specs=[_rows(tm, d), _rows(tm, 2 * d), _rows(tm, d), _rows(tm, d)] + streams * 2
        + [_resident((2, d)), _resident((d, d)), _resident((d, cw)), _resident((d, ATTN_W))],
        out_specs=[_rows(tm, 2 * d), _rows(tm, d), _rows(tm, d), _rows(tm, cw)] + streams * 2 + [_acc_spec(d)],
        out_shape=[jax.ShapeDtypeStruct((t, 2 * d), BF16), jax.ShapeDtypeStruct((t, d), BF16),
                   jax.ShapeDtypeStruct((t, d), BF16), jax.ShapeDtypeStruct((t, cw), BF16)]
        + [jax.ShapeDtypeStruct((dil, t // dil, GROUP_W), BF16) for dil in DILATIONS]
        + [jax.ShapeDtypeStruct((dil, t // dil, GROUP_W), F32) for dil in DILATIONS]
        + [jax.ShapeDtypeStruct((SUBLANES, d), F32)],
        args=(dx1, gates, yap, ybp, *[_stream_view(a, dil) for a, dil in zip(os, DILATIONS)],
              *[_stream_view(a, dil) for a, dil in zip(lses, DILATIONS)], b_gate, w_o, w_pa, w_pb),
        semantics=("arbitrary",))
    dgates, dyap, dybp, dya = res[:4]
    dos = [a.reshape(t, GROUP_W) for a in res[4:7]]
    dls = [a.reshape(t, GROUP_W) for a in res[7:10]]
    return dgates, dyap, dybp, dya, dos, dls, res[10]


def _attn_bwd(s, do, lse, dl, dil, carry=None):
    t = s.shape[0] * s.shape[1]
    nb = t // QBLK
    per_stream = nb // dil
    count = BWD_STEP_BLOCKS
    assert per_stream % count == 0

    def body(q_ref, qn_ref, kc_ref, kp_ref, vc_ref, vp_ref, do_ref, don_ref, lse_ref, lsen_ref, dl_ref, dln_ref,
             ds_ref):
        b = pl.program_id(0)
        lane, heads = _head_masks()
        first_has_prev = lax.rem(count * b, per_stream) != 0
        last_has_next = lax.rem(count * (b + 1), per_stream) != 0

        def cols(v):
            return jnp.concatenate([jnp.sum(jnp.where(lane == h * HEAD_DIM, v, 0.0), axis=1, keepdims=True)
                                    for h in range(HEADS_PER_GROUP)], axis=0)

        def pair(qs, dos, k, v, valid, lse_c, dl_c):
            s = jnp.where(valid, _dot_nt(qs, k) * ATTN_SCALE, NEG_INF)
            p = jnp.exp(s - lse_c)
            ds = p * (_dot_nt(dos, v) - dl_c)
            return p.astype(BF16), ds.astype(BF16)

        for j in range(count):
            rows, hi = slice(j * QBLK, (j + 1) * QBLK), slice((j + 1) * QBLK, (j + 2) * QBLK)
            q, do, lse, dl = q_ref[rows, :], do_ref[rows, :], lse_ref[rows, :], dl_ref[rows, :]
            kc, vc = kc_ref[rows, :], vc_ref[rows, :]
            if j == 0:
                k2 = jnp.concatenate([kp_ref[...], kc], axis=0)
                v2 = jnp.concatenate([vp_ref[...], vc], axis=0)
                mask = _band_mask(first_has_prev)
            else:
                both = slice((j - 1) * QBLK, (j + 1) * QBLK)
                k2, v2, mask = kc_ref[both, :], vc_ref[both, :], _band_mask(True)
            if j < count - 1:
                qn, don, lsen, dln = q_ref[hi, :], do_ref[hi, :], lse_ref[hi, :], dl_ref[hi, :]
                mask_n = _next_mask(True)
            else:
                qn, don, lsen, dln = qn_ref[...], don_ref[...], lsen_ref[...], dln_ref[...]
                mask_n = _next_mask(last_has_next)
            qs, qns = _stack_heads(q, heads), _stack_heads(qn, heads)
            dos, dons = _stack_heads(do, heads), _stack_heads(don, heads)
            p_q, ds_q = pair(qs, dos, k2, v2, mask, cols(lse), cols(dl))
            p_n, ds_n = pair(qns, dons, kc, vc, mask_n, cols(lsen), cols(dln))
            dq = _merge_heads(_dot(ds_q, k2), heads)
            dk = _dot_tn(jnp.concatenate([ds_q[:, QBLK:], ds_n], axis=0), jnp.concatenate([qs, qns], axis=0))
            dv = _dot_tn(jnp.concatenate([p_q[:, QBLK:], p_n], axis=0), jnp.concatenate([dos, dons], axis=0))
            ds_ref[rows, 0:GROUP_W] = (dq * ATTN_SCALE).astype(BF16)
            ds_ref[rows, GROUP_W:2 * GROUP_W] = (dk * ATTN_SCALE).astype(BF16)
            ds_ref[rows, 2 * GROUP_W:3 * GROUP_W] = dv.astype(BF16)

    sv = s.reshape(t, 3 * GROUP_W)
    cur, nxt = _pair_block(0, count), _edge_block(0, count, nb, count)
    return _call(
        body, name=f"attn_bwd_d{dil}", grid=(nb // count,),
        in_specs=[cur, nxt, _pair_block(1, count), _edge_block(1, -1, nb, count), _pair_block(2, count),
                  _edge_block(2, -1, nb, count), cur, nxt, cur, nxt, cur, nxt],
        out_specs=[pl.BlockSpec((count * QBLK, 3 * GROUP_W), lambda b: (b, 0))],
        out_shape=[jax.ShapeDtypeStruct((t, 3 * GROUP_W), BF16)],
        args=(sv, sv, sv, sv, sv, sv, do, do, lse, lse, dl, dl), semantics=("parallel",), carry=carry)


def _conv_mixer_bwd(abcv, dya, conv_w, conv_b):
    t = abcv.shape[0]
    cw = conv_w.shape[1]
    tm = min(1024, t)
    last = t // tm - 1

    def body(a_ref, ap_ref, an_ref, dya_ref, dyan_ref, cw_ref, cb_ref, d_ref, acc_ref):
        m = pl.program_id(0)

        @pl.when(m == 0)
        def _():
            acc_ref[...] = jnp.zeros_like(acc_ref)

        ab = a_ref[:, 0:cw].astype(F32)
        ac = a_ref[:, cw:2 * cw].astype(F32)
        av = a_ref[:, 2 * cw:3 * cw].astype(F32)
        u = ac * av
        hu = ap_ref[:, cw:2 * cw].astype(F32) * ap_ref[:, 2 * cw:3 * cw].astype(F32)
        hu = jnp.where(m > 0, hu, 0.0)
        u1 = _shift_down(u, hu, 1)
        u2 = _shift_down(u, hu, 2)
        cv = cw_ref[0:1, :] * u2 + cw_ref[1:2, :] * u1 + cw_ref[2:3, :] * u + cb_ref[...]
        dya_v = dya_ref[...].astype(F32)
        dcv = dya_v * ab
        ndcv = jnp.where(m < last, dyan_ref[...].astype(F32) * an_ref[:, 0:cw].astype(F32), 0.0)
        du = (cw_ref[2:3, :] * dcv + cw_ref[1:2, :] * _shift_up(dcv, ndcv, 1)
              + cw_ref[0:1, :] * _shift_up(dcv, ndcv, 2))
        d_ref[:, 0:cw] = (dya_v * cv).astype(BF16)
        d_ref[:, cw:2 * cw] = (du * av).astype(BF16)
        d_ref[:, 2 * cw:3 * cw] = (du * ac).astype(BF16)
        acc_ref[...] += _stack_rows([_colsum(dcv * u2), _colsum(dcv * u1), _colsum(dcv * u), _colsum(dcv)], cw)

    return pl.pallas_call(
        body, name="conv_mixer_bwd", grid=(t // tm,),
        in_specs=[_rows(tm, 3 * cw), _prev_halo(tm, 3 * cw), _next_halo(tm, 3 * cw, t), _rows(tm, cw),
                  _next_halo(tm, cw, t), _resident((3, cw)), _resident((1, cw))],
        out_specs=[_rows(tm, 3 * cw), _acc_spec(cw)],
        out_shape=[jax.ShapeDtypeStruct((t, 3 * cw), BF16), jax.ShapeDtypeStruct((SUBLANES, cw), F32)],
        compiler_params=_params("arbitrary"),
    )(abcv, abcv, abcv, dya, dya, conv_w, conv_b)


def _in_proj_bwd(x, dx1, dabcv, dss, dgates, w_in, g1, carry=None):
    t, d = x.shape
    qkv0 = dabcv.shape[1]
    n = w_in.shape[0]
    tm = min(ROWS_MATMUL, t)

    def body(x_ref, dx1_ref, da_ref, ds0_ref, ds1_ref, ds2_ref, dg_ref, w_ref, g_ref, dx_ref, acc_ref):
        m = pl.program_id(0)

        @pl.when(m == 0)
        def _():
            acc_ref[...] = jnp.zeros_like(acc_ref)

        dss_tok = [_load_streams(ds_ref, dil, tm) for ds_ref, dil in zip((ds0_ref, ds1_ref, ds2_ref), DILATIONS)]
        dqkv = jnp.concatenate([ds[:, j * GROUP_W:(j + 1) * GROUP_W] for j in range(3) for ds in dss_tok], axis=1)
        dh = (_dot(da_ref[...], w_ref[0:qkv0, :]) + _dot(dqkv, w_ref[qkv0:qkv0 + 3 * ATTN_W, :])
              + _dot(dg_ref[...], w_ref[qkv0 + 3 * ATTN_W:n, :]))
        dx, dg = _rms_bwd(x_ref[...], g_ref[...], dh)
        dx_ref[...] = dx1_ref[...] + dx
        acc_ref[...] += _stack_rows([_colsum(dg)], d)

    return _call(
        body, name="in_proj_bwd", grid=(t // tm,),
        in_specs=[_rows(tm, d), _rows(tm, d), _rows(tm, qkv0)]
        + [_stream_spec(dil, tm, 3 * GROUP_W) for dil in DILATIONS]
        + [_rows(tm, 2 * d), _resident((n, d)), _resident((1, d))],
        out_specs=[_rows(tm, d), _acc_spec(d)],
        out_shape=[jax.ShapeDtypeStruct((t, d), F32), jax.ShapeDtypeStruct((SUBLANES, d), F32)],
        args=(x, dx1, dabcv, *[_stream_view(a, dil) for a, dil in zip(dss, DILATIONS)], dgates, w_in, g1),
        semantics=("arbitrary",), carry=carry)


def _dw_in_qkv(ds, h, dil):
    t, d = h.shape
    tk = min(1024, t)
    sub = min(256, t)
    width = 3 * GROUP_W

    def body(ds_ref, h_ref, o_ref, acc_ref):
        k = pl.program_id(0)

        @pl.when(k == 0)
        def _():
            acc_ref[...] = jnp.zeros_like(acc_ref)

        upd = None
        for i in range(tk // sub):
            rows = ds_ref[:, i * (sub // dil):(i + 1) * (sub // dil), :].reshape(sub, width)
            if dil > 1:
                rows = _permute_rows(_perm(dil, sub, inverse=True), rows)
            term = _dot_tn(rows, h_ref[i * sub:(i + 1) * sub, :])
            upd = term if upd is None else upd + term
        acc_ref[...] += upd

        @pl.when(k == t // tk - 1)
        def _():
            o_ref[...] = acc_ref[...].astype(BF16)

    return pl.pallas_call(
        body, name=f"dw_in_qkv_d{dil}", grid=(t // tk,),
        in_specs=[_stream_spec(dil, tk, width), _rows(tk, d)],
        out_specs=pl.BlockSpec((width, d), lambda k: (0, 0)),
        out_shape=jax.ShapeDtypeStruct((width, d), BF16),
        scratch_shapes=[pltpu.VMEM((width, d), F32)],
        compiler_params=_params("arbitrary"),
    )(_stream_view(ds, dil), h)


def _local_step(x, target, p, late):
    cw = p["conv_a_w"].shape[1]
    (h, abcv, gates, *ss), (g_up,) = _in_proj(x, p["norm_mix_g"], p["w_in"], cw,
                                              carry=_Exchange("gather", [late["w_up"]]))
    w_up = _full_from_gathered(g_up)
    (o0, lse0), (g_pa,) = _attn_fwd(ss[0], DILATIONS[0], carry=_Exchange("gather", [late["w_proj_a"]]))
    (o1, lse1), (g_pb,) = _attn_fwd(ss[1], DILATIONS[1], carry=_Exchange("gather", [late["w_proj_b"]]))
    (o2, lse2), (g_out,) = _attn_fwd(ss[2], DILATIONS[2], carry=_Exchange("gather", [late["w_out"]]))
    w_pa, w_pb, w_out = [_full_from_gathered(g) for g in (g_pa, g_pb, g_out)]
    os, lses = (o0, o1, o2), (lse0, lse1, lse2)
    (x1, ya, yb, yap, ybp, merged), (g_down,) = _mixer_out(
        x, abcv, gates, os, lses, p["conv_a_w"], p["conv_a_b"], p["b_gate"], w_pa, w_pb, w_out,
        carry=_Exchange("gather", [late["w_down"]]))
    w_down = _full_from_gathered(g_down)
    h2, up_pre, act, conv, dx2, dx2i, acc_gf, loss = _ffn_fwd(x1, target, p["norm_ffn_g"], w_up, p["ffn_conv_w"],
                                                              p["ffn_conv_b"], w_down, p["final_norm_g"])

    parts, got = {}, {}
    dup, acc_fb = _ffn_act_bwd(dx2i, conv, w_down)
    parts["w_down"] = _by_destination(_tn_matmul(act, dx2i, "dw_down"))
    (dpre, dx1, acc_g2, acc_fw), (got["w_down"],) = _ffn_up_bwd(dup, up_pre, x1, dx2, p["ffn_conv_w"], w_up,
                                                                p["norm_ffn_g"],
                                                                carry=_Exchange("scatter", [parts["w_down"]]))
    parts["w_up"] = _by_destination(_tn_matmul(dpre, h2, "dw_up"))
    dgates, dyap, dybp, dya, dos, dls, acc_bg = _mixer_bwd(dx1, gates, yap, ybp, os, lses, p["b_gate"], w_out,
                                                           w_pa, w_pb)
    parts["w_out"] = _by_destination(_tn_matmul(merged, dx1, "dw_out"))
    parts["w_proj_a"] = _by_destination(_tn_matmul(dyap, ya, "dw_proj_a"))
    parts["w_proj_b"] = _by_destination(_tn_matmul(dybp, yb, "dw_proj_b"))
    minor = ("w_out", "w_proj_a", "w_proj_b")
    half = parts["w_up"].shape[1] // 2
    (ds0,), received = _attn_bwd(ss[0], dos[0], lses[0], dls[0], DILATIONS[0],
                                 carry=_Exchange("scatter", [parts[n] for n in minor]))
    got.update(zip(minor, received))
    (ds1,), first_half = _attn_bwd(ss[1], dos[1], lses[1], dls[1], DILATIONS[1],
                                   carry=_Exchange("scatter", [parts["w_up"]], rows=(0, half)))
    (ds2,), (got["w_up"],) = _attn_bwd(ss[2], dos[2], lses[2], dls[2], DILATIONS[2],
                                       carry=_Exchange("scatter", [parts["w_up"]], rows=(half, half),
                                                       into=first_half))
    dss = [ds0, ds1, ds2]
    dabcv, acc_ca = _conv_mixer_bwd(abcv, dya, p["conv_a_w"], p["conv_a_b"])
    dw_s = [_dw_in_qkv(ds, h, dil) for ds, dil in zip(dss, DILATIONS)]
    dw_qkv = [w[j * GROUP_W:(j + 1) * GROUP_W] for j in range(3) for w in dw_s]
    g_w_in = jnp.concatenate([_tn_matmul(dabcv, h, "dw_in_a"), *dw_qkv, _tn_matmul(dgates, h, "dw_in_g")], axis=0)
    parts["w_in"] = _by_destination(g_w_in)
    (dx, acc_g1), (got["w_in"],) = _in_proj_bwd(x, dx1, dabcv, dss, dgates, p["w_in"], p["norm_mix_g"],
                                                carry=_Exchange("scatter", [parts["w_in"]]))
    small = dict(norm_mix_g=acc_g1[0:1], b_gate=acc_bg[0:2], conv_a_w=acc_ca[0:3], conv_a_b=acc_ca[3:4],
                 norm_ffn_g=acc_g2[0:1], ffn_conv_w=acc_fw[0:3], ffn_conv_b=acc_fb[0:1], final_norm_g=acc_gf[0:1])
    return loss[0, 0], dx, parts, got, small


def _all_gather(shards):
    n = len(shards)

    def body(*refs):
        ins, outs = refs[:n], refs[n:2 * n]
        send_sems, recv_sems, local_sems = refs[2 * n:]
        x, y, c = _mesh_pos()
        me, sibling = (x, y, c), (x, y, 1 - c)
        chips = [(1 - x, y), (x, 1 - y), (1 - x, 1 - y)]

        def copy(i, k, block, to, src=None):
            rows = outs[i].at[_dev_index(*block)]
            return pltpu.make_async_remote_copy(
                src_ref=rows if src is None else src, dst_ref=rows, send_sem=send_sems.at[i, k],
                recv_sem=recv_sems.at[i, k], device_id=to, device_id_type=MESH)

        mine, first, passed = [], [], []
        for i in range(n):
            cp = pltpu.make_async_copy(ins[i], outs[i].at[_dev_index(*me)], local_sems.at[i])
            cp.start()
            mine.append(cp)
            first.append(copy(i, 0, me, sibling, src=ins[i]))
            first += [copy(i, 1 + j, me, (*chip, c), src=ins[i]) for j, chip in enumerate(chips)]
        for cp in first:
            cp.start()
        for i in range(n):
            for j, chip in enumerate(chips):
                copy(i, 1 + j, (*chip, c), me).wait_recv()
                fw = copy(i, 4 + j, (*chip, c), sibling)
                fw.start()
                passed.append(fw)
        for i in range(n):
            copy(i, 0, sibling, me).wait_recv()
            for j, chip in enumerate(chips):
                copy(i, 4 + j, (*chip, 1 - c), me).wait_recv()
        for cp in first + passed:
            cp.wait_send()
        for cp in mine:
            cp.wait()

    return pl.pallas_call(
        body, name="all_gather_weights",
        in_specs=[ANY] * n, out_specs=[ANY] * n,
        out_shape=[jax.ShapeDtypeStruct((N_DEV,) + s.shape, s.dtype) for s in shards],
        scratch_shapes=[pltpu.SemaphoreType.DMA((n, 7)), pltpu.SemaphoreType.DMA((n, 7)),
                        pltpu.SemaphoreType.DMA((n,))],
    )(*shards)


def _all_reduce_small(v):
    r = v.shape[0]

    def body(v_ref, o_ref, gath, send_sems, recv_sems):
        x, y, c = _mesh_pos()
        me = _dev_index(x, y, c)
        gath[me] = v_ref[...]
        flips = [(kx, ky, kc) for kx in (0, 1) for ky in (0, 1) for kc in (0, 1)][1:]
        copies = []
        for k, (kx, ky, kc) in enumerate(flips):
            px = 1 - x if kx else x
            py = 1 - y if ky else y
            pc = 1 - c if kc else c
            cp = pltpu.make_async_remote_copy(
                src_ref=v_ref, dst_ref=gath.at[me], send_sem=send_sems.at[k], recv_sem=recv_sems.at[k],
                device_id=(px, py, pc), device_id_type=MESH)
            cp.start()
            copies.append((cp, _dev_index(px, py, pc)))
        for k, (cp, peer) in enumerate(copies):
            pltpu.make_async_remote_copy(
                src_ref=v_ref, dst_ref=gath.at[peer], send_sem=send_sems.at[k], recv_sem=recv_sems.at[k],
                device_id=(x, y, c), device_id_type=MESH).wait_recv()
        for cp, _ in copies:
            cp.wait_send()
        total = gath[0]
        for j in range(1, N_DEV):
            total = total + gath[j]
        o_ref[...] = total

    return pl.pallas_call(
        body, name="all_reduce_small",
        in_specs=[pl.BlockSpec(memory_space=pltpu.VMEM)], out_specs=pl.BlockSpec(memory_space=pltpu.VMEM),
        out_shape=jax.ShapeDtypeStruct((r, LANES), F32),
        scratch_shapes=[pltpu.VMEM((N_DEV, r, LANES), F32), pltpu.SemaphoreType.DMA((7,)),
                        pltpu.SemaphoreType.DMA((7,))],
    )(v)


def _adamw_math(w, g, m, v):
    m2 = ADAM_B1 * m + (1.0 - ADAM_B1) * g
    v2 = ADAM_B2 * v + (1.0 - ADAM_B2) * (g * g)
    m_hat = m2 / (1.0 - ADAM_B1 ** ADAM_STEP)
    v_hat = v2 / (1.0 - ADAM_B2 ** ADAM_STEP)
    delta = -ADAM_LR * (m_hat / (jnp.sqrt(v_hat) + ADAM_EPS) + ADAM_WD * w)
    return delta, m2, v2


def _adamw_big(w, m, v, part, got, me):
    r, c = part.shape[1:]
    flip = w.shape != (r, c)
    tr = r if flip else max(t for t in range(HALO, min(r, 512) + 1, HALO) if r % t == 0)

    def body(me_ref, w_ref, m_ref, v_ref, own_ref, *rest):
        del me_ref
        got_refs, (g_out, d_out, m_out, v_out) = rest[:N_DEV - 1], rest[N_DEV - 1:]
        g = own_ref[...].astype(F32)
        for ref in got_refs:
            g = g + ref[...].astype(F32)
        if flip:
            g = g.T
        delta, m2, v2 = _adamw_math(w_ref[...], g, m_ref[...], v_ref[...])
        g_out[...] = g
        d_out[...] = delta
        m_out[...] = m2
        v_out[...] = v2

    def peer_block(k):
        return pl.BlockSpec((None, tr, c), lambda i, me_ref: (jnp.bitwise_xor(me_ref[0], k), i, 0))

    plain = pl.BlockSpec(w.shape if flip else (tr, c), lambda i, me_ref: (i, 0))
    out = jax.ShapeDtypeStruct(w.shape, F32)
    return pl.pallas_call(
        body, name="adamw_big",
        grid_spec=pltpu.PrefetchScalarGridSpec(
            num_scalar_prefetch=1, grid=(r // tr,),
            in_specs=[plain, plain, plain] + [peer_block(k) for k in range(N_DEV)],
            out_specs=[plain] * 4),
        out_shape=[out] * 4,
        compiler_params=_params("parallel"),
    )(me, w, m, v, part, *([got] * (N_DEV - 1)))


def _adamw_small(ws, gs, ms, vs):
    n = len(ws)

    def body(*refs):
        ins, outs = refs[:4 * n], refs[4 * n:]
        for i in range(n):
            delta, m2, v2 = _adamw_math(ins[i][...], ins[n + i][...], ins[2 * n + i][...], ins[3 * n + i][...])
            outs[i][...] = delta
            outs[n + i][...] = m2
            outs[2 * n + i][...] = v2

    out = [jax.ShapeDtypeStruct(w.shape, F32) for w in ws]
    res = pl.pallas_call(body, name="adamw_small", out_shape=out * 3)(*ws, *gs, *ms, *vs)
    return res[:n], res[n:2 * n], res[2 * n:]


BIG = ("w_in", "w_proj_a", "w_proj_b", "w_out", "w_up", "w_down")
LATE = ("w_proj_a", "w_proj_b", "w_out", "w_up", "w_down")
COLUMN_SHARDED = ("w_in", "w_proj_a", "w_proj_b", "w_up")
WIDE_COLUMN_SHARDED = ("w_in", "w_up")
SMALL = ("norm_mix_g", "b_gate", "conv_a_w", "conv_a_b", "norm_ffn_g", "ffn_conv_w", "ffn_conv_b", "final_norm_g")
SMALL_SHARDED = ("b_gate", "conv_a_w", "ffn_conv_w")
WEIGHTS = ("norm_mix_g", "w_in", "b_gate", "conv_a_w", "conv_a_b", "w_proj_a", "w_proj_b", "w_out", "norm_ffn_g",
           "w_up", "ffn_conv_w", "ffn_conv_b", "w_down", "final_norm_g")


def _pack(vectors, rows):
    flat = jnp.concatenate([v.reshape(-1) for v in vectors])
    return jnp.pad(flat, (0, rows * LANES - flat.shape[0])).reshape(rows, LANES)


def _packed_rows(count):
    rows = -(-count // LANES)
    return -(-rows // SUBLANES) * SUBLANES


def _unpack(packed, shapes):
    flat = packed.reshape(-1)
    out, lo = [], 0
    for s in shapes:
        size = 1
        for dim in s:
            size *= dim
        out.append(flat[lo:lo + size].reshape(s))
        lo += size
    return out


def _full_from_gathered(gathered):
    _, r, c = gathered.shape
    return gathered.reshape(N_DEV * r, c)


def _by_destination(grad):
    rr, cc = grad.shape
    return grad.reshape(N_DEV, rr // N_DEV, cc)


def _block2d(name, a):
    a = a.reshape(a.shape[-2:])
    return a.T if name in WIDE_COLUMN_SHARDED else a


def kernel(x, norm_mix_g, w_in, b_gate, conv_a_w, conv_a_b, w_proj_a, w_proj_b, w_out, norm_ffn_g, w_up, ffn_conv_w, ffn_conv_b, w_down, final_norm_g, loss_target, m_norm_mix_g, m_w_in, m_b_gate, m_conv_a_w, m_conv_a_b, m_w_proj_a, m_w_proj_b, m_w_out, m_norm_ffn_g, m_w_up, m_ffn_conv_w, m_ffn_conv_b, m_w_down, m_final_norm_g, v_norm_mix_g, v_w_in, v_b_gate, v_conv_a_w, v_conv_a_b, v_w_proj_a, v_w_proj_b, v_w_out, v_norm_ffn_g, v_w_up, v_ffn_conv_w, v_ffn_conv_b, v_w_down, v_final_norm_g):
    given = dict(locals())
    shard = {n: given[n] for n in WEIGHTS}
    mom_m = {n: given["m_" + n] for n in WEIGHTS}
    mom_v = {n: given["v_" + n] for n in WEIGHTS}
    xi, yi, ci = _mesh_pos()
    me = _dev_index(xi, yi, ci)
    me1 = me.astype(jnp.int32).reshape(1)

    big2d = {n: _block2d(n, shard[n]) for n in BIG}
    small_shapes = [shard[n].shape[1:] for n in SMALL_SHARDED]
    n_small = sum(s[0] * s[1] for s in small_shapes)
    packed_small = _pack([shard[n] for n in SMALL_SHARDED], _packed_rows(n_small))
    gathered = _all_gather([big2d["w_in"].astype(BF16), packed_small])
    p = {"w_in": _full_from_gathered(gathered[0])}
    flat_small = gathered[-1].reshape(N_DEV, -1)
    lo = 0
    for n, (rows, width) in zip(SMALL_SHARDED, small_shapes):
        blocks = flat_small[:, lo:lo + rows * width].reshape(N_DEV, rows, width)
        p[n] = blocks.transpose(1, 0, 2).reshape(rows, N_DEV * width)
        lo += rows * width
    p["norm_mix_g"], p["norm_ffn_g"] = shard["norm_mix_g"], shard["norm_ffn_g"]
    p["conv_a_b"], p["ffn_conv_b"] = shard["conv_a_b"], shard["ffn_conv_b"]
    p["final_norm_g"] = shard["final_norm_g"].reshape(1, -1)
    late = {n: (big2d[n].T if n in ("w_proj_a", "w_proj_b") else big2d[n]).astype(BF16) for n in LATE}

    loss_part, dx, parts, got, g_small = _local_step(x[0], loss_target[0], p, late)

    results = {}
    for n in BIG:
        outs = _adamw_big(big2d[n], _block2d(n, mom_m[n]), _block2d(n, mom_v[n]), parts[n], got[n], me1)
        results[n] = [_block2d(n, o).reshape(shard[n].shape) for o in outs]

    small_full_shapes = [g_small[n].shape for n in SMALL]
    n_vec = sum(s[0] * s[1] for s in small_full_shapes) + 1
    packed = _pack([g_small[n] for n in SMALL] + [loss_part.reshape(1)], _packed_rows(n_vec))
    reduced = _all_reduce_small(packed)
    *g_full, loss_vec = _unpack(reduced, small_full_shapes + [(1,)])
    loss = loss_vec[0]
    own_g = []
    for n, g in zip(SMALL, g_full):
        if n in SMALL_SHARDED:
            width = shard[n].shape[-1]
            g = lax.dynamic_slice_in_dim(g, me * width, width, axis=1)
        own_g.append(g.reshape(shard[n].shape))
    def rows2d(a):
        return a.reshape(-1, a.shape[-1])

    deltas, new_ms, new_vs = _adamw_small([rows2d(shard[n]) for n in SMALL], [rows2d(g) for g in own_g],
                                          [rows2d(mom_m[n]) for n in SMALL], [rows2d(mom_v[n]) for n in SMALL])
    for i, n in enumerate(SMALL):
        results[n] = [own_g[i]] + [a.reshape(shard[n].shape) for a in (deltas[i], new_ms[i], new_vs[i])]

    grad_x = dx.reshape(x.shape)
    return (loss, grad_x, *[results[n][0] for n in WEIGHTS], *[results[n][1] for n in WEIGHTS],
            *[results[n][2] for n in WEIGHTS], *[results[n][3] for n in WEIGHTS])
```

```python
import functools

import jax
import jax.numpy as jnp
from jax import lax
from jax.experimental import pallas as pl
from jax.experimental.pallas import tpu as pltpu

F32 = jnp.float32
BF16 = jnp.bfloat16
MESH = pl.DeviceIdType.MESH

N_DEV = 8
RMS_EPS = 1e-6
NEG_INF = -1e30
N_GROUPS = 3
DILATIONS = (1, 4, 16)
HEADS_PER_GROUP = 4
HEAD_DIM = 64
GROUP_W = HEADS_PER_GROUP * HEAD_DIM
ATTN_W = N_GROUPS * GROUP_W
QBLK = 128
STEP_BLOCKS = 4
BWD_STEP_BLOCKS = 2
ATTN_SCALE = HEAD_DIM ** -0.5

ADAM_LR = 0.001
ADAM_B1 = 0.9
ADAM_B2 = 0.999
ADAM_EPS = 1e-08
ADAM_WD = 0.01
ADAM_STEP = 10

PERM_TOKENS = 256
ROWS_MATMUL = 512
HALO = 16
LANES = 128
SUBLANES = 8
VMEM_LIMIT_BYTES = 56 * 1024 * 1024


def _params(*sem):
    return pltpu.CompilerParams(dimension_semantics=sem, vmem_limit_bytes=VMEM_LIMIT_BYTES)


def _pick_tile(n, cap):
    if n <= cap:
        return n
    best = None
    for t in range(LANES, cap + 1, LANES):
        if n % t == 0:
            best = t
    assert best is not None, (n, cap)
    return best


def _rows(tm, c, j=0):
    return pl.BlockSpec((tm, c), lambda m: (m, j))


def _prev_halo(tm, c):
    return pl.BlockSpec((HALO, c), lambda m: (jnp.maximum(m * (tm // HALO) - 1, 0), 0))


def _next_halo(tm, c, t_total):
    last = t_total // HALO - 1
    return pl.BlockSpec((HALO, c), lambda m: (jnp.minimum((m + 1) * (tm // HALO), last), 0))


def _resident(shape):
    nd = len(shape)
    return pl.BlockSpec(shape, lambda *_: (0,) * nd, pipeline_mode=pl.Buffered(1))


def _acc_spec(c):
    return pl.BlockSpec((SUBLANES, c), lambda *_: (0, 0))


def _shift_down(u, halo, k):
    edge = jnp.concatenate([halo[HALO - SUBLANES:], u[:SUBLANES]], axis=0)
    head = pltpu.roll(edge, k, 0)[SUBLANES:]
    return jnp.concatenate([head, pltpu.roll(u, k, 0)[SUBLANES:]], axis=0)


def _shift_up(u, halo, k):
    n = u.shape[0]
    edge = jnp.concatenate([u[n - SUBLANES:], halo[:SUBLANES]], axis=0)
    tail = pltpu.roll(edge, 2 * SUBLANES - k, 0)[:SUBLANES]
    return jnp.concatenate([pltpu.roll(u, n - k, 0)[:n - SUBLANES], tail], axis=0)


def _interleave(tm, inverse=False):
    return _perm(tm // SUBLANES, tm, inverse)


def _edge_groups(u, halo, k, from_end):
    n = u.shape[0]
    sub = lax.broadcasted_iota(jnp.int32, (SUBLANES, u.shape[1]), 0)
    out = []
    for j in range(2 - k, 2):
        lo = n - HALO + j * SUBLANES if from_end else j * SUBLANES
        own, other = u[lo:lo + SUBLANES], halo[j * SUBLANES:(j + 1) * SUBLANES]
        if from_end:
            out.append(pltpu.roll(jnp.where(sub == SUBLANES - 1, other, own), 1, 0))
        else:
            out.append(pltpu.roll(jnp.where(sub == 0, other, own), SUBLANES - 1, 0))
    return out


def _shift_down_il(u, halo, k):
    return jnp.concatenate(_edge_groups(u, halo, k, True) + [u[:u.shape[0] - k * SUBLANES]], axis=0)


def _shift_up_il(u, halo, k):
    if k == 1:
        edge = _edge_groups(u, halo, 2, False)[:1]
    else:
        edge = _edge_groups(u, halo, 2, False)
    return jnp.concatenate([u[k * SUBLANES:]] + edge, axis=0)


def _stack_rows(rows, c):
    idx = lax.broadcasted_iota(jnp.int32, (SUBLANES, c), 0)
    out = jnp.zeros((SUBLANES, c), F32)
    for i, r in enumerate(rows):
        out = out + jnp.where(idx == i, r, 0.0)
    return out


def _colsum(v):
    return jnp.sum(v, axis=0, keepdims=True)


def _sigmoid(v):
    return 0.5 * jnp.tanh(0.5 * v) + 0.5


def _rms_fwd(xv, g):
    r = lax.rsqrt(jnp.mean(xv * xv, axis=-1, keepdims=True) + RMS_EPS)
    return xv * r * g, r


def _rms_bwd(xv, g, dy):
    r = lax.rsqrt(jnp.mean(xv * xv, axis=-1, keepdims=True) + RMS_EPS)
    xn = xv * r
    dxn = dy * g
    dx = r * (dxn - xn * jnp.mean(dxn * xn, axis=-1, keepdims=True))
    return dx, dy * xn


def _dot(a, b):
    return jnp.dot(a, b, preferred_element_type=F32)


def _dot_nt(a, b):
    return lax.dot_general(a, b, (((1,), (1,)), ((), ())), preferred_element_type=F32)


def _dot_tn(a, b):
    return lax.dot_general(a, b, (((0,), (0,)), ((), ())), preferred_element_type=F32)


def _perm(dil, n, inverse=False):
    i = lax.broadcasted_iota(jnp.int32, (n, n), 0)
    j = lax.broadcasted_iota(jnp.int32, (n, n), 1)
    if inverse:
        i, j = j, i
    per = n // dil
    return (j == (i % per) * dil + i // per).astype(BF16)


def _permute_rows(pm, v):
    if v.dtype == BF16:
        return _dot(pm, v).astype(BF16)
    hi = v.astype(BF16)
    lo = (v - hi.astype(F32)).astype(BF16)
    return _dot(pm, hi) + _dot(pm, lo)


def _stream_view(a, dil):
    t, c = a.shape
    return a.reshape(dil, t // dil, c)


def _stream_spec(dil, tm, c):
    return pl.BlockSpec((dil, tm // dil, c), lambda m: (0, m, 0))


def _load_streams(ref, dil, tm):
    c = ref.shape[-1]
    if dil == 1:
        return ref[...].reshape(tm, c)
    sub = min(PERM_TOKENS, tm)
    pm = _perm(dil, sub, inverse=True)
    parts = [_permute_rows(pm, ref[:, i * (sub // dil):(i + 1) * (sub // dil), :].reshape(sub, c))
             for i in range(tm // sub)]
    return parts[0] if len(parts) == 1 else jnp.concatenate(parts, axis=0)


def _store_streams(ref, dil, tm, v):
    if dil == 1:
        ref[...] = v.reshape(ref.shape).astype(ref.dtype)
        return
    sub = min(PERM_TOKENS, tm)
    pm = _perm(dil, sub)
    for i in range(tm // sub):
        piece = _permute_rows(pm, v[i * sub:(i + 1) * sub])
        ref[:, i * (sub // dil):(i + 1) * (sub // dil), :] = piece.reshape(dil, sub // dil, -1).astype(ref.dtype)


ANY = pl.BlockSpec(memory_space=pl.ANY)


def _mesh_pos():
    return lax.axis_index("x"), lax.axis_index("y"), lax.axis_index("c")


def _dev_index(px, py, pc):
    return 4 * px + 2 * py + pc


class _Exchange:
    def __init__(self, mode, arrays, rows=None, into=()):
        self.mode, self.arrays, self.rows, self.into = mode, list(arrays), rows, list(into)
        n = len(self.arrays)
        if mode == "gather":
            self.out_shape = [jax.ShapeDtypeStruct((N_DEV,) + a.shape, a.dtype) for a in self.arrays]
        else:
            self.out_shape = [jax.ShapeDtypeStruct(a.shape, a.dtype) for a in self.arrays]
        self.scratch = [pltpu.SemaphoreType.DMA((n, N_DEV - 1)), pltpu.SemaphoreType.DMA((n, N_DEV - 1)),
                        pltpu.SemaphoreType.DMA((n,))]

    def _peers(self):
        x, y, c = _mesh_pos()
        flips = [(kx, ky, kc) for kx in (0, 1) for ky in (0, 1) for kc in (0, 1)][1:]
        peers = [(1 - x if kx else x, 1 - y if ky else y, 1 - c if kc else c) for kx, ky, kc in flips]
        return _dev_index(x, y, c), peers

    def _copy(self, ins, outs, sems, i, k, peer, me, sending):
        src = ins[i] if self.mode == "gather" else ins[i].at[_dev_index(*peer)]
        dst = outs[i].at[me if sending else _dev_index(*peer)]
        if self.rows is not None:
            src, dst = src.at[pl.ds(*self.rows)], dst.at[pl.ds(*self.rows)]
        return pltpu.make_async_remote_copy(src_ref=src, dst_ref=dst, send_sem=sems[0].at[i, k],
                                            recv_sem=sems[1].at[i, k], device_id=peer, device_id_type=MESH)

    def _own(self, ins, outs, sems, i, me):
        return pltpu.make_async_copy(ins[i], outs[i].at[me], sems[2].at[i])

    def start(self, ins, outs, sems):
        me, peers = self._peers()
        for i in range(len(ins)):
            if self.mode == "gather":
                self._own(ins, outs, sems, i, me).start()
            for k, peer in enumerate(peers):
                self._copy(ins, outs, sems, i, k, peer, me, True).start()

    def wait(self, ins, outs, sems):
        me, peers = self._peers()
        for i in range(len(ins)):
            for k, peer in enumerate(peers):
                self._copy(ins, outs, sems, i, k, peer, me, False).wait_recv()
            for k, peer in enumerate(peers):
                self._copy(ins, outs, sems, i, k, peer, me, True).wait_send()
            if self.mode == "gather":
                self._own(ins, outs, sems, i, me).wait()


def _call(body, *, name, grid, in_specs, out_specs, out_shape, args, semantics, carry=None, scratch=()):
    if carry is None:
        return pl.pallas_call(body, name=name, grid=grid, in_specs=in_specs, out_specs=out_specs,
                              out_shape=out_shape, scratch_shapes=list(scratch),
                              compiler_params=_params(*semantics))(*args)
    n_in, n_out, n_x, n_s = len(in_specs), len(out_specs), len(carry.arrays), len(scratch)
    n_into = len(carry.into)
    all_in = n_in + n_x + n_into

    def carried(*refs):
        ins, x_ins = refs[:n_in], refs[n_in:n_in + n_x]
        outs = refs[all_in:all_in + n_out]
        x_outs = refs[all_in + n_out:all_in + n_out + n_x]
        own = refs[all_in + n_out + n_x:all_in + n_out + n_x + n_s]
        sems = refs[all_in + n_out + n_x + n_s:]
        first = functools.reduce(jnp.logical_and, [pl.program_id(a) == 0 for a in range(len(grid))])
        last = functools.reduce(jnp.logical_and, [pl.program_id(a) == grid[a] - 1 for a in range(len(grid))])

        @pl.when(first)
        def _():
            carry.start(x_ins, x_outs, sems)

        body(*ins, *outs, *own)

        @pl.when(last)
        def _():
            carry.wait(x_ins, x_outs, sems)

    res = pl.pallas_call(
        carried, name=name, grid=grid, in_specs=list(in_specs) + [ANY] * (n_x + n_into),
        out_specs=list(out_specs) + [ANY] * n_x, out_shape=list(out_shape) + carry.out_shape,
        input_output_aliases={n_in + n_x + i: n_out + i for i in range(n_into)},
        scratch_shapes=list(scratch) + carry.scratch, compiler_params=_params(*["arbitrary"] * len(grid)),
    )(*args, *carry.arrays, *carry.into)
    return list(res[:n_out]), list(res[n_out:])


def _in_proj(x, g, wt, cw, carry=None):
    t, d = x.shape
    n = wt.shape[0]
    tm = min(2 * ROWS_MATMUL, t)
    qkv0 = 3 * cw

    def body(x_ref, g_ref, wt_ref, h_ref, abcv_ref, gates_ref, *s_refs):
        h = _rms_fwd(x_ref[...], g_ref[...])[0].astype(BF16)
        h_ref[...] = h
        abcv_ref[...] = _dot_nt(h, wt_ref[0:qkv0, :]).astype(BF16)
        gates_ref[...] = _dot_nt(h, wt_ref[qkv0 + 3 * ATTN_W:n, :]).astype(BF16)
        qkv = _dot_nt(h, wt_ref[qkv0:qkv0 + 3 * ATTN_W, :]).astype(BF16)
        for gi, s_ref in enumerate(s_refs):
            cols = [qkv[:, j * ATTN_W + gi * GROUP_W:j * ATTN_W + (gi + 1) * GROUP_W] for j in range(3)]
            _store_streams(s_ref, DILATIONS[gi], tm, jnp.concatenate(cols, axis=1))

    return _call(
        body, name="in_proj", grid=(t // tm,),
        in_specs=[_rows(tm, d), _resident((1, d)), _resident((n, d))],
        out_specs=[_rows(tm, d), _rows(tm, qkv0), _rows(tm, 2 * d)]
        + [_stream_spec(dil, tm, 3 * GROUP_W) for dil in DILATIONS],
        out_shape=[jax.ShapeDtypeStruct((t, d), BF16), jax.ShapeDtypeStruct((t, qkv0), BF16),
                   jax.ShapeDtypeStruct((t, 2 * d), BF16)]
        + [jax.ShapeDtypeStruct((dil, t // dil, 3 * GROUP_W), BF16) for dil in DILATIONS],
        args=(x, g, wt), semantics=("parallel",), carry=carry)


def _head_masks():
    lane = lax.broadcasted_iota(jnp.int32, (1, GROUP_W), 1)
    return lane, [(lane // HEAD_DIM) == h for h in range(HEADS_PER_GROUP)]


def _stack_heads(v, heads):
    return jnp.concatenate([jnp.where(hm, v, jnp.zeros_like(v)) for hm in heads], axis=0)


def _merge_heads(v, heads):
    out = jnp.zeros((QBLK, GROUP_W), v.dtype)
    for h, hm in enumerate(heads):
        out = jnp.where(hm, v[h * QBLK:(h + 1) * QBLK], out)
    return out


def _pair_block(col, count=STEP_BLOCKS):
    return pl.BlockSpec((count * QBLK, GROUP_W), lambda b: (b, col))


def _edge_block(col, shift, nb, count=STEP_BLOCKS):
    return pl.BlockSpec((QBLK, GROUP_W), lambda b: (jnp.clip(count * b + shift, 0, nb - 1), col))


def _band_mask(has_prev):
    rows = HEADS_PER_GROUP * QBLK
    row = lax.broadcasted_iota(jnp.int32, (rows, 2 * QBLK), 0) & (QBLK - 1)
    col = lax.broadcasted_iota(jnp.int32, (rows, 2 * QBLK), 1)
    return ((col < QBLK) & (col >= row) & has_prev) | ((col >= QBLK) & (col - QBLK <= row))


def _next_mask(has_next):
    rows = HEADS_PER_GROUP * QBLK
    row = lax.broadcasted_iota(jnp.int32, (rows, QBLK), 0) & (QBLK - 1)
    col = lax.broadcasted_iota(jnp.int32, (rows, QBLK), 1)
    return (col >= row) & has_next


def _attn_fwd(s, dil, carry=None):
    t = s.shape[0] * s.shape[1]
    nb = t // QBLK
    per_stream = nb // dil
    assert per_stream % STEP_BLOCKS == 0

    def body(q_ref, kc_ref, kp_ref, vc_ref, vp_ref, o_ref, lse_ref):
        b = pl.program_id(0)
        _, heads = _head_masks()
        first_has_prev = lax.rem(STEP_BLOCKS * b, per_stream) != 0
        for j in range(STEP_BLOCKS):
            rows = slice(j * QBLK, (j + 1) * QBLK)
            if j == 0:
                k2 = jnp.concatenate([kp_ref[...], kc_ref[rows, :]], axis=0)
                v2 = jnp.concatenate([vp_ref[...], vc_ref[rows, :]], axis=0)
            else:
                both = slice((j - 1) * QBLK, (j + 1) * QBLK)
                k2, v2 = kc_ref[both, :], vc_ref[both, :]
            mask = _band_mask(first_has_prev if j == 0 else True)
            sc = jnp.where(mask, _dot_nt(_stack_heads(q_ref[rows, :], heads), k2) * ATTN_SCALE, NEG_INF)
            mx = jnp.max(sc, axis=1, keepdims=True)
            pr = jnp.exp(sc - mx)
            den = jnp.sum(pr, axis=1, keepdims=True)
            o_all = _dot(pr.astype(BF16), v2) / den
            o_ref[rows, :] = _merge_heads(o_all, heads).astype(BF16)
            lse_ref[rows, :] = _merge_heads(jnp.broadcast_to(mx + jnp.log(den), o_all.shape), heads)

    sv = s.reshape(t, 3 * GROUP_W)
    return _call(
        body, name=f"attn_fwd_d{dil}", grid=(nb // STEP_BLOCKS,),
        in_specs=[_pair_block(0), _pair_block(1), _edge_block(1, -1, nb), _pair_block(2), _edge_block(2, -1, nb)],
        out_specs=[_pair_block(0), _pair_block(0)],
        out_shape=[jax.ShapeDtypeStruct((t, GROUP_W), BF16), jax.ShapeDtypeStruct((t, GROUP_W), F32)],
        args=(sv, sv, sv, sv, sv), semantics=("parallel",), carry=carry)


def _group_softmax(parts):
    mx = jnp.maximum(jnp.maximum(parts[0], parts[1]), parts[2])
    es = [jnp.exp(p - mx) for p in parts]
    den = es[0] + es[1] + es[2]
    return [e / den for e in es]


def _mixer_out(x, abcv, gates, os, lses, conv_w, conv_b, b_gate, w_pa, w_pb, w_o, carry=None):
    t, d = x.shape
    cw = conv_w.shape[1]
    tm = min(ROWS_MATMUL, t)

    def body(x_ref, abcv_ref, halo_ref, gates_ref, o0_ref, o1_ref, o2_ref, l0_ref, l1_ref, l2_ref, cw_ref, cb_ref,
             bg_ref, wpa_ref, wpb_ref, wo_ref, x1_ref, ya_ref, yb_ref, yap_ref, ybp_ref, mg_ref):
        m = pl.program_id(0)
        ab = abcv_ref[:, 0:cw].astype(F32)
        u = abcv_ref[:, cw:2 * cw].astype(F32) * abcv_ref[:, 2 * cw:3 * cw].astype(F32)
        hu = halo_ref[:, cw:2 * cw].astype(F32) * halo_ref[:, 2 * cw:3 * cw].astype(F32)
        hu = jnp.where(m > 0, hu, 0.0)
        cv = (cw_ref[0:1, :] * _shift_down(u, hu, 2) + cw_ref[1:2, :] * _shift_down(u, hu, 1)
              + cw_ref[2:3, :] * u + cb_ref[...])
        ya = (ab * cv).astype(BF16)
        ya_ref[...] = ya
        alphas = _group_softmax([_load_streams(r, dil, tm) for r, dil in zip((l0_ref, l1_ref, l2_ref), DILATIONS)])
        for i, (o_ref, dil) in enumerate(zip((o0_ref, o1_ref, o2_ref), DILATIONS)):
            sl = slice(i * GROUP_W, (i + 1) * GROUP_W)
            yb_ref[:, sl] = (alphas[i] * _load_streams(o_ref, dil, tm).astype(F32)).astype(BF16)
        yap = _dot_nt(ya, wpa_ref[...])
        ybp = _dot_nt(yb_ref[...], wpb_ref[...])
        yap_ref[...] = yap.astype(BF16)
        ybp_ref[...] = ybp.astype(BF16)
        sa = _sigmoid(gates_ref[:, 0:d].astype(F32) + bg_ref[0:1, :])
        sb = _sigmoid(gates_ref[:, d:2 * d].astype(F32) + bg_ref[1:2, :])
        merged = (sa * yap + sb * ybp).astype(BF16)
        mg_ref[...] = merged
        x1_ref[...] = x_ref[...] + _dot(merged, wo_ref[...])

    return _call(
        body, name="mixer_out", grid=(t // tm,),
        in_specs=[_rows(tm, d), _rows(tm, 3 * cw), _prev_halo(tm, 3 * cw), _rows(tm, 2 * d)]
        + [_stream_spec(dil, tm, GROUP_W) for dil in DILATIONS] * 2
        + [_resident((3, cw)), _resident((1, cw)), _resident((2, d)),
           _resident((d, cw)), _resident((d, ATTN_W)), _resident((d, d))],
        out_specs=[_rows(tm, d), _rows(tm, cw), _rows(tm, ATTN_W), _rows(tm, d), _rows(tm, d), _rows(tm, d)],
        out_shape=[jax.ShapeDtypeStruct((t, d), F32), jax.ShapeDtypeStruct((t, cw), BF16),
                   jax.ShapeDtypeStruct((t, ATTN_W), BF16), jax.ShapeDtypeStruct((t, d), BF16),
                   jax.ShapeDtypeStruct((t, d), BF16), jax.ShapeDtypeStruct((t, d), BF16)],
        args=(x, abcv, abcv, gates, *[_stream_view(a, dil) for a, dil in zip(os, DILATIONS)],
              *[_stream_view(a, dil) for a, dil in zip(lses, DILATIONS)], conv_w, conv_b, b_gate, w_pa, w_pb, w_o),
        semantics=("parallel",), carry=carry)


def _ffn_fwd(x1, target, g2, w_ut, conv_w, conv_b, w_d, g_f, carry=None):
    t, d = x1.shape
    dff = w_d.shape[0]
    tm = min(256, t)
    ck = _pick_tile(dff, 2816)

    def body(x1_ref, tg_ref, g2_ref, wut_ref, cw_ref, cb_ref, wd_ref, gf_ref, h2_ref, up_ref, act_ref, conv_ref,
             dx2_ref, dx2i_ref, acc_ref, loss_ref, halo_ref):
        m = pl.program_id(0)

        @pl.when(m == 0)
        def _():
            acc_ref[...] = jnp.zeros_like(acc_ref)
            loss_ref[...] = jnp.zeros_like(loss_ref)
            halo_ref[...] = jnp.zeros_like(halo_ref)

        h2 = _permute_rows(_interleave(tm), _rms_fwd(x1_ref[...], g2_ref[...])[0].astype(BF16))
        h2_ref[...] = h2

        def conv(c0):
            p = _dot_nt(h2, wut_ref[c0:c0 + ck, :])
            up_ref[:, c0:c0 + ck] = p.astype(BF16)
            hp = halo_ref[:, c0:c0 + ck]
            halo_ref[:, c0:c0 + ck] = p[tm - HALO:, :]
            return (cw_ref[0:1, c0:c0 + ck] * _shift_down_il(p, hp, 2)
                    + cw_ref[1:2, c0:c0 + ck] * _shift_down_il(p, hp, 1)
                    + cw_ref[2:3, c0:c0 + ck] * p + cb_ref[:, c0:c0 + ck])

        down = jnp.zeros((tm, d), F32)
        for c0 in range(0, dff, ck):
            gate = conv(c0)
            val = conv(dff + c0)
            conv_ref[:, c0:c0 + ck] = gate.astype(BF16)
            conv_ref[:, dff + c0:dff + c0 + ck] = val.astype(BF16)
            act = (gate * _sigmoid(gate) * val).astype(BF16)
            act_ref[:, c0:c0 + ck] = act
            down = down + _dot(act, wd_ref[c0:c0 + ck, :])
        x2 = x1_ref[...] + _permute_rows(_interleave(tm, inverse=True), down)
        y, _ = _rms_fwd(x2, gf_ref[...])
        diff = y - tg_ref[...]
        loss_ref[...] += 0.5 * jnp.sum(jnp.mean(diff * diff, axis=-1, keepdims=True))
        dx2, dg = _rms_bwd(x2, gf_ref[...], diff * (1.0 / d))
        dx2_ref[...] = dx2
        dx2i_ref[...] = _permute_rows(_interleave(tm), dx2.astype(BF16))
        acc_ref[...] += _stack_rows([_colsum(dg)], d)

    return _call(
        body, name="ffn_fwd", grid=(t // tm,),
        in_specs=[_rows(tm, d), _rows(tm, d), _resident((1, d)), _resident((2 * dff, d)), _resident((3, 2 * dff)),
                  _resident((1, 2 * dff)), _resident((dff, d)), _resident((1, d))],
        out_specs=[_rows(tm, d), _rows(tm, 2 * dff), _rows(tm, dff), _rows(tm, 2 * dff), _rows(tm, d), _rows(tm, d),
                   _acc_spec(d), _acc_spec(LANES)],
        out_shape=[jax.ShapeDtypeStruct((t, d), BF16), jax.ShapeDtypeStruct((t, 2 * dff), BF16),
                   jax.ShapeDtypeStruct((t, dff), BF16), jax.ShapeDtypeStruct((t, 2 * dff), BF16),
                   jax.ShapeDtypeStruct((t, d), F32), jax.ShapeDtypeStruct((t, d), BF16),
                   jax.ShapeDtypeStruct((SUBLANES, d), F32), jax.ShapeDtypeStruct((SUBLANES, LANES), F32)],
        args=(x1, target, g2, w_ut, conv_w, conv_b, w_d, g_f), semantics=("arbitrary",), carry=carry,
        scratch=[pltpu.VMEM((HALO, 2 * dff), F32)])


def _ffn_act_bwd(dx2, conv, w_d):
    t, d = dx2.shape
    dff = w_d.shape[0]
    tm = min(256, t)
    ck = _pick_tile(dff, 2816)

    def body(dx2_ref, conv_ref, wd_ref, dup_ref, acc_ref):
        m = pl.program_id(0)

        @pl.when(m == 0)
        def _():
            acc_ref[...] = jnp.zeros_like(acc_ref)

        dx2v = dx2_ref[...]
        for c0 in range(0, dff, ck):
            dact = _dot_nt(dx2v, wd_ref[c0:c0 + ck, :])
            gate = conv_ref[:, c0:c0 + ck].astype(F32)
            val = conv_ref[:, dff + c0:dff + c0 + ck].astype(F32)
            sg = _sigmoid(gate)
            dval = dact * gate * sg
            dgate = dact * val * sg * (1.0 + gate * (1.0 - sg))
            dup_ref[:, c0:c0 + ck] = dgate.astype(BF16)
            dup_ref[:, dff + c0:dff + c0 + ck] = dval.astype(BF16)
            acc_ref[:, c0:c0 + ck] += _stack_rows([_colsum(dgate)], ck)
            acc_ref[:, dff + c0:dff + c0 + ck] += _stack_rows([_colsum(dval)], ck)

    return pl.pallas_call(
        body, name="ffn_act_bwd", grid=(t // tm,),
        in_specs=[_rows(tm, d), _rows(tm, 2 * dff), _resident((dff, d))],
        out_specs=[_rows(tm, 2 * dff), _acc_spec(2 * dff)],
        out_shape=[jax.ShapeDtypeStruct((t, 2 * dff), BF16), jax.ShapeDtypeStruct((SUBLANES, 2 * dff), F32)],
        compiler_params=_params("arbitrary"),
    )(dx2, conv, w_d)


def _ffn_up_bwd(dup, up_pre, x1, dx2, conv_w, w_u, g2, carry=None):
    t, d = x1.shape
    n = dup.shape[1]
    tm = min(256, t)
    ck = _pick_tile(n, 256)
    last = t // tm - 1

    def body(dup_ref, nxt_ref, up_ref, x1_ref, dx2_ref, cw_ref, wu_ref, g2_ref, dpre_ref, dx1_ref, acc_ref, accw_ref):
        m = pl.program_id(0)

        @pl.when(m == 0)
        def _():
            acc_ref[...] = jnp.zeros_like(acc_ref)
            accw_ref[...] = jnp.zeros_like(accw_ref)

        dh = jnp.zeros((tm, d), F32)
        for c0 in range(0, n, ck):
            du = dup_ref[:, c0:c0 + ck].astype(F32)
            hn = jnp.where(m < last, nxt_ref[:, c0:c0 + ck].astype(F32), 0.0)
            du1 = _shift_up_il(du, hn, 1)
            du2 = _shift_up_il(du, hn, 2)
            dpre = (cw_ref[2:3, c0:c0 + ck] * du + cw_ref[1:2, c0:c0 + ck] * du1
                    + cw_ref[0:1, c0:c0 + ck] * du2).astype(BF16)
            dpre_ref[:, c0:c0 + ck] = dpre
            dh = dh + _dot(dpre, wu_ref[c0:c0 + ck, :])
            p = up_ref[:, c0:c0 + ck].astype(F32)
            accw_ref[:, c0:c0 + ck] += _stack_rows([_colsum(du2 * p), _colsum(du1 * p), _colsum(du * p)], ck)
        dh = _permute_rows(_interleave(tm, inverse=True), dh)
        dx, dg = _rms_bwd(x1_ref[...], g2_ref[...], dh)
        dx1_ref[...] = dx2_ref[...] + dx
        acc_ref[...] += _stack_rows([_colsum(dg)], d)

    return _call(
        body, name="ffn_up_bwd", grid=(t // tm,),
        in_specs=[_rows(tm, n), _next_halo(tm, n, t), _rows(tm, n), _rows(tm, d), _rows(tm, d), _resident((3, n)),
                  _resident((n, d)), _resident((1, d))],
        out_specs=[_rows(tm, n), _rows(tm, d), _acc_spec(d), _acc_spec(n)],
        out_shape=[jax.ShapeDtypeStruct((t, n), BF16), jax.ShapeDtypeStruct((t, d), F32),
                   jax.ShapeDtypeStruct((SUBLANES, d), F32), jax.ShapeDtypeStruct((SUBLANES, n), F32)],
        args=(dup, dup, up_pre, x1, dx2, conv_w, w_u, g2), semantics=("arbitrary",), carry=carry)


def _tn_matmul(a, b, name):
    t, mdim = a.shape
    n = b.shape[1]
    tk = min(2048, t)
    tmm = _pick_tile(mdim, 1536)
    tn = _pick_tile(n, 1024)

    def body(a_ref, b_ref, o_ref, acc_ref):
        k = pl.program_id(2)

        @pl.when(k == 0)
        def _():
            acc_ref[...] = jnp.zeros_like(acc_ref)

        acc_ref[...] += _dot_tn(a_ref[...].astype(BF16), b_ref[...].astype(BF16))

        @pl.when(k == t // tk - 1)
        def _():
            o_ref[...] = acc_ref[...].astype(BF16)

    return pl.pallas_call(
        body, name=name, grid=(mdim // tmm, n // tn, t // tk),
        in_specs=[pl.BlockSpec((tk, tmm), lambda i, j, k: (k, i)), pl.BlockSpec((tk, tn), lambda i, j, k: (k, j))],
        out_specs=pl.BlockSpec((tmm, tn), lambda i, j, k: (i, j)),
        out_shape=jax.ShapeDtypeStruct((mdim, n), BF16),
        scratch_shapes=[pltpu.VMEM((tmm, tn), F32)],
        compiler_params=_params("parallel", "parallel", "arbitrary"),
    )(a, b)


def _mixer_bwd(dx1, gates, yap, ybp, os, lses, b_gate, w_o, w_pa, w_pb):
    t, d = dx1.shape
    cw = w_pa.shape[1]
    tm = min(ROWS_MATMUL, t)

    def body(dx1_ref, gates_ref, yap_ref, ybp_ref, o0_ref, o1_ref, o2_ref, l0_ref, l1_ref, l2_ref, bg_ref, wo_ref,
             wpa_ref, wpb_ref, dgates_ref, dyap_ref, dybp_ref, dya_ref, do0_ref, do1_ref, do2_ref, dl0_ref, dl1_ref,
             dl2_ref, acc_ref):
        m = pl.program_id(0)

        @pl.when(m == 0)
        def _():
            acc_ref[...] = jnp.zeros_like(acc_ref)

        dmg = _dot_nt(dx1_ref[...].astype(BF16), wo_ref[...])
        sa = _sigmoid(gates_ref[:, 0:d].astype(F32) + bg_ref[0:1, :])
        sb = _sigmoid(gates_ref[:, d:2 * d].astype(F32) + bg_ref[1:2, :])
        dyap = (dmg * sa).astype(BF16)
        dybp = (dmg * sb).astype(BF16)
        dga = dmg * yap_ref[...].astype(F32) * sa * (1.0 - sa)
        dgb = dmg * ybp_ref[...].astype(F32) * sb * (1.0 - sb)
        dyap_ref[...] = dyap
        dybp_ref[...] = dybp
        dgates_ref[:, 0:d] = dga.astype(BF16)
        dgates_ref[:, d:2 * d] = dgb.astype(BF16)
        acc_ref[...] += _stack_rows([_colsum(dga), _colsum(dgb)], d)
        dya_ref[...] = _dot(dyap, wpa_ref[...]).astype(BF16)
        dyb = _dot(dybp, wpb_ref[...])

        ri = lax.broadcasted_iota(jnp.int32, (GROUP_W, GROUP_W), 0) // HEAD_DIM
        ci = lax.broadcasted_iota(jnp.int32, (GROUP_W, GROUP_W), 1) // HEAD_DIM
        same_head = (ri == ci).astype(BF16)
        alphas = _group_softmax([_load_streams(r, dil, tm) for r, dil in zip((l0_ref, l1_ref, l2_ref), DILATIONS)])
        prod = jnp.zeros((tm, GROUP_W), F32)
        for i, (o_ref, do_ref, dil) in enumerate(zip((o0_ref, o1_ref, o2_ref), (do0_ref, do1_ref, do2_ref), DILATIONS)):
            dov = alphas[i] * dyb[:, i * GROUP_W:(i + 1) * GROUP_W]
            _store_streams(do_ref, dil, tm, dov.astype(BF16))
            prod = prod + dov * _load_streams(o_ref, dil, tm).astype(F32)
        hi = prod.astype(BF16)
        lo = (prod - hi.astype(F32)).astype(BF16)
        dtot = _dot(hi, same_head) + _dot(lo, same_head)
        for alpha, dl_ref, dil in zip(alphas, (dl0_ref, dl1_ref, dl2_ref), DILATIONS):
            _store_streams(dl_ref, dil, tm, alpha * dtot)

    streams = [_stream_spec(dil, tm, GROUP_W) for dil in DILATIONS]
    res = _call(
        body, name="mixer_bwd", grid=(t // tm,),
        in_specs=[_rows(tm, d), _rows(tm, 2 * d), _rows(tm, d), _rows(tm, d)] + streams * 2
        + [_resident((2, d)), _resident((d, d)), _resident((d, cw)), _resident((d, ATTN_W))],
        out_specs=[_rows(tm, 2 * d), _rows(tm, d), _rows(tm, d), _rows(tm, cw)] + streams * 2 + [_acc_spec(d)],
        out_shape=[jax.ShapeDtypeStruct((t, 2 * d), BF16), jax.ShapeDtypeStruct((t, d), BF16),
                   jax.ShapeDtypeStruct((t, d), BF16), jax.ShapeDtypeStruct((t, cw), BF16)]
        + [jax.ShapeDtypeStruct((dil, t // dil, GROUP_W), BF16) for dil in DILATIONS]
        + [jax.ShapeDtypeStruct((dil, t // dil, GROUP_W), F32) for dil in DILATIONS]
        + [jax.ShapeDtypeStruct((SUBLANES, d), F32)],
        args=(dx1, gates, yap, ybp, *[_stream_view(a, dil) for a, dil in zip(os, DILATIONS)],
              *[_stream_view(a, dil) for a, dil in zip(lses, DILATIONS)], b_gate, w_o, w_pa, w_pb),
        semantics=("arbitrary",))
    dgates, dyap, dybp, dya = res[:4]
    dos = [a.reshape(t, GROUP_W) for a in res[4:7]]
    dls = [a.reshape(t, GROUP_W) for a in res[7:10]]
    return dgates, dyap, dybp, dya, dos, dls, res[10]


def _attn_bwd(s, do, lse, dl, dil, carry=None):
    t = s.shape[0] * s.shape[1]
    nb = t // QBLK
    per_stream = nb // dil
    count = BWD_STEP_BLOCKS
    assert per_stream % count == 0

    def body(q_ref, qn_ref, kc_ref, kp_ref, vc_ref, vp_ref, do_ref, don_ref, lse_ref, lsen_ref, dl_ref, dln_ref,
             ds_ref):
        b = pl.program_id(0)
        lane, heads = _head_masks()
        first_has_prev = lax.rem(count * b, per_stream) != 0
        last_has_next = lax.rem(count * (b + 1), per_stream) != 0

        def cols(v):
            return jnp.concatenate([jnp.sum(jnp.where(lane == h * HEAD_DIM, v, 0.0), axis=1, keepdims=True)
                                    for h in range(HEADS_PER_GROUP)], axis=0)

        def pair(qs, dos, k, v, valid, lse_c, dl_c):
            s = jnp.where(valid, _dot_nt(qs, k) * ATTN_SCALE, NEG_INF)
            p = jnp.exp(s - lse_c)
            ds = p * (_dot_nt(dos, v) - dl_c)
            return p.astype(BF16), ds.astype(BF16)

        for j in range(count):
            rows, hi = slice(j * QBLK, (j + 1) * QBLK), slice((j + 1) * QBLK, (j + 2) * QBLK)
            q, do, lse, dl = q_ref[rows, :], do_ref[rows, :], lse_ref[rows, :], dl_ref[rows, :]
            kc, vc = kc_ref[rows, :], vc_ref[rows, :]
            if j == 0:
                k2 = jnp.concatenate([kp_ref[...], kc], axis=0)
                v2 = jnp.concatenate([vp_ref[...], vc], axis=0)
                mask = _band_mask(first_has_prev)
            else:
                both = slice((j - 1) * QBLK, (j + 1) * QBLK)
                k2, v2, mask = kc_ref[both, :], vc_ref[both, :], _band_mask(True)
            if j < count - 1:
                qn, don, lsen, dln = q_ref[hi, :], do_ref[hi, :], lse_ref[hi, :], dl_ref[hi, :]
                mask_n = _next_mask(True)
            else:
                qn, don, lsen, dln = qn_ref[...], don_ref[...], lsen_ref[...], dln_ref[...]
                mask_n = _next_mask(last_has_next)
            qs, qns = _stack_heads(q, heads), _stack_heads(qn, heads)
            dos, dons = _stack_heads(do, heads), _stack_heads(don, heads)
            p_q, ds_q = pair(qs, dos, k2, v2, mask, cols(lse), cols(dl))
            p_n, ds_n = pair(qns, dons, kc, vc, mask_n, cols(lsen), cols(dln))
            dq = _merge_heads(_dot(ds_q, k2), heads)
            dk = _dot_tn(jnp.concatenate([ds_q[:, QBLK:], ds_n], axis=0), jnp.concatenate([qs, qns], axis=0))
            dv = _dot_tn(jnp.concatenate([p_q[:, QBLK:], p_n], axis=0), jnp.concatenate([dos, dons], axis=0))
            ds_ref[rows, 0:GROUP_W] = (dq * ATTN_SCALE).astype(BF16)
            ds_ref[rows, GROUP_W:2 * GROUP_W] = (dk * ATTN_SCALE).astype(BF16)
            ds_ref[rows, 2 * GROUP_W:3 * GROUP_W] = dv.astype(BF16)

    sv = s.reshape(t, 3 * GROUP_W)
    cur, nxt = _pair_block(0, count), _edge_block(0, count, nb, count)
    return _call(
        body, name=f"attn_bwd_d{dil}", grid=(nb // count,),
        in_specs=[cur, nxt, _pair_block(1, count), _edge_block(1, -1, nb, count), _pair_block(2, count),
                  _edge_block(2, -1, nb, count), cur, nxt, cur, nxt, cur, nxt],
        out_specs=[pl.BlockSpec((count * QBLK, 3 * GROUP_W), lambda b: (b, 0))],
        out_shape=[jax.ShapeDtypeStruct((t, 3 * GROUP_W), BF16)],
        args=(sv, sv, sv, sv, sv, sv, do, do, lse, lse, dl, dl), semantics=("parallel",), carry=carry)


def _conv_mixer_bwd(abcv, dya, conv_w, conv_b):
    t = abcv.shape[0]
    cw = conv_w.shape[1]
    tm = min(1024, t)
    last = t // tm - 1

    def body(a_ref, ap_ref, an_ref, dya_ref, dyan_ref, cw_ref, cb_ref, d_ref, acc_ref):
        m = pl.program_id(0)

        @pl.when(m == 0)
        def _():
            acc_ref[...] = jnp.zeros_like(acc_ref)

        ab = a_ref[:, 0:cw].astype(F32)
        ac = a_ref[:, cw:2 * cw].astype(F32)
        av = a_ref[:, 2 * cw:3 * cw].astype(F32)
        u = ac * av
        hu = ap_ref[:, cw:2 * cw].astype(F32) * ap_ref[:, 2 * cw:3 * cw].astype(F32)
        hu = jnp.where(m > 0, hu, 0.0)
        u1 = _shift_down(u, hu, 1)
        u2 = _shift_down(u, hu, 2)
        cv = cw_ref[0:1, :] * u2 + cw_ref[1:2, :] * u1 + cw_ref[2:3, :] * u + cb_ref[...]
        dya_v = dya_ref[...].astype(F32)
        dcv = dya_v * ab
        ndcv = jnp.where(m < last, dyan_ref[...].astype(F32) * an_ref[:, 0:cw].astype(F32), 0.0)
        du = (cw_ref[2:3, :] * dcv + cw_ref[1:2, :] * _shift_up(dcv, ndcv, 1)
              + cw_ref[0:1, :] * _shift_up(dcv, ndcv, 2))
        d_ref[:, 0:cw] = (dya_v * cv).astype(BF16)
        d_ref[:, cw:2 * cw] = (du * av).astype(BF16)
        d_ref[:, 2 * cw:3 * cw] = (du * ac).astype(BF16)
        acc_ref[...] += _stack_rows([_colsum(dcv * u2), _colsum(dcv * u1), _colsum(dcv * u), _colsum(dcv)], cw)

    return pl.pallas_call(
        body, name="conv_mixer_bwd", grid=(t // tm,),
        in_specs=[_rows(tm, 3 * cw), _prev_halo(tm, 3 * cw), _next_halo(tm, 3 * cw, t), _rows(tm, cw),
                  _next_halo(tm, cw, t), _resident((3, cw)), _resident((1, cw))],
        out_specs=[_rows(tm, 3 * cw), _acc_spec(cw)],
        out_shape=[jax.ShapeDtypeStruct((t, 3 * cw), BF16), jax.ShapeDtypeStruct((SUBLANES, cw), F32)],
        compiler_params=_params("arbitrary"),
    )(abcv, abcv, abcv, dya, dya, conv_w, conv_b)


def _in_proj_bwd(x, dx1, dabcv, dss, dgates, w_in, g1, carry=None):
    t, d = x.shape
    qkv0 = dabcv.shape[1]
    n = w_in.shape[0]
    tm = min(ROWS_MATMUL, t)

    def body(x_ref, dx1_ref, da_ref, ds0_ref, ds1_ref, ds2_ref, dg_ref, w_ref, g_ref, dx_ref, acc_ref):
        m = pl.program_id(0)

        @pl.when(m == 0)
        def _():
            acc_ref[...] = jnp.zeros_like(acc_ref)

        dss_tok = [_load_streams(ds_ref, dil, tm) for ds_ref, dil in zip((ds0_ref, ds1_ref, ds2_ref), DILATIONS)]
        dqkv = jnp.concatenate([ds[:, j * GROUP_W:(j + 1) * GROUP_W] for j in range(3) for ds in dss_tok], axis=1)
        dh = (_dot(da_ref[...], w_ref[0:qkv0, :]) + _dot(dqkv, w_ref[qkv0:qkv0 + 3 * ATTN_W, :])
              + _dot(dg_ref[...], w_ref[qkv0 + 3 * ATTN_W:n, :]))
        dx, dg = _rms_bwd(x_ref[...], g_ref[...], dh)
        dx_ref[...] = dx1_ref[...] + dx
        acc_ref[...] += _stack_rows([_colsum(dg)], d)

    return _call(
        body, name="in_proj_bwd", grid=(t // tm,),
        in_specs=[_rows(tm, d), _rows(tm, d), _rows(tm, qkv0)]
        + [_stream_spec(dil, tm, 3 * GROUP_W) for dil in DILATIONS]
        + [_rows(tm, 2 * d), _resident((n, d)), _resident((1, d))],
        out_specs=[_rows(tm, d), _acc_spec(d)],
        out_shape=[jax.ShapeDtypeStruct((t, d), F32), jax.ShapeDtypeStruct((SUBLANES, d), F32)],
        args=(x, dx1, dabcv, *[_stream_view(a, dil) for a, dil in zip(dss, DILATIONS)], dgates, w_in, g1),
        semantics=("arbitrary",), carry=carry)


def _dw_in_qkv(ds, h, dil):
    t, d = h.shape
    tk = min(1024, t)
    sub = min(256, t)
    width = 3 * GROUP_W

    def body(ds_ref, h_ref, o_ref, acc_ref):
        k = pl.program_id(0)

        @pl.when(k == 0)
        def _():
            acc_ref[...] = jnp.zeros_like(acc_ref)

        upd = None
        for i in range(tk // sub):
            rows = ds_ref[:, i * (sub // dil):(i + 1) * (sub // dil), :].reshape(sub, width)
            if dil > 1:
                rows = _permute_rows(_perm(dil, sub, inverse=True), rows)
            term = _dot_tn(rows, h_ref[i * sub:(i + 1) * sub, :])
            upd = term if upd is None else upd + term
        acc_ref[...] += upd

        @pl.when(k == t // tk - 1)
        def _():
            o_ref[...] = acc_ref[...].astype(BF16)

    return pl.pallas_call(
        body, name=f"dw_in_qkv_d{dil}", grid=(t // tk,),
        in_specs=[_stream_spec(dil, tk, width), _rows(tk, d)],
        out_specs=pl.BlockSpec((width, d), lambda k: (0, 0)),
        out_shape=jax.ShapeDtypeStruct((width, d), BF16),
        scratch_shapes=[pltpu.VMEM((width, d), F32)],
        compiler_params=_params("arbitrary"),
    )(_stream_view(ds, dil), h)


def _local_step(x, target, p, late):
    cw = p["conv_a_w"].shape[1]
    (h, abcv, gates, *ss), (g_up,) = _in_proj(x, p["norm_mix_g"], p["w_in"], cw,
                                              carry=_Exchange("gather", [late["w_up"]]))
    w_up = _full_from_gathered(g_up)
    (o0, lse0), (g_pa,) = _attn_fwd(ss[0], DILATIONS[0], carry=_Exchange("gather", [late["w_proj_a"]]))
    (o1, lse1), (g_pb,) = _attn_fwd(ss[1], DILATIONS[1], carry=_Exchange("gather", [late["w_proj_b"]]))
    (o2, lse2), (g_out,) = _attn_fwd(ss[2], DILATIONS[2], carry=_Exchange("gather", [late["w_out"]]))
    w_pa, w_pb, w_out = [_full_from_gathered(g) for g in (g_pa, g_pb, g_out)]
    os, lses = (o0, o1, o2), (lse0, lse1, lse2)
    (x1, ya, yb, yap, ybp, merged), (g_down,) = _mixer_out(
        x, abcv, gates, os, lses, p["conv_a_w"], p["conv_a_b"], p["b_gate"], w_pa, w_pb, w_out,
        carry=_Exchange("gather", [late["w_down"]]))
    w_down = _full_from_gathered(g_down)
    h2, up_pre, act, conv, dx2, dx2i, acc_gf, loss = _ffn_fwd(x1, target, p["norm_ffn_g"], w_up, p["ffn_conv_w"],
                                                              p["ffn_conv_b"], w_down, p["final_norm_g"])

    parts, got = {}, {}
    dup, acc_fb = _ffn_act_bwd(dx2i, conv, w_down)
    parts["w_down"] = _by_destination(_tn_matmul(act, dx2i, "dw_down"))
    (dpre, dx1, acc_g2, acc_fw), (got["w_down"],) = _ffn_up_bwd(dup, up_pre, x1, dx2, p["ffn_conv_w"], w_up,
                                                                p["norm_ffn_g"],
                                                                carry=_Exchange("scatter", [parts["w_down"]]))
    parts["w_up"] = _by_destination(_tn_matmul(dpre, h2, "dw_up"))
    dgates, dyap, dybp, dya, dos, dls, acc_bg = _mixer_bwd(dx1, gates, yap, ybp, os, lses, p["b_gate"], w_out,
                                                           w_pa, w_pb)
    parts["w_out"] = _by_destination(_tn_matmul(merged, dx1, "dw_out"))
    parts["w_proj_a"] = _by_destination(_tn_matmul(dyap, ya, "dw_proj_a"))
    parts["w_proj_b"] = _by_destination(_tn_matmul(dybp, yb, "dw_proj_b"))
    minor = ("w_out", "w_proj_a", "w_proj_b")
    half = parts["w_up"].shape[1] // 2
    (ds0,), received = _attn_bwd(ss[0], dos[0], lses[0], dls[0], DILATIONS[0],
                                 carry=_Exchange("scatter", [parts[n] for n in minor]))
    got.update(zip(minor, received))
    (ds1,), first_half = _attn_bwd(ss[1], dos[1], lses[1], dls[1], DILATIONS[1],
                                   carry=_Exchange("scatter", [parts["w_up"]], rows=(0, half)))
    (ds2,), (got["w_up"],) = _attn_bwd(ss[2], dos[2], lses[2], dls[2], DILATIONS[2],
                                       carry=_Exchange("scatter", [parts["w_up"]], rows=(half, half),
                                                       into=first_half))
    dss = [ds0, ds1, ds2]
    dabcv, acc_ca = _conv_mixer_bwd(abcv, dya, p["conv_a_w"], p["conv_a_b"])
    dw_s = [_dw_in_qkv(ds, h, dil) for ds, dil in zip(dss, DILATIONS)]
    dw_qkv = [w[j * GROUP_W:(j + 1) * GROUP_W] for j in range(3) for w in dw_s]
    g_w_in = jnp.concatenate([_tn_matmul(dabcv, h, "dw_in_a"), *dw_qkv, _tn_matmul(dgates, h, "dw_in_g")], axis=0)
    parts["w_in"] = _by_destination(g_w_in)
    (dx, acc_g1), (got["w_in"],) = _in_proj_bwd(x, dx1, dabcv, dss, dgates, p["w_in"], p["norm_mix_g"],
                                                carry=_Exchange("scatter", [parts["w_in"]]))
    small = dict(norm_mix_g=acc_g1[0:1], b_gate=acc_bg[0:2], conv_a_w=acc_ca[0:3], conv_a_b=acc_ca[3:4],
                 norm_ffn_g=acc_g2[0:1], ffn_conv_w=acc_fw[0:3], ffn_conv_b=acc_fb[0:1], final_norm_g=acc_gf[0:1])
    return loss[0, 0], dx, parts, got, small


def _all_gather(shards):
    n = len(shards)

    def body(*refs):
        ins, outs = refs[:n], refs[n:2 * n]
        send_sems, recv_sems, local_sems = refs[2 * n:]
        x, y, c = _mesh_pos()
        me, sibling = (x, y, c), (x, y, 1 - c)
        chips = [(1 - x, y), (x, 1 - y), (1 - x, 1 - y)]

        def copy(i, k, block, to, src=None):
            rows = outs[i].at[_dev_index(*block)]
            return pltpu.make_async_remote_copy(
                src_ref=rows if src is None else src, dst_ref=rows, send_sem=send_sems.at[i, k],
                recv_sem=recv_sems.at[i, k], device_id=to, device_id_type=MESH)

        mine, first, passed = [], [], []
        for i in range(n):
            cp = pltpu.make_async_copy(ins[i], outs[i].at[_dev_index(*me)], local_sems.at[i])
            cp.start()
            mine.append(cp)
            first.append(copy(i, 0, me, sibling, src=ins[i]))
            first += [copy(i, 1 + j, me, (*chip, c), src=ins[i]) for j, chip in enumerate(chips)]
        for cp in first:
            cp.start()
        for i in range(n):
            for j, chip in enumerate(chips):
                copy(i, 1 + j, (*chip, c), me).wait_recv()
                fw = copy(i, 4 + j, (*chip, c), sibling)
                fw.start()
                passed.append(fw)
        for i in range(n):
            copy(i, 0, sibling, me).wait_recv()
            for j, chip in enumerate(chips):
                copy(i, 4 + j, (*chip, 1 - c), me).wait_recv()
        for cp in first + passed:
            cp.wait_send()
        for cp in mine:
            cp.wait()

    return pl.pallas_call(
        body, name="all_gather_weights",
        in_specs=[ANY] * n, out_specs=[ANY] * n,
        out_shape=[jax.ShapeDtypeStruct((N_DEV,) + s.shape, s.dtype) for s in shards],
        scratch_shapes=[pltpu.SemaphoreType.DMA((n, 7)), pltpu.SemaphoreType.DMA((n, 7)),
                        pltpu.SemaphoreType.DMA((n,))],
    )(*shards)


def _all_reduce_small(v):
    r = v.shape[0]

    def body(v_ref, o_ref, gath, send_sems, recv_sems):
        x, y, c = _mesh_pos()
        me = _dev_index(x, y, c)
        gath[me] = v_ref[...]
        flips = [(kx, ky, kc) for kx in (0, 1) for ky in (0, 1) for kc in (0, 1)][1:]
        copies = []
        for k, (kx, ky, kc) in enumerate(flips):
            px = 1 - x if kx else x
            py = 1 - y if ky else y
            pc = 1 - c if kc else c
            cp = pltpu.make_async_remote_copy(
                src_ref=v_ref, dst_ref=gath.at[me], send_sem=send_sems.at[k], recv_sem=recv_sems.at[k],
                device_id=(px, py, pc), device_id_type=MESH)
            cp.start()
            copies.append((cp, _dev_index(px, py, pc)))
        for k, (cp, peer) in enumerate(copies):
            pltpu.make_async_remote_copy(
                src_ref=v_ref, dst_ref=gath.at[peer], send_sem=send_sems.at[k], recv_sem=recv_sems.at[k],
                device_id=(x, y, c), device_id_type=MESH).wait_recv()
        for cp, _ in copies:
            cp.wait_send()
        total = gath[0]
        for j in range(1, N_DEV):
            total = total + gath[j]
        o_ref[...] = total

    return pl.pallas_call(
        body, name="all_reduce_small",
        in_specs=[pl.BlockSpec(memory_space=pltpu.VMEM)], out_specs=pl.BlockSpec(memory_space=pltpu.VMEM),
        out_shape=jax.ShapeDtypeStruct((r, LANES), F32),
        scratch_shapes=[pltpu.VMEM((N_DEV, r, LANES), F32), pltpu.SemaphoreType.DMA((7,)),
                        pltpu.SemaphoreType.DMA((7,))],
    )(v)


def _adamw_math(w, g, m, v):
    m2 = ADAM_B1 * m + (1.0 - ADAM_B1) * g
    v2 = ADAM_B2 * v + (1.0 - ADAM_B2) * (g * g)
    m_hat = m2 / (1.0 - ADAM_B1 ** ADAM_STEP)
    v_hat = v2 / (1.0 - ADAM_B2 ** ADAM_STEP)
    delta = -ADAM_LR * (m_hat / (jnp.sqrt(v_hat) + ADAM_EPS) + ADAM_WD * w)
    return delta, m2, v2


def _adamw_big(w, m, v, part, got, me):
    r, c = part.shape[1:]
    flip = w.shape != (r, c)
    tr = r if flip else max(t for t in range(HALO, min(r, 512) + 1, HALO) if r % t == 0)

    def body(me_ref, w_ref, m_ref, v_ref, own_ref, *rest):
        del me_ref
        got_refs, (g_out, d_out, m_out, v_out) = rest[:N_DEV - 1], rest[N_DEV - 1:]
        g = own_ref[...].astype(F32)
        for ref in got_refs:
            g = g + ref[...].astype(F32)
        if flip:
            g = g.T
        delta, m2, v2 = _adamw_math(w_ref[...], g, m_ref[...], v_ref[...])
        g_out[...] = g
        d_out[...] = delta
        m_out[...] = m2
        v_out[...] = v2

    def peer_block(k):
        return pl.BlockSpec((None, tr, c), lambda i, me_ref: (jnp.bitwise_xor(me_ref[0], k), i, 0))

    plain = pl.BlockSpec(w.shape if flip else (tr, c), lambda i, me_ref: (i, 0))
    out = jax.ShapeDtypeStruct(w.shape, F32)
    return pl.pallas_call(
        body, name="adamw_big",
        grid_spec=pltpu.PrefetchScalarGridSpec(
            num_scalar_prefetch=1, grid=(r // tr,),
            in_specs=[plain, plain, plain] + [peer_block(k) for k in range(N_DEV)],
            out_specs=[plain] * 4),
        out_shape=[out] * 4,
        compiler_params=_params("parallel"),
    )(me, w, m, v, part, *([got] * (N_DEV - 1)))


def _adamw_small(ws, gs, ms, vs):
    n = len(ws)

    def body(*refs):
        ins, outs = refs[:4 * n], refs[4 * n:]
        for i in range(n):
            delta, m2, v2 = _adamw_math(ins[i][...], ins[n + i][...], ins[2 * n + i][...], ins[3 * n + i][...])
            outs[i][...] = delta
            outs[n + i][...] = m2
            outs[2 * n + i][...] = v2

    out = [jax.ShapeDtypeStruct(w.shape, F32) for w in ws]
    res = pl.pallas_call(body, name="adamw_small", out_shape=out * 3)(*ws, *gs, *ms, *vs)
    return res[:n], res[n:2 * n], res[2 * n:]


BIG = ("w_in", "w_proj_a", "w_proj_b", "w_out", "w_up", "w_down")
LATE = ("w_proj_a", "w_proj_b", "w_out", "w_up", "w_down")
COLUMN_SHARDED = ("w_in", "w_proj_a", "w_proj_b", "w_up")
WIDE_COLUMN_SHARDED = ("w_in", "w_up")
SMALL = ("norm_mix_g", "b_gate", "conv_a_w", "conv_a_b", "norm_ffn_g", "ffn_conv_w", "ffn_conv_b", "final_norm_g")
SMALL_SHARDED = ("b_gate", "conv_a_w", "ffn_conv_w")
WEIGHTS = ("norm_mix_g", "w_in", "b_gate", "conv_a_w", "conv_a_b", "w_proj_a", "w_proj_b", "w_out", "norm_ffn_g",
           "w_up", "ffn_conv_w", "ffn_conv_b", "w_down", "final_norm_g")


def _pack(vectors, rows):
    flat = jnp.concatenate([v.reshape(-1) for v in vectors])
    return jnp.pad(flat, (0, rows * LANES - flat.shape[0])).reshape(rows, LANES)


def _packed_rows(count):
    rows = -(-count // LANES)
    return -(-rows // SUBLANES) * SUBLANES


def _unpack(packed, shapes):
    flat = packed.reshape(-1)
    out, lo = [], 0
    for s in shapes:
        size = 1
        for dim in s:
            size *= dim
        out.append(flat[lo:lo + size].reshape(s))
        lo += size
    return out


def _full_from_gathered(gathered):
    _, r, c = gathered.shape
    return gathered.reshape(N_DEV * r, c)


def _by_destination(grad):
    rr, cc = grad.shape
    return grad.reshape(N_DEV, rr // N_DEV, cc)


def _block2d(name, a):
    a = a.reshape(a.shape[-2:])
    return a.T if name in WIDE_COLUMN_SHARDED else a


def kernel(x, norm_mix_g, w_in, b_gate, conv_a_w, conv_a_b, w_proj_a, w_proj_b, w_out, norm_ffn_g, w_up, ffn_conv_w, ffn_conv_b, w_down, final_norm_g, loss_target, m_norm_mix_g, m_w_in, m_b_gate, m_conv_a_w, m_conv_a_b, m_w_proj_a, m_w_proj_b, m_w_out, m_norm_ffn_g, m_w_up, m_ffn_conv_w, m_ffn_conv_b, m_w_down, m_final_norm_g, v_norm_mix_g, v_w_in, v_b_gate, v_conv_a_w, v_conv_a_b, v_w_proj_a, v_w_proj_b, v_w_out, v_norm_ffn_g, v_w_up, v_ffn_conv_w, v_ffn_conv_b, v_w_down, v_final_norm_g):
    given = dict(locals())
    shard = {n: given[n] for n in WEIGHTS}
    mom_m = {n: given["m_" + n] for n in WEIGHTS}
    mom_v = {n: given["v_" + n] for n in WEIGHTS}
    xi, yi, ci = _mesh_pos()
    me = _dev_index(xi, yi, ci)
    me1 = me.astype(jnp.int32).reshape(1)

    big2d = {n: _block2d(n, shard[n]) for n in BIG}
    small_shapes = [shard[n].shape[1:] for n in SMALL_SHARDED]
    n_small = sum(s[0] * s[1] for s in small_shapes)
    packed_small = _pack([shard[n] for n in SMALL_SHARDED], _packed_rows(n_small))
    gathered = _all_gather([big2d["w_in"].astype(BF16), packed_small])
    p = {"w_in": _full_from_gathered(gathered[0])}
    flat_small = gathered[-1].reshape(N_DEV, -1)
    lo = 0
    for n, (rows, width) in zip(SMALL_SHARDED, small_shapes):
        blocks = flat_small[:, lo:lo + rows * width].reshape(N_DEV, rows, width)
        p[n] = blocks.transpose(1, 0, 2).reshape(rows, N_DEV * width)
        lo += rows * width
    p["norm_mix_g"], p["norm_ffn_g"] = shard["norm_mix_g"], shard["norm_ffn_g"]
    p["conv_a_b"], p["ffn_conv_b"] = shard["conv_a_b"], shard["ffn_conv_b"]
    p["final_norm_g"] = shard["final_norm_g"].reshape(1, -1)
    late = {n: (big2d[n].T if n in ("w_proj_a", "w_proj_b") else big2d[n]).astype(BF16) for n in LATE}

    loss_part, dx, parts, got, g_small = _local_step(x[0], loss_target[0], p, late)

    results = {}
    for n in BIG:
        outs = _adamw_big(big2d[n], _block2d(n, mom_m[n]), _block2d(n, mom_v[n]), parts[n], got[n], me1)
        results[n] = [_block2d(n, o).reshape(shard[n].shape) for o in outs]

    small_full_shapes = [g_small[n].shape for n in SMALL]
    n_vec = sum(s[0] * s[1] for s in small_full_shapes) + 1
    packed = _pack([g_small[n] for n in SMALL] + [loss_part.reshape(1)], _packed_rows(n_vec))
    reduced = _all_reduce_small(packed)
    *g_full, loss_vec = _unpack(reduced, small_full_shapes + [(1,)])
    loss = loss_vec[0]
    own_g = []
    for n, g in zip(SMALL, g_full):
        if n in SMALL_SHARDED:
            width = shard[n].shape[-1]
            g = lax.dynamic_slice_in_dim(g, me * width, width, axis=1)
        own_g.append(g.reshape(shard[n].shape))
    def rows2d(a):
        return a.reshape(-1, a.shape[-1])

    deltas, new_ms, new_vs = _adamw_small([rows2d(shard[n]) for n in SMALL], [rows2d(g) for g in own_g],
                                          [rows2d(mom_m[n]) for n in SMALL], [rows2d(mom_v[n]) for n in SMALL])
    for i, n in enumerate(SMALL):
        results[n] = [own_g[i]] + [a.reshape(shard[n].shape) for a in (deltas[i], new_ms[i], new_vs[i])]

    grad_x = dx.reshape(x.shape)
    return (loss, grad_x, *[results[n][0] for n in WEIGHTS], *[results[n][1] for n in WEIGHTS],
            *[results[n][2] for n in WEIGHTS], *[results[n][3] for n in WEIGHTS])
```
